```python
import math
import jax, jax.numpy as jnp
from jax import lax
import numpy as np


D_MODEL = 1024
BATCH = 16
SEQ = 256
DEPTH = 2
DEC_BATCH = 2
DEC_SEQ = 4096
PAST_LEN = 256

GRID_W = 64
Q_BLOCK = 128
CHUNK = 64
EPS = 1e-6
MLA_HEADS = 8
MLA_NOPE = 64
ROPE_DIM = 32
AXIS_DIM = ROPE_DIM // 2
MLA_V = 64
Q_LORA = 256
KV_LORA = 128
ROPE_THETA = 10000.0
GLA_HEADS = 4
GLA_DK = 64
GLA_DV = 128
GLA_RANK = 16
GLA_TAU = 16.0
GDN_HEADS = 8
GDN_DK = 64
GDN_DV = 64
CONV_K = 3
GDN_CONV_W = 2 * GDN_HEADS * GDN_DK + GDN_HEADS * GDN_DV
N_BRANCH = 3
BRANCH_W = 512
D_FF = ((8 * D_MODEL + 3 * 256 - 1) // (3 * 256)) * 256
MOD_W = 6 * D_MODEL
IN_SIZES = (
    Q_LORA, KV_LORA + ROPE_DIM,
    GLA_HEADS * GLA_DK, GLA_HEADS * GLA_DK, GLA_HEADS * GLA_DV,
    GLA_HEADS * GLA_DV, 2 * GLA_RANK,
    GDN_CONV_W, GDN_HEADS * GDN_DV, 2 * GDN_HEADS, 2 * GDN_HEADS,
    N_BRANCH * D_MODEL,
)
IN_W = sum(IN_SIZES)

kernel_name = "hybrid_mla_gla_gdn_dit_step"


def rmsnorm(x, g):
    xf = x.astype(jnp.float32)
    y = xf * lax.rsqrt(jnp.mean(xf * xf, axis=-1, keepdims=True) + EPS)
    return (y * g.astype(jnp.float32)).astype(x.dtype)


def l2norm(x):
    xf = x.astype(jnp.float32)
    return (xf * lax.rsqrt(jnp.sum(xf * xf, axis=-1, keepdims=True) + EPS)).astype(x.dtype)


def rope_half(x, ang):
    n = ang.shape[-1]
    x1, x2 = x[..., :n], x[..., n:]
    cos, sin = jnp.cos(ang).astype(x.dtype), jnp.sin(ang).astype(x.dtype)
    return jnp.concatenate([x1 * cos - x2 * sin, x1 * sin + x2 * cos], axis=-1)


def axial_rope(x, ang_r, ang_c):
    return jnp.concatenate([rope_half(x[..., :AXIS_DIM], ang_r), rope_half(x[..., AXIS_DIM:], ang_c)], axis=-1)


def axial_angles(n_tokens):
    rows = n_tokens // GRID_W
    row = jnp.repeat(jnp.arange(rows, dtype=jnp.float32), GRID_W)
    col = jnp.tile(jnp.arange(GRID_W, dtype=jnp.float32), rows)
    inv = ROPE_THETA ** (-jnp.arange(0, AXIS_DIM, 2, dtype=jnp.float32) / AXIS_DIM)
    return row[:, None] * inv, col[:, None] * inv


def centred_conv(x, w):
    t = x.shape[1]
    pad = CONV_K // 2
    xp = jnp.pad(x, ((0, 0), (pad, pad), (0, 0)))
    return sum(xp[:, j:j + t] * w[j] for j in range(CONV_K))


def block_attention(q_nope, q_rope, k_nope, k_rope, v):
    b, t, h, _ = q_nope.shape
    nb = t // Q_BLOCK
    scale = (MLA_NOPE + ROPE_DIM) ** -0.5

    def blocks(a):
        return jnp.moveaxis(a.reshape(b, nb, Q_BLOCK, h, a.shape[-1]), 1, 0)

    def one(qs):
        qn, qr = qs
        s = jnp.einsum('bqhd,bkhd->bhqk', qn, k_nope) + jnp.einsum('bqhr,bkr->bhqk', qr, k_rope)
        p = jax.nn.softmax(s.astype(jnp.float32) * scale, axis=-1).astype(v.dtype)
        return jnp.einsum('bhqk,bkhd->bqhd', p, v)

    o = lax.map(one, (blocks(q_nope), blocks(q_rope)))
    return jnp.moveaxis(o, 0, 1).reshape(b, t, h * MLA_V)


def mla_branch(q_lat, kv_lat, ctx_kv, ang, q_norm, w_uq, kv_norm, w_ukv):
    b, t, _ = q_lat.shape
    q = (rmsnorm(q_lat, q_norm) @ w_uq).reshape(b, t, MLA_HEADS, MLA_NOPE + ROPE_DIM)
    q_nope, q_rope = q[..., :MLA_NOPE], q[..., MLA_NOPE:]
    ckv = rmsnorm(kv_lat[..., :KV_LORA], kv_norm)
    k_rope = kv_lat[..., KV_LORA:]
    if ang is not None:
        ang_r, ang_c = ang
        q_rope = axial_rope(q_rope, ang_r[:, None], ang_c[:, None])
        k_rope = axial_rope(k_rope, ang_r, ang_c)
    own_kv = jnp.concatenate([ckv, k_rope], axis=-1)
    all_kv = own_kv if ctx_kv is None else jnp.concatenate([ctx_kv.astype(own_kv.dtype), own_kv], axis=1)
    s = all_kv.shape[1]
    kv = (all_kv[..., :KV_LORA] @ w_ukv).reshape(b, s, MLA_HEADS, MLA_NOPE + MLA_V)
    out = block_attention(q_nope, q_rope, kv[..., :MLA_NOPE], all_kv[..., KV_LORA:], kv[..., MLA_NOPE:])
    return out, own_kv


def to_chunks(x):
    b, t, h, d = x.shape
    return x.reshape(b, t // CHUNK, CHUNK, h, d).transpose(0, 1, 3, 2, 4)


def from_chunks(x):
    b, n, h, c, d = x.shape
    return x.transpose(0, 1, 3, 2, 4).reshape(b, n * c, h, d)


def gla_scan(q, k, v, log_a, s0):
    dt = q.dtype
    q, k, v = to_chunks(q), to_chunks(k), to_chunks(v)
    bcum = jnp.cumsum(to_chunks(log_a).astype(jnp.float32), axis=3)
    b_last = bcum[:, :, :, -1:, :]
    q_dec = q * jnp.exp(bcum).astype(dt)
    k_inv = k * jnp.exp(-bcum).astype(dt)
    k_end = k * jnp.exp(b_last - bcum).astype(dt)
    tri = jnp.tril(jnp.ones((CHUNK, CHUNK), dtype=bool))
    a_intra = jnp.where(tri, jnp.einsum('bnhck,bnhsk->bnhcs', q_dec, k_inv), 0)
    o_intra = jnp.einsum('bnhcs,bnhsv->bnhcv', a_intra, v)
    upd = jnp.einsum('bnhck,bnhcv->bnhkv', k_end, v)
    decay = jnp.exp(b_last[:, :, :, 0, :]).astype(dt)

    def step(s, inp):
        d_n, u_n = inp
        return (d_n[..., None] * s + u_n).astype(s.dtype), s

    s_fin, s_prev = lax.scan(step, s0.astype(dt), (jnp.moveaxis(decay, 1, 0), jnp.moveaxis(upd, 1, 0)))
    s_prev = jnp.moveaxis(s_prev, 0, 1)
    o = o_intra + jnp.einsum('bnhck,bnhkv->bnhcv', q_dec, s_prev)
    return from_chunks(o), s_fin


def gdn_scan(q, k, v, g, beta, s0):
    dt = q.dtype
    f32 = jnp.float32
    q, k, v = to_chunks(q), to_chunks(k), to_chunks(v)
    gam = jnp.cumsum(to_chunks(g[..., None])[..., 0], axis=-1)
    bet = to_chunks(beta[..., None])[..., 0]
    tri = jnp.tril(jnp.ones((CHUNK, CHUNK), dtype=bool))
    strict = jnp.tril(jnp.ones((CHUNK, CHUNK), dtype=bool), -1)
    decay = jnp.exp(jnp.where(tri, gam[..., :, None] - gam[..., None, :], -jnp.inf))
    kk = jnp.einsum('bnhck,bnhsk->bnhcs', k, k).astype(f32)
    m = jnp.where(strict, bet[..., :, None] * kk * decay, 0.0)
    rhs = jnp.concatenate([v.astype(f32) * bet[..., None],
                           k.astype(f32) * (bet * jnp.exp(gam))[..., None]], axis=-1)
    sol = lax.linalg.triangular_solve(m, rhs, left_side=True, lower=True, unit_diagonal=True)
    u = sol[..., :v.shape[-1]].astype(dt)
    w = sol[..., v.shape[-1]:].astype(dt)
    a_qk = (jnp.einsum('bnhck,bnhsk->bnhcs', q, k).astype(f32) * decay).astype(dt)
    q_dec = (q.astype(f32) * jnp.exp(gam)[..., None]).astype(dt)
    k_end = (k.astype(f32) * jnp.exp(gam[..., -1:] - gam)[..., None]).astype(dt)
    g_end = jnp.exp(gam[..., -1]).astype(dt)

    def step(s, inp):
        u_n, w_n, qd_n, ke_n, a_n, ge_n = inp
        v_new = u_n - jnp.einsum('bhck,bhkv->bhcv', w_n, s)
        o_n = jnp.einsum('bhck,bhkv->bhcv', qd_n, s) + jnp.einsum('bhcs,bhsv->bhcv', a_n, v_new)
        s = (ge_n[..., None, None] * s + jnp.einsum('bhck,bhcv->bhkv', ke_n, v_new)).astype(s.dtype)
        return s, o_n

    xs = tuple(jnp.moveaxis(a, 1, 0) for a in (u, w, q_dec, k_end, a_qk, g_end))
    s_fin, o = lax.scan(step, s0.astype(dt), xs)
    return from_chunks(jnp.moveaxis(o, 0, 1)), s_fin


def gla_branch(q, k, v, r, glr, s0, w_gate_up, b_gate, norm):
    b, t, _ = q.shape
    q = q.reshape(b, t, GLA_HEADS, GLA_DK) * GLA_DK ** -0.5
    k = k.reshape(b, t, GLA_HEADS, GLA_DK)
    v = v.reshape(b, t, GLA_HEADS, GLA_DV)
    z = jnp.einsum('btzr,zrk->btzk', glr.reshape(b, t, 2, GLA_RANK), w_gate_up) + b_gate
    log_a = (jax.nn.log_sigmoid(z.astype(jnp.float32)) / GLA_TAU).reshape(b, t, 2, GLA_HEADS, GLA_DK)
    o_f, s_f = gla_scan(q, k, v, log_a[:, :, 0], s0[:, 0])
    o_b, s_b = gla_scan(jnp.flip(q, 1), jnp.flip(k, 1), jnp.flip(v, 1), jnp.flip(log_a[:, :, 1], 1), s0[:, 1])
    o = rmsnorm(o_f + jnp.flip(o_b, 1), norm) * jax.nn.silu(r.reshape(b, t, GLA_HEADS, GLA_DV))
    return o.reshape(b, t, GLA_HEADS * GLA_DV), jnp.stack([s_f, s_b], axis=1)


def gdn_branch(qkv, z, a, bt, s0, conv_w, a_log, dt_bias, norm):
    b, t, _ = qkv.shape
    hk = GDN_HEADS * GDN_DK
    qkv = jax.nn.silu(centred_conv(qkv, conv_w))
    q = l2norm(qkv[..., :hk].reshape(b, t, GDN_HEADS, GDN_DK)) * GDN_DK ** -0.5
    k = l2norm(qkv[..., hk:2 * hk].reshape(b, t, GDN_HEADS, GDN_DK))
    v = qkv[..., 2 * hk:].reshape(b, t, GDN_HEADS, GDN_DV)
    a = a.reshape(b, t, 2, GDN_HEADS).astype(jnp.float32)
    bt = bt.reshape(b, t, 2, GDN_HEADS).astype(jnp.float32)
    g = -jnp.exp(a_log.astype(jnp.float32)) * jax.nn.softplus(a + dt_bias.astype(jnp.float32))
    beta = jax.nn.sigmoid(bt)
    o_f, s_f = gdn_scan(q, k, v, g[:, :, 0], beta[:, :, 0], s0[:, 0])
    o_b, s_b = gdn_scan(jnp.flip(q, 1), jnp.flip(k, 1), jnp.flip(v, 1),
                        jnp.flip(g[:, :, 1], 1), jnp.flip(beta[:, :, 1], 1), s0[:, 1])
    o = rmsnorm(o_f + jnp.flip(o_b, 1), norm) * jax.nn.silu(z.reshape(b, t, GDN_HEADS, GDN_DV))
    return o.reshape(b, t, GDN_HEADS * GDN_DV), jnp.stack([s_f, s_b], axis=1)


def trunk_layer(h, cond, ang, ctx_kv, s0_gla, s0_gdn, p, l):
    b, t, _ = h.shape
    mod = jax.nn.silu(cond) @ p['w_mod'][l] + p['b_mod'][l]
    sh_a, sc_a, g_a, sh_f, sc_f, g_f = jnp.split(mod[:, None, :], 6, axis=-1)
    xn = rmsnorm(h, p['norm_mix'][l]) * (1 + sc_a) + sh_a
    proj = xn @ p['w_in'][l]
    split_points = np.cumsum(IN_SIZES)[:-1].tolist()
    mq, mkv, gq, gk, gv, gr, glr, dqkv, dz, da, db, gates = jnp.split(proj, split_points, axis=-1)
    y_mla, own_kv = mla_branch(mq, mkv, ctx_kv, ang, p['mla_q_norm'][l], p['mla_w_uq'][l],
                               p['mla_kv_norm'][l], p['mla_w_ukv'][l])
    y_gla, st_gla = gla_branch(gq, gk, gv, gr, glr, s0_gla, p['gla_w_gate'][l], p['gla_b_gate'][l],
                               p['gla_norm'][l])
    y_gdn, st_gdn = gdn_branch(dqkv, dz, da, db, s0_gdn, p['gdn_conv'][l], p['gdn_a_log'][l],
                               p['gdn_dt_bias'][l], p['gdn_norm'][l])
    branches = jnp.stack([y_mla, y_gla, y_gdn], axis=2)
    proj_b = jnp.einsum('btnk,nkd->btnd', branches, p['w_branch'][l])
    gates = jax.nn.sigmoid(gates.reshape(b, t, N_BRANCH, D_MODEL))
    y = jnp.sum(gates * proj_b, axis=2) @ p['w_out'][l]
    h = h + g_a * y
    xf = rmsnorm(h, p['norm_ffn'][l]) * (1 + sc_f) + sh_f
    gu = xf @ p['ffn_w_in'][l]
    h = h + g_f * ((jax.nn.silu(gu[..., :D_FF]) * gu[..., D_FF:]) @ p['ffn_w_out'][l])
    return h, own_kv, st_gla, st_gdn


def setup_inputs(seed: int = 0) -> dict:
    key = jax.random.key(seed)
    ks = jax.random.split(key, 32)
    f32 = jnp.float32

    def nrm(k, shape, scale):
        return jax.random.normal(k, shape, f32) * scale

    dt = jnp.exp(jax.random.uniform(ks[20], (DEPTH, 2, GDN_HEADS), f32, math.log(1e-3), math.log(1e-1)))
    return {
        'x_prompt': nrm(ks[0], (BATCH, SEQ, D_MODEL), 1.0),
        'x_sample': nrm(ks[1], (DEC_BATCH, DEC_SEQ, D_MODEL), 1.0),
        'cache_mla': nrm(ks[2], (DEC_BATCH, DEPTH, PAST_LEN, KV_LORA + ROPE_DIM), 1.0),
        'state_gla': nrm(ks[3], (DEC_BATCH, DEPTH, 2, GLA_HEADS, GLA_DK, GLA_DV), 0.1),
        'state_gdn': nrm(ks[4], (DEC_BATCH, DEPTH, 2, GDN_HEADS, GDN_DK, GDN_DV), 0.1),
        'c': nrm(ks[5], (DEC_BATCH, D_MODEL), 1.0),
        'c_ctx': nrm(ks[6], (D_MODEL,), 1.0),
        'w_mod': nrm(ks[7], (DEPTH, D_MODEL, MOD_W), 0.5 * D_MODEL ** -0.5),
        'b_mod': nrm(ks[8], (DEPTH, MOD_W), 0.01),
        'norm_mix': 1.0 + nrm(ks[9], (DEPTH, D_MODEL), 0.02),
        'w_in': nrm(ks[10], (DEPTH, D_MODEL, IN_W), D_MODEL ** -0.5),
        'mla_q_norm': 1.0 + nrm(ks[11], (DEPTH, Q_LORA), 0.02),
        'mla_w_uq': nrm(ks[12], (DEPTH, Q_LORA, MLA_HEADS * (MLA_NOPE + ROPE_DIM)), Q_LORA ** -0.5),
        'mla_kv_norm': 1.0 + nrm(ks[13], (DEPTH, KV_LORA), 0.02),
        'mla_w_ukv': nrm(ks[14], (DEPTH, KV_LORA, MLA_HEADS * (MLA_NOPE + MLA_V)), KV_LORA ** -0.5),
        'gla_w_gate': nrm(ks[15], (DEPTH, 2, GLA_RANK, GLA_HEADS * GLA_DK), GLA_RANK ** -0.5),
        'gla_b_gate': nrm(ks[16], (DEPTH, 2, GLA_HEADS * GLA_DK), 0.01),
        'gla_norm': 1.0 + nrm(ks[17], (DEPTH, GLA_DV), 0.02),
        'gdn_conv': nrm(ks[18], (DEPTH, CONV_K, GDN_CONV_W), CONV_K ** -0.5),
        'gdn_a_log': jnp.log(jax.random.uniform(ks[19], (DEPTH, 2, GDN_HEADS), f32, 1.0, 16.0)),
        'gdn_dt_bias': dt + jnp.log(-jnp.expm1(-dt)),
        'gdn_norm': 1.0 + nrm(ks[21], (DEPTH, GDN_DV), 0.02),
        'w_branch': nrm(ks[22], (DEPTH, N_BRANCH, BRANCH_W, D_MODEL), BRANCH_W ** -0.5),
        'w_out': nrm(ks[23], (DEPTH, D_MODEL, D_MODEL), D_MODEL ** -0.5),
        'norm_ffn': 1.0 + nrm(ks[24], (DEPTH, D_MODEL), 0.02),
        'ffn_w_in': nrm(ks[25], (DEPTH, D_MODEL, 2 * D_FF), D_MODEL ** -0.5),
        'ffn_w_out': nrm(ks[26], (DEPTH, D_FF, D_MODEL), D_FF ** -0.5),
        'final_norm': 1.0 + nrm(ks[27], (D_MODEL,), 0.02),
    }


def reference(x_prompt, x_sample, cache_mla, state_gla, state_gdn, c, c_ctx,
              w_mod, b_mod, norm_mix, w_in, mla_q_norm, mla_w_uq, mla_kv_norm, mla_w_ukv,
              gla_w_gate, gla_b_gate, gla_norm, gdn_conv, gdn_a_log, gdn_dt_bias, gdn_norm,
              w_branch, w_out, norm_ffn, ffn_w_in, ffn_w_out, final_norm):
    p = dict(w_mod=w_mod, b_mod=b_mod, norm_mix=norm_mix, w_in=w_in, mla_q_norm=mla_q_norm,
             mla_w_uq=mla_w_uq, mla_kv_norm=mla_kv_norm, mla_w_ukv=mla_w_ukv, gla_w_gate=gla_w_gate,
             gla_b_gate=gla_b_gate, gla_norm=gla_norm, gdn_conv=gdn_conv, gdn_a_log=gdn_a_log,
             gdn_dt_bias=gdn_dt_bias, gdn_norm=gdn_norm, w_branch=w_branch, w_out=w_out,
             norm_ffn=norm_ffn, ffn_w_in=ffn_w_in, ffn_w_out=ffn_w_out)

    b = x_prompt.shape[0]
    h = x_prompt
    cond_ctx = c_ctx[None, :]
    zeros_gla = jnp.zeros((b, 2, GLA_HEADS, GLA_DK, GLA_DV), x_prompt.dtype)
    zeros_gdn = jnp.zeros((b, 2, GDN_HEADS, GDN_DK, GDN_DV), x_prompt.dtype)
    kv_list, gla_list, gdn_list = [], [], []
    for l in range(DEPTH):
        h, kv_l, sg_l, sd_l = trunk_layer(h, cond_ctx, None, None, zeros_gla, zeros_gdn, p, l)
        kv_list.append(kv_l)
        gla_list.append(sg_l)
        gdn_list.append(sd_l)
    y_prompt = rmsnorm(h, final_norm)
    new_cache_mla = jnp.stack(kv_list, axis=1)
    new_state_gla = jnp.stack(gla_list, axis=1)
    new_state_gdn = jnp.stack(gdn_list, axis=1)

    ang = axial_angles(x_sample.shape[1])
    h = x_sample
    for l in range(DEPTH):
        h, _, _, _ = trunk_layer(h, c, ang, cache_mla[:, l], state_gla[:, l], state_gdn[:, l], p, l)
    y_sample = rmsnorm(h, final_norm)
    return (y_prompt, y_sample, new_cache_mla, new_state_gla, new_state_gdn)
```

```python
import functools

import jax
import jax.numpy as jnp
from jax import lax
from jax.experimental import pallas as pl
from jax.experimental.pallas import tpu as pltpu

F32 = jnp.float32
BF16 = jnp.bfloat16

D_MODEL = 1024
BATCH = 16
SEQ = 256
DEPTH = 2
DEC_BATCH = 2
DEC_SEQ = 4096
PAST_LEN = 256
GRID_W = 64
CHUNK = 64
EPS = 1e-6
MLA_HEADS = 8
MLA_NOPE = 64
ROPE_DIM = 32
AXIS_DIM = ROPE_DIM // 2
MLA_V = 64
Q_LORA = 256
KV_LORA = 128
ROPE_THETA = 10000.0
GLA_HEADS = 4
GLA_DK = 64
GLA_DV = 128
GLA_RANK = 16
GLA_TAU = 16.0
GDN_HEADS = 8
GDN_DK = 64
GDN_DV = 64
D_FF = ((8 * D_MODEL + 3 * 256 - 1) // (3 * 256)) * 256
MOD_W = 6 * D_MODEL

NCTX = BATCH * SEQ
NSMP = DEC_BATCH * DEC_SEQ
R = NCTX + NSMP
LANE = 128
HEAD_W = 128
SEQ_BLK = 256
NCH = SEQ_BLK // CHUNK
MOD_ROWS = 8

PC_GATES = 0
PC_DQKV = 3072
PC_DZ = 4608
PC_GV = 5120
PC_GR = 5632
PC_MLA = 6144
PC_GQ = 6656
PC_GK = 6912
PC_GLR = 7168
PC_DAB = 7296
PROJ_W = 7680

TM_IN = 512
TN_IN = 1536
TM_PREP = 512
TM_MERGE = 512
TM_FFN = 512
TF_FFN = 1408
TQ_ATT = 256
VMEM_LIMIT = 48 * 1024 * 1024


def _cp(sem):
    return pltpu.CompilerParams(dimension_semantics=sem, vmem_limit_bytes=VMEM_LIMIT)


def _dot(a, b):
    return jnp.dot(a, b, preferred_element_type=F32)


def _dot_nt(a, b):
    return lax.dot_general(a, b, (((1,), (1,)), ((), ())), preferred_element_type=F32)


def _dot_tn(a, b):
    return lax.dot_general(a, b, (((0,), (0,)), ((), ())), preferred_element_type=F32)


def _split2(x):
    hi = x.astype(BF16)
    lo = (x - hi.astype(F32)).astype(BF16)
    return hi, lo


def _split3(x):
    hi = x.astype(BF16)
    r1 = x - hi.astype(F32)
    mid = r1.astype(BF16)
    lo = (r1 - mid.astype(F32)).astype(BF16)
    return hi, mid, lo


def _cumsum_rows(tri_b, x):
    hi, mid, lo = _split3(x)
    return _dot(tri_b, hi) + (_dot(tri_b, mid) + _dot(tri_b, lo))


def _silu(x):
    return x * jax.nn.sigmoid(x)


def _softplus(x):
    return jnp.maximum(x, 0.0) + jnp.log(1.0 + jnp.exp(-jnp.abs(x)))


def _log_sigmoid(x):
    return jnp.minimum(x, 0.0) - jnp.log(1.0 + jnp.exp(-jnp.abs(x)))


def _rms(x, g):
    return x * lax.rsqrt(jnp.mean(x * x, axis=-1, keepdims=True) + EPS) * g


def _mod_row(row_start):
    return jnp.where(row_start < NCTX, 0, 1 + (row_start - NCTX) // DEC_SEQ)


def _mod_spec(layer, which, tm):
    def imap(i, *_):
        return ((layer * MOD_ROWS + _mod_row(i * tm)) * 6 + which, 0, 0)
    return pl.BlockSpec((1, 1, D_MODEL), imap)


def _mod_kernel(c_ref, w_ref, b_ref, o_ref):
    x = _silu(c_ref[...])
    xh, xl = _split2(x)
    wh, wl = _split2(w_ref[0])
    o_ref[0] = _dot(xh, wh) + (_dot(xl, wh) + _dot(xh, wl)) + b_ref[0]


def _modulation(cond8, w_mod, b_mod):
    tn = 1536
    return pl.pallas_call(
        _mod_kernel,
        grid=(DEPTH, MOD_W // tn),
        in_specs=[pl.BlockSpec((MOD_ROWS, D_MODEL), lambda l, j: (0, 0)),
                  pl.BlockSpec((1, D_MODEL, tn), lambda l, j: (l, 0, j)),
                  pl.BlockSpec((1, 1, tn), lambda l, j: (l, 0, j))],
        out_specs=pl.BlockSpec((1, MOD_ROWS, tn), lambda l, j: (l, 0, j)),
        out_shape=jax.ShapeDtypeStruct((DEPTH, MOD_ROWS, MOD_W), F32),
        compiler_params=_cp(("arbitrary", "arbitrary")),
        name="modulation",
    )(cond8, w_mod, b_mod.reshape(DEPTH, 1, MOD_W))


def _inproj_kernel(h_ref, g_ref, sc_ref, sh_ref, w_ref, o_ref, xn_ref):
    @pl.when(pl.program_id(1) == 0)
    def _():
        y = _rms(h_ref[...], g_ref[0])
        xn_ref[...] = (y * (1.0 + sc_ref[0]) + sh_ref[0]).astype(BF16)

    o_ref[...] = _dot(xn_ref[...], w_ref[0]).astype(BF16)


def _inproj(h, mod3, norm_mix, w_in_p, layer):
    return pl.pallas_call(
        _inproj_kernel,
        grid=(R // TM_IN, PROJ_W // TN_IN),
        in_specs=[pl.BlockSpec((TM_IN, D_MODEL), lambda i, j: (i, 0)),
                  pl.BlockSpec((1, 1, D_MODEL), lambda i, j: (layer, 0, 0)),
                  _mod_spec(layer, 1, TM_IN),
                  _mod_spec(layer, 0, TM_IN),
                  pl.BlockSpec((1, D_MODEL, TN_IN), lambda i, j: (layer, 0, j))],
        out_specs=pl.BlockSpec((TM_IN, TN_IN), lambda i, j: (i, j)),
        out_shape=jax.ShapeDtypeStruct((R, PROJ_W), BF16),
        scratch_shapes=[pltpu.VMEM((TM_IN, D_MODEL), BF16)],
        compiler_params=_cp(("arbitrary", "arbitrary")),
        name="inproj",
    )(h, norm_mix.reshape(DEPTH, 1, D_MODEL), mod3, mod3, w_in_p)


def _mla_prep_kernel(pm_ref, tab_ref, qn_ref, kvn_ref, wq_ref, wk_ref, wv_ref,
                     q_ref, k_ref, v_ref, own_ref):
    pm = pm_ref[...].astype(F32)
    tab = tab_ref[...]
    qn = _rms(pm[:, :Q_LORA], qn_ref[0]).astype(BF16)
    q2 = _dot(qn, wq_ref[0])
    hw = MLA_HEADS * HEAD_W
    cq = jnp.tile(tab[:, :HEAD_W], (1, MLA_HEADS))
    sq = jnp.tile(tab[:, HEAD_W:2 * HEAD_W], (1, MLA_HEADS))
    q_ref[...] = (q2[:, :hw] * cq + q2[:, hw:] * sq).astype(BF16)
    ckv = _rms(pm[:, Q_LORA:Q_LORA + KV_LORA], kvn_ref[0])
    kr = pm[:, Q_LORA + KV_LORA:] * tab[:, 2 * HEAD_W:]
    lhs = jnp.concatenate([ckv, kr], axis=1)
    own_ref[...] = lhs
    lb = lhs.astype(BF16)
    k_ref[...] = _dot(lb, wk_ref[0]).astype(BF16)
    v_ref[...] = _dot(lb[:, :KV_LORA], wv_ref[0]).astype(BF16)


def _mla_prep(proj, tab, q_norm, kv_norm, wq_p, wk_p, wv_p, layer):
    tm = TM_PREP
    hw = MLA_HEADS * HEAD_W

    def tab_map(i):
        r0 = i * tm
        return (jnp.where(r0 < NCTX, 0, 1 + ((r0 - NCTX) % DEC_SEQ) // tm), 0)

    return pl.pallas_call(
        _mla_prep_kernel,
        grid=(R // tm,),
        in_specs=[pl.BlockSpec((tm, 512), lambda i: (i, PC_MLA // 512)),
                  pl.BlockSpec((tm, 3 * HEAD_W), tab_map),
                  pl.BlockSpec((1, 1, Q_LORA), lambda i: (layer, 0, 0)),
                  pl.BlockSpec((1, 1, KV_LORA), lambda i: (layer, 0, 0)),
                  pl.BlockSpec((1, Q_LORA, 2 * hw), lambda i: (layer, 0, 0)),
                  pl.BlockSpec((1, 2 * KV_LORA, hw), lambda i: (layer, 0, 0)),
                  pl.BlockSpec((1, KV_LORA, hw), lambda i: (layer, 0, 0))],
        out_specs=[pl.BlockSpec((tm, hw), lambda i: (i, 0)),
                   pl.BlockSpec((tm, hw), lambda i: (i, 0)),
                   pl.BlockSpec((tm, hw), lambda i: (i, 0)),
                   pl.BlockSpec((tm, 2 * KV_LORA), lambda i: (i, 0))],
        out_shape=[jax.ShapeDtypeStruct((R, hw), BF16),
                   jax.ShapeDtypeStruct((R, hw), BF16),
                   jax.ShapeDtypeStruct((R, hw), BF16),
                   jax.ShapeDtypeStruct((R, 2 * KV_LORA), F32)],
        compiler_params=_cp(("arbitrary",)),
        name="mla_prep",
    )(proj, tab, q_norm.reshape(DEPTH, 1, Q_LORA), kv_norm.reshape(DEPTH, 1, KV_LORA),
      wq_p, wk_p, wv_p)


def _kv_cache_kernel(c_ref, wk_ref, wv_ref, k_ref, v_ref):
    lb = c_ref[0, 0].astype(BF16)
    k_ref[0, 0] = _dot(lb, wk_ref[0]).astype(BF16)
    v_ref[0, 0] = _dot(lb[:, :KV_LORA], wv_ref[0]).astype(BF16)


def _kv_cache(cache_p, wk_p, wv_p):
    hw = MLA_HEADS * HEAD_W
    return pl.pallas_call(
        _kv_cache_kernel,
        grid=(DEC_BATCH, DEPTH),
        in_specs=[pl.BlockSpec((1, 1, PAST_LEN, 2 * KV_LORA), lambda b, l: (b, l, 0, 0)),
                  pl.BlockSpec((1, 2 * KV_LORA, hw), lambda b, l: (l, 0, 0)),
                  pl.BlockSpec((1, KV_LORA, hw), lambda b, l: (l, 0, 0))],
        out_specs=[pl.BlockSpec((1, 1, PAST_LEN, hw), lambda b, l: (b, l, 0, 0)),
                   pl.BlockSpec((1, 1, PAST_LEN, hw), lambda b, l: (b, l, 0, 0))],
        out_shape=[jax.ShapeDtypeStruct((DEC_BATCH, DEPTH, PAST_LEN, hw), BF16)] * 2,
        compiler_params=_cp(("arbitrary", "arbitrary")),
        name="kv_cache",
    )(cache_p, wk_p, wv_p)


def _attn_kernel(*refs, heads, has_cache):
    if has_cache:
        q_ref, k_ref, v_ref, kc_ref, vc_ref, o_ref = refs
    else:
        q_ref, k_ref, v_ref, o_ref = refs
    outs = []
    for h in range(heads):
        sl = slice(h * HEAD_W, (h + 1) * HEAD_W)
        q = q_ref[:, sl]
        s = _dot_nt(q, k_ref[:, sl])
        m = jnp.max(s, axis=-1, keepdims=True)
        if has_cache:
            sc = _dot_nt(q, kc_ref[0, 0, :, sl])
            m = jnp.maximum(m, jnp.max(sc, axis=-1, keepdims=True))
        p = jnp.exp(s - m)
        l = jnp.sum(p, axis=-1, keepdims=True)
        o = _dot(p.astype(BF16), v_ref[:, sl])
        if has_cache:
            pc = jnp.exp(sc - m)
            l = l + jnp.sum(pc, axis=-1, keepdims=True)
            o = o + _dot(pc.astype(BF16), vc_ref[0, 0, :, sl])
        outs.append(o / l)
    pairs = [outs[2 * i] + pltpu.roll(outs[2 * i + 1], MLA_V, 1) for i in range(heads // 2)]
    o_ref[...] = jnp.concatenate(pairs, axis=1).astype(BF16)


def _attention_ctx(qp, kp, vp):
    hw = MLA_HEADS * HEAD_W
    return pl.pallas_call(
        functools.partial(_attn_kernel, heads=MLA_HEADS, has_cache=False),
        grid=(BATCH,),
        in_specs=[pl.BlockSpec((SEQ, hw), lambda s: (s, 0))] * 3,
        out_specs=pl.BlockSpec((SEQ, MLA_HEADS * MLA_V), lambda s: (s, 0)),
        out_shape=jax.ShapeDtypeStruct((NCTX, MLA_HEADS * MLA_V), BF16),
        compiler_params=_cp(("arbitrary",)),
        name="attn_ctx",
    )(qp, kp, vp)


def _attention_smp(qp, kp, vp, kc, vc, layer):
    hpb = 2
    wq = hpb * HEAD_W
    nq = DEC_SEQ // TQ_ATT
    qoff = NCTX // TQ_ATT
    koff = NCTX // DEC_SEQ
    return pl.pallas_call(
        functools.partial(_attn_kernel, heads=hpb, has_cache=True),
        grid=(DEC_BATCH, MLA_HEADS // hpb, nq),
        in_specs=[pl.BlockSpec((TQ_ATT, wq), lambda b, g, i: (qoff + b * nq + i, g)),
                  pl.BlockSpec((DEC_SEQ, wq), lambda b, g, i: (koff + b, g)),
                  pl.BlockSpec((DEC_SEQ, wq), lambda b, g, i: (koff + b, g)),
                  pl.BlockSpec((1, 1, PAST_LEN, wq), lambda b, g, i: (b, layer, 0, g)),
                  pl.BlockSpec((1, 1, PAST_LEN, wq), lambda b, g, i: (b, layer, 0, g))],
        out_specs=pl.BlockSpec((TQ_ATT, hpb * MLA_V), lambda b, g, i: (b * nq + i, g)),
        out_shape=jax.ShapeDtypeStruct((NSMP, MLA_HEADS * MLA_V), BF16),
        compiler_params=_cp(("arbitrary", "arbitrary", "arbitrary")),
        name="attn_smp",
    )(qp, kp, vp, kc, vc)


def _seq_row_block(ctx, nblk):
    if ctx:
        return lambda d, s, j: s
    off = NCTX // SEQ_BLK
    return lambda d, s, j: off + s * nblk + j + d * (nblk - 1 - 2 * j)


def _chunk_masks(fwd):
    row = lax.broadcasted_iota(jnp.int32, (CHUNK, CHUNK), 0)
    col = lax.broadcasted_iota(jnp.int32, (CHUNK, CHUNK), 1)
    incl = (row - col) * jnp.where(fwd, 1, -1) >= 0
    return row, col, incl


def _gla_kernel(*refs, nblk, has_s0):
    if has_s0:
        q_ref, k_ref, v_ref, glr_ref, wg_ref, bg_ref, s0_ref, o_ref, s_scr = refs
    else:
        q_ref, k_ref, v_ref, glr_ref, wg_ref, bg_ref, o_ref, sfin_ref, s_scr = refs
    d = pl.program_id(0)
    j = pl.program_id(2)
    fwd = d == 0

    @pl.when(j == 0)
    def _():
        if has_s0:
            s_scr[...] = s0_ref[0, 0, 0]
        else:
            s_scr[...] = jnp.zeros_like(s_scr)

    _, _, incl = _chunk_masks(fwd)
    tri_b = jnp.where(incl, 1.0, 0.0).astype(BF16)
    wg = wg_ref[0, 0]
    bg = bg_ref[0, 0]

    def chunk(c, carry):
        ci = c + d * (NCH - 1 - 2 * c)
        rows = pl.ds(pl.multiple_of(ci * CHUNK, CHUNK), CHUNK)
        q = q_ref[rows, :].astype(F32) * (GLA_DK ** -0.5)
        k = k_ref[rows, :].astype(F32)
        v = v_ref[rows, :]
        z = _dot(glr_ref[rows, :], wg) + bg
        la = _log_sigmoid(z) * (1.0 / GLA_TAU)
        bc = _cumsum_rows(tri_b, la)
        bl = jnp.where(fwd, bc[CHUNK - 1:CHUNK], bc[0:1])
        qd = (q * jnp.exp(bc)).astype(BF16)
        ki = (k * jnp.exp(-bc)).astype(BF16)
        ke = (k * jnp.exp(bl - bc)).astype(BF16)
        dec_t = jnp.transpose(jnp.broadcast_to(jnp.exp(bl), (LANE, GLA_HEADS * GLA_DK)))
        for h in range(GLA_HEADS):
            sk = slice(h * GLA_DK, (h + 1) * GLA_DK)
            sv = slice(h * GLA_DV, (h + 1) * GLA_DV)
            a = jnp.where(incl, _dot_nt(qd[:, sk], ki[:, sk]), 0.0)
            s = s_scr[h]
            o = _dot(a.astype(BF16), v[:, sv]) + _dot(qd[:, sk], s.astype(BF16))
            o_ref[0, rows, sv] = o
            s_scr[h] = dec_t[sk, :] * s + _dot_tn(ke[:, sk], v[:, sv])
        return carry

    lax.fori_loop(0, NCH, chunk, 0)

    if not has_s0:
        @pl.when(j == nblk - 1)
        def _():
            sfin_ref[0, 0] = s_scr[...]


def _gla(proj, wg_p, bg_p, s0, layer, ctx):
    nseq, nblk = (BATCH, SEQ // SEQ_BLK) if ctx else (DEC_BATCH, DEC_SEQ // SEQ_BLK)
    rb = _seq_row_block(ctx, nblk)
    hk = GLA_HEADS * GLA_DK
    hv = GLA_HEADS * GLA_DV
    in_specs = [pl.BlockSpec((SEQ_BLK, hk), lambda d, s, j: (rb(d, s, j), PC_GQ // hk)),
                pl.BlockSpec((SEQ_BLK, hk), lambda d, s, j: (rb(d, s, j), PC_GK // hk)),
                pl.BlockSpec((SEQ_BLK, hv), lambda d, s, j: (rb(d, s, j), PC_GV // hv)),
                pl.BlockSpec((SEQ_BLK, LANE), lambda d, s, j: (rb(d, s, j), PC_GLR // LANE)),
                pl.BlockSpec((1, 1, LANE, hk), lambda d, s, j: (layer, d, 0, 0)),
                pl.BlockSpec((1, 1, 1, hk), lambda d, s, j: (layer, d, 0, 0))]
    args = [proj, proj, proj, proj, wg_p, bg_p]
    nrows = NCTX if ctx else NSMP
    roff = 0 if ctx else NCTX // SEQ_BLK
    o_spec = pl.BlockSpec((1, SEQ_BLK, hv), lambda d, s, j: (d, rb(d, s, j) - roff, 0))
    o_shape = jax.ShapeDtypeStruct((2, nrows, hv), F32)
    st_blk = (GLA_HEADS, GLA_DK, GLA_DV)
    if ctx:
        out_specs = [o_spec, pl.BlockSpec((1, 1) + st_blk, lambda d, s, j: (s, d, 0, 0, 0))]
        out_shape = [o_shape, jax.ShapeDtypeStruct((BATCH, 2) + st_blk, F32)]
    else:
        in_specs.append(pl.BlockSpec((1, 1, 1) + st_blk, lambda d, s, j: (s, layer, d, 0, 0, 0)))
        args.append(s0)
        out_specs, out_shape = o_spec, o_shape
    return pl.pallas_call(
        functools.partial(_gla_kernel, nblk=nblk, has_s0=not ctx),
        grid=(2, nseq, nblk),
        in_specs=in_specs, out_specs=out_specs, out_shape=out_shape,
        scratch_shapes=[pltpu.VMEM(st_blk, F32)],
        compiler_params=_cp(("arbitrary", "arbitrary", "arbitrary")),
        name="gla_ctx" if ctx else "gla_smp",
    )(*args)


GDN_G = 4
GDN_GW = GDN_G * GDN_DK
HALO = 16


def _gdn_kernel(*refs, nblk, has_s0):
    if has_s0:
        (x_ref, xp_ref, xn_ref, ab_ref, cw_ref, alog_ref, dtb_ref, s0_ref,
         o_ref, q_scr, k_scr, v_scr, s_scr) = refs
    else:
        (x_ref, xp_ref, xn_ref, ab_ref, cw_ref, alog_ref, dtb_ref,
         o_ref, sfin_ref, q_scr, k_scr, v_scr, s_scr) = refs
    d = pl.program_id(0)
    j = pl.program_id(2)
    fwd = d == 0
    bi = j + d * (nblk - 1 - 2 * j)
    hk = GDN_HEADS * GDN_DK
    ngrp = GDN_HEADS // GDN_G

    @pl.when(j == 0)
    def _():
        s_scr[...] = jnp.zeros_like(s_scr)
        if has_s0:
            for h in range(GDN_HEADS):
                g, hh = divmod(h, GDN_G)
                s_scr[g, hh * GDN_DK:(hh + 1) * GDN_DK, hh * GDN_DV:(hh + 1) * GDN_DV] = s0_ref[0, 0, 0, h]

    x = x_ref[...].astype(F32)
    rowi = lax.broadcasted_iota(jnp.int32, x.shape, 0)
    prev_row = jnp.where(bi > 0, xp_ref[...].astype(F32)[HALO - 1:HALO], 0.0)
    next_row = jnp.where(bi < nblk - 1, xn_ref[...].astype(F32)[0:1], 0.0)
    x_prev = jnp.where(rowi == 0, prev_row, pltpu.roll(x, 1, 0))
    x_next = jnp.where(rowi == SEQ_BLK - 1, next_row, pltpu.roll(x, SEQ_BLK - 1, 0))
    cw = cw_ref[0]
    y = _silu(x_prev * cw[0:1] + x * cw[1:2] + x_next * cw[2:3])

    lane = lax.broadcasted_iota(jnp.int32, (SEQ_BLK, LANE), 1)
    lo_half = lane < GDN_DK

    def l2n(t):
        parts = []
        for c in range(hk // LANE):
            tc = t[:, c * LANE:(c + 1) * LANE]
            sq = tc * tc
            s_lo = jnp.sum(jnp.where(lo_half, sq, 0.0), axis=-1, keepdims=True)
            s_hi = jnp.sum(jnp.where(lo_half, 0.0, sq), axis=-1, keepdims=True)
            parts.append(tc * lax.rsqrt(jnp.where(lo_half, s_lo, s_hi) + EPS))
        return jnp.concatenate(parts, axis=1)

    q_scr[...] = l2n(y[:, :hk]) * (GDN_DK ** -0.5)
    k_scr[...] = l2n(y[:, hk:2 * hk])
    v_scr[...] = y[:, 2 * hk:]

    _, _, incl64 = _chunk_masks(fwd)
    tri_b = jnp.where(incl64, 1.0, 0.0).astype(BF16)
    row = lax.broadcasted_iota(jnp.int32, (CHUNK, GDN_GW), 0)
    col = lax.broadcasted_iota(jnp.int32, (CHUNK, GDN_GW), 1) % CHUNK
    lag = (row - col) * jnp.where(fwd, 1, -1)
    incl = lag >= 0
    strict = lag > 0
    eye = col == row
    eye_f = jnp.where(eye, 1.0, 0.0)
    blk8 = (row // 8) == (col // 8)
    blk16 = (row // 16) == (col // 16)
    blk32 = (row // 32) == (col // 32)
    brow = lax.broadcasted_iota(jnp.int32, (GDN_GW, GDN_GW), 0) // CHUNK
    bcol = lax.broadcasted_iota(jnp.int32, (GDN_GW, GDN_GW), 1) // CHUNK
    same_head = brow == bcol

    def bdiag(t):
        tb = t.astype(BF16)
        return jnp.where(same_head, jnp.concatenate([tb] * GDN_G, axis=0), jnp.zeros((), BF16))

    def bdmm(a, b):
        return _dot(a.astype(BF16), bdiag(b))

    neg_a = -jnp.exp(alog_ref[0, 0])
    dtb = dtb_ref[0, 0]

    def chunk(c, carry):
        ci = c + d * (NCH - 1 - 2 * c)
        rows = pl.ds(pl.multiple_of(ci * CHUNK, CHUNK), CHUNK)
        ab = ab_ref[rows, :].astype(F32)
        g_all = neg_a * _softplus(ab + dtb)
        beta_all = jax.nn.sigmoid(ab)
        gam = _cumsum_rows(tri_b, g_all)
        qc = q_scr[rows, :]
        kc = k_scr[rows, :]
        vc = v_scr[rows, :]
        for g in range(ngrp):
            gl = slice(g * GDN_GW, (g + 1) * GDN_GW)
            gcol = jnp.concatenate(
                [jnp.broadcast_to(gam[:, g * GDN_G + h:g * GDN_G + h + 1], (CHUNK, GDN_DK))
                 for h in range(GDN_G)], axis=1)
            bcol_ = jnp.concatenate(
                [jnp.broadcast_to(beta_all[:, GDN_HEADS + g * GDN_G + h:GDN_HEADS + g * GDN_G + h + 1],
                                  (CHUNK, GDN_DK)) for h in range(GDN_G)], axis=1)
            grow = jnp.sum(jnp.where(eye, gcol, 0.0), axis=0, keepdims=True)
            glast = jnp.where(fwd, gcol[CHUNK - 1:CHUNK], gcol[0:1])
            decay = jnp.where(incl, jnp.exp(gcol - grow), 0.0)
            egc = jnp.exp(gcol)
            qg = qc[:, gl]
            kg = kc[:, gl]
            vg = vc[:, gl]
            kt = bdiag(kg)
            kk = _dot_nt(kg.astype(BF16), kt)
            qk = _dot_nt(qg.astype(BF16), kt)
            m = jnp.where(strict, bcol_ * kk * decay, 0.0)
            aqk = qk * decay
            n8 = jnp.where(blk8, m, 0.0)
            n2 = bdmm(n8, n8)
            n4 = bdmm(n2, n2)
            p1 = eye_f - n8
            p1 = p1 + bdmm(p1, n2)
            dinv = p1 + bdmm(p1, n4)
            for inner, outer in ((blk8, blk16), (blk16, blk32), (blk32, None)):
                off = jnp.logical_not(inner) if outer is None else jnp.logical_and(outer, jnp.logical_not(inner))
                dinv = dinv - bdmm(bdmm(dinv, jnp.where(off, m, 0.0)), dinv)
            u = bdmm(dinv, vg * bcol_)
            w = bdmm(dinv, kg * (bcol_ * egc))
            sb = s_scr[g]
            sbb = sb.astype(BF16)
            v_new = u - _dot(w.astype(BF16), sbb)
            o = _dot((qg * egc).astype(BF16), sbb) + bdmm(aqk, v_new)
            o_ref[0, rows, gl] = o
            kend = (kg * jnp.exp(glast - gcol)).astype(BF16)
            upd = _dot_tn(kend, v_new.astype(BF16))
            s_scr[g] = jnp.exp(glast) * sb + jnp.where(same_head, upd, 0.0)
        return carry

    lax.fori_loop(0, NCH, chunk, 0)

    if not has_s0:
        @pl.when(j == nblk - 1)
        def _():
            for h in range(GDN_HEADS):
                g, hh = divmod(h, GDN_G)
                sfin_ref[0, 0, h] = s_scr[g, hh * GDN_DK:(hh + 1) * GDN_DK, hh * GDN_DV:(hh + 1) * GDN_DV]


def _gdn(proj, conv_w, alog_p, dtb_p, s0, layer, ctx):
    nseq, nblk = (BATCH, SEQ // SEQ_BLK) if ctx else (DEC_BATCH, DEC_SEQ // SEQ_BLK)
    rb = _seq_row_block(ctx, nblk)
    cw = 2 * GDN_HEADS * GDN_DK + GDN_HEADS * GDN_DV
    hv = GDN_HEADS * GDN_DV
    halo = SEQ_BLK // HALO
    last_h = R // HALO - 1
    in_specs = [pl.BlockSpec((SEQ_BLK, cw), lambda d, s, j: (rb(d, s, j), PC_DQKV // cw)),
                pl.BlockSpec((HALO, cw), lambda d, s, j: (jnp.maximum(rb(d, s, j) * halo - 1, 0), PC_DQKV // cw)),
                pl.BlockSpec((HALO, cw), lambda d, s, j: (jnp.minimum((rb(d, s, j) + 1) * halo, last_h), PC_DQKV // cw)),
                pl.BlockSpec((SEQ_BLK, LANE), lambda d, s, j: (rb(d, s, j), PC_DAB // LANE + d)),
                pl.BlockSpec((1, 3, cw), lambda d, s, j: (layer, 0, 0)),
                pl.BlockSpec((1, 1, 1, LANE), lambda d, s, j: (layer, d, 0, 0)),
                pl.BlockSpec((1, 1, 1, LANE), lambda d, s, j: (layer, d, 0, 0))]
    args = [proj, proj, proj, proj, conv_w, alog_p, dtb_p]
    nrows = NCTX if ctx else NSMP
    roff = 0 if ctx else NCTX // SEQ_BLK
    o_spec = pl.BlockSpec((1, SEQ_BLK, hv), lambda d, s, j: (d, rb(d, s, j) - roff, 0))
    o_shape = jax.ShapeDtypeStruct((2, nrows, hv), F32)
    st_blk = (GDN_HEADS, GDN_DK, GDN_DV)
    if ctx:
        out_specs = [o_spec, pl.BlockSpec((1, 1) + st_blk, lambda d, s, j: (s, d, 0, 0, 0))]
        out_shape = [o_shape, jax.ShapeDtypeStruct((BATCH, 2) + st_blk, F32)]
    else:
        in_specs.append(pl.BlockSpec((1, 1, 1) + st_blk, lambda d, s, j: (s, layer, d, 0, 0, 0)))
        args.append(s0)
        out_specs, out_shape = o_spec, o_shape
    return pl.pallas_call(
        functools.partial(_gdn_kernel, nblk=nblk, has_s0=not ctx),
        grid=(2, nseq, nblk),
        in_specs=in_specs, out_specs=out_specs, out_shape=out_shape,
        scratch_shapes=[pltpu.VMEM((SEQ_BLK, GDN_HEADS * GDN_DK), F32),
                        pltpu.VMEM((SEQ_BLK, GDN_HEADS * GDN_DK), F32),
                        pltpu.VMEM((SEQ_BLK, hv), F32),
                        pltpu.VMEM((GDN_HEADS // GDN_G, GDN_GW, GDN_GW), F32)],
        compiler_params=_cp(("arbitrary", "arbitrary", "arbitrary")),
        name="gdn_ctx" if ctx else "gdn_smp",
    )(*args)


def _group_rms(x, width):
    parts = []
    lane = lax.broadcasted_iota(jnp.int32, (x.shape[0], LANE), 1)
    lo_half = lane < 64
    for c in range(x.shape[1] // LANE):
        xc = x[:, c * LANE:(c + 1) * LANE]
        sq = xc * xc
        if width == LANE:
            ms = jnp.mean(sq, axis=-1, keepdims=True)
        else:
            s_lo = jnp.sum(jnp.where(lo_half, sq, 0.0), axis=-1, keepdims=True)
            s_hi = jnp.sum(jnp.where(lo_half, 0.0, sq), axis=-1, keepdims=True)
            ms = jnp.where(lo_half, s_lo, s_hi) * (1.0 / width)
        parts.append(xc * lax.rsqrt(ms + EPS))
    return jnp.concatenate(parts, axis=1)


def _merge_kernel(ym_ref, og_ref, gr_ref, od_ref, dz_ref, gt_ref, h_ref, ga_ref,
                  gn_ref, dn_ref, wb_ref, wo_ref, o_ref):
    y_gla = _group_rms(og_ref[0] + og_ref[1], GLA_DV) * gn_ref[0] * _silu(gr_ref[...].astype(F32))
    y_gdn = _group_rms(od_ref[0] + od_ref[1], GDN_DV) * dn_ref[0] * _silu(dz_ref[...].astype(F32))
    gates = jax.nn.sigmoid(gt_ref[...].astype(F32))
    d = D_MODEL
    acc = gates[:, :d] * _dot(ym_ref[...], wb_ref[0, 0])
    acc = acc + gates[:, d:2 * d] * _dot(y_gla.astype(BF16), wb_ref[0, 1])
    acc = acc + gates[:, 2 * d:] * _dot(y_gdn.astype(BF16), wb_ref[0, 2])
    y = _dot(acc.astype(BF16), wo_ref[0])
    o_ref[...] = h_ref[...] + ga_ref[0] * y


def _merge(y_mla, o_gla, o_gdn, proj, h, mod3, gla_norm_p, gdn_norm_p, w_branch_b, w_out_b, layer):
    tm = TM_MERGE
    bw = 512
    return pl.pallas_call(
        _merge_kernel,
        grid=(R // tm,),
        in_specs=[pl.BlockSpec((tm, bw), lambda i: (i, 0)),
                  pl.BlockSpec((2, tm, bw), lambda i: (0, i, 0)),
                  pl.BlockSpec((tm, bw), lambda i: (i, PC_GR // bw)),
                  pl.BlockSpec((2, tm, bw), lambda i: (0, i, 0)),
                  pl.BlockSpec((tm, bw), lambda i: (i, PC_DZ // bw)),
                  pl.BlockSpec((tm, 3 * D_MODEL), lambda i: (i, PC_GATES // (3 * D_MODEL))),
                  pl.BlockSpec((tm, D_MODEL), lambda i: (i, 0)),
                  _mod_spec(layer, 2, tm),
                  pl.BlockSpec((1, 1, bw), lambda i: (layer, 0, 0)),
                  pl.BlockSpec((1, 1, bw), lambda i: (layer, 0, 0)),
                  pl.BlockSpec((1, 3, bw, D_MODEL), lambda i: (layer, 0, 0, 0)),
                  pl.BlockSpec((1, D_MODEL, D_MODEL), lambda i: (layer, 0, 0))],
        out_specs=pl.BlockSpec((tm, D_MODEL), lambda i: (i, 0)),
        out_shape=jax.ShapeDtypeStruct((R, D_MODEL), F32),
        compiler_params=_cp(("arbitrary",)),
        name="merge",
    )(y_mla, o_gla, proj, o_gdn, proj, proj, h, mod3, gla_norm_p, gdn_norm_p, w_branch_b, w_out_b)


def _ffn_kernel(h_ref, g_ref, sc_ref, sh_ref, gf_ref, wg_ref, wu_ref, wo_ref, fn_ref, o_ref,
                xf_ref, acc_ref, *, final):
    f = pl.program_id(1)

    @pl.when(f == 0)
    def _():
        y = _rms(h_ref[...], g_ref[0])
        xf_ref[...] = (y * (1.0 + sc_ref[0]) + sh_ref[0]).astype(BF16)
        acc_ref[...] = jnp.zeros_like(acc_ref)

    xf = xf_ref[...]
    a = _silu(_dot(xf, wg_ref[0])) * _dot(xf, wu_ref[0])
    acc_ref[...] += _dot(a.astype(BF16), wo_ref[0])

    @pl.when(f == pl.num_programs(1) - 1)
    def _():
        out = h_ref[...] + gf_ref[0] * acc_ref[...]
        if final:
            out = _rms(out, fn_ref[...])
        o_ref[...] = out


def _ffn(h, mod3, norm_ffn, wg_b, wu_b, wo_b, final_norm, layer, final):
    tm, tf = TM_FFN, TF_FFN
    return pl.pallas_call(
        functools.partial(_ffn_kernel, final=final),
        grid=(R // tm, D_FF // tf),
        in_specs=[pl.BlockSpec((tm, D_MODEL), lambda i, f: (i, 0)),
                  pl.BlockSpec((1, 1, D_MODEL), lambda i, f: (layer, 0, 0)),
                  _mod_spec(layer, 4, tm),
                  _mod_spec(layer, 3, tm),
                  _mod_spec(layer, 5, tm),
                  pl.BlockSpec((1, D_MODEL, tf), lambda i, f: (layer, 0, f)),
                  pl.BlockSpec((1, D_MODEL, tf), lambda i, f: (layer, 0, f)),
                  pl.BlockSpec((1, tf, D_MODEL), lambda i, f: (layer, f, 0)),
                  pl.BlockSpec((1, D_MODEL), lambda i, f: (0, 0))],
        out_specs=pl.BlockSpec((tm, D_MODEL), lambda i, f: (i, 0)),
        out_shape=jax.ShapeDtypeStruct((R, D_MODEL), F32),
        scratch_shapes=[pltpu.VMEM((tm, D_MODEL), BF16), pltpu.VMEM((tm, D_MODEL), F32)],
        compiler_params=_cp(("arbitrary", "arbitrary")),
        name="ffn_final" if final else "ffn",
    )(h, norm_ffn.reshape(DEPTH, 1, D_MODEL), mod3, mod3, mod3, wg_b, wu_b, wo_b,
      final_norm.reshape(1, D_MODEL))


def _rope_partner(w):
    h = AXIS_DIM // 2
    parts = []
    for a in range(2):
        x1 = w[..., a * AXIS_DIM:a * AXIS_DIM + h]
        x2 = w[..., a * AXIS_DIM + h:(a + 1) * AXIS_DIM]
        parts += [-x2, x1]
    return jnp.concatenate(parts, axis=-1)


def _pack_w_in(w_in):
    c = [0]
    for s in (Q_LORA, KV_LORA + ROPE_DIM, 256, 256, 512, 512, 2 * GLA_RANK, 1536, 512,
              2 * GDN_HEADS, 2 * GDN_HEADS, 3 * D_MODEL):
        c.append(c[-1] + s)
    seg = [w_in[..., c[i]:c[i + 1]] for i in range(12)]
    mq, mkv, gq, gk, gv, gr, glr, dqkv, dz, da, db, gates = seg

    def zeros(n):
        return jnp.zeros(w_in.shape[:-1] + (n,), w_in.dtype)

    kr = mkv[..., KV_LORA:]
    mla = jnp.concatenate([mq, mkv[..., :KV_LORA], kr, _rope_partner(kr), zeros(64)], axis=-1)
    dab = []
    for d in range(2):
        dab += [da[..., d * GDN_HEADS:(d + 1) * GDN_HEADS], db[..., d * GDN_HEADS:(d + 1) * GDN_HEADS],
                zeros(LANE - 2 * GDN_HEADS)]
    packed = jnp.concatenate([gates, dqkv, dz, gv, gr, mla, gq, gk, glr, zeros(LANE - 2 * GLA_RANK)]
                             + dab + [zeros(PROJ_W - PC_DAB - 2 * LANE)], axis=-1)
    return packed.astype(BF16)


def _pack_mla_weights(w_uq, w_ukv):
    l = w_uq.shape[0]
    qh = w_uq.reshape(l, Q_LORA, MLA_HEADS, MLA_NOPE + ROPE_DIM)
    zq = jnp.zeros((l, Q_LORA, MLA_HEADS, HEAD_W - MLA_NOPE - ROPE_DIM), w_uq.dtype)
    zn = jnp.zeros((l, Q_LORA, MLA_HEADS, MLA_NOPE), w_uq.dtype)
    wq = jnp.concatenate([qh, zq], axis=-1).reshape(l, Q_LORA, MLA_HEADS * HEAD_W)
    wq_sw = jnp.concatenate([zn, _rope_partner(qh[..., MLA_NOPE:]), zq], axis=-1).reshape(l, Q_LORA, MLA_HEADS * HEAD_W)
    wq_p = jnp.concatenate([wq, wq_sw], axis=-1).astype(BF16)

    kvh = w_ukv.reshape(l, KV_LORA, MLA_HEADS, MLA_NOPE + MLA_V)
    zk = jnp.zeros((l, KV_LORA, MLA_HEADS, HEAD_W - MLA_NOPE), w_ukv.dtype)
    wk_top = jnp.concatenate([kvh[..., :MLA_NOPE], zk], axis=-1).reshape(l, KV_LORA, MLA_HEADS * HEAD_W)
    place = jnp.zeros((KV_LORA, MLA_HEADS, HEAD_W), w_ukv.dtype)
    idx = jnp.arange(ROPE_DIM)
    for rep in range(2):
        place = place.at[rep * ROPE_DIM + idx, :, MLA_NOPE + idx].set(1.0)
    wk_bot = jnp.broadcast_to(place.reshape(1, KV_LORA, MLA_HEADS * HEAD_W), (l, KV_LORA, MLA_HEADS * HEAD_W))
    wk_p = jnp.concatenate([wk_top, wk_bot], axis=1).astype(BF16)
    zv = jnp.zeros((l, KV_LORA, MLA_HEADS, HEAD_W - MLA_V), w_ukv.dtype)
    wv_p = jnp.concatenate([kvh[..., MLA_NOPE:], zv], axis=-1).reshape(l, KV_LORA, MLA_HEADS * HEAD_W).astype(BF16)
    return wq_p, wk_p, wv_p


def _rope_tables(tm):
    rows = DEC_SEQ // GRID_W
    row = jnp.repeat(jnp.arange(rows, dtype=F32), GRID_W)
    col = jnp.tile(jnp.arange(GRID_W, dtype=F32), rows)
    inv = ROPE_THETA ** (-jnp.arange(0, AXIS_DIM, 2, dtype=F32) / AXIS_DIM)
    ang_r, ang_c = row[:, None] * inv, col[:, None] * inv
    cos32 = jnp.concatenate([jnp.cos(ang_r)] * 2 + [jnp.cos(ang_c)] * 2, axis=-1)
    sin32 = jnp.concatenate([jnp.sin(ang_r)] * 2 + [jnp.sin(ang_c)] * 2, axis=-1)
    cos32 = jnp.concatenate([jnp.ones((tm, ROPE_DIM), F32), cos32], axis=0)
    sin32 = jnp.concatenate([jnp.zeros((tm, ROPE_DIM), F32), sin32], axis=0)
    n = cos32.shape[0]
    scale = (MLA_NOPE + ROPE_DIM) ** -0.5
    pad = HEAD_W - MLA_NOPE - ROPE_DIM
    cq = jnp.concatenate([jnp.ones((n, MLA_NOPE), F32), cos32, jnp.zeros((n, pad), F32)], axis=-1) * scale
    sq = jnp.concatenate([jnp.zeros((n, MLA_NOPE), F32), sin32, jnp.zeros((n, pad), F32)], axis=-1) * scale
    ck = jnp.concatenate([cos32, sin32, jnp.zeros((n, HEAD_W - 2 * ROPE_DIM), F32)], axis=-1)
    return jnp.concatenate([cq, sq, ck], axis=-1)


def _pad_lanes(x, n):
    return jnp.pad(x, [(0, 0)] * (x.ndim - 1) + [(0, n - x.shape[-1])])


def kernel(x_prompt, x_sample, cache_mla, state_gla, state_gdn, c, c_ctx, w_mod, b_mod, norm_mix, w_in,
           mla_q_norm, mla_w_uq, mla_kv_norm, mla_w_ukv, gla_w_gate, gla_b_gate, gla_norm, gdn_conv,
           gdn_a_log, gdn_dt_bias, gdn_norm, w_branch, w_out, norm_ffn, ffn_w_in, ffn_w_out, final_norm):
    w_in_p = _pack_w_in(w_in)
    wq_p, wk_p, wv_p = _pack_mla_weights(mla_w_uq, mla_w_ukv)
    tab = _rope_tables(TM_PREP)
    wg_p = jnp.zeros((DEPTH, 2, LANE, GLA_HEADS * GLA_DK), F32)
    for d in range(2):
        wg_p = wg_p.at[:, d, d * GLA_RANK:(d + 1) * GLA_RANK, :].set(gla_w_gate[:, d])
    wg_p = wg_p.astype(BF16)
    bg_p = gla_b_gate.reshape(DEPTH, 2, 1, GLA_HEADS * GLA_DK)
    alog_p = _pad_lanes(gdn_a_log, LANE).reshape(DEPTH, 2, 1, LANE)
    dtb_p = _pad_lanes(gdn_dt_bias, LANE).reshape(DEPTH, 2, 1, LANE)
    gla_norm_p = jnp.tile(gla_norm, (1, GLA_HEADS)).reshape(DEPTH, 1, GLA_HEADS * GLA_DV)
    gdn_norm_p = jnp.tile(gdn_norm, (1, GDN_HEADS)).reshape(DEPTH, 1, GDN_HEADS * GDN_DV)
    w_branch_b = w_branch.astype(BF16)
    w_out_b = w_out.astype(BF16)
    wg_b = ffn_w_in[..., :D_FF].astype(BF16)
    wu_b = ffn_w_in[..., D_FF:].astype(BF16)
    wo_b = ffn_w_out.astype(BF16)
    cache_p = _pad_lanes(cache_mla, 2 * KV_LORA)
    cond8 = jnp.concatenate([c_ctx[None, :], c, jnp.zeros((MOD_ROWS - 1 - DEC_BATCH, D_MODEL), F32)], axis=0)

    mod = _modulation(cond8, w_mod, b_mod)
    mod3 = mod.reshape(DEPTH * MOD_ROWS * 6, 1, D_MODEL)
    kc, vc = _kv_cache(cache_p, wk_p, wv_p)

    h = jnp.concatenate([x_prompt.reshape(NCTX, D_MODEL), x_sample.reshape(NSMP, D_MODEL)], axis=0)
    kv_list, gla_list, gdn_list = [], [], []
    for l in range(DEPTH):
        proj = _inproj(h, mod3, norm_mix, w_in_p, l)
        qp, kp, vp, own = _mla_prep(proj, tab, mla_q_norm, mla_kv_norm, wq_p, wk_p, wv_p, l)
        y_mla = jnp.concatenate([_attention_ctx(qp, kp, vp), _attention_smp(qp, kp, vp, kc, vc, l)], axis=0)
        og_c, sg = _gla(proj, wg_p, bg_p, None, l, True)
        og_s = _gla(proj, wg_p, bg_p, state_gla, l, False)
        od_c, sd = _gdn(proj, gdn_conv, alog_p, dtb_p, None, l, True)
        od_s = _gdn(proj, gdn_conv, alog_p, dtb_p, state_gdn, l, False)
        o_gla = jnp.concatenate([og_c, og_s], axis=1)
        o_gdn = jnp.concatenate([od_c, od_s], axis=1)
        h = _merge(y_mla, o_gla, o_gdn, proj, h, mod3, gla_norm_p, gdn_norm_p, w_branch_b, w_out_b, l)
        h = _ffn(h, mod3, norm_ffn, wg_b, wu_b, wo_b, final_norm, l, l == DEPTH - 1)
        kv_list.append(own[:NCTX, :KV_LORA + ROPE_DIM].reshape(BATCH, SEQ, KV_LORA + ROPE_DIM))
        gla_list.append(sg)
        gdn_list.append(sd)

    y_prompt = h[:NCTX].reshape(BATCH, SEQ, D_MODEL)
    y_sample = h[NCTX:].reshape(DEC_BATCH, DEC_SEQ, D_MODEL)
    return (y_prompt, y_sample, jnp.stack(kv_list, axis=1), jnp.stack(gla_list, axis=1),
            jnp.stack(gdn_list, axis=1))
```

```python
import functools

import jax
import jax.numpy as jnp
from jax import lax
from jax.experimental import pallas as pl
from jax.experimental.pallas import tpu as pltpu

F32 = jnp.float32
BF16 = jnp.bfloat16

D_MODEL = 1024
BATCH = 16
SEQ = 256
DEPTH = 2
DEC_BATCH = 2
DEC_SEQ = 4096
PAST_LEN = 256
GRID_W = 64
CHUNK = 64
EPS = 1e-6
MLA_HEADS = 8
MLA_NOPE = 64
ROPE_DIM = 32
AXIS_DIM = ROPE_DIM // 2
MLA_V = 64
Q_LORA = 256
KV_LORA = 128
ROPE_THETA = 10000.0
GLA_HEADS = 4
GLA_DK = 64
GLA_DV = 128
GLA_RANK = 16
GLA_TAU = 16.0
GDN_HEADS = 8
GDN_DK = 64
GDN_DV = 64
D_FF = ((8 * D_MODEL + 3 * 256 - 1) // (3 * 256)) * 256
MOD_W = 6 * D_MODEL

NCTX = BATCH * SEQ
NSMP = DEC_BATCH * DEC_SEQ
R = NCTX + NSMP
LANE = 128
HEAD_W = 128
SEQ_BLK = 256
NCH = SEQ_BLK // CHUNK
MOD_ROWS = 8

PC_GATES = 0
PC_DQKV = 3072
PC_DZ = 4608
PC_GV = 5120
PC_GR = 5632
PC_MLA = 6144
PC_GQ = 6656
PC_GK = 6912
PC_GLR = 7168
PC_DAB = 7296
PROJ_W = 7680

TM_IN = 512
TN_IN = 1536
TM_PREP = 512
TM_MERGE = 512
TM_FFN = 512
TF_FFN = 1408
TQ_ATT = 256
VMEM_LIMIT = 48 * 1024 * 1024


def _cp(sem):
    return pltpu.CompilerParams(dimension_semantics=sem, vmem_limit_bytes=VMEM_LIMIT)


def _dot(a, b):
    return jnp.dot(a, b, preferred_element_type=F32)


def _dot_nt(a, b):
    return lax.dot_general(a, b, (((1,), (1,)), ((), ())), preferred_element_type=F32)


def _dot_tn(a, b):
    return lax.dot_general(a, b, (((0,), (0,)), ((), ())), preferred_element_type=F32)


def _split2(x):
    hi = x.astype(BF16)
    lo = (x - hi.astype(F32)).astype(BF16)
    return hi, lo


def _split3(x):
    hi = x.astype(BF16)
    r1 = x - hi.astype(F32)
    mid = r1.astype(BF16)
    lo = (r1 - mid.astype(F32)).astype(BF16)
    return hi, mid, lo


def _cumsum_rows(tri_b, x):
    hi, mid, lo = _split3(x)
    return _dot(tri_b, hi) + (_dot(tri_b, mid) + _dot(tri_b, lo))


def _silu(x):
    return x * jax.nn.sigmoid(x)


def _softplus(x):
    return jnp.maximum(x, 0.0) + jnp.log(1.0 + jnp.exp(-jnp.abs(x)))


def _log_sigmoid(x):
    return jnp.minimum(x, 0.0) - jnp.log(1.0 + jnp.exp(-jnp.abs(x)))


def _rms(x, g):
    return x * lax.rsqrt(jnp.mean(x * x, axis=-1, keepdims=True) + EPS) * g


def _mod_row(row_start):
    return jnp.where(row_start < NCTX, 0, 1 + (row_start - NCTX) // DEC_SEQ)


def _mod_spec(layer, which, tm):
    def imap(i, *_):
        return ((layer * MOD_ROWS + _mod_row(i * tm)) * 6 + which, 0, 0)
    return pl.BlockSpec((1, 1, D_MODEL), imap)


def _mod_kernel(c_ref, w_ref, b_ref, o_ref):
    x = _silu(c_ref[...])
    xh, xl = _split2(x)
    wh, wl = _split2(w_ref[0])
    o_ref[0] = _dot(xh, wh) + (_dot(xl, wh) + _dot(xh, wl)) + b_ref[0]


def _modulation(cond8, w_mod, b_mod):
    tn = 1536
    return pl.pallas_call(
        _mod_kernel,
        grid=(DEPTH, MOD_W // tn),
        in_specs=[pl.BlockSpec((MOD_ROWS, D_MODEL), lambda l, j: (0, 0)),
                  pl.BlockSpec((1, D_MODEL, tn), lambda l, j: (l, 0, j)),
                  pl.BlockSpec((1, 1, tn), lambda l, j: (l, 0, j))],
        out_specs=pl.BlockSpec((1, MOD_ROWS, tn), lambda l, j: (l, 0, j)),
        out_shape=jax.ShapeDtypeStruct((DEPTH, MOD_ROWS, MOD_W), F32),
        compiler_params=_cp(("arbitrary", "arbitrary")),
        name="modulation",
    )(cond8, w_mod, b_mod.reshape(DEPTH, 1, MOD_W))


def _inproj_kernel(h_ref, g_ref, sc_ref, sh_ref, w_ref, o_ref, xn_ref):
    @pl.when(pl.program_id(1) == 0)
    def _():
        y = _rms(h_ref[...], g_ref[0])
        xn_ref[...] = (y * (1.0 + sc_ref[0]) + sh_ref[0]).astype(BF16)

    o_ref[...] = _dot(xn_ref[...], w_ref[0]).astype(BF16)


def _inproj(h, mod3, norm_mix, w_in_p, layer):
    return pl.pallas_call(
        _inproj_kernel,
        grid=(R // TM_IN, PROJ_W // TN_IN),
        in_specs=[pl.BlockSpec((TM_IN, D_MODEL), lambda i, j: (i, 0)),
                  pl.BlockSpec((1, 1, D_MODEL), lambda i, j: (layer, 0, 0)),
                  _mod_spec(layer, 1, TM_IN),
                  _mod_spec(layer, 0, TM_IN),
                  pl.BlockSpec((1, D_MODEL, TN_IN), lambda i, j: (layer, 0, j))],
        out_specs=pl.BlockSpec((TM_IN, TN_IN), lambda i, j: (i, j)),
        out_shape=jax.ShapeDtypeStruct((R, PROJ_W), BF16),
        scratch_shapes=[pltpu.VMEM((TM_IN, D_MODEL), BF16)],
        compiler_params=_cp(("arbitrary", "arbitrary")),
        name="inproj",
    )(h, norm_mix.reshape(DEPTH, 1, D_MODEL), mod3, mod3, w_in_p)


def _mla_prep_kernel(pm_ref, tab_ref, qn_ref, kvn_ref, wq_ref, wk_ref, wv_ref,
                     q_ref, k_ref, v_ref, own_ref):
    pm = pm_ref[...].astype(F32)
    tab = tab_ref[...]
    qn = _rms(pm[:, :Q_LORA], qn_ref[0]).astype(BF16)
    q2 = _dot(qn, wq_ref[0])
    hw = MLA_HEADS * HEAD_W
    cq = jnp.tile(tab[:, :HEAD_W], (1, MLA_HEADS))
    sq = jnp.tile(tab[:, HEAD_W:2 * HEAD_W], (1, MLA_HEADS))
    q_ref[...] = (q2[:, :hw] * cq + q2[:, hw:] * sq).astype(BF16)
    ckv = _rms(pm[:, Q_LORA:Q_LORA + KV_LORA], kvn_ref[0])
    kr = pm[:, Q_LORA + KV_LORA:] * tab[:, 2 * HEAD_W:]
    lhs = jnp.concatenate([ckv, kr], axis=1)
    own_ref[...] = lhs
    lb = lhs.astype(BF16)
    k_ref[...] = _dot(lb, wk_ref[0]).astype(BF16)
    v_ref[...] = _dot(lb[:, :KV_LORA], wv_ref[0]).astype(BF16)


def _mla_prep(proj, tab, q_norm, kv_norm, wq_p, wk_p, wv_p, layer):
    tm = TM_PREP
    hw = MLA_HEADS * HEAD_W

    def tab_map(i):
        r0 = i * tm
        return (jnp.where(r0 < NCTX, 0, 1 + ((r0 - NCTX) % DEC_SEQ) // tm), 0)

    return pl.pallas_call(
        _mla_prep_kernel,
        grid=(R // tm,),
        in_specs=[pl.BlockSpec((tm, 512), lambda i: (i, PC_MLA // 512)),
                  pl.BlockSpec((tm, 3 * HEAD_W), tab_map),
                  pl.BlockSpec((1, 1, Q_LORA), lambda i: (layer, 0, 0)),
                  pl.BlockSpec((1, 1, KV_LORA), lambda i: (layer, 0, 0)),
                  pl.BlockSpec((1, Q_LORA, 2 * hw), lambda i: (layer, 0, 0)),
                  pl.BlockSpec((1, 2 * KV_LORA, hw), lambda i: (layer, 0, 0)),
                  pl.BlockSpec((1, KV_LORA, hw), lambda i: (layer, 0, 0))],
        out_specs=[pl.BlockSpec((tm, hw), lambda i: (i, 0)),
                   pl.BlockSpec((tm, hw), lambda i: (i, 0)),
                   pl.BlockSpec((tm, hw), lambda i: (i, 0)),
                   pl.BlockSpec((tm, 2 * KV_LORA), lambda i: (i, 0))],
        out_shape=[jax.ShapeDtypeStruct((R, hw), BF16),
                   jax.ShapeDtypeStruct((R, hw), BF16),
                   jax.ShapeDtypeStruct((R, hw), BF16),
                   jax.ShapeDtypeStruct((R, 2 * KV_LORA), F32)],
        compiler_params=_cp(("arbitrary",)),
        name="mla_prep",
    )(proj, tab, q_norm.reshape(DEPTH, 1, Q_LORA), kv_norm.reshape(DEPTH, 1, KV_LORA),
      wq_p, wk_p, wv_p)


def _kv_cache_kernel(c_ref, wk_ref, wv_ref, k_ref, v_ref):
    lb = c_ref[0, 0].astype(BF16)
    k_ref[0, 0] = _dot(lb, wk_ref[0]).astype(BF16)
    v_ref[0, 0] = _dot(lb[:, :KV_LORA], wv_ref[0]).astype(BF16)


def _kv_cache(cache_p, wk_p, wv_p):
    hw = MLA_HEADS * HEAD_W
    return pl.pallas_call(
        _kv_cache_kernel,
        grid=(DEC_BATCH, DEPTH),
        in_specs=[pl.BlockSpec((1, 1, PAST_LEN, 2 * KV_LORA), lambda b, l: (b, l, 0, 0)),
                  pl.BlockSpec((1, 2 * KV_LORA, hw), lambda b, l: (l, 0, 0)),
                  pl.BlockSpec((1, KV_LORA, hw), lambda b, l: (l, 0, 0))],
        out_specs=[pl.BlockSpec((1, 1, PAST_LEN, hw), lambda b, l: (b, l, 0, 0)),
                   pl.BlockSpec((1, 1, PAST_LEN, hw), lambda b, l: (b, l, 0, 0))],
        out_shape=[jax.ShapeDtypeStruct((DEC_BATCH, DEPTH, PAST_LEN, hw), BF16)] * 2,
        compiler_params=_cp(("arbitrary", "arbitrary")),
        name="kv_cache",
    )(cache_p, wk_p, wv_p)


def _attn_kernel(*refs, heads, has_cache):
    if has_cache:
        q_ref, k_ref, v_ref, kc_ref, vc_ref, o_ref = refs
    else:
        q_ref, k_ref, v_ref, o_ref = refs
    outs = []
    for h in range(heads):
        sl = slice(h * HEAD_W, (h + 1) * HEAD_W)
        q = q_ref[:, sl]
        s = _dot_nt(q, k_ref[:, sl])
        m = jnp.max(s, axis=-1, keepdims=True)
        if has_cache:
            sc = _dot_nt(q, kc_ref[0, 0, :, sl])
            m = jnp.maximum(m, jnp.max(sc, axis=-1, keepdims=True))
        p = jnp.exp(s - m)
        l = jnp.sum(p, axis=-1, keepdims=True)
        o = _dot(p.astype(BF16), v_ref[:, sl])
        if has_cache:
            pc = jnp.exp(sc - m)
            l = l + jnp.sum(pc, axis=-1, keepdims=True)
            o = o + _dot(pc.astype(BF16), vc_ref[0, 0, :, sl])
        outs.append(o / l)
    pairs = [outs[2 * i] + pltpu.roll(outs[2 * i + 1], MLA_V, 1) for i in range(heads // 2)]
    o_ref[...] = jnp.concatenate(pairs, axis=1).astype(BF16)


def _attention_ctx(qp, kp, vp):
    hw = MLA_HEADS * HEAD_W
    return pl.pallas_call(
        functools.partial(_attn_kernel, heads=MLA_HEADS, has_cache=False),
        grid=(BATCH,),
        in_specs=[pl.BlockSpec((SEQ, hw), lambda s: (s, 0))] * 3,
        out_specs=pl.BlockSpec((SEQ, MLA_HEADS * MLA_V), lambda s: (s, 0)),
        out_shape=jax.ShapeDtypeStruct((NCTX, MLA_HEADS * MLA_V), BF16),
        compiler_params=_cp(("arbitrary",)),
        name="attn_ctx",
    )(qp, kp, vp)


def _attention_smp(qp, kp, vp, kc, vc, layer):
    hpb = 2
    wq = hpb * HEAD_W
    nq = DEC_SEQ // TQ_ATT
    qoff = NCTX // TQ_ATT
    koff = NCTX // DEC_SEQ
    return pl.pallas_call(
        functools.partial(_attn_kernel, heads=hpb, has_cache=True),
        grid=(DEC_BATCH, MLA_HEADS // hpb, nq),
        in_specs=[pl.BlockSpec((TQ_ATT, wq), lambda b, g, i: (qoff + b * nq + i, g)),
                  pl.BlockSpec((DEC_SEQ, wq), lambda b, g, i: (koff + b, g)),
                  pl.BlockSpec((DEC_SEQ, wq), lambda b, g, i: (koff + b, g)),
                  pl.BlockSpec((1, 1, PAST_LEN, wq), lambda b, g, i: (b, layer, 0, g)),
                  pl.BlockSpec((1, 1, PAST_LEN, wq), lambda b, g, i: (b, layer, 0, g))],
        out_specs=pl.BlockSpec((TQ_ATT, hpb * MLA_V), lambda b, g, i: (b * nq + i, g)),
        out_shape=jax.ShapeDtypeStruct((NSMP, MLA_HEADS * MLA_V), BF16),
        compiler_params=_cp(("arbitrary", "arbitrary", "arbitrary")),
        name="attn_smp",
    )(qp, kp, vp, kc, vc)


def _seq_row_block(ctx, nblk):
    if ctx:
        return lambda d, s, j: s
    off = NCTX // SEQ_BLK
    return lambda d, s, j: off + s * nblk + j + d * (nblk - 1 - 2 * j)


def _chunk_masks(fwd):
    row = lax.broadcasted_iota(jnp.int32, (CHUNK, CHUNK), 0)
    col = lax.broadcasted_iota(jnp.int32, (CHUNK, CHUNK), 1)
    incl = (row - col) * jnp.where(fwd, 1, -1) >= 0
    return row, col, incl


def _gla_kernel(*refs, nblk, has_s0):
    if has_s0:
        q_ref, k_ref, v_ref, glr_ref, wg_ref, bg_ref, s0_ref, o_ref, s_scr = refs
    else:
        q_ref, k_ref, v_ref, glr_ref, wg_ref, bg_ref, o_ref, sfin_ref, s_scr = refs
    d = pl.program_id(0)
    j = pl.program_id(2)
    fwd = d == 0

    @pl.when(j == 0)
    def _():
        if has_s0:
            s_scr[...] = s0_ref[0, 0, 0]
        else:
            s_scr[...] = jnp.zeros_like(s_scr)

    _, _, incl = _chunk_masks(fwd)
    tri_b = jnp.where(incl, 1.0, 0.0).astype(BF16)
    wg = wg_ref[0, 0]
    bg = bg_ref[0, 0]

    ch = []
    for c in range(NCH):
        ci = c + d * (NCH - 1 - 2 * c)
        rows = pl.ds(pl.multiple_of(ci * CHUNK, CHUNK), CHUNK)
        ch.append(dict(rows=rows, z=_dot(glr_ref[rows, :], wg) + bg))
    for t in ch:
        t["la"] = _split3(_log_sigmoid(t["z"]) * (1.0 / GLA_TAU))
    for t in ch:
        hi, mid, lo = t["la"]
        t["bc"] = _dot(tri_b, hi) + (_dot(tri_b, mid) + _dot(tri_b, lo))
    for t in ch:
        rows, bc = t["rows"], t["bc"]
        q = q_ref[rows, :].astype(F32) * (GLA_DK ** -0.5)
        k = k_ref[rows, :].astype(F32)
        bl = jnp.where(fwd, bc[CHUNK - 1:CHUNK], bc[0:1])
        t["qd"] = (q * jnp.exp(bc)).astype(BF16)
        t["ki"] = (k * jnp.exp(-bc)).astype(BF16)
        t["ke"] = (k * jnp.exp(bl - bc)).astype(BF16)
        t["dec_t"] = jnp.transpose(jnp.broadcast_to(jnp.exp(bl), (LANE, GLA_HEADS * GLA_DK)))
        t["v"] = v_ref[rows, :]
    heads = [(slice(h * GLA_DK, (h + 1) * GLA_DK), slice(h * GLA_DV, (h + 1) * GLA_DV))
             for h in range(GLA_HEADS)]
    for t in ch:
        t["a"] = [jnp.where(incl, _dot_nt(t["qd"][:, sk], t["ki"][:, sk]), 0.0).astype(BF16)
                  for sk, _ in heads]
    for t in ch:
        t["oi"] = [_dot(t["a"][h], t["v"][:, sv]) for h, (_, sv) in enumerate(heads)]
        t["upd"] = [_dot_tn(t["ke"][:, sk], t["v"][:, sv]) for sk, sv in heads]
    states = [s_scr[h] for h in range(GLA_HEADS)]
    for t in ch:
        for h, (sk, sv) in enumerate(heads):
            o_ref[0, t["rows"], sv] = t["oi"][h] + _dot(t["qd"][:, sk], states[h].astype(BF16))
            states[h] = t["dec_t"][sk, :] * states[h] + t["upd"][h]
    for h in range(GLA_HEADS):
        s_scr[h] = states[h]

    if not has_s0:
        @pl.when(j == nblk - 1)
        def _():
            sfin_ref[0, 0] = s_scr[...]


def _gla(proj, wg_p, bg_p, s0, layer, ctx):
    nseq, nblk = (BATCH, SEQ // SEQ_BLK) if ctx else (DEC_BATCH, DEC_SEQ // SEQ_BLK)
    rb = _seq_row_block(ctx, nblk)
    hk = GLA_HEADS * GLA_DK
    hv = GLA_HEADS * GLA_DV
    in_specs = [pl.BlockSpec((SEQ_BLK, hk), lambda d, s, j: (rb(d, s, j), PC_GQ // hk)),
                pl.BlockSpec((SEQ_BLK, hk), lambda d, s, j: (rb(d, s, j), PC_GK // hk)),
                pl.BlockSpec((SEQ_BLK, hv), lambda d, s, j: (rb(d, s, j), PC_GV // hv)),
                pl.BlockSpec((SEQ_BLK, LANE), lambda d, s, j: (rb(d, s, j), PC_GLR // LANE)),
                pl.BlockSpec((1, 1, LANE, hk), lambda d, s, j: (layer, d, 0, 0)),
                pl.BlockSpec((1, 1, 1, hk), lambda d, s, j: (layer, d, 0, 0))]
    args = [proj, proj, proj, proj, wg_p, bg_p]
    nrows = NCTX if ctx else NSMP
    roff = 0 if ctx else NCTX // SEQ_BLK
    o_spec = pl.BlockSpec((1, SEQ_BLK, hv), lambda d, s, j: (d, rb(d, s, j) - roff, 0))
    o_shape = jax.ShapeDtypeStruct((2, nrows, hv), F32)
    st_blk = (GLA_HEADS, GLA_DK, GLA_DV)
    if ctx:
        out_specs = [o_spec, pl.BlockSpec((1, 1) + st_blk, lambda d, s, j: (s, d, 0, 0, 0))]
        out_shape = [o_shape, jax.ShapeDtypeStruct((BATCH, 2) + st_blk, F32)]
    else:
        in_specs.append(pl.BlockSpec((1, 1, 1) + st_blk, lambda d, s, j: (s, layer, d, 0, 0, 0)))
        args.append(s0)
        out_specs, out_shape = o_spec, o_shape
    return pl.pallas_call(
        functools.partial(_gla_kernel, nblk=nblk, has_s0=not ctx),
        grid=(2, nseq, nblk),
        in_specs=in_specs, out_specs=out_specs, out_shape=out_shape,
        scratch_shapes=[pltpu.VMEM(st_blk, F32)],
        compiler_params=_cp(("arbitrary", "arbitrary", "arbitrary")),
        name="gla_ctx" if ctx else "gla_smp",
    )(*args)


GDN_G = 4
GDN_GW = GDN_G * GDN_DK
HALO = 16


def _gdn_kernel(*refs, nblk, has_s0):
    if has_s0:
        (x_ref, xp_ref, xn_ref, ab_ref, cw_ref, alog_ref, dtb_ref, s0_ref,
         o_ref, q_scr, k_scr, v_scr, s_scr) = refs
    else:
        (x_ref, xp_ref, xn_ref, ab_ref, cw_ref, alog_ref, dtb_ref,
         o_ref, sfin_ref, q_scr, k_scr, v_scr, s_scr) = refs
    d = pl.program_id(0)
    j = pl.program_id(2)
    fwd = d == 0
    bi = j + d * (nblk - 1 - 2 * j)
    hk = GDN_HEADS * GDN_DK
    ngrp = GDN_HEADS // GDN_G

    @pl.when(j == 0)
    def _():
        s_scr[...] = jnp.zeros_like(s_scr)
        if has_s0:
            for h in range(GDN_HEADS):
                g, hh = divmod(h, GDN_G)
                s_scr[g, hh * GDN_DK:(hh + 1) * GDN_DK, hh * GDN_DV:(hh + 1) * GDN_DV] = s0_ref[0, 0, 0, h]

    x = x_ref[...].astype(F32)
    rowi = lax.broadcasted_iota(jnp.int32, x.shape, 0)
    prev_row = jnp.where(bi > 0, xp_ref[...].astype(F32)[HALO - 1:HALO], 0.0)
    next_row = jnp.where(bi < nblk - 1, xn_ref[...].astype(F32)[0:1], 0.0)
    x_prev = jnp.where(rowi == 0, prev_row, pltpu.roll(x, 1, 0))
    x_next = jnp.where(rowi == SEQ_BLK - 1, next_row, pltpu.roll(x, SEQ_BLK - 1, 0))
    cw = cw_ref[0]
    y = _silu(x_prev * cw[0:1] + x * cw[1:2] + x_next * cw[2:3])

    lane = lax.broadcasted_iota(jnp.int32, (SEQ_BLK, LANE), 1)
    lo_half = lane < GDN_DK

    def l2n(t):
        parts = []
        for c in range(hk // LANE):
            tc = t[:, c * LANE:(c + 1) * LANE]
            sq = tc * tc
            s_lo = jnp.sum(jnp.where(lo_half, sq, 0.0), axis=-1, keepdims=True)
            s_hi = jnp.sum(jnp.where(lo_half, 0.0, sq), axis=-1, keepdims=True)
            parts.append(tc * lax.rsqrt(jnp.where(lo_half, s_lo, s_hi) + EPS))
        return jnp.concatenate(parts, axis=1)

    q_scr[...] = l2n(y[:, :hk]) * (GDN_DK ** -0.5)
    k_scr[...] = l2n(y[:, hk:2 * hk])
    v_scr[...] = y[:, 2 * hk:]

    _, _, incl64 = _chunk_masks(fwd)
    tri_b = jnp.where(incl64, 1.0, 0.0).astype(BF16)
    row = lax.broadcasted_iota(jnp.int32, (CHUNK, GDN_GW), 0)
    col = lax.broadcasted_iota(jnp.int32, (CHUNK, GDN_GW), 1) % CHUNK
    lag = (row - col) * jnp.where(fwd, 1, -1)
    incl = lag >= 0
    strict = lag > 0
    eye = col == row
    eye_f = jnp.where(eye, 1.0, 0.0)
    blk8 = (row // 8) == (col // 8)
    blk16 = (row // 16) == (col // 16)
    blk32 = (row // 32) == (col // 32)
    brow = lax.broadcasted_iota(jnp.int32, (GDN_GW, GDN_GW), 0) // CHUNK
    bcol = lax.broadcasted_iota(jnp.int32, (GDN_GW, GDN_GW), 1) // CHUNK
    same_head = brow == bcol

    def bdiag(t):
        tb = t.astype(BF16)
        return jnp.where(same_head, jnp.concatenate([tb] * GDN_G, axis=0), jnp.zeros((), BF16))

    def bdmm(a, b):
        return _dot(a.astype(BF16), bdiag(b))

    neg_a = -jnp.exp(alog_ref[0, 0])
    dtb = dtb_ref[0, 0]

    chains = [(c, g) for c in range(NCH) for g in range(ngrp)]
    rows_of = []
    gams, betas = [], []
    for c in range(NCH):
        ci = c + d * (NCH - 1 - 2 * c)
        rows = pl.ds(pl.multiple_of(ci * CHUNK, CHUNK), CHUNK)
        rows_of.append(rows)
        ab = ab_ref[rows, :].astype(F32)
        gams.append(_cumsum_rows(tri_b, neg_a * _softplus(ab + dtb)))
        betas.append(jax.nn.sigmoid(ab))

    ch = []
    for c, g in chains:
        gl = slice(g * GDN_GW, (g + 1) * GDN_GW)
        gam, beta_all = gams[c], betas[c]
        gcol = jnp.concatenate(
            [jnp.broadcast_to(gam[:, g * GDN_G + h:g * GDN_G + h + 1], (CHUNK, GDN_DK))
             for h in range(GDN_G)], axis=1)
        bcol_ = jnp.concatenate(
            [jnp.broadcast_to(beta_all[:, GDN_HEADS + g * GDN_G + h:GDN_HEADS + g * GDN_G + h + 1],
                              (CHUNK, GDN_DK)) for h in range(GDN_G)], axis=1)
        grow = jnp.sum(jnp.where(eye, gcol, 0.0), axis=0, keepdims=True)
        glast = jnp.where(fwd, gcol[CHUNK - 1:CHUNK], gcol[0:1])
        decay = jnp.where(incl, jnp.exp(gcol - grow), 0.0)
        egc = jnp.exp(gcol)
        qg = q_scr[rows_of[c], gl]
        kg = k_scr[rows_of[c], gl]
        vg = v_scr[rows_of[c], gl]
        kq = _dot_nt(jnp.concatenate([kg, qg], axis=0).astype(BF16), bdiag(kg))
        m = jnp.where(strict, bcol_ * kq[:CHUNK] * decay, 0.0)
        ch.append(dict(gl=gl, rows=rows_of[c], g=g, m=m,
                       aqk=(kq[CHUNK:] * decay).astype(BF16),
                       vb=vg * bcol_, kb=kg * (bcol_ * egc),
                       qe=(qg * egc).astype(BF16),
                       kend=(kg * jnp.exp(glast - gcol)).astype(BF16),
                       eg=jnp.exp(glast)))
    for t in ch:
        t["n8"] = jnp.where(blk8, t["m"], 0.0)
        t["n2"] = bdmm(t["n8"], t["n8"])
    for t in ch:
        t["p1"] = eye_f - t["n8"]
        t["n4"] = bdmm(t["n2"], t["n2"])
        t["p1"] = t["p1"] + bdmm(t["p1"], t["n2"])
    for t in ch:
        t["dinv"] = t["p1"] + bdmm(t["p1"], t["n4"])
    for inner, outer in ((blk8, blk16), (blk16, blk32), (blk32, None)):
        off = jnp.logical_not(inner) if outer is None else jnp.logical_and(outer, jnp.logical_not(inner))
        for t in ch:
            t["dl"] = bdmm(t["dinv"], jnp.where(off, t["m"], 0.0))
        for t in ch:
            t["dinv"] = t["dinv"] - bdmm(t["dl"], t["dinv"])
    for t in ch:
        t["u"] = bdmm(t["dinv"], t["vb"])
        t["w"] = bdmm(t["dinv"], t["kb"]).astype(BF16)

    states = [s_scr[g] for g in range(ngrp)]
    for c in range(NCH):
        cur = ch[c * ngrp:(c + 1) * ngrp]
        sbb = [states[t["g"]].astype(BF16) for t in cur]
        ws = [_dot(t["w"], sbb[i]) for i, t in enumerate(cur)]
        os_ = [_dot(t["qe"], sbb[i]) for i, t in enumerate(cur)]
        for i, t in enumerate(cur):
            v_new = t["u"] - ws[i]
            o_ref[0, t["rows"], t["gl"]] = os_[i] + bdmm(t["aqk"], v_new)
            upd = _dot_tn(t["kend"], v_new.astype(BF16))
            states[t["g"]] = t["eg"] * states[t["g"]] + jnp.where(same_head, upd, 0.0)
    for g in range(ngrp):
        s_scr[g] = states[g]

    if not has_s0:
        @pl.when(j == nblk - 1)
        def _():
            for h in range(GDN_HEADS):
                g, hh = divmod(h, GDN_G)
                sfin_ref[0, 0, h] = s_scr[g, hh * GDN_DK:(hh + 1) * GDN_DK, hh * GDN_DV:(hh + 1) * GDN_DV]


def _gdn(proj, conv_w, alog_p, dtb_p, s0, layer, ctx):
    nseq, nblk = (BATCH, SEQ // SEQ_BLK) if ctx else (DEC_BATCH, DEC_SEQ // SEQ_BLK)
    rb = _seq_row_block(ctx, nblk)
    cw = 2 * GDN_HEADS * GDN_DK + GDN_HEADS * GDN_DV
    hv = GDN_HEADS * GDN_DV
    halo = SEQ_BLK // HALO
    last_h = R // HALO - 1
    in_specs = [pl.BlockSpec((SEQ_BLK, cw), lambda d, s, j: (rb(d, s, j), PC_DQKV // cw)),
                pl.BlockSpec((HALO, cw), lambda d, s, j: (jnp.maximum(rb(d, s, j) * halo - 1, 0), PC_DQKV // cw)),
                pl.BlockSpec((HALO, cw), lambda d, s, j: (jnp.minimum((rb(d, s, j) + 1) * halo, last_h), PC_DQKV // cw)),
                pl.BlockSpec((SEQ_BLK, LANE), lambda d, s, j: (rb(d, s, j), PC_DAB // LANE + d)),
                pl.BlockSpec((1, 3, cw), lambda d, s, j: (layer, 0, 0)),
                pl.BlockSpec((1, 1, 1, LANE), lambda d, s, j: (layer, d, 0, 0)),
                pl.BlockSpec((1, 1, 1, LANE), lambda d, s, j: (layer, d, 0, 0))]
    args = [proj, proj, proj, proj, conv_w, alog_p, dtb_p]
    nrows = NCTX if ctx else NSMP
    roff = 0 if ctx else NCTX // SEQ_BLK
    o_spec = pl.BlockSpec((1, SEQ_BLK, hv), lambda d, s, j: (d, rb(d, s, j) - roff, 0))
    o_shape = jax.ShapeDtypeStruct((2, nrows, hv), F32)
    st_blk = (GDN_HEADS, GDN_DK, GDN_DV)
    if ctx:
        out_specs = [o_spec, pl.BlockSpec((1, 1) + st_blk, lambda d, s, j: (s, d, 0, 0, 0))]
        out_shape = [o_shape, jax.ShapeDtypeStruct((BATCH, 2) + st_blk, F32)]
    else:
        in_specs.append(pl.BlockSpec((1, 1, 1) + st_blk, lambda d, s, j: (s, layer, d, 0, 0, 0)))
        args.append(s0)
        out_specs, out_shape = o_spec, o_shape
    return pl.pallas_call(
        functools.partial(_gdn_kernel, nblk=nblk, has_s0=not ctx),
        grid=(2, nseq, nblk),
        in_specs=in_specs, out_specs=out_specs, out_shape=out_shape,
        scratch_shapes=[pltpu.VMEM((SEQ_BLK, GDN_HEADS * GDN_DK), F32),
                        pltpu.VMEM((SEQ_BLK, GDN_HEADS * GDN_DK), F32),
                        pltpu.VMEM((SEQ_BLK, hv), F32),
                        pltpu.VMEM((GDN_HEADS // GDN_G, GDN_GW, GDN_GW), F32)],
        compiler_params=_cp(("arbitrary", "arbitrary", "arbitrary")),
        name="gdn_ctx" if ctx else "gdn_smp",
    )(*args)


def _group_rms(x, width):
    parts = []
    lane = lax.broadcasted_iota(jnp.int32, (x.shape[0], LANE), 1)
    lo_half = lane < 64
    for c in range(x.shape[1] // LANE):
        xc = x[:, c * LANE:(c + 1) * LANE]
        sq = xc * xc
        if width == LANE:
            ms = jnp.mean(sq, axis=-1, keepdims=True)
        else:
            s_lo = jnp.sum(jnp.where(lo_half, sq, 0.0), axis=-1, keepdims=True)
            s_hi = jnp.sum(jnp.where(lo_half, 0.0, sq), axis=-1, keepdims=True)
            ms = jnp.where(lo_half, s_lo, s_hi) * (1.0 / width)
        parts.append(xc * lax.rsqrt(ms + EPS))
    return jnp.concatenate(parts, axis=1)


def _merge_kernel(ym_ref, og_ref, gr_ref, od_ref, dz_ref, gt_ref, h_ref, ga_ref,
                  gn_ref, dn_ref, wb_ref, wo_ref, o_ref):
    y_gla = _group_rms(og_ref[0] + og_ref[1], GLA_DV) * gn_ref[0] * _silu(gr_ref[...].astype(F32))
    y_gdn = _group_rms(od_ref[0] + od_ref[1], GDN_DV) * dn_ref[0] * _silu(dz_ref[...].astype(F32))
    gates = jax.nn.sigmoid(gt_ref[...].astype(F32))
    d = D_MODEL
    acc = gates[:, :d] * _dot(ym_ref[...], wb_ref[0, 0])
    acc = acc + gates[:, d:2 * d] * _dot(y_gla.astype(BF16), wb_ref[0, 1])
    acc = acc + gates[:, 2 * d:] * _dot(y_gdn.astype(BF16), wb_ref[0, 2])
    y = _dot(acc.astype(BF16), wo_ref[0])
    o_ref[...] = h_ref[...] + ga_ref[0] * y


def _merge(y_mla, o_gla, o_gdn, proj, h, mod3, gla_norm_p, gdn_norm_p, w_branch_b, w_out_b, layer):
    tm = TM_MERGE
    bw = 512
    return pl.pallas_call(
        _merge_kernel,
        grid=(R // tm,),
        in_specs=[pl.BlockSpec((tm, bw), lambda i: (i, 0)),
                  pl.BlockSpec((2, tm, bw), lambda i: (0, i, 0)),
                  pl.BlockSpec((tm, bw), lambda i: (i, PC_GR // bw)),
                  pl.BlockSpec((2, tm, bw), lambda i: (0, i, 0)),
                  pl.BlockSpec((tm, bw), lambda i: (i, PC_DZ // bw)),
                  pl.BlockSpec((tm, 3 * D_MODEL), lambda i: (i, PC_GATES // (3 * D_MODEL))),
                  pl.BlockSpec((tm, D_MODEL), lambda i: (i, 0)),
                  _mod_spec(layer, 2, tm),
                  pl.BlockSpec((1, 1, bw), lambda i: (layer, 0, 0)),
                  pl.BlockSpec((1, 1, bw), lambda i: (layer, 0, 0)),
                  pl.BlockSpec((1, 3, bw, D_MODEL), lambda i: (layer, 0, 0, 0)),
                  pl.BlockSpec((1, D_MODEL, D_MODEL), lambda i: (layer, 0, 0))],
        out_specs=pl.BlockSpec((tm, D_MODEL), lambda i: (i, 0)),
        out_shape=jax.ShapeDtypeStruct((R, D_MODEL), F32),
        compiler_params=_cp(("arbitrary",)),
        name="merge",
    )(y_mla, o_gla, proj, o_gdn, proj, proj, h, mod3, gla_norm_p, gdn_norm_p, w_branch_b, w_out_b)


def _ffn_kernel(h_ref, g_ref, sc_ref, sh_ref, gf_ref, wg_ref, wu_ref, wo_ref, fn_ref, o_ref,
                xf_ref, acc_ref, *, final):
    f = pl.program_id(1)

    @pl.when(f == 0)
    def _():
        y = _rms(h_ref[...], g_ref[0])
        xf_ref[...] = (y * (1.0 + sc_ref[0]) + sh_ref[0]).astype(BF16)
        acc_ref[...] = jnp.zeros_like(acc_ref)

    xf = xf_ref[...]
    a = _silu(_dot(xf, wg_ref[0])) * _dot(xf, wu_ref[0])
    acc_ref[...] += _dot(a.astype(BF16), wo_ref[0])

    @pl.when(f == pl.num_programs(1) - 1)
    def _():
        out = h_ref[...] + gf_ref[0] * acc_ref[...]
        if final:
            out = _rms(out, fn_ref[...])
        o_ref[...] = out


def _ffn(h, mod3, norm_ffn, wg_b, wu_b, wo_b, final_norm, layer, final):
    tm, tf = TM_FFN, TF_FFN
    return pl.pallas_call(
        functools.partial(_ffn_kernel, final=final),
        grid=(R // tm, D_FF // tf),
        in_specs=[pl.BlockSpec((tm, D_MODEL), lambda i, f: (i, 0)),
                  pl.BlockSpec((1, 1, D_MODEL), lambda i, f: (layer, 0, 0)),
                  _mod_spec(layer, 4, tm),
                  _mod_spec(layer, 3, tm),
                  _mod_spec(layer, 5, tm),
                  pl.BlockSpec((1, D_MODEL, tf), lambda i, f: (layer, 0, f)),
                  pl.BlockSpec((1, D_MODEL, tf), lambda i, f: (layer, 0, f)),
                  pl.BlockSpec((1, tf, D_MODEL), lambda i, f: (layer, f, 0)),
                  pl.BlockSpec((1, D_MODEL), lambda i, f: (0, 0))],
        out_specs=pl.BlockSpec((tm, D_MODEL), lambda i, f: (i, 0)),
        out_shape=jax.ShapeDtypeStruct((R, D_MODEL), F32),
        scratch_shapes=[pltpu.VMEM((tm, D_MODEL), BF16), pltpu.VMEM((tm, D_MODEL), F32)],
        compiler_params=_cp(("arbitrary", "arbitrary")),
        name="ffn_final" if final else "ffn",
    )(h, norm_ffn.reshape(DEPTH, 1, D_MODEL), mod3, mod3, mod3, wg_b, wu_b, wo_b,
      final_norm.reshape(1, D_MODEL))


def _rope_partner(w):
    h = AXIS_DIM // 2
    parts = []
    for a in range(2):
        x1 = w[..., a * AXIS_DIM:a * AXIS_DIM + h]
        x2 = w[..., a * AXIS_DIM + h:(a + 1) * AXIS_DIM]
        parts += [-x2, x1]
    return jnp.concatenate(parts, axis=-1)


def _pack_w_in(w_in):
    c = [0]
    for s in (Q_LORA, KV_LORA + ROPE_DIM, 256, 256, 512, 512, 2 * GLA_RANK, 1536, 512,
              2 * GDN_HEADS, 2 * GDN_HEADS, 3 * D_MODEL):
        c.append(c[-1] + s)
    seg = [w_in[..., c[i]:c[i + 1]] for i in range(12)]
    mq, mkv, gq, gk, gv, gr, glr, dqkv, dz, da, db, gates = seg

    def zeros(n):
        return jnp.zeros(w_in.shape[:-1] + (n,), w_in.dtype)

    kr = mkv[..., KV_LORA:]
    mla = jnp.concatenate([mq, mkv[..., :KV_LORA], kr, _rope_partner(kr), zeros(64)], axis=-1)
    dab = []
    for d in range(2):
        dab += [da[..., d * GDN_HEADS:(d + 1) * GDN_HEADS], db[..., d * GDN_HEADS:(d + 1) * GDN_HEADS],
                zeros(LANE - 2 * GDN_HEADS)]
    packed = jnp.concatenate([gates, dqkv, dz, gv, gr, mla, gq, gk, glr, zeros(LANE - 2 * GLA_RANK)]
                             + dab + [zeros(PROJ_W - PC_DAB - 2 * LANE)], axis=-1)
    return packed.astype(BF16)


def _pack_mla_weights(w_uq, w_ukv):
    l = w_uq.shape[0]
    qh = w_uq.reshape(l, Q_LORA, MLA_HEADS, MLA_NOPE + ROPE_DIM)
    zq = jnp.zeros((l, Q_LORA, MLA_HEADS, HEAD_W - MLA_NOPE - ROPE_DIM), w_uq.dtype)
    zn = jnp.zeros((l, Q_LORA, MLA_HEADS, MLA_NOPE), w_uq.dtype)
    wq = jnp.concatenate([qh, zq], axis=-1).reshape(l, Q_LORA, MLA_HEADS * HEAD_W)
    wq_sw = jnp.concatenate([zn, _rope_partner(qh[..., MLA_NOPE:]), zq], axis=-1).reshape(l, Q_LORA, MLA_HEADS * HEAD_W)
    wq_p = jnp.concatenate([wq, wq_sw], axis=-1).astype(BF16)

    kvh = w_ukv.reshape(l, KV_LORA, MLA_HEADS, MLA_NOPE + MLA_V)
    zk = jnp.zeros((l, KV_LORA, MLA_HEADS, HEAD_W - MLA_NOPE), w_ukv.dtype)
    wk_top = jnp.concatenate([kvh[..., :MLA_NOPE], zk], axis=-1).reshape(l, KV_LORA, MLA_HEADS * HEAD_W)
    place = jnp.zeros((KV_LORA, MLA_HEADS, HEAD_W), w_ukv.dtype)
    idx = jnp.arange(ROPE_DIM)
    for rep in range(2):
        place = place.at[rep * ROPE_DIM + idx, :, MLA_NOPE + idx].set(1.0)
    wk_bot = jnp.broadcast_to(place.reshape(1, KV_LORA, MLA_HEADS * HEAD_W), (l, KV_LORA, MLA_HEADS * HEAD_W))
    wk_p = jnp.concatenate([wk_top, wk_bot], axis=1).astype(BF16)
    zv = jnp.zeros((l, KV_LORA, MLA_HEADS, HEAD_W - MLA_V), w_ukv.dtype)
    wv_p = jnp.concatenate([kvh[..., MLA_NOPE:], zv], axis=-1).reshape(l, KV_LORA, MLA_HEADS * HEAD_W).astype(BF16)
    return wq_p, wk_p, wv_p


def _rope_tables(tm):
    rows = DEC_SEQ // GRID_W
    row = jnp.repeat(jnp.arange(rows, dtype=F32), GRID_W)
    col = jnp.tile(jnp.arange(GRID_W, dtype=F32), rows)
    inv = ROPE_THETA ** (-jnp.arange(0, AXIS_DIM, 2, dtype=F32) / AXIS_DIM)
    ang_r, ang_c = row[:, None] * inv, col[:, None] * inv
    cos32 = jnp.concatenate([jnp.cos(ang_r)] * 2 + [jnp.cos(ang_c)] * 2, axis=-1)
    sin32 = jnp.concatenate([jnp.sin(ang_r)] * 2 + [jnp.sin(ang_c)] * 2, axis=-1)
    cos32 = jnp.concatenate([jnp.ones((tm, ROPE_DIM), F32), cos32], axis=0)
    sin32 = jnp.concatenate([jnp.zeros((tm, ROPE_DIM), F32), sin32], axis=0)
    n = cos32.shape[0]
    scale = (MLA_NOPE + ROPE_DIM) ** -0.5
    pad = HEAD_W - MLA_NOPE - ROPE_DIM
    cq = jnp.concatenate([jnp.ones((n, MLA_NOPE), F32), cos32, jnp.zeros((n, pad), F32)], axis=-1) * scale
    sq = jnp.concatenate([jnp.zeros((n, MLA_NOPE), F32), sin32, jnp.zeros((n, pad), F32)], axis=-1) * scale
    ck = jnp.concatenate([cos32, sin32, jnp.zeros((n, HEAD_W - 2 * ROPE_DIM), F32)], axis=-1)
    return jnp.concatenate([cq, sq, ck], axis=-1)


def _pad_lanes(x, n):
    return jnp.pad(x, [(0, 0)] * (x.ndim - 1) + [(0, n - x.shape[-1])])


def kernel(x_prompt, x_sample, cache_mla, state_gla, state_gdn, c, c_ctx, w_mod, b_mod, norm_mix, w_in,
           mla_q_norm, mla_w_uq, mla_kv_norm, mla_w_ukv, gla_w_gate, gla_b_gate, gla_norm, gdn_conv,
           gdn_a_log, gdn_dt_bias, gdn_norm, w_branch, w_out, norm_ffn, ffn_w_in, ffn_w_out, final_norm):
    w_in_p = _pack_w_in(w_in)
    wq_p, wk_p, wv_p = _pack_mla_weights(mla_w_uq, mla_w_ukv)
    tab = _rope_tables(TM_PREP)
    wg_p = jnp.zeros((DEPTH, 2, LANE, GLA_HEADS * GLA_DK), F32)
    for d in range(2):
        wg_p = wg_p.at[:, d, d * GLA_RANK:(d + 1) * GLA_RANK, :].set(gla_w_gate[:, d])
    wg_p = wg_p.astype(BF16)
    bg_p = gla_b_gate.reshape(DEPTH, 2, 1, GLA_HEADS * GLA_DK)
    alog_p = _pad_lanes(gdn_a_log, LANE).reshape(DEPTH, 2, 1, LANE)
    dtb_p = _pad_lanes(gdn_dt_bias, LANE).reshape(DEPTH, 2, 1, LANE)
    gla_norm_p = jnp.tile(gla_norm, (1, GLA_HEADS)).reshape(DEPTH, 1, GLA_HEADS * GLA_DV)
    gdn_norm_p = jnp.tile(gdn_norm, (1, GDN_HEADS)).reshape(DEPTH, 1, GDN_HEADS * GDN_DV)
    w_branch_b = w_branch.astype(BF16)
    w_out_b = w_out.astype(BF16)
    wg_b = ffn_w_in[..., :D_FF].astype(BF16)
    wu_b = ffn_w_in[..., D_FF:].astype(BF16)
    wo_b = ffn_w_out.astype(BF16)
    cache_p = _pad_lanes(cache_mla, 2 * KV_LORA)
    cond8 = jnp.concatenate([c_ctx[None, :], c, jnp.zeros((MOD_ROWS - 1 - DEC_BATCH, D_MODEL), F32)], axis=0)

    mod = _modulation(cond8, w_mod, b_mod)
    mod3 = mod.reshape(DEPTH * MOD_ROWS * 6, 1, D_MODEL)
    kc, vc = _kv_cache(cache_p, wk_p, wv_p)

    h = jnp.concatenate([x_prompt.reshape(NCTX, D_MODEL), x_sample.reshape(NSMP, D_MODEL)], axis=0)
    kv_list, gla_list, gdn_list = [], [], []
    for l in range(DEPTH):
        proj = _inproj(h, mod3, norm_mix, w_in_p, l)
        qp, kp, vp, own = _mla_prep(proj, tab, mla_q_norm, mla_kv_norm, wq_p, wk_p, wv_p, l)
        y_mla = jnp.concatenate([_attention_ctx(qp, kp, vp), _attention_smp(qp, kp, vp, kc, vc, l)], axis=0)
        og_c, sg = _gla(proj, wg_p, bg_p, None, l, True)
        og_s = _gla(proj, wg_p, bg_p, state_gla, l, False)
        od_c, sd = _gdn(proj, gdn_conv, alog_p, dtb_p, None, l, True)
        od_s = _gdn(proj, gdn_conv, alog_p, dtb_p, state_gdn, l, False)
        o_gla = jnp.concatenate([og_c, og_s], axis=1)
        o_gdn = jnp.concatenate([od_c, od_s], axis=1)
        h = _merge(y_mla, o_gla, o_gdn, proj, h, mod3, gla_norm_p, gdn_norm_p, w_branch_b, w_out_b, l)
        h = _ffn(h, mod3, norm_ffn, wg_b, wu_b, wo_b, final_norm, l, l == DEPTH - 1)
        kv_list.append(own[:NCTX, :KV_LORA + ROPE_DIM].reshape(BATCH, SEQ, KV_LORA + ROPE_DIM))
        gla_list.append(sg)
        gdn_list.append(sd)

    y_prompt = h[:NCTX].reshape(BATCH, SEQ, D_MODEL)
    y_sample = h[NCTX:].reshape(DEC_BATCH, DEC_SEQ, D_MODEL)
    return (y_prompt, y_sample, jnp.stack(kv_list, axis=1), jnp.stack(gla_list, axis=1),
            jnp.stack(gdn_list, axis=1))
```

```python
import functools

import jax
import jax.numpy as jnp
from jax import lax
from jax.experimental import pallas as pl
from jax.experimental.pallas import tpu as pltpu

F32 = jnp.float32
BF16 = jnp.bfloat16

D_MODEL = 1024
BATCH = 16
SEQ = 256
DEPTH = 2
DEC_BATCH = 2
DEC_SEQ = 4096
PAST_LEN = 256
GRID_W = 64
CHUNK = 64
EPS = 1e-6
MLA_HEADS = 8
MLA_NOPE = 64
ROPE_DIM = 32
AXIS_DIM = ROPE_DIM // 2
MLA_V = 64
Q_LORA = 256
KV_LORA = 128
ROPE_THETA = 10000.0
GLA_HEADS = 4
GLA_DK = 64
GLA_DV = 128
GLA_RANK = 16
GLA_TAU = 16.0
GDN_HEADS = 8
GDN_DK = 64
GDN_DV = 64
D_FF = ((8 * D_MODEL + 3 * 256 - 1) // (3 * 256)) * 256
MOD_W = 6 * D_MODEL

NCTX = BATCH * SEQ
NSMP = DEC_BATCH * DEC_SEQ
R = NCTX + NSMP
LANE = 128
HEAD_W = 128
VT_ROWS = MLA_HEADS * MLA_V
SEQ_BLK = 256
NCH = SEQ_BLK // CHUNK
MOD_ROWS = 8

PC_GATES = 0
PC_DQKV = 3072
PC_DZ = 4608
PC_GV = 5120
PC_GR = 5632
PC_MLA = 6144
PC_GQ = 6656
PC_GK = 6912
PC_GLR = 7168
PC_DAB = 7296
PROJ_W = 7680

TM_IN = 512
TN_IN = 1536
TM_PREP = 512
TM_MERGE = 512
TM_FFN = 1024
TF_FFN = 256
TQ_ATT = 256
TK_ATT = 512
ATT_AHEAD = 8
VMEM_LIMIT = 48 * 1024 * 1024


def _cp(sem):
    return pltpu.CompilerParams(dimension_semantics=sem, vmem_limit_bytes=VMEM_LIMIT)


def _dot(a, b):
    return jnp.dot(a, b, preferred_element_type=F32)


def _dot_nt(a, b):
    return lax.dot_general(a, b, (((1,), (1,)), ((), ())), preferred_element_type=F32)


def _dot_tn(a, b):
    return lax.dot_general(a, b, (((0,), (0,)), ((), ())), preferred_element_type=F32)


def _split2(x):
    hi = x.astype(BF16)
    lo = (x - hi.astype(F32)).astype(BF16)
    return hi, lo


def _split3(x):
    hi = x.astype(BF16)
    r1 = x - hi.astype(F32)
    mid = r1.astype(BF16)
    lo = (r1 - mid.astype(F32)).astype(BF16)
    return hi, mid, lo


def _cumsum_rows(tri_b, x):
    hi, mid, lo = _split3(x)
    return _dot(tri_b, hi) + (_dot(tri_b, mid) + _dot(tri_b, lo))


def _silu(x):
    return x * jax.nn.sigmoid(x)


def _softplus(x):
    return jnp.maximum(x, 0.0) + jnp.log(1.0 + jnp.exp(-jnp.abs(x)))


def _log_sigmoid(x):
    return jnp.minimum(x, 0.0) - jnp.log(1.0 + jnp.exp(-jnp.abs(x)))


def _rms(x, g):
    return x * lax.rsqrt(jnp.mean(x * x, axis=-1, keepdims=True) + EPS) * g


def _mod_row(row_start):
    return jnp.where(row_start < NCTX, 0, 1 + (row_start - NCTX) // DEC_SEQ)


def _mod_spec(layer, which, tm, axis=0):
    def imap(*idx):
        return ((layer * MOD_ROWS + _mod_row(idx[axis] * tm)) * 6 + which, 0, 0)
    return pl.BlockSpec((1, 1, D_MODEL), imap)


def _mod_kernel(c_ref, w_ref, b_ref, o_ref):
    x = _silu(c_ref[...])
    xh, xl = _split2(x)
    wh, wl = _split2(w_ref[0])
    o_ref[0] = _dot(xh, wh) + (_dot(xl, wh) + _dot(xh, wl)) + b_ref[0]


def _modulation(cond8, w_mod, b_mod):
    tn = 1536
    return pl.pallas_call(
        _mod_kernel,
        grid=(DEPTH, MOD_W // tn),
        in_specs=[pl.BlockSpec((MOD_ROWS, D_MODEL), lambda l, j: (0, 0)),
                  pl.BlockSpec((1, D_MODEL, tn), lambda l, j: (l, 0, j)),
                  pl.BlockSpec((1, 1, tn), lambda l, j: (l, 0, j))],
        out_specs=pl.BlockSpec((1, MOD_ROWS, tn), lambda l, j: (l, 0, j)),
        out_shape=jax.ShapeDtypeStruct((DEPTH, MOD_ROWS, MOD_W), F32),
        compiler_params=_cp(("arbitrary", "arbitrary")),
        name="modulation",
    )(cond8, w_mod, b_mod.reshape(DEPTH, 1, MOD_W))


def _inproj_kernel(h_ref, g_ref, sc_ref, sh_ref, w_ref, o_ref):
    y = _rms(h_ref[...], g_ref[0])
    xn = (y * (1.0 + sc_ref[0]) + sh_ref[0]).astype(BF16)
    o_ref[...] = _dot(xn, w_ref[0]).astype(BF16)


def _inproj(h, mod3, norm_mix, w_in_p, layer):
    return pl.pallas_call(
        _inproj_kernel,
        grid=(PROJ_W // TN_IN, R // TM_IN),
        in_specs=[pl.BlockSpec((TM_IN, D_MODEL), lambda j, i: (i, 0)),
                  pl.BlockSpec((1, 1, D_MODEL), lambda j, i: (layer, 0, 0)),
                  _mod_spec(layer, 1, TM_IN, axis=1),
                  _mod_spec(layer, 0, TM_IN, axis=1),
                  pl.BlockSpec((1, D_MODEL, TN_IN), lambda j, i: (layer, 0, j))],
        out_specs=pl.BlockSpec((TM_IN, TN_IN), lambda j, i: (i, j)),
        out_shape=jax.ShapeDtypeStruct((R, PROJ_W), BF16),
        compiler_params=_cp(("arbitrary", "arbitrary")),
        name="inproj",
    )(h, norm_mix.reshape(DEPTH, 1, D_MODEL), mod3, mod3, w_in_p)


def _mla_prep_kernel(pm_ref, tab_ref, qn_ref, kvn_ref, wq_ref, wk_ref, wv_ref,
                     q_ref, k_ref, v_ref, own_ref):
    pm = pm_ref[...].astype(F32)
    tab = tab_ref[...]
    qn = _rms(pm[:, :Q_LORA], qn_ref[0]).astype(BF16)
    q2 = _dot(qn, wq_ref[0])
    hw = MLA_HEADS * HEAD_W
    cq = jnp.tile(tab[:, :HEAD_W], (1, MLA_HEADS))
    sq = jnp.tile(tab[:, HEAD_W:2 * HEAD_W], (1, MLA_HEADS))
    q_ref[...] = (q2[:, :hw] * cq + q2[:, hw:] * sq).astype(BF16)
    ckv = _rms(pm[:, Q_LORA:Q_LORA + KV_LORA], kvn_ref[0])
    kr = pm[:, Q_LORA + KV_LORA:] * tab[:, 2 * HEAD_W:]
    lhs = jnp.concatenate([ckv, kr], axis=1)
    own_ref[...] = lhs
    lb = lhs.astype(BF16)
    k_ref[...] = _dot(lb, wk_ref[0]).astype(BF16)
    v_ref[...] = _dot_nt(wv_ref[0], lb[:, :KV_LORA]).astype(BF16)


def _mla_prep(proj, tab, q_norm, kv_norm, wq_p, wk_p, wv_p, layer):
    tm = TM_PREP
    hw = MLA_HEADS * HEAD_W

    def tab_map(i):
        r0 = i * tm
        return (jnp.where(r0 < NCTX, 0, 1 + ((r0 - NCTX) % DEC_SEQ) // tm), 0)

    return pl.pallas_call(
        _mla_prep_kernel,
        grid=(R // tm,),
        in_specs=[pl.BlockSpec((tm, 512), lambda i: (i, PC_MLA // 512)),
                  pl.BlockSpec((tm, 3 * HEAD_W), tab_map),
                  pl.BlockSpec((1, 1, Q_LORA), lambda i: (layer, 0, 0)),
                  pl.BlockSpec((1, 1, KV_LORA), lambda i: (layer, 0, 0)),
                  pl.BlockSpec((1, Q_LORA, 2 * hw), lambda i: (layer, 0, 0)),
                  pl.BlockSpec((1, 2 * KV_LORA, hw), lambda i: (layer, 0, 0)),
                  pl.BlockSpec((1, VT_ROWS, KV_LORA), lambda i: (layer, 0, 0))],
        out_specs=[pl.BlockSpec((tm, hw), lambda i: (i, 0)),
                   pl.BlockSpec((tm, hw), lambda i: (i, 0)),
                   pl.BlockSpec((VT_ROWS, tm), lambda i: (0, i)),
                   pl.BlockSpec((tm, 2 * KV_LORA), lambda i: (i, 0))],
        out_shape=[jax.ShapeDtypeStruct((R, hw), BF16),
                   jax.ShapeDtypeStruct((R, hw), BF16),
                   jax.ShapeDtypeStruct((VT_ROWS, R), BF16),
                   jax.ShapeDtypeStruct((R, 2 * KV_LORA), F32)],
        compiler_params=_cp(("arbitrary",)),
        name="mla_prep",
    )(proj, tab, q_norm.reshape(DEPTH, 1, Q_LORA), kv_norm.reshape(DEPTH, 1, KV_LORA),
      wq_p, wk_p, wv_p)


def _kv_cache_kernel(c_ref, wk_ref, wv_ref, k_ref, v_ref):
    lb = c_ref[0, 0].astype(BF16)
    k_ref[0, 0] = _dot(lb, wk_ref[0]).astype(BF16)
    v_ref[0, 0] = _dot_nt(wv_ref[0], lb[:, :KV_LORA]).astype(BF16)


def _kv_cache(cache_p, wk_p, wv_p):
    hw = MLA_HEADS * HEAD_W
    return pl.pallas_call(
        _kv_cache_kernel,
        grid=(DEC_BATCH, DEPTH),
        in_specs=[pl.BlockSpec((1, 1, PAST_LEN, 2 * KV_LORA), lambda b, l: (b, l, 0, 0)),
                  pl.BlockSpec((1, 2 * KV_LORA, hw), lambda b, l: (l, 0, 0)),
                  pl.BlockSpec((1, VT_ROWS, KV_LORA), lambda b, l: (l, 0, 0))],
        out_specs=[pl.BlockSpec((1, 1, PAST_LEN, hw), lambda b, l: (b, l, 0, 0)),
                   pl.BlockSpec((1, 1, VT_ROWS, PAST_LEN), lambda b, l: (b, l, 0, 0))],
        out_shape=[jax.ShapeDtypeStruct((DEC_BATCH, DEPTH, PAST_LEN, hw), BF16),
                   jax.ShapeDtypeStruct((DEC_BATCH, DEPTH, VT_ROWS, PAST_LEN), BF16)],
        compiler_params=_cp(("arbitrary", "arbitrary")),
        name="kv_cache",
    )(cache_p, wk_p, wv_p)


def _attn_kernel(*refs, heads, has_cache):
    if has_cache:
        q_ref, k_ref, vt_ref, kc_ref, vct_ref, o_ref = refs
    else:
        q_ref, k_ref, vt_ref, o_ref = refs
    n_keys = k_ref.shape[0]
    tk = min(TK_ATT, n_keys)
    tiles = ([("cache", 0, PAST_LEN)] if has_cache else []) + [("own", s0, tk) for s0 in range(0, n_keys, tk)]
    units = [(h, t) for t in range(len(tiles)) for h in range(heads)]
    qs = [q_ref[:, h * HEAD_W:(h + 1) * HEAD_W] for h in range(heads)]

    def scores(h, t):
        kind, s0, n = tiles[t]
        sl = slice(h * HEAD_W, (h + 1) * HEAD_W)
        keys = kc_ref[0, 0, :, sl] if kind == "cache" else k_ref[s0:s0 + n, sl]
        return _dot_nt(keys, qs[h])

    def values_t(h, t):
        kind, s0, n = tiles[t]
        sv = slice(h * MLA_V, (h + 1) * MLA_V)
        return vct_ref[0, 0, sv, :] if kind == "cache" else vt_ref[sv, s0:s0 + n]

    ahead = ATT_AHEAD
    pending = {u: scores(*u) for u in units[:ahead]}
    m, l, acc = [None] * heads, [None] * heads, [None] * heads
    for i, (h, t) in enumerate(units):
        if i + ahead < len(units):
            pending[units[i + ahead]] = scores(*units[i + ahead])
        st = pending.pop((h, t))
        mt = jnp.max(st, axis=0, keepdims=True)
        if m[h] is None:
            m[h] = mt
            p = jnp.exp(st - mt)
            l[h] = jnp.sum(p, axis=0, keepdims=True)
            acc[h] = _dot(values_t(h, t), p.astype(BF16))
        else:
            m_new = jnp.maximum(m[h], mt)
            alpha = jnp.exp(m[h] - m_new)
            p = jnp.exp(st - m_new)
            l[h] = alpha * l[h] + jnp.sum(p, axis=0, keepdims=True)
            acc[h] = alpha * acc[h] + _dot(values_t(h, t), p.astype(BF16))
            m[h] = m_new
    outs = [acc[h] / l[h] for h in range(heads)]
    o_ref[...] = jnp.concatenate(outs, axis=0).T.astype(BF16)


def _attention_ctx(qp, kp, vtp):
    hw = MLA_HEADS * HEAD_W
    return pl.pallas_call(
        functools.partial(_attn_kernel, heads=MLA_HEADS, has_cache=False),
        grid=(BATCH,),
        in_specs=[pl.BlockSpec((SEQ, hw), lambda s: (s, 0)),
                  pl.BlockSpec((SEQ, hw), lambda s: (s, 0)),
                  pl.BlockSpec((VT_ROWS, SEQ), lambda s: (0, s))],
        out_specs=pl.BlockSpec((SEQ, MLA_HEADS * MLA_V), lambda s: (s, 0)),
        out_shape=jax.ShapeDtypeStruct((NCTX, MLA_HEADS * MLA_V), BF16),
        compiler_params=_cp(("arbitrary",)),
        name="attn_ctx",
    )(qp, kp, vtp)


def _attention_smp(qp, kp, vtp, kc, vct, layer):
    hpb = 2
    wq = hpb * HEAD_W
    wv = hpb * MLA_V
    nq = DEC_SEQ // TQ_ATT
    qoff = NCTX // TQ_ATT
    koff = NCTX // DEC_SEQ
    return pl.pallas_call(
        functools.partial(_attn_kernel, heads=hpb, has_cache=True),
        grid=(DEC_BATCH, MLA_HEADS // hpb, nq),
        in_specs=[pl.BlockSpec((TQ_ATT, wq), lambda b, g, i: (qoff + b * nq + i, g)),
                  pl.BlockSpec((DEC_SEQ, wq), lambda b, g, i: (koff + b, g)),
                  pl.BlockSpec((wv, DEC_SEQ), lambda b, g, i: (g, koff + b)),
                  pl.BlockSpec((1, 1, PAST_LEN, wq), lambda b, g, i: (b, layer, 0, g)),
                  pl.BlockSpec((1, 1, wv, PAST_LEN), lambda b, g, i: (b, layer, g, 0))],
        out_specs=pl.BlockSpec((TQ_ATT, hpb * MLA_V), lambda b, g, i: (b * nq + i, g)),
        out_shape=jax.ShapeDtypeStruct((NSMP, MLA_HEADS * MLA_V), BF16),
        compiler_params=_cp(("arbitrary", "arbitrary", "arbitrary")),
        name="attn_smp",
    )(qp, kp, vtp, kc, vct)


def _seq_row_block(ctx, nblk):
    if ctx:
        return lambda d, s, j: s
    off = NCTX // SEQ_BLK
    return lambda d, s, j: off + s * nblk + j + d * (nblk - 1 - 2 * j)


def _chunk_masks(fwd):
    row = lax.broadcasted_iota(jnp.int32, (CHUNK, CHUNK), 0)
    col = lax.broadcasted_iota(jnp.int32, (CHUNK, CHUNK), 1)
    incl = (row - col) * jnp.where(fwd, 1, -1) >= 0
    return row, col, incl


def _gla_kernel(*refs, nblk, has_s0):
    if has_s0:
        q_ref, k_ref, v_ref, glr_ref, wg_ref, bg_ref, s0_ref, o_ref, s_scr = refs
    else:
        q_ref, k_ref, v_ref, glr_ref, wg_ref, bg_ref, o_ref, sfin_ref, s_scr = refs
    d = pl.program_id(0)
    j = pl.program_id(2)
    fwd = d == 0

    @pl.when(j == 0)
    def _():
        if has_s0:
            s_scr[...] = s0_ref[0, 0, 0]
        else:
            s_scr[...] = jnp.zeros_like(s_scr)

    _, _, incl = _chunk_masks(fwd)
    tri_b = jnp.where(incl, 1.0, 0.0).astype(BF16)
    wg = wg_ref[0, 0]
    bg = bg_ref[0, 0]

    ch = []
    for c in range(NCH):
        ci = c + d * (NCH - 1 - 2 * c)
        rows = pl.ds(pl.multiple_of(ci * CHUNK, CHUNK), CHUNK)
        ch.append(dict(rows=rows, z=_dot(glr_ref[rows, :], wg) + bg))
    for t in ch:
        t["la"] = _split3(_log_sigmoid(t["z"]) * (1.0 / GLA_TAU))
    for t in ch:
        hi, mid, lo = t["la"]
        t["bc"] = _dot(tri_b, hi) + (_dot(tri_b, mid) + _dot(tri_b, lo))
    for t in ch:
        rows, bc = t["rows"], t["bc"]
        q = q_ref[rows, :].astype(F32) * (GLA_DK ** -0.5)
        k = k_ref[rows, :].astype(F32)
        bl = jnp.where(fwd, bc[CHUNK - 1:CHUNK], bc[0:1])
        t["qd"] = (q * jnp.exp(bc)).astype(BF16)
        t["ki"] = (k * jnp.exp(-bc)).astype(BF16)
        t["ke"] = (k * jnp.exp(bl - bc)).astype(BF16)
        t["dec_t"] = jnp.transpose(jnp.broadcast_to(jnp.exp(bl), (LANE, GLA_HEADS * GLA_DK)))
        t["v"] = v_ref[rows, :]
    heads = [(slice(h * GLA_DK, (h + 1) * GLA_DK), slice(h * GLA_DV, (h + 1) * GLA_DV))
             for h in range(GLA_HEADS)]
    for t in ch:
        t["a"] = [jnp.where(incl, _dot_nt(t["qd"][:, sk], t["ki"][:, sk]), 0.0).astype(BF16)
                  for sk, _ in heads]
    for t in ch:
        t["oi"] = [_dot(t["a"][h], t["v"][:, sv]) for h, (_, sv) in enumerate(heads)]
        t["upd"] = [_dot_tn(t["ke"][:, sk], t["v"][:, sv]) for sk, sv in heads]
    states = [s_scr[h] for h in range(GLA_HEADS)]
    for t in ch:
        for h, (sk, sv) in enumerate(heads):
            o_ref[0, t["rows"], sv] = t["oi"][h] + _dot(t["qd"][:, sk], states[h].astype(BF16))
            states[h] = t["dec_t"][sk, :] * states[h] + t["upd"][h]
    for h in range(GLA_HEADS):
        s_scr[h] = states[h]

    if not has_s0:
        @pl.when(j == nblk - 1)
        def _():
            sfin_ref[0, 0] = s_scr[...]


def _gla(proj, wg_p, bg_p, s0, layer, ctx):
    nseq, nblk = (BATCH, SEQ // SEQ_BLK) if ctx else (DEC_BATCH, DEC_SEQ // SEQ_BLK)
    rb = _seq_row_block(ctx, nblk)
    hk = GLA_HEADS * GLA_DK
    hv = GLA_HEADS * GLA_DV
    in_specs = [pl.BlockSpec((SEQ_BLK, hk), lambda d, s, j: (rb(d, s, j), PC_GQ // hk)),
                pl.BlockSpec((SEQ_BLK, hk), lambda d, s, j: (rb(d, s, j), PC_GK // hk)),
                pl.BlockSpec((SEQ_BLK, hv), lambda d, s, j: (rb(d, s, j), PC_GV // hv)),
                pl.BlockSpec((SEQ_BLK, LANE), lambda d, s, j: (rb(d, s, j), PC_GLR // LANE)),
                pl.BlockSpec((1, 1, LANE, hk), lambda d, s, j: (layer, d, 0, 0)),
                pl.BlockSpec((1, 1, 1, hk), lambda d, s, j: (layer, d, 0, 0))]
    args = [proj, proj, proj, proj, wg_p, bg_p]
    nrows = NCTX if ctx else NSMP
    roff = 0 if ctx else NCTX // SEQ_BLK
    o_spec = pl.BlockSpec((1, SEQ_BLK, hv), lambda d, s, j: (d, rb(d, s, j) - roff, 0))
    o_shape = jax.ShapeDtypeStruct((2, nrows, hv), F32)
    st_blk = (GLA_HEADS, GLA_DK, GLA_DV)
    if ctx:
        out_specs = [o_spec, pl.BlockSpec((1, 1) + st_blk, lambda d, s, j: (s, d, 0, 0, 0))]
        out_shape = [o_shape, jax.ShapeDtypeStruct((BATCH, 2) + st_blk, F32)]
    else:
        in_specs.append(pl.BlockSpec((1, 1, 1) + st_blk, lambda d, s, j: (s, layer, d, 0, 0, 0)))
        args.append(s0)
        out_specs, out_shape = o_spec, o_shape
    return pl.pallas_call(
        functools.partial(_gla_kernel, nblk=nblk, has_s0=not ctx),
        grid=(2, nseq, nblk),
        in_specs=in_specs, out_specs=out_specs, out_shape=out_shape,
        scratch_shapes=[pltpu.VMEM(st_blk, F32)],
        compiler_params=_cp(("arbitrary", "arbitrary", "arbitrary")),
        name="gla_ctx" if ctx else "gla_smp",
    )(*args)


GDN_G = 4
GDN_GW = GDN_G * GDN_DK
HALO = 16


def _gdn_kernel(*refs, nblk, has_s0):
    if has_s0:
        (x_ref, xp_ref, xn_ref, ab_ref, cw_ref, alog_ref, dtb_ref, s0_ref,
         o_ref, q_scr, k_scr, v_scr, s_scr) = refs
    else:
        (x_ref, xp_ref, xn_ref, ab_ref, cw_ref, alog_ref, dtb_ref,
         o_ref, sfin_ref, q_scr, k_scr, v_scr, s_scr) = refs
    d = pl.program_id(0)
    j = pl.program_id(2)
    fwd = d == 0
    bi = j + d * (nblk - 1 - 2 * j)
    hk = GDN_HEADS * GDN_DK
    ngrp = GDN_HEADS // GDN_G

    @pl.when(j == 0)
    def _():
        s_scr[...] = jnp.zeros_like(s_scr)
        if has_s0:
            for h in range(GDN_HEADS):
                g, hh = divmod(h, GDN_G)
                s_scr[g, hh * GDN_DK:(hh + 1) * GDN_DK, hh * GDN_DV:(hh + 1) * GDN_DV] = s0_ref[0, 0, 0, h]

    x = x_ref[...].astype(F32)
    rowi = lax.broadcasted_iota(jnp.int32, x.shape, 0)
    prev_row = jnp.where(bi > 0, xp_ref[...].astype(F32)[HALO - 1:HALO], 0.0)
    next_row = jnp.where(bi < nblk - 1, xn_ref[...].astype(F32)[0:1], 0.0)
    x_prev = jnp.where(rowi == 0, prev_row, pltpu.roll(x, 1, 0))
    x_next = jnp.where(rowi == SEQ_BLK - 1, next_row, pltpu.roll(x, SEQ_BLK - 1, 0))
    cw = cw_ref[0]
    y = _silu(x_prev * cw[0:1] + x * cw[1:2] + x_next * cw[2:3])

    lane = lax.broadcasted_iota(jnp.int32, (SEQ_BLK, LANE), 1)
    lo_half = lane < GDN_DK

    def l2n(t):
        parts = []
        for c in range(hk // LANE):
            tc = t[:, c * LANE:(c + 1) * LANE]
            sq = tc * tc
            s_lo = jnp.sum(jnp.where(lo_half, sq, 0.0), axis=-1, keepdims=True)
            s_hi = jnp.sum(jnp.where(lo_half, 0.0, sq), axis=-1, keepdims=True)
            parts.append(tc * lax.rsqrt(jnp.where(lo_half, s_lo, s_hi) + EPS))
        return jnp.concatenate(parts, axis=1)

    q_scr[...] = l2n(y[:, :hk]) * (GDN_DK ** -0.5)
    k_scr[...] = l2n(y[:, hk:2 * hk])
    v_scr[...] = y[:, 2 * hk:]

    _, _, incl64 = _chunk_masks(fwd)
    tri_b = jnp.where(incl64, 1.0, 0.0).astype(BF16)
    row = lax.broadcasted_iota(jnp.int32, (CHUNK, GDN_GW), 0)
    col = lax.broadcasted_iota(jnp.int32, (CHUNK, GDN_GW), 1) % CHUNK
    lag = (row - col) * jnp.where(fwd, 1, -1)
    incl = lag >= 0
    strict = lag > 0
    eye = col == row
    eye_f = jnp.where(eye, 1.0, 0.0)
    blk8 = (row // 8) == (col // 8)
    blk16 = (row // 16) == (col // 16)
    blk32 = (row // 32) == (col // 32)
    brow = lax.broadcasted_iota(jnp.int32, (GDN_GW, GDN_GW), 0) // CHUNK
    bcol = lax.broadcasted_iota(jnp.int32, (GDN_GW, GDN_GW), 1) // CHUNK
    same_head = brow == bcol

    def bdiag(t):
        tb = t.astype(BF16)
        return jnp.where(same_head, jnp.concatenate([tb] * GDN_G, axis=0), jnp.zeros((), BF16))

    def bdmm(a, b):
        return _dot(a.astype(BF16), bdiag(b))

    neg_a = -jnp.exp(alog_ref[0, 0])
    dtb = dtb_ref[0, 0]

    chains = [(c, g) for c in range(NCH) for g in range(ngrp)]
    rows_of = []
    gams, betas = [], []
    for c in range(NCH):
        ci = c + d * (NCH - 1 - 2 * c)
        rows = pl.ds(pl.multiple_of(ci * CHUNK, CHUNK), CHUNK)
        rows_of.append(rows)
        ab = ab_ref[rows, :].astype(F32)
        gams.append(_cumsum_rows(tri_b, neg_a * _softplus(ab + dtb)))
        betas.append(jax.nn.sigmoid(ab))

    ch = []
    for c, g in chains:
        gl = slice(g * GDN_GW, (g + 1) * GDN_GW)
        gam, beta_all = gams[c], betas[c]
        gcol = jnp.concatenate(
            [jnp.broadcast_to(gam[:, g * GDN_G + h:g * GDN_G + h + 1], (CHUNK, GDN_DK))
             for h in range(GDN_G)], axis=1)
        bcol_ = jnp.concatenate(
            [jnp.broadcast_to(beta_all[:, GDN_HEADS + g * GDN_G + h:GDN_HEADS + g * GDN_G + h + 1],
                              (CHUNK, GDN_DK)) for h in range(GDN_G)], axis=1)
        grow = jnp.sum(jnp.where(eye, gcol, 0.0), axis=0, keepdims=True)
        glast = jnp.where(fwd, gcol[CHUNK - 1:CHUNK], gcol[0:1])
        decay = jnp.where(incl, jnp.exp(gcol - grow), 0.0)
        egc = jnp.exp(gcol)
        qg = q_scr[rows_of[c], gl]
        kg = k_scr[rows_of[c], gl]
        vg = v_scr[rows_of[c], gl]
        kq = _dot_nt(jnp.concatenate([kg, qg], axis=0).astype(BF16), bdiag(kg))
        m = jnp.where(strict, bcol_ * kq[:CHUNK] * decay, 0.0)
        ch.append(dict(gl=gl, rows=rows_of[c], g=g, m=m,
                       aqk=(kq[CHUNK:] * decay).astype(BF16),
                       vb=vg * bcol_, kb=kg * (bcol_ * egc),
                       qe=(qg * egc).astype(BF16),
                       kend=(kg * jnp.exp(glast - gcol)).astype(BF16),
                       eg=jnp.exp(glast)))
    for t in ch:
        t["n8"] = jnp.where(blk8, t["m"], 0.0)
        t["n2"] = bdmm(t["n8"], t["n8"])
    for t in ch:
        t["p1"] = eye_f - t["n8"]
        t["n4"] = bdmm(t["n2"], t["n2"])
        t["p1"] = t["p1"] + bdmm(t["p1"], t["n2"])
    for t in ch:
        t["dinv"] = t["p1"] + bdmm(t["p1"], t["n4"])
    for inner, outer in ((blk8, blk16), (blk16, blk32), (blk32, None)):
        off = jnp.logical_not(inner) if outer is None else jnp.logical_and(outer, jnp.logical_not(inner))
        for t in ch:
            t["dl"] = bdmm(t["dinv"], jnp.where(off, t["m"], 0.0))
        for t in ch:
            t["dinv"] = t["dinv"] - bdmm(t["dl"], t["dinv"])
    for t in ch:
        t["u"] = bdmm(t["dinv"], t["vb"])
        t["w"] = bdmm(t["dinv"], t["kb"]).astype(BF16)

    states = [s_scr[g] for g in range(ngrp)]
    for c in range(NCH):
        cur = ch[c * ngrp:(c + 1) * ngrp]
        sbb = [states[t["g"]].astype(BF16) for t in cur]
        ws = [_dot(t["w"], sbb[i]) for i, t in enumerate(cur)]
        os_ = [_dot(t["qe"], sbb[i]) for i, t in enumerate(cur)]
        for i, t in enumerate(cur):
            v_new = t["u"] - ws[i]
            o_ref[0, t["rows"], t["gl"]] = os_[i] + bdmm(t["aqk"], v_new)
            upd = _dot_tn(t["kend"], v_new.astype(BF16))
            states[t["g"]] = t["eg"] * states[t["g"]] + jnp.where(same_head, upd, 0.0)
    for g in range(ngrp):
        s_scr[g] = states[g]

    if not has_s0:
        @pl.when(j == nblk - 1)
        def _():
            for h in range(GDN_HEADS):
                g, hh = divmod(h, GDN_G)
                sfin_ref[0, 0, h] = s_scr[g, hh * GDN_DK:(hh + 1) * GDN_DK, hh * GDN_DV:(hh + 1) * GDN_DV]


def _gdn(proj, conv_w, alog_p, dtb_p, s0, layer, ctx):
    nseq, nblk = (BATCH, SEQ // SEQ_BLK) if ctx else (DEC_BATCH, DEC_SEQ // SEQ_BLK)
    rb = _seq_row_block(ctx, nblk)
    cw = 2 * GDN_HEADS * GDN_DK + GDN_HEADS * GDN_DV
    hv = GDN_HEADS * GDN_DV
    halo = SEQ_BLK // HALO
    last_h = R // HALO - 1
    in_specs = [pl.BlockSpec((SEQ_BLK, cw), lambda d, s, j: (rb(d, s, j), PC_DQKV // cw)),
                pl.BlockSpec((HALO, cw), lambda d, s, j: (jnp.maximum(rb(d, s, j) * halo - 1, 0), PC_DQKV // cw)),
                pl.BlockSpec((HALO, cw), lambda d, s, j: (jnp.minimum((rb(d, s, j) + 1) * halo, last_h), PC_DQKV // cw)),
                pl.BlockSpec((SEQ_BLK, LANE), lambda d, s, j: (rb(d, s, j), PC_DAB // LANE + d)),
                pl.BlockSpec((1, 3, cw), lambda d, s, j: (layer, 0, 0)),
                pl.BlockSpec((1, 1, 1, LANE), lambda d, s, j: (layer, d, 0, 0)),
                pl.BlockSpec((1, 1, 1, LANE), lambda d, s, j: (layer, d, 0, 0))]
    args = [proj, proj, proj, proj, conv_w, alog_p, dtb_p]
    nrows = NCTX if ctx else NSMP
    roff = 0 if ctx else NCTX // SEQ_BLK
    o_spec = pl.BlockSpec((1, SEQ_BLK, hv), lambda d, s, j: (d, rb(d, s, j) - roff, 0))
    o_shape = jax.ShapeDtypeStruct((2, nrows, hv), F32)
    st_blk = (GDN_HEADS, GDN_DK, GDN_DV)
    if ctx:
        out_specs = [o_spec, pl.BlockSpec((1, 1) + st_blk, lambda d, s, j: (s, d, 0, 0, 0))]
        out_shape = [o_shape, jax.ShapeDtypeStruct((BATCH, 2) + st_blk, F32)]
    else:
        in_specs.append(pl.BlockSpec((1, 1, 1) + st_blk, lambda d, s, j: (s, layer, d, 0, 0, 0)))
        args.append(s0)
        out_specs, out_shape = o_spec, o_shape
    return pl.pallas_call(
        functools.partial(_gdn_kernel, nblk=nblk, has_s0=not ctx),
        grid=(2, nseq, nblk),
        in_specs=in_specs, out_specs=out_specs, out_shape=out_shape,
        scratch_shapes=[pltpu.VMEM((SEQ_BLK, GDN_HEADS * GDN_DK), F32),
                        pltpu.VMEM((SEQ_BLK, GDN_HEADS * GDN_DK), F32),
                        pltpu.VMEM((SEQ_BLK, hv), F32),
                        pltpu.VMEM((GDN_HEADS // GDN_G, GDN_GW, GDN_GW), F32)],
        compiler_params=_cp(("arbitrary", "arbitrary", "arbitrary")),
        name="gdn_ctx" if ctx else "gdn_smp",
    )(*args)


def _group_rms(x, width):
    parts = []
    lane = lax.broadcasted_iota(jnp.int32, (x.shape[0], LANE), 1)
    lo_half = lane < 64
    for c in range(x.shape[1] // LANE):
        xc = x[:, c * LANE:(c + 1) * LANE]
        sq = xc * xc
        if width == LANE:
            ms = jnp.mean(sq, axis=-1, keepdims=True)
        else:
            s_lo = jnp.sum(jnp.where(lo_half, sq, 0.0), axis=-1, keepdims=True)
            s_hi = jnp.sum(jnp.where(lo_half, 0.0, sq), axis=-1, keepdims=True)
            ms = jnp.where(lo_half, s_lo, s_hi) * (1.0 / width)
        parts.append(xc * lax.rsqrt(ms + EPS))
    return jnp.concatenate(parts, axis=1)


def _merge_kernel(ymc_ref, yms_ref, ogc_ref, ogs_ref, gr_ref, odc_ref, ods_ref, dz_ref, gt_ref, h_ref,
                  ga_ref, gn_ref, dn_ref, wb_ref, wo_ref, o_ref, ym_scr, og_scr, od_scr):
    is_ctx = pl.program_id(0) < NCTX // TM_MERGE

    @pl.when(is_ctx)
    def _():
        ym_scr[...] = ymc_ref[...]
        og_scr[...] = ogc_ref[0] + ogc_ref[1]
        od_scr[...] = odc_ref[0] + odc_ref[1]

    @pl.when(jnp.logical_not(is_ctx))
    def _():
        ym_scr[...] = yms_ref[...]
        og_scr[...] = ogs_ref[0] + ogs_ref[1]
        od_scr[...] = ods_ref[0] + ods_ref[1]

    y_gla = _group_rms(og_scr[...], GLA_DV) * gn_ref[0] * _silu(gr_ref[...].astype(F32))
    y_gdn = _group_rms(od_scr[...], GDN_DV) * dn_ref[0] * _silu(dz_ref[...].astype(F32))
    gates = jax.nn.sigmoid(gt_ref[...].astype(F32))
    d = D_MODEL
    acc = gates[:, :d] * _dot(ym_scr[...], wb_ref[0, 0])
    acc = acc + gates[:, d:2 * d] * _dot(y_gla.astype(BF16), wb_ref[0, 1])
    acc = acc + gates[:, 2 * d:] * _dot(y_gdn.astype(BF16), wb_ref[0, 2])
    y = _dot(acc.astype(BF16), wo_ref[0])
    o_ref[...] = h_ref[...] + ga_ref[0] * y


def _merge(y_mla, o_gla, o_gdn, proj, h, mod3, gla_norm_p, gdn_norm_p, w_branch_b, w_out_b, layer):
    tm = TM_MERGE
    bw = 512
    nc = NCTX // tm

    def cmap(i):
        return jnp.minimum(i, nc - 1)

    def smap(i):
        return jnp.maximum(i - nc, 0)

    return pl.pallas_call(
        _merge_kernel,
        grid=(R // tm,),
        in_specs=[pl.BlockSpec((tm, bw), lambda i: (cmap(i), 0)),
                  pl.BlockSpec((tm, bw), lambda i: (smap(i), 0)),
                  pl.BlockSpec((2, tm, bw), lambda i: (0, cmap(i), 0)),
                  pl.BlockSpec((2, tm, bw), lambda i: (0, smap(i), 0)),
                  pl.BlockSpec((tm, bw), lambda i: (i, PC_GR // bw)),
                  pl.BlockSpec((2, tm, bw), lambda i: (0, cmap(i), 0)),
                  pl.BlockSpec((2, tm, bw), lambda i: (0, smap(i), 0)),
                  pl.BlockSpec((tm, bw), lambda i: (i, PC_DZ // bw)),
                  pl.BlockSpec((tm, 3 * D_MODEL), lambda i: (i, PC_GATES // (3 * D_MODEL))),
                  pl.BlockSpec((tm, D_MODEL), lambda i: (i, 0)),
                  _mod_spec(layer, 2, tm),
                  pl.BlockSpec((1, 1, bw), lambda i: (layer, 0, 0)),
                  pl.BlockSpec((1, 1, bw), lambda i: (layer, 0, 0)),
                  pl.BlockSpec((1, 3, bw, D_MODEL), lambda i: (layer, 0, 0, 0)),
                  pl.BlockSpec((1, D_MODEL, D_MODEL), lambda i: (layer, 0, 0))],
        out_specs=pl.BlockSpec((tm, D_MODEL), lambda i: (i, 0)),
        out_shape=jax.ShapeDtypeStruct((R, D_MODEL), F32),
        scratch_shapes=[pltpu.VMEM((tm, bw), BF16), pltpu.VMEM((tm, bw), F32), pltpu.VMEM((tm, bw), F32)],
        compiler_params=_cp(("arbitrary",)),
        name="merge",
    )(y_mla[0], y_mla[1], o_gla[0], o_gla[1], proj, o_gdn[0], o_gdn[1], proj, proj, h, mod3,
      gla_norm_p, gdn_norm_p, w_branch_b, w_out_b)


def _ffn_kernel(h_ref, g_ref, sc_ref, sh_ref, gf_ref, wg_ref, wu_ref, wo_ref, fn_ref, o_ref,
                xf_ref, acc_ref, *, final):
    f = pl.program_id(1)

    @pl.when(f == 0)
    def _():
        y = _rms(h_ref[...], g_ref[0])
        xf_ref[...] = (y * (1.0 + sc_ref[0]) + sh_ref[0]).astype(BF16)
        acc_ref[...] = jnp.zeros_like(acc_ref)

    xf = xf_ref[...]
    a = _silu(_dot(xf, wg_ref[0])) * _dot(xf, wu_ref[0])
    acc_ref[...] += _dot(a.astype(BF16), wo_ref[0])

    @pl.when(f == pl.num_programs(1) - 1)
    def _():
        out = h_ref[...] + gf_ref[0] * acc_ref[...]
        if final:
            out = _rms(out, fn_ref[...])
        o_ref[...] = out


def _ffn(h, mod3, norm_ffn, wg_b, wu_b, wo_b, final_norm, layer, final):
    tm, tf = TM_FFN, TF_FFN
    return pl.pallas_call(
        functools.partial(_ffn_kernel, final=final),
        grid=(R // tm, D_FF // tf),
        in_specs=[pl.BlockSpec((tm, D_MODEL), lambda i, f: (i, 0)),
                  pl.BlockSpec((1, 1, D_MODEL), lambda i, f: (layer, 0, 0)),
                  _mod_spec(layer, 4, tm),
                  _mod_spec(layer, 3, tm),
                  _mod_spec(layer, 5, tm),
                  pl.BlockSpec((1, D_MODEL, tf), lambda i, f: (layer, 0, f)),
                  pl.BlockSpec((1, D_MODEL, tf), lambda i, f: (layer, 0, f)),
                  pl.BlockSpec((1, tf, D_MODEL), lambda i, f: (layer, f, 0)),
                  pl.BlockSpec((1, D_MODEL), lambda i, f: (0, 0))],
        out_specs=pl.BlockSpec((tm, D_MODEL), lambda i, f: (i, 0)),
        out_shape=jax.ShapeDtypeStruct((R, D_MODEL), F32),
        scratch_shapes=[pltpu.VMEM((tm, D_MODEL), BF16), pltpu.VMEM((tm, D_MODEL), F32)],
        compiler_params=_cp(("arbitrary", "arbitrary")),
        name="ffn_final" if final else "ffn",
    )(h, norm_ffn.reshape(DEPTH, 1, D_MODEL), mod3, mod3, mod3, wg_b, wu_b, wo_b,
      final_norm.reshape(1, D_MODEL))


def _rope_partner(w):
    h = AXIS_DIM // 2
    parts = []
    for a in range(2):
        x1 = w[..., a * AXIS_DIM:a * AXIS_DIM + h]
        x2 = w[..., a * AXIS_DIM + h:(a + 1) * AXIS_DIM]
        parts += [-x2, x1]
    return jnp.concatenate(parts, axis=-1)


def _pack_w_in(w_in):
    c = [0]
    for s in (Q_LORA, KV_LORA + ROPE_DIM, 256, 256, 512, 512, 2 * GLA_RANK, 1536, 512,
              2 * GDN_HEADS, 2 * GDN_HEADS, 3 * D_MODEL):
        c.append(c[-1] + s)
    seg = [w_in[..., c[i]:c[i + 1]] for i in range(12)]
    mq, mkv, gq, gk, gv, gr, glr, dqkv, dz, da, db, gates = seg

    def zeros(n):
        return jnp.zeros(w_in.shape[:-1] + (n,), w_in.dtype)

    kr = mkv[..., KV_LORA:]
    mla = jnp.concatenate([mq, mkv[..., :KV_LORA], kr, _rope_partner(kr), zeros(64)], axis=-1)
    dab = []
    for d in range(2):
        dab += [da[..., d * GDN_HEADS:(d + 1) * GDN_HEADS], db[..., d * GDN_HEADS:(d + 1) * GDN_HEADS],
                zeros(LANE - 2 * GDN_HEADS)]
    packed = jnp.concatenate([gates, dqkv, dz, gv, gr, mla, gq, gk, glr, zeros(LANE - 2 * GLA_RANK)]
                             + dab + [zeros(PROJ_W - PC_DAB - 2 * LANE)], axis=-1)
    return packed.astype(BF16)


def _pack_mla_weights(w_uq, w_ukv):
    l = w_uq.shape[0]
    qh = w_uq.reshape(l, Q_LORA, MLA_HEADS, MLA_NOPE + ROPE_DIM)
    zq = jnp.zeros((l, Q_LORA, MLA_HEADS, HEAD_W - MLA_NOPE - ROPE_DIM), w_uq.dtype)
    zn = jnp.zeros((l, Q_LORA, MLA_HEADS, MLA_NOPE), w_uq.dtype)
    wq = jnp.concatenate([qh, zq], axis=-1).reshape(l, Q_LORA, MLA_HEADS * HEAD_W)
    wq_sw = jnp.concatenate([zn, _rope_partner(qh[..., MLA_NOPE:]), zq], axis=-1).reshape(l, Q_LORA, MLA_HEADS * HEAD_W)
    wq_p = jnp.concatenate([wq, wq_sw], axis=-1).astype(BF16)

    kvh = w_ukv.reshape(l, KV_LORA, MLA_HEADS, MLA_NOPE + MLA_V)
    zk = jnp.zeros((l, KV_LORA, MLA_HEADS, HEAD_W - MLA_NOPE), w_ukv.dtype)
    wk_top = jnp.concatenate([kvh[..., :MLA_NOPE], zk], axis=-1).reshape(l, KV_LORA, MLA_HEADS * HEAD_W)
    place = jnp.zeros((KV_LORA, MLA_HEADS, HEAD_W), w_ukv.dtype)
    idx = jnp.arange(ROPE_DIM)
    for rep in range(2):
        place = place.at[rep * ROPE_DIM + idx, :, MLA_NOPE + idx].set(1.0)
    wk_bot = jnp.broadcast_to(place.reshape(1, KV_LORA, MLA_HEADS * HEAD_W), (l, KV_LORA, MLA_HEADS * HEAD_W))
    wk_p = jnp.concatenate([wk_top, wk_bot], axis=1).astype(BF16)
    wv_p = jnp.transpose(kvh[..., MLA_NOPE:], (0, 2, 3, 1)).reshape(l, VT_ROWS, KV_LORA).astype(BF16)
    return wq_p, wk_p, wv_p


def _rope_tables(tm):
    rows = DEC_SEQ // GRID_W
    row = jnp.repeat(jnp.arange(rows, dtype=F32), GRID_W)
    col = jnp.tile(jnp.arange(GRID_W, dtype=F32), rows)
    inv = ROPE_THETA ** (-jnp.arange(0, AXIS_DIM, 2, dtype=F32) / AXIS_DIM)
    ang_r, ang_c = row[:, None] * inv, col[:, None] * inv
    cos32 = jnp.concatenate([jnp.cos(ang_r)] * 2 + [jnp.cos(ang_c)] * 2, axis=-1)
    sin32 = jnp.concatenate([jnp.sin(ang_r)] * 2 + [jnp.sin(ang_c)] * 2, axis=-1)
    cos32 = jnp.concatenate([jnp.ones((tm, ROPE_DIM), F32), cos32], axis=0)
    sin32 = jnp.concatenate([jnp.zeros((tm, ROPE_DIM), F32), sin32], axis=0)
    n = cos32.shape[0]
    scale = (MLA_NOPE + ROPE_DIM) ** -0.5
    pad = HEAD_W - MLA_NOPE - ROPE_DIM
    cq = jnp.concatenate([jnp.ones((n, MLA_NOPE), F32), cos32, jnp.zeros((n, pad), F32)], axis=-1) * scale
    sq = jnp.concatenate([jnp.zeros((n, MLA_NOPE), F32), sin32, jnp.zeros((n, pad), F32)], axis=-1) * scale
    ck = jnp.concatenate([cos32, sin32, jnp.zeros((n, HEAD_W - 2 * ROPE_DIM), F32)], axis=-1)
    return jnp.concatenate([cq, sq, ck], axis=-1)


def _pad_lanes(x, n):
    return jnp.pad(x, [(0, 0)] * (x.ndim - 1) + [(0, n - x.shape[-1])])


def kernel(x_prompt, x_sample, cache_mla, state_gla, state_gdn, c, c_ctx, w_mod, b_mod, norm_mix, w_in,
           mla_q_norm, mla_w_uq, mla_kv_norm, mla_w_ukv, gla_w_gate, gla_b_gate, gla_norm, gdn_conv,
           gdn_a_log, gdn_dt_bias, gdn_norm, w_branch, w_out, norm_ffn, ffn_w_in, ffn_w_out, final_norm):
    w_in_p = _pack_w_in(w_in)
    wq_p, wk_p, wv_p = _pack_mla_weights(mla_w_uq, mla_w_ukv)
    tab = _rope_tables(TM_PREP)
    wg_p = jnp.zeros((DEPTH, 2, LANE, GLA_HEADS * GLA_DK), F32)
    for d in range(2):
        wg_p = wg_p.at[:, d, d * GLA_RANK:(d + 1) * GLA_RANK, :].set(gla_w_gate[:, d])
    wg_p = wg_p.astype(BF16)
    bg_p = gla_b_gate.reshape(DEPTH, 2, 1, GLA_HEADS * GLA_DK)
    alog_p = _pad_lanes(gdn_a_log, LANE).reshape(DEPTH, 2, 1, LANE)
    dtb_p = _pad_lanes(gdn_dt_bias, LANE).reshape(DEPTH, 2, 1, LANE)
    gla_norm_p = jnp.tile(gla_norm, (1, GLA_HEADS)).reshape(DEPTH, 1, GLA_HEADS * GLA_DV)
    gdn_norm_p = jnp.tile(gdn_norm, (1, GDN_HEADS)).reshape(DEPTH, 1, GDN_HEADS * GDN_DV)
    w_branch_b = w_branch.astype(BF16)
    w_out_b = w_out.astype(BF16)
    wg_b = ffn_w_in[..., :D_FF].astype(BF16)
    wu_b = ffn_w_in[..., D_FF:].astype(BF16)
    wo_b = ffn_w_out.astype(BF16)
    cache_p = _pad_lanes(cache_mla, 2 * KV_LORA)
    cond8 = jnp.concatenate([c_ctx[None, :], c, jnp.zeros((MOD_ROWS - 1 - DEC_BATCH, D_MODEL), F32)], axis=0)

    mod = _modulation(cond8, w_mod, b_mod)
    mod3 = mod.reshape(DEPTH * MOD_ROWS * 6, 1, D_MODEL)
    kc, vc = _kv_cache(cache_p, wk_p, wv_p)

    h = jnp.concatenate([x_prompt.reshape(NCTX, D_MODEL), x_sample.reshape(NSMP, D_MODEL)], axis=0)
    kv_list, gla_list, gdn_list = [], [], []
    for l in range(DEPTH):
        proj = _inproj(h, mod3, norm_mix, w_in_p, l)
        qp, kp, vtp, own = _mla_prep(proj, tab, mla_q_norm, mla_kv_norm, wq_p, wk_p, wv_p, l)
        y_mla = (_attention_ctx(qp, kp, vtp), _attention_smp(qp, kp, vtp, kc, vc, l))
        og_c, sg = _gla(proj, wg_p, bg_p, None, l, True)
        og_s = _gla(proj, wg_p, bg_p, state_gla, l, False)
        od_c, sd = _gdn(proj, gdn_conv, alog_p, dtb_p, None, l, True)
        od_s = _gdn(proj, gdn_conv, alog_p, dtb_p, state_gdn, l, False)
        h = _merge(y_mla, (og_c, og_s), (od_c, od_s), proj, h, mod3, gla_norm_p, gdn_norm_p,
                   w_branch_b, w_out_b, l)
        h = _ffn(h, mod3, norm_ffn, wg_b, wu_b, wo_b, final_norm, l, l == DEPTH - 1)
        kv_list.append(own[:NCTX, :KV_LORA + ROPE_DIM].reshape(BATCH, SEQ, KV_LORA + ROPE_DIM))
        gla_list.append(sg)
        gdn_list.append(sd)

    y_prompt = h[:NCTX].reshape(BATCH, SEQ, D_MODEL)
    y_sample = h[NCTX:].reshape(DEC_BATCH, DEC_SEQ, D_MODEL)
    return (y_prompt, y_sample, jnp.stack(kv_list, axis=1), jnp.stack(gla_list, axis=1),
            jnp.stack(gdn_list, axis=1))
```

```python
import functools

import jax
import jax.numpy as jnp
from jax import lax
from jax.experimental import pallas as pl
from jax.experimental.pallas import tpu as pltpu

F32 = jnp.float32
BF16 = jnp.bfloat16

D_MODEL = 1024
BATCH = 16
SEQ = 256
DEPTH = 2
DEC_BATCH = 2
DEC_SEQ = 4096
PAST_LEN = 256
GRID_W = 64
CHUNK = 64
EPS = 1e-6
MLA_HEADS = 8
MLA_NOPE = 64
ROPE_DIM = 32
AXIS_DIM = ROPE_DIM // 2
MLA_V = 64
Q_LORA = 256
KV_LORA = 128
ROPE_THETA = 10000.0
GLA_HEADS = 4
GLA_DK = 64
GLA_DV = 128
GLA_RANK = 16
GLA_TAU = 16.0
GDN_HEADS = 8
GDN_DK = 64
GDN_DV = 64
D_FF = ((8 * D_MODEL + 3 * 256 - 1) // (3 * 256)) * 256
MOD_W = 6 * D_MODEL

NCTX = BATCH * SEQ
NSMP = DEC_BATCH * DEC_SEQ
R = NCTX + NSMP
LANE = 128
HEAD_W = 128
VT_ROWS = MLA_HEADS * MLA_V
SEQ_BLK = 256
NCH = SEQ_BLK // CHUNK
MOD_ROWS = 8

PC_GATES = 0
PC_DQKV = 3072
PC_DZ = 4608
PC_GV = 5120
PC_GR = 5632
PC_MLA = 6144
PC_GQ = 6656
PC_GK = 6912
PC_GLR = 7168
PC_DAB = 7296
PROJ_W = 7680

TM_IN = 512
TN_IN = 1536
TM_PREP = 512
TM_MERGE = 512
TM_FFN = 1024
TF_FFN = 256
TQ_ATT = 256
TK_ATT = 512
ATT_AHEAD = 8
VMEM_LIMIT = 48 * 1024 * 1024


def _cp(sem):
    return pltpu.CompilerParams(dimension_semantics=sem, vmem_limit_bytes=VMEM_LIMIT)


def _dot(a, b):
    return jnp.dot(a, b, preferred_element_type=F32)


def _dot_nt(a, b):
    return lax.dot_general(a, b, (((1,), (1,)), ((), ())), preferred_element_type=F32)


def _dot_tn(a, b):
    return lax.dot_general(a, b, (((0,), (0,)), ((), ())), preferred_element_type=F32)


def _split2(x):
    hi = x.astype(BF16)
    lo = (x - hi.astype(F32)).astype(BF16)
    return hi, lo


def _split3(x):
    hi = x.astype(BF16)
    r1 = x - hi.astype(F32)
    mid = r1.astype(BF16)
    lo = (r1 - mid.astype(F32)).astype(BF16)
    return hi, mid, lo


def _cumsum_rows(tri_b, x):
    hi, mid, lo = _split3(x)
    return _dot(tri_b, hi) + (_dot(tri_b, mid) + _dot(tri_b, lo))


def _silu(x):
    return x * jax.nn.sigmoid(x)


def _softplus(x):
    return jnp.maximum(x, 0.0) + jnp.log(1.0 + jnp.exp(-jnp.abs(x)))


def _log_sigmoid(x):
    return jnp.minimum(x, 0.0) - jnp.log(1.0 + jnp.exp(-jnp.abs(x)))


def _rms(x, g):
    return x * lax.rsqrt(jnp.mean(x * x, axis=-1, keepdims=True) + EPS) * g


def _mod_row(row_start):
    return jnp.where(row_start < NCTX, 0, 1 + (row_start - NCTX) // DEC_SEQ)


def _mod_spec(layer, which, tm, axis=0):
    def imap(*idx):
        return ((layer * MOD_ROWS + _mod_row(idx[axis] * tm)) * 6 + which, 0, 0)
    return pl.BlockSpec((1, 1, D_MODEL), imap)


def _mod_kernel(c_ref, w_ref, b_ref, o_ref):
    x = _silu(c_ref[...])
    xh, xl = _split2(x)
    wh, wl = _split2(w_ref[0])
    o_ref[0] = _dot(xh, wh) + (_dot(xl, wh) + _dot(xh, wl)) + b_ref[0]


def _modulation(cond8, w_mod, b_mod):
    tn = 1536
    return pl.pallas_call(
        _mod_kernel,
        grid=(DEPTH, MOD_W // tn),
        in_specs=[pl.BlockSpec((MOD_ROWS, D_MODEL), lambda l, j: (0, 0)),
                  pl.BlockSpec((1, D_MODEL, tn), lambda l, j: (l, 0, j)),
                  pl.BlockSpec((1, 1, tn), lambda l, j: (l, 0, j))],
        out_specs=pl.BlockSpec((1, MOD_ROWS, tn), lambda l, j: (l, 0, j)),
        out_shape=jax.ShapeDtypeStruct((DEPTH, MOD_ROWS, MOD_W), F32),
        compiler_params=_cp(("arbitrary", "arbitrary")),
        name="modulation",
    )(cond8, w_mod, b_mod.reshape(DEPTH, 1, MOD_W))


def _family_specs(tm, width, axis=0):
    nc = NCTX // tm
    return [pl.BlockSpec((tm, width), lambda *idx: (jnp.minimum(idx[axis], nc - 1), 0)),
            pl.BlockSpec((tm, width), lambda *idx: (jnp.maximum(idx[axis] - nc, 0), 0))]


def _family_rows(refs, tm, axis=0):
    is_ctx = pl.program_id(axis) < NCTX // tm
    return jnp.where(is_ctx, refs[0][...], refs[1][...])


def _inproj_kernel(*refs, split):
    if split:
        hc_ref, hs_ref, g_ref, sc_ref, sh_ref, w_ref, o_ref = refs
        x = _family_rows((hc_ref, hs_ref), TM_IN, axis=1)
    else:
        h_ref, g_ref, sc_ref, sh_ref, w_ref, o_ref = refs
        x = h_ref[...]
    y = _rms(x, g_ref[0])
    xn = (y * (1.0 + sc_ref[0]) + sh_ref[0]).astype(BF16)
    o_ref[...] = _dot(xn, w_ref[0]).astype(BF16)


def _inproj(h, mod3, norm_mix, w_in_p, layer):
    split = isinstance(h, tuple)
    h_specs = (_family_specs(TM_IN, D_MODEL, axis=1) if split
               else [pl.BlockSpec((TM_IN, D_MODEL), lambda j, i: (i, 0))])
    h_args = list(h) if split else [h]
    return pl.pallas_call(
        functools.partial(_inproj_kernel, split=split),
        grid=(PROJ_W // TN_IN, R // TM_IN),
        in_specs=h_specs + [
                  pl.BlockSpec((1, 1, D_MODEL), lambda j, i: (layer, 0, 0)),
                  _mod_spec(layer, 1, TM_IN, axis=1),
                  _mod_spec(layer, 0, TM_IN, axis=1),
                  pl.BlockSpec((1, D_MODEL, TN_IN), lambda j, i: (layer, 0, j))],
        out_specs=pl.BlockSpec((TM_IN, TN_IN), lambda j, i: (i, j)),
        out_shape=jax.ShapeDtypeStruct((R, PROJ_W), BF16),
        compiler_params=_cp(("arbitrary", "arbitrary")),
        name="inproj",
    )(*h_args, norm_mix.reshape(DEPTH, 1, D_MODEL), mod3, mod3, w_in_p)


def _mla_prep_kernel(pm_ref, tab_ref, qn_ref, kvn_ref, wq_ref, wk_ref, wv_ref,
                     q_ref, k_ref, v_ref, own_ref):
    pm = pm_ref[...].astype(F32)
    tab = tab_ref[...]
    qn = _rms(pm[:, :Q_LORA], qn_ref[0]).astype(BF16)
    q2 = _dot(qn, wq_ref[0])
    hw = MLA_HEADS * HEAD_W
    cq = jnp.tile(tab[:, :HEAD_W], (1, MLA_HEADS))
    sq = jnp.tile(tab[:, HEAD_W:2 * HEAD_W], (1, MLA_HEADS))
    q_ref[...] = (q2[:, :hw] * cq + q2[:, hw:] * sq).astype(BF16)
    ckv = _rms(pm[:, Q_LORA:Q_LORA + KV_LORA], kvn_ref[0])
    kr = pm[:, Q_LORA + KV_LORA:] * tab[:, 2 * HEAD_W:]
    lhs = jnp.concatenate([ckv, kr], axis=1)
    own_ref[...] = lhs
    lb = lhs.astype(BF16)
    k_ref[...] = _dot(lb, wk_ref[0]).astype(BF16)
    v_ref[...] = _dot_nt(wv_ref[0], lb[:, :KV_LORA]).astype(BF16)


def _mla_prep(proj, tab, q_norm, kv_norm, wq_p, wk_p, wv_p, layer):
    tm = TM_PREP
    hw = MLA_HEADS * HEAD_W

    def tab_map(i):
        r0 = i * tm
        return (jnp.where(r0 < NCTX, 0, 1 + ((r0 - NCTX) % DEC_SEQ) // tm), 0)

    return pl.pallas_call(
        _mla_prep_kernel,
        grid=(R // tm,),
        in_specs=[pl.BlockSpec((tm, 512), lambda i: (i, PC_MLA // 512)),
                  pl.BlockSpec((tm, 3 * HEAD_W), tab_map),
                  pl.BlockSpec((1, 1, Q_LORA), lambda i: (layer, 0, 0)),
                  pl.BlockSpec((1, 1, KV_LORA), lambda i: (layer, 0, 0)),
                  pl.BlockSpec((1, Q_LORA, 2 * hw), lambda i: (layer, 0, 0)),
                  pl.BlockSpec((1, 2 * KV_LORA, hw), lambda i: (layer, 0, 0)),
                  pl.BlockSpec((1, VT_ROWS, KV_LORA), lambda i: (layer, 0, 0))],
        out_specs=[pl.BlockSpec((tm, hw), lambda i: (i, 0)),
                   pl.BlockSpec((tm, hw), lambda i: (i, 0)),
                   pl.BlockSpec((VT_ROWS, tm), lambda i: (0, i)),
                   pl.BlockSpec((tm, 2 * KV_LORA), lambda i: (i, 0))],
        out_shape=[jax.ShapeDtypeStruct((R, hw), BF16),
                   jax.ShapeDtypeStruct((R, hw), BF16),
                   jax.ShapeDtypeStruct((VT_ROWS, R), BF16),
                   jax.ShapeDtypeStruct((R, 2 * KV_LORA), F32)],
        compiler_params=_cp(("arbitrary",)),
        name="mla_prep",
    )(proj, tab, q_norm.reshape(DEPTH, 1, Q_LORA), kv_norm.reshape(DEPTH, 1, KV_LORA),
      wq_p, wk_p, wv_p)


def _kv_cache_kernel(c_ref, wk_ref, wv_ref, k_ref, v_ref):
    lb = c_ref[0, 0].astype(BF16)
    k_ref[0, 0] = _dot(lb, wk_ref[0]).astype(BF16)
    v_ref[0, 0] = _dot_nt(wv_ref[0], lb[:, :KV_LORA]).astype(BF16)


def _kv_cache(cache_p, wk_p, wv_p):
    hw = MLA_HEADS * HEAD_W
    return pl.pallas_call(
        _kv_cache_kernel,
        grid=(DEC_BATCH, DEPTH),
        in_specs=[pl.BlockSpec((1, 1, PAST_LEN, 2 * KV_LORA), lambda b, l: (b, l, 0, 0)),
                  pl.BlockSpec((1, 2 * KV_LORA, hw), lambda b, l: (l, 0, 0)),
                  pl.BlockSpec((1, VT_ROWS, KV_LORA), lambda b, l: (l, 0, 0))],
        out_specs=[pl.BlockSpec((1, 1, PAST_LEN, hw), lambda b, l: (b, l, 0, 0)),
                   pl.BlockSpec((1, 1, VT_ROWS, PAST_LEN), lambda b, l: (b, l, 0, 0))],
        out_shape=[jax.ShapeDtypeStruct((DEC_BATCH, DEPTH, PAST_LEN, hw), BF16),
                   jax.ShapeDtypeStruct((DEC_BATCH, DEPTH, VT_ROWS, PAST_LEN), BF16)],
        compiler_params=_cp(("arbitrary", "arbitrary")),
        name="kv_cache",
    )(cache_p, wk_p, wv_p)


def _attn_kernel(*refs, heads, has_cache):
    if has_cache:
        q_ref, k_ref, vt_ref, kc_ref, vct_ref, o_ref = refs
    else:
        q_ref, k_ref, vt_ref, o_ref = refs
    n_keys = k_ref.shape[0]
    tk = min(TK_ATT, n_keys)
    tiles = ([("cache", 0, PAST_LEN)] if has_cache else []) + [("own", s0, tk) for s0 in range(0, n_keys, tk)]
    units = [(h, t) for t in range(len(tiles)) for h in range(heads)]
    qs = [q_ref[:, h * HEAD_W:(h + 1) * HEAD_W] for h in range(heads)]

    def scores(h, t):
        kind, s0, n = tiles[t]
        sl = slice(h * HEAD_W, (h + 1) * HEAD_W)
        keys = kc_ref[0, 0, :, sl] if kind == "cache" else k_ref[s0:s0 + n, sl]
        return _dot_nt(keys, qs[h])

    def values_t(h, t):
        kind, s0, n = tiles[t]
        sv = slice(h * MLA_V, (h + 1) * MLA_V)
        return vct_ref[0, 0, sv, :] if kind == "cache" else vt_ref[sv, s0:s0 + n]

    ahead = ATT_AHEAD
    pending = {u: scores(*u) for u in units[:ahead]}
    m, l, acc = [None] * heads, [None] * heads, [None] * heads
    for i, (h, t) in enumerate(units):
        if i + ahead < len(units):
            pending[units[i + ahead]] = scores(*units[i + ahead])
        st = pending.pop((h, t))
        mt = jnp.max(st, axis=0, keepdims=True)
        if m[h] is None:
            m[h] = mt
            p = jnp.exp(st - mt)
            l[h] = jnp.sum(p, axis=0, keepdims=True)
            acc[h] = _dot(values_t(h, t), p.astype(BF16))
        else:
            m_new = jnp.maximum(m[h], mt)
            alpha = jnp.exp(m[h] - m_new)
            p = jnp.exp(st - m_new)
            l[h] = alpha * l[h] + jnp.sum(p, axis=0, keepdims=True)
            acc[h] = alpha * acc[h] + _dot(values_t(h, t), p.astype(BF16))
            m[h] = m_new
    outs = [acc[h] / l[h] for h in range(heads)]
    o_ref[...] = jnp.concatenate(outs, axis=0).T.astype(BF16)


def _attention_ctx(qp, kp, vtp):
    hw = MLA_HEADS * HEAD_W
    return pl.pallas_call(
        functools.partial(_attn_kernel, heads=MLA_HEADS, has_cache=False),
        grid=(BATCH,),
        in_specs=[pl.BlockSpec((SEQ, hw), lambda s: (s, 0)),
                  pl.BlockSpec((SEQ, hw), lambda s: (s, 0)),
                  pl.BlockSpec((VT_ROWS, SEQ), lambda s: (0, s))],
        out_specs=pl.BlockSpec((SEQ, MLA_HEADS * MLA_V), lambda s: (s, 0)),
        out_shape=jax.ShapeDtypeStruct((NCTX, MLA_HEADS * MLA_V), BF16),
        compiler_params=_cp(("arbitrary",)),
        name="attn_ctx",
    )(qp, kp, vtp)


def _attention_smp(qp, kp, vtp, kc, vct, layer):
    hpb = 2
    wq = hpb * HEAD_W
    wv = hpb * MLA_V
    nq = DEC_SEQ // TQ_ATT
    qoff = NCTX // TQ_ATT
    koff = NCTX // DEC_SEQ
    return pl.pallas_call(
        functools.partial(_attn_kernel, heads=hpb, has_cache=True),
        grid=(DEC_BATCH, MLA_HEADS // hpb, nq),
        in_specs=[pl.BlockSpec((TQ_ATT, wq), lambda b, g, i: (qoff + b * nq + i, g)),
                  pl.BlockSpec((DEC_SEQ, wq), lambda b, g, i: (koff + b, g)),
                  pl.BlockSpec((wv, DEC_SEQ), lambda b, g, i: (g, koff + b)),
                  pl.BlockSpec((1, 1, PAST_LEN, wq), lambda b, g, i: (b, layer, 0, g)),
                  pl.BlockSpec((1, 1, wv, PAST_LEN), lambda b, g, i: (b, layer, g, 0))],
        out_specs=pl.BlockSpec((TQ_ATT, hpb * MLA_V), lambda b, g, i: (b * nq + i, g)),
        out_shape=jax.ShapeDtypeStruct((NSMP, MLA_HEADS * MLA_V), BF16),
        compiler_params=_cp(("arbitrary", "arbitrary", "arbitrary")),
        name="attn_smp",
    )(qp, kp, vtp, kc, vct)


def _seq_row_block(ctx, nblk):
    if ctx:
        return lambda d, s, j: s
    off = NCTX // SEQ_BLK
    return lambda d, s, j: off + s * nblk + j + d * (nblk - 1 - 2 * j)


def _chunk_masks(fwd):
    row = lax.broadcasted_iota(jnp.int32, (CHUNK, CHUNK), 0)
    col = lax.broadcasted_iota(jnp.int32, (CHUNK, CHUNK), 1)
    incl = (row - col) * jnp.where(fwd, 1, -1) >= 0
    return row, col, incl


def _gla_kernel(*refs, nblk, has_s0):
    if has_s0:
        q_ref, k_ref, v_ref, glr_ref, wg_ref, bg_ref, s0_ref, o_ref, s_scr = refs
    else:
        q_ref, k_ref, v_ref, glr_ref, wg_ref, bg_ref, o_ref, sfin_ref, s_scr = refs
    d = pl.program_id(0)
    j = pl.program_id(2)
    fwd = d == 0

    @pl.when(j == 0)
    def _():
        if has_s0:
            s_scr[...] = s0_ref[0, 0, 0]
        else:
            s_scr[...] = jnp.zeros_like(s_scr)

    _, _, incl = _chunk_masks(fwd)
    tri_b = jnp.where(incl, 1.0, 0.0).astype(BF16)
    wg = wg_ref[0, 0]
    bg = bg_ref[0, 0]

    ch = []
    for c in range(NCH):
        ci = c + d * (NCH - 1 - 2 * c)
        rows = pl.ds(pl.multiple_of(ci * CHUNK, CHUNK), CHUNK)
        ch.append(dict(rows=rows, z=_dot(glr_ref[rows, :], wg) + bg))
    for t in ch:
        t["la"] = _split3(_log_sigmoid(t["z"]) * (1.0 / GLA_TAU))
    for t in ch:
        hi, mid, lo = t["la"]
        t["bc"] = _dot(tri_b, hi) + (_dot(tri_b, mid) + _dot(tri_b, lo))
    for t in ch:
        rows, bc = t["rows"], t["bc"]
        q = q_ref[rows, :].astype(F32) * (GLA_DK ** -0.5)
        k = k_ref[rows, :].astype(F32)
        bl = jnp.where(fwd, bc[CHUNK - 1:CHUNK], bc[0:1])
        t["qd"] = (q * jnp.exp(bc)).astype(BF16)
        t["ki"] = (k * jnp.exp(-bc)).astype(BF16)
        t["ke"] = (k * jnp.exp(bl - bc)).astype(BF16)
        t["dec_t"] = jnp.transpose(jnp.broadcast_to(jnp.exp(bl), (LANE, GLA_HEADS * GLA_DK)))
        t["v"] = v_ref[rows, :]
    heads = [(slice(h * GLA_DK, (h + 1) * GLA_DK), slice(h * GLA_DV, (h + 1) * GLA_DV))
             for h in range(GLA_HEADS)]
    for t in ch:
        t["a"] = [jnp.where(incl, _dot_nt(t["qd"][:, sk], t["ki"][:, sk]), 0.0).astype(BF16)
                  for sk, _ in heads]
    for t in ch:
        t["oi"] = [_dot(t["a"][h], t["v"][:, sv]) for h, (_, sv) in enumerate(heads)]
        t["upd"] = [_dot_tn(t["ke"][:, sk], t["v"][:, sv]) for sk, sv in heads]
    states = [s_scr[h] for h in range(GLA_HEADS)]
    for t in ch:
        for h, (sk, sv) in enumerate(heads):
            o_ref[0, t["rows"], sv] = t["oi"][h] + _dot(t["qd"][:, sk], states[h].astype(BF16))
            states[h] = t["dec_t"][sk, :] * states[h] + t["upd"][h]
    for h in range(GLA_HEADS):
        s_scr[h] = states[h]

    if not has_s0:
        @pl.when(j == nblk - 1)
        def _():
            sfin_ref[0, 0] = s_scr[...]


def _gla(proj, wg_p, bg_p, s0, layer, ctx):
    nseq, nblk = (BATCH, SEQ // SEQ_BLK) if ctx else (DEC_BATCH, DEC_SEQ // SEQ_BLK)
    rb = _seq_row_block(ctx, nblk)
    hk = GLA_HEADS * GLA_DK
    hv = GLA_HEADS * GLA_DV
    in_specs = [pl.BlockSpec((SEQ_BLK, hk), lambda d, s, j: (rb(d, s, j), PC_GQ // hk)),
                pl.BlockSpec((SEQ_BLK, hk), lambda d, s, j: (rb(d, s, j), PC_GK // hk)),
                pl.BlockSpec((SEQ_BLK, hv), lambda d, s, j: (rb(d, s, j), PC_GV // hv)),
                pl.BlockSpec((SEQ_BLK, LANE), lambda d, s, j: (rb(d, s, j), PC_GLR // LANE)),
                pl.BlockSpec((1, 1, LANE, hk), lambda d, s, j: (layer, d, 0, 0)),
                pl.BlockSpec((1, 1, 1, hk), lambda d, s, j: (layer, d, 0, 0))]
    args = [proj, proj, proj, proj, wg_p, bg_p]
    nrows = NCTX if ctx else NSMP
    roff = 0 if ctx else NCTX // SEQ_BLK
    o_spec = pl.BlockSpec((1, SEQ_BLK, hv), lambda d, s, j: (d, rb(d, s, j) - roff, 0))
    o_shape = jax.ShapeDtypeStruct((2, nrows, hv), F32)
    st_blk = (GLA_HEADS, GLA_DK, GLA_DV)
    if ctx:
        out_specs = [o_spec, pl.BlockSpec((1, 1) + st_blk, lambda d, s, j: (s, d, 0, 0, 0))]
        out_shape = [o_shape, jax.ShapeDtypeStruct((BATCH, 2) + st_blk, F32)]
    else:
        in_specs.append(pl.BlockSpec((1, 1, 1) + st_blk, lambda d, s, j: (s, layer, d, 0, 0, 0)))
        args.append(s0)
        out_specs, out_shape = o_spec, o_shape
    return pl.pallas_call(
        functools.partial(_gla_kernel, nblk=nblk, has_s0=not ctx),
        grid=(2, nseq, nblk),
        in_specs=in_specs, out_specs=out_specs, out_shape=out_shape,
        scratch_shapes=[pltpu.VMEM(st_blk, F32)],
        compiler_params=_cp(("arbitrary", "arbitrary", "arbitrary")),
        name="gla_ctx" if ctx else "gla_smp",
    )(*args)


GDN_G = 2
GDN_GW = GDN_G * GDN_DK
HALO = 16


def _gdn_pre_kernel(x_ref, xp_ref, xn_ref, cw_ref, o_ref, *, blocks_per_seq_ctx, blocks_per_seq_smp):
    i = pl.program_id(0)
    nctx = NCTX // SEQ_BLK
    bi = jnp.where(i < nctx, i % blocks_per_seq_ctx, (i - nctx) % blocks_per_seq_smp)
    nblk = jnp.where(i < nctx, blocks_per_seq_ctx, blocks_per_seq_smp)
    hk = GDN_HEADS * GDN_DK
    x = x_ref[...].astype(F32)
    rowi = lax.broadcasted_iota(jnp.int32, x.shape, 0)
    prev_row = jnp.where(bi > 0, xp_ref[...].astype(F32)[HALO - 1:HALO], 0.0)
    next_row = jnp.where(bi < nblk - 1, xn_ref[...].astype(F32)[0:1], 0.0)
    x_prev = jnp.where(rowi == 0, prev_row, pltpu.roll(x, 1, 0))
    x_next = jnp.where(rowi == SEQ_BLK - 1, next_row, pltpu.roll(x, SEQ_BLK - 1, 0))
    cw = cw_ref[0]
    y = _silu(x_prev * cw[0:1] + x * cw[1:2] + x_next * cw[2:3])

    lane = lax.broadcasted_iota(jnp.int32, (SEQ_BLK, LANE), 1)
    lo_half = lane < GDN_DK

    def l2n(t, scale):
        parts = []
        for c in range(hk // LANE):
            tc = t[:, c * LANE:(c + 1) * LANE]
            sq = tc * tc
            s_lo = jnp.sum(jnp.where(lo_half, sq, 0.0), axis=-1, keepdims=True)
            s_hi = jnp.sum(jnp.where(lo_half, 0.0, sq), axis=-1, keepdims=True)
            parts.append(tc * (lax.rsqrt(jnp.where(lo_half, s_lo, s_hi) + EPS) * scale))
        return jnp.concatenate(parts, axis=1)

    o_ref[:, :hk] = l2n(y[:, :hk], GDN_DK ** -0.5).astype(BF16)
    o_ref[:, hk:2 * hk] = l2n(y[:, hk:2 * hk], 1.0).astype(BF16)
    o_ref[:, 2 * hk:] = y[:, 2 * hk:].astype(BF16)


def _gdn_pre(proj, conv_w, layer):
    cw = 2 * GDN_HEADS * GDN_DK + GDN_HEADS * GDN_DV
    halo = SEQ_BLK // HALO
    last_h = R // HALO - 1
    return pl.pallas_call(
        functools.partial(_gdn_pre_kernel, blocks_per_seq_ctx=SEQ // SEQ_BLK,
                          blocks_per_seq_smp=DEC_SEQ // SEQ_BLK),
        grid=(R // SEQ_BLK,),
        in_specs=[pl.BlockSpec((SEQ_BLK, cw), lambda i: (i, PC_DQKV // cw)),
                  pl.BlockSpec((HALO, cw), lambda i: (jnp.maximum(i * halo - 1, 0), PC_DQKV // cw)),
                  pl.BlockSpec((HALO, cw), lambda i: (jnp.minimum((i + 1) * halo, last_h), PC_DQKV // cw)),
                  pl.BlockSpec((1, 3, cw), lambda i: (layer, 0, 0))],
        out_specs=pl.BlockSpec((SEQ_BLK, cw), lambda i: (i, 0)),
        out_shape=jax.ShapeDtypeStruct((R, cw), BF16),
        compiler_params=_cp(("arbitrary",)),
        name="gdn_pre",
    )(proj, proj, proj, conv_w)


def _gdn_kernel(*refs, nblk, has_s0):
    per_dir = [refs[0:4], refs[4:8]]
    alog_ref, dtb_ref = refs[8:10]
    if has_s0:
        s0_ref, of_ref, ob_ref, s_scr = refs[10:]
    else:
        of_ref, ob_ref, sfin_ref, s_scr = refs[10:]
    o_refs = [of_ref, ob_ref]
    j = pl.program_id(1)
    ngrp = GDN_HEADS // GDN_G

    def head_block(h):
        g, hh = divmod(h, GDN_G)
        return g, slice(hh * GDN_DK, (hh + 1) * GDN_DK), slice(hh * GDN_DV, (hh + 1) * GDN_DV)

    @pl.when(j == 0)
    def _():
        s_scr[...] = jnp.zeros_like(s_scr)
        if has_s0:
            for dd in range(2):
                for h in range(GDN_HEADS):
                    g, rk, rv = head_block(h)
                    s_scr[dd, g, rk, rv] = s0_ref[0, 0, dd, h]

    row = lax.broadcasted_iota(jnp.int32, (CHUNK, GDN_GW), 0)
    col = lax.broadcasted_iota(jnp.int32, (CHUNK, GDN_GW), 1) % CHUNK
    r64 = lax.broadcasted_iota(jnp.int32, (CHUNK, CHUNK), 0)
    c64 = lax.broadcasted_iota(jnp.int32, (CHUNK, CHUNK), 1)
    dir_masks = []
    for dd in range(2):
        lag = (row - col) if dd == 0 else (col - row)
        tri = (c64 <= r64) if dd == 0 else (c64 >= r64)
        dir_masks.append(dict(incl=lag >= 0, strict=lag > 0,
                              tri_b=jnp.where(tri, 1.0, 0.0).astype(BF16)))
    eye = col == row
    eye_f = jnp.where(eye, 1.0, 0.0)
    blk8 = (row // 8) == (col // 8)
    blk16 = (row // 16) == (col // 16)
    blk32 = (row // 32) == (col // 32)
    brow = lax.broadcasted_iota(jnp.int32, (GDN_GW, GDN_GW), 0) // CHUNK
    bcol = lax.broadcasted_iota(jnp.int32, (GDN_GW, GDN_GW), 1) // CHUNK
    same_head = brow == bcol

    def bdiag(t):
        tb = t.astype(BF16)
        return jnp.where(same_head, jnp.concatenate([tb] * GDN_G, axis=0), jnp.zeros((), BF16))

    def bdmm(a, b):
        return _dot(a.astype(BF16), bdiag(b))

    def setup(dd, c):
        q_ref, k_ref, v_ref, ab_ref = per_dir[dd]
        mk = dir_masks[dd]
        ci = c if dd == 0 else NCH - 1 - c
        rows = slice(ci * CHUNK, (ci + 1) * CHUNK)
        ab = ab_ref[rows, :].astype(F32)
        neg_a = -jnp.exp(alog_ref[0, dd])
        gam = _cumsum_rows(mk["tri_b"], neg_a * _softplus(ab + dtb_ref[0, dd]))
        beta_all = jax.nn.sigmoid(ab)
        out = []
        for g in range(ngrp):
            gl = slice(g * GDN_GW, (g + 1) * GDN_GW)
            gcol = jnp.concatenate(
                [jnp.broadcast_to(gam[:, g * GDN_G + h:g * GDN_G + h + 1], (CHUNK, GDN_DK))
                 for h in range(GDN_G)], axis=1)
            bcol_ = jnp.concatenate(
                [jnp.broadcast_to(beta_all[:, GDN_HEADS + g * GDN_G + h:GDN_HEADS + g * GDN_G + h + 1],
                                  (CHUNK, GDN_DK)) for h in range(GDN_G)], axis=1)
            grow = jnp.sum(jnp.where(eye, gcol, 0.0), axis=0, keepdims=True)
            glast = gcol[CHUNK - 1:CHUNK] if dd == 0 else gcol[0:1]
            decay = jnp.where(mk["incl"], jnp.exp(gcol - grow), 0.0)
            egc = jnp.exp(gcol)
            qg = q_ref[rows, gl].astype(F32)
            kg = k_ref[rows, gl].astype(F32)
            vg = v_ref[rows, gl].astype(F32)
            kq = _dot_nt(jnp.concatenate([k_ref[rows, gl], q_ref[rows, gl]], axis=0),
                         bdiag(k_ref[rows, gl]))
            out.append(dict(dd=dd, g=g, rows=rows, gl=gl,
                            m=jnp.where(mk["strict"], bcol_ * kq[:CHUNK] * decay, 0.0),
                            aqk=(kq[CHUNK:] * decay).astype(BF16),
                            vb=vg * bcol_, kb=kg * (bcol_ * egc), qe=qg * egc,
                            kend=(kg * jnp.exp(glast - gcol)).astype(BF16),
                            eg=jnp.exp(glast)))
        return out

    def stages(ch):
        def neumann_a():
            for t in ch:
                t["n8"] = jnp.where(blk8, t["m"], 0.0)
                t["n2"] = bdmm(t["n8"], t["n8"])

        def neumann_b():
            for t in ch:
                p1 = eye_f - t["n8"]
                t["n4"] = bdmm(t["n2"], t["n2"])
                t["p1"] = p1 + bdmm(p1, t["n2"])

        def neumann_c():
            for t in ch:
                t["dinv"] = t["p1"] + bdmm(t["p1"], t["n4"])

        def double_a(off):
            def run():
                for t in ch:
                    t["dl"] = bdmm(t["dinv"], jnp.where(off, t["m"], 0.0))
            return run

        def double_b():
            for t in ch:
                t["dinv"] = t["dinv"] - bdmm(t["dl"], t["dinv"])

        def solve():
            for t in ch:
                t["u"] = bdmm(t["dinv"], t["vb"])
                t["w"] = bdmm(t["dinv"], t["kb"])

        def fold():
            for t in ch:
                wu = jnp.concatenate([t["w"], t["u"]], axis=1).astype(BF16)
                cb = _dot_tn(t["kend"], wu)
                t["c"] = jnp.where(same_head, cb[:, :GDN_GW], 0.0).astype(BF16)
                t["b"] = jnp.where(same_head, cb[:, GDN_GW:], 0.0)
                ao = _dot(t["aqk"], jnp.concatenate([bdiag(t["w"]), bdiag(t["u"])], axis=1))
                t["qt"] = (t["qe"] - ao[:, :GDN_GW]).astype(BF16)
                t["o"] = ao[:, GDN_GW:]

        out = [neumann_a, neumann_b, neumann_c]
        for inner, outer in ((blk8, blk16), (blk16, blk32), (blk32, None)):
            off = jnp.logical_not(inner) if outer is None else jnp.logical_and(outer, jnp.logical_not(inner))
            out += [double_a(off), double_b]
        return out + [solve, fold]

    states = [[s_scr[dd, g] for g in range(ngrp)] for dd in range(2)]

    def recur(cur):
        ys = [_dot(jnp.concatenate([t["qt"], t["c"]], axis=0), states[t["dd"]][t["g"]].astype(BF16))
              for t in cur]
        for t, y in zip(cur, ys):
            o_refs[t["dd"]][t["rows"], t["gl"]] = y[:CHUNK] + t["o"]
            states[t["dd"]][t["g"]] = t["eg"] * states[t["dd"]][t["g"]] - y[CHUNK:] + t["b"]

    half = NCH // 2
    chains = {}
    for c in range(half):
        for dd in range(2):
            chains[(dd, c)] = setup(dd, c)
    later = [(dd, c) for c in range(half, NCH) for dd in range(2)]
    wave_a = [t for c in range(half) for dd in range(2) for t in chains[(dd, c)]]
    for k, stage in enumerate(stages(wave_a)):
        stage()
        if k % 2 == 0 and later:
            key = later.pop(0)
            chains[key] = setup(*key)
    for key in later:
        chains[key] = setup(*key)
    wave_b = [t for c in range(half, NCH) for dd in range(2) for t in chains[(dd, c)]]
    done = 0
    for k, stage in enumerate(stages(wave_b)):
        stage()
        if k % 3 == 1 and done < half:
            recur(chains[(0, done)] + chains[(1, done)])
            done += 1
    for c in range(done, NCH):
        recur(chains[(0, c)] + chains[(1, c)])
    for dd in range(2):
        for g in range(ngrp):
            s_scr[dd, g] = states[dd][g]

    if not has_s0:
        @pl.when(j == nblk - 1)
        def _():
            for dd in range(2):
                for h in range(GDN_HEADS):
                    g, rk, rv = head_block(h)
                    sfin_ref[0, dd, h] = s_scr[dd, g, rk, rv]


def _gdn(qkv, proj, alog_p, dtb_p, s0, layer, ctx):
    nseq, nblk = (BATCH, SEQ // SEQ_BLK) if ctx else (DEC_BATCH, DEC_SEQ // SEQ_BLK)
    roff = 0 if ctx else NCTX // SEQ_BLK
    hk = GDN_HEADS * GDN_DK
    hv = GDN_HEADS * GDN_DV

    def rb(dd):
        return lambda s, j: roff + s * nblk + (j if dd == 0 else nblk - 1 - j)

    in_specs, args = [], []
    for dd in range(2):
        r = rb(dd)
        in_specs += [pl.BlockSpec((SEQ_BLK, hk), lambda s, j, r=r: (r(s, j), 0)),
                     pl.BlockSpec((SEQ_BLK, hk), lambda s, j, r=r: (r(s, j), 1)),
                     pl.BlockSpec((SEQ_BLK, hv), lambda s, j, r=r: (r(s, j), 2 * hk // hv)),
                     pl.BlockSpec((SEQ_BLK, LANE), lambda s, j, r=r, dd=dd: (r(s, j), PC_DAB // LANE + dd))]
        args += [qkv, qkv, qkv, proj]
    in_specs += [pl.BlockSpec((1, 2, 1, LANE), lambda s, j: (layer, 0, 0, 0))] * 2
    args += [alog_p, dtb_p]
    nrows = NCTX if ctx else NSMP
    out_specs = [pl.BlockSpec((SEQ_BLK, hv), lambda s, j, r=rb(dd): (r(s, j) - roff, 0)) for dd in range(2)]
    out_shape = [jax.ShapeDtypeStruct((nrows, hv), F32)] * 2
    st_blk = (2, GDN_HEADS, GDN_DK, GDN_DV)
    if ctx:
        out_specs.append(pl.BlockSpec((1,) + st_blk, lambda s, j: (s, 0, 0, 0, 0)))
        out_shape.append(jax.ShapeDtypeStruct((BATCH,) + st_blk, F32))
    else:
        in_specs.append(pl.BlockSpec((1, 1) + st_blk, lambda s, j: (s, layer, 0, 0, 0, 0)))
        args.append(s0)
    return pl.pallas_call(
        functools.partial(_gdn_kernel, nblk=nblk, has_s0=not ctx),
        grid=(nseq, nblk),
        in_specs=in_specs, out_specs=out_specs, out_shape=out_shape,
        scratch_shapes=[pltpu.VMEM((2, GDN_HEADS // GDN_G, GDN_GW, GDN_GW), F32)],
        compiler_params=_cp(("arbitrary", "arbitrary")),
        name="gdn_ctx" if ctx else "gdn_smp",
    )(*args)


def _group_rms(x, width):
    parts = []
    lane = lax.broadcasted_iota(jnp.int32, (x.shape[0], LANE), 1)
    lo_half = lane < 64
    for c in range(x.shape[1] // LANE):
        xc = x[:, c * LANE:(c + 1) * LANE]
        sq = xc * xc
        if width == LANE:
            ms = jnp.mean(sq, axis=-1, keepdims=True)
        else:
            s_lo = jnp.sum(jnp.where(lo_half, sq, 0.0), axis=-1, keepdims=True)
            s_hi = jnp.sum(jnp.where(lo_half, 0.0, sq), axis=-1, keepdims=True)
            ms = jnp.where(lo_half, s_lo, s_hi) * (1.0 / width)
        parts.append(xc * lax.rsqrt(ms + EPS))
    return jnp.concatenate(parts, axis=1)


def _merge_kernel(ymc_ref, yms_ref, ogc_ref, ogs_ref, gr_ref, odcf_ref, odcb_ref, odsf_ref, odsb_ref,
                  dz_ref, gt_ref, *rest, split_h):
    if split_h:
        hc_ref, hs_ref, *rest = rest
        h_rows = lambda: _family_rows((hc_ref, hs_ref), TM_MERGE)
    else:
        h_ref, *rest = rest
        h_rows = lambda: h_ref[...]
    ga_ref, gn_ref, dn_ref, wb_ref, wo_ref, o_ref, ym_scr, og_scr, od_scr = rest
    is_ctx = pl.program_id(0) < NCTX // TM_MERGE

    @pl.when(is_ctx)
    def _():
        ym_scr[...] = ymc_ref[...]
        og_scr[...] = ogc_ref[0] + ogc_ref[1]
        od_scr[...] = odcf_ref[...] + odcb_ref[...]

    @pl.when(jnp.logical_not(is_ctx))
    def _():
        ym_scr[...] = yms_ref[...]
        og_scr[...] = ogs_ref[0] + ogs_ref[1]
        od_scr[...] = odsf_ref[...] + odsb_ref[...]

    y_gla = _group_rms(og_scr[...], GLA_DV) * gn_ref[0] * _silu(gr_ref[...].astype(F32))
    y_gdn = _group_rms(od_scr[...], GDN_DV) * dn_ref[0] * _silu(dz_ref[...].astype(F32))
    gates = jax.nn.sigmoid(gt_ref[...].astype(F32))
    d = D_MODEL
    acc = gates[:, :d] * _dot(ym_scr[...], wb_ref[0, 0])
    acc = acc + gates[:, d:2 * d] * _dot(y_gla.astype(BF16), wb_ref[0, 1])
    acc = acc + gates[:, 2 * d:] * _dot(y_gdn.astype(BF16), wb_ref[0, 2])
    y = _dot(acc.astype(BF16), wo_ref[0])
    o_ref[...] = h_rows() + ga_ref[0] * y


def _merge(y_mla, o_gla, o_gdn, proj, h, mod3, gla_norm_p, gdn_norm_p, w_branch_b, w_out_b, layer):
    tm = TM_MERGE
    bw = 512
    nc = NCTX // tm
    split_h = isinstance(h, tuple)
    h_specs = _family_specs(tm, D_MODEL) if split_h else [pl.BlockSpec((tm, D_MODEL), lambda i: (i, 0))]
    h_args = list(h) if split_h else [h]

    def cmap(i):
        return jnp.minimum(i, nc - 1)

    def smap(i):
        return jnp.maximum(i - nc, 0)

    return pl.pallas_call(
        functools.partial(_merge_kernel, split_h=split_h),
        grid=(R // tm,),
        in_specs=[pl.BlockSpec((tm, bw), lambda i: (cmap(i), 0)),
                  pl.BlockSpec((tm, bw), lambda i: (smap(i), 0)),
                  pl.BlockSpec((2, tm, bw), lambda i: (0, cmap(i), 0)),
                  pl.BlockSpec((2, tm, bw), lambda i: (0, smap(i), 0)),
                  pl.BlockSpec((tm, bw), lambda i: (i, PC_GR // bw)),
                  pl.BlockSpec((tm, bw), lambda i: (cmap(i), 0)),
                  pl.BlockSpec((tm, bw), lambda i: (cmap(i), 0)),
                  pl.BlockSpec((tm, bw), lambda i: (smap(i), 0)),
                  pl.BlockSpec((tm, bw), lambda i: (smap(i), 0)),
                  pl.BlockSpec((tm, bw), lambda i: (i, PC_DZ // bw)),
                  pl.BlockSpec((tm, 3 * D_MODEL), lambda i: (i, PC_GATES // (3 * D_MODEL)))] + h_specs + [
                  _mod_spec(layer, 2, tm),
                  pl.BlockSpec((1, 1, bw), lambda i: (layer, 0, 0)),
                  pl.BlockSpec((1, 1, bw), lambda i: (layer, 0, 0)),
                  pl.BlockSpec((1, 3, bw, D_MODEL), lambda i: (layer, 0, 0, 0)),
                  pl.BlockSpec((1, D_MODEL, D_MODEL), lambda i: (layer, 0, 0))],
        out_specs=pl.BlockSpec((tm, D_MODEL), lambda i: (i, 0)),
        out_shape=jax.ShapeDtypeStruct((R, D_MODEL), F32),
        scratch_shapes=[pltpu.VMEM((tm, bw), BF16), pltpu.VMEM((tm, bw), F32), pltpu.VMEM((tm, bw), F32)],
        compiler_params=_cp(("arbitrary",)),
        name="merge",
    )(y_mla[0], y_mla[1], o_gla[0], o_gla[1], proj, o_gdn[0][0], o_gdn[0][1], o_gdn[1][0], o_gdn[1][1],
      proj, proj, *h_args, mod3, gla_norm_p, gdn_norm_p, w_branch_b, w_out_b)


def _ffn_kernel(h_ref, g_ref, sc_ref, sh_ref, gf_ref, wg_ref, wu_ref, wo_ref, fn_ref, *rest, final):
    if final:
        oc_ref, os_ref, xf_ref, acc_ref = rest
    else:
        o_ref, xf_ref, acc_ref = rest
    f = pl.program_id(1)

    @pl.when(f == 0)
    def _():
        y = _rms(h_ref[...], g_ref[0])
        xf_ref[...] = (y * (1.0 + sc_ref[0]) + sh_ref[0]).astype(BF16)
        acc_ref[...] = jnp.zeros_like(acc_ref)

    xf = xf_ref[...]
    a = _silu(_dot(xf, wg_ref[0])) * _dot(xf, wu_ref[0])
    acc_ref[...] += _dot(a.astype(BF16), wo_ref[0])

    @pl.when(f == pl.num_programs(1) - 1)
    def _():
        out = h_ref[...] + gf_ref[0] * acc_ref[...]
        if not final:
            o_ref[...] = out
        else:
            out = _rms(out, fn_ref[...])
            is_ctx = pl.program_id(0) < NCTX // TM_FFN

            @pl.when(is_ctx)
            def _():
                oc_ref[...] = out

            @pl.when(jnp.logical_not(is_ctx))
            def _():
                os_ref[...] = out


def _ffn(h, mod3, norm_ffn, wg_b, wu_b, wo_b, final_norm, layer, final):
    tm, tf = TM_FFN, TF_FFN
    if final:
        nc = NCTX // tm
        out_specs = [pl.BlockSpec((tm, D_MODEL), lambda i, f: (jnp.minimum(i, nc - 1), 0)),
                     pl.BlockSpec((tm, D_MODEL), lambda i, f: (jnp.maximum(i - nc, 0), 0))]
        out_shape = [jax.ShapeDtypeStruct((NCTX, D_MODEL), F32), jax.ShapeDtypeStruct((NSMP, D_MODEL), F32)]
    else:
        out_specs = pl.BlockSpec((tm, D_MODEL), lambda i, f: (i, 0))
        out_shape = jax.ShapeDtypeStruct((R, D_MODEL), F32)
    return pl.pallas_call(
        functools.partial(_ffn_kernel, final=final),
        grid=(R // tm, D_FF // tf),
        in_specs=[pl.BlockSpec((tm, D_MODEL), lambda i, f: (i, 0)),
                  pl.BlockSpec((1, 1, D_MODEL), lambda i, f: (layer, 0, 0)),
                  _mod_spec(layer, 4, tm),
                  _mod_spec(layer, 3, tm),
                  _mod_spec(layer, 5, tm),
                  pl.BlockSpec((1, D_MODEL, tf), lambda i, f: (layer, 0, f)),
                  pl.BlockSpec((1, D_MODEL, tf), lambda i, f: (layer, 0, f)),
                  pl.BlockSpec((1, tf, D_MODEL), lambda i, f: (layer, f, 0)),
                  pl.BlockSpec((1, D_MODEL), lambda i, f: (0, 0))],
        out_specs=out_specs, out_shape=out_shape,
        scratch_shapes=[pltpu.VMEM((tm, D_MODEL), BF16), pltpu.VMEM((tm, D_MODEL), F32)],
        compiler_params=_cp(("arbitrary", "arbitrary")),
        name="ffn_final" if final else "ffn",
    )(h, norm_ffn.reshape(DEPTH, 1, D_MODEL), mod3, mod3, mod3, wg_b, wu_b, wo_b,
      final_norm.reshape(1, D_MODEL))


def _rope_partner(w):
    h = AXIS_DIM // 2
    parts = []
    for a in range(2):
        x1 = w[..., a * AXIS_DIM:a * AXIS_DIM + h]
        x2 = w[..., a * AXIS_DIM + h:(a + 1) * AXIS_DIM]
        parts += [-x2, x1]
    return jnp.concatenate(parts, axis=-1)


def _pack_w_in(w_in):
    c = [0]
    for s in (Q_LORA, KV_LORA + ROPE_DIM, 256, 256, 512, 512, 2 * GLA_RANK, 1536, 512,
              2 * GDN_HEADS, 2 * GDN_HEADS, 3 * D_MODEL):
        c.append(c[-1] + s)
    seg = [w_in[..., c[i]:c[i + 1]] for i in range(12)]
    mq, mkv, gq, gk, gv, gr, glr, dqkv, dz, da, db, gates = seg

    def zeros(n):
        return jnp.zeros(w_in.shape[:-1] + (n,), w_in.dtype)

    kr = mkv[..., KV_LORA:]
    mla = jnp.concatenate([mq, mkv[..., :KV_LORA], kr, _rope_partner(kr), zeros(64)], axis=-1)
    dab = []
    for d in range(2):
        dab += [da[..., d * GDN_HEADS:(d + 1) * GDN_HEADS], db[..., d * GDN_HEADS:(d + 1) * GDN_HEADS],
                zeros(LANE - 2 * GDN_HEADS)]
    packed = jnp.concatenate([gates, dqkv, dz, gv, gr, mla, gq, gk, glr, zeros(LANE - 2 * GLA_RANK)]
                             + dab + [zeros(PROJ_W - PC_DAB - 2 * LANE)], axis=-1)
    return packed.astype(BF16)


def _pack_mla_weights(w_uq, w_ukv):
    l = w_uq.shape[0]
    qh = w_uq.reshape(l, Q_LORA, MLA_HEADS, MLA_NOPE + ROPE_DIM)
    zq = jnp.zeros((l, Q_LORA, MLA_HEADS, HEAD_W - MLA_NOPE - ROPE_DIM), w_uq.dtype)
    zn = jnp.zeros((l, Q_LORA, MLA_HEADS, MLA_NOPE), w_uq.dtype)
    wq = jnp.concatenate([qh, zq], axis=-1).reshape(l, Q_LORA, MLA_HEADS * HEAD_W)
    wq_sw = jnp.concatenate([zn, _rope_partner(qh[..., MLA_NOPE:]), zq], axis=-1).reshape(l, Q_LORA, MLA_HEADS * HEAD_W)
    wq_p = jnp.concatenate([wq, wq_sw], axis=-1).astype(BF16)

    kvh = w_ukv.reshape(l, KV_LORA, MLA_HEADS, MLA_NOPE + MLA_V)
    zk = jnp.zeros((l, KV_LORA, MLA_HEADS, HEAD_W - MLA_NOPE), w_ukv.dtype)
    wk_top = jnp.concatenate([kvh[..., :MLA_NOPE], zk], axis=-1).reshape(l, KV_LORA, MLA_HEADS * HEAD_W)
    place = jnp.zeros((KV_LORA, MLA_HEADS, HEAD_W), w_ukv.dtype)
    idx = jnp.arange(ROPE_DIM)
    for rep in range(2):
        place = place.at[rep * ROPE_DIM + idx, :, MLA_NOPE + idx].set(1.0)
    wk_bot = jnp.broadcast_to(place.reshape(1, KV_LORA, MLA_HEADS * HEAD_W), (l, KV_LORA, MLA_HEADS * HEAD_W))
    wk_p = jnp.concatenate([wk_top, wk_bot], axis=1).astype(BF16)
    wv_p = jnp.transpose(kvh[..., MLA_NOPE:], (0, 2, 3, 1)).reshape(l, VT_ROWS, KV_LORA).astype(BF16)
    return wq_p, wk_p, wv_p


def _rope_tables(tm):
    rows = DEC_SEQ // GRID_W
    row = jnp.repeat(jnp.arange(rows, dtype=F32), GRID_W)
    col = jnp.tile(jnp.arange(GRID_W, dtype=F32), rows)
    inv = ROPE_THETA ** (-jnp.arange(0, AXIS_DIM, 2, dtype=F32) / AXIS_DIM)
    ang_r, ang_c = row[:, None] * inv, col[:, None] * inv
    cos32 = jnp.concatenate([jnp.cos(ang_r)] * 2 + [jnp.cos(ang_c)] * 2, axis=-1)
    sin32 = jnp.concatenate([jnp.sin(ang_r)] * 2 + [jnp.sin(ang_c)] * 2, axis=-1)
    cos32 = jnp.concatenate([jnp.ones((tm, ROPE_DIM), F32), cos32], axis=0)
    sin32 = jnp.concatenate([jnp.zeros((tm, ROPE_DIM), F32), sin32], axis=0)
    n = cos32.shape[0]
    scale = (MLA_NOPE + ROPE_DIM) ** -0.5
    pad = HEAD_W - MLA_NOPE - ROPE_DIM
    cq = jnp.concatenate([jnp.ones((n, MLA_NOPE), F32), cos32, jnp.zeros((n, pad), F32)], axis=-1) * scale
    sq = jnp.concatenate([jnp.zeros((n, MLA_NOPE), F32), sin32, jnp.zeros((n, pad), F32)], axis=-1) * scale
    ck = jnp.concatenate([cos32, sin32, jnp.zeros((n, HEAD_W - 2 * ROPE_DIM), F32)], axis=-1)
    return jnp.concatenate([cq, sq, ck], axis=-1)


def _pad_lanes(x, n):
    return jnp.pad(x, [(0, 0)] * (x.ndim - 1) + [(0, n - x.shape[-1])])


def kernel(x_prompt, x_sample, cache_mla, state_gla, state_gdn, c, c_ctx, w_mod, b_mod, norm_mix, w_in,
           mla_q_norm, mla_w_uq, mla_kv_norm, mla_w_ukv, gla_w_gate, gla_b_gate, gla_norm, gdn_conv,
           gdn_a_log, gdn_dt_bias, gdn_norm, w_branch, w_out, norm_ffn, ffn_w_in, ffn_w_out, final_norm):
    w_in_p = _pack_w_in(w_in)
    wq_p, wk_p, wv_p = _pack_mla_weights(mla_w_uq, mla_w_ukv)
    tab = _rope_tables(TM_PREP)
    wg_p = jnp.zeros((DEPTH, 2, LANE, GLA_HEADS * GLA_DK), F32)
    for d in range(2):
        wg_p = wg_p.at[:, d, d * GLA_RANK:(d + 1) * GLA_RANK, :].set(gla_w_gate[:, d])
    wg_p = wg_p.astype(BF16)
    bg_p = gla_b_gate.reshape(DEPTH, 2, 1, GLA_HEADS * GLA_DK)
    alog_p = _pad_lanes(gdn_a_log, LANE).reshape(DEPTH, 2, 1, LANE)
    dtb_p = _pad_lanes(gdn_dt_bias, LANE).reshape(DEPTH, 2, 1, LANE)
    gla_norm_p = jnp.tile(gla_norm, (1, GLA_HEADS)).reshape(DEPTH, 1, GLA_HEADS * GLA_DV)
    gdn_norm_p = jnp.tile(gdn_norm, (1, GDN_HEADS)).reshape(DEPTH, 1, GDN_HEADS * GDN_DV)
    w_branch_b = w_branch.astype(BF16)
    w_out_b = w_out.astype(BF16)
    wg_b = ffn_w_in[..., :D_FF].astype(BF16)
    wu_b = ffn_w_in[..., D_FF:].astype(BF16)
    wo_b = ffn_w_out.astype(BF16)
    cache_p = _pad_lanes(cache_mla, 2 * KV_LORA)
    cond8 = jnp.concatenate([c_ctx[None, :], c, jnp.zeros((MOD_ROWS - 1 - DEC_BATCH, D_MODEL), F32)], axis=0)

    mod = _modulation(cond8, w_mod, b_mod)
    mod3 = mod.reshape(DEPTH * MOD_ROWS * 6, 1, D_MODEL)
    kc, vc = _kv_cache(cache_p, wk_p, wv_p)

    h = (x_prompt.reshape(NCTX, D_MODEL), x_sample.reshape(NSMP, D_MODEL))
    kv_list, gla_list, gdn_list = [], [], []
    for l in range(DEPTH):
        proj = _inproj(h, mod3, norm_mix, w_in_p, l)
        qp, kp, vtp, own = _mla_prep(proj, tab, mla_q_norm, mla_kv_norm, wq_p, wk_p, wv_p, l)
        y_mla = (_attention_ctx(qp, kp, vtp), _attention_smp(qp, kp, vtp, kc, vc, l))
        og_c, sg = _gla(proj, wg_p, bg_p, None, l, True)
        og_s = _gla(proj, wg_p, bg_p, state_gla, l, False)
        qkv = _gdn_pre(proj, gdn_conv, l)
        odf_c, odb_c, sd = _gdn(qkv, proj, alog_p, dtb_p, None, l, True)
        odf_s, odb_s = _gdn(qkv, proj, alog_p, dtb_p, state_gdn, l, False)
        h = _merge(y_mla, (og_c, og_s), ((odf_c, odb_c), (odf_s, odb_s)), proj, h, mod3, gla_norm_p, gdn_norm_p,
                   w_branch_b, w_out_b, l)
        h = _ffn(h, mod3, norm_ffn, wg_b, wu_b, wo_b, final_norm, l, l == DEPTH - 1)
        kv_list.append(own[:NCTX, :KV_LORA + ROPE_DIM].reshape(BATCH, SEQ, KV_LORA + ROPE_DIM))
        gla_list.append(sg)
        gdn_list.append(sd)

    y_prompt = h[0].reshape(BATCH, SEQ, D_MODEL)
    y_sample = h[1].reshape(DEC_BATCH, DEC_SEQ, D_MODEL)
    return (y_prompt, y_sample, jnp.stack(kv_list, axis=1), jnp.stack(gla_list, axis=1),
            jnp.stack(gdn_list, axis=1))
```

```python
import functools
import math

import jax
import jax.numpy as jnp
from jax import lax
from jax.experimental import pallas as pl
from jax.experimental.pallas import tpu as pltpu

F32 = jnp.float32
BF16 = jnp.bfloat16

D_MODEL = 1024
BATCH = 16
SEQ = 256
DEPTH = 2
DEC_BATCH = 2
DEC_SEQ = 4096
PAST_LEN = 256
GRID_W = 64
CHUNK = 64
EPS = 1e-6
MLA_HEADS = 8
MLA_NOPE = 64
ROPE_DIM = 32
AXIS_DIM = ROPE_DIM // 2
MLA_V = 64
Q_LORA = 256
KV_LORA = 128
ROPE_THETA = 10000.0
GLA_HEADS = 4
GLA_DK = 64
GLA_DV = 128
GLA_RANK = 16
GLA_TAU = 16.0
GDN_HEADS = 8
GDN_DK = 64
GDN_DV = 64
D_FF = ((8 * D_MODEL + 3 * 256 - 1) // (3 * 256)) * 256
MOD_W = 6 * D_MODEL

NCTX = BATCH * SEQ
NSMP = DEC_BATCH * DEC_SEQ
R = NCTX + NSMP
LANE = 128
HEAD_W = 128
VT_ROWS = MLA_HEADS * MLA_V
SEQ_BLK = 256
NCH = SEQ_BLK // CHUNK
MOD_ROWS = 8

PC_GATES = 0
PC_DQKV = 3072
PC_DZ = 4608
PC_GV = 5120
PC_GR = 5632
PC_MLA = 6144
PC_GQ = 6656
PC_GK = 6912
PC_GLR = 7168
PC_DAB = 7296
PROJ_W = 7680

TM_IN = 1024
TN_IN = 1536
TM_PREP = 512
TM_MERGE = 512
TM_FFN = 512
TF_FFN = 256
TQ_ATT = 256
TK_ATT = 512
ONES_ROWS = 16
ATT_AHEAD = 8
VMEM_LIMIT = 48 * 1024 * 1024


def _cp(sem):
    return pltpu.CompilerParams(dimension_semantics=sem, vmem_limit_bytes=VMEM_LIMIT)


def _dot(a, b):
    return jnp.dot(a, b, preferred_element_type=F32)


def _dot_nt(a, b):
    return lax.dot_general(a, b, (((1,), (1,)), ((), ())), preferred_element_type=F32)


def _dot_tn(a, b):
    return lax.dot_general(a, b, (((0,), (0,)), ((), ())), preferred_element_type=F32)


def _split2(x):
    hi = x.astype(BF16)
    lo = (x - hi.astype(F32)).astype(BF16)
    return hi, lo


def _split3(x):
    hi = x.astype(BF16)
    r1 = x - hi.astype(F32)
    mid = r1.astype(BF16)
    lo = (r1 - mid.astype(F32)).astype(BF16)
    return hi, mid, lo


def _cumsum_rows(tri_b, x):
    hi, mid, lo = _split3(x)
    return _dot(tri_b, hi) + (_dot(tri_b, mid) + _dot(tri_b, lo))


def _sigmoid(x):
    return 0.5 * jnp.tanh(0.5 * x) + 0.5


def _silu(x):
    return x * _sigmoid(x)


def _softplus(x):
    return jnp.maximum(x, 0.0) + jnp.log(1.0 + jnp.exp(-jnp.abs(x)))


def _log_sigmoid(x):
    return jnp.minimum(x, 0.0) - jnp.log(1.0 + jnp.exp(-jnp.abs(x)))


def _rms(x, g):
    return x * lax.rsqrt(jnp.mean(x * x, axis=-1, keepdims=True) + EPS) * g


def _mod_row(row_start):
    return jnp.where(row_start < NCTX, 0, 1 + (row_start - NCTX) // DEC_SEQ)


def _mod_spec(layer, which, tm, axis=0):
    def imap(*idx):
        return ((layer * MOD_ROWS + _mod_row(idx[axis] * tm)) * 6 + which, 0, 0)
    return pl.BlockSpec((1, 1, D_MODEL), imap)


def _mod_kernel(c_ref, w_ref, b_ref, o_ref):
    x = _silu(c_ref[...])
    xh, xl = _split2(x)
    wh, wl = _split2(w_ref[0])
    o_ref[0] = _dot(xh, wh) + (_dot(xl, wh) + _dot(xh, wl)) + b_ref[0]


def _modulation(cond8, w_mod, b_mod):
    tn = 1536
    return pl.pallas_call(
        _mod_kernel,
        grid=(DEPTH, MOD_W // tn),
        in_specs=[pl.BlockSpec((MOD_ROWS, D_MODEL), lambda l, j: (0, 0)),
                  pl.BlockSpec((1, D_MODEL, tn), lambda l, j: (l, 0, j)),
                  pl.BlockSpec((1, 1, tn), lambda l, j: (l, 0, j))],
        out_specs=pl.BlockSpec((1, MOD_ROWS, tn), lambda l, j: (l, 0, j)),
        out_shape=jax.ShapeDtypeStruct((DEPTH, MOD_ROWS, MOD_W), F32),
        compiler_params=_cp(("arbitrary", "arbitrary")),
        name="modulation",
    )(cond8, w_mod, b_mod.reshape(DEPTH, 1, MOD_W))


def _family_specs(tm, width, axis=0):
    nc = NCTX // tm
    return [pl.BlockSpec((tm, width), lambda *idx: (jnp.minimum(idx[axis], nc - 1), 0)),
            pl.BlockSpec((tm, width), lambda *idx: (jnp.maximum(idx[axis] - nc, 0), 0))]


def _family_rows(refs, tm, axis=0):
    is_ctx = pl.program_id(axis) < NCTX // tm
    return jnp.where(is_ctx, refs[0][...], refs[1][...])


def _inproj_kernel(*refs, split):
    if split:
        hc_ref, hs_ref, g_ref, sc_ref, sh_ref, w_ref, o_ref = refs
        x = _family_rows((hc_ref, hs_ref), TM_IN, axis=1)
    else:
        h_ref, g_ref, sc_ref, sh_ref, w_ref, o_ref = refs
        x = h_ref[...]
    y = _rms(x, g_ref[0])
    xn = (y * (1.0 + sc_ref[0]) + sh_ref[0]).astype(BF16)
    o_ref[...] = _dot(xn, w_ref[0]).astype(BF16)


def _inproj(h, mod3, norm_mix, w_in_p, layer):
    split = isinstance(h, tuple)
    h_specs = (_family_specs(TM_IN, D_MODEL, axis=1) if split
               else [pl.BlockSpec((TM_IN, D_MODEL), lambda j, i: (i, 0))])
    h_args = list(h) if split else [h]
    return pl.pallas_call(
        functools.partial(_inproj_kernel, split=split),
        grid=(PROJ_W // TN_IN, R // TM_IN),
        in_specs=h_specs + [
                  pl.BlockSpec((1, 1, D_MODEL), lambda j, i: (layer, 0, 0)),
                  _mod_spec(layer, 1, TM_IN, axis=1),
                  _mod_spec(layer, 0, TM_IN, axis=1),
                  pl.BlockSpec((1, D_MODEL, TN_IN), lambda j, i: (layer, 0, j))],
        out_specs=pl.BlockSpec((TM_IN, TN_IN), lambda j, i: (i, j)),
        out_shape=jax.ShapeDtypeStruct((R, PROJ_W), BF16),
        compiler_params=_cp(("arbitrary", "arbitrary")),
        name="inproj",
    )(*h_args, norm_mix.reshape(DEPTH, 1, D_MODEL), mod3, mod3, w_in_p)


def _mla_prep_kernel(pm_ref, tab_ref, qn_ref, kvn_ref, wq_ref, wk_ref, wv_ref,
                     q_ref, k_ref, v_ref, own_ref):
    pm = pm_ref[...].astype(F32)
    tab = tab_ref[...]
    qn = _rms(pm[:, :Q_LORA], qn_ref[0]).astype(BF16)
    q2 = _dot(qn, wq_ref[0])
    hw = MLA_HEADS * HEAD_W
    cq = jnp.tile(tab[:, :HEAD_W], (1, MLA_HEADS))
    sq = jnp.tile(tab[:, HEAD_W:2 * HEAD_W], (1, MLA_HEADS))
    q_ref[...] = (q2[:, :hw] * cq + q2[:, hw:] * sq).astype(BF16)
    ckv = _rms(pm[:, Q_LORA:Q_LORA + KV_LORA], kvn_ref[0])
    kr = pm[:, Q_LORA + KV_LORA:] * tab[:, 2 * HEAD_W:]
    lhs = jnp.concatenate([ckv, kr], axis=1)
    own_ref[...] = lhs
    lb = lhs.astype(BF16)
    k_ref[...] = _dot(lb, wk_ref[0]).astype(BF16)
    v_ref[...] = _dot_nt(wv_ref[0], lb[:, :KV_LORA]).astype(BF16)


def _mla_prep(proj, tab, q_norm, kv_norm, wq_p, wk_p, wv_p, layer):
    tm = TM_PREP
    hw = MLA_HEADS * HEAD_W

    def tab_map(i):
        r0 = i * tm
        return (jnp.where(r0 < NCTX, 0, 1 + ((r0 - NCTX) % DEC_SEQ) // tm), 0)

    return pl.pallas_call(
        _mla_prep_kernel,
        grid=(R // tm,),
        in_specs=[pl.BlockSpec((tm, 512), lambda i: (i, PC_MLA // 512)),
                  pl.BlockSpec((tm, 3 * HEAD_W), tab_map),
                  pl.BlockSpec((1, 1, Q_LORA), lambda i: (layer, 0, 0)),
                  pl.BlockSpec((1, 1, KV_LORA), lambda i: (layer, 0, 0)),
                  pl.BlockSpec((1, Q_LORA, 2 * hw), lambda i: (layer, 0, 0)),
                  pl.BlockSpec((1, 2 * KV_LORA, hw), lambda i: (layer, 0, 0)),
                  pl.BlockSpec((1, VT_ROWS, KV_LORA), lambda i: (layer, 0, 0))],
        out_specs=[pl.BlockSpec((tm, hw), lambda i: (i, 0)),
                   pl.BlockSpec((tm, hw), lambda i: (i, 0)),
                   pl.BlockSpec((VT_ROWS, tm), lambda i: (0, i)),
                   pl.BlockSpec((tm, 2 * KV_LORA), lambda i: (i, 0))],
        out_shape=[jax.ShapeDtypeStruct((R, hw), BF16),
                   jax.ShapeDtypeStruct((R, hw), BF16),
                   jax.ShapeDtypeStruct((VT_ROWS, R), BF16),
                   jax.ShapeDtypeStruct((R, 2 * KV_LORA), F32)],
        compiler_params=_cp(("arbitrary",)),
        name="mla_prep",
    )(proj, tab, q_norm.reshape(DEPTH, 1, Q_LORA), kv_norm.reshape(DEPTH, 1, KV_LORA),
      wq_p, wk_p, wv_p)


def _kv_cache_kernel(c_ref, wk_ref, wv_ref, k_ref, v_ref):
    lb = c_ref[0, 0].astype(BF16)
    k_ref[0, 0] = _dot(lb, wk_ref[0]).astype(BF16)
    v_ref[0, 0] = _dot_nt(wv_ref[0], lb[:, :KV_LORA]).astype(BF16)


def _kv_cache(cache_p, wk_p, wv_p):
    hw = MLA_HEADS * HEAD_W
    return pl.pallas_call(
        _kv_cache_kernel,
        grid=(DEC_BATCH, DEPTH),
        in_specs=[pl.BlockSpec((1, 1, PAST_LEN, 2 * KV_LORA), lambda b, l: (b, l, 0, 0)),
                  pl.BlockSpec((1, 2 * KV_LORA, hw), lambda b, l: (l, 0, 0)),
                  pl.BlockSpec((1, VT_ROWS, KV_LORA), lambda b, l: (l, 0, 0))],
        out_specs=[pl.BlockSpec((1, 1, PAST_LEN, hw), lambda b, l: (b, l, 0, 0)),
                   pl.BlockSpec((1, 1, VT_ROWS, PAST_LEN), lambda b, l: (b, l, 0, 0))],
        out_shape=[jax.ShapeDtypeStruct((DEC_BATCH, DEPTH, PAST_LEN, hw), BF16),
                   jax.ShapeDtypeStruct((DEC_BATCH, DEPTH, VT_ROWS, PAST_LEN), BF16)],
        compiler_params=_cp(("arbitrary", "arbitrary")),
        name="kv_cache",
    )(cache_p, wk_p, wv_p)


def _attn_kernel(*refs, heads, has_cache):
    if has_cache:
        q_ref, k_ref, vt_ref, kc_ref, vct_ref, o_ref = refs
    else:
        q_ref, k_ref, vt_ref, o_ref = refs
    n_keys = k_ref.shape[0]
    tk = min(TK_ATT, n_keys)
    tiles = ([("cache", 0, PAST_LEN)] if has_cache else []) + [("own", s0, tk) for s0 in range(0, n_keys, tk)]
    units = [(h, t) for t in range(len(tiles)) for h in range(heads)]
    qs = [q_ref[:, h * HEAD_W:(h + 1) * HEAD_W] for h in range(heads)]

    def scores(h, t):
        kind, s0, n = tiles[t]
        sl = slice(h * HEAD_W, (h + 1) * HEAD_W)
        keys = kc_ref[0, 0, :, sl] if kind == "cache" else k_ref[s0:s0 + n, sl]
        return _dot_nt(keys, qs[h])

    def values_t(h, t):
        kind, s0, n = tiles[t]
        sv = slice(h * MLA_V, (h + 1) * MLA_V)
        vt = vct_ref[0, 0, sv, :] if kind == "cache" else vt_ref[sv, s0:s0 + n]
        return jnp.concatenate([vt, jnp.ones((ONES_ROWS, n), BF16)], axis=0)

    ahead = ATT_AHEAD
    pending = {u: scores(*u) for u in units[:ahead]}
    m, acc = [None] * heads, [None] * heads
    for i, (h, t) in enumerate(units):
        if i + ahead < len(units):
            pending[units[i + ahead]] = scores(*units[i + ahead])
        st = pending.pop((h, t))
        mt = jnp.max(st, axis=0, keepdims=True)
        if m[h] is None:
            m[h] = mt
            acc[h] = _dot(values_t(h, t), jnp.exp2(st - mt).astype(BF16))
        else:
            m_new = jnp.maximum(m[h], mt)
            alpha = jnp.exp2(m[h] - m_new)
            acc[h] = alpha * acc[h] + _dot(values_t(h, t), jnp.exp2(st - m_new).astype(BF16))
            m[h] = m_new
    outs = [acc[h][:MLA_V] / acc[h][MLA_V:MLA_V + 1] for h in range(heads)]
    o_ref[...] = jnp.concatenate(outs, axis=0).T.astype(BF16)


def _attention_ctx(qp, kp, vtp):
    hw = MLA_HEADS * HEAD_W
    return pl.pallas_call(
        functools.partial(_attn_kernel, heads=MLA_HEADS, has_cache=False),
        grid=(BATCH,),
        in_specs=[pl.BlockSpec((SEQ, hw), lambda s: (s, 0)),
                  pl.BlockSpec((SEQ, hw), lambda s: (s, 0)),
                  pl.BlockSpec((VT_ROWS, SEQ), lambda s: (0, s))],
        out_specs=pl.BlockSpec((SEQ, MLA_HEADS * MLA_V), lambda s: (s, 0)),
        out_shape=jax.ShapeDtypeStruct((NCTX, MLA_HEADS * MLA_V), BF16),
        compiler_params=_cp(("arbitrary",)),
        name="attn_ctx",
    )(qp, kp, vtp)


def _attention_smp(qp, kp, vtp, kc, vct, layer):
    hpb = 2
    wq = hpb * HEAD_W
    wv = hpb * MLA_V
    nq = DEC_SEQ // TQ_ATT
    qoff = NCTX // TQ_ATT
    koff = NCTX // DEC_SEQ
    return pl.pallas_call(
        functools.partial(_attn_kernel, heads=hpb, has_cache=True),
        grid=(DEC_BATCH, MLA_HEADS // hpb, nq),
        in_specs=[pl.BlockSpec((TQ_ATT, wq), lambda b, g, i: (qoff + b * nq + i, g)),
                  pl.BlockSpec((DEC_SEQ, wq), lambda b, g, i: (koff + b, g)),
                  pl.BlockSpec((wv, DEC_SEQ), lambda b, g, i: (g, koff + b)),
                  pl.BlockSpec((1, 1, PAST_LEN, wq), lambda b, g, i: (b, layer, 0, g)),
                  pl.BlockSpec((1, 1, wv, PAST_LEN), lambda b, g, i: (b, layer, g, 0))],
        out_specs=pl.BlockSpec((TQ_ATT, hpb * MLA_V), lambda b, g, i: (b * nq + i, g)),
        out_shape=jax.ShapeDtypeStruct((NSMP, MLA_HEADS * MLA_V), BF16),
        compiler_params=_cp(("arbitrary", "arbitrary", "arbitrary")),
        name="attn_smp",
    )(qp, kp, vtp, kc, vct)


def _seq_row_block(ctx, nblk):
    if ctx:
        return lambda d, s, j: s
    off = NCTX // SEQ_BLK
    return lambda d, s, j: off + s * nblk + j + d * (nblk - 1 - 2 * j)


def _chunk_masks(fwd):
    row = lax.broadcasted_iota(jnp.int32, (CHUNK, CHUNK), 0)
    col = lax.broadcasted_iota(jnp.int32, (CHUNK, CHUNK), 1)
    incl = (row - col) * jnp.where(fwd, 1, -1) >= 0
    return row, col, incl


def _gla_kernel(*refs, nblk, has_s0):
    if has_s0:
        q_ref, k_ref, v_ref, glr_ref, wg_ref, bg_ref, s0_ref, o_ref, s_scr = refs
    else:
        q_ref, k_ref, v_ref, glr_ref, wg_ref, bg_ref, o_ref, sfin_ref, s_scr = refs
    d = pl.program_id(0)
    j = pl.program_id(2)
    fwd = d == 0

    @pl.when(j == 0)
    def _():
        if has_s0:
            s_scr[...] = s0_ref[0, 0, 0]
        else:
            s_scr[...] = jnp.zeros_like(s_scr)

    _, _, incl = _chunk_masks(fwd)
    tri_b = jnp.where(incl, 1.0, 0.0).astype(BF16)
    wg = wg_ref[0, 0]
    bg = bg_ref[0, 0]

    ch = []
    for c in range(NCH):
        ci = c + d * (NCH - 1 - 2 * c)
        rows = pl.ds(pl.multiple_of(ci * CHUNK, CHUNK), CHUNK)
        ch.append(dict(rows=rows, z=_dot(glr_ref[rows, :], wg) + bg))
    for t in ch:
        t["la"] = _split3(_log_sigmoid(t["z"]) * (1.0 / GLA_TAU))
    for t in ch:
        hi, mid, lo = t["la"]
        t["bc"] = _dot(tri_b, hi) + (_dot(tri_b, mid) + _dot(tri_b, lo))
    for t in ch:
        rows, bc = t["rows"], t["bc"]
        q = q_ref[rows, :].astype(F32) * (GLA_DK ** -0.5)
        k = k_ref[rows, :].astype(F32)
        bl = jnp.where(fwd, bc[CHUNK - 1:CHUNK], bc[0:1])
        t["qd"] = (q * jnp.exp(bc)).astype(BF16)
        t["ki"] = (k * jnp.exp(-bc)).astype(BF16)
        t["ke"] = (k * jnp.exp(bl - bc)).astype(BF16)
        t["dec_t"] = jnp.transpose(jnp.broadcast_to(jnp.exp(bl), (LANE, GLA_HEADS * GLA_DK)))
        t["v"] = v_ref[rows, :]
    heads = [(slice(h * GLA_DK, (h + 1) * GLA_DK), slice(h * GLA_DV, (h + 1) * GLA_DV))
             for h in range(GLA_HEADS)]
    for t in ch:
        t["a"] = [jnp.where(incl, _dot_nt(t["qd"][:, sk], t["ki"][:, sk]), 0.0).astype(BF16)
                  for sk, _ in heads]
    for t in ch:
        t["oi"] = [_dot(t["a"][h], t["v"][:, sv]) for h, (_, sv) in enumerate(heads)]
        t["upd"] = [_dot_tn(t["ke"][:, sk], t["v"][:, sv]) for sk, sv in heads]
    states = [s_scr[h] for h in range(GLA_HEADS)]
    for t in ch:
        for h, (sk, sv) in enumerate(heads):
            o_ref[0, t["rows"], sv] = t["oi"][h] + _dot(t["qd"][:, sk], states[h].astype(BF16))
            states[h] = t["dec_t"][sk, :] * states[h] + t["upd"][h]
    for h in range(GLA_HEADS):
        s_scr[h] = states[h]

    if not has_s0:
        @pl.when(j == nblk - 1)
        def _():
            sfin_ref[0, 0] = s_scr[...]


def _gla(proj, wg_p, bg_p, s0, layer, ctx):
    nseq, nblk = (BATCH, SEQ // SEQ_BLK) if ctx else (DEC_BATCH, DEC_SEQ // SEQ_BLK)
    rb = _seq_row_block(ctx, nblk)
    hk = GLA_HEADS * GLA_DK
    hv = GLA_HEADS * GLA_DV
    in_specs = [pl.BlockSpec((SEQ_BLK, hk), lambda d, s, j: (rb(d, s, j), PC_GQ // hk)),
                pl.BlockSpec((SEQ_BLK, hk), lambda d, s, j: (rb(d, s, j), PC_GK // hk)),
                pl.BlockSpec((SEQ_BLK, hv), lambda d, s, j: (rb(d, s, j), PC_GV // hv)),
                pl.BlockSpec((SEQ_BLK, LANE), lambda d, s, j: (rb(d, s, j), PC_GLR // LANE)),
                pl.BlockSpec((1, 1, LANE, hk), lambda d, s, j: (layer, d, 0, 0)),
                pl.BlockSpec((1, 1, 1, hk), lambda d, s, j: (layer, d, 0, 0))]
    args = [proj, proj, proj, proj, wg_p, bg_p]
    nrows = NCTX if ctx else NSMP
    roff = 0 if ctx else NCTX // SEQ_BLK
    o_spec = pl.BlockSpec((1, SEQ_BLK, hv), lambda d, s, j: (d, rb(d, s, j) - roff, 0))
    o_shape = jax.ShapeDtypeStruct((2, nrows, hv), F32)
    st_blk = (GLA_HEADS, GLA_DK, GLA_DV)
    if ctx:
        out_specs = [o_spec, pl.BlockSpec((1, 1) + st_blk, lambda d, s, j: (s, d, 0, 0, 0))]
        out_shape = [o_shape, jax.ShapeDtypeStruct((BATCH, 2) + st_blk, F32)]
    else:
        in_specs.append(pl.BlockSpec((1, 1, 1) + st_blk, lambda d, s, j: (s, layer, d, 0, 0, 0)))
        args.append(s0)
        out_specs, out_shape = o_spec, o_shape
    return pl.pallas_call(
        functools.partial(_gla_kernel, nblk=nblk, has_s0=not ctx),
        grid=(2, nseq, nblk),
        in_specs=in_specs, out_specs=out_specs, out_shape=out_shape,
        scratch_shapes=[pltpu.VMEM(st_blk, F32)],
        compiler_params=_cp(("arbitrary", "arbitrary", "arbitrary")),
        name="gla_ctx" if ctx else "gla_smp",
    )(*args)


GDN_G = 2
GDN_GW = GDN_G * GDN_DK
HALO = 16


def _gdn_pre_kernel(x_ref, xp_ref, xn_ref, cw_ref, o_ref, *, blocks_per_seq_ctx, blocks_per_seq_smp):
    i = pl.program_id(0)
    nctx = NCTX // SEQ_BLK
    bi = jnp.where(i < nctx, i % blocks_per_seq_ctx, (i - nctx) % blocks_per_seq_smp)
    nblk = jnp.where(i < nctx, blocks_per_seq_ctx, blocks_per_seq_smp)
    hk = GDN_HEADS * GDN_DK
    x = x_ref[...].astype(F32)
    rowi = lax.broadcasted_iota(jnp.int32, x.shape, 0)
    prev_row = jnp.where(bi > 0, xp_ref[...].astype(F32)[HALO - 1:HALO], 0.0)
    next_row = jnp.where(bi < nblk - 1, xn_ref[...].astype(F32)[0:1], 0.0)
    x_prev = jnp.where(rowi == 0, prev_row, pltpu.roll(x, 1, 0))
    x_next = jnp.where(rowi == SEQ_BLK - 1, next_row, pltpu.roll(x, SEQ_BLK - 1, 0))
    cw = cw_ref[0]
    y = _silu(x_prev * cw[0:1] + x * cw[1:2] + x_next * cw[2:3])

    lane = lax.broadcasted_iota(jnp.int32, (SEQ_BLK, LANE), 1)
    lo_half = lane < GDN_DK

    def l2n(t, scale):
        parts = []
        for c in range(hk // LANE):
            tc = t[:, c * LANE:(c + 1) * LANE]
            sq = tc * tc
            s_lo = jnp.sum(jnp.where(lo_half, sq, 0.0), axis=-1, keepdims=True)
            s_hi = jnp.sum(jnp.where(lo_half, 0.0, sq), axis=-1, keepdims=True)
            parts.append(tc * (lax.rsqrt(jnp.where(lo_half, s_lo, s_hi) + EPS) * scale))
        return jnp.concatenate(parts, axis=1)

    o_ref[:, :hk] = l2n(y[:, :hk], GDN_DK ** -0.5).astype(BF16)
    o_ref[:, hk:2 * hk] = l2n(y[:, hk:2 * hk], 1.0).astype(BF16)
    o_ref[:, 2 * hk:] = y[:, 2 * hk:].astype(BF16)


def _gdn_pre(proj, conv_w, layer):
    cw = 2 * GDN_HEADS * GDN_DK + GDN_HEADS * GDN_DV
    halo = SEQ_BLK // HALO
    last_h = R // HALO - 1
    return pl.pallas_call(
        functools.partial(_gdn_pre_kernel, blocks_per_seq_ctx=SEQ // SEQ_BLK,
                          blocks_per_seq_smp=DEC_SEQ // SEQ_BLK),
        grid=(R // SEQ_BLK,),
        in_specs=[pl.BlockSpec((SEQ_BLK, cw), lambda i: (i, PC_DQKV // cw)),
                  pl.BlockSpec((HALO, cw), lambda i: (jnp.maximum(i * halo - 1, 0), PC_DQKV // cw)),
                  pl.BlockSpec((HALO, cw), lambda i: (jnp.minimum((i + 1) * halo, last_h), PC_DQKV // cw)),
                  pl.BlockSpec((1, 3, cw), lambda i: (layer, 0, 0))],
        out_specs=pl.BlockSpec((SEQ_BLK, cw), lambda i: (i, 0)),
        out_shape=jax.ShapeDtypeStruct((R, cw), BF16),
        compiler_params=_cp(("arbitrary",)),
        name="gdn_pre",
    )(proj, proj, proj, conv_w)


def _gdn_kernel(*refs, nblk, has_s0):
    per_dir = [refs[0:4], refs[4:8]]
    alog_ref, dtb_ref = refs[8:10]
    if has_s0:
        s0_ref, of_ref, ob_ref, s_scr = refs[10:]
    else:
        of_ref, ob_ref, sfin_ref, s_scr = refs[10:]
    o_refs = [of_ref, ob_ref]
    j = pl.program_id(1)
    ngrp = GDN_HEADS // GDN_G

    def head_block(h):
        g, hh = divmod(h, GDN_G)
        return g, slice(hh * GDN_DK, (hh + 1) * GDN_DK), slice(hh * GDN_DV, (hh + 1) * GDN_DV)

    @pl.when(j == 0)
    def _():
        s_scr[...] = jnp.zeros_like(s_scr)
        if has_s0:
            for dd in range(2):
                for h in range(GDN_HEADS):
                    g, rk, rv = head_block(h)
                    s_scr[dd, g, rk, rv] = s0_ref[0, 0, dd, h]

    row = lax.broadcasted_iota(jnp.int32, (CHUNK, GDN_GW), 0)
    col = lax.broadcasted_iota(jnp.int32, (CHUNK, GDN_GW), 1) % CHUNK
    r64 = lax.broadcasted_iota(jnp.int32, (CHUNK, CHUNK), 0)
    c64 = lax.broadcasted_iota(jnp.int32, (CHUNK, CHUNK), 1)
    dir_masks = []
    for dd in range(2):
        lag = (row - col) if dd == 0 else (col - row)
        tri = (c64 <= r64) if dd == 0 else (c64 >= r64)
        dir_masks.append(dict(incl=lag >= 0, strict=lag > 0,
                              tri_b=jnp.where(tri, 1.0, 0.0).astype(BF16)))
    eye = col == row
    eye_f = jnp.where(eye, 1.0, 0.0)
    blk8 = (row // 8) == (col // 8)
    blk16 = (row // 16) == (col // 16)
    blk32 = (row // 32) == (col // 32)
    brow = lax.broadcasted_iota(jnp.int32, (GDN_GW, GDN_GW), 0) // CHUNK
    bcol = lax.broadcasted_iota(jnp.int32, (GDN_GW, GDN_GW), 1) // CHUNK
    same_head = brow == bcol

    def bdiag(t):
        tb = t.astype(BF16)
        return jnp.where(same_head, jnp.concatenate([tb] * GDN_G, axis=0), jnp.zeros((), BF16))

    def bdmm(a, b):
        return _dot(a.astype(BF16), bdiag(b))

    def setup(dd, c):
        q_ref, k_ref, v_ref, ab_ref = per_dir[dd]
        mk = dir_masks[dd]
        ci = c if dd == 0 else NCH - 1 - c
        rows = slice(ci * CHUNK, (ci + 1) * CHUNK)
        ab = ab_ref[rows, :].astype(F32)
        neg_a = -jnp.exp(alog_ref[0, dd])
        gam = _cumsum_rows(mk["tri_b"], neg_a * _softplus(ab + dtb_ref[0, dd]))
        beta_all = _sigmoid(ab)
        out = []
        for g in range(ngrp):
            gl = slice(g * GDN_GW, (g + 1) * GDN_GW)
            gcol = jnp.concatenate(
                [jnp.broadcast_to(gam[:, g * GDN_G + h:g * GDN_G + h + 1], (CHUNK, GDN_DK))
                 for h in range(GDN_G)], axis=1)
            bcol_ = jnp.concatenate(
                [jnp.broadcast_to(beta_all[:, GDN_HEADS + g * GDN_G + h:GDN_HEADS + g * GDN_G + h + 1],
                                  (CHUNK, GDN_DK)) for h in range(GDN_G)], axis=1)
            grow = jnp.sum(jnp.where(eye, gcol, 0.0), axis=0, keepdims=True)
            glast = gcol[CHUNK - 1:CHUNK] if dd == 0 else gcol[0:1]
            decay = jnp.where(mk["incl"], jnp.exp(gcol - grow), 0.0)
            egc = jnp.exp(gcol)
            qg = q_ref[rows, gl].astype(F32)
            kg = k_ref[rows, gl].astype(F32)
            vg = v_ref[rows, gl].astype(F32)
            kq = _dot_nt(jnp.concatenate([k_ref[rows, gl], q_ref[rows, gl]], axis=0),
                         bdiag(k_ref[rows, gl]))
            out.append(dict(dd=dd, g=g, rows=rows, gl=gl,
                            m=jnp.where(mk["strict"], bcol_ * kq[:CHUNK] * decay, 0.0),
                            aqk=(kq[CHUNK:] * decay).astype(BF16),
                            vb=vg * bcol_, kb=kg * (bcol_ * egc), qe=qg * egc,
                            kend=(kg * jnp.exp(glast - gcol)).astype(BF16),
                            eg=jnp.exp(glast)))
        return out

    def stages(ch):
        def neumann_a():
            for t in ch:
                t["n8"] = jnp.where(blk8, t["m"], 0.0)
                t["n2"] = bdmm(t["n8"], t["n8"])

        def neumann_b():
            for t in ch:
                p1 = eye_f - t["n8"]
                t["n4"] = bdmm(t["n2"], t["n2"])
                t["p1"] = p1 + bdmm(p1, t["n2"])

        def neumann_c():
            for t in ch:
                t["dinv"] = t["p1"] + bdmm(t["p1"], t["n4"])

        def double_a(off):
            def run():
                for t in ch:
                    t["dl"] = bdmm(t["dinv"], jnp.where(off, t["m"], 0.0))
            return run

        def double_b():
            for t in ch:
                t["dinv"] = t["dinv"] - bdmm(t["dl"], t["dinv"])

        def solve():
            for t in ch:
                t["u"] = bdmm(t["dinv"], t["vb"])
                t["w"] = bdmm(t["dinv"], t["kb"])

        def fold():
            for t in ch:
                wu = jnp.concatenate([t["w"], t["u"]], axis=1).astype(BF16)
                cb = _dot_tn(t["kend"], wu)
                t["c"] = jnp.where(same_head, cb[:, :GDN_GW], 0.0).astype(BF16)
                t["b"] = jnp.where(same_head, cb[:, GDN_GW:], 0.0)
                ao = _dot(t["aqk"], jnp.concatenate([bdiag(t["w"]), bdiag(t["u"])], axis=1))
                t["qt"] = (t["qe"] - ao[:, :GDN_GW]).astype(BF16)
                t["o"] = ao[:, GDN_GW:]

        out = [neumann_a, neumann_b, neumann_c]
        for inner, outer in ((blk8, blk16), (blk16, blk32), (blk32, None)):
            off = jnp.logical_not(inner) if outer is None else jnp.logical_and(outer, jnp.logical_not(inner))
            out += [double_a(off), double_b]
        return out + [solve, fold]

    states = [[s_scr[dd, g] for g in range(ngrp)] for dd in range(2)]

    def recur(cur):
        ys = [_dot(jnp.concatenate([t["qt"], t["c"]], axis=0), states[t["dd"]][t["g"]].astype(BF16))
              for t in cur]
        for t, y in zip(cur, ys):
            o_refs[t["dd"]][t["rows"], t["gl"]] = y[:CHUNK] + t["o"]
            states[t["dd"]][t["g"]] = t["eg"] * states[t["dd"]][t["g"]] - y[CHUNK:] + t["b"]

    half = NCH // 2
    chains = {}
    for c in range(half):
        for dd in range(2):
            chains[(dd, c)] = setup(dd, c)
    later = [(dd, c) for c in range(half, NCH) for dd in range(2)]
    wave_a = [t for c in range(half) for dd in range(2) for t in chains[(dd, c)]]
    for k, stage in enumerate(stages(wave_a)):
        stage()
        if k % 2 == 0 and later:
            key = later.pop(0)
            chains[key] = setup(*key)
    for key in later:
        chains[key] = setup(*key)
    wave_b = [t for c in range(half, NCH) for dd in range(2) for t in chains[(dd, c)]]
    done = 0
    for k, stage in enumerate(stages(wave_b)):
        stage()
        if k % 3 == 1 and done < half:
            recur(chains[(0, done)] + chains[(1, done)])
            done += 1
    for c in range(done, NCH):
        recur(chains[(0, c)] + chains[(1, c)])
    for dd in range(2):
        for g in range(ngrp):
            s_scr[dd, g] = states[dd][g]

    if not has_s0:
        @pl.when(j == nblk - 1)
        def _():
            for dd in range(2):
                for h in range(GDN_HEADS):
                    g, rk, rv = head_block(h)
                    sfin_ref[0, dd, h] = s_scr[dd, g, rk, rv]


def _gdn(qkv, proj, alog_p, dtb_p, s0, layer, ctx):
    nseq, nblk = (BATCH, SEQ // SEQ_BLK) if ctx else (DEC_BATCH, DEC_SEQ // SEQ_BLK)
    roff = 0 if ctx else NCTX // SEQ_BLK
    hk = GDN_HEADS * GDN_DK
    hv = GDN_HEADS * GDN_DV

    def rb(dd):
        return lambda s, j: roff + s * nblk + (j if dd == 0 else nblk - 1 - j)

    in_specs, args = [], []
    for dd in range(2):
        r = rb(dd)
        in_specs += [pl.BlockSpec((SEQ_BLK, hk), lambda s, j, r=r: (r(s, j), 0)),
                     pl.BlockSpec((SEQ_BLK, hk), lambda s, j, r=r: (r(s, j), 1)),
                     pl.BlockSpec((SEQ_BLK, hv), lambda s, j, r=r: (r(s, j), 2 * hk // hv)),
                     pl.BlockSpec((SEQ_BLK, LANE), lambda s, j, r=r, dd=dd: (r(s, j), PC_DAB // LANE + dd))]
        args += [qkv, qkv, qkv, proj]
    in_specs += [pl.BlockSpec((1, 2, 1, LANE), lambda s, j: (layer, 0, 0, 0))] * 2
    args += [alog_p, dtb_p]
    nrows = NCTX if ctx else NSMP
    out_specs = [pl.BlockSpec((SEQ_BLK, hv), lambda s, j, r=rb(dd): (r(s, j) - roff, 0)) for dd in range(2)]
    out_shape = [jax.ShapeDtypeStruct((nrows, hv), F32)] * 2
    st_blk = (2, GDN_HEADS, GDN_DK, GDN_DV)
    if ctx:
        out_specs.append(pl.BlockSpec((1,) + st_blk, lambda s, j: (s, 0, 0, 0, 0)))
        out_shape.append(jax.ShapeDtypeStruct((BATCH,) + st_blk, F32))
    else:
        in_specs.append(pl.BlockSpec((1, 1) + st_blk, lambda s, j: (s, layer, 0, 0, 0, 0)))
        args.append(s0)
    return pl.pallas_call(
        functools.partial(_gdn_kernel, nblk=nblk, has_s0=not ctx),
        grid=(nseq, nblk),
        in_specs=in_specs, out_specs=out_specs, out_shape=out_shape,
        scratch_shapes=[pltpu.VMEM((2, GDN_HEADS // GDN_G, GDN_GW, GDN_GW), F32)],
        compiler_params=_cp(("arbitrary", "arbitrary")),
        name="gdn_ctx" if ctx else "gdn_smp",
    )(*args)


def _group_rms(x, width):
    parts = []
    lane = lax.broadcasted_iota(jnp.int32, (x.shape[0], LANE), 1)
    lo_half = lane < 64
    for c in range(x.shape[1] // LANE):
        xc = x[:, c * LANE:(c + 1) * LANE]
        sq = xc * xc
        if width == LANE:
            ms = jnp.mean(sq, axis=-1, keepdims=True)
        else:
            s_lo = jnp.sum(jnp.where(lo_half, sq, 0.0), axis=-1, keepdims=True)
            s_hi = jnp.sum(jnp.where(lo_half, 0.0, sq), axis=-1, keepdims=True)
            ms = jnp.where(lo_half, s_lo, s_hi) * (1.0 / width)
        parts.append(xc * lax.rsqrt(ms + EPS))
    return jnp.concatenate(parts, axis=1)


def _merge_kernel(ymc_ref, yms_ref, ogc_ref, ogs_ref, gr_ref, odcf_ref, odcb_ref, odsf_ref, odsb_ref,
                  dz_ref, gt_ref, *rest, split_h):
    if split_h:
        hc_ref, hs_ref, *rest = rest
        h_rows = lambda: _family_rows((hc_ref, hs_ref), TM_MERGE)
    else:
        h_ref, *rest = rest
        h_rows = lambda: h_ref[...]
    ga_ref, gn_ref, dn_ref, wb_ref, wo_ref, o_ref, ym_scr, og_scr, od_scr = rest
    is_ctx = pl.program_id(0) < NCTX // TM_MERGE

    @pl.when(is_ctx)
    def _():
        ym_scr[...] = ymc_ref[...]
        og_scr[...] = ogc_ref[0] + ogc_ref[1]
        od_scr[...] = odcf_ref[...] + odcb_ref[...]

    @pl.when(jnp.logical_not(is_ctx))
    def _():
        ym_scr[...] = yms_ref[...]
        og_scr[...] = ogs_ref[0] + ogs_ref[1]
        od_scr[...] = odsf_ref[...] + odsb_ref[...]

    y_gla = _group_rms(og_scr[...], GLA_DV) * gn_ref[0] * _silu(gr_ref[...].astype(F32))
    y_gdn = _group_rms(od_scr[...], GDN_DV) * dn_ref[0] * _silu(dz_ref[...].astype(F32))
    gates = _sigmoid(gt_ref[...].astype(F32))
    d = D_MODEL
    acc = gates[:, :d] * _dot(ym_scr[...], wb_ref[0, 0])
    acc = acc + gates[:, d:2 * d] * _dot(y_gla.astype(BF16), wb_ref[0, 1])
    acc = acc + gates[:, 2 * d:] * _dot(y_gdn.astype(BF16), wb_ref[0, 2])
    y = _dot(acc.astype(BF16), wo_ref[0])
    o_ref[...] = h_rows() + ga_ref[0] * y


def _merge(y_mla, o_gla, o_gdn, proj, h, mod3, gla_norm_p, gdn_norm_p, w_branch_b, w_out_b, layer):
    tm = TM_MERGE
    bw = 512
    nc = NCTX // tm
    split_h = isinstance(h, tuple)
    h_specs = _family_specs(tm, D_MODEL) if split_h else [pl.BlockSpec((tm, D_MODEL), lambda i: (i, 0))]
    h_args = list(h) if split_h else [h]

    def cmap(i):
        return jnp.minimum(i, nc - 1)

    def smap(i):
        return jnp.maximum(i - nc, 0)

    return pl.pallas_call(
        functools.partial(_merge_kernel, split_h=split_h),
        grid=(R // tm,),
        in_specs=[pl.BlockSpec((tm, bw), lambda i: (cmap(i), 0)),
                  pl.BlockSpec((tm, bw), lambda i: (smap(i), 0)),
                  pl.BlockSpec((2, tm, bw), lambda i: (0, cmap(i), 0)),
                  pl.BlockSpec((2, tm, bw), lambda i: (0, smap(i), 0)),
                  pl.BlockSpec((tm, bw), lambda i: (i, PC_GR // bw)),
                  pl.BlockSpec((tm, bw), lambda i: (cmap(i), 0)),
                  pl.BlockSpec((tm, bw), lambda i: (cmap(i), 0)),
                  pl.BlockSpec((tm, bw), lambda i: (smap(i), 0)),
                  pl.BlockSpec((tm, bw), lambda i: (smap(i), 0)),
                  pl.BlockSpec((tm, bw), lambda i: (i, PC_DZ // bw)),
                  pl.BlockSpec((tm, 3 * D_MODEL), lambda i: (i, PC_GATES // (3 * D_MODEL)))] + h_specs + [
                  _mod_spec(layer, 2, tm),
                  pl.BlockSpec((1, 1, bw), lambda i: (layer, 0, 0)),
                  pl.BlockSpec((1, 1, bw), lambda i: (layer, 0, 0)),
                  pl.BlockSpec((1, 3, bw, D_MODEL), lambda i: (layer, 0, 0, 0)),
                  pl.BlockSpec((1, D_MODEL, D_MODEL), lambda i: (layer, 0, 0))],
        out_specs=pl.BlockSpec((tm, D_MODEL), lambda i: (i, 0)),
        out_shape=jax.ShapeDtypeStruct((R, D_MODEL), F32),
        scratch_shapes=[pltpu.VMEM((tm, bw), BF16), pltpu.VMEM((tm, bw), F32), pltpu.VMEM((tm, bw), F32)],
        compiler_params=_cp(("arbitrary",)),
        name="merge",
    )(y_mla[0], y_mla[1], o_gla[0], o_gla[1], proj, o_gdn[0][0], o_gdn[0][1], o_gdn[1][0], o_gdn[1][1],
      proj, proj, *h_args, mod3, gla_norm_p, gdn_norm_p, w_branch_b, w_out_b)


def _ffn_kernel(h_ref, g_ref, sc_ref, sh_ref, gf_ref, wi_ref, wo_ref, fn_ref, *o_refs, final):
    h = h_ref[...]
    xf = (_rms(h, g_ref[0]) * (1.0 + sc_ref[0]) + sh_ref[0]).astype(BF16)

    def gate_up(c):
        f0 = c * TF_FFN
        return (_dot(xf, wi_ref[0, :, f0:f0 + TF_FFN]),
                _dot(xf, wi_ref[0, :, D_FF + f0:D_FF + f0 + TF_FFN]))

    n_chunks = D_FF // TF_FFN
    nxt = gate_up(0)
    acc = None
    for c in range(n_chunks):
        g, u = nxt
        if c + 1 < n_chunks:
            nxt = gate_up(c + 1)
        part = _dot((_silu(g) * u).astype(BF16), wo_ref[0, c * TF_FFN:(c + 1) * TF_FFN, :])
        acc = part if acc is None else acc + part
    out = h + gf_ref[0] * acc
    if not final:
        o_refs[0][...] = out
    else:
        out = _rms(out, fn_ref[...])
        is_ctx = pl.program_id(0) < NCTX // TM_FFN

        @pl.when(is_ctx)
        def _():
            o_refs[0][...] = out

        @pl.when(jnp.logical_not(is_ctx))
        def _():
            o_refs[1][...] = out


def _ffn(h, mod3, norm_ffn, wi_b, wo_b, final_norm, layer, final):
    tm = TM_FFN
    if final:
        nc = NCTX // tm
        out_specs = [pl.BlockSpec((tm, D_MODEL), lambda i: (jnp.minimum(i, nc - 1), 0)),
                     pl.BlockSpec((tm, D_MODEL), lambda i: (jnp.maximum(i - nc, 0), 0))]
        out_shape = [jax.ShapeDtypeStruct((NCTX, D_MODEL), F32), jax.ShapeDtypeStruct((NSMP, D_MODEL), F32)]
    else:
        out_specs = pl.BlockSpec((tm, D_MODEL), lambda i: (i, 0))
        out_shape = jax.ShapeDtypeStruct((R, D_MODEL), F32)
    return pl.pallas_call(
        functools.partial(_ffn_kernel, final=final),
        grid=(R // tm,),
        in_specs=[pl.BlockSpec((tm, D_MODEL), lambda i: (i, 0)),
                  pl.BlockSpec((1, 1, D_MODEL), lambda i: (layer, 0, 0)),
                  _mod_spec(layer, 4, tm),
                  _mod_spec(layer, 3, tm),
                  _mod_spec(layer, 5, tm),
                  pl.BlockSpec((1, D_MODEL, 2 * D_FF), lambda i: (layer, 0, 0), pipeline_mode=pl.Buffered(1)),
                  pl.BlockSpec((1, D_FF, D_MODEL), lambda i: (layer, 0, 0), pipeline_mode=pl.Buffered(1)),
                  pl.BlockSpec((1, D_MODEL), lambda i: (0, 0))],
        out_specs=out_specs, out_shape=out_shape,
        compiler_params=_cp(("arbitrary",)),
        name="ffn_final" if final else "ffn",
    )(h, norm_ffn.reshape(DEPTH, 1, D_MODEL), mod3, mod3, mod3, wi_b, wo_b, final_norm.reshape(1, D_MODEL))


def _rope_partner(w):
    h = AXIS_DIM // 2
    parts = []
    for a in range(2):
        x1 = w[..., a * AXIS_DIM:a * AXIS_DIM + h]
        x2 = w[..., a * AXIS_DIM + h:(a + 1) * AXIS_DIM]
        parts += [-x2, x1]
    return jnp.concatenate(parts, axis=-1)


def _pack_w_in(w_in):
    c = [0]
    for s in (Q_LORA, KV_LORA + ROPE_DIM, 256, 256, 512, 512, 2 * GLA_RANK, 1536, 512,
              2 * GDN_HEADS, 2 * GDN_HEADS, 3 * D_MODEL):
        c.append(c[-1] + s)
    seg = [w_in[..., c[i]:c[i + 1]] for i in range(12)]
    mq, mkv, gq, gk, gv, gr, glr, dqkv, dz, da, db, gates = seg

    def zeros(n):
        return jnp.zeros(w_in.shape[:-1] + (n,), w_in.dtype)

    kr = mkv[..., KV_LORA:]
    mla = jnp.concatenate([mq, mkv[..., :KV_LORA], kr, _rope_partner(kr), zeros(64)], axis=-1)
    dab = []
    for d in range(2):
        dab += [da[..., d * GDN_HEADS:(d + 1) * GDN_HEADS], db[..., d * GDN_HEADS:(d + 1) * GDN_HEADS],
                zeros(LANE - 2 * GDN_HEADS)]
    packed = jnp.concatenate([gates, dqkv, dz, gv, gr, mla, gq, gk, glr, zeros(LANE - 2 * GLA_RANK)]
                             + dab + [zeros(PROJ_W - PC_DAB - 2 * LANE)], axis=-1)
    return packed.astype(BF16)


def _pack_mla_weights(w_uq, w_ukv):
    l = w_uq.shape[0]
    qh = w_uq.reshape(l, Q_LORA, MLA_HEADS, MLA_NOPE + ROPE_DIM)
    zq = jnp.zeros((l, Q_LORA, MLA_HEADS, HEAD_W - MLA_NOPE - ROPE_DIM), w_uq.dtype)
    zn = jnp.zeros((l, Q_LORA, MLA_HEADS, MLA_NOPE), w_uq.dtype)
    wq = jnp.concatenate([qh, zq], axis=-1).reshape(l, Q_LORA, MLA_HEADS * HEAD_W)
    wq_sw = jnp.concatenate([zn, _rope_partner(qh[..., MLA_NOPE:]), zq], axis=-1).reshape(l, Q_LORA, MLA_HEADS * HEAD_W)
    wq_p = jnp.concatenate([wq, wq_sw], axis=-1).astype(BF16)

    kvh = w_ukv.reshape(l, KV_LORA, MLA_HEADS, MLA_NOPE + MLA_V)
    zk = jnp.zeros((l, KV_LORA, MLA_HEADS, HEAD_W - MLA_NOPE), w_ukv.dtype)
    wk_top = jnp.concatenate([kvh[..., :MLA_NOPE], zk], axis=-1).reshape(l, KV_LORA, MLA_HEADS * HEAD_W)
    place = jnp.zeros((KV_LORA, MLA_HEADS, HEAD_W), w_ukv.dtype)
    idx = jnp.arange(ROPE_DIM)
    for rep in range(2):
        place = place.at[rep * ROPE_DIM + idx, :, MLA_NOPE + idx].set(1.0)
    wk_bot = jnp.broadcast_to(place.reshape(1, KV_LORA, MLA_HEADS * HEAD_W), (l, KV_LORA, MLA_HEADS * HEAD_W))
    wk_p = jnp.concatenate([wk_top, wk_bot], axis=1).astype(BF16)
    wv_p = jnp.transpose(kvh[..., MLA_NOPE:], (0, 2, 3, 1)).reshape(l, VT_ROWS, KV_LORA).astype(BF16)
    return wq_p, wk_p, wv_p


def _rope_tables(tm):
    rows = DEC_SEQ // GRID_W
    row = jnp.repeat(jnp.arange(rows, dtype=F32), GRID_W)
    col = jnp.tile(jnp.arange(GRID_W, dtype=F32), rows)
    inv = ROPE_THETA ** (-jnp.arange(0, AXIS_DIM, 2, dtype=F32) / AXIS_DIM)
    ang_r, ang_c = row[:, None] * inv, col[:, None] * inv
    cos32 = jnp.concatenate([jnp.cos(ang_r)] * 2 + [jnp.cos(ang_c)] * 2, axis=-1)
    sin32 = jnp.concatenate([jnp.sin(ang_r)] * 2 + [jnp.sin(ang_c)] * 2, axis=-1)
    cos32 = jnp.concatenate([jnp.ones((tm, ROPE_DIM), F32), cos32], axis=0)
    sin32 = jnp.concatenate([jnp.zeros((tm, ROPE_DIM), F32), sin32], axis=0)
    n = cos32.shape[0]
    scale = (MLA_NOPE + ROPE_DIM) ** -0.5 * math.log2(math.e)
    pad = HEAD_W - MLA_NOPE - ROPE_DIM
    cq = jnp.concatenate([jnp.ones((n, MLA_NOPE), F32), cos32, jnp.zeros((n, pad), F32)], axis=-1) * scale
    sq = jnp.concatenate([jnp.zeros((n, MLA_NOPE), F32), sin32, jnp.zeros((n, pad), F32)], axis=-1) * scale
    ck = jnp.concatenate([cos32, sin32, jnp.zeros((n, HEAD_W - 2 * ROPE_DIM), F32)], axis=-1)
    return jnp.concatenate([cq, sq, ck], axis=-1)


def _pad_lanes(x, n):
    return jnp.pad(x, [(0, 0)] * (x.ndim - 1) + [(0, n - x.shape[-1])])


def kernel(x_prompt, x_sample, cache_mla, state_gla, state_gdn, c, c_ctx, w_mod, b_mod, norm_mix, w_in,
           mla_q_norm, mla_w_uq, mla_kv_norm, mla_w_ukv, gla_w_gate, gla_b_gate, gla_norm, gdn_conv,
           gdn_a_log, gdn_dt_bias, gdn_norm, w_branch, w_out, norm_ffn, ffn_w_in, ffn_w_out, final_norm):
    w_in_p = _pack_w_in(w_in)
    wq_p, wk_p, wv_p = _pack_mla_weights(mla_w_uq, mla_w_ukv)
    tab = _rope_tables(TM_PREP)
    wg_p = jnp.zeros((DEPTH, 2, LANE, GLA_HEADS * GLA_DK), F32)
    for d in range(2):
        wg_p = wg_p.at[:, d, d * GLA_RANK:(d + 1) * GLA_RANK, :].set(gla_w_gate[:, d])
    wg_p = wg_p.astype(BF16)
    bg_p = gla_b_gate.reshape(DEPTH, 2, 1, GLA_HEADS * GLA_DK)
    alog_p = _pad_lanes(gdn_a_log, LANE).reshape(DEPTH, 2, 1, LANE)
    dtb_p = _pad_lanes(gdn_dt_bias, LANE).reshape(DEPTH, 2, 1, LANE)
    gla_norm_p = jnp.tile(gla_norm, (1, GLA_HEADS)).reshape(DEPTH, 1, GLA_HEADS * GLA_DV)
    gdn_norm_p = jnp.tile(gdn_norm, (1, GDN_HEADS)).reshape(DEPTH, 1, GDN_HEADS * GDN_DV)
    w_branch_b = w_branch.astype(BF16)
    w_out_b = w_out.astype(BF16)
    wi_b = ffn_w_in.astype(BF16)
    wo_b = ffn_w_out.astype(BF16)
    cache_p = _pad_lanes(cache_mla, 2 * KV_LORA)
    cond8 = jnp.concatenate([c_ctx[None, :], c, jnp.zeros((MOD_ROWS - 1 - DEC_BATCH, D_MODEL), F32)], axis=0)

    mod = _modulation(cond8, w_mod, b_mod)
    mod3 = mod.reshape(DEPTH * MOD_ROWS * 6, 1, D_MODEL)
    kc, vc = _kv_cache(cache_p, wk_p, wv_p)

    h = (x_prompt.reshape(NCTX, D_MODEL), x_sample.reshape(NSMP, D_MODEL))
    kv_list, gla_list, gdn_list = [], [], []
    for l in range(DEPTH):
        proj = _inproj(h, mod3, norm_mix, w_in_p, l)
        qp, kp, vtp, own = _mla_prep(proj, tab, mla_q_norm, mla_kv_norm, wq_p, wk_p, wv_p, l)
        y_mla = (_attention_ctx(qp, kp, vtp), _attention_smp(qp, kp, vtp, kc, vc, l))
        og_c, sg = _gla(proj, wg_p, bg_p, None, l, True)
        og_s = _gla(proj, wg_p, bg_p, state_gla, l, False)
        qkv = _gdn_pre(proj, gdn_conv, l)
        odf_c, odb_c, sd = _gdn(qkv, proj, alog_p, dtb_p, None, l, True)
        odf_s, odb_s = _gdn(qkv, proj, alog_p, dtb_p, state_gdn, l, False)
        h = _merge(y_mla, (og_c, og_s), ((odf_c, odb_c), (odf_s, odb_s)), proj, h, mod3, gla_norm_p, gdn_norm_p,
                   w_branch_b, w_out_b, l)
        h = _ffn(h, mod3, norm_ffn, wi_b, wo_b, final_norm, l, l == DEPTH - 1)
        kv_list.append(own[:NCTX, :KV_LORA + ROPE_DIM].reshape(BATCH, SEQ, KV_LORA + ROPE_DIM))
        gla_list.append(sg)
        gdn_list.append(sd)

    y_prompt = h[0].reshape(BATCH, SEQ, D_MODEL)
    y_sample = h[1].reshape(DEC_BATCH, DEC_SEQ, D_MODEL)
    return (y_prompt, y_sample, jnp.stack(kv_list, axis=1), jnp.stack(gla_list, axis=1),
            jnp.stack(gdn_list, axis=1))
```

```python
import functools
import math

import jax
import jax.numpy as jnp
from jax import lax
from jax.experimental import pallas as pl
from jax.experimental.pallas import tpu as pltpu

F32 = jnp.float32
BF16 = jnp.bfloat16

D_MODEL = 1024
BATCH = 16
SEQ = 256
DEPTH = 2
DEC_BATCH = 2
DEC_SEQ = 4096
PAST_LEN = 256
GRID_W = 64
CHUNK = 64
EPS = 1e-6
MLA_HEADS = 8
MLA_NOPE = 64
ROPE_DIM = 32
AXIS_DIM = ROPE_DIM // 2
MLA_V = 64
Q_LORA = 256
KV_LORA = 128
ROPE_THETA = 10000.0
GLA_HEADS = 4
GLA_DK = 64
GLA_DV = 128
GLA_RANK = 16
GLA_TAU = 16.0
GDN_HEADS = 8
GDN_DK = 64
GDN_DV = 64
D_FF = ((8 * D_MODEL + 3 * 256 - 1) // (3 * 256)) * 256
MOD_W = 6 * D_MODEL

NCTX = BATCH * SEQ
NSMP = DEC_BATCH * DEC_SEQ
R = NCTX + NSMP
LANE = 128
HEAD_W = 128
VT_ROWS = MLA_HEADS * MLA_V
SEQ_BLK = 256
NCH = SEQ_BLK // CHUNK
GLA_BLK_CTX = 256
GLA_BLK_SMP = 512
MOD_ROWS = 8

PC_GATES = 0
PC_DQKV = 3072
PC_DZ = 4608
PC_GV = 5120
PC_GR = 5632
PC_MLA = 6144
PC_GQ = 6656
PC_GK = 6912
PC_GLR = 7168
PC_DAB = 7296
PROJ_W = 7680

TM_IN = 1024
TN_IN = 1536
TM_PREP = 512
TM_MERGE = 512
TM_FFN = 512
TF_FFN = 256
TQ_ATT = 512
TK_ATT = 512
ONES_ROWS = 16
ATT_AHEAD = 8
VMEM_LIMIT = 48 * 1024 * 1024


def _cp(sem):
    return pltpu.CompilerParams(dimension_semantics=sem, vmem_limit_bytes=VMEM_LIMIT)


def _dot(a, b):
    return jnp.dot(a, b, preferred_element_type=F32)


def _dot_nt(a, b):
    return lax.dot_general(a, b, (((1,), (1,)), ((), ())), preferred_element_type=F32)


def _dot_tn(a, b):
    return lax.dot_general(a, b, (((0,), (0,)), ((), ())), preferred_element_type=F32)


def _split2(x):
    hi = x.astype(BF16)
    lo = (x - hi.astype(F32)).astype(BF16)
    return hi, lo


def _split3(x):
    hi = x.astype(BF16)
    r1 = x - hi.astype(F32)
    mid = r1.astype(BF16)
    lo = (r1 - mid.astype(F32)).astype(BF16)
    return hi, mid, lo


def _cumsum_rows(tri_b, x):
    hi, mid, lo = _split3(x)
    return _dot(tri_b, hi) + (_dot(tri_b, mid) + _dot(tri_b, lo))


def _sigmoid(x):
    return 0.5 * jnp.tanh(0.5 * x) + 0.5


def _silu(x):
    return x * _sigmoid(x)


def _softplus(x):
    return jnp.maximum(x, 0.0) + jnp.log(1.0 + jnp.exp(-jnp.abs(x)))


def _log_sigmoid(x):
    return jnp.minimum(x, 0.0) - jnp.log(1.0 + jnp.exp(-jnp.abs(x)))


def _rms(x, g):
    return x * lax.rsqrt(jnp.mean(x * x, axis=-1, keepdims=True) + EPS) * g


def _mod_row(row_start):
    return jnp.where(row_start < NCTX, 0, 1 + (row_start - NCTX) // DEC_SEQ)


def _mod_spec(layer, which, tm, axis=0):
    def imap(*idx):
        return ((layer * MOD_ROWS + _mod_row(idx[axis] * tm)) * 6 + which, 0, 0)
    return pl.BlockSpec((1, 1, D_MODEL), imap)


def _mod_kernel(c_ref, w_ref, b_ref, o_ref):
    x = _silu(c_ref[...])
    xh, xl = _split2(x)
    wh, wl = _split2(w_ref[0])
    o_ref[0] = _dot(xh, wh) + (_dot(xl, wh) + _dot(xh, wl)) + b_ref[0]


def _modulation(cond8, w_mod, b_mod):
    tn = 1536
    return pl.pallas_call(
        _mod_kernel,
        grid=(DEPTH, MOD_W // tn),
        in_specs=[pl.BlockSpec((MOD_ROWS, D_MODEL), lambda l, j: (0, 0)),
                  pl.BlockSpec((1, D_MODEL, tn), lambda l, j: (l, 0, j)),
                  pl.BlockSpec((1, 1, tn), lambda l, j: (l, 0, j))],
        out_specs=pl.BlockSpec((1, MOD_ROWS, tn), lambda l, j: (l, 0, j)),
        out_shape=jax.ShapeDtypeStruct((DEPTH, MOD_ROWS, MOD_W), F32),
        compiler_params=_cp(("arbitrary", "arbitrary")),
        name="modulation",
    )(cond8, w_mod, b_mod.reshape(DEPTH, 1, MOD_W))


def _family_specs(tm, width, axis=0):
    nc = NCTX // tm
    return [pl.BlockSpec((tm, width), lambda *idx: (jnp.minimum(idx[axis], nc - 1), 0)),
            pl.BlockSpec((tm, width), lambda *idx: (jnp.maximum(idx[axis] - nc, 0), 0))]


def _family_rows(refs, tm, axis=0):
    is_ctx = pl.program_id(axis) < NCTX // tm
    return jnp.where(is_ctx, refs[0][...], refs[1][...])


def _inproj_kernel(*refs, split):
    if split:
        hc_ref, hs_ref, g_ref, sc_ref, sh_ref, w_ref, o_ref = refs
        x = _family_rows((hc_ref, hs_ref), TM_IN, axis=1)
    else:
        h_ref, g_ref, sc_ref, sh_ref, w_ref, o_ref = refs
        x = h_ref[...]
    y = _rms(x, g_ref[0])
    xn = (y * (1.0 + sc_ref[0]) + sh_ref[0]).astype(BF16)
    o_ref[...] = _dot(xn, w_ref[0]).astype(BF16)


def _inproj(h, mod3, norm_mix, w_in_p, layer):
    split = isinstance(h, tuple)
    h_specs = (_family_specs(TM_IN, D_MODEL, axis=1) if split
               else [pl.BlockSpec((TM_IN, D_MODEL), lambda j, i: (i, 0))])
    h_args = list(h) if split else [h]
    return pl.pallas_call(
        functools.partial(_inproj_kernel, split=split),
        grid=(PROJ_W // TN_IN, R // TM_IN),
        in_specs=h_specs + [
                  pl.BlockSpec((1, 1, D_MODEL), lambda j, i: (layer, 0, 0)),
                  _mod_spec(layer, 1, TM_IN, axis=1),
                  _mod_spec(layer, 0, TM_IN, axis=1),
                  pl.BlockSpec((1, D_MODEL, TN_IN), lambda j, i: (layer, 0, j))],
        out_specs=pl.BlockSpec((TM_IN, TN_IN), lambda j, i: (i, j)),
        out_shape=jax.ShapeDtypeStruct((R, PROJ_W), BF16),
        compiler_params=_cp(("arbitrary", "arbitrary")),
        name="inproj",
    )(*h_args, norm_mix.reshape(DEPTH, 1, D_MODEL), mod3, mod3, w_in_p)


def _mla_prep_kernel(pm_ref, tab_ref, qn_ref, kvn_ref, wq_ref, wk_ref, wv_ref,
                     q_ref, k_ref, v_ref, own_ref):
    pm = pm_ref[...].astype(F32)
    tab = tab_ref[...]
    qn = _rms(pm[:, :Q_LORA], qn_ref[0]).astype(BF16)
    q2 = _dot(qn, wq_ref[0])
    hw = MLA_HEADS * HEAD_W
    cq = jnp.tile(tab[:, :HEAD_W], (1, MLA_HEADS))
    sq = jnp.tile(tab[:, HEAD_W:2 * HEAD_W], (1, MLA_HEADS))
    q_ref[...] = (q2[:, :hw] * cq + q2[:, hw:] * sq).astype(BF16)
    ckv = _rms(pm[:, Q_LORA:Q_LORA + KV_LORA], kvn_ref[0])
    kr = pm[:, Q_LORA + KV_LORA:] * tab[:, 2 * HEAD_W:]
    lhs = jnp.concatenate([ckv, kr], axis=1)
    own_ref[...] = lhs
    lb = lhs.astype(BF16)
    k_ref[...] = _dot(lb, wk_ref[0]).astype(BF16)
    v_ref[...] = _dot_nt(wv_ref[0], lb[:, :KV_LORA]).astype(BF16)


def _mla_prep(proj, tab, q_norm, kv_norm, wq_p, wk_p, wv_p, layer):
    tm = TM_PREP
    hw = MLA_HEADS * HEAD_W

    def tab_map(i):
        r0 = i * tm
        return (jnp.where(r0 < NCTX, 0, 1 + ((r0 - NCTX) % DEC_SEQ) // tm), 0)

    return pl.pallas_call(
        _mla_prep_kernel,
        grid=(R // tm,),
        in_specs=[pl.BlockSpec((tm, 512), lambda i: (i, PC_MLA // 512)),
                  pl.BlockSpec((tm, 3 * HEAD_W), tab_map),
                  pl.BlockSpec((1, 1, Q_LORA), lambda i: (layer, 0, 0)),
                  pl.BlockSpec((1, 1, KV_LORA), lambda i: (layer, 0, 0)),
                  pl.BlockSpec((1, Q_LORA, 2 * hw), lambda i: (layer, 0, 0)),
                  pl.BlockSpec((1, 2 * KV_LORA, hw), lambda i: (layer, 0, 0)),
                  pl.BlockSpec((1, VT_ROWS, KV_LORA), lambda i: (layer, 0, 0))],
        out_specs=[pl.BlockSpec((tm, hw), lambda i: (i, 0)),
                   pl.BlockSpec((tm, hw), lambda i: (i, 0)),
                   pl.BlockSpec((VT_ROWS, tm), lambda i: (0, i)),
                   pl.BlockSpec((tm, 2 * KV_LORA), lambda i: (i, 0))],
        out_shape=[jax.ShapeDtypeStruct((R, hw), BF16),
                   jax.ShapeDtypeStruct((R, hw), BF16),
                   jax.ShapeDtypeStruct((VT_ROWS, R), BF16),
                   jax.ShapeDtypeStruct((R, 2 * KV_LORA), F32)],
        compiler_params=_cp(("arbitrary",)),
        name="mla_prep",
    )(proj, tab, q_norm.reshape(DEPTH, 1, Q_LORA), kv_norm.reshape(DEPTH, 1, KV_LORA),
      wq_p, wk_p, wv_p)


def _kv_cache_kernel(c_ref, wk_ref, wv_ref, k_ref, v_ref):
    lb = c_ref[0, 0].astype(BF16)
    k_ref[0, 0] = _dot(lb, wk_ref[0]).astype(BF16)
    v_ref[0, 0] = _dot_nt(wv_ref[0], lb[:, :KV_LORA]).astype(BF16)


def _kv_cache(cache_p, wk_p, wv_p):
    hw = MLA_HEADS * HEAD_W
    return pl.pallas_call(
        _kv_cache_kernel,
        grid=(DEC_BATCH, DEPTH),
        in_specs=[pl.BlockSpec((1, 1, PAST_LEN, 2 * KV_LORA), lambda b, l: (b, l, 0, 0)),
                  pl.BlockSpec((1, 2 * KV_LORA, hw), lambda b, l: (l, 0, 0)),
                  pl.BlockSpec((1, VT_ROWS, KV_LORA), lambda b, l: (l, 0, 0))],
        out_specs=[pl.BlockSpec((1, 1, PAST_LEN, hw), lambda b, l: (b, l, 0, 0)),
                   pl.BlockSpec((1, 1, VT_ROWS, PAST_LEN), lambda b, l: (b, l, 0, 0))],
        out_shape=[jax.ShapeDtypeStruct((DEC_BATCH, DEPTH, PAST_LEN, hw), BF16),
                   jax.ShapeDtypeStruct((DEC_BATCH, DEPTH, VT_ROWS, PAST_LEN), BF16)],
        compiler_params=_cp(("arbitrary", "arbitrary")),
        name="kv_cache",
    )(cache_p, wk_p, wv_p)


def _attn_kernel(*refs, heads, has_cache):
    if has_cache:
        q_ref, k_ref, vt_ref, kc_ref, vct_ref, o_ref = refs
    else:
        q_ref, k_ref, vt_ref, o_ref = refs
    n_keys = k_ref.shape[0]
    tk = min(TK_ATT, n_keys)
    tiles = ([("cache", 0, PAST_LEN)] if has_cache else []) + [("own", s0, tk) for s0 in range(0, n_keys, tk)]
    units = [(h, t) for t in range(len(tiles)) for h in range(heads)]
    qs = [q_ref[:, h * HEAD_W:(h + 1) * HEAD_W] for h in range(heads)]

    def scores(h, t):
        kind, s0, n = tiles[t]
        sl = slice(h * HEAD_W, (h + 1) * HEAD_W)
        keys = kc_ref[0, 0, :, sl] if kind == "cache" else k_ref[s0:s0 + n, sl]
        return _dot_nt(keys, qs[h])

    def values_t(h, t):
        kind, s0, n = tiles[t]
        sv = slice(h * MLA_V, (h + 1) * MLA_V)
        vt = vct_ref[0, 0, sv, :] if kind == "cache" else vt_ref[sv, s0:s0 + n]
        return jnp.concatenate([vt, jnp.ones((ONES_ROWS, n), BF16)], axis=0)

    ahead = ATT_AHEAD
    pending = {u: scores(*u) for u in units[:ahead]}
    m, acc = [None] * heads, [None] * heads
    for i, (h, t) in enumerate(units):
        if i + ahead < len(units):
            pending[units[i + ahead]] = scores(*units[i + ahead])
        st = pending.pop((h, t))
        mt = jnp.max(st, axis=0, keepdims=True)
        if m[h] is None:
            m[h] = mt
            acc[h] = _dot(values_t(h, t), jnp.exp2(st - mt).astype(BF16))
        else:
            m_new = jnp.maximum(m[h], mt)
            alpha = jnp.exp2(m[h] - m_new)
            acc[h] = alpha * acc[h] + _dot(values_t(h, t), jnp.exp2(st - m_new).astype(BF16))
            m[h] = m_new
    outs = [acc[h][:MLA_V] / acc[h][MLA_V:MLA_V + 1] for h in range(heads)]
    o_ref[...] = jnp.concatenate(outs, axis=0).T.astype(BF16)


def _attention_ctx(qp, kp, vtp):
    hw = MLA_HEADS * HEAD_W
    return pl.pallas_call(
        functools.partial(_attn_kernel, heads=MLA_HEADS, has_cache=False),
        grid=(BATCH,),
        in_specs=[pl.BlockSpec((SEQ, hw), lambda s: (s, 0)),
                  pl.BlockSpec((SEQ, hw), lambda s: (s, 0)),
                  pl.BlockSpec((VT_ROWS, SEQ), lambda s: (0, s))],
        out_specs=pl.BlockSpec((SEQ, MLA_HEADS * MLA_V), lambda s: (s, 0)),
        out_shape=jax.ShapeDtypeStruct((NCTX, MLA_HEADS * MLA_V), BF16),
        compiler_params=_cp(("arbitrary",)),
        name="attn_ctx",
    )(qp, kp, vtp)


def _attention_smp(qp, kp, vtp, kc, vct, layer):
    hpb = 2
    wq = hpb * HEAD_W
    wv = hpb * MLA_V
    nq = DEC_SEQ // TQ_ATT
    qoff = NCTX // TQ_ATT
    koff = NCTX // DEC_SEQ
    return pl.pallas_call(
        functools.partial(_attn_kernel, heads=hpb, has_cache=True),
        grid=(DEC_BATCH, MLA_HEADS // hpb, nq),
        in_specs=[pl.BlockSpec((TQ_ATT, wq), lambda b, g, i: (qoff + b * nq + i, g)),
                  pl.BlockSpec((DEC_SEQ, wq), lambda b, g, i: (koff + b, g)),
                  pl.BlockSpec((wv, DEC_SEQ), lambda b, g, i: (g, koff + b)),
                  pl.BlockSpec((1, 1, PAST_LEN, wq), lambda b, g, i: (b, layer, 0, g)),
                  pl.BlockSpec((1, 1, wv, PAST_LEN), lambda b, g, i: (b, layer, g, 0))],
        out_specs=pl.BlockSpec((TQ_ATT, hpb * MLA_V), lambda b, g, i: (b * nq + i, g)),
        out_shape=jax.ShapeDtypeStruct((NSMP, MLA_HEADS * MLA_V), BF16),
        compiler_params=_cp(("arbitrary", "arbitrary", "arbitrary")),
        name="attn_smp",
    )(qp, kp, vtp, kc, vct)


def _scan_specs(ctx, blk, widths_and_cols, dir_cols=()):
    nseq, seq_len = (BATCH, SEQ) if ctx else (DEC_BATCH, DEC_SEQ)
    nblk = seq_len // blk
    roff = 0 if ctx else NCTX // blk

    def rb(dd):
        return lambda s, j: roff + s * nblk + (j if dd == 0 else nblk - 1 - j)

    specs = []
    for dd in range(2):
        r = rb(dd)
        specs += [pl.BlockSpec((blk, w), lambda s, j, r=r, c=c: (r(s, j), c)) for w, c in widths_and_cols]
        specs += [pl.BlockSpec((blk, w), lambda s, j, r=r, c=c + dd: (r(s, j), c)) for w, c in dir_cols]
    out_maps = [lambda s, j, r=rb(dd): (r(s, j) - roff, 0) for dd in range(2)]
    return nseq, nblk, specs, out_maps


def _gla_kernel(*refs, nblk, has_s0, nch):
    per_dir = [refs[0:4], refs[4:8]]
    wg_ref, bg_ref = refs[8:10]
    if has_s0:
        s0_ref, of_ref, ob_ref, s_scr = refs[10:]
    else:
        of_ref, ob_ref, sfin_ref, s_scr = refs[10:]
    o_refs = [of_ref, ob_ref]
    j = pl.program_id(1)

    @pl.when(j == 0)
    def _():
        if has_s0:
            s_scr[...] = s0_ref[0, 0]
        else:
            s_scr[...] = jnp.zeros_like(s_scr)

    r64 = lax.broadcasted_iota(jnp.int32, (CHUNK, CHUNK), 0)
    c64 = lax.broadcasted_iota(jnp.int32, (CHUNK, CHUNK), 1)
    incl = [c64 <= r64, c64 >= r64]
    tri_b = [jnp.where(m, 1.0, 0.0).astype(BF16) for m in incl]
    heads = [(slice(h * GLA_DK, (h + 1) * GLA_DK), slice(h * GLA_DV, (h + 1) * GLA_DV))
             for h in range(GLA_HEADS)]

    ch = []
    for c in range(nch):
        for dd in range(2):
            ci = c if dd == 0 else nch - 1 - c
            rows = slice(ci * CHUNK, (ci + 1) * CHUNK)
            glr_ref = per_dir[dd][3]
            ch.append(dict(dd=dd, rows=rows, z=_dot(glr_ref[rows, :], wg_ref[0, dd]) + bg_ref[0, dd]))
    for t in ch:
        t["la"] = _split3(_log_sigmoid(t["z"]) * (1.0 / GLA_TAU))
    for t in ch:
        hi, mid, lo = t["la"]
        tb = tri_b[t["dd"]]
        t["bc"] = _dot(tb, hi) + (_dot(tb, mid) + _dot(tb, lo))
    for t in ch:
        q_ref, k_ref, v_ref, _ = per_dir[t["dd"]]
        rows, bc = t["rows"], t["bc"]
        q = q_ref[rows, :].astype(F32) * (GLA_DK ** -0.5)
        k = k_ref[rows, :].astype(F32)
        bl = bc[CHUNK - 1:CHUNK] if t["dd"] == 0 else bc[0:1]
        t["qd"] = (q * jnp.exp(bc)).astype(BF16)
        t["ki"] = (k * jnp.exp(-bc)).astype(BF16)
        t["ke"] = (k * jnp.exp(bl - bc)).astype(BF16)
        t["dec_t"] = jnp.transpose(jnp.broadcast_to(jnp.exp(bl), (LANE, GLA_HEADS * GLA_DK)))
        t["v"] = v_ref[rows, :]
    for t in ch:
        t["a"] = [jnp.where(incl[t["dd"]], _dot_nt(t["qd"][:, sk], t["ki"][:, sk]), 0.0).astype(BF16)
                  for sk, _ in heads]
    for t in ch:
        t["oi"] = [_dot(t["a"][h], t["v"][:, sv]) for h, (_, sv) in enumerate(heads)]
        t["upd"] = [_dot_tn(t["ke"][:, sk], t["v"][:, sv]) for sk, sv in heads]
    states = [[s_scr[dd, h] for h in range(GLA_HEADS)] for dd in range(2)]
    for t in ch:
        st = states[t["dd"]]
        t["s_in"] = [s.astype(BF16) for s in st]
        for h, (sk, _) in enumerate(heads):
            st[h] = t["dec_t"][sk, :] * st[h] + t["upd"][h]
    for t in ch:
        for h, (sk, sv) in enumerate(heads):
            o_refs[t["dd"]][t["rows"], sv] = t["oi"][h] + _dot(t["qd"][:, sk], t["s_in"][h])
    for dd in range(2):
        for h in range(GLA_HEADS):
            s_scr[dd, h] = states[dd][h]

    if not has_s0:
        @pl.when(j == nblk - 1)
        def _():
            sfin_ref[0] = s_scr[...]


def _gla(proj, wg_p, bg_p, s0, layer, ctx):
    blk = GLA_BLK_CTX if ctx else GLA_BLK_SMP
    hk = GLA_HEADS * GLA_DK
    hv = GLA_HEADS * GLA_DV
    nseq, nblk, in_specs, out_maps = _scan_specs(
        ctx, blk, [(hk, PC_GQ // hk), (hk, PC_GK // hk), (hv, PC_GV // hv), (LANE, PC_GLR // LANE)])
    in_specs += [pl.BlockSpec((1, 2, LANE, hk), lambda s, j: (layer, 0, 0, 0)),
                 pl.BlockSpec((1, 2, 1, hk), lambda s, j: (layer, 0, 0, 0))]
    args = [proj] * 8 + [wg_p, bg_p]
    nrows = NCTX if ctx else NSMP
    out_specs = [pl.BlockSpec((blk, hv), m) for m in out_maps]
    out_shape = [jax.ShapeDtypeStruct((nrows, hv), F32)] * 2
    st_blk = (2, GLA_HEADS, GLA_DK, GLA_DV)
    if ctx:
        out_specs.append(pl.BlockSpec((1,) + st_blk, lambda s, j: (s, 0, 0, 0, 0)))
        out_shape.append(jax.ShapeDtypeStruct((BATCH,) + st_blk, F32))
    else:
        in_specs.append(pl.BlockSpec((1, 1) + st_blk, lambda s, j: (s, layer, 0, 0, 0, 0)))
        args.append(s0)
    return pl.pallas_call(
        functools.partial(_gla_kernel, nblk=nblk, has_s0=not ctx, nch=blk // CHUNK),
        grid=(nseq, nblk),
        in_specs=in_specs, out_specs=out_specs, out_shape=out_shape,
        scratch_shapes=[pltpu.VMEM(st_blk, F32)],
        compiler_params=_cp(("arbitrary", "arbitrary")),
        name="gla_ctx" if ctx else "gla_smp",
    )(*args)


GDN_G = 2
GDN_GW = GDN_G * GDN_DK
HALO = 16


def _gdn_pre_kernel(x_ref, xp_ref, xn_ref, cw_ref, o_ref, *, blocks_per_seq_ctx, blocks_per_seq_smp):
    i = pl.program_id(0)
    nctx = NCTX // SEQ_BLK
    bi = jnp.where(i < nctx, i % blocks_per_seq_ctx, (i - nctx) % blocks_per_seq_smp)
    nblk = jnp.where(i < nctx, blocks_per_seq_ctx, blocks_per_seq_smp)
    hk = GDN_HEADS * GDN_DK
    x = x_ref[...].astype(F32)
    rowi = lax.broadcasted_iota(jnp.int32, x.shape, 0)
    prev_row = jnp.where(bi > 0, xp_ref[...].astype(F32)[HALO - 1:HALO], 0.0)
    next_row = jnp.where(bi < nblk - 1, xn_ref[...].astype(F32)[0:1], 0.0)
    x_prev = jnp.where(rowi == 0, prev_row, pltpu.roll(x, 1, 0))
    x_next = jnp.where(rowi == SEQ_BLK - 1, next_row, pltpu.roll(x, SEQ_BLK - 1, 0))
    cw = cw_ref[0]
    y = _silu(x_prev * cw[0:1] + x * cw[1:2] + x_next * cw[2:3])

    lane = lax.broadcasted_iota(jnp.int32, (SEQ_BLK, LANE), 1)
    lo_half = lane < GDN_DK

    def l2n(t, scale):
        parts = []
        for c in range(hk // LANE):
            tc = t[:, c * LANE:(c + 1) * LANE]
            sq = tc * tc
            s_lo = jnp.sum(jnp.where(lo_half, sq, 0.0), axis=-1, keepdims=True)
            s_hi = jnp.sum(jnp.where(lo_half, 0.0, sq), axis=-1, keepdims=True)
            parts.append(tc * (lax.rsqrt(jnp.where(lo_half, s_lo, s_hi) + EPS) * scale))
        return jnp.concatenate(parts, axis=1)

    o_ref[:, :hk] = l2n(y[:, :hk], GDN_DK ** -0.5).astype(BF16)
    o_ref[:, hk:2 * hk] = l2n(y[:, hk:2 * hk], 1.0).astype(BF16)
    o_ref[:, 2 * hk:] = y[:, 2 * hk:].astype(BF16)


def _gdn_pre(proj, conv_w, layer):
    cw = 2 * GDN_HEADS * GDN_DK + GDN_HEADS * GDN_DV
    halo = SEQ_BLK // HALO
    last_h = R // HALO - 1
    return pl.pallas_call(
        functools.partial(_gdn_pre_kernel, blocks_per_seq_ctx=SEQ // SEQ_BLK,
                          blocks_per_seq_smp=DEC_SEQ // SEQ_BLK),
        grid=(R // SEQ_BLK,),
        in_specs=[pl.BlockSpec((SEQ_BLK, cw), lambda i: (i, PC_DQKV // cw)),
                  pl.BlockSpec((HALO, cw), lambda i: (jnp.maximum(i * halo - 1, 0), PC_DQKV // cw)),
                  pl.BlockSpec((HALO, cw), lambda i: (jnp.minimum((i + 1) * halo, last_h), PC_DQKV // cw)),
                  pl.BlockSpec((1, 3, cw), lambda i: (layer, 0, 0))],
        out_specs=pl.BlockSpec((SEQ_BLK, cw), lambda i: (i, 0)),
        out_shape=jax.ShapeDtypeStruct((R, cw), BF16),
        compiler_params=_cp(("arbitrary",)),
        name="gdn_pre",
    )(proj, proj, proj, conv_w)


def _gdn_kernel(*refs, nblk, has_s0):
    per_dir = [refs[0:4], refs[4:8]]
    alog_ref, dtb_ref = refs[8:10]
    if has_s0:
        s0_ref, of_ref, ob_ref, s_scr = refs[10:]
    else:
        of_ref, ob_ref, sfin_ref, s_scr = refs[10:]
    o_refs = [of_ref, ob_ref]
    j = pl.program_id(1)
    ngrp = GDN_HEADS // GDN_G

    def head_block(h):
        g, hh = divmod(h, GDN_G)
        return g, slice(hh * GDN_DK, (hh + 1) * GDN_DK), slice(hh * GDN_DV, (hh + 1) * GDN_DV)

    @pl.when(j == 0)
    def _():
        s_scr[...] = jnp.zeros_like(s_scr)
        if has_s0:
            for dd in range(2):
                for h in range(GDN_HEADS):
                    g, rk, rv = head_block(h)
                    s_scr[dd, g, rk, rv] = s0_ref[0, 0, dd, h]

    row = lax.broadcasted_iota(jnp.int32, (CHUNK, GDN_GW), 0)
    col = lax.broadcasted_iota(jnp.int32, (CHUNK, GDN_GW), 1) % CHUNK
    r64 = lax.broadcasted_iota(jnp.int32, (CHUNK, CHUNK), 0)
    c64 = lax.broadcasted_iota(jnp.int32, (CHUNK, CHUNK), 1)
    dir_masks = []
    for dd in range(2):
        lag = (row - col) if dd == 0 else (col - row)
        tri = (c64 <= r64) if dd == 0 else (c64 >= r64)
        dir_masks.append(dict(incl=lag >= 0, strict=lag > 0,
                              tri_b=jnp.where(tri, 1.0, 0.0).astype(BF16)))
    eye = col == row
    eye_f = jnp.where(eye, 1.0, 0.0)
    blk8 = (row // 8) == (col // 8)
    blk16 = (row // 16) == (col // 16)
    blk32 = (row // 32) == (col // 32)
    brow = lax.broadcasted_iota(jnp.int32, (GDN_GW, GDN_GW), 0) // CHUNK
    bcol = lax.broadcasted_iota(jnp.int32, (GDN_GW, GDN_GW), 1) // CHUNK
    same_head = brow == bcol

    def bdiag(t):
        tb = t.astype(BF16)
        return jnp.where(same_head, jnp.concatenate([tb] * GDN_G, axis=0), jnp.zeros((), BF16))

    def bdmm(a, b):
        return _dot(a.astype(BF16), bdiag(b))

    def setup(dd, c):
        q_ref, k_ref, v_ref, ab_ref = per_dir[dd]
        mk = dir_masks[dd]
        ci = c if dd == 0 else NCH - 1 - c
        rows = slice(ci * CHUNK, (ci + 1) * CHUNK)
        ab = ab_ref[rows, :].astype(F32)
        neg_a = -jnp.exp(alog_ref[0, dd])
        gam = _cumsum_rows(mk["tri_b"], neg_a * _softplus(ab + dtb_ref[0, dd]))
        beta_all = _sigmoid(ab)
        out = []
        for g in range(ngrp):
            gl = slice(g * GDN_GW, (g + 1) * GDN_GW)
            gcol = jnp.concatenate(
                [jnp.broadcast_to(gam[:, g * GDN_G + h:g * GDN_G + h + 1], (CHUNK, GDN_DK))
                 for h in range(GDN_G)], axis=1)
            bcol_ = jnp.concatenate(
                [jnp.broadcast_to(beta_all[:, GDN_HEADS + g * GDN_G + h:GDN_HEADS + g * GDN_G + h + 1],
                                  (CHUNK, GDN_DK)) for h in range(GDN_G)], axis=1)
            grow = jnp.sum(jnp.where(eye, gcol, 0.0), axis=0, keepdims=True)
            glast = gcol[CHUNK - 1:CHUNK] if dd == 0 else gcol[0:1]
            decay = jnp.where(mk["incl"], jnp.exp(gcol - grow), 0.0)
            egc = jnp.exp(gcol)
            qg = q_ref[rows, gl].astype(F32)
            kg = k_ref[rows, gl].astype(F32)
            vg = v_ref[rows, gl].astype(F32)
            kq = _dot_nt(jnp.concatenate([k_ref[rows, gl], q_ref[rows, gl]], axis=0),
                         bdiag(k_ref[rows, gl]))
            out.append(dict(dd=dd, g=g, rows=rows, gl=gl,
                            m=jnp.where(mk["strict"], bcol_ * kq[:CHUNK] * decay, 0.0),
                            aqk=(kq[CHUNK:] * decay).astype(BF16),
                            vb=vg * bcol_, kb=kg * (bcol_ * egc), qe=qg * egc,
                            kend=(kg * jnp.exp(glast - gcol)).astype(BF16),
                            eg=jnp.exp(glast)))
        return out

    def stages(ch):
        def neumann_a():
            for t in ch:
                t["n8"] = jnp.where(blk8, t["m"], 0.0)
                t["n2"] = bdmm(t["n8"], t["n8"])

        def neumann_b():
            for t in ch:
                p1 = eye_f - t["n8"]
                t["n4"] = bdmm(t["n2"], t["n2"])
                t["p1"] = p1 + bdmm(p1, t["n2"])

        def neumann_c():
            for t in ch:
                t["dinv"] = t["p1"] + bdmm(t["p1"], t["n4"])

        def double_a(off):
            def run():
                for t in ch:
                    t["dl"] = bdmm(t["dinv"], jnp.where(off, t["m"], 0.0))
            return run

        def double_b():
            for t in ch:
                t["dinv"] = t["dinv"] - bdmm(t["dl"], t["dinv"])

        def solve():
            for t in ch:
                t["u"] = bdmm(t["dinv"], t["vb"])
                t["w"] = bdmm(t["dinv"], t["kb"])

        def fold():
            for t in ch:
                wu = jnp.concatenate([t["w"], t["u"]], axis=1).astype(BF16)
                cb = _dot_tn(t["kend"], wu)
                t["c"] = jnp.where(same_head, cb[:, :GDN_GW], 0.0).astype(BF16)
                t["b"] = jnp.where(same_head, cb[:, GDN_GW:], 0.0)
                ao = _dot(t["aqk"], jnp.concatenate([bdiag(t["w"]), bdiag(t["u"])], axis=1))
                t["qt"] = (t["qe"] - ao[:, :GDN_GW]).astype(BF16)
                t["o"] = ao[:, GDN_GW:]

        out = [neumann_a, neumann_b, neumann_c]
        for inner, outer in ((blk8, blk16), (blk16, blk32), (blk32, None)):
            off = jnp.logical_not(inner) if outer is None else jnp.logical_and(outer, jnp.logical_not(inner))
            out += [double_a(off), double_b]
        return out + [solve, fold]

    states = [[s_scr[dd, g] for g in range(ngrp)] for dd in range(2)]

    def recur(cur):
        ys = [_dot(jnp.concatenate([t["qt"], t["c"]], axis=0), states[t["dd"]][t["g"]].astype(BF16))
              for t in cur]
        for t, y in zip(cur, ys):
            o_refs[t["dd"]][t["rows"], t["gl"]] = y[:CHUNK] + t["o"]
            states[t["dd"]][t["g"]] = t["eg"] * states[t["dd"]][t["g"]] - y[CHUNK:] + t["b"]

    half = NCH // 2
    chains = {}
    for c in range(half):
        for dd in range(2):
            chains[(dd, c)] = setup(dd, c)
    later = [(dd, c) for c in range(half, NCH) for dd in range(2)]
    wave_a = [t for c in range(half) for dd in range(2) for t in chains[(dd, c)]]
    for k, stage in enumerate(stages(wave_a)):
        stage()
        if k % 2 == 0 and later:
            key = later.pop(0)
            chains[key] = setup(*key)
    for key in later:
        chains[key] = setup(*key)
    wave_b = [t for c in range(half, NCH) for dd in range(2) for t in chains[(dd, c)]]
    done = 0
    for k, stage in enumerate(stages(wave_b)):
        stage()
        if k % 3 == 1 and done < half:
            recur(chains[(0, done)] + chains[(1, done)])
            done += 1
    for c in range(done, NCH):
        recur(chains[(0, c)] + chains[(1, c)])
    for dd in range(2):
        for g in range(ngrp):
            s_scr[dd, g] = states[dd][g]

    if not has_s0:
        @pl.when(j == nblk - 1)
        def _():
            for dd in range(2):
                for h in range(GDN_HEADS):
                    g, rk, rv = head_block(h)
                    sfin_ref[0, dd, h] = s_scr[dd, g, rk, rv]


def _gdn(qkv, proj, alog_p, dtb_p, s0, layer, ctx):
    nseq, nblk = (BATCH, SEQ // SEQ_BLK) if ctx else (DEC_BATCH, DEC_SEQ // SEQ_BLK)
    roff = 0 if ctx else NCTX // SEQ_BLK
    hk = GDN_HEADS * GDN_DK
    hv = GDN_HEADS * GDN_DV

    def rb(dd):
        return lambda s, j: roff + s * nblk + (j if dd == 0 else nblk - 1 - j)

    in_specs, args = [], []
    for dd in range(2):
        r = rb(dd)
        in_specs += [pl.BlockSpec((SEQ_BLK, hk), lambda s, j, r=r: (r(s, j), 0)),
                     pl.BlockSpec((SEQ_BLK, hk), lambda s, j, r=r: (r(s, j), 1)),
                     pl.BlockSpec((SEQ_BLK, hv), lambda s, j, r=r: (r(s, j), 2 * hk // hv)),
                     pl.BlockSpec((SEQ_BLK, LANE), lambda s, j, r=r, dd=dd: (r(s, j), PC_DAB // LANE + dd))]
        args += [qkv, qkv, qkv, proj]
    in_specs += [pl.BlockSpec((1, 2, 1, LANE), lambda s, j: (layer, 0, 0, 0))] * 2
    args += [alog_p, dtb_p]
    nrows = NCTX if ctx else NSMP
    out_specs = [pl.BlockSpec((SEQ_BLK, hv), lambda s, j, r=rb(dd): (r(s, j) - roff, 0)) for dd in range(2)]
    out_shape = [jax.ShapeDtypeStruct((nrows, hv), F32)] * 2
    st_blk = (2, GDN_HEADS, GDN_DK, GDN_DV)
    if ctx:
        out_specs.append(pl.BlockSpec((1,) + st_blk, lambda s, j: (s, 0, 0, 0, 0)))
        out_shape.append(jax.ShapeDtypeStruct((BATCH,) + st_blk, F32))
    else:
        in_specs.append(pl.BlockSpec((1, 1) + st_blk, lambda s, j: (s, layer, 0, 0, 0, 0)))
        args.append(s0)
    return pl.pallas_call(
        functools.partial(_gdn_kernel, nblk=nblk, has_s0=not ctx),
        grid=(nseq, nblk),
        in_specs=in_specs, out_specs=out_specs, out_shape=out_shape,
        scratch_shapes=[pltpu.VMEM((2, GDN_HEADS // GDN_G, GDN_GW, GDN_GW), F32)],
        compiler_params=_cp(("arbitrary", "arbitrary")),
        name="gdn_ctx" if ctx else "gdn_smp",
    )(*args)


def _group_rms(x, width):
    parts = []
    lane = lax.broadcasted_iota(jnp.int32, (x.shape[0], LANE), 1)
    lo_half = lane < 64
    for c in range(x.shape[1] // LANE):
        xc = x[:, c * LANE:(c + 1) * LANE]
        sq = xc * xc
        if width == LANE:
            ms = jnp.mean(sq, axis=-1, keepdims=True)
        else:
            s_lo = jnp.sum(jnp.where(lo_half, sq, 0.0), axis=-1, keepdims=True)
            s_hi = jnp.sum(jnp.where(lo_half, 0.0, sq), axis=-1, keepdims=True)
            ms = jnp.where(lo_half, s_lo, s_hi) * (1.0 / width)
        parts.append(xc * lax.rsqrt(ms + EPS))
    return jnp.concatenate(parts, axis=1)


def _merge_kernel(ymc_ref, yms_ref, ogcf_ref, ogcb_ref, ogsf_ref, ogsb_ref, gr_ref,
                  odcf_ref, odcb_ref, odsf_ref, odsb_ref,
                  dz_ref, gt_ref, *rest, split_h):
    if split_h:
        hc_ref, hs_ref, *rest = rest
        h_rows = lambda: _family_rows((hc_ref, hs_ref), TM_MERGE)
    else:
        h_ref, *rest = rest
        h_rows = lambda: h_ref[...]
    ga_ref, gn_ref, dn_ref, wb_ref, wo_ref, o_ref, ym_scr, og_scr, od_scr = rest
    is_ctx = pl.program_id(0) < NCTX // TM_MERGE

    @pl.when(is_ctx)
    def _():
        ym_scr[...] = ymc_ref[...]
        og_scr[...] = ogcf_ref[...] + ogcb_ref[...]
        od_scr[...] = odcf_ref[...] + odcb_ref[...]

    @pl.when(jnp.logical_not(is_ctx))
    def _():
        ym_scr[...] = yms_ref[...]
        og_scr[...] = ogsf_ref[...] + ogsb_ref[...]
        od_scr[...] = odsf_ref[...] + odsb_ref[...]

    y_gla = _group_rms(og_scr[...], GLA_DV) * gn_ref[0] * _silu(gr_ref[...].astype(F32))
    y_gdn = _group_rms(od_scr[...], GDN_DV) * dn_ref[0] * _silu(dz_ref[...].astype(F32))
    gates = _sigmoid(gt_ref[...].astype(F32))
    d = D_MODEL
    acc = gates[:, :d] * _dot(ym_scr[...], wb_ref[0, 0])
    acc = acc + gates[:, d:2 * d] * _dot(y_gla.astype(BF16), wb_ref[0, 1])
    acc = acc + gates[:, 2 * d:] * _dot(y_gdn.astype(BF16), wb_ref[0, 2])
    y = _dot(acc.astype(BF16), wo_ref[0])
    o_ref[...] = h_rows() + ga_ref[0] * y


def _merge(y_mla, o_gla, o_gdn, proj, h, mod3, gla_norm_p, gdn_norm_p, w_branch_b, w_out_b, layer):
    tm = TM_MERGE
    bw = 512
    nc = NCTX // tm
    split_h = isinstance(h, tuple)
    h_specs = _family_specs(tm, D_MODEL) if split_h else [pl.BlockSpec((tm, D_MODEL), lambda i: (i, 0))]
    h_args = list(h) if split_h else [h]

    def cmap(i):
        return jnp.minimum(i, nc - 1)

    def smap(i):
        return jnp.maximum(i - nc, 0)

    return pl.pallas_call(
        functools.partial(_merge_kernel, split_h=split_h),
        grid=(R // tm,),
        in_specs=[pl.BlockSpec((tm, bw), lambda i: (cmap(i), 0)),
                  pl.BlockSpec((tm, bw), lambda i: (smap(i), 0)),
                  pl.BlockSpec((tm, bw), lambda i: (cmap(i), 0)),
                  pl.BlockSpec((tm, bw), lambda i: (cmap(i), 0)),
                  pl.BlockSpec((tm, bw), lambda i: (smap(i), 0)),
                  pl.BlockSpec((tm, bw), lambda i: (smap(i), 0)),
                  pl.BlockSpec((tm, bw), lambda i: (i, PC_GR // bw)),
                  pl.BlockSpec((tm, bw), lambda i: (cmap(i), 0)),
                  pl.BlockSpec((tm, bw), lambda i: (cmap(i), 0)),
                  pl.BlockSpec((tm, bw), lambda i: (smap(i), 0)),
                  pl.BlockSpec((tm, bw), lambda i: (smap(i), 0)),
                  pl.BlockSpec((tm, bw), lambda i: (i, PC_DZ // bw)),
                  pl.BlockSpec((tm, 3 * D_MODEL), lambda i: (i, PC_GATES // (3 * D_MODEL)))] + h_specs + [
                  _mod_spec(layer, 2, tm),
                  pl.BlockSpec((1, 1, bw), lambda i: (layer, 0, 0)),
                  pl.BlockSpec((1, 1, bw), lambda i: (layer, 0, 0)),
                  pl.BlockSpec((1, 3, bw, D_MODEL), lambda i: (layer, 0, 0, 0)),
                  pl.BlockSpec((1, D_MODEL, D_MODEL), lambda i: (layer, 0, 0))],
        out_specs=pl.BlockSpec((tm, D_MODEL), lambda i: (i, 0)),
        out_shape=jax.ShapeDtypeStruct((R, D_MODEL), F32),
        scratch_shapes=[pltpu.VMEM((tm, bw), BF16), pltpu.VMEM((tm, bw), F32), pltpu.VMEM((tm, bw), F32)],
        compiler_params=_cp(("arbitrary",)),
        name="merge",
    )(y_mla[0], y_mla[1], o_gla[0][0], o_gla[0][1], o_gla[1][0], o_gla[1][1], proj,
      o_gdn[0][0], o_gdn[0][1], o_gdn[1][0], o_gdn[1][1],
      proj, proj, *h_args, mod3, gla_norm_p, gdn_norm_p, w_branch_b, w_out_b)


def _ffn_kernel(h_ref, g_ref, sc_ref, sh_ref, gf_ref, wi_ref, wo_ref, fn_ref, *o_refs, final):
    h = h_ref[...]
    xf = (_rms(h, g_ref[0]) * (1.0 + sc_ref[0]) + sh_ref[0]).astype(BF16)

    def gate_up(c):
        f0 = c * TF_FFN
        return (_dot(xf, wi_ref[0, :, f0:f0 + TF_FFN]),
                _dot(xf, wi_ref[0, :, D_FF + f0:D_FF + f0 + TF_FFN]))

    n_chunks = D_FF // TF_FFN
    nxt = gate_up(0)
    acc = None
    for c in range(n_chunks):
        g, u = nxt
        if c + 1 < n_chunks:
            nxt = gate_up(c + 1)
        part = _dot((_silu(g) * u).astype(BF16), wo_ref[0, c * TF_FFN:(c + 1) * TF_FFN, :])
        acc = part if acc is None else acc + part
    out = h + gf_ref[0] * acc
    if not final:
        o_refs[0][...] = out
    else:
        out = _rms(out, fn_ref[...])
        is_ctx = pl.program_id(0) < NCTX // TM_FFN

        @pl.when(is_ctx)
        def _():
            o_refs[0][...] = out

        @pl.when(jnp.logical_not(is_ctx))
        def _():
            o_refs[1][...] = out


def _ffn(h, mod3, norm_ffn, wi_b, wo_b, final_norm, layer, final):
    tm = TM_FFN
    if final:
        nc = NCTX // tm
        out_specs = [pl.BlockSpec((tm, D_MODEL), lambda i: (jnp.minimum(i, nc - 1), 0)),
                     pl.BlockSpec((tm, D_MODEL), lambda i: (jnp.maximum(i - nc, 0), 0))]
        out_shape = [jax.ShapeDtypeStruct((NCTX, D_MODEL), F32), jax.ShapeDtypeStruct((NSMP, D_MODEL), F32)]
    else:
        out_specs = pl.BlockSpec((tm, D_MODEL), lambda i: (i, 0))
        out_shape = jax.ShapeDtypeStruct((R, D_MODEL), F32)
    return pl.pallas_call(
        functools.partial(_ffn_kernel, final=final),
        grid=(R // tm,),
        in_specs=[pl.BlockSpec((tm, D_MODEL), lambda i: (i, 0)),
                  pl.BlockSpec((1, 1, D_MODEL), lambda i: (layer, 0, 0)),
                  _mod_spec(layer, 4, tm),
                  _mod_spec(layer, 3, tm),
                  _mod_spec(layer, 5, tm),
                  pl.BlockSpec((1, D_MODEL, 2 * D_FF), lambda i: (layer, 0, 0), pipeline_mode=pl.Buffered(1)),
                  pl.BlockSpec((1, D_FF, D_MODEL), lambda i: (layer, 0, 0), pipeline_mode=pl.Buffered(1)),
                  pl.BlockSpec((1, D_MODEL), lambda i: (0, 0))],
        out_specs=out_specs, out_shape=out_shape,
        compiler_params=_cp(("arbitrary",)),
        name="ffn_final" if final else "ffn",
    )(h, norm_ffn.reshape(DEPTH, 1, D_MODEL), mod3, mod3, mod3, wi_b, wo_b, final_norm.reshape(1, D_MODEL))


def _rope_partner(w):
    h = AXIS_DIM // 2
    parts = []
    for a in range(2):
        x1 = w[..., a * AXIS_DIM:a * AXIS_DIM + h]
        x2 = w[..., a * AXIS_DIM + h:(a + 1) * AXIS_DIM]
        parts += [-x2, x1]
    return jnp.concatenate(parts, axis=-1)


def _pack_w_in(w_in):
    c = [0]
    for s in (Q_LORA, KV_LORA + ROPE_DIM, 256, 256, 512, 512, 2 * GLA_RANK, 1536, 512,
              2 * GDN_HEADS, 2 * GDN_HEADS, 3 * D_MODEL):
        c.append(c[-1] + s)
    seg = [w_in[..., c[i]:c[i + 1]] for i in range(12)]
    mq, mkv, gq, gk, gv, gr, glr, dqkv, dz, da, db, gates = seg

    def zeros(n):
        return jnp.zeros(w_in.shape[:-1] + (n,), w_in.dtype)

    kr = mkv[..., KV_LORA:]
    mla = jnp.concatenate([mq, mkv[..., :KV_LORA], kr, _rope_partner(kr), zeros(64)], axis=-1)
    dab = []
    for d in range(2):
        dab += [da[..., d * GDN_HEADS:(d + 1) * GDN_HEADS], db[..., d * GDN_HEADS:(d + 1) * GDN_HEADS],
                zeros(LANE - 2 * GDN_HEADS)]
    packed = jnp.concatenate([gates, dqkv, dz, gv, gr, mla, gq, gk, glr, zeros(LANE - 2 * GLA_RANK)]
                             + dab + [zeros(PROJ_W - PC_DAB - 2 * LANE)], axis=-1)
    return packed.astype(BF16)


def _pack_mla_weights(w_uq, w_ukv):
    l = w_uq.shape[0]
    qh = w_uq.reshape(l, Q_LORA, MLA_HEADS, MLA_NOPE + ROPE_DIM)
    zq = jnp.zeros((l, Q_LORA, MLA_HEADS, HEAD_W - MLA_NOPE - ROPE_DIM), w_uq.dtype)
    zn = jnp.zeros((l, Q_LORA, MLA_HEADS, MLA_NOPE), w_uq.dtype)
    wq = jnp.concatenate([qh, zq], axis=-1).reshape(l, Q_LORA, MLA_HEADS * HEAD_W)
    wq_sw = jnp.concatenate([zn, _rope_partner(qh[..., MLA_NOPE:]), zq], axis=-1).reshape(l, Q_LORA, MLA_HEADS * HEAD_W)
    wq_p = jnp.concatenate([wq, wq_sw], axis=-1).astype(BF16)

    kvh = w_ukv.reshape(l, KV_LORA, MLA_HEADS, MLA_NOPE + MLA_V)
    zk = jnp.zeros((l, KV_LORA, MLA_HEADS, HEAD_W - MLA_NOPE), w_ukv.dtype)
    wk_top = jnp.concatenate([kvh[..., :MLA_NOPE], zk], axis=-1).reshape(l, KV_LORA, MLA_HEADS * HEAD_W)
    place = jnp.zeros((KV_LORA, MLA_HEADS, HEAD_W), w_ukv.dtype)
    idx = jnp.arange(ROPE_DIM)
    for rep in range(2):
        place = place.at[rep * ROPE_DIM + idx, :, MLA_NOPE + idx].set(1.0)
    wk_bot = jnp.broadcast_to(place.reshape(1, KV_LORA, MLA_HEADS * HEAD_W), (l, KV_LORA, MLA_HEADS * HEAD_W))
    wk_p = jnp.concatenate([wk_top, wk_bot], axis=1).astype(BF16)
    wv_p = jnp.transpose(kvh[..., MLA_NOPE:], (0, 2, 3, 1)).reshape(l, VT_ROWS, KV_LORA).astype(BF16)
    return wq_p, wk_p, wv_p


def _rope_tables(tm):
    rows = DEC_SEQ // GRID_W
    row = jnp.repeat(jnp.arange(rows, dtype=F32), GRID_W)
    col = jnp.tile(jnp.arange(GRID_W, dtype=F32), rows)
    inv = ROPE_THETA ** (-jnp.arange(0, AXIS_DIM, 2, dtype=F32) / AXIS_DIM)
    ang_r, ang_c = row[:, None] * inv, col[:, None] * inv
    cos32 = jnp.concatenate([jnp.cos(ang_r)] * 2 + [jnp.cos(ang_c)] * 2, axis=-1)
    sin32 = jnp.concatenate([jnp.sin(ang_r)] * 2 + [jnp.sin(ang_c)] * 2, axis=-1)
    cos32 = jnp.concatenate([jnp.ones((tm, ROPE_DIM), F32), cos32], axis=0)
    sin32 = jnp.concatenate([jnp.zeros((tm, ROPE_DIM), F32), sin32], axis=0)
    n = cos32.shape[0]
    scale = (MLA_NOPE + ROPE_DIM) ** -0.5 * math.log2(math.e)
    pad = HEAD_W - MLA_NOPE - ROPE_DIM
    cq = jnp.concatenate([jnp.ones((n, MLA_NOPE), F32), cos32, jnp.zeros((n, pad), F32)], axis=-1) * scale
    sq = jnp.concatenate([jnp.zeros((n, MLA_NOPE), F32), sin32, jnp.zeros((n, pad), F32)], axis=-1) * scale
    ck = jnp.concatenate([cos32, sin32, jnp.zeros((n, HEAD_W - 2 * ROPE_DIM), F32)], axis=-1)
    return jnp.concatenate([cq, sq, ck], axis=-1)


def _pad_lanes(x, n):
    return jnp.pad(x, [(0, 0)] * (x.ndim - 1) + [(0, n - x.shape[-1])])


def kernel(x_prompt, x_sample, cache_mla, state_gla, state_gdn, c, c_ctx, w_mod, b_mod, norm_mix, w_in,
           mla_q_norm, mla_w_uq, mla_kv_norm, mla_w_ukv, gla_w_gate, gla_b_gate, gla_norm, gdn_conv,
           gdn_a_log, gdn_dt_bias, gdn_norm, w_branch, w_out, norm_ffn, ffn_w_in, ffn_w_out, final_norm):
    w_in_p = _pack_w_in(w_in)
    wq_p, wk_p, wv_p = _pack_mla_weights(mla_w_uq, mla_w_ukv)
    tab = _rope_tables(TM_PREP)
    wg_p = jnp.zeros((DEPTH, 2, LANE, GLA_HEADS * GLA_DK), F32)
    for d in range(2):
        wg_p = wg_p.at[:, d, d * GLA_RANK:(d + 1) * GLA_RANK, :].set(gla_w_gate[:, d])
    wg_p = wg_p.astype(BF16)
    bg_p = gla_b_gate.reshape(DEPTH, 2, 1, GLA_HEADS * GLA_DK)
    alog_p = _pad_lanes(gdn_a_log, LANE).reshape(DEPTH, 2, 1, LANE)
    dtb_p = _pad_lanes(gdn_dt_bias, LANE).reshape(DEPTH, 2, 1, LANE)
    gla_norm_p = jnp.tile(gla_norm, (1, GLA_HEADS)).reshape(DEPTH, 1, GLA_HEADS * GLA_DV)
    gdn_norm_p = jnp.tile(gdn_norm, (1, GDN_HEADS)).reshape(DEPTH, 1, GDN_HEADS * GDN_DV)
    w_branch_b = w_branch.astype(BF16)
    w_out_b = w_out.astype(BF16)
    wi_b = ffn_w_in.astype(BF16)
    wo_b = ffn_w_out.astype(BF16)
    cache_p = _pad_lanes(cache_mla, 2 * KV_LORA)
    cond8 = jnp.concatenate([c_ctx[None, :], c, jnp.zeros((MOD_ROWS - 1 - DEC_BATCH, D_MODEL), F32)], axis=0)

    mod = _modulation(cond8, w_mod, b_mod)
    mod3 = mod.reshape(DEPTH * MOD_ROWS * 6, 1, D_MODEL)
    kc, vc = _kv_cache(cache_p, wk_p, wv_p)

    h = (x_prompt.reshape(NCTX, D_MODEL), x_sample.reshape(NSMP, D_MODEL))
    kv_list, gla_list, gdn_list = [], [], []
    for l in range(DEPTH):
        proj = _inproj(h, mod3, norm_mix, w_in_p, l)
        qp, kp, vtp, own = _mla_prep(proj, tab, mla_q_norm, mla_kv_norm, wq_p, wk_p, wv_p, l)
        y_mla = (_attention_ctx(qp, kp, vtp), _attention_smp(qp, kp, vtp, kc, vc, l))
        ogf_c, ogb_c, sg = _gla(proj, wg_p, bg_p, None, l, True)
        ogf_s, ogb_s = _gla(proj, wg_p, bg_p, state_gla, l, False)
        qkv = _gdn_pre(proj, gdn_conv, l)
        odf_c, odb_c, sd = _gdn(qkv, proj, alog_p, dtb_p, None, l, True)
        odf_s, odb_s = _gdn(qkv, proj, alog_p, dtb_p, state_gdn, l, False)
        h = _merge(y_mla, ((ogf_c, ogb_c), (ogf_s, ogb_s)), ((odf_c, odb_c), (odf_s, odb_s)), proj, h, mod3,
                   gla_norm_p, gdn_norm_p,
                   w_branch_b, w_out_b, l)
        h = _ffn(h, mod3, norm_ffn, wi_b, wo_b, final_norm, l, l == DEPTH - 1)
        kv_list.append(own[:NCTX, :KV_LORA + ROPE_DIM].reshape(BATCH, SEQ, KV_LORA + ROPE_DIM))
        gla_list.append(sg)
        gdn_list.append(sd)

    y_prompt = h[0].reshape(BATCH, SEQ, D_MODEL)
    y_sample = h[1].reshape(DEC_BATCH, DEC_SEQ, D_MODEL)
    return (y_prompt, y_sample, jnp.stack(kv_list, axis=1), jnp.stack(gla_list, axis=1),
            jnp.stack(gdn_list, axis=1))
```

```python
import functools
import math

import jax
import jax.numpy as jnp
from jax import lax
from jax.experimental import pallas as pl
from jax.experimental.pallas import tpu as pltpu

F32 = jnp.float32
BF16 = jnp.bfloat16

D_MODEL = 1024
BATCH = 16
SEQ = 256
DEPTH = 2
DEC_BATCH = 2
DEC_SEQ = 4096
PAST_LEN = 256
GRID_W = 64
CHUNK = 64
EPS = 1e-6
MLA_HEADS = 8
MLA_NOPE = 64
ROPE_DIM = 32
AXIS_DIM = ROPE_DIM // 2
MLA_V = 64
Q_LORA = 256
KV_LORA = 128
ROPE_THETA = 10000.0
GLA_HEADS = 4
GLA_DK = 64
GLA_DV = 128
GLA_RANK = 16
GLA_TAU = 16.0
GDN_HEADS = 8
GDN_DK = 64
GDN_DV = 64
D_FF = ((8 * D_MODEL + 3 * 256 - 1) // (3 * 256)) * 256
MOD_W = 6 * D_MODEL

NCTX = BATCH * SEQ
NSMP = DEC_BATCH * DEC_SEQ
R = NCTX + NSMP
LANE = 128
HEAD_W = 128
VT_ROWS = MLA_HEADS * MLA_V
SEQ_BLK = 256
NCH = SEQ_BLK // CHUNK
GLA_BLK_CTX = 256
GLA_BLK_SMP = 512
MOD_ROWS = 8

PC_GATES = 0
PC_DQKV = 3072
PC_DZ = 4608
PC_GV = 5120
PC_GR = 5632
PC_MLA = 6144
PC_GQ = 6656
PC_GK = 6912
PC_GLR = 7168
PC_DAB = 7296
PROJ_W = 7680

TM_IN = 1024
TN_IN = 1536
TM_PREP = 512
TM_MERGE = 512
TM_FFN = 512
TF_FFN = 256
TQ_ATT = 256
TK_ATT = 512
ONES_ROWS = 16
ATT_AHEAD = 8
VMEM_LIMIT = 48 * 1024 * 1024


def _cp(sem):
    return pltpu.CompilerParams(dimension_semantics=sem, vmem_limit_bytes=VMEM_LIMIT)


def _dot(a, b):
    return jnp.dot(a, b, preferred_element_type=F32)


def _dot_nt(a, b):
    return lax.dot_general(a, b, (((1,), (1,)), ((), ())), preferred_element_type=F32)


def _dot_tn(a, b):
    return lax.dot_general(a, b, (((0,), (0,)), ((), ())), preferred_element_type=F32)


def _split2(x):
    hi = x.astype(BF16)
    lo = (x - hi.astype(F32)).astype(BF16)
    return hi, lo


def _split3(x):
    hi = x.astype(BF16)
    r1 = x - hi.astype(F32)
    mid = r1.astype(BF16)
    lo = (r1 - mid.astype(F32)).astype(BF16)
    return hi, mid, lo


def _cumsum_rows(tri_b, x):
    hi, mid, lo = _split3(x)
    return _dot(tri_b, hi) + (_dot(tri_b, mid) + _dot(tri_b, lo))


def _sigmoid(x):
    return 0.5 * jnp.tanh(0.5 * x) + 0.5


def _silu(x):
    return x * _sigmoid(x)


def _softplus(x):
    return jnp.maximum(x, 0.0) + jnp.log(1.0 + jnp.exp(-jnp.abs(x)))


def _log_sigmoid(x):
    return jnp.minimum(x, 0.0) - jnp.log(1.0 + jnp.exp(-jnp.abs(x)))


def _rms(x, g):
    return x * lax.rsqrt(jnp.mean(x * x, axis=-1, keepdims=True) + EPS) * g


def _mod_row(row_start):
    return jnp.where(row_start < NCTX, 0, 1 + (row_start - NCTX) // DEC_SEQ)


def _mod_spec(layer, which, tm, axis=0):
    def imap(*idx):
        return ((layer * MOD_ROWS + _mod_row(idx[axis] * tm)) * 6 + which, 0, 0)
    return pl.BlockSpec((1, 1, D_MODEL), imap)


def _mod_kernel(c_ref, w_ref, b_ref, o_ref):
    x = _silu(c_ref[...])
    xh, xl = _split2(x)
    wh, wl = _split2(w_ref[0])
    o_ref[0] = _dot(xh, wh) + (_dot(xl, wh) + _dot(xh, wl)) + b_ref[0]


def _modulation(cond8, w_mod, b_mod):
    tn = 1536
    return pl.pallas_call(
        _mod_kernel,
        grid=(DEPTH, MOD_W // tn),
        in_specs=[pl.BlockSpec((MOD_ROWS, D_MODEL), lambda l, j: (0, 0)),
                  pl.BlockSpec((1, D_MODEL, tn), lambda l, j: (l, 0, j)),
                  pl.BlockSpec((1, 1, tn), lambda l, j: (l, 0, j))],
        out_specs=pl.BlockSpec((1, MOD_ROWS, tn), lambda l, j: (l, 0, j)),
        out_shape=jax.ShapeDtypeStruct((DEPTH, MOD_ROWS, MOD_W), F32),
        compiler_params=_cp(("arbitrary", "arbitrary")),
        name="modulation",
    )(cond8, w_mod, b_mod.reshape(DEPTH, 1, MOD_W))


def _family_specs(tm, width, axis=0):
    nc = NCTX // tm
    return [pl.BlockSpec((tm, width), lambda *idx: (jnp.minimum(idx[axis], nc - 1), 0)),
            pl.BlockSpec((tm, width), lambda *idx: (jnp.maximum(idx[axis] - nc, 0), 0))]


def _family_rows(refs, tm, axis=0):
    is_ctx = pl.program_id(axis) < NCTX // tm
    return jnp.where(is_ctx, refs[0][...], refs[1][...])


def _inproj_kernel(*refs, split):
    if split:
        hc_ref, hs_ref, g_ref, sc_ref, sh_ref, w_ref, o_ref = refs
        x = _family_rows((hc_ref, hs_ref), TM_IN, axis=1)
    else:
        h_ref, g_ref, sc_ref, sh_ref, w_ref, o_ref = refs
        x = h_ref[...]
    y = _rms(x, g_ref[0])
    xn = (y * (1.0 + sc_ref[0]) + sh_ref[0]).astype(BF16)
    o_ref[...] = _dot(xn, w_ref[0]).astype(BF16)


def _inproj(h, mod3, norm_mix, w_in_p, layer):
    split = isinstance(h, tuple)
    h_specs = (_family_specs(TM_IN, D_MODEL, axis=1) if split
               else [pl.BlockSpec((TM_IN, D_MODEL), lambda j, i: (i, 0))])
    h_args = list(h) if split else [h]
    return pl.pallas_call(
        functools.partial(_inproj_kernel, split=split),
        grid=(PROJ_W // TN_IN, R // TM_IN),
        in_specs=h_specs + [
                  pl.BlockSpec((1, 1, D_MODEL), lambda j, i: (layer, 0, 0)),
                  _mod_spec(layer, 1, TM_IN, axis=1),
                  _mod_spec(layer, 0, TM_IN, axis=1),
                  pl.BlockSpec((1, D_MODEL, TN_IN), lambda j, i: (layer, 0, j))],
        out_specs=pl.BlockSpec((TM_IN, TN_IN), lambda j, i: (i, j)),
        out_shape=jax.ShapeDtypeStruct((R, PROJ_W), BF16),
        compiler_params=_cp(("arbitrary", "arbitrary")),
        name="inproj",
    )(*h_args, norm_mix.reshape(DEPTH, 1, D_MODEL), mod3, mod3, w_in_p)


def _mla_prep_kernel(pm_ref, tab_ref, qn_ref, kvn_ref, wq_ref, wk_ref, wv_ref,
                     q_ref, k_ref, v_ref, own_ref):
    pm = pm_ref[...].astype(F32)
    tab = tab_ref[...]
    qn = _rms(pm[:, :Q_LORA], qn_ref[0]).astype(BF16)
    q2 = _dot(qn, wq_ref[0])
    hw = MLA_HEADS * HEAD_W
    cq = jnp.tile(tab[:, :HEAD_W], (1, MLA_HEADS))
    sq = jnp.tile(tab[:, HEAD_W:2 * HEAD_W], (1, MLA_HEADS))
    q_ref[...] = (q2[:, :hw] * cq + q2[:, hw:] * sq).astype(BF16)
    ckv = _rms(pm[:, Q_LORA:Q_LORA + KV_LORA], kvn_ref[0])
    kr = pm[:, Q_LORA + KV_LORA:] * tab[:, 2 * HEAD_W:]
    lhs = jnp.concatenate([ckv, kr], axis=1)
    own_ref[...] = lhs
    lb = lhs.astype(BF16)
    k_ref[...] = _dot(lb, wk_ref[0]).astype(BF16)
    v_ref[...] = _dot_nt(wv_ref[0], lb[:, :KV_LORA]).astype(BF16)


def _mla_prep(proj, tab, q_norm, kv_norm, wq_p, wk_p, wv_p, layer):
    tm = TM_PREP
    hw = MLA_HEADS * HEAD_W

    def tab_map(i):
        r0 = i * tm
        return (jnp.where(r0 < NCTX, 0, 1 + ((r0 - NCTX) % DEC_SEQ) // tm), 0)

    return pl.pallas_call(
        _mla_prep_kernel,
        grid=(R // tm,),
        in_specs=[pl.BlockSpec((tm, 512), lambda i: (i, PC_MLA // 512)),
                  pl.BlockSpec((tm, 3 * HEAD_W), tab_map),
                  pl.BlockSpec((1, 1, Q_LORA), lambda i: (layer, 0, 0)),
                  pl.BlockSpec((1, 1, KV_LORA), lambda i: (layer, 0, 0)),
                  pl.BlockSpec((1, Q_LORA, 2 * hw), lambda i: (layer, 0, 0)),
                  pl.BlockSpec((1, 2 * KV_LORA, hw), lambda i: (layer, 0, 0)),
                  pl.BlockSpec((1, VT_ROWS, KV_LORA), lambda i: (layer, 0, 0))],
        out_specs=[pl.BlockSpec((tm, hw), lambda i: (i, 0)),
                   pl.BlockSpec((tm, hw), lambda i: (i, 0)),
                   pl.BlockSpec((VT_ROWS, tm), lambda i: (0, i)),
                   pl.BlockSpec((tm, 2 * KV_LORA), lambda i: (i, 0))],
        out_shape=[jax.ShapeDtypeStruct((R, hw), BF16),
                   jax.ShapeDtypeStruct((R, hw), BF16),
                   jax.ShapeDtypeStruct((VT_ROWS, R), BF16),
                   jax.ShapeDtypeStruct((R, 2 * KV_LORA), F32)],
        compiler_params=_cp(("arbitrary",)),
        name="mla_prep",
    )(proj, tab, q_norm.reshape(DEPTH, 1, Q_LORA), kv_norm.reshape(DEPTH, 1, KV_LORA),
      wq_p, wk_p, wv_p)


def _kv_cache_kernel(c_ref, wk_ref, wv_ref, k_ref, v_ref):
    lb = c_ref[0, 0].astype(BF16)
    k_ref[0, 0] = _dot(lb, wk_ref[0]).astype(BF16)
    v_ref[0, 0] = _dot_nt(wv_ref[0], lb[:, :KV_LORA]).astype(BF16)


def _kv_cache(cache_p, wk_p, wv_p):
    hw = MLA_HEADS * HEAD_W
    return pl.pallas_call(
        _kv_cache_kernel,
        grid=(DEC_BATCH, DEPTH),
        in_specs=[pl.BlockSpec((1, 1, PAST_LEN, 2 * KV_LORA), lambda b, l: (b, l, 0, 0)),
                  pl.BlockSpec((1, 2 * KV_LORA, hw), lambda b, l: (l, 0, 0)),
                  pl.BlockSpec((1, VT_ROWS, KV_LORA), lambda b, l: (l, 0, 0))],
        out_specs=[pl.BlockSpec((1, 1, PAST_LEN, hw), lambda b, l: (b, l, 0, 0)),
                   pl.BlockSpec((1, 1, VT_ROWS, PAST_LEN), lambda b, l: (b, l, 0, 0))],
        out_shape=[jax.ShapeDtypeStruct((DEC_BATCH, DEPTH, PAST_LEN, hw), BF16),
                   jax.ShapeDtypeStruct((DEC_BATCH, DEPTH, VT_ROWS, PAST_LEN), BF16)],
        compiler_params=_cp(("arbitrary", "arbitrary")),
        name="kv_cache",
    )(cache_p, wk_p, wv_p)


def _attn_kernel(*refs, heads, has_cache):
    if has_cache:
        q_ref, k_ref, vt_ref, kc_ref, vct_ref, o_ref = refs
    else:
        q_ref, k_ref, vt_ref, o_ref = refs
    n_keys = k_ref.shape[0]
    tk = min(TK_ATT, n_keys)
    tiles = ([("cache", 0, PAST_LEN)] if has_cache else []) + [("own", s0, tk) for s0 in range(0, n_keys, tk)]
    units = [(h, t) for t in range(len(tiles)) for h in range(heads)]
    qs = [q_ref[:, h * HEAD_W:(h + 1) * HEAD_W] for h in range(heads)]

    def scores(h, t):
        kind, s0, n = tiles[t]
        sl = slice(h * HEAD_W, (h + 1) * HEAD_W)
        keys = kc_ref[0, 0, :, sl] if kind == "cache" else k_ref[s0:s0 + n, sl]
        return _dot_nt(keys, qs[h])

    def values_t(h, t):
        kind, s0, n = tiles[t]
        sv = slice(h * MLA_V, (h + 1) * MLA_V)
        vt = vct_ref[0, 0, sv, :] if kind == "cache" else vt_ref[sv, s0:s0 + n]
        return jnp.concatenate([vt, jnp.ones((ONES_ROWS, n), BF16)], axis=0)

    ahead = ATT_AHEAD
    pending = {u: scores(*u) for u in units[:ahead]}
    m, acc = [None] * heads, [None] * heads
    for i, (h, t) in enumerate(units):
        if i + ahead < len(units):
            pending[units[i + ahead]] = scores(*units[i + ahead])
        st = pending.pop((h, t))
        mt = jnp.max(st, axis=0, keepdims=True)
        if m[h] is None:
            m[h] = mt
            acc[h] = _dot(values_t(h, t), jnp.exp2(st - mt).astype(BF16))
        else:
            m_new = jnp.maximum(m[h], mt)
            alpha = jnp.exp2(m[h] - m_new)
            acc[h] = alpha * acc[h] + _dot(values_t(h, t), jnp.exp2(st - m_new).astype(BF16))
            m[h] = m_new
    outs = [acc[h][:MLA_V] / acc[h][MLA_V:MLA_V + 1] for h in range(heads)]
    o_ref[...] = jnp.concatenate(outs, axis=0).T.astype(BF16)


def _attention_ctx(qp, kp, vtp):
    hw = MLA_HEADS * HEAD_W
    return pl.pallas_call(
        functools.partial(_attn_kernel, heads=MLA_HEADS, has_cache=False),
        grid=(BATCH,),
        in_specs=[pl.BlockSpec((SEQ, hw), lambda s: (s, 0)),
                  pl.BlockSpec((SEQ, hw), lambda s: (s, 0)),
                  pl.BlockSpec((VT_ROWS, SEQ), lambda s: (0, s))],
        out_specs=pl.BlockSpec((SEQ, MLA_HEADS * MLA_V), lambda s: (s, 0)),
        out_shape=jax.ShapeDtypeStruct((NCTX, MLA_HEADS * MLA_V), BF16),
        compiler_params=_cp(("arbitrary",)),
        name="attn_ctx",
    )(qp, kp, vtp)


def _attention_smp(qp, kp, vtp, kc, vct, layer):
    hpb = 4
    wq = hpb * HEAD_W
    wv = hpb * MLA_V
    nq = DEC_SEQ // TQ_ATT
    qoff = NCTX // TQ_ATT
    koff = NCTX // DEC_SEQ
    return pl.pallas_call(
        functools.partial(_attn_kernel, heads=hpb, has_cache=True),
        grid=(DEC_BATCH, MLA_HEADS // hpb, nq),
        in_specs=[pl.BlockSpec((TQ_ATT, wq), lambda b, g, i: (qoff + b * nq + i, g)),
                  pl.BlockSpec((DEC_SEQ, wq), lambda b, g, i: (koff + b, g)),
                  pl.BlockSpec((wv, DEC_SEQ), lambda b, g, i: (g, koff + b)),
                  pl.BlockSpec((1, 1, PAST_LEN, wq), lambda b, g, i: (b, layer, 0, g)),
                  pl.BlockSpec((1, 1, wv, PAST_LEN), lambda b, g, i: (b, layer, g, 0))],
        out_specs=pl.BlockSpec((TQ_ATT, hpb * MLA_V), lambda b, g, i: (b * nq + i, g)),
        out_shape=jax.ShapeDtypeStruct((NSMP, MLA_HEADS * MLA_V), BF16),
        compiler_params=_cp(("arbitrary", "arbitrary", "arbitrary")),
        name="attn_smp",
    )(qp, kp, vtp, kc, vct)


def _scan_specs(ctx, blk, widths_and_cols, dir_cols=()):
    nseq, seq_len = (BATCH, SEQ) if ctx else (DEC_BATCH, DEC_SEQ)
    nblk = seq_len // blk
    roff = 0 if ctx else NCTX // blk

    def rb(dd):
        return lambda s, j: roff + s * nblk + (j if dd == 0 else nblk - 1 - j)

    specs = []
    for dd in range(2):
        r = rb(dd)
        specs += [pl.BlockSpec((blk, w), lambda s, j, r=r, c=c: (r(s, j), c)) for w, c in widths_and_cols]
        specs += [pl.BlockSpec((blk, w), lambda s, j, r=r, c=c + dd: (r(s, j), c)) for w, c in dir_cols]
    out_maps = [lambda s, j, r=rb(dd): (r(s, j) - roff, 0) for dd in range(2)]
    return nseq, nblk, specs, out_maps


def _gla_kernel(*refs, nblk, has_s0, nch):
    per_dir = [refs[0:4], refs[4:8]]
    wg_ref, bg_ref = refs[8:10]
    if has_s0:
        s0_ref, of_ref, ob_ref, s_scr = refs[10:]
    else:
        of_ref, ob_ref, sfin_ref, s_scr = refs[10:]
    o_refs = [of_ref, ob_ref]
    j = pl.program_id(1)

    @pl.when(j == 0)
    def _():
        if has_s0:
            s_scr[...] = s0_ref[0, 0]
        else:
            s_scr[...] = jnp.zeros_like(s_scr)

    r64 = lax.broadcasted_iota(jnp.int32, (CHUNK, CHUNK), 0)
    c64 = lax.broadcasted_iota(jnp.int32, (CHUNK, CHUNK), 1)
    incl = [c64 <= r64, c64 >= r64]
    tri_b = [jnp.where(m, 1.0, 0.0).astype(BF16) for m in incl]
    heads = [(slice(h * GLA_DK, (h + 1) * GLA_DK), slice(h * GLA_DV, (h + 1) * GLA_DV))
             for h in range(GLA_HEADS)]

    ch = []
    for c in range(nch):
        for dd in range(2):
            ci = c if dd == 0 else nch - 1 - c
            rows = slice(ci * CHUNK, (ci + 1) * CHUNK)
            glr_ref = per_dir[dd][3]
            ch.append(dict(dd=dd, rows=rows, z=_dot(glr_ref[rows, :], wg_ref[0, dd]) + bg_ref[0, dd]))
    for t in ch:
        t["la"] = _split3(_log_sigmoid(t["z"]) * (1.0 / GLA_TAU))
    for t in ch:
        hi, mid, lo = t["la"]
        tb = tri_b[t["dd"]]
        t["bc"] = _dot(tb, hi) + (_dot(tb, mid) + _dot(tb, lo))
    for t in ch:
        q_ref, k_ref, v_ref, _ = per_dir[t["dd"]]
        rows, bc = t["rows"], t["bc"]
        q = q_ref[rows, :].astype(F32) * (GLA_DK ** -0.5)
        k = k_ref[rows, :].astype(F32)
        bl = bc[CHUNK - 1:CHUNK] if t["dd"] == 0 else bc[0:1]
        t["qd"] = (q * jnp.exp(bc)).astype(BF16)
        t["ki"] = (k * jnp.exp(-bc)).astype(BF16)
        t["ke"] = (k * jnp.exp(bl - bc)).astype(BF16)
        t["dec_t"] = jnp.transpose(jnp.broadcast_to(jnp.exp(bl), (LANE, GLA_HEADS * GLA_DK)))
        t["v"] = v_ref[rows, :]
    for t in ch:
        t["a"] = [jnp.where(incl[t["dd"]], _dot_nt(t["qd"][:, sk], t["ki"][:, sk]), 0.0).astype(BF16)
                  for sk, _ in heads]
    for t in ch:
        t["oi"] = [_dot(t["a"][h], t["v"][:, sv]) for h, (_, sv) in enumerate(heads)]
        t["upd"] = [_dot_tn(t["ke"][:, sk], t["v"][:, sv]) for sk, sv in heads]
    states = [[s_scr[dd, h] for h in range(GLA_HEADS)] for dd in range(2)]
    for t in ch:
        st = states[t["dd"]]
        t["s_in"] = [s.astype(BF16) for s in st]
        for h, (sk, _) in enumerate(heads):
            st[h] = t["dec_t"][sk, :] * st[h] + t["upd"][h]
    for t in ch:
        for h, (sk, sv) in enumerate(heads):
            o_refs[t["dd"]][t["rows"], sv] = t["oi"][h] + _dot(t["qd"][:, sk], t["s_in"][h])
    for dd in range(2):
        for h in range(GLA_HEADS):
            s_scr[dd, h] = states[dd][h]

    if not has_s0:
        @pl.when(j == nblk - 1)
        def _():
            sfin_ref[0] = s_scr[...]


def _gla(proj, wg_p, bg_p, s0, layer, ctx):
    blk = GLA_BLK_CTX if ctx else GLA_BLK_SMP
    hk = GLA_HEADS * GLA_DK
    hv = GLA_HEADS * GLA_DV
    nseq, nblk, in_specs, out_maps = _scan_specs(
        ctx, blk, [(hk, PC_GQ // hk), (hk, PC_GK // hk), (hv, PC_GV // hv), (LANE, PC_GLR // LANE)])
    in_specs += [pl.BlockSpec((1, 2, LANE, hk), lambda s, j: (layer, 0, 0, 0)),
                 pl.BlockSpec((1, 2, 1, hk), lambda s, j: (layer, 0, 0, 0))]
    args = [proj] * 8 + [wg_p, bg_p]
    nrows = NCTX if ctx else NSMP
    out_specs = [pl.BlockSpec((blk, hv), m) for m in out_maps]
    out_shape = [jax.ShapeDtypeStruct((nrows, hv), F32)] * 2
    st_blk = (2, GLA_HEADS, GLA_DK, GLA_DV)
    if ctx:
        out_specs.append(pl.BlockSpec((1,) + st_blk, lambda s, j: (s, 0, 0, 0, 0)))
        out_shape.append(jax.ShapeDtypeStruct((BATCH,) + st_blk, F32))
    else:
        in_specs.append(pl.BlockSpec((1, 1) + st_blk, lambda s, j: (s, layer, 0, 0, 0, 0)))
        args.append(s0)
    return pl.pallas_call(
        functools.partial(_gla_kernel, nblk=nblk, has_s0=not ctx, nch=blk // CHUNK),
        grid=(nseq, nblk),
        in_specs=in_specs, out_specs=out_specs, out_shape=out_shape,
        scratch_shapes=[pltpu.VMEM(st_blk, F32)],
        compiler_params=_cp(("arbitrary", "arbitrary")),
        name="gla_ctx" if ctx else "gla_smp",
    )(*args)


GDN_G = 2
GDN_GW = GDN_G * GDN_DK
HALO = 16


def _gdn_pre_kernel(x_ref, xp_ref, xn_ref, cw_ref, o_ref, *, blocks_per_seq_ctx, blocks_per_seq_smp):
    i = pl.program_id(0)
    nctx = NCTX // SEQ_BLK
    bi = jnp.where(i < nctx, i % blocks_per_seq_ctx, (i - nctx) % blocks_per_seq_smp)
    nblk = jnp.where(i < nctx, blocks_per_seq_ctx, blocks_per_seq_smp)
    hk = GDN_HEADS * GDN_DK
    x = x_ref[...].astype(F32)
    rowi = lax.broadcasted_iota(jnp.int32, x.shape, 0)
    prev_row = jnp.where(bi > 0, xp_ref[...].astype(F32)[HALO - 1:HALO], 0.0)
    next_row = jnp.where(bi < nblk - 1, xn_ref[...].astype(F32)[0:1], 0.0)
    x_prev = jnp.where(rowi == 0, prev_row, pltpu.roll(x, 1, 0))
    x_next = jnp.where(rowi == SEQ_BLK - 1, next_row, pltpu.roll(x, SEQ_BLK - 1, 0))
    cw = cw_ref[0]
    y = _silu(x_prev * cw[0:1] + x * cw[1:2] + x_next * cw[2:3])

    lane = lax.broadcasted_iota(jnp.int32, (SEQ_BLK, LANE), 1)
    lo_half = lane < GDN_DK

    def l2n(t, scale):
        parts = []
        for c in range(hk // LANE):
            tc = t[:, c * LANE:(c + 1) * LANE]
            sq = tc * tc
            s_lo = jnp.sum(jnp.where(lo_half, sq, 0.0), axis=-1, keepdims=True)
            s_hi = jnp.sum(jnp.where(lo_half, 0.0, sq), axis=-1, keepdims=True)
            parts.append(tc * (lax.rsqrt(jnp.where(lo_half, s_lo, s_hi) + EPS) * scale))
        return jnp.concatenate(parts, axis=1)

    o_ref[:, :hk] = l2n(y[:, :hk], GDN_DK ** -0.5).astype(BF16)
    o_ref[:, hk:2 * hk] = l2n(y[:, hk:2 * hk], 1.0).astype(BF16)
    o_ref[:, 2 * hk:] = y[:, 2 * hk:].astype(BF16)


def _gdn_pre(proj, conv_w, layer):
    cw = 2 * GDN_HEADS * GDN_DK + GDN_HEADS * GDN_DV
    halo = SEQ_BLK // HALO
    last_h = R // HALO - 1
    return pl.pallas_call(
        functools.partial(_gdn_pre_kernel, blocks_per_seq_ctx=SEQ // SEQ_BLK,
                          blocks_per_seq_smp=DEC_SEQ // SEQ_BLK),
        grid=(R // SEQ_BLK,),
        in_specs=[pl.BlockSpec((SEQ_BLK, cw), lambda i: (i, PC_DQKV // cw)),
                  pl.BlockSpec((HALO, cw), lambda i: (jnp.maximum(i * halo - 1, 0), PC_DQKV // cw)),
                  pl.BlockSpec((HALO, cw), lambda i: (jnp.minimum((i + 1) * halo, last_h), PC_DQKV // cw)),
                  pl.BlockSpec((1, 3, cw), lambda i: (layer, 0, 0))],
        out_specs=pl.BlockSpec((SEQ_BLK, cw), lambda i: (i, 0)),
        out_shape=jax.ShapeDtypeStruct((R, cw), BF16),
        compiler_params=_cp(("arbitrary",)),
        name="gdn_pre",
    )(proj, proj, proj, conv_w)


def _gdn_kernel(*refs, nblk, has_s0):
    per_dir = [refs[0:4], refs[4:8]]
    alog_ref, dtb_ref = refs[8:10]
    if has_s0:
        s0_ref, of_ref, ob_ref, s_scr = refs[10:]
    else:
        of_ref, ob_ref, sfin_ref, s_scr = refs[10:]
    o_refs = [of_ref, ob_ref]
    j = pl.program_id(1)
    ngrp = GDN_HEADS // GDN_G

    def head_block(h):
        g, hh = divmod(h, GDN_G)
        return g, slice(hh * GDN_DK, (hh + 1) * GDN_DK), slice(hh * GDN_DV, (hh + 1) * GDN_DV)

    @pl.when(j == 0)
    def _():
        s_scr[...] = jnp.zeros_like(s_scr)
        if has_s0:
            for dd in range(2):
                for h in range(GDN_HEADS):
                    g, rk, rv = head_block(h)
                    s_scr[dd, g, rk, rv] = s0_ref[0, 0, dd, h]

    row = lax.broadcasted_iota(jnp.int32, (CHUNK, GDN_GW), 0)
    col = lax.broadcasted_iota(jnp.int32, (CHUNK, GDN_GW), 1) % CHUNK
    r64 = lax.broadcasted_iota(jnp.int32, (CHUNK, CHUNK), 0)
    c64 = lax.broadcasted_iota(jnp.int32, (CHUNK, CHUNK), 1)
    dir_masks = []
    for dd in range(2):
        lag = (row - col) if dd == 0 else (col - row)
        tri = (c64 <= r64) if dd == 0 else (c64 >= r64)
        dir_masks.append(dict(incl=lag >= 0, strict=lag > 0,
                              tri_b=jnp.where(tri, 1.0, 0.0).astype(BF16)))
    eye = col == row
    eye_f = jnp.where(eye, 1.0, 0.0)
    blk8 = (row // 8) == (col // 8)
    blk16 = (row // 16) == (col // 16)
    blk32 = (row // 32) == (col // 32)
    brow = lax.broadcasted_iota(jnp.int32, (GDN_GW, GDN_GW), 0) // CHUNK
    bcol = lax.broadcasted_iota(jnp.int32, (GDN_GW, GDN_GW), 1) // CHUNK
    same_head = brow == bcol

    def bdiag(t):
        tb = t.astype(BF16)
        return jnp.where(same_head, jnp.concatenate([tb] * GDN_G, axis=0), jnp.zeros((), BF16))

    def bdmm(a, b):
        return _dot(a.astype(BF16), bdiag(b))

    def setup(dd, c):
        q_ref, k_ref, v_ref, ab_ref = per_dir[dd]
        mk = dir_masks[dd]
        ci = c if dd == 0 else NCH - 1 - c
        rows = slice(ci * CHUNK, (ci + 1) * CHUNK)
        ab = ab_ref[rows, :].astype(F32)
        neg_a = -jnp.exp(alog_ref[0, dd])
        gam = _cumsum_rows(mk["tri_b"], neg_a * _softplus(ab + dtb_ref[0, dd]))
        beta_all = _sigmoid(ab)
        out = []
        for g in range(ngrp):
            gl = slice(g * GDN_GW, (g + 1) * GDN_GW)
            gcol = jnp.concatenate(
                [jnp.broadcast_to(gam[:, g * GDN_G + h:g * GDN_G + h + 1], (CHUNK, GDN_DK))
                 for h in range(GDN_G)], axis=1)
            bcol_ = jnp.concatenate(
                [jnp.broadcast_to(beta_all[:, GDN_HEADS + g * GDN_G + h:GDN_HEADS + g * GDN_G + h + 1],
                                  (CHUNK, GDN_DK)) for h in range(GDN_G)], axis=1)
            grow = jnp.sum(jnp.where(eye, gcol, 0.0), axis=0, keepdims=True)
            glast = gcol[CHUNK - 1:CHUNK] if dd == 0 else gcol[0:1]
            decay = jnp.where(mk["incl"], jnp.exp(gcol - grow), 0.0)
            egc = jnp.exp(gcol)
            qg = q_ref[rows, gl].astype(F32)
            kg = k_ref[rows, gl].astype(F32)
            vg = v_ref[rows, gl].astype(F32)
            kq = _dot_nt(jnp.concatenate([k_ref[rows, gl], q_ref[rows, gl]], axis=0),
                         bdiag(k_ref[rows, gl]))
            out.append(dict(dd=dd, g=g, rows=rows, gl=gl,
                            m=jnp.where(mk["strict"], bcol_ * kq[:CHUNK] * decay, 0.0),
                            aqk=(kq[CHUNK:] * decay).astype(BF16),
                            vb=vg * bcol_, kb=kg * (bcol_ * egc), qe=qg * egc,
                            kend=(kg * jnp.exp(glast - gcol)).astype(BF16),
                            eg=jnp.exp(glast)))
        return out

    def stages(ch):
        pairs = list(zip(ch[0::2], ch[1::2]))

        def bdmm_pairs(lhs, rhs, out):
            for ta, tb in pairs:
                a = jnp.concatenate([ta[lhs].astype(BF16), tb[lhs].astype(BF16)], axis=0)
                w = jnp.concatenate([bdiag(ta[rhs]), bdiag(tb[rhs])], axis=1)
                y = _dot(a, w)
                ta[out] = y[:CHUNK, :GDN_GW]
                tb[out] = y[CHUNK:, GDN_GW:]

        def neumann_a():
            for t in ch:
                t["n8"] = jnp.where(blk8, t["m"], 0.0)
            bdmm_pairs("n8", "n8", "n2")

        def neumann_b():
            for t in ch:
                t["p1"] = eye_f - t["n8"]
            bdmm_pairs("n2", "n2", "n4")
            bdmm_pairs("p1", "n2", "p1n2")
            for t in ch:
                t["p1"] = t["p1"] + t["p1n2"]

        def neumann_c():
            bdmm_pairs("p1", "n4", "p1n4")
            for t in ch:
                t["dinv"] = t["p1"] + t["p1n4"]

        def double_a(off):
            def run():
                for t in ch:
                    t["off"] = jnp.where(off, t["m"], 0.0)
                bdmm_pairs("dinv", "off", "dl")
            return run

        def double_b():
            bdmm_pairs("dl", "dinv", "dld")
            for t in ch:
                t["dinv"] = t["dinv"] - t["dld"]

        def solve():
            for t in ch:
                uw = _dot(t["dinv"].astype(BF16), jnp.concatenate([bdiag(t["vb"]), bdiag(t["kb"])], axis=1))
                t["u"] = uw[:, :GDN_GW]
                t["w"] = uw[:, GDN_GW:]

        def fold():
            for t in ch:
                wu = jnp.concatenate([t["w"], t["u"]], axis=1).astype(BF16)
                cb = _dot_tn(t["kend"], wu)
                t["c"] = jnp.where(same_head, cb[:, :GDN_GW], 0.0).astype(BF16)
                t["b"] = jnp.where(same_head, cb[:, GDN_GW:], 0.0)
                ao = _dot(t["aqk"], jnp.concatenate([bdiag(t["w"]), bdiag(t["u"])], axis=1))
                t["qt"] = (t["qe"] - ao[:, :GDN_GW]).astype(BF16)
                t["o"] = ao[:, GDN_GW:]

        out = [neumann_a, neumann_b, neumann_c]
        for inner, outer in ((blk8, blk16), (blk16, blk32), (blk32, None)):
            off = jnp.logical_not(inner) if outer is None else jnp.logical_and(outer, jnp.logical_not(inner))
            out += [double_a(off), double_b]
        return out + [solve, fold]

    states = [[s_scr[dd, g] for g in range(ngrp)] for dd in range(2)]

    def recur(cur):
        ys = [_dot(jnp.concatenate([t["qt"], t["c"]], axis=0), states[t["dd"]][t["g"]].astype(BF16))
              for t in cur]
        for t, y in zip(cur, ys):
            o_refs[t["dd"]][t["rows"], t["gl"]] = y[:CHUNK] + t["o"]
            states[t["dd"]][t["g"]] = t["eg"] * states[t["dd"]][t["g"]] - y[CHUNK:] + t["b"]

    half = NCH // 2
    chains = {}
    for c in range(half):
        for dd in range(2):
            chains[(dd, c)] = setup(dd, c)
    later = [(dd, c) for c in range(half, NCH) for dd in range(2)]
    wave_a = [t for c in range(half) for dd in range(2) for t in chains[(dd, c)]]
    for k, stage in enumerate(stages(wave_a)):
        stage()
        if k % 2 == 0 and later:
            key = later.pop(0)
            chains[key] = setup(*key)
    for key in later:
        chains[key] = setup(*key)
    wave_b = [t for c in range(half, NCH) for dd in range(2) for t in chains[(dd, c)]]
    done = 0
    for k, stage in enumerate(stages(wave_b)):
        stage()
        if k % 3 == 1 and done < half:
            recur(chains[(0, done)] + chains[(1, done)])
            done += 1
    for c in range(done, NCH):
        recur(chains[(0, c)] + chains[(1, c)])
    for dd in range(2):
        for g in range(ngrp):
            s_scr[dd, g] = states[dd][g]

    if not has_s0:
        @pl.when(j == nblk - 1)
        def _():
            for dd in range(2):
                for h in range(GDN_HEADS):
                    g, rk, rv = head_block(h)
                    sfin_ref[0, dd, h] = s_scr[dd, g, rk, rv]


def _gdn(qkv, proj, alog_p, dtb_p, s0, layer, ctx):
    nseq, nblk = (BATCH, SEQ // SEQ_BLK) if ctx else (DEC_BATCH, DEC_SEQ // SEQ_BLK)
    roff = 0 if ctx else NCTX // SEQ_BLK
    hk = GDN_HEADS * GDN_DK
    hv = GDN_HEADS * GDN_DV

    def rb(dd):
        return lambda s, j: roff + s * nblk + (j if dd == 0 else nblk - 1 - j)

    in_specs, args = [], []
    for dd in range(2):
        r = rb(dd)
        in_specs += [pl.BlockSpec((SEQ_BLK, hk), lambda s, j, r=r: (r(s, j), 0)),
                     pl.BlockSpec((SEQ_BLK, hk), lambda s, j, r=r: (r(s, j), 1)),
                     pl.BlockSpec((SEQ_BLK, hv), lambda s, j, r=r: (r(s, j), 2 * hk // hv)),
                     pl.BlockSpec((SEQ_BLK, LANE), lambda s, j, r=r, dd=dd: (r(s, j), PC_DAB // LANE + dd))]
        args += [qkv, qkv, qkv, proj]
    in_specs += [pl.BlockSpec((1, 2, 1, LANE), lambda s, j: (layer, 0, 0, 0))] * 2
    args += [alog_p, dtb_p]
    nrows = NCTX if ctx else NSMP
    out_specs = [pl.BlockSpec((SEQ_BLK, hv), lambda s, j, r=rb(dd): (r(s, j) - roff, 0)) for dd in range(2)]
    out_shape = [jax.ShapeDtypeStruct((nrows, hv), F32)] * 2
    st_blk = (2, GDN_HEADS, GDN_DK, GDN_DV)
    if ctx:
        out_specs.append(pl.BlockSpec((1,) + st_blk, lambda s, j: (s, 0, 0, 0, 0)))
        out_shape.append(jax.ShapeDtypeStruct((BATCH,) + st_blk, F32))
    else:
        in_specs.append(pl.BlockSpec((1, 1) + st_blk, lambda s, j: (s, layer, 0, 0, 0, 0)))
        args.append(s0)
    return pl.pallas_call(
        functools.partial(_gdn_kernel, nblk=nblk, has_s0=not ctx),
        grid=(nseq, nblk),
        in_specs=in_specs, out_specs=out_specs, out_shape=out_shape,
        scratch_shapes=[pltpu.VMEM((2, GDN_HEADS // GDN_G, GDN_GW, GDN_GW), F32)],
        compiler_params=_cp(("arbitrary", "arbitrary")),
        name="gdn_ctx" if ctx else "gdn_smp",
    )(*args)


def _group_rms(x, width):
    parts = []
    lane = lax.broadcasted_iota(jnp.int32, (x.shape[0], LANE), 1)
    lo_half = lane < 64
    for c in range(x.shape[1] // LANE):
        xc = x[:, c * LANE:(c + 1) * LANE]
        sq = xc * xc
        if width == LANE:
            ms = jnp.mean(sq, axis=-1, keepdims=True)
        else:
            s_lo = jnp.sum(jnp.where(lo_half, sq, 0.0), axis=-1, keepdims=True)
            s_hi = jnp.sum(jnp.where(lo_half, 0.0, sq), axis=-1, keepdims=True)
            ms = jnp.where(lo_half, s_lo, s_hi) * (1.0 / width)
        parts.append(xc * lax.rsqrt(ms + EPS))
    return jnp.concatenate(parts, axis=1)


def _merge_kernel(ymc_ref, yms_ref, ogcf_ref, ogcb_ref, ogsf_ref, ogsb_ref, gr_ref,
                  odcf_ref, odcb_ref, odsf_ref, odsb_ref,
                  dz_ref, gt_ref, *rest, split_h):
    if split_h:
        hc_ref, hs_ref, *rest = rest
        h_rows = lambda: _family_rows((hc_ref, hs_ref), TM_MERGE)
    else:
        h_ref, *rest = rest
        h_rows = lambda: h_ref[...]
    ga_ref, gn_ref, dn_ref, wb_ref, wo_ref, o_ref, ym_scr, og_scr, od_scr = rest
    is_ctx = pl.program_id(0) < NCTX // TM_MERGE

    @pl.when(is_ctx)
    def _():
        ym_scr[...] = ymc_ref[...]
        og_scr[...] = ogcf_ref[...] + ogcb_ref[...]
        od_scr[...] = odcf_ref[...] + odcb_ref[...]

    @pl.when(jnp.logical_not(is_ctx))
    def _():
        ym_scr[...] = yms_ref[...]
        og_scr[...] = ogsf_ref[...] + ogsb_ref[...]
        od_scr[...] = odsf_ref[...] + odsb_ref[...]

    y_gla = _group_rms(og_scr[...], GLA_DV) * gn_ref[0] * _silu(gr_ref[...].astype(F32))
    y_gdn = _group_rms(od_scr[...], GDN_DV) * dn_ref[0] * _silu(dz_ref[...].astype(F32))
    gates = _sigmoid(gt_ref[...].astype(F32))
    d = D_MODEL
    acc = gates[:, :d] * _dot(ym_scr[...], wb_ref[0, 0])
    acc = acc + gates[:, d:2 * d] * _dot(y_gla.astype(BF16), wb_ref[0, 1])
    acc = acc + gates[:, 2 * d:] * _dot(y_gdn.astype(BF16), wb_ref[0, 2])
    y = _dot(acc.astype(BF16), wo_ref[0])
    o_ref[...] = h_rows() + ga_ref[0] * y


def _merge(y_mla, o_gla, o_gdn, proj, h, mod3, gla_norm_p, gdn_norm_p, w_branch_b, w_out_b, layer):
    tm = TM_MERGE
    bw = 512
    nc = NCTX // tm
    split_h = isinstance(h, tuple)
    h_specs = _family_specs(tm, D_MODEL) if split_h else [pl.BlockSpec((tm, D_MODEL), lambda i: (i, 0))]
    h_args = list(h) if split_h else [h]

    def cmap(i):
        return jnp.minimum(i, nc - 1)

    def smap(i):
        return jnp.maximum(i - nc, 0)

    return pl.pallas_call(
        functools.partial(_merge_kernel, split_h=split_h),
        grid=(R // tm,),
        in_specs=[pl.BlockSpec((tm, bw), lambda i: (cmap(i), 0)),
                  pl.BlockSpec((tm, bw), lambda i: (smap(i), 0)),
                  pl.BlockSpec((tm, bw), lambda i: (cmap(i), 0)),
                  pl.BlockSpec((tm, bw), lambda i: (cmap(i), 0)),
                  pl.BlockSpec((tm, bw), lambda i: (smap(i), 0)),
                  pl.BlockSpec((tm, bw), lambda i: (smap(i), 0)),
                  pl.BlockSpec((tm, bw), lambda i: (i, PC_GR // bw)),
                  pl.BlockSpec((tm, bw), lambda i: (cmap(i), 0)),
                  pl.BlockSpec((tm, bw), lambda i: (cmap(i), 0)),
                  pl.BlockSpec((tm, bw), lambda i: (smap(i), 0)),
                  pl.BlockSpec((tm, bw), lambda i: (smap(i), 0)),
                  pl.BlockSpec((tm, bw), lambda i: (i, PC_DZ // bw)),
                  pl.BlockSpec((tm, 3 * D_MODEL), lambda i: (i, PC_GATES // (3 * D_MODEL)))] + h_specs + [
                  _mod_spec(layer, 2, tm),
                  pl.BlockSpec((1, 1, bw), lambda i: (layer, 0, 0)),
                  pl.BlockSpec((1, 1, bw), lambda i: (layer, 0, 0)),
                  pl.BlockSpec((1, 3, bw, D_MODEL), lambda i: (layer, 0, 0, 0)),
                  pl.BlockSpec((1, D_MODEL, D_MODEL), lambda i: (layer, 0, 0))],
        out_specs=pl.BlockSpec((tm, D_MODEL), lambda i: (i, 0)),
        out_shape=jax.ShapeDtypeStruct((R, D_MODEL), F32),
        scratch_shapes=[pltpu.VMEM((tm, bw), BF16), pltpu.VMEM((tm, bw), F32), pltpu.VMEM((tm, bw), F32)],
        compiler_params=_cp(("arbitrary",)),
        name="merge",
    )(y_mla[0], y_mla[1], o_gla[0][0], o_gla[0][1], o_gla[1][0], o_gla[1][1], proj,
      o_gdn[0][0], o_gdn[0][1], o_gdn[1][0], o_gdn[1][1],
      proj, proj, *h_args, mod3, gla_norm_p, gdn_norm_p, w_branch_b, w_out_b)


def _ffn_kernel(h_ref, g_ref, sc_ref, sh_ref, gf_ref, wi_ref, wo_ref, fn_ref, *o_refs, final):
    h = h_ref[...]
    xf = (_rms(h, g_ref[0]) * (1.0 + sc_ref[0]) + sh_ref[0]).astype(BF16)

    def gate_up(c):
        f0 = c * TF_FFN
        return (_dot(xf, wi_ref[0, :, f0:f0 + TF_FFN]),
                _dot(xf, wi_ref[0, :, D_FF + f0:D_FF + f0 + TF_FFN]))

    n_chunks = D_FF // TF_FFN
    nxt = gate_up(0)
    acc = None
    for c in range(n_chunks):
        g, u = nxt
        if c + 1 < n_chunks:
            nxt = gate_up(c + 1)
        part = _dot((_silu(g) * u).astype(BF16), wo_ref[0, c * TF_FFN:(c + 1) * TF_FFN, :])
        acc = part if acc is None else acc + part
    out = h + gf_ref[0] * acc
    if not final:
        o_refs[0][...] = out
    else:
        out = _rms(out, fn_ref[...])
        is_ctx = pl.program_id(0) < NCTX // TM_FFN

        @pl.when(is_ctx)
        def _():
            o_refs[0][...] = out

        @pl.when(jnp.logical_not(is_ctx))
        def _():
            o_refs[1][...] = out


def _ffn(h, mod3, norm_ffn, wi_b, wo_b, final_norm, layer, final):
    tm = TM_FFN
    if final:
        nc = NCTX // tm
        out_specs = [pl.BlockSpec((tm, D_MODEL), lambda i: (jnp.minimum(i, nc - 1), 0)),
                     pl.BlockSpec((tm, D_MODEL), lambda i: (jnp.maximum(i - nc, 0), 0))]
        out_shape = [jax.ShapeDtypeStruct((NCTX, D_MODEL), F32), jax.ShapeDtypeStruct((NSMP, D_MODEL), F32)]
    else:
        out_specs = pl.BlockSpec((tm, D_MODEL), lambda i: (i, 0))
        out_shape = jax.ShapeDtypeStruct((R, D_MODEL), F32)
    return pl.pallas_call(
        functools.partial(_ffn_kernel, final=final),
        grid=(R // tm,),
        in_specs=[pl.BlockSpec((tm, D_MODEL), lambda i: (i, 0)),
                  pl.BlockSpec((1, 1, D_MODEL), lambda i: (layer, 0, 0)),
                  _mod_spec(layer, 4, tm),
                  _mod_spec(layer, 3, tm),
                  _mod_spec(layer, 5, tm),
                  pl.BlockSpec((1, D_MODEL, 2 * D_FF), lambda i: (layer, 0, 0), pipeline_mode=pl.Buffered(1)),
                  pl.BlockSpec((1, D_FF, D_MODEL), lambda i: (layer, 0, 0), pipeline_mode=pl.Buffered(1)),
                  pl.BlockSpec((1, D_MODEL), lambda i: (0, 0))],
        out_specs=out_specs, out_shape=out_shape,
        compiler_params=_cp(("arbitrary",)),
        name="ffn_final" if final else "ffn",
    )(h, norm_ffn.reshape(DEPTH, 1, D_MODEL), mod3, mod3, mod3, wi_b, wo_b, final_norm.reshape(1, D_MODEL))


def _rope_partner(w):
    h = AXIS_DIM // 2
    parts = []
    for a in range(2):
        x1 = w[..., a * AXIS_DIM:a * AXIS_DIM + h]
        x2 = w[..., a * AXIS_DIM + h:(a + 1) * AXIS_DIM]
        parts += [-x2, x1]
    return jnp.concatenate(parts, axis=-1)


def _pack_w_in_kernel(w_ref, o_ref):
    o_ref[0] = _pack_columns(w_ref[0].astype(F32)).astype(BF16)


def _pack_w_in(w_in):
    tr = 256
    return pl.pallas_call(
        _pack_w_in_kernel,
        grid=(DEPTH, D_MODEL // tr),
        in_specs=[pl.BlockSpec((1, tr, w_in.shape[-1]), lambda l, i: (l, i, 0))],
        out_specs=pl.BlockSpec((1, tr, PROJ_W), lambda l, i: (l, i, 0)),
        out_shape=jax.ShapeDtypeStruct((DEPTH, D_MODEL, PROJ_W), BF16),
        compiler_params=_cp(("arbitrary", "arbitrary")),
        name="pack_w_in",
    )(w_in.astype(BF16))


def _pack_columns(w_in):
    c = [0]
    for s in (Q_LORA, KV_LORA + ROPE_DIM, 256, 256, 512, 512, 2 * GLA_RANK, 1536, 512,
              2 * GDN_HEADS, 2 * GDN_HEADS, 3 * D_MODEL):
        c.append(c[-1] + s)
    seg = [w_in[..., c[i]:c[i + 1]] for i in range(12)]
    mq, mkv, gq, gk, gv, gr, glr, dqkv, dz, da, db, gates = seg

    def zeros(n):
        return jnp.zeros(w_in.shape[:-1] + (n,), w_in.dtype)

    kr = mkv[..., KV_LORA:]
    mla = jnp.concatenate([mq, mkv[..., :KV_LORA], kr, _rope_partner(kr), zeros(64)], axis=-1)
    dab = []
    for d in range(2):
        dab += [da[..., d * GDN_HEADS:(d + 1) * GDN_HEADS], db[..., d * GDN_HEADS:(d + 1) * GDN_HEADS],
                zeros(LANE - 2 * GDN_HEADS)]
    packed = jnp.concatenate([gates, dqkv, dz, gv, gr, mla, gq, gk, glr, zeros(LANE - 2 * GLA_RANK)]
                             + dab + [zeros(PROJ_W - PC_DAB - 2 * LANE)], axis=-1)
    return packed


def _pack_mla_weights(w_uq, w_ukv):
    l = w_uq.shape[0]
    qh = w_uq.reshape(l, Q_LORA, MLA_HEADS, MLA_NOPE + ROPE_DIM)
    zq = jnp.zeros((l, Q_LORA, MLA_HEADS, HEAD_W - MLA_NOPE - ROPE_DIM), w_uq.dtype)
    zn = jnp.zeros((l, Q_LORA, MLA_HEADS, MLA_NOPE), w_uq.dtype)
    wq = jnp.concatenate([qh, zq], axis=-1).reshape(l, Q_LORA, MLA_HEADS * HEAD_W)
    wq_sw = jnp.concatenate([zn, _rope_partner(qh[..., MLA_NOPE:]), zq], axis=-1).reshape(l, Q_LORA, MLA_HEADS * HEAD_W)
    wq_p = jnp.concatenate([wq, wq_sw], axis=-1).astype(BF16)

    kvh = w_ukv.reshape(l, KV_LORA, MLA_HEADS, MLA_NOPE + MLA_V)
    zk = jnp.zeros((l, KV_LORA, MLA_HEADS, HEAD_W - MLA_NOPE), w_ukv.dtype)
    wk_top = jnp.concatenate([kvh[..., :MLA_NOPE], zk], axis=-1).reshape(l, KV_LORA, MLA_HEADS * HEAD_W)
    place = jnp.zeros((KV_LORA, MLA_HEADS, HEAD_W), w_ukv.dtype)
    idx = jnp.arange(ROPE_DIM)
    for rep in range(2):
        place = place.at[rep * ROPE_DIM + idx, :, MLA_NOPE + idx].set(1.0)
    wk_bot = jnp.broadcast_to(place.reshape(1, KV_LORA, MLA_HEADS * HEAD_W), (l, KV_LORA, MLA_HEADS * HEAD_W))
    wk_p = jnp.concatenate([wk_top, wk_bot], axis=1).astype(BF16)
    wv_p = jnp.transpose(kvh[..., MLA_NOPE:], (0, 2, 3, 1)).reshape(l, VT_ROWS, KV_LORA).astype(BF16)
    return wq_p, wk_p, wv_p


def _rope_tables(tm):
    rows = DEC_SEQ // GRID_W
    row = jnp.repeat(jnp.arange(rows, dtype=F32), GRID_W)
    col = jnp.tile(jnp.arange(GRID_W, dtype=F32), rows)
    inv = ROPE_THETA ** (-jnp.arange(0, AXIS_DIM, 2, dtype=F32) / AXIS_DIM)
    ang_r, ang_c = row[:, None] * inv, col[:, None] * inv
    cos32 = jnp.concatenate([jnp.cos(ang_r)] * 2 + [jnp.cos(ang_c)] * 2, axis=-1)
    sin32 = jnp.concatenate([jnp.sin(ang_r)] * 2 + [jnp.sin(ang_c)] * 2, axis=-1)
    cos32 = jnp.concatenate([jnp.ones((tm, ROPE_DIM), F32), cos32], axis=0)
    sin32 = jnp.concatenate([jnp.zeros((tm, ROPE_DIM), F32), sin32], axis=0)
    n = cos32.shape[0]
    scale = (MLA_NOPE + ROPE_DIM) ** -0.5 * math.log2(math.e)
    pad = HEAD_W - MLA_NOPE - ROPE_DIM
    cq = jnp.concatenate([jnp.ones((n, MLA_NOPE), F32), cos32, jnp.zeros((n, pad), F32)], axis=-1) * scale
    sq = jnp.concatenate([jnp.zeros((n, MLA_NOPE), F32), sin32, jnp.zeros((n, pad), F32)], axis=-1) * scale
    ck = jnp.concatenate([cos32, sin32, jnp.zeros((n, HEAD_W - 2 * ROPE_DIM), F32)], axis=-1)
    return jnp.concatenate([cq, sq, ck], axis=-1)


def _pad_lanes(x, n):
    return jnp.pad(x, [(0, 0)] * (x.ndim - 1) + [(0, n - x.shape[-1])])


def kernel(x_prompt, x_sample, cache_mla, state_gla, state_gdn, c, c_ctx, w_mod, b_mod, norm_mix, w_in,
           mla_q_norm, mla_w_uq, mla_kv_norm, mla_w_ukv, gla_w_gate, gla_b_gate, gla_norm, gdn_conv,
           gdn_a_log, gdn_dt_bias, gdn_norm, w_branch, w_out, norm_ffn, ffn_w_in, ffn_w_out, final_norm):
    w_in_p = _pack_w_in(w_in)
    wq_p, wk_p, wv_p = _pack_mla_weights(mla_w_uq, mla_w_ukv)
    tab = _rope_tables(TM_PREP)
    wg_p = jnp.zeros((DEPTH, 2, LANE, GLA_HEADS * GLA_DK), F32)
    for d in range(2):
        wg_p = wg_p.at[:, d, d * GLA_RANK:(d + 1) * GLA_RANK, :].set(gla_w_gate[:, d])
    wg_p = wg_p.astype(BF16)
    bg_p = gla_b_gate.reshape(DEPTH, 2, 1, GLA_HEADS * GLA_DK)
    alog_p = _pad_lanes(gdn_a_log, LANE).reshape(DEPTH, 2, 1, LANE)
    dtb_p = _pad_lanes(gdn_dt_bias, LANE).reshape(DEPTH, 2, 1, LANE)
    gla_norm_p = jnp.tile(gla_norm, (1, GLA_HEADS)).reshape(DEPTH, 1, GLA_HEADS * GLA_DV)
    gdn_norm_p = jnp.tile(gdn_norm, (1, GDN_HEADS)).reshape(DEPTH, 1, GDN_HEADS * GDN_DV)
    w_branch_b = w_branch.astype(BF16)
    w_out_b = w_out.astype(BF16)
    wi_b = ffn_w_in.astype(BF16)
    wo_b = ffn_w_out.astype(BF16)
    cache_p = _pad_lanes(cache_mla, 2 * KV_LORA)
    cond8 = jnp.concatenate([c_ctx[None, :], c, jnp.zeros((MOD_ROWS - 1 - DEC_BATCH, D_MODEL), F32)], axis=0)

    mod = _modulation(cond8, w_mod, b_mod)
    mod3 = mod.reshape(DEPTH * MOD_ROWS * 6, 1, D_MODEL)
    kc, vc = _kv_cache(cache_p, wk_p, wv_p)

    h = (x_prompt.reshape(NCTX, D_MODEL), x_sample.reshape(NSMP, D_MODEL))
    kv_list, gla_list, gdn_list = [], [], []
    for l in range(DEPTH):
        proj = _inproj(h, mod3, norm_mix, w_in_p, l)
        qp, kp, vtp, own = _mla_prep(proj, tab, mla_q_norm, mla_kv_norm, wq_p, wk_p, wv_p, l)
        y_mla = (_attention_ctx(qp, kp, vtp), _attention_smp(qp, kp, vtp, kc, vc, l))
        ogf_c, ogb_c, sg = _gla(proj, wg_p, bg_p, None, l, True)
        ogf_s, ogb_s = _gla(proj, wg_p, bg_p, state_gla, l, False)
        qkv = _gdn_pre(proj, gdn_conv, l)
        odf_c, odb_c, sd = _gdn(qkv, proj, alog_p, dtb_p, None, l, True)
        odf_s, odb_s = _gdn(qkv, proj, alog_p, dtb_p, state_gdn, l, False)
        h = _merge(y_mla, ((ogf_c, ogb_c), (ogf_s, ogb_s)), ((odf_c, odb_c), (odf_s, odb_s)), proj, h, mod3,
                   gla_norm_p, gdn_norm_p,
                   w_branch_b, w_out_b, l)
        h = _ffn(h, mod3, norm_ffn, wi_b, wo_b, final_norm, l, l == DEPTH - 1)
        kv_list.append(own[:NCTX, :KV_LORA + ROPE_DIM].reshape(BATCH, SEQ, KV_LORA + ROPE_DIM))
        gla_list.append(sg)
        gdn_list.append(sd)

    y_prompt = h[0].reshape(BATCH, SEQ, D_MODEL)
    y_sample = h[1].reshape(DEC_BATCH, DEC_SEQ, D_MODEL)
    return (y_prompt, y_sample, jnp.stack(kv_list, axis=1), jnp.stack(gla_list, axis=1),
            jnp.stack(gdn_list, axis=1))
```

```python
import functools
import math

import jax
import jax.numpy as jnp
from jax import lax
from jax.experimental import pallas as pl
from jax.experimental.pallas import tpu as pltpu

F32 = jnp.float32
BF16 = jnp.bfloat16

D_MODEL = 1024
BATCH = 16
SEQ = 256
DEPTH = 2
DEC_BATCH = 2
DEC_SEQ = 4096
PAST_LEN = 256
GRID_W = 64
CHUNK = 64
EPS = 1e-6
MLA_HEADS = 8
MLA_NOPE = 64
ROPE_DIM = 32
AXIS_DIM = ROPE_DIM // 2
MLA_V = 64
Q_LORA = 256
KV_LORA = 128
ROPE_THETA = 10000.0
GLA_HEADS = 4
GLA_DK = 64
GLA_DV = 128
GLA_RANK = 16
GLA_TAU = 16.0
GDN_HEADS = 8
GDN_DK = 64
GDN_DV = 64
D_FF = ((8 * D_MODEL + 3 * 256 - 1) // (3 * 256)) * 256
MOD_W = 6 * D_MODEL

NCTX = BATCH * SEQ
NSMP = DEC_BATCH * DEC_SEQ
R = NCTX + NSMP
LANE = 128
HEAD_W = 128
VT_ROWS = MLA_HEADS * MLA_V
SEQ_BLK = 256
NCH = SEQ_BLK // CHUNK
GLA_BLK_CTX = 256
GLA_BLK_SMP = 512
MOD_ROWS = 8

PC_GATES = 0
PC_DQKV = 3072
PC_DZ = 4608
PC_GV = 5120
PC_GR = 5632
PC_MLA = 6144
PC_GQ = 6656
PC_GK = 6912
PC_GLR = 7168
PC_DAB = 7296
PROJ_W = 7680

TM_IN = 1024
TN_IN = 2560
TM_PREP = 512
TM_MERGE = 512
TM_FFN = 512
TF_FFN = 256
TQ_ATT = 256
TK_ATT = 512
ONES_ROWS = 16
ATT_AHEAD = 8
VMEM_LIMIT = 48 * 1024 * 1024


def _cp(sem):
    return pltpu.CompilerParams(dimension_semantics=sem, vmem_limit_bytes=VMEM_LIMIT)


def _dot(a, b):
    return jnp.dot(a, b, preferred_element_type=F32)


def _dot_nt(a, b):
    return lax.dot_general(a, b, (((1,), (1,)), ((), ())), preferred_element_type=F32)


def _dot_tn(a, b):
    return lax.dot_general(a, b, (((0,), (0,)), ((), ())), preferred_element_type=F32)


def _split2(x):
    hi = x.astype(BF16)
    lo = (x - hi.astype(F32)).astype(BF16)
    return hi, lo


def _split3(x):
    hi = x.astype(BF16)
    r1 = x - hi.astype(F32)
    mid = r1.astype(BF16)
    lo = (r1 - mid.astype(F32)).astype(BF16)
    return hi, mid, lo


def _cumsum_rows(tri_b, x):
    hi, mid, lo = _split3(x)
    return _dot(tri_b, hi) + (_dot(tri_b, mid) + _dot(tri_b, lo))


def _sigmoid(x):
    return 0.5 * jnp.tanh(0.5 * x) + 0.5


def _silu(x):
    return x * _sigmoid(x)


def _silu_of_twice(hx):
    return hx + hx * jnp.tanh(hx)


def _softplus(x):
    return jnp.maximum(x, 0.0) + jnp.log(1.0 + jnp.exp(-jnp.abs(x)))


def _log_sigmoid(x):
    return jnp.minimum(x, 0.0) - jnp.log(1.0 + jnp.exp(-jnp.abs(x)))


def _rms(x, g):
    return x * lax.rsqrt(jnp.mean(x * x, axis=-1, keepdims=True) + EPS) * g


def _mod_row(row_start):
    return jnp.where(row_start < NCTX, 0, 1 + (row_start - NCTX) // DEC_SEQ)


def _mod_spec(layer, which, tm, axis=0):
    def imap(*idx):
        return ((layer * MOD_ROWS + _mod_row(idx[axis] * tm)) * 6 + which, 0, 0)
    return pl.BlockSpec((1, 1, D_MODEL), imap)


def _mod_kernel(c_ref, w_ref, b_ref, o_ref):
    x = _silu(c_ref[...])
    xh, xl = _split2(x)
    wh, wl = _split2(w_ref[0])
    o_ref[0] = _dot(xh, wh) + (_dot(xl, wh) + _dot(xh, wl)) + b_ref[0]


def _modulation(cond8, w_mod, b_mod):
    tn = 1536
    return pl.pallas_call(
        _mod_kernel,
        grid=(DEPTH, MOD_W // tn),
        in_specs=[pl.BlockSpec((MOD_ROWS, D_MODEL), lambda l, j: (0, 0)),
                  pl.BlockSpec((1, D_MODEL, tn), lambda l, j: (l, 0, j)),
                  pl.BlockSpec((1, 1, tn), lambda l, j: (l, 0, j))],
        out_specs=pl.BlockSpec((1, MOD_ROWS, tn), lambda l, j: (l, 0, j)),
        out_shape=jax.ShapeDtypeStruct((DEPTH, MOD_ROWS, MOD_W), F32),
        compiler_params=_cp(("arbitrary", "arbitrary")),
        name="modulation",
    )(cond8, w_mod, b_mod.reshape(DEPTH, 1, MOD_W))


def _family_specs(tm, width, axis=0):
    nc = NCTX // tm
    return [pl.BlockSpec((tm, width), lambda *idx: (jnp.minimum(idx[axis], nc - 1), 0)),
            pl.BlockSpec((tm, width), lambda *idx: (jnp.maximum(idx[axis] - nc, 0), 0))]


def _family_rows(refs, tm, axis=0):
    is_ctx = pl.program_id(axis) < NCTX // tm
    return jnp.where(is_ctx, refs[0][...], refs[1][...])


def _inproj_kernel(*refs, normed):
    if normed:
        xn_ref, w_ref, o_ref = refs
        xn = xn_ref[...]
    else:
        hc_ref, hs_ref, g_ref, sc_ref, sh_ref, w_ref, o_ref = refs
        y = _rms(_family_rows((hc_ref, hs_ref), TM_IN, axis=1), g_ref[0])
        xn = (y * (1.0 + sc_ref[0]) + sh_ref[0]).astype(BF16)
    o_ref[...] = _dot(xn, w_ref[0]).astype(BF16)


def _inproj(h, mod3, norm_mix, w_in_p, layer):
    normed = not isinstance(h, tuple)
    if normed:
        in_specs = [pl.BlockSpec((TM_IN, D_MODEL), lambda j, i: (i, 0))]
        args = [h]
    else:
        in_specs = _family_specs(TM_IN, D_MODEL, axis=1) + [
            pl.BlockSpec((1, 1, D_MODEL), lambda j, i: (layer, 0, 0)),
            _mod_spec(layer, 1, TM_IN, axis=1),
            _mod_spec(layer, 0, TM_IN, axis=1)]
        args = list(h) + [norm_mix.reshape(DEPTH, 1, D_MODEL), mod3, mod3]
    return pl.pallas_call(
        functools.partial(_inproj_kernel, normed=normed),
        grid=(PROJ_W // TN_IN, R // TM_IN),
        in_specs=in_specs + [pl.BlockSpec((1, D_MODEL, TN_IN), lambda j, i: (layer, 0, j))],
        out_specs=pl.BlockSpec((TM_IN, TN_IN), lambda j, i: (i, j)),
        out_shape=jax.ShapeDtypeStruct((R, PROJ_W), BF16),
        compiler_params=_cp(("arbitrary", "arbitrary")),
        name="inproj",
    )(*args, w_in_p)


def _mla_prep_kernel(pm_ref, tab_ref, qn_ref, kvn_ref, wq_ref, wk_ref, wv_ref,
                     q_ref, k_ref, v_ref, own_ref):
    pm = pm_ref[...].astype(F32)
    tab = tab_ref[...]
    qn = _rms(pm[:, :Q_LORA], qn_ref[0]).astype(BF16)
    q2 = _dot(qn, wq_ref[0])
    hw = MLA_HEADS * HEAD_W
    cq = jnp.tile(tab[:, :HEAD_W], (1, MLA_HEADS))
    sq = jnp.tile(tab[:, HEAD_W:2 * HEAD_W], (1, MLA_HEADS))
    q_ref[...] = (q2[:, :hw] * cq + q2[:, hw:] * sq).astype(BF16)
    ckv = _rms(pm[:, Q_LORA:Q_LORA + KV_LORA], kvn_ref[0])
    kr = pm[:, Q_LORA + KV_LORA:] * tab[:, 2 * HEAD_W:]
    lhs = jnp.concatenate([ckv, kr], axis=1)
    own_ref[...] = lhs
    lb = lhs.astype(BF16)
    k_ref[...] = _dot(lb, wk_ref[0]).astype(BF16)
    v_ref[...] = _dot_nt(wv_ref[0], lb[:, :KV_LORA]).astype(BF16)


def _mla_prep(proj, tab, q_norm, kv_norm, wq_p, wk_p, wv_p, layer):
    tm = TM_PREP
    hw = MLA_HEADS * HEAD_W

    def tab_map(i):
        r0 = i * tm
        return (jnp.where(r0 < NCTX, 0, 1 + ((r0 - NCTX) % DEC_SEQ) // tm), 0)

    return pl.pallas_call(
        _mla_prep_kernel,
        grid=(R // tm,),
        in_specs=[pl.BlockSpec((tm, 512), lambda i: (i, PC_MLA // 512)),
                  pl.BlockSpec((tm, 3 * HEAD_W), tab_map),
                  pl.BlockSpec((1, 1, Q_LORA), lambda i: (layer, 0, 0)),
                  pl.BlockSpec((1, 1, KV_LORA), lambda i: (layer, 0, 0)),
                  pl.BlockSpec((1, Q_LORA, 2 * hw), lambda i: (layer, 0, 0)),
                  pl.BlockSpec((1, 2 * KV_LORA, hw), lambda i: (layer, 0, 0)),
                  pl.BlockSpec((1, VT_ROWS, KV_LORA), lambda i: (layer, 0, 0))],
        out_specs=[pl.BlockSpec((tm, hw), lambda i: (i, 0)),
                   pl.BlockSpec((tm, hw), lambda i: (i, 0)),
                   pl.BlockSpec((VT_ROWS, tm), lambda i: (0, i)),
                   pl.BlockSpec((tm, 2 * KV_LORA), lambda i: (i, 0))],
        out_shape=[jax.ShapeDtypeStruct((R, hw), BF16),
                   jax.ShapeDtypeStruct((R, hw), BF16),
                   jax.ShapeDtypeStruct((VT_ROWS, R), BF16),
                   jax.ShapeDtypeStruct((R, 2 * KV_LORA), F32)],
        compiler_params=_cp(("arbitrary",)),
        name="mla_prep",
    )(proj, tab, q_norm.reshape(DEPTH, 1, Q_LORA), kv_norm.reshape(DEPTH, 1, KV_LORA),
      wq_p, wk_p, wv_p)


def _kv_cache_kernel(c_ref, wk_ref, wv_ref, k_ref, v_ref):
    lb = c_ref[0, 0].astype(BF16)
    k_ref[0, 0] = _dot(lb, wk_ref[0]).astype(BF16)
    v_ref[0, 0] = _dot_nt(wv_ref[0], lb[:, :KV_LORA]).astype(BF16)


def _kv_cache(cache_p, wk_p, wv_p):
    hw = MLA_HEADS * HEAD_W
    return pl.pallas_call(
        _kv_cache_kernel,
        grid=(DEC_BATCH, DEPTH),
        in_specs=[pl.BlockSpec((1, 1, PAST_LEN, 2 * KV_LORA), lambda b, l: (b, l, 0, 0)),
                  pl.BlockSpec((1, 2 * KV_LORA, hw), lambda b, l: (l, 0, 0)),
                  pl.BlockSpec((1, VT_ROWS, KV_LORA), lambda b, l: (l, 0, 0))],
        out_specs=[pl.BlockSpec((1, 1, PAST_LEN, hw), lambda b, l: (b, l, 0, 0)),
                   pl.BlockSpec((1, 1, VT_ROWS, PAST_LEN), lambda b, l: (b, l, 0, 0))],
        out_shape=[jax.ShapeDtypeStruct((DEC_BATCH, DEPTH, PAST_LEN, hw), BF16),
                   jax.ShapeDtypeStruct((DEC_BATCH, DEPTH, VT_ROWS, PAST_LEN), BF16)],
        compiler_params=_cp(("arbitrary", "arbitrary")),
        name="kv_cache",
    )(cache_p, wk_p, wv_p)


def _attn_kernel(*refs, heads, has_cache):
    if has_cache:
        q_ref, k_ref, vt_ref, kc_ref, vct_ref, o_ref = refs
    else:
        q_ref, k_ref, vt_ref, o_ref = refs
    n_keys = k_ref.shape[0]
    tk = min(TK_ATT, n_keys)
    tiles = ([("cache", 0, PAST_LEN)] if has_cache else []) + [("own", s0, tk) for s0 in range(0, n_keys, tk)]
    units = [(h, t) for t in range(len(tiles)) for h in range(heads)]
    qs = [q_ref[:, h * HEAD_W:(h + 1) * HEAD_W] for h in range(heads)]

    def scores(h, t):
        kind, s0, n = tiles[t]
        sl = slice(h * HEAD_W, (h + 1) * HEAD_W)
        keys = kc_ref[0, 0, :, sl] if kind == "cache" else k_ref[s0:s0 + n, sl]
        return _dot_nt(keys, qs[h])

    def values_t(h, t):
        kind, s0, n = tiles[t]
        sv = slice(h * MLA_V, (h + 1) * MLA_V)
        vt = vct_ref[0, 0, sv, :] if kind == "cache" else vt_ref[sv, s0:s0 + n]
        return jnp.concatenate([vt, jnp.ones((ONES_ROWS, n), BF16)], axis=0)

    ahead = ATT_AHEAD
    pending = {u: scores(*u) for u in units[:ahead]}
    m, acc = [None] * heads, [None] * heads
    for i, (h, t) in enumerate(units):
        if i + ahead < len(units):
            pending[units[i + ahead]] = scores(*units[i + ahead])
        st = pending.pop((h, t))
        mt = jnp.max(st, axis=0, keepdims=True)
        if m[h] is None:
            m[h] = mt
            acc[h] = _dot(values_t(h, t), jnp.exp2(st - mt).astype(BF16))
        else:
            m_new = jnp.maximum(m[h], mt)
            alpha = jnp.exp2(m[h] - m_new)
            acc[h] = alpha * acc[h] + _dot(values_t(h, t), jnp.exp2(st - m_new).astype(BF16))
            m[h] = m_new
    outs = [acc[h][:MLA_V] / acc[h][MLA_V:MLA_V + 1] for h in range(heads)]
    o_ref[...] = jnp.concatenate(outs, axis=0).T.astype(BF16)


def _attention_ctx(qp, kp, vtp):
    hw = MLA_HEADS * HEAD_W
    return pl.pallas_call(
        functools.partial(_attn_kernel, heads=MLA_HEADS, has_cache=False),
        grid=(BATCH,),
        in_specs=[pl.BlockSpec((SEQ, hw), lambda s: (s, 0)),
                  pl.BlockSpec((SEQ, hw), lambda s: (s, 0)),
                  pl.BlockSpec((VT_ROWS, SEQ), lambda s: (0, s))],
        out_specs=pl.BlockSpec((SEQ, MLA_HEADS * MLA_V), lambda s: (s, 0)),
        out_shape=jax.ShapeDtypeStruct((NCTX, MLA_HEADS * MLA_V), BF16),
        compiler_params=_cp(("arbitrary",)),
        name="attn_ctx",
    )(qp, kp, vtp)


def _attention_smp(qp, kp, vtp, kc, vct, layer):
    hpb = 4
    wq = hpb * HEAD_W
    wv = hpb * MLA_V
    nq = DEC_SEQ // TQ_ATT
    qoff = NCTX // TQ_ATT
    koff = NCTX // DEC_SEQ
    return pl.pallas_call(
        functools.partial(_attn_kernel, heads=hpb, has_cache=True),
        grid=(DEC_BATCH, MLA_HEADS // hpb, nq),
        in_specs=[pl.BlockSpec((TQ_ATT, wq), lambda b, g, i: (qoff + b * nq + i, g)),
                  pl.BlockSpec((DEC_SEQ, wq), lambda b, g, i: (koff + b, g)),
                  pl.BlockSpec((wv, DEC_SEQ), lambda b, g, i: (g, koff + b)),
                  pl.BlockSpec((1, 1, PAST_LEN, wq), lambda b, g, i: (b, layer, 0, g)),
                  pl.BlockSpec((1, 1, wv, PAST_LEN), lambda b, g, i: (b, layer, g, 0))],
        out_specs=pl.BlockSpec((TQ_ATT, hpb * MLA_V), lambda b, g, i: (b * nq + i, g)),
        out_shape=jax.ShapeDtypeStruct((NSMP, MLA_HEADS * MLA_V), BF16),
        compiler_params=_cp(("arbitrary", "arbitrary", "arbitrary")),
        name="attn_smp",
    )(qp, kp, vtp, kc, vct)


def _scan_specs(ctx, blk, widths_and_cols, dir_cols=()):
    nseq, seq_len = (BATCH, SEQ) if ctx else (DEC_BATCH, DEC_SEQ)
    nblk = seq_len // blk
    roff = 0 if ctx else NCTX // blk

    def rb(dd):
        return lambda s, j: roff + s * nblk + (j if dd == 0 else nblk - 1 - j)

    specs = []
    for dd in range(2):
        r = rb(dd)
        specs += [pl.BlockSpec((blk, w), lambda s, j, r=r, c=c: (r(s, j), c)) for w, c in widths_and_cols]
        specs += [pl.BlockSpec((blk, w), lambda s, j, r=r, c=c + dd: (r(s, j), c)) for w, c in dir_cols]
    out_maps = [lambda s, j, r=rb(dd): (r(s, j) - roff, 0) for dd in range(2)]
    return nseq, nblk, specs, out_maps


def _gla_kernel(*refs, nblk, has_s0, nch):
    per_dir = [refs[0:4], refs[4:8]]
    wg_ref, bg_ref = refs[8:10]
    if has_s0:
        s0_ref, of_ref, ob_ref, s_scr = refs[10:]
    else:
        of_ref, ob_ref, sfin_ref, s_scr = refs[10:]
    o_refs = [of_ref, ob_ref]
    j = pl.program_id(1)

    @pl.when(j == 0)
    def _():
        if has_s0:
            s_scr[...] = s0_ref[0, 0]
        else:
            s_scr[...] = jnp.zeros_like(s_scr)

    r64 = lax.broadcasted_iota(jnp.int32, (CHUNK, CHUNK), 0)
    c64 = lax.broadcasted_iota(jnp.int32, (CHUNK, CHUNK), 1)
    incl = [c64 <= r64, c64 >= r64]
    tri_b = [jnp.where(m, 1.0, 0.0).astype(BF16) for m in incl]
    heads = [(slice(h * GLA_DK, (h + 1) * GLA_DK), slice(h * GLA_DV, (h + 1) * GLA_DV))
             for h in range(GLA_HEADS)]

    ch = []
    for c in range(nch):
        for dd in range(2):
            ci = c if dd == 0 else nch - 1 - c
            rows = slice(ci * CHUNK, (ci + 1) * CHUNK)
            glr_ref = per_dir[dd][3]
            ch.append(dict(dd=dd, rows=rows, z=_dot(glr_ref[rows, :], wg_ref[0, dd]) + bg_ref[0, dd]))
    for t in ch:
        t["la"] = _split3(_log_sigmoid(t["z"]) * (1.0 / GLA_TAU))
    for t in ch:
        hi, mid, lo = t["la"]
        tb = tri_b[t["dd"]]
        t["bc"] = _dot(tb, hi) + (_dot(tb, mid) + _dot(tb, lo))
    for t in ch:
        q_ref, k_ref, v_ref, _ = per_dir[t["dd"]]
        rows, bc = t["rows"], t["bc"]
        q = q_ref[rows, :].astype(F32) * (GLA_DK ** -0.5)
        k = k_ref[rows, :].astype(F32)
        bl = bc[CHUNK - 1:CHUNK] if t["dd"] == 0 else bc[0:1]
        t["qd"] = (q * jnp.exp(bc)).astype(BF16)
        t["ki"] = (k * jnp.exp(-bc)).astype(BF16)
        t["ke"] = (k * jnp.exp(bl - bc)).astype(BF16)
        t["dec_t"] = jnp.transpose(jnp.broadcast_to(jnp.exp(bl), (LANE, GLA_HEADS * GLA_DK)))
        t["v"] = v_ref[rows, :]
    for t in ch:
        t["a"] = [jnp.where(incl[t["dd"]], _dot_nt(t["qd"][:, sk], t["ki"][:, sk]), 0.0).astype(BF16)
                  for sk, _ in heads]
    for t in ch:
        t["oi"] = [_dot(t["a"][h], t["v"][:, sv]) for h, (_, sv) in enumerate(heads)]
        t["upd"] = [_dot_tn(t["ke"][:, sk], t["v"][:, sv]) for sk, sv in heads]
    states = [[s_scr[dd, h] for h in range(GLA_HEADS)] for dd in range(2)]
    for t in ch:
        st = states[t["dd"]]
        t["s_in"] = [s.astype(BF16) for s in st]
        for h, (sk, _) in enumerate(heads):
            st[h] = t["dec_t"][sk, :] * st[h] + t["upd"][h]
    for t in ch:
        for h, (sk, sv) in enumerate(heads):
            o_refs[t["dd"]][t["rows"], sv] = t["oi"][h] + _dot(t["qd"][:, sk], t["s_in"][h])
    for dd in range(2):
        for h in range(GLA_HEADS):
            s_scr[dd, h] = states[dd][h]

    if not has_s0:
        @pl.when(j == nblk - 1)
        def _():
            sfin_ref[0] = s_scr[...]


def _gla(proj, wg_p, bg_p, s0, layer, ctx):
    blk = GLA_BLK_CTX if ctx else GLA_BLK_SMP
    hk = GLA_HEADS * GLA_DK
    hv = GLA_HEADS * GLA_DV
    nseq, nblk, in_specs, out_maps = _scan_specs(
        ctx, blk, [(hk, PC_GQ // hk), (hk, PC_GK // hk), (hv, PC_GV // hv), (LANE, PC_GLR // LANE)])
    in_specs += [pl.BlockSpec((1, 2, LANE, hk), lambda s, j: (layer, 0, 0, 0)),
                 pl.BlockSpec((1, 2, 1, hk), lambda s, j: (layer, 0, 0, 0))]
    args = [proj] * 8 + [wg_p, bg_p]
    nrows = NCTX if ctx else NSMP
    out_specs = [pl.BlockSpec((blk, hv), m) for m in out_maps]
    out_shape = [jax.ShapeDtypeStruct((nrows, hv), F32)] * 2
    st_blk = (2, GLA_HEADS, GLA_DK, GLA_DV)
    if ctx:
        out_specs.append(pl.BlockSpec((1,) + st_blk, lambda s, j: (s, 0, 0, 0, 0)))
        out_shape.append(jax.ShapeDtypeStruct((BATCH,) + st_blk, F32))
    else:
        in_specs.append(pl.BlockSpec((1, 1) + st_blk, lambda s, j: (s, layer, 0, 0, 0, 0)))
        args.append(s0)
    return pl.pallas_call(
        functools.partial(_gla_kernel, nblk=nblk, has_s0=not ctx, nch=blk // CHUNK),
        grid=(nseq, nblk),
        in_specs=in_specs, out_specs=out_specs, out_shape=out_shape,
        scratch_shapes=[pltpu.VMEM(st_blk, F32)],
        compiler_params=_cp(("arbitrary", "arbitrary")),
        name="gla_ctx" if ctx else "gla_smp",
    )(*args)


GDN_G = 2
GDN_GW = GDN_G * GDN_DK
HALO = 16


def _gdn_pre_kernel(x_ref, xp_ref, xn_ref, cw_ref, o_ref, *, blocks_per_seq_ctx, blocks_per_seq_smp):
    i = pl.program_id(0)
    nctx = NCTX // SEQ_BLK
    bi = jnp.where(i < nctx, i % blocks_per_seq_ctx, (i - nctx) % blocks_per_seq_smp)
    nblk = jnp.where(i < nctx, blocks_per_seq_ctx, blocks_per_seq_smp)
    hk = GDN_HEADS * GDN_DK
    x = x_ref[...].astype(F32)
    rowi = lax.broadcasted_iota(jnp.int32, x.shape, 0)
    prev_row = jnp.where(bi > 0, xp_ref[...].astype(F32)[HALO - 1:HALO], 0.0)
    next_row = jnp.where(bi < nblk - 1, xn_ref[...].astype(F32)[0:1], 0.0)
    x_prev = jnp.where(rowi == 0, prev_row, pltpu.roll(x, 1, 0))
    x_next = jnp.where(rowi == SEQ_BLK - 1, next_row, pltpu.roll(x, SEQ_BLK - 1, 0))
    cw = 0.5 * cw_ref[0]
    y = _silu_of_twice(x_prev * cw[0:1] + x * cw[1:2] + x_next * cw[2:3])

    lane = lax.broadcasted_iota(jnp.int32, (SEQ_BLK, LANE), 1)
    lo_half = lane < GDN_DK

    def l2n(t, scale):
        parts = []
        for c in range(hk // LANE):
            tc = t[:, c * LANE:(c + 1) * LANE]
            sq = tc * tc
            s_lo = jnp.sum(jnp.where(lo_half, sq, 0.0), axis=-1, keepdims=True)
            s_hi = jnp.sum(jnp.where(lo_half, 0.0, sq), axis=-1, keepdims=True)
            parts.append(tc * (lax.rsqrt(jnp.where(lo_half, s_lo, s_hi) + EPS) * scale))
        return jnp.concatenate(parts, axis=1)

    o_ref[:, :hk] = l2n(y[:, :hk], GDN_DK ** -0.5).astype(BF16)
    o_ref[:, hk:2 * hk] = l2n(y[:, hk:2 * hk], 1.0).astype(BF16)
    o_ref[:, 2 * hk:] = y[:, 2 * hk:].astype(BF16)


def _gdn_pre(proj, conv_w, layer):
    cw = 2 * GDN_HEADS * GDN_DK + GDN_HEADS * GDN_DV
    halo = SEQ_BLK // HALO
    last_h = R // HALO - 1
    return pl.pallas_call(
        functools.partial(_gdn_pre_kernel, blocks_per_seq_ctx=SEQ // SEQ_BLK,
                          blocks_per_seq_smp=DEC_SEQ // SEQ_BLK),
        grid=(R // SEQ_BLK,),
        in_specs=[pl.BlockSpec((SEQ_BLK, cw), lambda i: (i, PC_DQKV // cw)),
                  pl.BlockSpec((HALO, cw), lambda i: (jnp.maximum(i * halo - 1, 0), PC_DQKV // cw)),
                  pl.BlockSpec((HALO, cw), lambda i: (jnp.minimum((i + 1) * halo, last_h), PC_DQKV // cw)),
                  pl.BlockSpec((1, 3, cw), lambda i: (layer, 0, 0))],
        out_specs=pl.BlockSpec((SEQ_BLK, cw), lambda i: (i, 0)),
        out_shape=jax.ShapeDtypeStruct((R, cw), BF16),
        compiler_params=_cp(("arbitrary",)),
        name="gdn_pre",
    )(proj, proj, proj, conv_w)


def _gdn_kernel(*refs, nblk, has_s0):
    per_dir = [refs[0:4], refs[4:8]]
    alog_ref, dtb_ref = refs[8:10]
    if has_s0:
        s0_ref, of_ref, ob_ref, s_scr = refs[10:]
    else:
        of_ref, ob_ref, sfin_ref, s_scr = refs[10:]
    o_refs = [of_ref, ob_ref]
    j = pl.program_id(1)
    ngrp = GDN_HEADS // GDN_G

    def head_block(h):
        g, hh = divmod(h, GDN_G)
        return g, slice(hh * GDN_DK, (hh + 1) * GDN_DK), slice(hh * GDN_DV, (hh + 1) * GDN_DV)

    @pl.when(j == 0)
    def _():
        s_scr[...] = jnp.zeros_like(s_scr)
        if has_s0:
            for dd in range(2):
                for h in range(GDN_HEADS):
                    g, rk, rv = head_block(h)
                    s_scr[dd, g, rk, rv] = s0_ref[0, 0, dd, h]

    row = lax.broadcasted_iota(jnp.int32, (CHUNK, GDN_GW), 0)
    col = lax.broadcasted_iota(jnp.int32, (CHUNK, GDN_GW), 1) % CHUNK
    r64 = lax.broadcasted_iota(jnp.int32, (CHUNK, CHUNK), 0)
    c64 = lax.broadcasted_iota(jnp.int32, (CHUNK, CHUNK), 1)
    dir_masks = []
    for dd in range(2):
        lag = (row - col) if dd == 0 else (col - row)
        tri = (c64 <= r64) if dd == 0 else (c64 >= r64)
        dir_masks.append(dict(incl=lag >= 0, strict=lag > 0,
                              tri_b=jnp.where(tri, 1.0, 0.0).astype(BF16)))
    eye = col == row
    eye_f = jnp.where(eye, 1.0, 0.0)
    blk8 = (row // 8) == (col // 8)
    blk16 = (row // 16) == (col // 16)
    blk32 = (row // 32) == (col // 32)
    brow = lax.broadcasted_iota(jnp.int32, (GDN_GW, GDN_GW), 0) // CHUNK
    bcol = lax.broadcasted_iota(jnp.int32, (GDN_GW, GDN_GW), 1) // CHUNK
    same_head = brow == bcol

    def bdiag(t):
        tb = t.astype(BF16)
        return jnp.where(same_head, jnp.concatenate([tb] * GDN_G, axis=0), jnp.zeros((), BF16))

    def bdmm(a, b):
        return _dot(a.astype(BF16), bdiag(b))

    def setup(dd, c):
        q_ref, k_ref, v_ref, ab_ref = per_dir[dd]
        mk = dir_masks[dd]
        ci = c if dd == 0 else NCH - 1 - c
        rows = slice(ci * CHUNK, (ci + 1) * CHUNK)
        ab = ab_ref[rows, :].astype(F32)
        neg_a = -jnp.exp(alog_ref[0, dd])
        gam = _cumsum_rows(mk["tri_b"], neg_a * _softplus(ab + dtb_ref[0, dd]))
        beta_all = _sigmoid(ab)
        out = []
        for g in range(ngrp):
            gl = slice(g * GDN_GW, (g + 1) * GDN_GW)
            gcol = jnp.concatenate(
                [jnp.broadcast_to(gam[:, g * GDN_G + h:g * GDN_G + h + 1], (CHUNK, GDN_DK))
                 for h in range(GDN_G)], axis=1)
            bcol_ = jnp.concatenate(
                [jnp.broadcast_to(beta_all[:, GDN_HEADS + g * GDN_G + h:GDN_HEADS + g * GDN_G + h + 1],
                                  (CHUNK, GDN_DK)) for h in range(GDN_G)], axis=1)
            grow = jnp.sum(jnp.where(eye, gcol, 0.0), axis=0, keepdims=True)
            glast = gcol[CHUNK - 1:CHUNK] if dd == 0 else gcol[0:1]
            decay = jnp.where(mk["incl"], jnp.exp(gcol - grow), 0.0)
            egc = jnp.exp(gcol)
            qg = q_ref[rows, gl].astype(F32)
            kg = k_ref[rows, gl].astype(F32)
            vg = v_ref[rows, gl].astype(F32)
            kq = _dot_nt(jnp.concatenate([k_ref[rows, gl], q_ref[rows, gl]], axis=0),
                         bdiag(k_ref[rows, gl]))
            out.append(dict(dd=dd, g=g, rows=rows, gl=gl,
                            m=jnp.where(mk["strict"], bcol_ * kq[:CHUNK] * decay, 0.0),
                            aqk=(kq[CHUNK:] * decay).astype(BF16),
                            vb=vg * bcol_, kb=kg * (bcol_ * egc), qe=qg * egc,
                            kend=(kg * jnp.exp(glast - gcol)).astype(BF16),
                            eg=jnp.exp(glast)))
        return out

    def stages(ch):
        pairs = list(zip(ch[0::2], ch[1::2]))

        def bdmm_pairs(lhs, rhs, out):
            for ta, tb in pairs:
                a = jnp.concatenate([ta[lhs].astype(BF16), tb[lhs].astype(BF16)], axis=0)
                w = jnp.concatenate([bdiag(ta[rhs]), bdiag(tb[rhs])], axis=1)
                y = _dot(a, w)
                ta[out] = y[:CHUNK, :GDN_GW]
                tb[out] = y[CHUNK:, GDN_GW:]

        def neumann_a():
            for t in ch:
                t["n8"] = jnp.where(blk8, t["m"], 0.0)
            bdmm_pairs("n8", "n8", "n2")

        def neumann_b():
            for t in ch:
                t["p1"] = eye_f - t["n8"]
            bdmm_pairs("n2", "n2", "n4")
            bdmm_pairs("p1", "n2", "p1n2")
            for t in ch:
                t["p1"] = t["p1"] + t["p1n2"]

        def neumann_c():
            bdmm_pairs("p1", "n4", "p1n4")
            for t in ch:
                t["dinv"] = t["p1"] + t["p1n4"]

        def double_a(off):
            def run():
                for t in ch:
                    t["off"] = jnp.where(off, t["m"], 0.0)
                bdmm_pairs("dinv", "off", "dl")
            return run

        def double_b():
            bdmm_pairs("dl", "dinv", "dld")
            for t in ch:
                t["dinv"] = t["dinv"] - t["dld"]

        def solve():
            for t in ch:
                uw = _dot(t["dinv"].astype(BF16), jnp.concatenate([bdiag(t["vb"]), bdiag(t["kb"])], axis=1))
                t["u"] = uw[:, :GDN_GW]
                t["w"] = uw[:, GDN_GW:]

        def fold():
            for t in ch:
                wu = jnp.concatenate([t["w"], t["u"]], axis=1).astype(BF16)
                cb = _dot_tn(t["kend"], wu)
                t["c"] = jnp.where(same_head, cb[:, :GDN_GW], 0.0).astype(BF16)
                t["b"] = jnp.where(same_head, cb[:, GDN_GW:], 0.0)
                ao = _dot(t["aqk"], jnp.concatenate([bdiag(t["w"]), bdiag(t["u"])], axis=1))
                t["qt"] = (t["qe"] - ao[:, :GDN_GW]).astype(BF16)
                t["o"] = ao[:, GDN_GW:]

        out = [neumann_a, neumann_b, neumann_c]
        for inner, outer in ((blk8, blk16), (blk16, blk32), (blk32, None)):
            off = jnp.logical_not(inner) if outer is None else jnp.logical_and(outer, jnp.logical_not(inner))
            out += [double_a(off), double_b]
        return out + [solve, fold]

    states = [[s_scr[dd, g] for g in range(ngrp)] for dd in range(2)]

    def recur(cur):
        ys = [_dot(jnp.concatenate([t["qt"], t["c"]], axis=0), states[t["dd"]][t["g"]].astype(BF16))
              for t in cur]
        for t, y in zip(cur, ys):
            o_refs[t["dd"]][t["rows"], t["gl"]] = y[:CHUNK] + t["o"]
            states[t["dd"]][t["g"]] = t["eg"] * states[t["dd"]][t["g"]] - y[CHUNK:] + t["b"]

    half = NCH // 2
    chains = {}
    for c in range(half):
        for dd in range(2):
            chains[(dd, c)] = setup(dd, c)
    later = [(dd, c) for c in range(half, NCH) for dd in range(2)]
    wave_a = [t for c in range(half) for dd in range(2) for t in chains[(dd, c)]]
    for k, stage in enumerate(stages(wave_a)):
        stage()
        if k % 2 == 0 and later:
            key = later.pop(0)
            chains[key] = setup(*key)
    for key in later:
        chains[key] = setup(*key)
    wave_b = [t for c in range(half, NCH) for dd in range(2) for t in chains[(dd, c)]]
    done = 0
    for k, stage in enumerate(stages(wave_b)):
        stage()
        if k % 3 == 1 and done < half:
            recur(chains[(0, done)] + chains[(1, done)])
            done += 1
    for c in range(done, NCH):
        recur(chains[(0, c)] + chains[(1, c)])
    for dd in range(2):
        for g in range(ngrp):
            s_scr[dd, g] = states[dd][g]

    if not has_s0:
        @pl.when(j == nblk - 1)
        def _():
            for dd in range(2):
                for h in range(GDN_HEADS):
                    g, rk, rv = head_block(h)
                    sfin_ref[0, dd, h] = s_scr[dd, g, rk, rv]


def _gdn(qkv, proj, alog_p, dtb_p, s0, layer, ctx):
    nseq, nblk = (BATCH, SEQ // SEQ_BLK) if ctx else (DEC_BATCH, DEC_SEQ // SEQ_BLK)
    roff = 0 if ctx else NCTX // SEQ_BLK
    hk = GDN_HEADS * GDN_DK
    hv = GDN_HEADS * GDN_DV

    def rb(dd):
        return lambda s, j: roff + s * nblk + (j if dd == 0 else nblk - 1 - j)

    in_specs, args = [], []
    for dd in range(2):
        r = rb(dd)
        in_specs += [pl.BlockSpec((SEQ_BLK, hk), lambda s, j, r=r: (r(s, j), 0)),
                     pl.BlockSpec((SEQ_BLK, hk), lambda s, j, r=r: (r(s, j), 1)),
                     pl.BlockSpec((SEQ_BLK, hv), lambda s, j, r=r: (r(s, j), 2 * hk // hv)),
                     pl.BlockSpec((SEQ_BLK, LANE), lambda s, j, r=r, dd=dd: (r(s, j), PC_DAB // LANE + dd))]
        args += [qkv, qkv, qkv, proj]
    in_specs += [pl.BlockSpec((1, 2, 1, LANE), lambda s, j: (layer, 0, 0, 0))] * 2
    args += [alog_p, dtb_p]
    nrows = NCTX if ctx else NSMP
    out_specs = [pl.BlockSpec((SEQ_BLK, hv), lambda s, j, r=rb(dd): (r(s, j) - roff, 0)) for dd in range(2)]
    out_shape = [jax.ShapeDtypeStruct((nrows, hv), F32)] * 2
    st_blk = (2, GDN_HEADS, GDN_DK, GDN_DV)
    if ctx:
        out_specs.append(pl.BlockSpec((1,) + st_blk, lambda s, j: (s, 0, 0, 0, 0)))
        out_shape.append(jax.ShapeDtypeStruct((BATCH,) + st_blk, F32))
    else:
        in_specs.append(pl.BlockSpec((1, 1) + st_blk, lambda s, j: (s, layer, 0, 0, 0, 0)))
        args.append(s0)
    return pl.pallas_call(
        functools.partial(_gdn_kernel, nblk=nblk, has_s0=not ctx),
        grid=(nseq, nblk),
        in_specs=in_specs, out_specs=out_specs, out_shape=out_shape,
        scratch_shapes=[pltpu.VMEM((2, GDN_HEADS // GDN_G, GDN_GW, GDN_GW), F32)],
        compiler_params=_cp(("arbitrary", "arbitrary")),
        name="gdn_ctx" if ctx else "gdn_smp",
    )(*args)


def _group_rms(x, width):
    parts = []
    lane = lax.broadcasted_iota(jnp.int32, (x.shape[0], LANE), 1)
    lo_half = lane < 64
    for c in range(x.shape[1] // LANE):
        xc = x[:, c * LANE:(c + 1) * LANE]
        sq = xc * xc
        if width == LANE:
            ms = jnp.mean(sq, axis=-1, keepdims=True)
        else:
            s_lo = jnp.sum(jnp.where(lo_half, sq, 0.0), axis=-1, keepdims=True)
            s_hi = jnp.sum(jnp.where(lo_half, 0.0, sq), axis=-1, keepdims=True)
            ms = jnp.where(lo_half, s_lo, s_hi) * (1.0 / width)
        parts.append(xc * lax.rsqrt(ms + EPS))
    return jnp.concatenate(parts, axis=1)


def _merge_kernel(ymc_ref, yms_ref, ogcf_ref, ogcb_ref, ogsf_ref, ogsb_ref, gr_ref,
                  odcf_ref, odcb_ref, odsf_ref, odsb_ref,
                  dz_ref, gt_ref, *rest, split_h):
    if split_h:
        hc_ref, hs_ref, *rest = rest
        h_rows = lambda: _family_rows((hc_ref, hs_ref), TM_MERGE)
    else:
        h_ref, *rest = rest
        h_rows = lambda: h_ref[...]
    ga_ref, gn_ref, dn_ref, wb_ref, wo_ref, o_ref, ym_scr, og_scr, od_scr = rest
    is_ctx = pl.program_id(0) < NCTX // TM_MERGE

    @pl.when(is_ctx)
    def _():
        ym_scr[...] = ymc_ref[...]
        og_scr[...] = ogcf_ref[...] + ogcb_ref[...]
        od_scr[...] = odcf_ref[...] + odcb_ref[...]

    @pl.when(jnp.logical_not(is_ctx))
    def _():
        ym_scr[...] = yms_ref[...]
        og_scr[...] = ogsf_ref[...] + ogsb_ref[...]
        od_scr[...] = odsf_ref[...] + odsb_ref[...]

    y_gla = _group_rms(og_scr[...], GLA_DV) * gn_ref[0] * _silu_of_twice(gr_ref[...].astype(F32))
    y_gdn = _group_rms(od_scr[...], GDN_DV) * dn_ref[0] * _silu_of_twice(dz_ref[...].astype(F32))
    t = jnp.tanh(gt_ref[...].astype(F32))
    d = D_MODEL
    acc = None
    for n, y_n in enumerate((ym_scr[...], y_gla.astype(BF16), y_gdn.astype(BF16))):
        p = _dot(y_n, wb_ref[0, n])
        gated = p + p * t[:, n * d:(n + 1) * d]
        acc = gated if acc is None else acc + gated
    y = _dot(acc.astype(BF16), wo_ref[0])
    o_ref[...] = h_rows() + ga_ref[0] * y


def _merge(y_mla, o_gla, o_gdn, proj, h, mod3, gla_norm_p, gdn_norm_p, w_branch_b, w_out_b, layer):
    tm = TM_MERGE
    bw = 512
    nc = NCTX // tm
    split_h = isinstance(h, tuple)
    h_specs = _family_specs(tm, D_MODEL) if split_h else [pl.BlockSpec((tm, D_MODEL), lambda i: (i, 0))]
    h_args = list(h) if split_h else [h]

    def cmap(i):
        return jnp.minimum(i, nc - 1)

    def smap(i):
        return jnp.maximum(i - nc, 0)

    return pl.pallas_call(
        functools.partial(_merge_kernel, split_h=split_h),
        grid=(R // tm,),
        in_specs=[pl.BlockSpec((tm, bw), lambda i: (cmap(i), 0)),
                  pl.BlockSpec((tm, bw), lambda i: (smap(i), 0)),
                  pl.BlockSpec((tm, bw), lambda i: (cmap(i), 0)),
                  pl.BlockSpec((tm, bw), lambda i: (cmap(i), 0)),
                  pl.BlockSpec((tm, bw), lambda i: (smap(i), 0)),
                  pl.BlockSpec((tm, bw), lambda i: (smap(i), 0)),
                  pl.BlockSpec((tm, bw), lambda i: (i, PC_GR // bw)),
                  pl.BlockSpec((tm, bw), lambda i: (cmap(i), 0)),
                  pl.BlockSpec((tm, bw), lambda i: (cmap(i), 0)),
                  pl.BlockSpec((tm, bw), lambda i: (smap(i), 0)),
                  pl.BlockSpec((tm, bw), lambda i: (smap(i), 0)),
                  pl.BlockSpec((tm, bw), lambda i: (i, PC_DZ // bw)),
                  pl.BlockSpec((tm, 3 * D_MODEL), lambda i: (i, PC_GATES // (3 * D_MODEL)))] + h_specs + [
                  _mod_spec(layer, 2, tm),
                  pl.BlockSpec((1, 1, bw), lambda i: (layer, 0, 0)),
                  pl.BlockSpec((1, 1, bw), lambda i: (layer, 0, 0)),
                  pl.BlockSpec((1, 3, bw, D_MODEL), lambda i: (layer, 0, 0, 0), pipeline_mode=pl.Buffered(1)),
                  pl.BlockSpec((1, D_MODEL, D_MODEL), lambda i: (layer, 0, 0), pipeline_mode=pl.Buffered(1))],
        out_specs=pl.BlockSpec((tm, D_MODEL), lambda i: (i, 0)),
        out_shape=jax.ShapeDtypeStruct((R, D_MODEL), F32),
        scratch_shapes=[pltpu.VMEM((tm, bw), BF16), pltpu.VMEM((tm, bw), F32), pltpu.VMEM((tm, bw), F32)],
        compiler_params=_cp(("arbitrary",)),
        name="merge",
    )(y_mla[0], y_mla[1], o_gla[0][0], o_gla[0][1], o_gla[1][0], o_gla[1][1], proj,
      o_gdn[0][0], o_gdn[0][1], o_gdn[1][0], o_gdn[1][1],
      proj, proj, *h_args, mod3, gla_norm_p, gdn_norm_p, w_branch_b, w_out_b)


def _ffn_kernel(h_ref, g_ref, sc_ref, sh_ref, gf_ref, wi_ref, wo_ref, *rest, final):
    if final:
        fn_ref, *o_refs = rest
    else:
        gn_ref, scn_ref, shn_ref, *o_refs = rest
    h = h_ref[...]
    xf = (_rms(h, g_ref[0]) * (1.0 + sc_ref[0]) + sh_ref[0]).astype(BF16)

    def gate_up(c):
        f0 = c * TF_FFN
        return (_dot(xf, wi_ref[0, :, f0:f0 + TF_FFN]),
                _dot(xf, wi_ref[0, :, D_FF + f0:D_FF + f0 + TF_FFN]))

    n_chunks = D_FF // TF_FFN
    nxt = gate_up(0)
    acc = None
    for c in range(n_chunks):
        g, u = nxt
        if c + 1 < n_chunks:
            nxt = gate_up(c + 1)
        part = _dot((_silu(g) * u).astype(BF16), wo_ref[0, c * TF_FFN:(c + 1) * TF_FFN, :])
        acc = part if acc is None else acc + part
    out = h + gf_ref[0] * acc
    if not final:
        o_refs[0][...] = out
        o_refs[1][...] = (_rms(out, gn_ref[0]) * (1.0 + scn_ref[0]) + shn_ref[0]).astype(BF16)
    else:
        out = _rms(out, fn_ref[...])
        is_ctx = pl.program_id(0) < NCTX // TM_FFN

        @pl.when(is_ctx)
        def _():
            o_refs[0][...] = out

        @pl.when(jnp.logical_not(is_ctx))
        def _():
            o_refs[1][...] = out


def _ffn(h, mod3, norm_ffn, wi_b, wo_b, tail, layer, final):
    tm = TM_FFN
    if final:
        nc = NCTX // tm
        out_specs = [pl.BlockSpec((tm, D_MODEL), lambda i: (jnp.minimum(i, nc - 1), 0)),
                     pl.BlockSpec((tm, D_MODEL), lambda i: (jnp.maximum(i - nc, 0), 0))]
        out_shape = [jax.ShapeDtypeStruct((NCTX, D_MODEL), F32), jax.ShapeDtypeStruct((NSMP, D_MODEL), F32)]
        tail_specs = [pl.BlockSpec((1, D_MODEL), lambda i: (0, 0))]
        tail_args = [tail.reshape(1, D_MODEL)]
    else:
        out_specs = [pl.BlockSpec((tm, D_MODEL), lambda i: (i, 0))] * 2
        out_shape = [jax.ShapeDtypeStruct((R, D_MODEL), F32), jax.ShapeDtypeStruct((R, D_MODEL), BF16)]
        tail_specs = [pl.BlockSpec((1, 1, D_MODEL), lambda i: (layer + 1, 0, 0)),
                      _mod_spec(layer + 1, 1, tm), _mod_spec(layer + 1, 0, tm)]
        tail_args = [tail.reshape(DEPTH, 1, D_MODEL), mod3, mod3]
    return pl.pallas_call(
        functools.partial(_ffn_kernel, final=final),
        grid=(R // tm,),
        in_specs=[pl.BlockSpec((tm, D_MODEL), lambda i: (i, 0)),
                  pl.BlockSpec((1, 1, D_MODEL), lambda i: (layer, 0, 0)),
                  _mod_spec(layer, 4, tm),
                  _mod_spec(layer, 3, tm),
                  _mod_spec(layer, 5, tm),
                  pl.BlockSpec((1, D_MODEL, 2 * D_FF), lambda i: (layer, 0, 0), pipeline_mode=pl.Buffered(1)),
                  pl.BlockSpec((1, D_FF, D_MODEL), lambda i: (layer, 0, 0), pipeline_mode=pl.Buffered(1))]
        + tail_specs,
        out_specs=out_specs, out_shape=out_shape,
        compiler_params=_cp(("arbitrary",)),
        name="ffn_final" if final else "ffn",
    )(h, norm_ffn.reshape(DEPTH, 1, D_MODEL), mod3, mod3, mod3, wi_b, wo_b, *tail_args)


def _rope_partner(w):
    h = AXIS_DIM // 2
    parts = []
    for a in range(2):
        x1 = w[..., a * AXIS_DIM:a * AXIS_DIM + h]
        x2 = w[..., a * AXIS_DIM + h:(a + 1) * AXIS_DIM]
        parts += [-x2, x1]
    return jnp.concatenate(parts, axis=-1)


def _pack_w_in_kernel(w_ref, o_ref):
    o_ref[0] = _pack_columns(w_ref[0].astype(F32)).astype(BF16)


def _pack_w_in(w_in):
    tr = 256
    return pl.pallas_call(
        _pack_w_in_kernel,
        grid=(DEPTH, D_MODEL // tr),
        in_specs=[pl.BlockSpec((1, tr, w_in.shape[-1]), lambda l, i: (l, i, 0))],
        out_specs=pl.BlockSpec((1, tr, PROJ_W), lambda l, i: (l, i, 0)),
        out_shape=jax.ShapeDtypeStruct((DEPTH, D_MODEL, PROJ_W), BF16),
        compiler_params=_cp(("arbitrary", "arbitrary")),
        name="pack_w_in",
    )(w_in.astype(BF16))


def _pack_columns(w_in):
    c = [0]
    for s in (Q_LORA, KV_LORA + ROPE_DIM, 256, 256, 512, 512, 2 * GLA_RANK, 1536, 512,
              2 * GDN_HEADS, 2 * GDN_HEADS, 3 * D_MODEL):
        c.append(c[-1] + s)
    seg = [w_in[..., c[i]:c[i + 1]] for i in range(12)]
    mq, mkv, gq, gk, gv, gr, glr, dqkv, dz, da, db, gates = seg

    def zeros(n):
        return jnp.zeros(w_in.shape[:-1] + (n,), w_in.dtype)

    kr = mkv[..., KV_LORA:]
    mla = jnp.concatenate([mq, mkv[..., :KV_LORA], kr, _rope_partner(kr), zeros(64)], axis=-1)
    dab = []
    for d in range(2):
        dab += [da[..., d * GDN_HEADS:(d + 1) * GDN_HEADS], db[..., d * GDN_HEADS:(d + 1) * GDN_HEADS],
                zeros(LANE - 2 * GDN_HEADS)]
    packed = jnp.concatenate([0.5 * gates, dqkv, 0.5 * dz, gv, 0.5 * gr, mla, gq, gk, glr, zeros(LANE - 2 * GLA_RANK)]
                             + dab + [zeros(PROJ_W - PC_DAB - 2 * LANE)], axis=-1)
    return packed


def _pack_mla_weights(w_uq, w_ukv):
    l = w_uq.shape[0]
    qh = w_uq.reshape(l, Q_LORA, MLA_HEADS, MLA_NOPE + ROPE_DIM)
    zq = jnp.zeros((l, Q_LORA, MLA_HEADS, HEAD_W - MLA_NOPE - ROPE_DIM), w_uq.dtype)
    zn = jnp.zeros((l, Q_LORA, MLA_HEADS, MLA_NOPE), w_uq.dtype)
    wq = jnp.concatenate([qh, zq], axis=-1).reshape(l, Q_LORA, MLA_HEADS * HEAD_W)
    wq_sw = jnp.concatenate([zn, _rope_partner(qh[..., MLA_NOPE:]), zq], axis=-1).reshape(l, Q_LORA, MLA_HEADS * HEAD_W)
    wq_p = jnp.concatenate([wq, wq_sw], axis=-1).astype(BF16)

    kvh = w_ukv.reshape(l, KV_LORA, MLA_HEADS, MLA_NOPE + MLA_V)
    zk = jnp.zeros((l, KV_LORA, MLA_HEADS, HEAD_W - MLA_NOPE), w_ukv.dtype)
    wk_top = jnp.concatenate([kvh[..., :MLA_NOPE], zk], axis=-1).reshape(l, KV_LORA, MLA_HEADS * HEAD_W)
    place = jnp.zeros((KV_LORA, MLA_HEADS, HEAD_W), w_ukv.dtype)
    idx = jnp.arange(ROPE_DIM)
    for rep in range(2):
        place = place.at[rep * ROPE_DIM + idx, :, MLA_NOPE + idx].set(1.0)
    wk_bot = jnp.broadcast_to(place.reshape(1, KV_LORA, MLA_HEADS * HEAD_W), (l, KV_LORA, MLA_HEADS * HEAD_W))
    wk_p = jnp.concatenate([wk_top, wk_bot], axis=1).astype(BF16)
    wv_p = jnp.transpose(kvh[..., MLA_NOPE:], (0, 2, 3, 1)).reshape(l, VT_ROWS, KV_LORA).astype(BF16)
    return wq_p, wk_p, wv_p


def _rope_tables(tm):
    rows = DEC_SEQ // GRID_W
    row = jnp.repeat(jnp.arange(rows, dtype=F32), GRID_W)
    col = jnp.tile(jnp.arange(GRID_W, dtype=F32), rows)
    inv = ROPE_THETA ** (-jnp.arange(0, AXIS_DIM, 2, dtype=F32) / AXIS_DIM)
    ang_r, ang_c = row[:, None] * inv, col[:, None] * inv
    cos32 = jnp.concatenate([jnp.cos(ang_r)] * 2 + [jnp.cos(ang_c)] * 2, axis=-1)
    sin32 = jnp.concatenate([jnp.sin(ang_r)] * 2 + [jnp.sin(ang_c)] * 2, axis=-1)
    cos32 = jnp.concatenate([jnp.ones((tm, ROPE_DIM), F32), cos32], axis=0)
    sin32 = jnp.concatenate([jnp.zeros((tm, ROPE_DIM), F32), sin32], axis=0)
    n = cos32.shape[0]
    scale = (MLA_NOPE + ROPE_DIM) ** -0.5 * math.log2(math.e)
    pad = HEAD_W - MLA_NOPE - ROPE_DIM
    cq = jnp.concatenate([jnp.ones((n, MLA_NOPE), F32), cos32, jnp.zeros((n, pad), F32)], axis=-1) * scale
    sq = jnp.concatenate([jnp.zeros((n, MLA_NOPE), F32), sin32, jnp.zeros((n, pad), F32)], axis=-1) * scale
    ck = jnp.concatenate([cos32, sin32, jnp.zeros((n, HEAD_W - 2 * ROPE_DIM), F32)], axis=-1)
    return jnp.concatenate([cq, sq, ck], axis=-1)


def _pad_lanes(x, n):
    return jnp.pad(x, [(0, 0)] * (x.ndim - 1) + [(0, n - x.shape[-1])])


def kernel(x_prompt, x_sample, cache_mla, state_gla, state_gdn, c, c_ctx, w_mod, b_mod, norm_mix, w_in,
           mla_q_norm, mla_w_uq, mla_kv_norm, mla_w_ukv, gla_w_gate, gla_b_gate, gla_norm, gdn_conv,
           gdn_a_log, gdn_dt_bias, gdn_norm, w_branch, w_out, norm_ffn, ffn_w_in, ffn_w_out, final_norm):
    w_in_p = _pack_w_in(w_in)
    wq_p, wk_p, wv_p = _pack_mla_weights(mla_w_uq, mla_w_ukv)
    tab = _rope_tables(TM_PREP)
    wg_p = jnp.zeros((DEPTH, 2, LANE, GLA_HEADS * GLA_DK), F32)
    for d in range(2):
        wg_p = wg_p.at[:, d, d * GLA_RANK:(d + 1) * GLA_RANK, :].set(gla_w_gate[:, d])
    wg_p = wg_p.astype(BF16)
    bg_p = gla_b_gate.reshape(DEPTH, 2, 1, GLA_HEADS * GLA_DK)
    alog_p = _pad_lanes(gdn_a_log, LANE).reshape(DEPTH, 2, 1, LANE)
    dtb_p = _pad_lanes(gdn_dt_bias, LANE).reshape(DEPTH, 2, 1, LANE)
    gla_norm_p = jnp.tile(gla_norm, (1, GLA_HEADS)).reshape(DEPTH, 1, GLA_HEADS * GLA_DV)
    gdn_norm_p = jnp.tile(gdn_norm, (1, GDN_HEADS)).reshape(DEPTH, 1, GDN_HEADS * GDN_DV)
    w_branch_b = (0.5 * w_branch).astype(BF16)
    w_out_b = w_out.astype(BF16)
    wi_b = ffn_w_in.astype(BF16)
    wo_b = ffn_w_out.astype(BF16)
    cache_p = _pad_lanes(cache_mla, 2 * KV_LORA)
    cond8 = jnp.concatenate([c_ctx[None, :], c, jnp.zeros((MOD_ROWS - 1 - DEC_BATCH, D_MODEL), F32)], axis=0)

    mod = _modulation(cond8, w_mod, b_mod)
    mod3 = mod.reshape(DEPTH * MOD_ROWS * 6, 1, D_MODEL)
    kc, vc = _kv_cache(cache_p, wk_p, wv_p)

    h = (x_prompt.reshape(NCTX, D_MODEL), x_sample.reshape(NSMP, D_MODEL))
    kv_list, gla_list, gdn_list = [], [], []
    for l in range(DEPTH):
        proj = _inproj(h if l == 0 else xn, mod3, norm_mix, w_in_p, l)
        qp, kp, vtp, own = _mla_prep(proj, tab, mla_q_norm, mla_kv_norm, wq_p, wk_p, wv_p, l)
        y_mla = (_attention_ctx(qp, kp, vtp), _attention_smp(qp, kp, vtp, kc, vc, l))
        ogf_c, ogb_c, sg = _gla(proj, wg_p, bg_p, None, l, True)
        ogf_s, ogb_s = _gla(proj, wg_p, bg_p, state_gla, l, False)
        qkv = _gdn_pre(proj, gdn_conv, l)
        odf_c, odb_c, sd = _gdn(qkv, proj, alog_p, dtb_p, None, l, True)
        odf_s, odb_s = _gdn(qkv, proj, alog_p, dtb_p, state_gdn, l, False)
        h = _merge(y_mla, ((ogf_c, ogb_c), (ogf_s, ogb_s)), ((odf_c, odb_c), (odf_s, odb_s)), proj, h, mod3,
                   gla_norm_p, gdn_norm_p,
                   w_branch_b, w_out_b, l)
        if l == DEPTH - 1:
            h = _ffn(h, mod3, norm_ffn, wi_b, wo_b, final_norm, l, True)
        else:
            h, xn = _ffn(h, mod3, norm_ffn, wi_b, wo_b, norm_mix, l, False)
        kv_list.append(own[:NCTX, :KV_LORA + ROPE_DIM].reshape(BATCH, SEQ, KV_LORA + ROPE_DIM))
        gla_list.append(sg)
        gdn_list.append(sd)

    y_prompt = h[0].reshape(BATCH, SEQ, D_MODEL)
    y_sample = h[1].reshape(DEC_BATCH, DEC_SEQ, D_MODEL)
    return (y_prompt, y_sample, jnp.stack(kv_list, axis=1), jnp.stack(gla_list, axis=1),
            jnp.stack(gdn_list, axis=1))
```

```python
import functools
import math

import numpy as np
import jax
import jax.numpy as jnp
from jax import lax
from jax.experimental import pallas as pl
from jax.experimental.pallas import tpu as pltpu

F32 = jnp.float32
BF16 = jnp.bfloat16

D_MODEL = 1024
BATCH = 16
SEQ = 256
DEPTH = 2
DEC_BATCH = 2
DEC_SEQ = 4096
PAST_LEN = 256
GRID_W = 64
CHUNK = 64
EPS = 1e-6
MLA_HEADS = 8
MLA_NOPE = 64
ROPE_DIM = 32
AXIS_DIM = ROPE_DIM // 2
MLA_V = 64
Q_LORA = 256
KV_LORA = 128
ROPE_THETA = 10000.0
GLA_HEADS = 4
GLA_DK = 64
GLA_DV = 128
GLA_RANK = 16
GLA_TAU = 16.0
GDN_HEADS = 8
GDN_DK = 64
GDN_DV = 64
D_FF = ((8 * D_MODEL + 3 * 256 - 1) // (3 * 256)) * 256
MOD_W = 6 * D_MODEL

NCTX = BATCH * SEQ
NSMP = DEC_BATCH * DEC_SEQ
R = NCTX + NSMP
LANE = 128
HEAD_W = 128
VT_ROWS = MLA_HEADS * MLA_V
SEQ_BLK = 256
NCH = SEQ_BLK // CHUNK
GLA_BLK_CTX = 256
GLA_BLK_SMP = 512
MOD_ROWS = 8

PC_GATES = 0
PC_DQKV = 3072
PC_DZ = 4608
PC_GV = 5120
PC_GR = 5632
PC_MLA = 6144
PC_GQ = 6656
PC_GK = 6912
PC_GLR = 7168
PC_DAB = 7296
PROJ_W = 7680

TM_IN = 1024
TN_IN = 2560
TM_PREP = 512
TM_MERGE = 512
TM_FFN = 512
TF_FFN = 256
TQ_ATT = 256
TK_ATT = 512
ONES_ROWS = 16
ATT_AHEAD = 8
VMEM_LIMIT = 48 * 1024 * 1024


def _cp(sem):
    return pltpu.CompilerParams(dimension_semantics=sem, vmem_limit_bytes=VMEM_LIMIT)


def _dot(a, b):
    return jnp.dot(a, b, preferred_element_type=F32)


def _dot_nt(a, b):
    return lax.dot_general(a, b, (((1,), (1,)), ((), ())), preferred_element_type=F32)


def _dot_tn(a, b):
    return lax.dot_general(a, b, (((0,), (0,)), ((), ())), preferred_element_type=F32)


def _split2(x):
    hi = x.astype(BF16)
    lo = (x - hi.astype(F32)).astype(BF16)
    return hi, lo


def _split3(x):
    hi = x.astype(BF16)
    r1 = x - hi.astype(F32)
    mid = r1.astype(BF16)
    lo = (r1 - mid.astype(F32)).astype(BF16)
    return hi, mid, lo


def _cumsum_rows(tri_b, x):
    hi, mid, lo = _split3(x)
    return _dot(tri_b, hi) + (_dot(tri_b, mid) + _dot(tri_b, lo))


def _sigmoid(x):
    return 0.5 * jnp.tanh(0.5 * x) + 0.5


def _silu(x):
    return x * _sigmoid(x)


def _silu_of_twice(hx):
    return hx + hx * jnp.tanh(hx)


def _softplus(x):
    return jnp.maximum(x, 0.0) + jnp.log(1.0 + jnp.exp(-jnp.abs(x)))


def _log_sigmoid(x):
    return jnp.minimum(x, 0.0) - jnp.log(1.0 + jnp.exp(-jnp.abs(x)))


def _rms(x, g):
    return x * lax.rsqrt(jnp.mean(x * x, axis=-1, keepdims=True) + EPS) * g


def _mod_row(row_start):
    return jnp.where(row_start < NCTX, 0, 1 + (row_start - NCTX) // DEC_SEQ)


def _mod_spec(layer, which, tm, axis=0):
    def imap(*idx):
        return ((layer * MOD_ROWS + _mod_row(idx[axis] * tm)) * 6 + which, 0, 0)
    return pl.BlockSpec((1, 1, D_MODEL), imap)


def _mod_kernel(c_ref, w_ref, b_ref, o_ref):
    x = _silu(c_ref[...])
    xh, xl = _split2(x)
    wh, wl = _split2(w_ref[0])
    o_ref[0] = _dot(xh, wh) + (_dot(xl, wh) + _dot(xh, wl)) + b_ref[0]


def _modulation(cond8, w_mod, b_mod):
    tn = 1536
    return pl.pallas_call(
        _mod_kernel,
        grid=(DEPTH, MOD_W // tn),
        in_specs=[pl.BlockSpec((MOD_ROWS, D_MODEL), lambda l, j: (0, 0)),
                  pl.BlockSpec((1, D_MODEL, tn), lambda l, j: (l, 0, j)),
                  pl.BlockSpec((1, 1, tn), lambda l, j: (l, 0, j))],
        out_specs=pl.BlockSpec((1, MOD_ROWS, tn), lambda l, j: (l, 0, j)),
        out_shape=jax.ShapeDtypeStruct((DEPTH, MOD_ROWS, MOD_W), F32),
        compiler_params=_cp(("arbitrary", "arbitrary")),
        name="modulation",
    )(cond8, w_mod, b_mod.reshape(DEPTH, 1, MOD_W))


def _family_specs(tm, width, axis=0):
    nc = NCTX // tm
    return [pl.BlockSpec((tm, width), lambda *idx: (jnp.minimum(idx[axis], nc - 1), 0)),
            pl.BlockSpec((tm, width), lambda *idx: (jnp.maximum(idx[axis] - nc, 0), 0))]


def _family_rows(refs, tm, axis=0):
    is_ctx = pl.program_id(axis) < NCTX // tm
    return jnp.where(is_ctx, refs[0][...], refs[1][...])


def _inproj_kernel(*refs, normed):
    if normed:
        xn_ref, w_ref, o_ref = refs
        xn = xn_ref[...]
    else:
        hc_ref, hs_ref, g_ref, sc_ref, sh_ref, w_ref, o_ref = refs
        y = _rms(_family_rows((hc_ref, hs_ref), TM_IN, axis=1), g_ref[0])
        xn = (y * (1.0 + sc_ref[0]) + sh_ref[0]).astype(BF16)
    o_ref[...] = _dot_nt(xn, w_ref[0]).astype(BF16)


def _inproj(h, mod3, norm_mix, w_in_p, layer):
    normed = not isinstance(h, tuple)
    if normed:
        in_specs = [pl.BlockSpec((TM_IN, D_MODEL), lambda j, i: (i, 0))]
        args = [h]
    else:
        in_specs = _family_specs(TM_IN, D_MODEL, axis=1) + [
            pl.BlockSpec((1, 1, D_MODEL), lambda j, i: (layer, 0, 0)),
            _mod_spec(layer, 1, TM_IN, axis=1),
            _mod_spec(layer, 0, TM_IN, axis=1)]
        args = list(h) + [norm_mix.reshape(DEPTH, 1, D_MODEL), mod3, mod3]
    return pl.pallas_call(
        functools.partial(_inproj_kernel, normed=normed),
        grid=(PROJ_W // TN_IN, R // TM_IN),
        in_specs=in_specs + [pl.BlockSpec((1, TN_IN, D_MODEL), lambda j, i: (layer, j, 0))],
        out_specs=pl.BlockSpec((TM_IN, TN_IN), lambda j, i: (i, j)),
        out_shape=jax.ShapeDtypeStruct((R, PROJ_W), BF16),
        compiler_params=_cp(("arbitrary", "arbitrary")),
        name="inproj",
    )(*args, w_in_p)


def _mla_prep_kernel(pm_ref, tab_ref, qn_ref, kvn_ref, wq_ref, wk_ref, wv_ref,
                     q_ref, k_ref, v_ref, own_ref):
    pm = pm_ref[...].astype(F32)
    tab = tab_ref[...]
    qn = _rms(pm[:, :Q_LORA], qn_ref[0]).astype(BF16)
    q2 = _dot(qn, wq_ref[0])
    hw = MLA_HEADS * HEAD_W
    cq = jnp.tile(tab[:, :HEAD_W], (1, MLA_HEADS))
    sq = jnp.tile(tab[:, HEAD_W:2 * HEAD_W], (1, MLA_HEADS))
    q_ref[...] = (q2[:, :hw] * cq + q2[:, hw:] * sq).astype(BF16)
    ckv = _rms(pm[:, Q_LORA:Q_LORA + KV_LORA], kvn_ref[0])
    kr = pm[:, Q_LORA + KV_LORA:] * tab[:, 2 * HEAD_W:]
    lhs = jnp.concatenate([ckv, kr], axis=1)
    own_ref[...] = lhs
    lb = lhs.astype(BF16)
    k_ref[...] = _dot(lb, wk_ref[0]).astype(BF16)
    v_ref[...] = _dot_nt(wv_ref[0], lb[:, :KV_LORA]).astype(BF16)


def _mla_prep(proj, tab, q_norm, kv_norm, wq_p, wk_p, wv_p, layer):
    tm = TM_PREP
    hw = MLA_HEADS * HEAD_W

    def tab_map(i):
        r0 = i * tm
        return (jnp.where(r0 < NCTX, 0, 1 + ((r0 - NCTX) % DEC_SEQ) // tm), 0)

    return pl.pallas_call(
        _mla_prep_kernel,
        grid=(R // tm,),
        in_specs=[pl.BlockSpec((tm, 512), lambda i: (i, PC_MLA // 512)),
                  pl.BlockSpec((tm, 3 * HEAD_W), tab_map),
                  pl.BlockSpec((1, 1, Q_LORA), lambda i: (layer, 0, 0)),
                  pl.BlockSpec((1, 1, KV_LORA), lambda i: (layer, 0, 0)),
                  pl.BlockSpec((1, Q_LORA, 2 * hw), lambda i: (layer, 0, 0)),
                  pl.BlockSpec((1, 2 * KV_LORA, hw), lambda i: (layer, 0, 0)),
                  pl.BlockSpec((1, VT_ROWS, KV_LORA), lambda i: (layer, 0, 0))],
        out_specs=[pl.BlockSpec((tm, hw), lambda i: (i, 0)),
                   pl.BlockSpec((tm, hw), lambda i: (i, 0)),
                   pl.BlockSpec((VT_ROWS, tm), lambda i: (0, i)),
                   pl.BlockSpec((tm, 2 * KV_LORA), lambda i: (i, 0))],
        out_shape=[jax.ShapeDtypeStruct((R, hw), BF16),
                   jax.ShapeDtypeStruct((R, hw), BF16),
                   jax.ShapeDtypeStruct((VT_ROWS, R), BF16),
                   jax.ShapeDtypeStruct((R, 2 * KV_LORA), F32)],
        compiler_params=_cp(("arbitrary",)),
        name="mla_prep",
    )(proj, tab, q_norm.reshape(DEPTH, 1, Q_LORA), kv_norm.reshape(DEPTH, 1, KV_LORA),
      wq_p, wk_p, wv_p)


def _kv_cache_kernel(c_ref, wk_ref, wv_ref, k_ref, v_ref):
    lb = c_ref[0, 0].astype(BF16)
    k_ref[0, 0] = _dot(lb, wk_ref[0]).astype(BF16)
    v_ref[0, 0] = _dot_nt(wv_ref[0], lb[:, :KV_LORA]).astype(BF16)


def _kv_cache(cache_p, wk_p, wv_p):
    hw = MLA_HEADS * HEAD_W
    return pl.pallas_call(
        _kv_cache_kernel,
        grid=(DEC_BATCH, DEPTH),
        in_specs=[pl.BlockSpec((1, 1, PAST_LEN, 2 * KV_LORA), lambda b, l: (b, l, 0, 0)),
                  pl.BlockSpec((1, 2 * KV_LORA, hw), lambda b, l: (l, 0, 0)),
                  pl.BlockSpec((1, VT_ROWS, KV_LORA), lambda b, l: (l, 0, 0))],
        out_specs=[pl.BlockSpec((1, 1, PAST_LEN, hw), lambda b, l: (b, l, 0, 0)),
                   pl.BlockSpec((1, 1, VT_ROWS, PAST_LEN), lambda b, l: (b, l, 0, 0))],
        out_shape=[jax.ShapeDtypeStruct((DEC_BATCH, DEPTH, PAST_LEN, hw), BF16),
                   jax.ShapeDtypeStruct((DEC_BATCH, DEPTH, VT_ROWS, PAST_LEN), BF16)],
        compiler_params=_cp(("arbitrary", "arbitrary")),
        name="kv_cache",
    )(cache_p, wk_p, wv_p)


def _attn_kernel(*refs, heads, has_cache):
    if has_cache:
        q_ref, k_ref, vt_ref, kc_ref, vct_ref, o_ref = refs
    else:
        q_ref, k_ref, vt_ref, o_ref = refs
    n_keys = k_ref.shape[0]
    tk = min(TK_ATT, n_keys)
    tiles = ([("cache", 0, PAST_LEN)] if has_cache else []) + [("own", s0, tk) for s0 in range(0, n_keys, tk)]
    units = [(h, t) for t in range(len(tiles)) for h in range(heads)]
    qs = [q_ref[:, h * HEAD_W:(h + 1) * HEAD_W] for h in range(heads)]

    def scores(h, t):
        kind, s0, n = tiles[t]
        sl = slice(h * HEAD_W, (h + 1) * HEAD_W)
        keys = kc_ref[0, 0, :, sl] if kind == "cache" else k_ref[s0:s0 + n, sl]
        return _dot_nt(keys, qs[h])

    def values_t(h, t):
        kind, s0, n = tiles[t]
        sv = slice(h * MLA_V, (h + 1) * MLA_V)
        vt = vct_ref[0, 0, sv, :] if kind == "cache" else vt_ref[sv, s0:s0 + n]
        return jnp.concatenate([vt, jnp.ones((ONES_ROWS, n), BF16)], axis=0)

    ahead = ATT_AHEAD
    pending = {u: scores(*u) for u in units[:ahead]}
    m, acc = [None] * heads, [None] * heads
    for i, (h, t) in enumerate(units):
        if i + ahead < len(units):
            pending[units[i + ahead]] = scores(*units[i + ahead])
        st = pending.pop((h, t))
        mt = jnp.max(st, axis=0, keepdims=True)
        if m[h] is None:
            m[h] = mt
            acc[h] = _dot(values_t(h, t), jnp.exp2(st - mt).astype(BF16))
        else:
            m_new = jnp.maximum(m[h], mt)
            alpha = jnp.exp2(m[h] - m_new)
            acc[h] = alpha * acc[h] + _dot(values_t(h, t), jnp.exp2(st - m_new).astype(BF16))
            m[h] = m_new
    outs = [acc[h][:MLA_V] / acc[h][MLA_V:MLA_V + 1] for h in range(heads)]
    o_ref[...] = jnp.concatenate(outs, axis=0).T.astype(BF16)


def _attention_ctx(qp, kp, vtp):
    hw = MLA_HEADS * HEAD_W
    return pl.pallas_call(
        functools.partial(_attn_kernel, heads=MLA_HEADS, has_cache=False),
        grid=(BATCH,),
        in_specs=[pl.BlockSpec((SEQ, hw), lambda s: (s, 0)),
                  pl.BlockSpec((SEQ, hw), lambda s: (s, 0)),
                  pl.BlockSpec((VT_ROWS, SEQ), lambda s: (0, s))],
        out_specs=pl.BlockSpec((SEQ, MLA_HEADS * MLA_V), lambda s: (s, 0)),
        out_shape=jax.ShapeDtypeStruct((NCTX, MLA_HEADS * MLA_V), BF16),
        compiler_params=_cp(("arbitrary",)),
        name="attn_ctx",
    )(qp, kp, vtp)


def _attention_smp(qp, kp, vtp, kc, vct, layer):
    hpb = 4
    wq = hpb * HEAD_W
    wv = hpb * MLA_V
    nq = DEC_SEQ // TQ_ATT
    qoff = NCTX // TQ_ATT
    koff = NCTX // DEC_SEQ
    return pl.pallas_call(
        functools.partial(_attn_kernel, heads=hpb, has_cache=True),
        grid=(DEC_BATCH, MLA_HEADS // hpb, nq),
        in_specs=[pl.BlockSpec((TQ_ATT, wq), lambda b, g, i: (qoff + b * nq + i, g)),
                  pl.BlockSpec((DEC_SEQ, wq), lambda b, g, i: (koff + b, g)),
                  pl.BlockSpec((wv, DEC_SEQ), lambda b, g, i: (g, koff + b)),
                  pl.BlockSpec((1, 1, PAST_LEN, wq), lambda b, g, i: (b, layer, 0, g)),
                  pl.BlockSpec((1, 1, wv, PAST_LEN), lambda b, g, i: (b, layer, g, 0))],
        out_specs=pl.BlockSpec((TQ_ATT, hpb * MLA_V), lambda b, g, i: (b * nq + i, g)),
        out_shape=jax.ShapeDtypeStruct((NSMP, MLA_HEADS * MLA_V), BF16),
        compiler_params=_cp(("arbitrary", "arbitrary", "arbitrary")),
        name="attn_smp",
    )(qp, kp, vtp, kc, vct)


def _scan_specs(ctx, blk, widths_and_cols, dir_cols=()):
    nseq, seq_len = (BATCH, SEQ) if ctx else (DEC_BATCH, DEC_SEQ)
    nblk = seq_len // blk
    roff = 0 if ctx else NCTX // blk

    def rb(dd):
        return lambda s, j: roff + s * nblk + (j if dd == 0 else nblk - 1 - j)

    specs = []
    for dd in range(2):
        r = rb(dd)
        specs += [pl.BlockSpec((blk, w), lambda s, j, r=r, c=c: (r(s, j), c)) for w, c in widths_and_cols]
        specs += [pl.BlockSpec((blk, w), lambda s, j, r=r, c=c + dd: (r(s, j), c)) for w, c in dir_cols]
    out_maps = [lambda s, j, r=rb(dd): (r(s, j) - roff, 0) for dd in range(2)]
    return nseq, nblk, specs, out_maps


def _gla_kernel(*refs, nblk, has_s0, nch):
    per_dir = [refs[0:4], refs[4:8]]
    wg_ref, bg_ref = refs[8:10]
    if has_s0:
        s0_ref, of_ref, ob_ref, s_scr = refs[10:]
    else:
        of_ref, ob_ref, sfin_ref, s_scr = refs[10:]
    o_refs = [of_ref, ob_ref]
    j = pl.program_id(1)

    @pl.when(j == 0)
    def _():
        if has_s0:
            s_scr[...] = s0_ref[0, 0]
        else:
            s_scr[...] = jnp.zeros_like(s_scr)

    r64 = lax.broadcasted_iota(jnp.int32, (CHUNK, CHUNK), 0)
    c64 = lax.broadcasted_iota(jnp.int32, (CHUNK, CHUNK), 1)
    incl = [c64 <= r64, c64 >= r64]
    tri_b = [jnp.where(m, 1.0, 0.0).astype(BF16) for m in incl]
    heads = [(slice(h * GLA_DK, (h + 1) * GLA_DK), slice(h * GLA_DV, (h + 1) * GLA_DV))
             for h in range(GLA_HEADS)]

    ch = []
    for c in range(nch):
        for dd in range(2):
            ci = c if dd == 0 else nch - 1 - c
            rows = slice(ci * CHUNK, (ci + 1) * CHUNK)
            glr_ref = per_dir[dd][3]
            ch.append(dict(dd=dd, rows=rows, z=_dot(glr_ref[rows, :], wg_ref[0, dd]) + bg_ref[0, dd]))
    for t in ch:
        t["la"] = _split3(_log_sigmoid(t["z"]) * (1.0 / GLA_TAU))
    for t in ch:
        hi, mid, lo = t["la"]
        tb = tri_b[t["dd"]]
        t["bc"] = _dot(tb, hi) + (_dot(tb, mid) + _dot(tb, lo))
    for t in ch:
        q_ref, k_ref, v_ref, _ = per_dir[t["dd"]]
        rows, bc = t["rows"], t["bc"]
        q = q_ref[rows, :].astype(F32) * (GLA_DK ** -0.5)
        k = k_ref[rows, :].astype(F32)
        bl = bc[CHUNK - 1:CHUNK] if t["dd"] == 0 else bc[0:1]
        t["qd"] = (q * jnp.exp(bc)).astype(BF16)
        t["ki"] = (k * jnp.exp(-bc)).astype(BF16)
        t["ke"] = (k * jnp.exp(bl - bc)).astype(BF16)
        t["dec_t"] = jnp.transpose(jnp.broadcast_to(jnp.exp(bl), (LANE, GLA_HEADS * GLA_DK)))
        t["v"] = v_ref[rows, :]
    for t in ch:
        t["a"] = [jnp.where(incl[t["dd"]], _dot_nt(t["qd"][:, sk], t["ki"][:, sk]), 0.0).astype(BF16)
                  for sk, _ in heads]
    for t in ch:
        t["oi"] = [_dot(t["a"][h], t["v"][:, sv]) for h, (_, sv) in enumerate(heads)]
        t["upd"] = [_dot_tn(t["ke"][:, sk], t["v"][:, sv]) for sk, sv in heads]
    states = [[s_scr[dd, h] for h in range(GLA_HEADS)] for dd in range(2)]
    for t in ch:
        st = states[t["dd"]]
        t["s_in"] = [s.astype(BF16) for s in st]
        for h, (sk, _) in enumerate(heads):
            st[h] = t["dec_t"][sk, :] * st[h] + t["upd"][h]
    for t in ch:
        for h, (sk, sv) in enumerate(heads):
            o_refs[t["dd"]][t["rows"], sv] = t["oi"][h] + _dot(t["qd"][:, sk], t["s_in"][h])
    for dd in range(2):
        for h in range(GLA_HEADS):
            s_scr[dd, h] = states[dd][h]

    if not has_s0:
        @pl.when(j == nblk - 1)
        def _():
            sfin_ref[0] = s_scr[...]


def _gla(proj, wg_p, bg_p, s0, layer, ctx):
    blk = GLA_BLK_CTX if ctx else GLA_BLK_SMP
    hk = GLA_HEADS * GLA_DK
    hv = GLA_HEADS * GLA_DV
    nseq, nblk, in_specs, out_maps = _scan_specs(
        ctx, blk, [(hk, PC_GQ // hk), (hk, PC_GK // hk), (hv, PC_GV // hv), (LANE, PC_GLR // LANE)])
    in_specs += [pl.BlockSpec((1, 2, LANE, hk), lambda s, j: (layer, 0, 0, 0)),
                 pl.BlockSpec((1, 2, 1, hk), lambda s, j: (layer, 0, 0, 0))]
    args = [proj] * 8 + [wg_p, bg_p]
    nrows = NCTX if ctx else NSMP
    out_specs = [pl.BlockSpec((blk, hv), m) for m in out_maps]
    out_shape = [jax.ShapeDtypeStruct((nrows, hv), F32)] * 2
    st_blk = (2, GLA_HEADS, GLA_DK, GLA_DV)
    if ctx:
        out_specs.append(pl.BlockSpec((1,) + st_blk, lambda s, j: (s, 0, 0, 0, 0)))
        out_shape.append(jax.ShapeDtypeStruct((BATCH,) + st_blk, F32))
    else:
        in_specs.append(pl.BlockSpec((1, 1) + st_blk, lambda s, j: (s, layer, 0, 0, 0, 0)))
        args.append(s0)
    return pl.pallas_call(
        functools.partial(_gla_kernel, nblk=nblk, has_s0=not ctx, nch=blk // CHUNK),
        grid=(nseq, nblk),
        in_specs=in_specs, out_specs=out_specs, out_shape=out_shape,
        scratch_shapes=[pltpu.VMEM(st_blk, F32)],
        compiler_params=_cp(("arbitrary", "arbitrary")),
        name="gla_ctx" if ctx else "gla_smp",
    )(*args)


GDN_G = 2
GDN_GW = GDN_G * GDN_DK
HALO = 16


def _gdn_pre_kernel(x_ref, xp_ref, xn_ref, cw_ref, o_ref, *, blocks_per_seq_ctx, blocks_per_seq_smp):
    i = pl.program_id(0)
    nctx = NCTX // SEQ_BLK
    bi = jnp.where(i < nctx, i % blocks_per_seq_ctx, (i - nctx) % blocks_per_seq_smp)
    nblk = jnp.where(i < nctx, blocks_per_seq_ctx, blocks_per_seq_smp)
    hk = GDN_HEADS * GDN_DK
    x = x_ref[...].astype(F32)
    rowi = lax.broadcasted_iota(jnp.int32, x.shape, 0)
    prev_row = jnp.where(bi > 0, xp_ref[...].astype(F32)[HALO - 1:HALO], 0.0)
    next_row = jnp.where(bi < nblk - 1, xn_ref[...].astype(F32)[0:1], 0.0)
    x_prev = jnp.where(rowi == 0, prev_row, pltpu.roll(x, 1, 0))
    x_next = jnp.where(rowi == SEQ_BLK - 1, next_row, pltpu.roll(x, SEQ_BLK - 1, 0))
    cw = 0.5 * cw_ref[0]
    y = _silu_of_twice(x_prev * cw[0:1] + x * cw[1:2] + x_next * cw[2:3])

    lane = lax.broadcasted_iota(jnp.int32, (SEQ_BLK, LANE), 1)
    lo_half = lane < GDN_DK

    def l2n(t, scale):
        parts = []
        for c in range(hk // LANE):
            tc = t[:, c * LANE:(c + 1) * LANE]
            sq = tc * tc
            s_lo = jnp.sum(jnp.where(lo_half, sq, 0.0), axis=-1, keepdims=True)
            s_hi = jnp.sum(jnp.where(lo_half, 0.0, sq), axis=-1, keepdims=True)
            parts.append(tc * (lax.rsqrt(jnp.where(lo_half, s_lo, s_hi) + EPS) * scale))
        return jnp.concatenate(parts, axis=1)

    o_ref[:, :hk] = l2n(y[:, :hk], GDN_DK ** -0.5).astype(BF16)
    o_ref[:, hk:2 * hk] = l2n(y[:, hk:2 * hk], 1.0).astype(BF16)
    o_ref[:, 2 * hk:] = y[:, 2 * hk:].astype(BF16)


def _gdn_pre(proj, conv_w, layer):
    cw = 2 * GDN_HEADS * GDN_DK + GDN_HEADS * GDN_DV
    halo = SEQ_BLK // HALO
    last_h = R // HALO - 1
    return pl.pallas_call(
        functools.partial(_gdn_pre_kernel, blocks_per_seq_ctx=SEQ // SEQ_BLK,
                          blocks_per_seq_smp=DEC_SEQ // SEQ_BLK),
        grid=(R // SEQ_BLK,),
        in_specs=[pl.BlockSpec((SEQ_BLK, cw), lambda i: (i, PC_DQKV // cw)),
                  pl.BlockSpec((HALO, cw), lambda i: (jnp.maximum(i * halo - 1, 0), PC_DQKV // cw)),
                  pl.BlockSpec((HALO, cw), lambda i: (jnp.minimum((i + 1) * halo, last_h), PC_DQKV // cw)),
                  pl.BlockSpec((1, 3, cw), lambda i: (layer, 0, 0))],
        out_specs=pl.BlockSpec((SEQ_BLK, cw), lambda i: (i, 0)),
        out_shape=jax.ShapeDtypeStruct((R, cw), BF16),
        compiler_params=_cp(("arbitrary",)),
        name="gdn_pre",
    )(proj, proj, proj, conv_w)


def _gdn_kernel(*refs, nblk, has_s0):
    per_dir = [refs[0:4], refs[4:8]]
    alog_ref, dtb_ref = refs[8:10]
    if has_s0:
        s0_ref, of_ref, ob_ref, s_scr = refs[10:]
    else:
        of_ref, ob_ref, sfin_ref, s_scr = refs[10:]
    o_refs = [of_ref, ob_ref]
    j = pl.program_id(1)
    ngrp = GDN_HEADS // GDN_G

    def head_block(h):
        g, hh = divmod(h, GDN_G)
        return g, slice(hh * GDN_DK, (hh + 1) * GDN_DK), slice(hh * GDN_DV, (hh + 1) * GDN_DV)

    @pl.when(j == 0)
    def _():
        s_scr[...] = jnp.zeros_like(s_scr)
        if has_s0:
            for dd in range(2):
                for h in range(GDN_HEADS):
                    g, rk, rv = head_block(h)
                    s_scr[dd, g, rk, rv] = s0_ref[0, 0, dd, h]

    row = lax.broadcasted_iota(jnp.int32, (CHUNK, GDN_GW), 0)
    col = lax.broadcasted_iota(jnp.int32, (CHUNK, GDN_GW), 1) % CHUNK
    r64 = lax.broadcasted_iota(jnp.int32, (CHUNK, CHUNK), 0)
    c64 = lax.broadcasted_iota(jnp.int32, (CHUNK, CHUNK), 1)
    dir_masks = []
    for dd in range(2):
        lag = (row - col) if dd == 0 else (col - row)
        tri = (c64 <= r64) if dd == 0 else (c64 >= r64)
        dir_masks.append(dict(incl=lag >= 0, strict=lag > 0,
                              tri_b=jnp.where(tri, 1.0, 0.0).astype(BF16)))
    eye = col == row
    eye_f = jnp.where(eye, 1.0, 0.0)
    blk8 = (row // 8) == (col // 8)
    blk16 = (row // 16) == (col // 16)
    blk32 = (row // 32) == (col // 32)
    brow = lax.broadcasted_iota(jnp.int32, (GDN_GW, GDN_GW), 0) // CHUNK
    bcol = lax.broadcasted_iota(jnp.int32, (GDN_GW, GDN_GW), 1) // CHUNK
    same_head = brow == bcol

    def bdiag(t):
        tb = t.astype(BF16)
        return jnp.where(same_head, jnp.concatenate([tb] * GDN_G, axis=0), jnp.zeros((), BF16))

    def bdmm(a, b):
        return _dot(a.astype(BF16), bdiag(b))

    def setup(dd, c):
        q_ref, k_ref, v_ref, ab_ref = per_dir[dd]
        mk = dir_masks[dd]
        ci = c if dd == 0 else NCH - 1 - c
        rows = slice(ci * CHUNK, (ci + 1) * CHUNK)
        ab = ab_ref[rows, :].astype(F32)
        neg_a = -jnp.exp(alog_ref[0, dd])
        gam = _cumsum_rows(mk["tri_b"], neg_a * _softplus(ab + dtb_ref[0, dd]))
        beta_all = _sigmoid(ab)
        out = []
        for g in range(ngrp):
            gl = slice(g * GDN_GW, (g + 1) * GDN_GW)
            gcol = jnp.concatenate(
                [jnp.broadcast_to(gam[:, g * GDN_G + h:g * GDN_G + h + 1], (CHUNK, GDN_DK))
                 for h in range(GDN_G)], axis=1)
            bcol_ = jnp.concatenate(
                [jnp.broadcast_to(beta_all[:, GDN_HEADS + g * GDN_G + h:GDN_HEADS + g * GDN_G + h + 1],
                                  (CHUNK, GDN_DK)) for h in range(GDN_G)], axis=1)
            grow = jnp.sum(jnp.where(eye, gcol, 0.0), axis=0, keepdims=True)
            glast = gcol[CHUNK - 1:CHUNK] if dd == 0 else gcol[0:1]
            decay = jnp.where(mk["incl"], jnp.exp(gcol - grow), 0.0)
            egc = jnp.exp(gcol)
            qg = q_ref[rows, gl].astype(F32)
            kg = k_ref[rows, gl].astype(F32)
            vg = v_ref[rows, gl].astype(F32)
            kq = _dot_nt(jnp.concatenate([k_ref[rows, gl], q_ref[rows, gl]], axis=0),
                         bdiag(k_ref[rows, gl]))
            out.append(dict(dd=dd, g=g, rows=rows, gl=gl,
                            m=jnp.where(mk["strict"], bcol_ * kq[:CHUNK] * decay, 0.0),
                            aqk=(kq[CHUNK:] * decay).astype(BF16),
                            vb=vg * bcol_, kb=kg * (bcol_ * egc), qe=qg * egc,
                            kend=(kg * jnp.exp(glast - gcol)).astype(BF16),
                            eg=jnp.exp(glast)))
        return out

    def stages(ch):
        pairs = list(zip(ch[0::2], ch[1::2]))

        def bdmm_pairs(lhs, rhs, out):
            for ta, tb in pairs:
                a = jnp.concatenate([ta[lhs].astype(BF16), tb[lhs].astype(BF16)], axis=0)
                w = jnp.concatenate([bdiag(ta[rhs]), bdiag(tb[rhs])], axis=1)
                y = _dot(a, w)
                ta[out] = y[:CHUNK, :GDN_GW]
                tb[out] = y[CHUNK:, GDN_GW:]

        def neumann_a():
            for t in ch:
                t["n8"] = jnp.where(blk8, t["m"], 0.0)
            bdmm_pairs("n8", "n8", "n2")

        def neumann_b():
            for t in ch:
                t["p1"] = eye_f - t["n8"]
            bdmm_pairs("n2", "n2", "n4")
            bdmm_pairs("p1", "n2", "p1n2")
            for t in ch:
                t["p1"] = t["p1"] + t["p1n2"]

        def neumann_c():
            bdmm_pairs("p1", "n4", "p1n4")
            for t in ch:
                t["dinv"] = t["p1"] + t["p1n4"]

        def double_a(off):
            def run():
                for t in ch:
                    t["off"] = jnp.where(off, t["m"], 0.0)
                bdmm_pairs("dinv", "off", "dl")
            return run

        def double_b():
            bdmm_pairs("dl", "dinv", "dld")
            for t in ch:
                t["dinv"] = t["dinv"] - t["dld"]

        def solve():
            for t in ch:
                uw = _dot(t["dinv"].astype(BF16), jnp.concatenate([bdiag(t["vb"]), bdiag(t["kb"])], axis=1))
                t["u"] = uw[:, :GDN_GW]
                t["w"] = uw[:, GDN_GW:]

        def fold():
            for t in ch:
                wu = jnp.concatenate([t["w"], t["u"]], axis=1).astype(BF16)
                cb = _dot_tn(t["kend"], wu)
                t["c"] = jnp.where(same_head, cb[:, :GDN_GW], 0.0).astype(BF16)
                t["b"] = jnp.where(same_head, cb[:, GDN_GW:], 0.0)
                ao = _dot(t["aqk"], jnp.concatenate([bdiag(t["w"]), bdiag(t["u"])], axis=1))
                t["qt"] = (t["qe"] - ao[:, :GDN_GW]).astype(BF16)
                t["o"] = ao[:, GDN_GW:]

        out = [neumann_a, neumann_b, neumann_c]
        for inner, outer in ((blk8, blk16), (blk16, blk32), (blk32, None)):
            off = jnp.logical_not(inner) if outer is None else jnp.logical_and(outer, jnp.logical_not(inner))
            out += [double_a(off), double_b]
        return out + [solve, fold]

    states = [[s_scr[dd, g] for g in range(ngrp)] for dd in range(2)]

    def recur(cur):
        ys = [_dot(jnp.concatenate([t["qt"], t["c"]], axis=0), states[t["dd"]][t["g"]].astype(BF16))
              for t in cur]
        for t, y in zip(cur, ys):
            o_refs[t["dd"]][t["rows"], t["gl"]] = y[:CHUNK] + t["o"]
            states[t["dd"]][t["g"]] = t["eg"] * states[t["dd"]][t["g"]] - y[CHUNK:] + t["b"]

    half = NCH // 2
    chains = {}
    for c in range(half):
        for dd in range(2):
            chains[(dd, c)] = setup(dd, c)
    later = [(dd, c) for c in range(half, NCH) for dd in range(2)]
    wave_a = [t for c in range(half) for dd in range(2) for t in chains[(dd, c)]]
    for k, stage in enumerate(stages(wave_a)):
        stage()
        if k % 2 == 0 and later:
            key = later.pop(0)
            chains[key] = setup(*key)
    for key in later:
        chains[key] = setup(*key)
    wave_b = [t for c in range(half, NCH) for dd in range(2) for t in chains[(dd, c)]]
    done = 0
    for k, stage in enumerate(stages(wave_b)):
        stage()
        if k % 3 == 1 and done < half:
            recur(chains[(0, done)] + chains[(1, done)])
            done += 1
    for c in range(done, NCH):
        recur(chains[(0, c)] + chains[(1, c)])
    for dd in range(2):
        for g in range(ngrp):
            s_scr[dd, g] = states[dd][g]

    if not has_s0:
        @pl.when(j == nblk - 1)
        def _():
            for dd in range(2):
                for h in range(GDN_HEADS):
                    g, rk, rv = head_block(h)
                    sfin_ref[0, dd, h] = s_scr[dd, g, rk, rv]


def _gdn(qkv, proj, alog_p, dtb_p, s0, layer, ctx):
    nseq, nblk = (BATCH, SEQ // SEQ_BLK) if ctx else (DEC_BATCH, DEC_SEQ // SEQ_BLK)
    roff = 0 if ctx else NCTX // SEQ_BLK
    hk = GDN_HEADS * GDN_DK
    hv = GDN_HEADS * GDN_DV

    def rb(dd):
        return lambda s, j: roff + s * nblk + (j if dd == 0 else nblk - 1 - j)

    in_specs, args = [], []
    for dd in range(2):
        r = rb(dd)
        in_specs += [pl.BlockSpec((SEQ_BLK, hk), lambda s, j, r=r: (r(s, j), 0)),
                     pl.BlockSpec((SEQ_BLK, hk), lambda s, j, r=r: (r(s, j), 1)),
                     pl.BlockSpec((SEQ_BLK, hv), lambda s, j, r=r: (r(s, j), 2 * hk // hv)),
                     pl.BlockSpec((SEQ_BLK, LANE), lambda s, j, r=r, dd=dd: (r(s, j), PC_DAB // LANE + dd))]
        args += [qkv, qkv, qkv, proj]
    in_specs += [pl.BlockSpec((1, 2, 1, LANE), lambda s, j: (layer, 0, 0, 0))] * 2
    args += [alog_p, dtb_p]
    nrows = NCTX if ctx else NSMP
    out_specs = [pl.BlockSpec((SEQ_BLK, hv), lambda s, j, r=rb(dd): (r(s, j) - roff, 0)) for dd in range(2)]
    out_shape = [jax.ShapeDtypeStruct((nrows, hv), F32)] * 2
    st_blk = (2, GDN_HEADS, GDN_DK, GDN_DV)
    if ctx:
        out_specs.append(pl.BlockSpec((1,) + st_blk, lambda s, j: (s, 0, 0, 0, 0)))
        out_shape.append(jax.ShapeDtypeStruct((BATCH,) + st_blk, F32))
    else:
        in_specs.append(pl.BlockSpec((1, 1) + st_blk, lambda s, j: (s, layer, 0, 0, 0, 0)))
        args.append(s0)
    return pl.pallas_call(
        functools.partial(_gdn_kernel, nblk=nblk, has_s0=not ctx),
        grid=(nseq, nblk),
        in_specs=in_specs, out_specs=out_specs, out_shape=out_shape,
        scratch_shapes=[pltpu.VMEM((2, GDN_HEADS // GDN_G, GDN_GW, GDN_GW), F32)],
        compiler_params=_cp(("arbitrary", "arbitrary")),
        name="gdn_ctx" if ctx else "gdn_smp",
    )(*args)


def _group_rms(x, width):
    parts = []
    lane = lax.broadcasted_iota(jnp.int32, (x.shape[0], LANE), 1)
    lo_half = lane < 64
    for c in range(x.shape[1] // LANE):
        xc = x[:, c * LANE:(c + 1) * LANE]
        sq = xc * xc
        if width == LANE:
            ms = jnp.mean(sq, axis=-1, keepdims=True)
        else:
            s_lo = jnp.sum(jnp.where(lo_half, sq, 0.0), axis=-1, keepdims=True)
            s_hi = jnp.sum(jnp.where(lo_half, 0.0, sq), axis=-1, keepdims=True)
            ms = jnp.where(lo_half, s_lo, s_hi) * (1.0 / width)
        parts.append(xc * lax.rsqrt(ms + EPS))
    return jnp.concatenate(parts, axis=1)


def _merge_kernel(ymc_ref, yms_ref, ogcf_ref, ogcb_ref, ogsf_ref, ogsb_ref, gr_ref,
                  odcf_ref, odcb_ref, odsf_ref, odsb_ref,
                  dz_ref, gt_ref, *rest, split_h):
    if split_h:
        hc_ref, hs_ref, *rest = rest
        h_rows = lambda: _family_rows((hc_ref, hs_ref), TM_MERGE)
    else:
        h_ref, *rest = rest
        h_rows = lambda: h_ref[...]
    ga_ref, gn_ref, dn_ref, wb_ref, wo_ref, o_ref, ym_scr, og_scr, od_scr = rest
    is_ctx = pl.program_id(0) < NCTX // TM_MERGE

    @pl.when(is_ctx)
    def _():
        ym_scr[...] = ymc_ref[...]
        og_scr[...] = ogcf_ref[...] + ogcb_ref[...]
        od_scr[...] = odcf_ref[...] + odcb_ref[...]

    @pl.when(jnp.logical_not(is_ctx))
    def _():
        ym_scr[...] = yms_ref[...]
        og_scr[...] = ogsf_ref[...] + ogsb_ref[...]
        od_scr[...] = odsf_ref[...] + odsb_ref[...]

    y_gla = _group_rms(og_scr[...], GLA_DV) * gn_ref[0] * _silu_of_twice(gr_ref[...].astype(F32))
    y_gdn = _group_rms(od_scr[...], GDN_DV) * dn_ref[0] * _silu_of_twice(dz_ref[...].astype(F32))
    t = jnp.tanh(gt_ref[...].astype(F32))
    d = D_MODEL
    acc = None
    for n, y_n in enumerate((ym_scr[...], y_gla.astype(BF16), y_gdn.astype(BF16))):
        p = _dot(y_n, wb_ref[0, n])
        gated = p + p * t[:, n * d:(n + 1) * d]
        acc = gated if acc is None else acc + gated
    y = _dot(acc.astype(BF16), wo_ref[0])
    o_ref[...] = h_rows() + ga_ref[0] * y


def _merge(y_mla, o_gla, o_gdn, proj, h, mod3, gla_norm_p, gdn_norm_p, w_branch_b, w_out_b, layer):
    tm = TM_MERGE
    bw = 512
    nc = NCTX // tm
    split_h = isinstance(h, tuple)
    h_specs = _family_specs(tm, D_MODEL) if split_h else [pl.BlockSpec((tm, D_MODEL), lambda i: (i, 0))]
    h_args = list(h) if split_h else [h]

    def cmap(i):
        return jnp.minimum(i, nc - 1)

    def smap(i):
        return jnp.maximum(i - nc, 0)

    return pl.pallas_call(
        functools.partial(_merge_kernel, split_h=split_h),
        grid=(R // tm,),
        in_specs=[pl.BlockSpec((tm, bw), lambda i: (cmap(i), 0)),
                  pl.BlockSpec((tm, bw), lambda i: (smap(i), 0)),
                  pl.BlockSpec((tm, bw), lambda i: (cmap(i), 0)),
                  pl.BlockSpec((tm, bw), lambda i: (cmap(i), 0)),
                  pl.BlockSpec((tm, bw), lambda i: (smap(i), 0)),
                  pl.BlockSpec((tm, bw), lambda i: (smap(i), 0)),
                  pl.BlockSpec((tm, bw), lambda i: (i, PC_GR // bw)),
                  pl.BlockSpec((tm, bw), lambda i: (cmap(i), 0)),
                  pl.BlockSpec((tm, bw), lambda i: (cmap(i), 0)),
                  pl.BlockSpec((tm, bw), lambda i: (smap(i), 0)),
                  pl.BlockSpec((tm, bw), lambda i: (smap(i), 0)),
                  pl.BlockSpec((tm, bw), lambda i: (i, PC_DZ // bw)),
                  pl.BlockSpec((tm, 3 * D_MODEL), lambda i: (i, PC_GATES // (3 * D_MODEL)))] + h_specs + [
                  _mod_spec(layer, 2, tm),
                  pl.BlockSpec((1, 1, bw), lambda i: (layer, 0, 0)),
                  pl.BlockSpec((1, 1, bw), lambda i: (layer, 0, 0)),
                  pl.BlockSpec((1, 3, bw, D_MODEL), lambda i: (layer, 0, 0, 0), pipeline_mode=pl.Buffered(1)),
                  pl.BlockSpec((1, D_MODEL, D_MODEL), lambda i: (layer, 0, 0), pipeline_mode=pl.Buffered(1))],
        out_specs=pl.BlockSpec((tm, D_MODEL), lambda i: (i, 0)),
        out_shape=jax.ShapeDtypeStruct((R, D_MODEL), F32),
        scratch_shapes=[pltpu.VMEM((tm, bw), BF16), pltpu.VMEM((tm, bw), F32), pltpu.VMEM((tm, bw), F32)],
        compiler_params=_cp(("arbitrary",)),
        name="merge",
    )(y_mla[0], y_mla[1], o_gla[0][0], o_gla[0][1], o_gla[1][0], o_gla[1][1], proj,
      o_gdn[0][0], o_gdn[0][1], o_gdn[1][0], o_gdn[1][1],
      proj, proj, *h_args, mod3, gla_norm_p, gdn_norm_p, w_branch_b, w_out_b)


def _ffn_kernel(h_ref, g_ref, sc_ref, sh_ref, gf_ref, wi_ref, wo_ref, *rest, final):
    if final:
        fn_ref, *o_refs = rest
    else:
        gn_ref, scn_ref, shn_ref, *o_refs = rest
    h = h_ref[...]
    xf = (_rms(h, g_ref[0]) * (1.0 + sc_ref[0]) + sh_ref[0]).astype(BF16)

    def gate_up(c):
        f0 = c * TF_FFN
        return (_dot(xf, wi_ref[0, :, f0:f0 + TF_FFN]),
                _dot(xf, wi_ref[0, :, D_FF + f0:D_FF + f0 + TF_FFN]))

    n_chunks = D_FF // TF_FFN
    nxt = gate_up(0)
    acc = None
    for c in range(n_chunks):
        g, u = nxt
        if c + 1 < n_chunks:
            nxt = gate_up(c + 1)
        part = _dot((_silu(g) * u).astype(BF16), wo_ref[0, c * TF_FFN:(c + 1) * TF_FFN, :])
        acc = part if acc is None else acc + part
    out = h + gf_ref[0] * acc
    if not final:
        o_refs[0][...] = out
        o_refs[1][...] = (_rms(out, gn_ref[0]) * (1.0 + scn_ref[0]) + shn_ref[0]).astype(BF16)
    else:
        out = _rms(out, fn_ref[...])
        is_ctx = pl.program_id(0) < NCTX // TM_FFN

        @pl.when(is_ctx)
        def _():
            o_refs[0][...] = out

        @pl.when(jnp.logical_not(is_ctx))
        def _():
            o_refs[1][...] = out


def _ffn(h, mod3, norm_ffn, wi_b, wo_b, tail, layer, final):
    tm = TM_FFN
    if final:
        nc = NCTX // tm
        out_specs = [pl.BlockSpec((tm, D_MODEL), lambda i: (jnp.minimum(i, nc - 1), 0)),
                     pl.BlockSpec((tm, D_MODEL), lambda i: (jnp.maximum(i - nc, 0), 0))]
        out_shape = [jax.ShapeDtypeStruct((NCTX, D_MODEL), F32), jax.ShapeDtypeStruct((NSMP, D_MODEL), F32)]
        tail_specs = [pl.BlockSpec((1, D_MODEL), lambda i: (0, 0))]
        tail_args = [tail.reshape(1, D_MODEL)]
    else:
        out_specs = [pl.BlockSpec((tm, D_MODEL), lambda i: (i, 0))] * 2
        out_shape = [jax.ShapeDtypeStruct((R, D_MODEL), F32), jax.ShapeDtypeStruct((R, D_MODEL), BF16)]
        tail_specs = [pl.BlockSpec((1, 1, D_MODEL), lambda i: (layer + 1, 0, 0)),
                      _mod_spec(layer + 1, 1, tm), _mod_spec(layer + 1, 0, tm)]
        tail_args = [tail.reshape(DEPTH, 1, D_MODEL), mod3, mod3]
    return pl.pallas_call(
        functools.partial(_ffn_kernel, final=final),
        grid=(R // tm,),
        in_specs=[pl.BlockSpec((tm, D_MODEL), lambda i: (i, 0)),
                  pl.BlockSpec((1, 1, D_MODEL), lambda i: (layer, 0, 0)),
                  _mod_spec(layer, 4, tm),
                  _mod_spec(layer, 3, tm),
                  _mod_spec(layer, 5, tm),
                  pl.BlockSpec((1, D_MODEL, 2 * D_FF), lambda i: (layer, 0, 0), pipeline_mode=pl.Buffered(1)),
                  pl.BlockSpec((1, D_FF, D_MODEL), lambda i: (layer, 0, 0), pipeline_mode=pl.Buffered(1))]
        + tail_specs,
        out_specs=out_specs, out_shape=out_shape,
        compiler_params=_cp(("arbitrary",)),
        name="ffn_final" if final else "ffn",
    )(h, norm_ffn.reshape(DEPTH, 1, D_MODEL), mod3, mod3, mod3, wi_b, wo_b, *tail_args)


def _rope_partner(w, axis=-1):
    h = AXIS_DIM // 2
    axis = axis % w.ndim
    parts = []
    for a in range(2):
        x1 = lax.slice_in_dim(w, a * AXIS_DIM, a * AXIS_DIM + h, axis=axis)
        x2 = lax.slice_in_dim(w, a * AXIS_DIM + h, (a + 1) * AXIS_DIM, axis=axis)
        parts += [-x2, x1]
    return jnp.concatenate(parts, axis=axis)


def _pack_w_in_kernel(w_ref, o_ref):
    o_ref[0] = _pack_channels(w_ref[0]).astype(BF16)


def _pack_w_in(w_in):
    tc = 256
    return pl.pallas_call(
        _pack_w_in_kernel,
        grid=(DEPTH, D_MODEL // tc),
        in_specs=[pl.BlockSpec((1, w_in.shape[-1], tc), lambda l, i: (l, 0, i))],
        out_specs=pl.BlockSpec((1, PROJ_W, tc), lambda l, i: (l, 0, i)),
        out_shape=jax.ShapeDtypeStruct((DEPTH, PROJ_W, D_MODEL), BF16),
        compiler_params=_cp(("arbitrary", "arbitrary")),
        name="pack_w_in",
    )(jnp.swapaxes(w_in, 1, 2))


def _pack_channels(wt):
    c = [0]
    for s in (Q_LORA, KV_LORA + ROPE_DIM, 256, 256, 512, 512, 2 * GLA_RANK, 1536, 512,
              2 * GDN_HEADS, 2 * GDN_HEADS, 3 * D_MODEL):
        c.append(c[-1] + s)
    mq, mkv, gq, gk, gv, gr, glr, dqkv, dz, da, db, gates = [wt[c[i]:c[i + 1]] for i in range(12)]

    def zeros(n):
        return jnp.zeros((n,) + wt.shape[1:], wt.dtype)

    kr = mkv[KV_LORA:]
    mla = [mq, mkv[:KV_LORA], kr, _rope_partner(kr, axis=0), zeros(64)]
    dab = []
    for d in range(2):
        dab += [da[d * GDN_HEADS:(d + 1) * GDN_HEADS], db[d * GDN_HEADS:(d + 1) * GDN_HEADS],
                zeros(LANE - 2 * GDN_HEADS)]
    return jnp.concatenate([0.5 * gates, dqkv, 0.5 * dz, gv, 0.5 * gr] + mla
                           + [gq, gk, glr, zeros(LANE - 2 * GLA_RANK)]
                           + dab + [zeros(PROJ_W - PC_DAB - 2 * LANE)], axis=0)


def _pack_mla_weights(w_uq, w_ukv):
    l = w_uq.shape[0]
    qh = w_uq.reshape(l, Q_LORA, MLA_HEADS, MLA_NOPE + ROPE_DIM)
    zq = jnp.zeros((l, Q_LORA, MLA_HEADS, HEAD_W - MLA_NOPE - ROPE_DIM), w_uq.dtype)
    zn = jnp.zeros((l, Q_LORA, MLA_HEADS, MLA_NOPE), w_uq.dtype)
    wq = jnp.concatenate([qh, zq], axis=-1).reshape(l, Q_LORA, MLA_HEADS * HEAD_W)
    wq_sw = jnp.concatenate([zn, _rope_partner(qh[..., MLA_NOPE:]), zq], axis=-1).reshape(l, Q_LORA, MLA_HEADS * HEAD_W)
    wq_p = jnp.concatenate([wq, wq_sw], axis=-1).astype(BF16)

    kvh = w_ukv.reshape(l, KV_LORA, MLA_HEADS, MLA_NOPE + MLA_V)
    zk = jnp.zeros((l, KV_LORA, MLA_HEADS, HEAD_W - MLA_NOPE), w_ukv.dtype)
    wk_top = jnp.concatenate([kvh[..., :MLA_NOPE], zk], axis=-1).reshape(l, KV_LORA, MLA_HEADS * HEAD_W)
    place = np.zeros((KV_LORA, MLA_HEADS, HEAD_W), np.float32)
    idx = np.arange(ROPE_DIM)
    for rep in range(2):
        place[rep * ROPE_DIM + idx, :, MLA_NOPE + idx] = 1.0
    wk_bot = jnp.broadcast_to(jnp.asarray(place.reshape(1, KV_LORA, MLA_HEADS * HEAD_W)),
                              (l, KV_LORA, MLA_HEADS * HEAD_W))
    wk_p = jnp.concatenate([wk_top, wk_bot], axis=1).astype(BF16)
    wv_p = jnp.transpose(kvh[..., MLA_NOPE:], (0, 2, 3, 1)).reshape(l, VT_ROWS, KV_LORA).astype(BF16)
    return wq_p, wk_p, wv_p


def _rope_tables(tm):
    f32 = np.float32
    rows = DEC_SEQ // GRID_W
    row = np.repeat(np.arange(rows, dtype=np.float64), GRID_W)
    col = np.tile(np.arange(GRID_W, dtype=np.float64), rows)
    inv = ROPE_THETA ** (-np.arange(0, AXIS_DIM, 2, dtype=np.float64) / AXIS_DIM)
    ang_r, ang_c = row[:, None] * inv, col[:, None] * inv
    cos32 = np.concatenate([np.cos(ang_r)] * 2 + [np.cos(ang_c)] * 2, axis=-1)
    sin32 = np.concatenate([np.sin(ang_r)] * 2 + [np.sin(ang_c)] * 2, axis=-1)
    cos32 = np.concatenate([np.ones((tm, ROPE_DIM)), cos32], axis=0)
    sin32 = np.concatenate([np.zeros((tm, ROPE_DIM)), sin32], axis=0)
    n = cos32.shape[0]
    scale = (MLA_NOPE + ROPE_DIM) ** -0.5 * math.log2(math.e)
    pad = HEAD_W - MLA_NOPE - ROPE_DIM
    cq = np.concatenate([np.ones((n, MLA_NOPE)), cos32, np.zeros((n, pad))], axis=-1) * scale
    sq = np.concatenate([np.zeros((n, MLA_NOPE)), sin32, np.zeros((n, pad))], axis=-1) * scale
    ck = np.concatenate([cos32, sin32, np.zeros((n, HEAD_W - 2 * ROPE_DIM))], axis=-1)
    return jnp.asarray(np.concatenate([cq, sq, ck], axis=-1).astype(f32))


def _pad_lanes(x, n):
    return jnp.pad(x, [(0, 0)] * (x.ndim - 1) + [(0, n - x.shape[-1])])


def kernel(x_prompt, x_sample, cache_mla, state_gla, state_gdn, c, c_ctx, w_mod, b_mod, norm_mix, w_in,
           mla_q_norm, mla_w_uq, mla_kv_norm, mla_w_ukv, gla_w_gate, gla_b_gate, gla_norm, gdn_conv,
           gdn_a_log, gdn_dt_bias, gdn_norm, w_branch, w_out, norm_ffn, ffn_w_in, ffn_w_out, final_norm):
    w_in_p = _pack_w_in(w_in)
    wq_p, wk_p, wv_p = _pack_mla_weights(mla_w_uq, mla_w_ukv)
    tab = _rope_tables(TM_PREP)
    wg_p = jnp.zeros((DEPTH, 2, LANE, GLA_HEADS * GLA_DK), F32)
    for d in range(2):
        wg_p = wg_p.at[:, d, d * GLA_RANK:(d + 1) * GLA_RANK, :].set(gla_w_gate[:, d])
    wg_p = wg_p.astype(BF16)
    bg_p = gla_b_gate.reshape(DEPTH, 2, 1, GLA_HEADS * GLA_DK)
    alog_p = _pad_lanes(gdn_a_log, LANE).reshape(DEPTH, 2, 1, LANE)
    dtb_p = _pad_lanes(gdn_dt_bias, LANE).reshape(DEPTH, 2, 1, LANE)
    gla_norm_p = jnp.tile(gla_norm, (1, GLA_HEADS)).reshape(DEPTH, 1, GLA_HEADS * GLA_DV)
    gdn_norm_p = jnp.tile(gdn_norm, (1, GDN_HEADS)).reshape(DEPTH, 1, GDN_HEADS * GDN_DV)
    w_branch_b = (0.5 * w_branch).astype(BF16)
    w_out_b = w_out.astype(BF16)
    wi_b = ffn_w_in.astype(BF16)
    wo_b = ffn_w_out.astype(BF16)
    cache_p = _pad_lanes(cache_mla, 2 * KV_LORA)
    cond8 = jnp.concatenate([c_ctx[None, :], c, jnp.zeros((MOD_ROWS - 1 - DEC_BATCH, D_MODEL), F32)], axis=0)

    mod = _modulation(cond8, w_mod, b_mod)
    mod3 = mod.reshape(DEPTH * MOD_ROWS * 6, 1, D_MODEL)
    kc, vc = _kv_cache(cache_p, wk_p, wv_p)

    h = (x_prompt.reshape(NCTX, D_MODEL), x_sample.reshape(NSMP, D_MODEL))
    kv_list, gla_list, gdn_list = [], [], []
    for l in range(DEPTH):
        proj = _inproj(h if l == 0 else xn, mod3, norm_mix, w_in_p, l)
        qp, kp, vtp, own = _mla_prep(proj, tab, mla_q_norm, mla_kv_norm, wq_p, wk_p, wv_p, l)
        y_mla = (_attention_ctx(qp, kp, vtp), _attention_smp(qp, kp, vtp, kc, vc, l))
        ogf_c, ogb_c, sg = _gla(proj, wg_p, bg_p, None, l, True)
        ogf_s, ogb_s = _gla(proj, wg_p, bg_p, state_gla, l, False)
        qkv = _gdn_pre(proj, gdn_conv, l)
        odf_c, odb_c, sd = _gdn(qkv, proj, alog_p, dtb_p, None, l, True)
        odf_s, odb_s = _gdn(qkv, proj, alog_p, dtb_p, state_gdn, l, False)
        h = _merge(y_mla, ((ogf_c, ogb_c), (ogf_s, ogb_s)), ((odf_c, odb_c), (odf_s, odb_s)), proj, h, mod3,
                   gla_norm_p, gdn_norm_p,
                   w_branch_b, w_out_b, l)
        if l == DEPTH - 1:
            h = _ffn(h, mod3, norm_ffn, wi_b, wo_b, final_norm, l, True)
        else:
            h, xn = _ffn(h, mod3, norm_ffn, wi_b, wo_b, norm_mix, l, False)
        kv_list.append(own[:NCTX, :KV_LORA + ROPE_DIM].reshape(BATCH, SEQ, KV_LORA + ROPE_DIM))
        gla_list.append(sg)
        gdn_list.append(sd)

    y_prompt = h[0].reshape(BATCH, SEQ, D_MODEL)
    y_sample = h[1].reshape(DEC_BATCH, DEC_SEQ, D_MODEL)
    return (y_prompt, y_sample, jnp.stack(kv_list, axis=1), jnp.stack(gla_list, axis=1),
            jnp.stack(gdn_list, axis=1))
```

```python
import functools
import math

import numpy as np
import jax
import jax.numpy as jnp
from jax import lax
from jax.experimental import pallas as pl
from jax.experimental.pallas import tpu as pltpu

F32 = jnp.float32
BF16 = jnp.bfloat16

D_MODEL = 1024
BATCH = 16
SEQ = 256
DEPTH = 2
DEC_BATCH = 2
DEC_SEQ = 4096
PAST_LEN = 256
GRID_W = 64
CHUNK = 64
EPS = 1e-6
MLA_HEADS = 8
MLA_NOPE = 64
ROPE_DIM = 32
AXIS_DIM = ROPE_DIM // 2
MLA_V = 64
Q_LORA = 256
KV_LORA = 128
ROPE_THETA = 10000.0
GLA_HEADS = 4
GLA_DK = 64
GLA_DV = 128
GLA_RANK = 16
GLA_TAU = 16.0
GDN_HEADS = 8
GDN_DK = 64
GDN_DV = 64
D_FF = ((8 * D_MODEL + 3 * 256 - 1) // (3 * 256)) * 256
MOD_W = 6 * D_MODEL

NCTX = BATCH * SEQ
NSMP = DEC_BATCH * DEC_SEQ
R = NCTX + NSMP
LANE = 128
HEAD_W = 128
VT_ROWS = MLA_HEADS * MLA_V
SEQ_BLK = 256
NCH = SEQ_BLK // CHUNK
GLA_BLK_CTX = 256
GLA_BLK_SMP = 512
MOD_ROWS = 8

PC_GATES = 0
PC_DQKV = 3072
PC_DZ = 4608
PC_GV = 5120
PC_GR = 5632
PC_MLA = 6144
PC_GQ = 6656
PC_GK = 6912
PC_GLR = 7168
PC_DAB = 7296
PROJ_W = 7680

TM_IN = 1024
TN_IN = 2560
TM_PREP = 1024
GDN_PRE_BLK = 1024
TM_MERGE = 512
TM_FFN = 512
TF_FFN = 256
TQ_ATT = 256
TK_ATT = 512
ONES_ROWS = 16
ATT_AHEAD = 8
VMEM_LIMIT = 48 * 1024 * 1024


def _cp(sem):
    return pltpu.CompilerParams(dimension_semantics=sem, vmem_limit_bytes=VMEM_LIMIT)


def _dot(a, b):
    return jnp.dot(a, b, preferred_element_type=F32)


def _dot_nt(a, b):
    return lax.dot_general(a, b, (((1,), (1,)), ((), ())), preferred_element_type=F32)


def _dot_tn(a, b):
    return lax.dot_general(a, b, (((0,), (0,)), ((), ())), preferred_element_type=F32)


def _split2(x):
    hi = x.astype(BF16)
    lo = (x - hi.astype(F32)).astype(BF16)
    return hi, lo


def _split3(x):
    hi = x.astype(BF16)
    r1 = x - hi.astype(F32)
    mid = r1.astype(BF16)
    lo = (r1 - mid.astype(F32)).astype(BF16)
    return hi, mid, lo


def _cumsum_rows(tri_b, x):
    hi, mid, lo = _split3(x)
    return _dot(tri_b, hi) + (_dot(tri_b, mid) + _dot(tri_b, lo))


def _sigmoid(x):
    return 0.5 * jnp.tanh(0.5 * x) + 0.5


def _silu(x):
    return x * _sigmoid(x)


def _silu_of_twice(hx):
    return hx + hx * jnp.tanh(hx)


def _softplus(x):
    return jnp.maximum(x, 0.0) + jnp.log(1.0 + jnp.exp(-jnp.abs(x)))


def _log_sigmoid(x):
    return jnp.minimum(x, 0.0) - jnp.log(1.0 + jnp.exp(-jnp.abs(x)))


def _rms(x, g):
    return x * lax.rsqrt(jnp.mean(x * x, axis=-1, keepdims=True) + EPS) * g


def _mod_row(row_start):
    return jnp.where(row_start < NCTX, 0, 1 + (row_start - NCTX) // DEC_SEQ)


def _mod_spec(layer, which, tm, axis=0):
    def imap(*idx):
        return ((layer * MOD_ROWS + _mod_row(idx[axis] * tm)) * 6 + which, 0, 0)
    return pl.BlockSpec((1, 1, D_MODEL), imap)


def _mod_kernel(c_ref, w_ref, b_ref, o_ref):
    x = _silu(c_ref[...])
    xh, xl = _split2(x)
    wh, wl = _split2(w_ref[0])
    o_ref[0] = _dot(xh, wh) + (_dot(xl, wh) + _dot(xh, wl)) + b_ref[0]


def _modulation(cond8, w_mod, b_mod):
    tn = 1536
    return pl.pallas_call(
        _mod_kernel,
        grid=(DEPTH, MOD_W // tn),
        in_specs=[pl.BlockSpec((MOD_ROWS, D_MODEL), lambda l, j: (0, 0)),
                  pl.BlockSpec((1, D_MODEL, tn), lambda l, j: (l, 0, j)),
                  pl.BlockSpec((1, 1, tn), lambda l, j: (l, 0, j))],
        out_specs=pl.BlockSpec((1, MOD_ROWS, tn), lambda l, j: (l, 0, j)),
        out_shape=jax.ShapeDtypeStruct((DEPTH, MOD_ROWS, MOD_W), F32),
        compiler_params=_cp(("arbitrary", "arbitrary")),
        name="modulation",
    )(cond8, w_mod, b_mod.reshape(DEPTH, 1, MOD_W))


def _family_specs(tm, width, axis=0):
    nc = NCTX // tm
    return [pl.BlockSpec((tm, width), lambda *idx: (jnp.minimum(idx[axis], nc - 1), 0)),
            pl.BlockSpec((tm, width), lambda *idx: (jnp.maximum(idx[axis] - nc, 0), 0))]


def _family_rows(refs, tm, axis=0):
    is_ctx = pl.program_id(axis) < NCTX // tm
    return jnp.where(is_ctx, refs[0][...], refs[1][...])


def _inproj_kernel(*refs, normed):
    if normed:
        xn_ref, w_ref, o_ref = refs
        xn = xn_ref[...]
    else:
        hc_ref, hs_ref, g_ref, sc_ref, sh_ref, w_ref, o_ref = refs
        y = _rms(_family_rows((hc_ref, hs_ref), TM_IN, axis=1), g_ref[0])
        xn = (y * (1.0 + sc_ref[0]) + sh_ref[0]).astype(BF16)
    o_ref[...] = _dot_nt(xn, w_ref[0]).astype(BF16)


def _inproj(h, mod3, norm_mix, w_in_p, layer):
    normed = not isinstance(h, tuple)
    if normed:
        in_specs = [pl.BlockSpec((TM_IN, D_MODEL), lambda j, i: (i, 0))]
        args = [h]
    else:
        in_specs = _family_specs(TM_IN, D_MODEL, axis=1) + [
            pl.BlockSpec((1, 1, D_MODEL), lambda j, i: (layer, 0, 0)),
            _mod_spec(layer, 1, TM_IN, axis=1),
            _mod_spec(layer, 0, TM_IN, axis=1)]
        args = list(h) + [norm_mix.reshape(DEPTH, 1, D_MODEL), mod3, mod3]
    return pl.pallas_call(
        functools.partial(_inproj_kernel, normed=normed),
        grid=(PROJ_W // TN_IN, R // TM_IN),
        in_specs=in_specs + [pl.BlockSpec((1, TN_IN, D_MODEL), lambda j, i: (layer, j, 0))],
        out_specs=pl.BlockSpec((TM_IN, TN_IN), lambda j, i: (i, j)),
        out_shape=jax.ShapeDtypeStruct((R, PROJ_W), BF16),
        compiler_params=_cp(("arbitrary", "arbitrary")),
        name="inproj",
    )(*args, w_in_p)


def _mla_prep_kernel(pm_ref, tab_ref, qn_ref, kvn_ref, wq_ref, wk_ref, wv_ref,
                     q_ref, k_ref, v_ref, own_ref):
    pm = pm_ref[...].astype(F32)
    tab = tab_ref[...]
    qn = _rms(pm[:, :Q_LORA], qn_ref[0]).astype(BF16)
    q2 = _dot(qn, wq_ref[0])
    hw = MLA_HEADS * HEAD_W
    cq = jnp.tile(tab[:, :HEAD_W], (1, MLA_HEADS))
    sq = jnp.tile(tab[:, HEAD_W:2 * HEAD_W], (1, MLA_HEADS))
    q_ref[...] = (q2[:, :hw] * cq + q2[:, hw:] * sq).astype(BF16)
    ckv = _rms(pm[:, Q_LORA:Q_LORA + KV_LORA], kvn_ref[0])
    kr = pm[:, Q_LORA + KV_LORA:] * tab[:, 2 * HEAD_W:]
    lhs = jnp.concatenate([ckv, kr], axis=1)
    own_ref[...] = lhs
    lb = lhs.astype(BF16)
    k_ref[...] = _dot(lb, wk_ref[0]).astype(BF16)
    v_ref[...] = _dot_nt(wv_ref[0], lb[:, :KV_LORA]).astype(BF16)


def _mla_prep(proj, tab, q_norm, kv_norm, wq_p, wk_p, wv_p, layer):
    tm = TM_PREP
    hw = MLA_HEADS * HEAD_W

    def tab_map(i):
        r0 = i * tm
        return (jnp.where(r0 < NCTX, 0, 1 + ((r0 - NCTX) % DEC_SEQ) // tm), 0)

    return pl.pallas_call(
        _mla_prep_kernel,
        grid=(R // tm,),
        in_specs=[pl.BlockSpec((tm, 512), lambda i: (i, PC_MLA // 512)),
                  pl.BlockSpec((tm, 3 * HEAD_W), tab_map),
                  pl.BlockSpec((1, 1, Q_LORA), lambda i: (layer, 0, 0)),
                  pl.BlockSpec((1, 1, KV_LORA), lambda i: (layer, 0, 0)),
                  pl.BlockSpec((1, Q_LORA, 2 * hw), lambda i: (layer, 0, 0)),
                  pl.BlockSpec((1, 2 * KV_LORA, hw), lambda i: (layer, 0, 0)),
                  pl.BlockSpec((1, VT_ROWS, KV_LORA), lambda i: (layer, 0, 0))],
        out_specs=[pl.BlockSpec((tm, hw), lambda i: (i, 0)),
                   pl.BlockSpec((tm, hw), lambda i: (i, 0)),
                   pl.BlockSpec((VT_ROWS, tm), lambda i: (0, i)),
                   pl.BlockSpec((tm, 2 * KV_LORA), lambda i: (i, 0))],
        out_shape=[jax.ShapeDtypeStruct((R, hw), BF16),
                   jax.ShapeDtypeStruct((R, hw), BF16),
                   jax.ShapeDtypeStruct((VT_ROWS, R), BF16),
                   jax.ShapeDtypeStruct((R, 2 * KV_LORA), F32)],
        compiler_params=_cp(("arbitrary",)),
        name="mla_prep",
    )(proj, tab, q_norm.reshape(DEPTH, 1, Q_LORA), kv_norm.reshape(DEPTH, 1, KV_LORA),
      wq_p, wk_p, wv_p)


def _kv_cache_kernel(c_ref, wk_ref, wv_ref, k_ref, v_ref):
    lb = c_ref[0, 0].astype(BF16)
    k_ref[0, 0] = _dot(lb, wk_ref[0]).astype(BF16)
    v_ref[0, 0] = _dot_nt(wv_ref[0], lb[:, :KV_LORA]).astype(BF16)


def _kv_cache(cache_p, wk_p, wv_p):
    hw = MLA_HEADS * HEAD_W
    return pl.pallas_call(
        _kv_cache_kernel,
        grid=(DEC_BATCH, DEPTH),
        in_specs=[pl.BlockSpec((1, 1, PAST_LEN, 2 * KV_LORA), lambda b, l: (b, l, 0, 0)),
                  pl.BlockSpec((1, 2 * KV_LORA, hw), lambda b, l: (l, 0, 0)),
                  pl.BlockSpec((1, VT_ROWS, KV_LORA), lambda b, l: (l, 0, 0))],
        out_specs=[pl.BlockSpec((1, 1, PAST_LEN, hw), lambda b, l: (b, l, 0, 0)),
                   pl.BlockSpec((1, 1, VT_ROWS, PAST_LEN), lambda b, l: (b, l, 0, 0))],
        out_shape=[jax.ShapeDtypeStruct((DEC_BATCH, DEPTH, PAST_LEN, hw), BF16),
                   jax.ShapeDtypeStruct((DEC_BATCH, DEPTH, VT_ROWS, PAST_LEN), BF16)],
        compiler_params=_cp(("arbitrary", "arbitrary")),
        name="kv_cache",
    )(cache_p, wk_p, wv_p)


def _attn_kernel(*refs, heads, has_cache):
    if has_cache:
        q_ref, k_ref, vt_ref, kc_ref, vct_ref, o_ref = refs
    else:
        q_ref, k_ref, vt_ref, o_ref = refs
    n_keys = k_ref.shape[0]
    tk = min(TK_ATT, n_keys)
    tiles = ([("cache", 0, PAST_LEN)] if has_cache else []) + [("own", s0, tk) for s0 in range(0, n_keys, tk)]
    units = [(h, t) for t in range(len(tiles)) for h in range(heads)]
    qs = [q_ref[:, h * HEAD_W:(h + 1) * HEAD_W] for h in range(heads)]

    def scores(h, t):
        kind, s0, n = tiles[t]
        sl = slice(h * HEAD_W, (h + 1) * HEAD_W)
        keys = kc_ref[0, 0, :, sl] if kind == "cache" else k_ref[s0:s0 + n, sl]
        st = _dot_nt(keys, qs[h])
        return st, jnp.max(st, axis=0, keepdims=True)

    def values_t(h, t):
        kind, s0, n = tiles[t]
        sv = slice(h * MLA_V, (h + 1) * MLA_V)
        vt = vct_ref[0, 0, sv, :] if kind == "cache" else vt_ref[sv, s0:s0 + n]
        return jnp.concatenate([vt, jnp.ones((ONES_ROWS, n), BF16)], axis=0)

    ahead = ATT_AHEAD
    pending = {u: scores(*u) for u in units[:ahead]}
    m, acc = [None] * heads, [None] * heads
    for i, (h, t) in enumerate(units):
        if i + ahead < len(units):
            pending[units[i + ahead]] = scores(*units[i + ahead])
        st, mt = pending.pop((h, t))
        if m[h] is None:
            m[h] = mt
            acc[h] = _dot(values_t(h, t), jnp.exp2(st - mt).astype(BF16))
        else:
            m_new = jnp.maximum(m[h], mt)
            alpha = jnp.exp2(m[h] - m_new)
            acc[h] = alpha * acc[h] + _dot(values_t(h, t), jnp.exp2(st - m_new).astype(BF16))
            m[h] = m_new
    outs = [acc[h][:MLA_V] / acc[h][MLA_V:MLA_V + 1] for h in range(heads)]
    o_ref[...] = jnp.concatenate(outs, axis=0).T.astype(BF16)


def _attention_ctx(qp, kp, vtp):
    hw = MLA_HEADS * HEAD_W
    return pl.pallas_call(
        functools.partial(_attn_kernel, heads=MLA_HEADS, has_cache=False),
        grid=(BATCH,),
        in_specs=[pl.BlockSpec((SEQ, hw), lambda s: (s, 0)),
                  pl.BlockSpec((SEQ, hw), lambda s: (s, 0)),
                  pl.BlockSpec((VT_ROWS, SEQ), lambda s: (0, s))],
        out_specs=pl.BlockSpec((SEQ, MLA_HEADS * MLA_V), lambda s: (s, 0)),
        out_shape=jax.ShapeDtypeStruct((NCTX, MLA_HEADS * MLA_V), BF16),
        compiler_params=_cp(("arbitrary",)),
        name="attn_ctx",
    )(qp, kp, vtp)


def _attention_smp(qp, kp, vtp, kc, vct, layer):
    hpb = 8
    wq = hpb * HEAD_W
    wv = hpb * MLA_V
    nq = DEC_SEQ // TQ_ATT
    qoff = NCTX // TQ_ATT
    koff = NCTX // DEC_SEQ
    return pl.pallas_call(
        functools.partial(_attn_kernel, heads=hpb, has_cache=True),
        grid=(DEC_BATCH, MLA_HEADS // hpb, nq),
        in_specs=[pl.BlockSpec((TQ_ATT, wq), lambda b, g, i: (qoff + b * nq + i, g)),
                  pl.BlockSpec((DEC_SEQ, wq), lambda b, g, i: (koff + b, g)),
                  pl.BlockSpec((wv, DEC_SEQ), lambda b, g, i: (g, koff + b)),
                  pl.BlockSpec((1, 1, PAST_LEN, wq), lambda b, g, i: (b, layer, 0, g)),
                  pl.BlockSpec((1, 1, wv, PAST_LEN), lambda b, g, i: (b, layer, g, 0))],
        out_specs=pl.BlockSpec((TQ_ATT, hpb * MLA_V), lambda b, g, i: (b * nq + i, g)),
        out_shape=jax.ShapeDtypeStruct((NSMP, MLA_HEADS * MLA_V), BF16),
        compiler_params=_cp(("arbitrary", "arbitrary", "arbitrary")),
        name="attn_smp",
    )(qp, kp, vtp, kc, vct)


def _scan_specs(ctx, blk, widths_and_cols, dir_cols=()):
    nseq, seq_len = (BATCH, SEQ) if ctx else (DEC_BATCH, DEC_SEQ)
    nblk = seq_len // blk
    roff = 0 if ctx else NCTX // blk

    def rb(dd):
        return lambda s, j: roff + s * nblk + (j if dd == 0 else nblk - 1 - j)

    specs = []
    for dd in range(2):
        r = rb(dd)
        specs += [pl.BlockSpec((blk, w), lambda s, j, r=r, c=c: (r(s, j), c)) for w, c in widths_and_cols]
        specs += [pl.BlockSpec((blk, w), lambda s, j, r=r, c=c + dd: (r(s, j), c)) for w, c in dir_cols]
    out_maps = [lambda s, j, r=rb(dd): (r(s, j) - roff, 0) for dd in range(2)]
    return nseq, nblk, specs, out_maps


def _gla_kernel(*refs, nblk, has_s0, nch):
    per_dir = [refs[0:4], refs[4:8]]
    wg_ref, bg_ref = refs[8:10]
    if has_s0:
        s0_ref, of_ref, ob_ref, s_scr = refs[10:]
    else:
        of_ref, ob_ref, sfin_ref, s_scr = refs[10:]
    o_refs = [of_ref, ob_ref]
    j = pl.program_id(1)

    @pl.when(j == 0)
    def _():
        if has_s0:
            s_scr[...] = s0_ref[0, 0]
        else:
            s_scr[...] = jnp.zeros_like(s_scr)

    r64 = lax.broadcasted_iota(jnp.int32, (CHUNK, CHUNK), 0)
    c64 = lax.broadcasted_iota(jnp.int32, (CHUNK, CHUNK), 1)
    incl = [c64 <= r64, c64 >= r64]
    tri_b = [jnp.where(m, 1.0, 0.0).astype(BF16) for m in incl]
    heads = [(slice(h * GLA_DK, (h + 1) * GLA_DK), slice(h * GLA_DV, (h + 1) * GLA_DV))
             for h in range(GLA_HEADS)]

    ch = []
    for c in range(nch):
        for dd in range(2):
            ci = c if dd == 0 else nch - 1 - c
            rows = slice(ci * CHUNK, (ci + 1) * CHUNK)
            glr_ref = per_dir[dd][3]
            ch.append(dict(dd=dd, rows=rows, z=_dot(glr_ref[rows, :], wg_ref[0, dd]) + bg_ref[0, dd]))
    for t in ch:
        t["la"] = _split3(_log_sigmoid(t["z"]) * (1.0 / GLA_TAU))
    for t in ch:
        hi, mid, lo = t["la"]
        tb = tri_b[t["dd"]]
        t["bc"] = _dot(tb, hi) + (_dot(tb, mid) + _dot(tb, lo))
    for t in ch:
        q_ref, k_ref, v_ref, _ = per_dir[t["dd"]]
        rows, bc = t["rows"], t["bc"]
        q = q_ref[rows, :].astype(F32) * (GLA_DK ** -0.5)
        k = k_ref[rows, :].astype(F32)
        bl = bc[CHUNK - 1:CHUNK] if t["dd"] == 0 else bc[0:1]
        t["qd"] = (q * jnp.exp(bc)).astype(BF16)
        t["ki"] = (k * jnp.exp(-bc)).astype(BF16)
        t["ke"] = (k * jnp.exp(bl - bc)).astype(BF16)
        t["dec_t"] = jnp.transpose(jnp.broadcast_to(jnp.exp(bl), (LANE, GLA_HEADS * GLA_DK)))
        t["v"] = v_ref[rows, :]
    for t in ch:
        t["a"] = [jnp.where(incl[t["dd"]], _dot_nt(t["qd"][:, sk], t["ki"][:, sk]), 0.0).astype(BF16)
                  for sk, _ in heads]
    for t in ch:
        t["oi"] = [_dot(t["a"][h], t["v"][:, sv]) for h, (_, sv) in enumerate(heads)]
        t["upd"] = [_dot_tn(t["ke"][:, sk], t["v"][:, sv]) for sk, sv in heads]
    states = [[s_scr[dd, h] for h in range(GLA_HEADS)] for dd in range(2)]
    for t in ch:
        st = states[t["dd"]]
        t["s_in"] = [s.astype(BF16) for s in st]
        for h, (sk, _) in enumerate(heads):
            st[h] = t["dec_t"][sk, :] * st[h] + t["upd"][h]
    for t in ch:
        for h, (sk, sv) in enumerate(heads):
            o_refs[t["dd"]][t["rows"], sv] = t["oi"][h] + _dot(t["qd"][:, sk], t["s_in"][h])
    for dd in range(2):
        for h in range(GLA_HEADS):
            s_scr[dd, h] = states[dd][h]

    if not has_s0:
        @pl.when(j == nblk - 1)
        def _():
            sfin_ref[0] = s_scr[...]


def _gla(proj, wg_p, bg_p, s0, layer, ctx):
    blk = GLA_BLK_CTX if ctx else GLA_BLK_SMP
    hk = GLA_HEADS * GLA_DK
    hv = GLA_HEADS * GLA_DV
    nseq, nblk, in_specs, out_maps = _scan_specs(
        ctx, blk, [(hk, PC_GQ // hk), (hk, PC_GK // hk), (hv, PC_GV // hv), (LANE, PC_GLR // LANE)])
    in_specs += [pl.BlockSpec((1, 2, LANE, hk), lambda s, j: (layer, 0, 0, 0)),
                 pl.BlockSpec((1, 2, 1, hk), lambda s, j: (layer, 0, 0, 0))]
    args = [proj] * 8 + [wg_p, bg_p]
    nrows = NCTX if ctx else NSMP
    out_specs = [pl.BlockSpec((blk, hv), m) for m in out_maps]
    out_shape = [jax.ShapeDtypeStruct((nrows, hv), F32)] * 2
    st_blk = (2, GLA_HEADS, GLA_DK, GLA_DV)
    if ctx:
        out_specs.append(pl.BlockSpec((1,) + st_blk, lambda s, j: (s, 0, 0, 0, 0)))
        out_shape.append(jax.ShapeDtypeStruct((BATCH,) + st_blk, F32))
    else:
        in_specs.append(pl.BlockSpec((1, 1) + st_blk, lambda s, j: (s, layer, 0, 0, 0, 0)))
        args.append(s0)
    return pl.pallas_call(
        functools.partial(_gla_kernel, nblk=nblk, has_s0=not ctx, nch=blk // CHUNK),
        grid=(nseq, nblk),
        in_specs=in_specs, out_specs=out_specs, out_shape=out_shape,
        scratch_shapes=[pltpu.VMEM(st_blk, F32)],
        compiler_params=_cp(("arbitrary", "arbitrary")),
        name="gla_ctx" if ctx else "gla_smp",
    )(*args)


GDN_G = 2
GDN_GW = GDN_G * GDN_DK
HALO = 16


def _gdn_pre_kernel(x_ref, xp_ref, xn_ref, cw_ref, o_ref):
    blk = x_ref.shape[0]
    hk = GDN_HEADS * GDN_DK
    sub = SEQ
    cw = 0.5 * cw_ref[0]
    rowi = lax.broadcasted_iota(jnp.int32, (sub, 1), 0)
    lane = lax.broadcasted_iota(jnp.int32, (sub, LANE), 1)
    lo_half = lane < GDN_DK

    def l2n(t, scale):
        parts = []
        for c in range(hk // LANE):
            tc = t[:, c * LANE:(c + 1) * LANE]
            sq = tc * tc
            s_lo = jnp.sum(jnp.where(lo_half, sq, 0.0), axis=-1, keepdims=True)
            s_hi = jnp.sum(jnp.where(lo_half, 0.0, sq), axis=-1, keepdims=True)
            parts.append(tc * (lax.rsqrt(jnp.where(lo_half, s_lo, s_hi) + EPS) * scale))
        return jnp.concatenate(parts, axis=1)

    for s in range(blk // sub):
        rows = slice(s * sub, (s + 1) * sub)
        x = x_ref[rows, :].astype(F32)
        r0 = pl.program_id(0) * blk + s * sub
        pos0 = jnp.where(r0 < NCTX, r0 % SEQ, (r0 - NCTX) % DEC_SEQ)
        seq_len = jnp.where(r0 < NCTX, SEQ, DEC_SEQ)
        before = xp_ref[...] if s == 0 else x_ref[s * sub - HALO:s * sub, :]
        after = xn_ref[...] if s == blk // sub - 1 else x_ref[(s + 1) * sub:(s + 1) * sub + HALO, :]
        prev_row = jnp.where(pos0 > 0, before.astype(F32)[HALO - 1:HALO], 0.0)
        next_row = jnp.where(pos0 + sub < seq_len, after.astype(F32)[0:1], 0.0)
        x_prev = jnp.where(rowi == 0, prev_row, pltpu.roll(x, 1, 0))
        x_next = jnp.where(rowi == sub - 1, next_row, pltpu.roll(x, sub - 1, 0))
        y = _silu_of_twice(x_prev * cw[0:1] + x * cw[1:2] + x_next * cw[2:3])
        o_ref[rows, :hk] = l2n(y[:, :hk], GDN_DK ** -0.5).astype(BF16)
        o_ref[rows, hk:2 * hk] = l2n(y[:, hk:2 * hk], 1.0).astype(BF16)
        o_ref[rows, 2 * hk:] = y[:, 2 * hk:].astype(BF16)


def _gdn_pre(proj, conv_w, layer):
    cw = 2 * GDN_HEADS * GDN_DK + GDN_HEADS * GDN_DV
    blk = GDN_PRE_BLK
    halo = blk // HALO
    last_h = R // HALO - 1
    return pl.pallas_call(
        _gdn_pre_kernel,
        grid=(R // blk,),
        in_specs=[pl.BlockSpec((blk, cw), lambda i: (i, PC_DQKV // cw)),
                  pl.BlockSpec((HALO, cw), lambda i: (jnp.maximum(i * halo - 1, 0), PC_DQKV // cw)),
                  pl.BlockSpec((HALO, cw), lambda i: (jnp.minimum((i + 1) * halo, last_h), PC_DQKV // cw)),
                  pl.BlockSpec((1, 3, cw), lambda i: (layer, 0, 0))],
        out_specs=pl.BlockSpec((blk, cw), lambda i: (i, 0)),
        out_shape=jax.ShapeDtypeStruct((R, cw), BF16),
        compiler_params=_cp(("arbitrary",)),
        name="gdn_pre",
    )(proj, proj, proj, conv_w)


def _gdn_kernel(*refs, nblk, has_s0):
    per_dir = [refs[0:4], refs[4:8]]
    alog_ref, dtb_ref = refs[8:10]
    if has_s0:
        s0_ref, of_ref, ob_ref, s_scr = refs[10:]
    else:
        of_ref, ob_ref, sfin_ref, s_scr = refs[10:]
    o_refs = [of_ref, ob_ref]
    j = pl.program_id(1)
    ngrp = GDN_HEADS // GDN_G

    def head_block(h):
        g, hh = divmod(h, GDN_G)
        return g, slice(hh * GDN_DK, (hh + 1) * GDN_DK), slice(hh * GDN_DV, (hh + 1) * GDN_DV)

    @pl.when(j == 0)
    def _():
        s_scr[...] = jnp.zeros_like(s_scr)
        if has_s0:
            for dd in range(2):
                for h in range(GDN_HEADS):
                    g, rk, rv = head_block(h)
                    s_scr[dd, g, rk, rv] = s0_ref[0, 0, dd, h]

    row = lax.broadcasted_iota(jnp.int32, (CHUNK, GDN_GW), 0)
    col = lax.broadcasted_iota(jnp.int32, (CHUNK, GDN_GW), 1) % CHUNK
    r64 = lax.broadcasted_iota(jnp.int32, (CHUNK, CHUNK), 0)
    c64 = lax.broadcasted_iota(jnp.int32, (CHUNK, CHUNK), 1)
    dir_masks = []
    for dd in range(2):
        lag = (row - col) if dd == 0 else (col - row)
        tri = (c64 <= r64) if dd == 0 else (c64 >= r64)
        dir_masks.append(dict(incl=lag >= 0, strict=lag > 0,
                              tri_b=jnp.where(tri, 1.0, 0.0).astype(BF16)))
    eye = col == row
    eye_f = jnp.where(eye, 1.0, 0.0)
    blk8 = (row // 8) == (col // 8)
    blk16 = (row // 16) == (col // 16)
    blk32 = (row // 32) == (col // 32)
    brow = lax.broadcasted_iota(jnp.int32, (GDN_GW, GDN_GW), 0) // CHUNK
    bcol = lax.broadcasted_iota(jnp.int32, (GDN_GW, GDN_GW), 1) // CHUNK
    same_head = brow == bcol

    def bdiag(t):
        tb = t.astype(BF16)
        return jnp.where(same_head, jnp.concatenate([tb] * GDN_G, axis=0), jnp.zeros((), BF16))

    def bdmm(a, b):
        return _dot(a.astype(BF16), bdiag(b))

    def setup(dd, c):
        q_ref, k_ref, v_ref, ab_ref = per_dir[dd]
        mk = dir_masks[dd]
        ci = c if dd == 0 else NCH - 1 - c
        rows = slice(ci * CHUNK, (ci + 1) * CHUNK)
        ab = ab_ref[rows, :].astype(F32)
        neg_a = -jnp.exp(alog_ref[0, dd])
        gam = _cumsum_rows(mk["tri_b"], neg_a * _softplus(ab + dtb_ref[0, dd]))
        beta_all = _sigmoid(ab)
        out = []
        for g in range(ngrp):
            gl = slice(g * GDN_GW, (g + 1) * GDN_GW)
            gcol = jnp.concatenate(
                [jnp.broadcast_to(gam[:, g * GDN_G + h:g * GDN_G + h + 1], (CHUNK, GDN_DK))
                 for h in range(GDN_G)], axis=1)
            bcol_ = jnp.concatenate(
                [jnp.broadcast_to(beta_all[:, GDN_HEADS + g * GDN_G + h:GDN_HEADS + g * GDN_G + h + 1],
                                  (CHUNK, GDN_DK)) for h in range(GDN_G)], axis=1)
            grow = jnp.sum(jnp.where(eye, gcol, 0.0), axis=0, keepdims=True)
            glast = gcol[CHUNK - 1:CHUNK] if dd == 0 else gcol[0:1]
            decay = jnp.where(mk["incl"], jnp.exp(gcol - grow), 0.0)
            egc = jnp.exp(gcol)
            qg = q_ref[rows, gl].astype(F32)
            kg = k_ref[rows, gl].astype(F32)
            vg = v_ref[rows, gl].astype(F32)
            kq = _dot_nt(jnp.concatenate([k_ref[rows, gl], q_ref[rows, gl]], axis=0),
                         bdiag(k_ref[rows, gl]))
            out.append(dict(dd=dd, g=g, rows=rows, gl=gl,
                            m=jnp.where(mk["strict"], bcol_ * kq[:CHUNK] * decay, 0.0),
                            aqk=(kq[CHUNK:] * decay).astype(BF16),
                            vb=vg * bcol_, kb=kg * (bcol_ * egc), qe=qg * egc,
                            kend=(kg * jnp.exp(glast - gcol)).astype(BF16),
                            eg=jnp.exp(glast)))
        return out

    def stages(ch):
        pairs = list(zip(ch[0::2], ch[1::2]))

        def bdmm_pairs(lhs, rhs, out):
            for ta, tb in pairs:
                a = jnp.concatenate([ta[lhs].astype(BF16), tb[lhs].astype(BF16)], axis=0)
                w = jnp.concatenate([bdiag(ta[rhs]), bdiag(tb[rhs])], axis=1)
                y = _dot(a, w)
                ta[out] = y[:CHUNK, :GDN_GW]
                tb[out] = y[CHUNK:, GDN_GW:]

        def neumann_a():
            for t in ch:
                t["n8"] = jnp.where(blk8, t["m"], 0.0)
            bdmm_pairs("n8", "n8", "n2")

        def neumann_b():
            for t in ch:
                t["p1"] = eye_f - t["n8"]
            bdmm_pairs("n2", "n2", "n4")
            bdmm_pairs("p1", "n2", "p1n2")
            for t in ch:
                t["p1"] = t["p1"] + t["p1n2"]

        def neumann_c():
            bdmm_pairs("p1", "n4", "p1n4")
            for t in ch:
                t["dinv"] = t["p1"] + t["p1n4"]

        def double_a(off):
            def run():
                for t in ch:
                    t["off"] = jnp.where(off, t["m"], 0.0)
                bdmm_pairs("dinv", "off", "dl")
            return run

        def double_b():
            bdmm_pairs("dl", "dinv", "dld")
            for t in ch:
                t["dinv"] = t["dinv"] - t["dld"]

        def solve():
            for t in ch:
                uw = _dot(t["dinv"].astype(BF16), jnp.concatenate([bdiag(t["vb"]), bdiag(t["kb"])], axis=1))
                t["u"] = uw[:, :GDN_GW]
                t["w"] = uw[:, GDN_GW:]

        def fold():
            for t in ch:
                wu = jnp.concatenate([t["w"], t["u"]], axis=1).astype(BF16)
                cb = _dot_tn(t["kend"], wu)
                t["c"] = jnp.where(same_head, cb[:, :GDN_GW], 0.0).astype(BF16)
                t["b"] = jnp.where(same_head, cb[:, GDN_GW:], 0.0)
                ao = _dot(t["aqk"], jnp.concatenate([bdiag(t["w"]), bdiag(t["u"])], axis=1))
                t["qt"] = (t["qe"] - ao[:, :GDN_GW]).astype(BF16)
                t["o"] = ao[:, GDN_GW:]

        out = [neumann_a, neumann_b, neumann_c]
        for inner, outer in ((blk8, blk16), (blk16, blk32), (blk32, None)):
            off = jnp.logical_not(inner) if outer is None else jnp.logical_and(outer, jnp.logical_not(inner))
            out += [double_a(off), double_b]
        return out + [solve, fold]

    states = [[s_scr[dd, g] for g in range(ngrp)] for dd in range(2)]

    def recur(cur):
        ys = [_dot(jnp.concatenate([t["qt"], t["c"]], axis=0), states[t["dd"]][t["g"]].astype(BF16))
              for t in cur]
        for t, y in zip(cur, ys):
            o_refs[t["dd"]][t["rows"], t["gl"]] = y[:CHUNK] + t["o"]
            states[t["dd"]][t["g"]] = t["eg"] * states[t["dd"]][t["g"]] - y[CHUNK:] + t["b"]

    half = NCH // 2
    chains = {}
    for c in range(half):
        for dd in range(2):
            chains[(dd, c)] = setup(dd, c)
    later = [(dd, c) for c in range(half, NCH) for dd in range(2)]
    wave_a = [t for c in range(half) for dd in range(2) for t in chains[(dd, c)]]
    for k, stage in enumerate(stages(wave_a)):
        stage()
        if k % 2 == 0 and later:
            key = later.pop(0)
            chains[key] = setup(*key)
    for key in later:
        chains[key] = setup(*key)
    wave_b = [t for c in range(half, NCH) for dd in range(2) for t in chains[(dd, c)]]
    done = 0
    for k, stage in enumerate(stages(wave_b)):
        stage()
        if k % 3 == 1 and done < half:
            recur(chains[(0, done)] + chains[(1, done)])
            done += 1
    for c in range(done, NCH):
        recur(chains[(0, c)] + chains[(1, c)])
    for dd in range(2):
        for g in range(ngrp):
            s_scr[dd, g] = states[dd][g]

    if not has_s0:
        @pl.when(j == nblk - 1)
        def _():
            for dd in range(2):
                for h in range(GDN_HEADS):
                    g, rk, rv = head_block(h)
                    sfin_ref[0, dd, h] = s_scr[dd, g, rk, rv]


def _gdn(qkv, proj, alog_p, dtb_p, s0, layer, ctx):
    nseq, nblk = (BATCH, SEQ // SEQ_BLK) if ctx else (DEC_BATCH, DEC_SEQ // SEQ_BLK)
    roff = 0 if ctx else NCTX // SEQ_BLK
    hk = GDN_HEADS * GDN_DK
    hv = GDN_HEADS * GDN_DV

    def rb(dd):
        return lambda s, j: roff + s * nblk + (j if dd == 0 else nblk - 1 - j)

    in_specs, args = [], []
    for dd in range(2):
        r = rb(dd)
        in_specs += [pl.BlockSpec((SEQ_BLK, hk), lambda s, j, r=r: (r(s, j), 0)),
                     pl.BlockSpec((SEQ_BLK, hk), lambda s, j, r=r: (r(s, j), 1)),
                     pl.BlockSpec((SEQ_BLK, hv), lambda s, j, r=r: (r(s, j), 2 * hk // hv)),
                     pl.BlockSpec((SEQ_BLK, LANE), lambda s, j, r=r, dd=dd: (r(s, j), PC_DAB // LANE + dd))]
        args += [qkv, qkv, qkv, proj]
    in_specs += [pl.BlockSpec((1, 2, 1, LANE), lambda s, j: (layer, 0, 0, 0))] * 2
    args += [alog_p, dtb_p]
    nrows = NCTX if ctx else NSMP
    out_specs = [pl.BlockSpec((SEQ_BLK, hv), lambda s, j, r=rb(dd): (r(s, j) - roff, 0)) for dd in range(2)]
    out_shape = [jax.ShapeDtypeStruct((nrows, hv), F32)] * 2
    st_blk = (2, GDN_HEADS, GDN_DK, GDN_DV)
    if ctx:
        out_specs.append(pl.BlockSpec((1,) + st_blk, lambda s, j: (s, 0, 0, 0, 0)))
        out_shape.append(jax.ShapeDtypeStruct((BATCH,) + st_blk, F32))
    else:
        in_specs.append(pl.BlockSpec((1, 1) + st_blk, lambda s, j: (s, layer, 0, 0, 0, 0)))
        args.append(s0)
    return pl.pallas_call(
        functools.partial(_gdn_kernel, nblk=nblk, has_s0=not ctx),
        grid=(nseq, nblk),
        in_specs=in_specs, out_specs=out_specs, out_shape=out_shape,
        scratch_shapes=[pltpu.VMEM((2, GDN_HEADS // GDN_G, GDN_GW, GDN_GW), F32)],
        compiler_params=_cp(("arbitrary", "arbitrary")),
        name="gdn_ctx" if ctx else "gdn_smp",
    )(*args)


def _group_rms(x, width):
    parts = []
    lane = lax.broadcasted_iota(jnp.int32, (x.shape[0], LANE), 1)
    lo_half = lane < 64
    for c in range(x.shape[1] // LANE):
        xc = x[:, c * LANE:(c + 1) * LANE]
        sq = xc * xc
        if width == LANE:
            ms = jnp.mean(sq, axis=-1, keepdims=True)
        else:
            s_lo = jnp.sum(jnp.where(lo_half, sq, 0.0), axis=-1, keepdims=True)
            s_hi = jnp.sum(jnp.where(lo_half, 0.0, sq), axis=-1, keepdims=True)
            ms = jnp.where(lo_half, s_lo, s_hi) * (1.0 / width)
        parts.append(xc * lax.rsqrt(ms + EPS))
    return jnp.concatenate(parts, axis=1)


def _merge_kernel(ymc_ref, yms_ref, ogcf_ref, ogcb_ref, ogsf_ref, ogsb_ref, gr_ref,
                  odcf_ref, odcb_ref, odsf_ref, odsb_ref,
                  dz_ref, gt_ref, *rest, split_h):
    if split_h:
        hc_ref, hs_ref, *rest = rest
        h_rows = lambda: _family_rows((hc_ref, hs_ref), TM_MERGE)
    else:
        h_ref, *rest = rest
        h_rows = lambda: h_ref[...]
    ga_ref, gn_ref, dn_ref, wb_ref, wo_ref, o_ref, ym_scr, og_scr, od_scr = rest
    is_ctx = pl.program_id(0) < NCTX // TM_MERGE

    @pl.when(is_ctx)
    def _():
        ym_scr[...] = ymc_ref[...]
        og_scr[...] = ogcf_ref[...] + ogcb_ref[...]
        od_scr[...] = odcf_ref[...] + odcb_ref[...]

    @pl.when(jnp.logical_not(is_ctx))
    def _():
        ym_scr[...] = yms_ref[...]
        og_scr[...] = ogsf_ref[...] + ogsb_ref[...]
        od_scr[...] = odsf_ref[...] + odsb_ref[...]

    y_gla = _group_rms(og_scr[...], GLA_DV) * gn_ref[0] * _silu_of_twice(gr_ref[...].astype(F32))
    y_gdn = _group_rms(od_scr[...], GDN_DV) * dn_ref[0] * _silu_of_twice(dz_ref[...].astype(F32))
    t = jnp.tanh(gt_ref[...].astype(F32))
    d = D_MODEL
    acc = None
    for n, y_n in enumerate((ym_scr[...], y_gla.astype(BF16), y_gdn.astype(BF16))):
        p = _dot(y_n, wb_ref[0, n])
        gated = p + p * t[:, n * d:(n + 1) * d]
        acc = gated if acc is None else acc + gated
    y = _dot(acc.astype(BF16), wo_ref[0])
    o_ref[...] = h_rows() + ga_ref[0] * y


def _merge(y_mla, o_gla, o_gdn, proj, h, mod3, gla_norm_p, gdn_norm_p, w_branch_b, w_out_b, layer):
    tm = TM_MERGE
    bw = 512
    nc = NCTX // tm
    split_h = isinstance(h, tuple)
    h_specs = _family_specs(tm, D_MODEL) if split_h else [pl.BlockSpec((tm, D_MODEL), lambda i: (i, 0))]
    h_args = list(h) if split_h else [h]

    def cmap(i):
        return jnp.minimum(i, nc - 1)

    def smap(i):
        return jnp.maximum(i - nc, 0)

    return pl.pallas_call(
        functools.partial(_merge_kernel, split_h=split_h),
        grid=(R // tm,),
        in_specs=[pl.BlockSpec((tm, bw), lambda i: (cmap(i), 0)),
                  pl.BlockSpec((tm, bw), lambda i: (smap(i), 0)),
                  pl.BlockSpec((tm, bw), lambda i: (cmap(i), 0)),
                  pl.BlockSpec((tm, bw), lambda i: (cmap(i), 0)),
                  pl.BlockSpec((tm, bw), lambda i: (smap(i), 0)),
                  pl.BlockSpec((tm, bw), lambda i: (smap(i), 0)),
                  pl.BlockSpec((tm, bw), lambda i: (i, PC_GR // bw)),
                  pl.BlockSpec((tm, bw), lambda i: (cmap(i), 0)),
                  pl.BlockSpec((tm, bw), lambda i: (cmap(i), 0)),
                  pl.BlockSpec((tm, bw), lambda i: (smap(i), 0)),
                  pl.BlockSpec((tm, bw), lambda i: (smap(i), 0)),
                  pl.BlockSpec((tm, bw), lambda i: (i, PC_DZ // bw)),
                  pl.BlockSpec((tm, 3 * D_MODEL), lambda i: (i, PC_GATES // (3 * D_MODEL)))] + h_specs + [
                  _mod_spec(layer, 2, tm),
                  pl.BlockSpec((1, 1, bw), lambda i: (layer, 0, 0)),
                  pl.BlockSpec((1, 1, bw), lambda i: (layer, 0, 0)),
                  pl.BlockSpec((1, 3, bw, D_MODEL), lambda i: (layer, 0, 0, 0), pipeline_mode=pl.Buffered(1)),
                  pl.BlockSpec((1, D_MODEL, D_MODEL), lambda i: (layer, 0, 0), pipeline_mode=pl.Buffered(1))],
        out_specs=pl.BlockSpec((tm, D_MODEL), lambda i: (i, 0)),
        out_shape=jax.ShapeDtypeStruct((R, D_MODEL), F32),
        scratch_shapes=[pltpu.VMEM((tm, bw), BF16), pltpu.VMEM((tm, bw), F32), pltpu.VMEM((tm, bw), F32)],
        compiler_params=_cp(("arbitrary",)),
        name="merge",
    )(y_mla[0], y_mla[1], o_gla[0][0], o_gla[0][1], o_gla[1][0], o_gla[1][1], proj,
      o_gdn[0][0], o_gdn[0][1], o_gdn[1][0], o_gdn[1][1],
      proj, proj, *h_args, mod3, gla_norm_p, gdn_norm_p, w_branch_b, w_out_b)


def _ffn_kernel(h_ref, g_ref, sc_ref, sh_ref, gf_ref, wi_ref, wo_ref, *rest, final):
    if final:
        fn_ref, *o_refs = rest
    else:
        gn_ref, scn_ref, shn_ref, *o_refs = rest
    h = h_ref[...]
    xf = (_rms(h, g_ref[0]) * (1.0 + sc_ref[0]) + sh_ref[0]).astype(BF16)

    def gate_up(c):
        f0 = c * TF_FFN
        return (_dot(xf, wi_ref[0, :, f0:f0 + TF_FFN]),
                _dot(xf, wi_ref[0, :, D_FF + f0:D_FF + f0 + TF_FFN]))

    n_chunks = D_FF // TF_FFN
    nxt = gate_up(0)
    acc = None
    for c in range(n_chunks):
        g, u = nxt
        if c + 1 < n_chunks:
            nxt = gate_up(c + 1)
        part = _dot((_silu(g) * u).astype(BF16), wo_ref[0, c * TF_FFN:(c + 1) * TF_FFN, :])
        acc = part if acc is None else acc + part
    out = h + gf_ref[0] * acc
    if not final:
        o_refs[0][...] = out
        o_refs[1][...] = (_rms(out, gn_ref[0]) * (1.0 + scn_ref[0]) + shn_ref[0]).astype(BF16)
    else:
        out = _rms(out, fn_ref[...])
        is_ctx = pl.program_id(0) < NCTX // TM_FFN

        @pl.when(is_ctx)
        def _():
            o_refs[0][...] = out

        @pl.when(jnp.logical_not(is_ctx))
        def _():
            o_refs[1][...] = out


def _ffn(h, mod3, norm_ffn, wi_b, wo_b, tail, layer, final):
    tm = TM_FFN
    if final:
        nc = NCTX // tm
        out_specs = [pl.BlockSpec((tm, D_MODEL), lambda i: (jnp.minimum(i, nc - 1), 0)),
                     pl.BlockSpec((tm, D_MODEL), lambda i: (jnp.maximum(i - nc, 0), 0))]
        out_shape = [jax.ShapeDtypeStruct((NCTX, D_MODEL), F32), jax.ShapeDtypeStruct((NSMP, D_MODEL), F32)]
        tail_specs = [pl.BlockSpec((1, D_MODEL), lambda i: (0, 0))]
        tail_args = [tail.reshape(1, D_MODEL)]
    else:
        out_specs = [pl.BlockSpec((tm, D_MODEL), lambda i: (i, 0))] * 2
        out_shape = [jax.ShapeDtypeStruct((R, D_MODEL), F32), jax.ShapeDtypeStruct((R, D_MODEL), BF16)]
        tail_specs = [pl.BlockSpec((1, 1, D_MODEL), lambda i: (layer + 1, 0, 0)),
                      _mod_spec(layer + 1, 1, tm), _mod_spec(layer + 1, 0, tm)]
        tail_args = [tail.reshape(DEPTH, 1, D_MODEL), mod3, mod3]
    return pl.pallas_call(
        functools.partial(_ffn_kernel, final=final),
        grid=(R // tm,),
        in_specs=[pl.BlockSpec((tm, D_MODEL), lambda i: (i, 0)),
                  pl.BlockSpec((1, 1, D_MODEL), lambda i: (layer, 0, 0)),
                  _mod_spec(layer, 4, tm),
                  _mod_spec(layer, 3, tm),
                  _mod_spec(layer, 5, tm),
                  pl.BlockSpec((1, D_MODEL, 2 * D_FF), lambda i: (layer, 0, 0), pipeline_mode=pl.Buffered(1)),
                  pl.BlockSpec((1, D_FF, D_MODEL), lambda i: (layer, 0, 0), pipeline_mode=pl.Buffered(1))]
        + tail_specs,
        out_specs=out_specs, out_shape=out_shape,
        compiler_params=_cp(("arbitrary",)),
        name="ffn_final" if final else "ffn",
    )(h, norm_ffn.reshape(DEPTH, 1, D_MODEL), mod3, mod3, mod3, wi_b, wo_b, *tail_args)


def _rope_partner(w, axis=-1):
    h = AXIS_DIM // 2
    axis = axis % w.ndim
    parts = []
    for a in range(2):
        x1 = lax.slice_in_dim(w, a * AXIS_DIM, a * AXIS_DIM + h, axis=axis)
        x2 = lax.slice_in_dim(w, a * AXIS_DIM + h, (a + 1) * AXIS_DIM, axis=axis)
        parts += [-x2, x1]
    return jnp.concatenate(parts, axis=axis)


def _pack_w_in_kernel(w_ref, o_ref):
    o_ref[0] = _pack_channels(w_ref[0]).astype(BF16)


def _pack_w_in(w_in):
    tc = 256
    return pl.pallas_call(
        _pack_w_in_kernel,
        grid=(DEPTH, D_MODEL // tc),
        in_specs=[pl.BlockSpec((1, w_in.shape[-1], tc), lambda l, i: (l, 0, i))],
        out_specs=pl.BlockSpec((1, PROJ_W, tc), lambda l, i: (l, 0, i)),
        out_shape=jax.ShapeDtypeStruct((DEPTH, PROJ_W, D_MODEL), BF16),
        compiler_params=_cp(("arbitrary", "arbitrary")),
        name="pack_w_in",
    )(jnp.swapaxes(w_in, 1, 2))


def _pack_channels(wt):
    c = [0]
    for s in (Q_LORA, KV_LORA + ROPE_DIM, 256, 256, 512, 512, 2 * GLA_RANK, 1536, 512,
              2 * GDN_HEADS, 2 * GDN_HEADS, 3 * D_MODEL):
        c.append(c[-1] + s)
    mq, mkv, gq, gk, gv, gr, glr, dqkv, dz, da, db, gates = [wt[c[i]:c[i + 1]] for i in range(12)]

    def zeros(n):
        return jnp.zeros((n,) + wt.shape[1:], wt.dtype)

    kr = mkv[KV_LORA:]
    mla = [mq, mkv[:KV_LORA], kr, _rope_partner(kr, axis=0), zeros(64)]
    dab = []
    for d in range(2):
        dab += [da[d * GDN_HEADS:(d + 1) * GDN_HEADS], db[d * GDN_HEADS:(d + 1) * GDN_HEADS],
                zeros(LANE - 2 * GDN_HEADS)]
    return jnp.concatenate([0.5 * gates, dqkv, 0.5 * dz, gv, 0.5 * gr] + mla
                           + [gq, gk, glr, zeros(LANE - 2 * GLA_RANK)]
                           + dab + [zeros(PROJ_W - PC_DAB - 2 * LANE)], axis=0)


def _pack_mla_weights(w_uq, w_ukv):
    l = w_uq.shape[0]
    qh = w_uq.reshape(l, Q_LORA, MLA_HEADS, MLA_NOPE + ROPE_DIM)
    zq = jnp.zeros((l, Q_LORA, MLA_HEADS, HEAD_W - MLA_NOPE - ROPE_DIM), w_uq.dtype)
    zn = jnp.zeros((l, Q_LORA, MLA_HEADS, MLA_NOPE), w_uq.dtype)
    wq = jnp.concatenate([qh, zq], axis=-1).reshape(l, Q_LORA, MLA_HEADS * HEAD_W)
    wq_sw = jnp.concatenate([zn, _rope_partner(qh[..., MLA_NOPE:]), zq], axis=-1).reshape(l, Q_LORA, MLA_HEADS * HEAD_W)
    wq_p = jnp.concatenate([wq, wq_sw], axis=-1).astype(BF16)

    kvh = w_ukv.reshape(l, KV_LORA, MLA_HEADS, MLA_NOPE + MLA_V)
    zk = jnp.zeros((l, KV_LORA, MLA_HEADS, HEAD_W - MLA_NOPE), w_ukv.dtype)
    wk_top = jnp.concatenate([kvh[..., :MLA_NOPE], zk], axis=-1).reshape(l, KV_LORA, MLA_HEADS * HEAD_W)
    place = np.zeros((KV_LORA, MLA_HEADS, HEAD_W), np.float32)
    idx = np.arange(ROPE_DIM)
    for rep in range(2):
        place[rep * ROPE_DIM + idx, :, MLA_NOPE + idx] = 1.0
    wk_bot = jnp.broadcast_to(jnp.asarray(place.reshape(1, KV_LORA, MLA_HEADS * HEAD_W)),
                              (l, KV_LORA, MLA_HEADS * HEAD_W))
    wk_p = jnp.concatenate([wk_top, wk_bot], axis=1).astype(BF16)
    wv_p = jnp.transpose(kvh[..., MLA_NOPE:], (0, 2, 3, 1)).reshape(l, VT_ROWS, KV_LORA).astype(BF16)
    return wq_p, wk_p, wv_p


def _rope_tables(tm):
    f32 = np.float32
    rows = DEC_SEQ // GRID_W
    row = np.repeat(np.arange(rows, dtype=np.float64), GRID_W)
    col = np.tile(np.arange(GRID_W, dtype=np.float64), rows)
    inv = ROPE_THETA ** (-np.arange(0, AXIS_DIM, 2, dtype=np.float64) / AXIS_DIM)
    ang_r, ang_c = row[:, None] * inv, col[:, None] * inv
    cos32 = np.concatenate([np.cos(ang_r)] * 2 + [np.cos(ang_c)] * 2, axis=-1)
    sin32 = np.concatenate([np.sin(ang_r)] * 2 + [np.sin(ang_c)] * 2, axis=-1)
    cos32 = np.concatenate([np.ones((tm, ROPE_DIM)), cos32], axis=0)
    sin32 = np.concatenate([np.zeros((tm, ROPE_DIM)), sin32], axis=0)
    n = cos32.shape[0]
    scale = (MLA_NOPE + ROPE_DIM) ** -0.5 * math.log2(math.e)
    pad = HEAD_W - MLA_NOPE - ROPE_DIM
    cq = np.concatenate([np.ones((n, MLA_NOPE)), cos32, np.zeros((n, pad))], axis=-1) * scale
    sq = np.concatenate([np.zeros((n, MLA_NOPE)), sin32, np.zeros((n, pad))], axis=-1) * scale
    ck = np.concatenate([cos32, sin32, np.zeros((n, HEAD_W - 2 * ROPE_DIM))], axis=-1)
    return jnp.asarray(np.concatenate([cq, sq, ck], axis=-1).astype(f32))


def _pad_lanes(x, n):
    return jnp.pad(x, [(0, 0)] * (x.ndim - 1) + [(0, n - x.shape[-1])])


def kernel(x_prompt, x_sample, cache_mla, state_gla, state_gdn, c, c_ctx, w_mod, b_mod, norm_mix, w_in,
           mla_q_norm, mla_w_uq, mla_kv_norm, mla_w_ukv, gla_w_gate, gla_b_gate, gla_norm, gdn_conv,
           gdn_a_log, gdn_dt_bias, gdn_norm, w_branch, w_out, norm_ffn, ffn_w_in, ffn_w_out, final_norm):
    w_in_p = _pack_w_in(w_in)
    wq_p, wk_p, wv_p = _pack_mla_weights(mla_w_uq, mla_w_ukv)
    tab = _rope_tables(TM_PREP)
    wg_p = jnp.zeros((DEPTH, 2, LANE, GLA_HEADS * GLA_DK), F32)
    for d in range(2):
        wg_p = wg_p.at[:, d, d * GLA_RANK:(d + 1) * GLA_RANK, :].set(gla_w_gate[:, d])
    wg_p = wg_p.astype(BF16)
    bg_p = gla_b_gate.reshape(DEPTH, 2, 1, GLA_HEADS * GLA_DK)
    alog_p = _pad_lanes(gdn_a_log, LANE).reshape(DEPTH, 2, 1, LANE)
    dtb_p = _pad_lanes(gdn_dt_bias, LANE).reshape(DEPTH, 2, 1, LANE)
    gla_norm_p = jnp.tile(gla_norm, (1, GLA_HEADS)).reshape(DEPTH, 1, GLA_HEADS * GLA_DV)
    gdn_norm_p = jnp.tile(gdn_norm, (1, GDN_HEADS)).reshape(DEPTH, 1, GDN_HEADS * GDN_DV)
    w_branch_b = (0.5 * w_branch).astype(BF16)
    w_out_b = w_out.astype(BF16)
    wi_b = ffn_w_in.astype(BF16)
    wo_b = ffn_w_out.astype(BF16)
    cache_p = _pad_lanes(cache_mla, 2 * KV_LORA)
    cond8 = jnp.concatenate([c_ctx[None, :], c, jnp.zeros((MOD_ROWS - 1 - DEC_BATCH, D_MODEL), F32)], axis=0)

    mod = _modulation(cond8, w_mod, b_mod)
    mod3 = mod.reshape(DEPTH * MOD_ROWS * 6, 1, D_MODEL)
    kc, vc = _kv_cache(cache_p, wk_p, wv_p)

    h = (x_prompt.reshape(NCTX, D_MODEL), x_sample.reshape(NSMP, D_MODEL))
    kv_list, gla_list, gdn_list = [], [], []
    for l in range(DEPTH):
        proj = _inproj(h if l == 0 else xn, mod3, norm_mix, w_in_p, l)
        qp, kp, vtp, own = _mla_prep(proj, tab, mla_q_norm, mla_kv_norm, wq_p, wk_p, wv_p, l)
        y_mla = (_attention_ctx(qp, kp, vtp), _attention_smp(qp, kp, vtp, kc, vc, l))
        ogf_c, ogb_c, sg = _gla(proj, wg_p, bg_p, None, l, True)
        ogf_s, ogb_s = _gla(proj, wg_p, bg_p, state_gla, l, False)
        qkv = _gdn_pre(proj, gdn_conv, l)
        odf_c, odb_c, sd = _gdn(qkv, proj, alog_p, dtb_p, None, l, True)
        odf_s, odb_s = _gdn(qkv, proj, alog_p, dtb_p, state_gdn, l, False)
        h = _merge(y_mla, ((ogf_c, ogb_c), (ogf_s, ogb_s)), ((odf_c, odb_c), (odf_s, odb_s)), proj, h, mod3,
                   gla_norm_p, gdn_norm_p,
                   w_branch_b, w_out_b, l)
        if l == DEPTH - 1:
            h = _ffn(h, mod3, norm_ffn, wi_b, wo_b, final_norm, l, True)
        else:
            h, xn = _ffn(h, mod3, norm_ffn, wi_b, wo_b, norm_mix, l, False)
        kv_list.append(own[:NCTX, :KV_LORA + ROPE_DIM].reshape(BATCH, SEQ, KV_LORA + ROPE_DIM))
        gla_list.append(sg)
        gdn_list.append(sd)

    y_prompt = h[0].reshape(BATCH, SEQ, D_MODEL)
    y_sample = h[1].reshape(DEC_BATCH, DEC_SEQ, D_MODEL)
    return (y_prompt, y_sample, jnp.stack(kv_list, axis=1), jnp.stack(gla_list, axis=1),
            jnp.stack(gdn_list, axis=1))
```

```python
import functools
import math

import numpy as np
import jax
import jax.numpy as jnp
from jax import lax
from jax.experimental import pallas as pl
from jax.experimental.pallas import tpu as pltpu

F32 = jnp.float32
BF16 = jnp.bfloat16

D_MODEL = 1024
BATCH = 16
SEQ = 256
DEPTH = 2
DEC_BATCH = 2
DEC_SEQ = 4096
PAST_LEN = 256
GRID_W = 64
CHUNK = 64
EPS = 1e-6
MLA_HEADS = 8
MLA_NOPE = 64
ROPE_DIM = 32
AXIS_DIM = ROPE_DIM // 2
MLA_V = 64
Q_LORA = 256
KV_LORA = 128
ROPE_THETA = 10000.0
GLA_HEADS = 4
GLA_DK = 64
GLA_DV = 128
GLA_RANK = 16
GLA_TAU = 16.0
GDN_HEADS = 8
GDN_DK = 64
GDN_DV = 64
D_FF = ((8 * D_MODEL + 3 * 256 - 1) // (3 * 256)) * 256
MOD_W = 6 * D_MODEL

NCTX = BATCH * SEQ
NSMP = DEC_BATCH * DEC_SEQ
R = NCTX + NSMP
LANE = 128
HEAD_W = 128
VT_ROWS = MLA_HEADS * MLA_V
SEQ_BLK = 256
NCH = SEQ_BLK // CHUNK
GLA_BLK_CTX = 256
GLA_BLK_SMP = 512
MOD_ROWS = 8

PC_GATES = 0
PC_DQKV = 3072
PC_DZ = 4608
PC_GV = 5120
PC_GR = 5632
PC_MLA = 6144
PC_GQ = 6656
PC_GK = 6912
PC_GLR = 7168
PC_DAB = 7296
PROJ_W = 7680

TM_IN = 1024
TN_IN = 2560
TM_PREP = 1024
GDN_PRE_BLK = 1024
TM_MERGE = 512
TM_FFN = 512
TF_FFN = 256
TQ_ATT = 256
TK_ATT = 512
ONES_ROWS = 16
ATT_AHEAD = 8
VMEM_LIMIT = 48 * 1024 * 1024


def _cp(sem):
    return pltpu.CompilerParams(dimension_semantics=sem, vmem_limit_bytes=VMEM_LIMIT)


def _dot(a, b):
    return jnp.dot(a, b, preferred_element_type=F32)


def _dot_nt(a, b):
    return lax.dot_general(a, b, (((1,), (1,)), ((), ())), preferred_element_type=F32)


def _dot_tn(a, b):
    return lax.dot_general(a, b, (((0,), (0,)), ((), ())), preferred_element_type=F32)


def _split2(x):
    hi = x.astype(BF16)
    lo = (x - hi.astype(F32)).astype(BF16)
    return hi, lo


def _split3(x):
    hi = x.astype(BF16)
    r1 = x - hi.astype(F32)
    mid = r1.astype(BF16)
    lo = (r1 - mid.astype(F32)).astype(BF16)
    return hi, mid, lo


def _cumsum_rows(tri_b, x):
    hi, mid, lo = _split3(x)
    return _dot(tri_b, hi) + (_dot(tri_b, mid) + _dot(tri_b, lo))


def _sigmoid(x):
    return 0.5 * jnp.tanh(0.5 * x) + 0.5


def _silu(x):
    return x * _sigmoid(x)


def _silu_of_twice(hx):
    return hx + hx * jnp.tanh(hx)


def _softplus(x):
    return jnp.maximum(x, 0.0) + jnp.log(1.0 + jnp.exp(-jnp.abs(x)))


def _log_sigmoid(x):
    return jnp.minimum(x, 0.0) - jnp.log(1.0 + jnp.exp(-jnp.abs(x)))


def _rms(x, g):
    return x * lax.rsqrt(jnp.mean(x * x, axis=-1, keepdims=True) + EPS) * g


def _mod_row(row_start):
    return jnp.where(row_start < NCTX, 0, 1 + (row_start - NCTX) // DEC_SEQ)


def _mod_spec(layer, which, tm, axis=0):
    def imap(*idx):
        return ((layer * MOD_ROWS + _mod_row(idx[axis] * tm)) * 6 + which, 0, 0)
    return pl.BlockSpec((1, 1, D_MODEL), imap)


def _mod_kernel(c_ref, w_ref, b_ref, o_ref):
    x = _silu(c_ref[...])
    xh, xl = _split2(x)
    wh, wl = _split2(w_ref[0])
    o_ref[0] = _dot(xh, wh) + (_dot(xl, wh) + _dot(xh, wl)) + b_ref[0]


def _modulation(cond8, w_mod, b_mod):
    tn = 1536
    return pl.pallas_call(
        _mod_kernel,
        grid=(DEPTH, MOD_W // tn),
        in_specs=[pl.BlockSpec((MOD_ROWS, D_MODEL), lambda l, j: (0, 0)),
                  pl.BlockSpec((1, D_MODEL, tn), lambda l, j: (l, 0, j)),
                  pl.BlockSpec((1, 1, tn), lambda l, j: (l, 0, j))],
        out_specs=pl.BlockSpec((1, MOD_ROWS, tn), lambda l, j: (l, 0, j)),
        out_shape=jax.ShapeDtypeStruct((DEPTH, MOD_ROWS, MOD_W), F32),
        compiler_params=_cp(("arbitrary", "arbitrary")),
        name="modulation",
    )(cond8, w_mod, b_mod.reshape(DEPTH, 1, MOD_W))


def _family_specs(tm, width, axis=0):
    nc = NCTX // tm
    return [pl.BlockSpec((tm, width), lambda *idx: (jnp.minimum(idx[axis], nc - 1), 0)),
            pl.BlockSpec((tm, width), lambda *idx: (jnp.maximum(idx[axis] - nc, 0), 0))]


def _family_rows(refs, tm, axis=0):
    is_ctx = pl.program_id(axis) < NCTX // tm
    return jnp.where(is_ctx, refs[0][...], refs[1][...])


def _inproj_kernel(*refs, normed):
    if normed:
        xn_ref, w_ref, o_ref = refs
        xn = xn_ref[...]
    else:
        hc_ref, hs_ref, g_ref, sc_ref, sh_ref, w_ref, o_ref = refs
        y = _rms(_family_rows((hc_ref, hs_ref), TM_IN, axis=1), g_ref[0])
        xn = (y * (1.0 + sc_ref[0]) + sh_ref[0]).astype(BF16)
    o_ref[...] = _dot_nt(xn, w_ref[0]).astype(BF16)


def _inproj(h, mod3, norm_mix, w_in_p, layer):
    normed = not isinstance(h, tuple)
    if normed:
        in_specs = [pl.BlockSpec((TM_IN, D_MODEL), lambda j, i: (i, 0))]
        args = [h]
    else:
        in_specs = _family_specs(TM_IN, D_MODEL, axis=1) + [
            pl.BlockSpec((1, 1, D_MODEL), lambda j, i: (layer, 0, 0)),
            _mod_spec(layer, 1, TM_IN, axis=1),
            _mod_spec(layer, 0, TM_IN, axis=1)]
        args = list(h) + [norm_mix.reshape(DEPTH, 1, D_MODEL), mod3, mod3]
    return pl.pallas_call(
        functools.partial(_inproj_kernel, normed=normed),
        grid=(PROJ_W // TN_IN, R // TM_IN),
        in_specs=in_specs + [pl.BlockSpec((1, TN_IN, D_MODEL), lambda j, i: (layer, j, 0))],
        out_specs=pl.BlockSpec((TM_IN, TN_IN), lambda j, i: (i, j)),
        out_shape=jax.ShapeDtypeStruct((R, PROJ_W), BF16),
        compiler_params=_cp(("arbitrary", "arbitrary")),
        name="inproj",
    )(*args, w_in_p)


def _mla_prep_kernel(pm_ref, tab_ref, qn_ref, kvn_ref, wq_ref, wk_ref, wv_ref,
                     q_ref, k_ref, v_ref, own_ref):
    pm = pm_ref[...].astype(F32)
    tab = tab_ref[...]
    qn = _rms(pm[:, :Q_LORA], qn_ref[0]).astype(BF16)
    q2 = _dot(qn, wq_ref[0])
    hw = MLA_HEADS * HEAD_W
    cq = jnp.tile(tab[:, :HEAD_W], (1, MLA_HEADS))
    sq = jnp.tile(tab[:, HEAD_W:2 * HEAD_W], (1, MLA_HEADS))
    q_ref[...] = (q2[:, :hw] * cq + q2[:, hw:] * sq).astype(BF16)
    ckv = _rms(pm[:, Q_LORA:Q_LORA + KV_LORA], kvn_ref[0])
    kr = pm[:, Q_LORA + KV_LORA:] * tab[:, 2 * HEAD_W:]
    lhs = jnp.concatenate([ckv, kr], axis=1)
    own_ref[...] = lhs
    lb = lhs.astype(BF16)
    k_ref[...] = _dot(lb, wk_ref[0]).astype(BF16)
    v_ref[...] = _dot_nt(wv_ref[0], lb[:, :KV_LORA]).astype(BF16)


def _mla_prep(proj, tab, q_norm, kv_norm, wq_p, wk_p, wv_p, layer):
    tm = TM_PREP
    hw = MLA_HEADS * HEAD_W

    def tab_map(i):
        r0 = i * tm
        return (jnp.where(r0 < NCTX, 0, 1 + ((r0 - NCTX) % DEC_SEQ) // tm), 0)

    return pl.pallas_call(
        _mla_prep_kernel,
        grid=(R // tm,),
        in_specs=[pl.BlockSpec((tm, 512), lambda i: (i, PC_MLA // 512)),
                  pl.BlockSpec((tm, 3 * HEAD_W), tab_map),
                  pl.BlockSpec((1, 1, Q_LORA), lambda i: (layer, 0, 0)),
                  pl.BlockSpec((1, 1, KV_LORA), lambda i: (layer, 0, 0)),
                  pl.BlockSpec((1, Q_LORA, 2 * hw), lambda i: (layer, 0, 0)),
                  pl.BlockSpec((1, 2 * KV_LORA, hw), lambda i: (layer, 0, 0)),
                  pl.BlockSpec((1, VT_ROWS, KV_LORA), lambda i: (layer, 0, 0))],
        out_specs=[pl.BlockSpec((tm, hw), lambda i: (i, 0)),
                   pl.BlockSpec((tm, hw), lambda i: (i, 0)),
                   pl.BlockSpec((VT_ROWS, tm), lambda i: (0, i)),
                   pl.BlockSpec((tm, 2 * KV_LORA), lambda i: (i, 0))],
        out_shape=[jax.ShapeDtypeStruct((R, hw), BF16),
                   jax.ShapeDtypeStruct((R, hw), BF16),
                   jax.ShapeDtypeStruct((VT_ROWS, R), BF16),
                   jax.ShapeDtypeStruct((R, 2 * KV_LORA), F32)],
        compiler_params=_cp(("arbitrary",)),
        name="mla_prep",
    )(proj, tab, q_norm.reshape(DEPTH, 1, Q_LORA), kv_norm.reshape(DEPTH, 1, KV_LORA),
      wq_p, wk_p, wv_p)


def _kv_cache_kernel(c_ref, wk_ref, wv_ref, k_ref, v_ref):
    lb = c_ref[0, 0].astype(BF16)
    k_ref[0, 0] = _dot(lb, wk_ref[0]).astype(BF16)
    v_ref[0, 0] = _dot_nt(wv_ref[0], lb[:, :KV_LORA]).astype(BF16)


def _kv_cache(cache_p, wk_p, wv_p):
    hw = MLA_HEADS * HEAD_W
    return pl.pallas_call(
        _kv_cache_kernel,
        grid=(DEC_BATCH, DEPTH),
        in_specs=[pl.BlockSpec((1, 1, PAST_LEN, 2 * KV_LORA), lambda b, l: (b, l, 0, 0)),
                  pl.BlockSpec((1, 2 * KV_LORA, hw), lambda b, l: (l, 0, 0)),
                  pl.BlockSpec((1, VT_ROWS, KV_LORA), lambda b, l: (l, 0, 0))],
        out_specs=[pl.BlockSpec((1, 1, PAST_LEN, hw), lambda b, l: (b, l, 0, 0)),
                   pl.BlockSpec((1, 1, VT_ROWS, PAST_LEN), lambda b, l: (b, l, 0, 0))],
        out_shape=[jax.ShapeDtypeStruct((DEC_BATCH, DEPTH, PAST_LEN, hw), BF16),
                   jax.ShapeDtypeStruct((DEC_BATCH, DEPTH, VT_ROWS, PAST_LEN), BF16)],
        compiler_params=_cp(("arbitrary", "arbitrary")),
        name="kv_cache",
    )(cache_p, wk_p, wv_p)


def _attn_kernel(*refs, heads, has_cache):
    if has_cache:
        q_ref, k_ref, vt_ref, kc_ref, vct_ref, o_ref = refs
    else:
        q_ref, k_ref, vt_ref, o_ref = refs
    n_keys = k_ref.shape[0]
    tk = min(TK_ATT, n_keys)
    tiles = ([("cache", 0, PAST_LEN)] if has_cache else []) + [("own", s0, tk) for s0 in range(0, n_keys, tk)]
    units = [(h, t) for t in range(len(tiles)) for h in range(heads)]
    qs = [q_ref[:, h * HEAD_W:(h + 1) * HEAD_W] for h in range(heads)]

    def scores(h, t):
        kind, s0, n = tiles[t]
        sl = slice(h * HEAD_W, (h + 1) * HEAD_W)
        keys = kc_ref[0, 0, :, sl] if kind == "cache" else k_ref[s0:s0 + n, sl]
        st = _dot_nt(keys, qs[h])
        return st, jnp.max(st, axis=0, keepdims=True)

    def values_t(h, t):
        kind, s0, n = tiles[t]
        sv = slice(h * MLA_V, (h + 1) * MLA_V)
        vt = vct_ref[0, 0, sv, :] if kind == "cache" else vt_ref[sv, s0:s0 + n]
        return jnp.concatenate([vt, jnp.ones((ONES_ROWS, n), BF16)], axis=0)

    ahead = ATT_AHEAD
    pending = {u: scores(*u) for u in units[:ahead]}
    m, acc = [None] * heads, [None] * heads
    for i, (h, t) in enumerate(units):
        if i + ahead < len(units):
            pending[units[i + ahead]] = scores(*units[i + ahead])
        st, mt = pending.pop((h, t))
        if m[h] is None:
            m[h] = mt
            acc[h] = _dot(values_t(h, t), jnp.exp2(st - mt).astype(BF16))
        else:
            m_new = jnp.maximum(m[h], mt)
            alpha = jnp.exp2(m[h] - m_new)
            acc[h] = alpha * acc[h] + _dot(values_t(h, t), jnp.exp2(st - m_new).astype(BF16))
            m[h] = m_new
    outs = [acc[h][:MLA_V] / acc[h][MLA_V:MLA_V + 1] for h in range(heads)]
    o_ref[...] = jnp.concatenate(outs, axis=0).T.astype(BF16)


def _attention_ctx(qp, kp, vtp):
    hw = MLA_HEADS * HEAD_W
    return pl.pallas_call(
        functools.partial(_attn_kernel, heads=MLA_HEADS, has_cache=False),
        grid=(BATCH,),
        in_specs=[pl.BlockSpec((SEQ, hw), lambda s: (s, 0)),
                  pl.BlockSpec((SEQ, hw), lambda s: (s, 0)),
                  pl.BlockSpec((VT_ROWS, SEQ), lambda s: (0, s))],
        out_specs=pl.BlockSpec((SEQ, MLA_HEADS * MLA_V), lambda s: (s, 0)),
        out_shape=jax.ShapeDtypeStruct((NCTX, MLA_HEADS * MLA_V), BF16),
        compiler_params=_cp(("arbitrary",)),
        name="attn_ctx",
    )(qp, kp, vtp)


def _attention_smp(qp, kp, vtp, kc, vct, layer):
    hpb = 4
    wq = hpb * HEAD_W
    wv = hpb * MLA_V
    nq = DEC_SEQ // TQ_ATT
    qoff = NCTX // TQ_ATT
    koff = NCTX // DEC_SEQ
    return pl.pallas_call(
        functools.partial(_attn_kernel, heads=hpb, has_cache=True),
        grid=(DEC_BATCH, MLA_HEADS // hpb, nq),
        in_specs=[pl.BlockSpec((TQ_ATT, wq), lambda b, g, i: (qoff + b * nq + i, g)),
                  pl.BlockSpec((DEC_SEQ, wq), lambda b, g, i: (koff + b, g)),
                  pl.BlockSpec((wv, DEC_SEQ), lambda b, g, i: (g, koff + b)),
                  pl.BlockSpec((1, 1, PAST_LEN, wq), lambda b, g, i: (b, layer, 0, g)),
                  pl.BlockSpec((1, 1, wv, PAST_LEN), lambda b, g, i: (b, layer, g, 0))],
        out_specs=pl.BlockSpec((TQ_ATT, hpb * MLA_V), lambda b, g, i: (b * nq + i, g)),
        out_shape=jax.ShapeDtypeStruct((NSMP, MLA_HEADS * MLA_V), BF16),
        compiler_params=_cp(("arbitrary", "arbitrary", "arbitrary")),
        name="attn_smp",
    )(qp, kp, vtp, kc, vct)


def _scan_specs(ctx, blk, widths_and_cols, dir_cols=()):
    nseq, seq_len = (BATCH, SEQ) if ctx else (DEC_BATCH, DEC_SEQ)
    nblk = seq_len // blk
    roff = 0 if ctx else NCTX // blk

    def rb(dd):
        return lambda s, j: roff + s * nblk + (j if dd == 0 else nblk - 1 - j)

    specs = []
    for dd in range(2):
        r = rb(dd)
        specs += [pl.BlockSpec((blk, w), lambda s, j, r=r, c=c: (r(s, j), c)) for w, c in widths_and_cols]
        specs += [pl.BlockSpec((blk, w), lambda s, j, r=r, c=c + dd: (r(s, j), c)) for w, c in dir_cols]
    out_maps = [lambda s, j, r=rb(dd): (r(s, j) - roff, 0) for dd in range(2)]
    return nseq, nblk, specs, out_maps


def _gla_kernel(*refs, nblk, has_s0, nch):
    per_dir = [refs[0:4], refs[4:8]]
    wg_ref, bg_ref = refs[8:10]
    if has_s0:
        s0_ref, of_ref, ob_ref, s_scr = refs[10:]
    else:
        of_ref, ob_ref, sfin_ref, s_scr = refs[10:]
    o_refs = [of_ref, ob_ref]
    j = pl.program_id(1)

    @pl.when(j == 0)
    def _():
        if has_s0:
            s_scr[...] = s0_ref[0, 0]
        else:
            s_scr[...] = jnp.zeros_like(s_scr)

    r64 = lax.broadcasted_iota(jnp.int32, (CHUNK, CHUNK), 0)
    c64 = lax.broadcasted_iota(jnp.int32, (CHUNK, CHUNK), 1)
    incl = [c64 <= r64, c64 >= r64]
    tri_b = [jnp.where(m, 1.0, 0.0).astype(BF16) for m in incl]
    heads = [(slice(h * GLA_DK, (h + 1) * GLA_DK), slice(h * GLA_DV, (h + 1) * GLA_DV))
             for h in range(GLA_HEADS)]

    ch = []
    for c in range(nch):
        for dd in range(2):
            ci = c if dd == 0 else nch - 1 - c
            rows = slice(ci * CHUNK, (ci + 1) * CHUNK)
            glr_ref = per_dir[dd][3]
            ch.append(dict(dd=dd, rows=rows, z=_dot(glr_ref[rows, :], wg_ref[0, dd]) + bg_ref[0, dd]))
    for t in ch:
        t["la"] = _split3(_log_sigmoid(t["z"]) * (1.0 / GLA_TAU))
    for t in ch:
        hi, mid, lo = t["la"]
        tb = tri_b[t["dd"]]
        t["bc"] = _dot(tb, hi) + (_dot(tb, mid) + _dot(tb, lo))
    for t in ch:
        q_ref, k_ref, v_ref, _ = per_dir[t["dd"]]
        rows, bc = t["rows"], t["bc"]
        q = q_ref[rows, :].astype(F32) * (GLA_DK ** -0.5)
        k = k_ref[rows, :].astype(F32)
        bl = bc[CHUNK - 1:CHUNK] if t["dd"] == 0 else bc[0:1]
        t["qd"] = (q * jnp.exp(bc)).astype(BF16)
        t["ki"] = (k * jnp.exp(-bc)).astype(BF16)
        t["ke"] = (k * jnp.exp(bl - bc)).astype(BF16)
        t["dec_t"] = jnp.transpose(jnp.broadcast_to(jnp.exp(bl), (LANE, GLA_HEADS * GLA_DK)))
        t["v"] = v_ref[rows, :]
    for t in ch:
        t["a"] = [jnp.where(incl[t["dd"]], _dot_nt(t["qd"][:, sk], t["ki"][:, sk]), 0.0).astype(BF16)
                  for sk, _ in heads]
    for t in ch:
        t["oi"] = [_dot(t["a"][h], t["v"][:, sv]) for h, (_, sv) in enumerate(heads)]
        t["upd"] = [_dot_tn(t["ke"][:, sk], t["v"][:, sv]) for sk, sv in heads]
    states = [[s_scr[dd, h] for h in range(GLA_HEADS)] for dd in range(2)]
    for t in ch:
        st = states[t["dd"]]
        t["s_in"] = [s.astype(BF16) for s in st]
        for h, (sk, _) in enumerate(heads):
            st[h] = t["dec_t"][sk, :] * st[h] + t["upd"][h]
    for t in ch:
        for h, (sk, sv) in enumerate(heads):
            o_refs[t["dd"]][t["rows"], sv] = t["oi"][h] + _dot(t["qd"][:, sk], t["s_in"][h])
    for dd in range(2):
        for h in range(GLA_HEADS):
            s_scr[dd, h] = states[dd][h]

    if not has_s0:
        @pl.when(j == nblk - 1)
        def _():
            sfin_ref[0] = s_scr[...]


def _gla(proj, wg_p, bg_p, s0, layer, ctx):
    blk = GLA_BLK_CTX if ctx else GLA_BLK_SMP
    hk = GLA_HEADS * GLA_DK
    hv = GLA_HEADS * GLA_DV
    nseq, nblk, in_specs, out_maps = _scan_specs(
        ctx, blk, [(hk, PC_GQ // hk), (hk, PC_GK // hk), (hv, PC_GV // hv), (LANE, PC_GLR // LANE)])
    in_specs += [pl.BlockSpec((1, 2, LANE, hk), lambda s, j: (layer, 0, 0, 0)),
                 pl.BlockSpec((1, 2, 1, hk), lambda s, j: (layer, 0, 0, 0))]
    args = [proj] * 8 + [wg_p, bg_p]
    nrows = NCTX if ctx else NSMP
    out_specs = [pl.BlockSpec((blk, hv), m) for m in out_maps]
    out_shape = [jax.ShapeDtypeStruct((nrows, hv), F32)] * 2
    st_blk = (2, GLA_HEADS, GLA_DK, GLA_DV)
    if ctx:
        out_specs.append(pl.BlockSpec((1,) + st_blk, lambda s, j: (s, 0, 0, 0, 0)))
        out_shape.append(jax.ShapeDtypeStruct((BATCH,) + st_blk, F32))
    else:
        in_specs.append(pl.BlockSpec((1, 1) + st_blk, lambda s, j: (s, layer, 0, 0, 0, 0)))
        args.append(s0)
    return pl.pallas_call(
        functools.partial(_gla_kernel, nblk=nblk, has_s0=not ctx, nch=blk // CHUNK),
        grid=(nseq, nblk),
        in_specs=in_specs, out_specs=out_specs, out_shape=out_shape,
        scratch_shapes=[pltpu.VMEM(st_blk, F32)],
        compiler_params=_cp(("arbitrary", "arbitrary")),
        name="gla_ctx" if ctx else "gla_smp",
    )(*args)


GDN_G = 2
GDN_GW = GDN_G * GDN_DK
HALO = 16


def _gdn_pre_kernel(x_ref, xp_ref, xn_ref, cw_ref, o_ref):
    blk = x_ref.shape[0]
    hk = GDN_HEADS * GDN_DK
    sub = SEQ
    cw = 0.5 * cw_ref[0]
    rowi = lax.broadcasted_iota(jnp.int32, (sub, 1), 0)
    lane = lax.broadcasted_iota(jnp.int32, (sub, LANE), 1)
    lo_half = lane < GDN_DK

    def l2n(t, scale):
        parts = []
        for c in range(hk // LANE):
            tc = t[:, c * LANE:(c + 1) * LANE]
            sq = tc * tc
            s_lo = jnp.sum(jnp.where(lo_half, sq, 0.0), axis=-1, keepdims=True)
            s_hi = jnp.sum(jnp.where(lo_half, 0.0, sq), axis=-1, keepdims=True)
            parts.append(tc * (lax.rsqrt(jnp.where(lo_half, s_lo, s_hi) + EPS) * scale))
        return jnp.concatenate(parts, axis=1)

    for s in range(blk // sub):
        rows = slice(s * sub, (s + 1) * sub)
        x = x_ref[rows, :].astype(F32)
        r0 = pl.program_id(0) * blk + s * sub
        pos0 = jnp.where(r0 < NCTX, r0 % SEQ, (r0 - NCTX) % DEC_SEQ)
        seq_len = jnp.where(r0 < NCTX, SEQ, DEC_SEQ)
        before = xp_ref[...] if s == 0 else x_ref[s * sub - HALO:s * sub, :]
        after = xn_ref[...] if s == blk // sub - 1 else x_ref[(s + 1) * sub:(s + 1) * sub + HALO, :]
        prev_row = jnp.where(pos0 > 0, before.astype(F32)[HALO - 1:HALO], 0.0)
        next_row = jnp.where(pos0 + sub < seq_len, after.astype(F32)[0:1], 0.0)
        x_prev = jnp.where(rowi == 0, prev_row, pltpu.roll(x, 1, 0))
        x_next = jnp.where(rowi == sub - 1, next_row, pltpu.roll(x, sub - 1, 0))
        y = _silu_of_twice(x_prev * cw[0:1] + x * cw[1:2] + x_next * cw[2:3])
        o_ref[rows, :hk] = l2n(y[:, :hk], GDN_DK ** -0.5).astype(BF16)
        o_ref[rows, hk:2 * hk] = l2n(y[:, hk:2 * hk], 1.0).astype(BF16)
        o_ref[rows, 2 * hk:] = y[:, 2 * hk:].astype(BF16)


def _gdn_pre(proj, conv_w, layer):
    cw = 2 * GDN_HEADS * GDN_DK + GDN_HEADS * GDN_DV
    blk = GDN_PRE_BLK
    halo = blk // HALO
    last_h = R // HALO - 1
    return pl.pallas_call(
        _gdn_pre_kernel,
        grid=(R // blk,),
        in_specs=[pl.BlockSpec((blk, cw), lambda i: (i, PC_DQKV // cw)),
                  pl.BlockSpec((HALO, cw), lambda i: (jnp.maximum(i * halo - 1, 0), PC_DQKV // cw)),
                  pl.BlockSpec((HALO, cw), lambda i: (jnp.minimum((i + 1) * halo, last_h), PC_DQKV // cw)),
                  pl.BlockSpec((1, 3, cw), lambda i: (layer, 0, 0))],
        out_specs=pl.BlockSpec((blk, cw), lambda i: (i, 0)),
        out_shape=jax.ShapeDtypeStruct((R, cw), BF16),
        compiler_params=_cp(("arbitrary",)),
        name="gdn_pre",
    )(proj, proj, proj, conv_w)


def _gdn_kernel(*refs, nblk, has_s0):
    per_dir = [refs[0:4], refs[4:8]]
    alog_ref, dtb_ref = refs[8:10]
    if has_s0:
        s0_ref, of_ref, ob_ref, s_scr = refs[10:]
    else:
        of_ref, ob_ref, sfin_ref, s_scr = refs[10:]
    o_refs = [of_ref, ob_ref]
    j = pl.program_id(1)
    ngrp = GDN_HEADS // GDN_G

    def head_block(h):
        g, hh = divmod(h, GDN_G)
        return g, slice(hh * GDN_DK, (hh + 1) * GDN_DK), slice(hh * GDN_DV, (hh + 1) * GDN_DV)

    @pl.when(j == 0)
    def _():
        s_scr[...] = jnp.zeros_like(s_scr)
        if has_s0:
            for dd in range(2):
                for h in range(GDN_HEADS):
                    g, rk, rv = head_block(h)
                    s_scr[dd, g, rk, rv] = s0_ref[0, 0, dd, h]

    row = lax.broadcasted_iota(jnp.int32, (CHUNK, GDN_GW), 0)
    col = lax.broadcasted_iota(jnp.int32, (CHUNK, GDN_GW), 1) % CHUNK
    r64 = lax.broadcasted_iota(jnp.int32, (CHUNK, CHUNK), 0)
    c64 = lax.broadcasted_iota(jnp.int32, (CHUNK, CHUNK), 1)
    dir_masks = []
    for dd in range(2):
        lag = (row - col) if dd == 0 else (col - row)
        tri = (c64 <= r64) if dd == 0 else (c64 >= r64)
        dir_masks.append(dict(incl=lag >= 0, strict=lag > 0,
                              tri_b=jnp.where(tri, 1.0, 0.0).astype(BF16)))
    eye = col == row
    eye_f = jnp.where(eye, 1.0, 0.0)
    blk8 = (row // 8) == (col // 8)
    blk16 = (row // 16) == (col // 16)
    blk32 = (row // 32) == (col // 32)
    brow = lax.broadcasted_iota(jnp.int32, (GDN_GW, GDN_GW), 0) // CHUNK
    bcol = lax.broadcasted_iota(jnp.int32, (GDN_GW, GDN_GW), 1) // CHUNK
    same_head = brow == bcol

    def bdiag(t):
        tb = t.astype(BF16)
        return jnp.where(same_head, jnp.concatenate([tb] * GDN_G, axis=0), jnp.zeros((), BF16))

    def bdmm(a, b):
        return _dot(a.astype(BF16), bdiag(b))

    def setup(dd, c):
        q_ref, k_ref, v_ref, ab_ref = per_dir[dd]
        mk = dir_masks[dd]
        ci = c if dd == 0 else NCH - 1 - c
        rows = slice(ci * CHUNK, (ci + 1) * CHUNK)
        ab = ab_ref[rows, :].astype(F32)
        neg_a = -jnp.exp(alog_ref[0, dd])
        gam = _cumsum_rows(mk["tri_b"], neg_a * _softplus(ab + dtb_ref[0, dd]))
        beta_all = _sigmoid(ab)
        out = []
        for g in range(ngrp):
            gl = slice(g * GDN_GW, (g + 1) * GDN_GW)
            gcol = jnp.concatenate(
                [jnp.broadcast_to(gam[:, g * GDN_G + h:g * GDN_G + h + 1], (CHUNK, GDN_DK))
                 for h in range(GDN_G)], axis=1)
            bcol_ = jnp.concatenate(
                [jnp.broadcast_to(beta_all[:, GDN_HEADS + g * GDN_G + h:GDN_HEADS + g * GDN_G + h + 1],
                                  (CHUNK, GDN_DK)) for h in range(GDN_G)], axis=1)
            grow = jnp.sum(jnp.where(eye, gcol, 0.0), axis=0, keepdims=True)
            glast = gcol[CHUNK - 1:CHUNK] if dd == 0 else gcol[0:1]
            decay = jnp.where(mk["incl"], jnp.exp(gcol - grow), 0.0)
            egc = jnp.exp(gcol)
            qg = q_ref[rows, gl].astype(F32)
            kg = k_ref[rows, gl].astype(F32)
            vg = v_ref[rows, gl].astype(F32)
            kq = _dot_nt(jnp.concatenate([k_ref[rows, gl], q_ref[rows, gl]], axis=0),
                         bdiag(k_ref[rows, gl]))
            out.append(dict(dd=dd, g=g, rows=rows, gl=gl,
                            m=jnp.where(mk["strict"], bcol_ * kq[:CHUNK] * decay, 0.0),
                            aqk=(kq[CHUNK:] * decay).astype(BF16),
                            vb=vg * bcol_, kb=kg * (bcol_ * egc), qe=qg * egc,
                            kend=(kg * jnp.exp(glast - gcol)).astype(BF16),
                            eg=jnp.exp(glast)))
        return out

    def stages(ch):
        pairs = list(zip(ch[0::2], ch[1::2]))

        def bdmm_pairs(lhs, rhs, out):
            for ta, tb in pairs:
                a = jnp.concatenate([ta[lhs].astype(BF16), tb[lhs].astype(BF16)], axis=0)
                w = jnp.concatenate([bdiag(ta[rhs]), bdiag(tb[rhs])], axis=1)
                y = _dot(a, w)
                ta[out] = y[:CHUNK, :GDN_GW]
                tb[out] = y[CHUNK:, GDN_GW:]

        def neumann_a():
            for t in ch:
                t["n8"] = jnp.where(blk8, t["m"], 0.0)
            bdmm_pairs("n8", "n8", "n2")

        def neumann_b():
            for t in ch:
                t["p1"] = eye_f - t["n8"]
            bdmm_pairs("n2", "n2", "n4")
            bdmm_pairs("p1", "n2", "p1n2")
            for t in ch:
                t["p1"] = t["p1"] + t["p1n2"]

        def neumann_c():
            bdmm_pairs("p1", "n4", "p1n4")
            for t in ch:
                t["dinv"] = t["p1"] + t["p1n4"]

        def double_a(off):
            def run():
                for t in ch:
                    t["off"] = jnp.where(off, t["m"], 0.0)
                bdmm_pairs("dinv", "off", "dl")
            return run

        def double_b():
            bdmm_pairs("dl", "dinv", "dld")
            for t in ch:
                t["dinv"] = t["dinv"] - t["dld"]

        def solve():
            for t in ch:
                uw = _dot(t["dinv"].astype(BF16), jnp.concatenate([bdiag(t["vb"]), bdiag(t["kb"])], axis=1))
                t["u"] = uw[:, :GDN_GW]
                t["w"] = uw[:, GDN_GW:]

        def fold():
            for t in ch:
                wu = jnp.concatenate([t["w"], t["u"]], axis=1).astype(BF16)
                cb = _dot_tn(t["kend"], wu)
                t["c"] = jnp.where(same_head, cb[:, :GDN_GW], 0.0).astype(BF16)
                t["b"] = jnp.where(same_head, cb[:, GDN_GW:], 0.0)
                ao = _dot(t["aqk"], jnp.concatenate([bdiag(t["w"]), bdiag(t["u"])], axis=1))
                t["qt"] = (t["qe"] - ao[:, :GDN_GW]).astype(BF16)
                t["o"] = ao[:, GDN_GW:]

        out = [neumann_a, neumann_b, neumann_c]
        for inner, outer in ((blk8, blk16), (blk16, blk32), (blk32, None)):
            off = jnp.logical_not(inner) if outer is None else jnp.logical_and(outer, jnp.logical_not(inner))
            out += [double_a(off), double_b]
        return out + [solve, fold]

    states = [[s_scr[dd, g] for g in range(ngrp)] for dd in range(2)]

    def recur(cur):
        ys = [_dot(jnp.concatenate([t["qt"], t["c"]], axis=0), states[t["dd"]][t["g"]].astype(BF16))
              for t in cur]
        for t, y in zip(cur, ys):
            o_refs[t["dd"]][t["rows"], t["gl"]] = y[:CHUNK] + t["o"]
            states[t["dd"]][t["g"]] = t["eg"] * states[t["dd"]][t["g"]] - y[CHUNK:] + t["b"]

    half = NCH // 2
    chains = {}
    for c in range(half):
        for dd in range(2):
            chains[(dd, c)] = setup(dd, c)
    later = [(dd, c) for c in range(half, NCH) for dd in range(2)]
    wave_a = [t for c in range(half) for dd in range(2) for t in chains[(dd, c)]]
    for k, stage in enumerate(stages(wave_a)):
        stage()
        if k % 2 == 0 and later:
            key = later.pop(0)
            chains[key] = setup(*key)
    for key in later:
        chains[key] = setup(*key)
    wave_b = [t for c in range(half, NCH) for dd in range(2) for t in chains[(dd, c)]]
    done = 0
    for k, stage in enumerate(stages(wave_b)):
        stage()
        if k % 3 == 1 and done < half:
            recur(chains[(0, done)] + chains[(1, done)])
            done += 1
    for c in range(done, NCH):
        recur(chains[(0, c)] + chains[(1, c)])
    for dd in range(2):
        for g in range(ngrp):
            s_scr[dd, g] = states[dd][g]

    if not has_s0:
        @pl.when(j == nblk - 1)
        def _():
            for dd in range(2):
                for h in range(GDN_HEADS):
                    g, rk, rv = head_block(h)
                    sfin_ref[0, dd, h] = s_scr[dd, g, rk, rv]


def _gdn(qkv, proj, alog_p, dtb_p, s0, layer, ctx):
    nseq, nblk = (BATCH, SEQ // SEQ_BLK) if ctx else (DEC_BATCH, DEC_SEQ // SEQ_BLK)
    roff = 0 if ctx else NCTX // SEQ_BLK
    hk = GDN_HEADS * GDN_DK
    hv = GDN_HEADS * GDN_DV

    def rb(dd):
        return lambda s, j: roff + s * nblk + (j if dd == 0 else nblk - 1 - j)

    in_specs, args = [], []
    for dd in range(2):
        r = rb(dd)
        in_specs += [pl.BlockSpec((SEQ_BLK, hk), lambda s, j, r=r: (r(s, j), 0)),
                     pl.BlockSpec((SEQ_BLK, hk), lambda s, j, r=r: (r(s, j), 1)),
                     pl.BlockSpec((SEQ_BLK, hv), lambda s, j, r=r: (r(s, j), 2 * hk // hv)),
                     pl.BlockSpec((SEQ_BLK, LANE), lambda s, j, r=r, dd=dd: (r(s, j), PC_DAB // LANE + dd))]
        args += [qkv, qkv, qkv, proj]
    in_specs += [pl.BlockSpec((1, 2, 1, LANE), lambda s, j: (layer, 0, 0, 0))] * 2
    args += [alog_p, dtb_p]
    nrows = NCTX if ctx else NSMP
    out_specs = [pl.BlockSpec((SEQ_BLK, hv), lambda s, j, r=rb(dd): (r(s, j) - roff, 0)) for dd in range(2)]
    out_shape = [jax.ShapeDtypeStruct((nrows, hv), F32)] * 2
    st_blk = (2, GDN_HEADS, GDN_DK, GDN_DV)
    if ctx:
        out_specs.append(pl.BlockSpec((1,) + st_blk, lambda s, j: (s, 0, 0, 0, 0)))
        out_shape.append(jax.ShapeDtypeStruct((BATCH,) + st_blk, F32))
    else:
        in_specs.append(pl.BlockSpec((1, 1) + st_blk, lambda s, j: (s, layer, 0, 0, 0, 0)))
        args.append(s0)
    return pl.pallas_call(
        functools.partial(_gdn_kernel, nblk=nblk, has_s0=not ctx),
        grid=(nseq, nblk),
        in_specs=in_specs, out_specs=out_specs, out_shape=out_shape,
        scratch_shapes=[pltpu.VMEM((2, GDN_HEADS // GDN_G, GDN_GW, GDN_GW), F32)],
        compiler_params=_cp(("arbitrary", "arbitrary")),
        name="gdn_ctx" if ctx else "gdn_smp",
    )(*args)


def _group_rms(x, width):
    parts = []
    lane = lax.broadcasted_iota(jnp.int32, (x.shape[0], LANE), 1)
    lo_half = lane < 64
    for c in range(x.shape[1] // LANE):
        xc = x[:, c * LANE:(c + 1) * LANE]
        sq = xc * xc
        if width == LANE:
            ms = jnp.mean(sq, axis=-1, keepdims=True)
        else:
            s_lo = jnp.sum(jnp.where(lo_half, sq, 0.0), axis=-1, keepdims=True)
            s_hi = jnp.sum(jnp.where(lo_half, 0.0, sq), axis=-1, keepdims=True)
            ms = jnp.where(lo_half, s_lo, s_hi) * (1.0 / width)
        parts.append(xc * lax.rsqrt(ms + EPS))
    return jnp.concatenate(parts, axis=1)


def _merge_kernel(ymc_ref, yms_ref, ogcf_ref, ogcb_ref, ogsf_ref, ogsb_ref, gr_ref,
                  odcf_ref, odcb_ref, odsf_ref, odsb_ref,
                  dz_ref, gt_ref, *rest, split_h):
    if split_h:
        hc_ref, hs_ref, *rest = rest
        h_rows = lambda: _family_rows((hc_ref, hs_ref), TM_MERGE)
    else:
        h_ref, *rest = rest
        h_rows = lambda: h_ref[...]
    ga_ref, gn_ref, dn_ref, wb_ref, wo_ref, o_ref, ym_scr, og_scr, od_scr = rest
    is_ctx = pl.program_id(0) < NCTX // TM_MERGE

    @pl.when(is_ctx)
    def _():
        ym_scr[...] = ymc_ref[...]
        og_scr[...] = ogcf_ref[...] + ogcb_ref[...]
        od_scr[...] = odcf_ref[...] + odcb_ref[...]

    @pl.when(jnp.logical_not(is_ctx))
    def _():
        ym_scr[...] = yms_ref[...]
        og_scr[...] = ogsf_ref[...] + ogsb_ref[...]
        od_scr[...] = odsf_ref[...] + odsb_ref[...]

    y_gla = _group_rms(og_scr[...], GLA_DV) * gn_ref[0] * _silu_of_twice(gr_ref[...].astype(F32))
    y_gdn = _group_rms(od_scr[...], GDN_DV) * dn_ref[0] * _silu_of_twice(dz_ref[...].astype(F32))
    t = jnp.tanh(gt_ref[...].astype(F32))
    d = D_MODEL
    acc = None
    for n, y_n in enumerate((ym_scr[...], y_gla.astype(BF16), y_gdn.astype(BF16))):
        p = _dot(y_n, wb_ref[0, n])
        gated = p + p * t[:, n * d:(n + 1) * d]
        acc = gated if acc is None else acc + gated
    y = _dot(acc.astype(BF16), wo_ref[0])
    o_ref[...] = h_rows() + ga_ref[0] * y


def _merge(y_mla, o_gla, o_gdn, proj, h, mod3, gla_norm_p, gdn_norm_p, w_branch_b, w_out_b, layer):
    tm = TM_MERGE
    bw = 512
    nc = NCTX // tm
    split_h = isinstance(h, tuple)
    h_specs = _family_specs(tm, D_MODEL) if split_h else [pl.BlockSpec((tm, D_MODEL), lambda i: (i, 0))]
    h_args = list(h) if split_h else [h]

    def cmap(i):
        return jnp.minimum(i, nc - 1)

    def smap(i):
        return jnp.maximum(i - nc, 0)

    return pl.pallas_call(
        functools.partial(_merge_kernel, split_h=split_h),
        grid=(R // tm,),
        in_specs=[pl.BlockSpec((tm, bw), lambda i: (cmap(i), 0)),
                  pl.BlockSpec((tm, bw), lambda i: (smap(i), 0)),
                  pl.BlockSpec((tm, bw), lambda i: (cmap(i), 0)),
                  pl.BlockSpec((tm, bw), lambda i: (cmap(i), 0)),
                  pl.BlockSpec((tm, bw), lambda i: (smap(i), 0)),
                  pl.BlockSpec((tm, bw), lambda i: (smap(i), 0)),
                  pl.BlockSpec((tm, bw), lambda i: (i, PC_GR // bw)),
                  pl.BlockSpec((tm, bw), lambda i: (cmap(i), 0)),
                  pl.BlockSpec((tm, bw), lambda i: (cmap(i), 0)),
                  pl.BlockSpec((tm, bw), lambda i: (smap(i), 0)),
                  pl.BlockSpec((tm, bw), lambda i: (smap(i), 0)),
                  pl.BlockSpec((tm, bw), lambda i: (i, PC_DZ // bw)),
                  pl.BlockSpec((tm, 3 * D_MODEL), lambda i: (i, PC_GATES // (3 * D_MODEL)))] + h_specs + [
                  _mod_spec(layer, 2, tm),
                  pl.BlockSpec((1, 1, bw), lambda i: (layer, 0, 0)),
                  pl.BlockSpec((1, 1, bw), lambda i: (layer, 0, 0)),
                  pl.BlockSpec((1, 3, bw, D_MODEL), lambda i: (layer, 0, 0, 0), pipeline_mode=pl.Buffered(1)),
                  pl.BlockSpec((1, D_MODEL, D_MODEL), lambda i: (layer, 0, 0), pipeline_mode=pl.Buffered(1))],
        out_specs=pl.BlockSpec((tm, D_MODEL), lambda i: (i, 0)),
        out_shape=jax.ShapeDtypeStruct((R, D_MODEL), F32),
        scratch_shapes=[pltpu.VMEM((tm, bw), BF16), pltpu.VMEM((tm, bw), F32), pltpu.VMEM((tm, bw), F32)],
        compiler_params=_cp(("arbitrary",)),
        name="merge",
    )(y_mla[0], y_mla[1], o_gla[0][0], o_gla[0][1], o_gla[1][0], o_gla[1][1], proj,
      o_gdn[0][0], o_gdn[0][1], o_gdn[1][0], o_gdn[1][1],
      proj, proj, *h_args, mod3, gla_norm_p, gdn_norm_p, w_branch_b, w_out_b)


def _ffn_kernel(h_ref, g_ref, sc_ref, sh_ref, gf_ref, wi_ref, wo_ref, *rest, final):
    if final:
        fn_ref, *o_refs = rest
    else:
        gn_ref, scn_ref, shn_ref, *o_refs = rest
    h = h_ref[...]
    xf = (_rms(h, g_ref[0]) * (1.0 + sc_ref[0]) + sh_ref[0]).astype(BF16)

    def gate_up(c):
        f0 = c * TF_FFN
        return (_dot(xf, wi_ref[0, :, f0:f0 + TF_FFN]),
                _dot(xf, wi_ref[0, :, D_FF + f0:D_FF + f0 + TF_FFN]))

    n_chunks = D_FF // TF_FFN
    nxt = gate_up(0)
    acc = None
    for c in range(n_chunks):
        g, u = nxt
        if c + 1 < n_chunks:
            nxt = gate_up(c + 1)
        part = _dot((_silu(g) * u).astype(BF16), wo_ref[0, c * TF_FFN:(c + 1) * TF_FFN, :])
        acc = part if acc is None else acc + part
    out = h + gf_ref[0] * acc
    if not final:
        o_refs[0][...] = out
        o_refs[1][...] = (_rms(out, gn_ref[0]) * (1.0 + scn_ref[0]) + shn_ref[0]).astype(BF16)
    else:
        out = _rms(out, fn_ref[...])
        is_ctx = pl.program_id(0) < NCTX // TM_FFN

        @pl.when(is_ctx)
        def _():
            o_refs[0][...] = out

        @pl.when(jnp.logical_not(is_ctx))
        def _():
            o_refs[1][...] = out


def _ffn(h, mod3, norm_ffn, wi_b, wo_b, tail, layer, final):
    tm = TM_FFN
    if final:
        nc = NCTX // tm
        out_specs = [pl.BlockSpec((tm, D_MODEL), lambda i: (jnp.minimum(i, nc - 1), 0)),
                     pl.BlockSpec((tm, D_MODEL), lambda i: (jnp.maximum(i - nc, 0), 0))]
        out_shape = [jax.ShapeDtypeStruct((NCTX, D_MODEL), F32), jax.ShapeDtypeStruct((NSMP, D_MODEL), F32)]
        tail_specs = [pl.BlockSpec((1, D_MODEL), lambda i: (0, 0))]
        tail_args = [tail.reshape(1, D_MODEL)]
    else:
        out_specs = [pl.BlockSpec((tm, D_MODEL), lambda i: (i, 0))] * 2
        out_shape = [jax.ShapeDtypeStruct((R, D_MODEL), F32), jax.ShapeDtypeStruct((R, D_MODEL), BF16)]
        tail_specs = [pl.BlockSpec((1, 1, D_MODEL), lambda i: (layer + 1, 0, 0)),
                      _mod_spec(layer + 1, 1, tm), _mod_spec(layer + 1, 0, tm)]
        tail_args = [tail.reshape(DEPTH, 1, D_MODEL), mod3, mod3]
    return pl.pallas_call(
        functools.partial(_ffn_kernel, final=final),
        grid=(R // tm,),
        in_specs=[pl.BlockSpec((tm, D_MODEL), lambda i: (i, 0)),
                  pl.BlockSpec((1, 1, D_MODEL), lambda i: (layer, 0, 0)),
                  _mod_spec(layer, 4, tm),
                  _mod_spec(layer, 3, tm),
                  _mod_spec(layer, 5, tm),
                  pl.BlockSpec((1, D_MODEL, 2 * D_FF), lambda i: (layer, 0, 0), pipeline_mode=pl.Buffered(1)),
                  pl.BlockSpec((1, D_FF, D_MODEL), lambda i: (layer, 0, 0), pipeline_mode=pl.Buffered(1))]
        + tail_specs,
        out_specs=out_specs, out_shape=out_shape,
        compiler_params=_cp(("arbitrary",)),
        name="ffn_final" if final else "ffn",
    )(h, norm_ffn.reshape(DEPTH, 1, D_MODEL), mod3, mod3, mod3, wi_b, wo_b, *tail_args)


def _rope_partner(w, axis=-1):
    h = AXIS_DIM // 2
    axis = axis % w.ndim
    parts = []
    for a in range(2):
        x1 = lax.slice_in_dim(w, a * AXIS_DIM, a * AXIS_DIM + h, axis=axis)
        x2 = lax.slice_in_dim(w, a * AXIS_DIM + h, (a + 1) * AXIS_DIM, axis=axis)
        parts += [-x2, x1]
    return jnp.concatenate(parts, axis=axis)


def _pack_w_in_kernel(w_ref, o_ref):
    o_ref[0] = _pack_channels(w_ref[0]).astype(BF16)


def _pack_w_in(w_in):
    tc = 256
    return pl.pallas_call(
        _pack_w_in_kernel,
        grid=(DEPTH, D_MODEL // tc),
        in_specs=[pl.BlockSpec((1, w_in.shape[-1], tc), lambda l, i: (l, 0, i))],
        out_specs=pl.BlockSpec((1, PROJ_W, tc), lambda l, i: (l, 0, i)),
        out_shape=jax.ShapeDtypeStruct((DEPTH, PROJ_W, D_MODEL), BF16),
        compiler_params=_cp(("arbitrary", "arbitrary")),
        name="pack_w_in",
    )(jnp.swapaxes(w_in, 1, 2))


def _pack_channels(wt):
    c = [0]
    for s in (Q_LORA, KV_LORA + ROPE_DIM, 256, 256, 512, 512, 2 * GLA_RANK, 1536, 512,
              2 * GDN_HEADS, 2 * GDN_HEADS, 3 * D_MODEL):
        c.append(c[-1] + s)
    mq, mkv, gq, gk, gv, gr, glr, dqkv, dz, da, db, gates = [wt[c[i]:c[i + 1]] for i in range(12)]

    def zeros(n):
        return jnp.zeros((n,) + wt.shape[1:], wt.dtype)

    kr = mkv[KV_LORA:]
    mla = [mq, mkv[:KV_LORA], kr, _rope_partner(kr, axis=0), zeros(64)]
    dab = []
    for d in range(2):
        dab += [da[d * GDN_HEADS:(d + 1) * GDN_HEADS], db[d * GDN_HEADS:(d + 1) * GDN_HEADS],
                zeros(LANE - 2 * GDN_HEADS)]
    return jnp.concatenate([0.5 * gates, dqkv, 0.5 * dz, gv, 0.5 * gr] + mla
                           + [gq, gk, glr, zeros(LANE - 2 * GLA_RANK)]
                           + dab + [zeros(PROJ_W - PC_DAB - 2 * LANE)], axis=0)


def _pack_mla_weights(w_uq, w_ukv):
    l = w_uq.shape[0]
    qh = w_uq.reshape(l, Q_LORA, MLA_HEADS, MLA_NOPE + ROPE_DIM)
    zq = jnp.zeros((l, Q_LORA, MLA_HEADS, HEAD_W - MLA_NOPE - ROPE_DIM), w_uq.dtype)
    zn = jnp.zeros((l, Q_LORA, MLA_HEADS, MLA_NOPE), w_uq.dtype)
    wq = jnp.concatenate([qh, zq], axis=-1).reshape(l, Q_LORA, MLA_HEADS * HEAD_W)
    wq_sw = jnp.concatenate([zn, _rope_partner(qh[..., MLA_NOPE:]), zq], axis=-1).reshape(l, Q_LORA, MLA_HEADS * HEAD_W)
    wq_p = jnp.concatenate([wq, wq_sw], axis=-1).astype(BF16)

    kvh = w_ukv.reshape(l, KV_LORA, MLA_HEADS, MLA_NOPE + MLA_V)
    zk = jnp.zeros((l, KV_LORA, MLA_HEADS, HEAD_W - MLA_NOPE), w_ukv.dtype)
    wk_top = jnp.concatenate([kvh[..., :MLA_NOPE], zk], axis=-1).reshape(l, KV_LORA, MLA_HEADS * HEAD_W)
    place = np.zeros((KV_LORA, MLA_HEADS, HEAD_W), np.float32)
    idx = np.arange(ROPE_DIM)
    for rep in range(2):
        place[rep * ROPE_DIM + idx, :, MLA_NOPE + idx] = 1.0
    wk_bot = jnp.broadcast_to(jnp.asarray(place.reshape(1, KV_LORA, MLA_HEADS * HEAD_W)),
                              (l, KV_LORA, MLA_HEADS * HEAD_W))
    wk_p = jnp.concatenate([wk_top, wk_bot], axis=1).astype(BF16)
    wv_p = jnp.transpose(kvh[..., MLA_NOPE:], (0, 2, 3, 1)).reshape(l, VT_ROWS, KV_LORA).astype(BF16)
    return wq_p, wk_p, wv_p


def _rope_tables(tm):
    f32 = np.float32
    rows = DEC_SEQ // GRID_W
    row = np.repeat(np.arange(rows, dtype=np.float64), GRID_W)
    col = np.tile(np.arange(GRID_W, dtype=np.float64), rows)
    inv = ROPE_THETA ** (-np.arange(0, AXIS_DIM, 2, dtype=np.float64) / AXIS_DIM)
    ang_r, ang_c = row[:, None] * inv, col[:, None] * inv
    cos32 = np.concatenate([np.cos(ang_r)] * 2 + [np.cos(ang_c)] * 2, axis=-1)
    sin32 = np.concatenate([np.sin(ang_r)] * 2 + [np.sin(ang_c)] * 2, axis=-1)
    cos32 = np.concatenate([np.ones((tm, ROPE_DIM)), cos32], axis=0)
    sin32 = np.concatenate([np.zeros((tm, ROPE_DIM)), sin32], axis=0)
    n = cos32.shape[0]
    scale = (MLA_NOPE + ROPE_DIM) ** -0.5 * math.log2(math.e)
    pad = HEAD_W - MLA_NOPE - ROPE_DIM
    cq = np.concatenate([np.ones((n, MLA_NOPE)), cos32, np.zeros((n, pad))], axis=-1) * scale
    sq = np.concatenate([np.zeros((n, MLA_NOPE)), sin32, np.zeros((n, pad))], axis=-1) * scale
    ck = np.concatenate([cos32, sin32, np.zeros((n, HEAD_W - 2 * ROPE_DIM))], axis=-1)
    return jnp.asarray(np.concatenate([cq, sq, ck], axis=-1).astype(f32))


def _pad_lanes(x, n):
    return jnp.pad(x, [(0, 0)] * (x.ndim - 1) + [(0, n - x.shape[-1])])


def kernel(x_prompt, x_sample, cache_mla, state_gla, state_gdn, c, c_ctx, w_mod, b_mod, norm_mix, w_in,
           mla_q_norm, mla_w_uq, mla_kv_norm, mla_w_ukv, gla_w_gate, gla_b_gate, gla_norm, gdn_conv,
           gdn_a_log, gdn_dt_bias, gdn_norm, w_branch, w_out, norm_ffn, ffn_w_in, ffn_w_out, final_norm):
    w_in_p = _pack_w_in(w_in)
    wq_p, wk_p, wv_p = _pack_mla_weights(mla_w_uq, mla_w_ukv)
    tab = _rope_tables(TM_PREP)
    wg_p = jnp.zeros((DEPTH, 2, LANE, GLA_HEADS * GLA_DK), F32)
    for d in range(2):
        wg_p = wg_p.at[:, d, d * GLA_RANK:(d + 1) * GLA_RANK, :].set(gla_w_gate[:, d])
    wg_p = wg_p.astype(BF16)
    bg_p = gla_b_gate.reshape(DEPTH, 2, 1, GLA_HEADS * GLA_DK)
    alog_p = _pad_lanes(gdn_a_log, LANE).reshape(DEPTH, 2, 1, LANE)
    dtb_p = _pad_lanes(gdn_dt_bias, LANE).reshape(DEPTH, 2, 1, LANE)
    gla_norm_p = jnp.tile(gla_norm, (1, GLA_HEADS)).reshape(DEPTH, 1, GLA_HEADS * GLA_DV)
    gdn_norm_p = jnp.tile(gdn_norm, (1, GDN_HEADS)).reshape(DEPTH, 1, GDN_HEADS * GDN_DV)
    w_branch_b = (0.5 * w_branch).astype(BF16)
    w_out_b = w_out.astype(BF16)
    wi_b = ffn_w_in.astype(BF16)
    wo_b = ffn_w_out.astype(BF16)
    cache_p = _pad_lanes(cache_mla, 2 * KV_LORA)
    cond8 = jnp.concatenate([c_ctx[None, :], c, jnp.zeros((MOD_ROWS - 1 - DEC_BATCH, D_MODEL), F32)], axis=0)

    mod = _modulation(cond8, w_mod, b_mod)
    mod3 = mod.reshape(DEPTH * MOD_ROWS * 6, 1, D_MODEL)
    kc, vc = _kv_cache(cache_p, wk_p, wv_p)

    h = (x_prompt.reshape(NCTX, D_MODEL), x_sample.reshape(NSMP, D_MODEL))
    kv_list, gla_list, gdn_list = [], [], []
    for l in range(DEPTH):
        proj = _inproj(h if l == 0 else xn, mod3, norm_mix, w_in_p, l)
        qp, kp, vtp, own = _mla_prep(proj, tab, mla_q_norm, mla_kv_norm, wq_p, wk_p, wv_p, l)
        y_mla = (_attention_ctx(qp, kp, vtp), _attention_smp(qp, kp, vtp, kc, vc, l))
        ogf_c, ogb_c, sg = _gla(proj, wg_p, bg_p, None, l, True)
        ogf_s, ogb_s = _gla(proj, wg_p, bg_p, state_gla, l, False)
        qkv = _gdn_pre(proj, gdn_conv, l)
        odf_c, odb_c, sd = _gdn(qkv, proj, alog_p, dtb_p, None, l, True)
        odf_s, odb_s = _gdn(qkv, proj, alog_p, dtb_p, state_gdn, l, False)
        h = _merge(y_mla, ((ogf_c, ogb_c), (ogf_s, ogb_s)), ((odf_c, odb_c), (odf_s, odb_s)), proj, h, mod3,
                   gla_norm_p, gdn_norm_p,
                   w_branch_b, w_out_b, l)
        if l == DEPTH - 1:
            h = _ffn(h, mod3, norm_ffn, wi_b, wo_b, final_norm, l, True)
        else:
            h, xn = _ffn(h, mod3, norm_ffn, wi_b, wo_b, norm_mix, l, False)
        kv_list.append(own[:NCTX, :KV_LORA + ROPE_DIM].reshape(BATCH, SEQ, KV_LORA + ROPE_DIM))
        gla_list.append(sg)
        gdn_list.append(sd)

    y_prompt = h[0].reshape(BATCH, SEQ, D_MODEL)
    y_sample = h[1].reshape(DEC_BATCH, DEC_SEQ, D_MODEL)
    return (y_prompt, y_sample, jnp.stack(kv_list, axis=1), jnp.stack(gla_list, axis=1),
            jnp.stack(gdn_list, axis=1))
```

```python
import functools
import math

import numpy as np
import jax
import jax.numpy as jnp
from jax import lax
from jax.experimental import pallas as pl
from jax.experimental.pallas import tpu as pltpu

F32 = jnp.float32
BF16 = jnp.bfloat16

D_MODEL = 1024
BATCH = 16
SEQ = 256
DEPTH = 2
DEC_BATCH = 2
DEC_SEQ = 4096
PAST_LEN = 256
GRID_W = 64
CHUNK = 64
EPS = 1e-6
MLA_HEADS = 8
MLA_NOPE = 64
ROPE_DIM = 32
AXIS_DIM = ROPE_DIM // 2
MLA_V = 64
Q_LORA = 256
KV_LORA = 128
ROPE_THETA = 10000.0
GLA_HEADS = 4
GLA_DK = 64
GLA_DV = 128
GLA_RANK = 16
GLA_TAU = 16.0
GDN_HEADS = 8
GDN_DK = 64
GDN_DV = 64
D_FF = ((8 * D_MODEL + 3 * 256 - 1) // (3 * 256)) * 256
MOD_W = 6 * D_MODEL

NCTX = BATCH * SEQ
NSMP = DEC_BATCH * DEC_SEQ
R = NCTX + NSMP
LANE = 128
HEAD_W = 128
VT_ROWS = MLA_HEADS * MLA_V
SEQ_BLK = 256
NCH = SEQ_BLK // CHUNK
GLA_BLK_CTX = 256
GLA_BLK_SMP = 512
MOD_ROWS = 8

PC_GATES = 0
PC_DQKV = 3072
PC_DZ = 4608
PC_GV = 5120
PC_GR = 5632
PC_MLA = 6144
PC_GQ = 6656
PC_GK = 6912
PC_GLR = 7168
PC_DAB = 7296
PROJ_W = 7680

TM_IN = 1024
TN_IN = 2560
TM_PREP = 1024
GDN_PRE_BLK = 1024
TM_MERGE = 512
TM_FFN = 512
TF_FFN = 256
TQ_ATT = 256
TK_ATT = 256
ONES_ROWS = 16
ATT_AHEAD = 8
VMEM_LIMIT = 48 * 1024 * 1024


def _cp(sem):
    return pltpu.CompilerParams(dimension_semantics=sem, vmem_limit_bytes=VMEM_LIMIT)


def _dot(a, b):
    return jnp.dot(a, b, preferred_element_type=F32)


def _dot_nt(a, b):
    return lax.dot_general(a, b, (((1,), (1,)), ((), ())), preferred_element_type=F32)


def _dot_tn(a, b):
    return lax.dot_general(a, b, (((0,), (0,)), ((), ())), preferred_element_type=F32)


def _split2(x):
    hi = x.astype(BF16)
    lo = (x - hi.astype(F32)).astype(BF16)
    return hi, lo


def _split3(x):
    hi = x.astype(BF16)
    r1 = x - hi.astype(F32)
    mid = r1.astype(BF16)
    lo = (r1 - mid.astype(F32)).astype(BF16)
    return hi, mid, lo


def _cumsum_rows(tri_b, x):
    hi, mid, lo = _split3(x)
    return _dot(tri_b, hi) + (_dot(tri_b, mid) + _dot(tri_b, lo))


def _sigmoid(x):
    return 0.5 * jnp.tanh(0.5 * x) + 0.5


def _silu(x):
    return x * _sigmoid(x)


def _silu_of_twice(hx):
    return hx + hx * jnp.tanh(hx)


def _softplus(x):
    return jnp.maximum(x, 0.0) + jnp.log(1.0 + jnp.exp(-jnp.abs(x)))


def _log_sigmoid(x):
    return jnp.minimum(x, 0.0) - jnp.log(1.0 + jnp.exp(-jnp.abs(x)))


def _rms(x, g):
    return x * lax.rsqrt(jnp.mean(x * x, axis=-1, keepdims=True) + EPS) * g


def _mod_row(row_start):
    return jnp.where(row_start < NCTX, 0, 1 + (row_start - NCTX) // DEC_SEQ)


def _mod_spec(layer, which, tm, axis=0):
    def imap(*idx):
        return ((layer * MOD_ROWS + _mod_row(idx[axis] * tm)) * 6 + which, 0, 0)
    return pl.BlockSpec((1, 1, D_MODEL), imap)


def _mod_kernel(c_ref, w_ref, b_ref, o_ref):
    x = _silu(c_ref[...])
    xh, xl = _split2(x)
    wh, wl = _split2(w_ref[0])
    o_ref[0] = _dot(xh, wh) + (_dot(xl, wh) + _dot(xh, wl)) + b_ref[0]


def _modulation(cond8, w_mod, b_mod):
    tn = 1536
    return pl.pallas_call(
        _mod_kernel,
        grid=(DEPTH, MOD_W // tn),
        in_specs=[pl.BlockSpec((MOD_ROWS, D_MODEL), lambda l, j: (0, 0)),
                  pl.BlockSpec((1, D_MODEL, tn), lambda l, j: (l, 0, j)),
                  pl.BlockSpec((1, 1, tn), lambda l, j: (l, 0, j))],
        out_specs=pl.BlockSpec((1, MOD_ROWS, tn), lambda l, j: (l, 0, j)),
        out_shape=jax.ShapeDtypeStruct((DEPTH, MOD_ROWS, MOD_W), F32),
        compiler_params=_cp(("arbitrary", "arbitrary")),
        name="modulation",
    )(cond8, w_mod, b_mod.reshape(DEPTH, 1, MOD_W))


def _family_specs(tm, width, axis=0):
    nc = NCTX // tm
    return [pl.BlockSpec((tm, width), lambda *idx: (jnp.minimum(idx[axis], nc - 1), 0)),
            pl.BlockSpec((tm, width), lambda *idx: (jnp.maximum(idx[axis] - nc, 0), 0))]


def _family_rows(refs, tm, axis=0):
    is_ctx = pl.program_id(axis) < NCTX // tm
    return jnp.where(is_ctx, refs[0][...], refs[1][...])


def _inproj_kernel(*refs, normed):
    if normed:
        xn_ref, w_ref, o_ref = refs
        xn = xn_ref[...]
    else:
        hc_ref, hs_ref, g_ref, sc_ref, sh_ref, w_ref, o_ref = refs
        y = _rms(_family_rows((hc_ref, hs_ref), TM_IN, axis=1), g_ref[0])
        xn = (y * (1.0 + sc_ref[0]) + sh_ref[0]).astype(BF16)
    o_ref[...] = _dot_nt(xn, w_ref[0]).astype(BF16)


def _inproj(h, mod3, norm_mix, w_in_p, layer):
    normed = not isinstance(h, tuple)
    if normed:
        in_specs = [pl.BlockSpec((TM_IN, D_MODEL), lambda j, i: (i, 0))]
        args = [h]
    else:
        in_specs = _family_specs(TM_IN, D_MODEL, axis=1) + [
            pl.BlockSpec((1, 1, D_MODEL), lambda j, i: (layer, 0, 0)),
            _mod_spec(layer, 1, TM_IN, axis=1),
            _mod_spec(layer, 0, TM_IN, axis=1)]
        args = list(h) + [norm_mix.reshape(DEPTH, 1, D_MODEL), mod3, mod3]
    return pl.pallas_call(
        functools.partial(_inproj_kernel, normed=normed),
        grid=(PROJ_W // TN_IN, R // TM_IN),
        in_specs=in_specs + [pl.BlockSpec((1, TN_IN, D_MODEL), lambda j, i: (layer, j, 0))],
        out_specs=pl.BlockSpec((TM_IN, TN_IN), lambda j, i: (i, j)),
        out_shape=jax.ShapeDtypeStruct((R, PROJ_W), BF16),
        compiler_params=_cp(("arbitrary", "arbitrary")),
        name="inproj",
    )(*args, w_in_p)


def _mla_prep_kernel(pm_ref, tab_ref, qn_ref, kvn_ref, wq_ref, wk_ref, wv_ref,
                     q_ref, k_ref, v_ref, own_ref):
    pm = pm_ref[...].astype(F32)
    tab = tab_ref[...]
    qn = _rms(pm[:, :Q_LORA], qn_ref[0]).astype(BF16)
    q2 = _dot(qn, wq_ref[0])
    hw = MLA_HEADS * HEAD_W
    cq = jnp.tile(tab[:, :HEAD_W], (1, MLA_HEADS))
    sq = jnp.tile(tab[:, HEAD_W:2 * HEAD_W], (1, MLA_HEADS))
    q_ref[...] = (q2[:, :hw] * cq + q2[:, hw:] * sq).astype(BF16)
    ckv = _rms(pm[:, Q_LORA:Q_LORA + KV_LORA], kvn_ref[0])
    kr = pm[:, Q_LORA + KV_LORA:] * tab[:, 2 * HEAD_W:]
    lhs = jnp.concatenate([ckv, kr], axis=1)
    own_ref[...] = lhs
    lb = lhs.astype(BF16)
    k_ref[...] = _dot(lb, wk_ref[0]).astype(BF16)
    v_ref[...] = _dot_nt(wv_ref[0], lb[:, :KV_LORA]).astype(BF16)


def _mla_prep(proj, tab, q_norm, kv_norm, wq_p, wk_p, wv_p, layer):
    tm = TM_PREP
    hw = MLA_HEADS * HEAD_W

    def tab_map(i):
        r0 = i * tm
        return (jnp.where(r0 < NCTX, 0, 1 + ((r0 - NCTX) % DEC_SEQ) // tm), 0)

    return pl.pallas_call(
        _mla_prep_kernel,
        grid=(R // tm,),
        in_specs=[pl.BlockSpec((tm, 512), lambda i: (i, PC_MLA // 512)),
                  pl.BlockSpec((tm, 3 * HEAD_W), tab_map),
                  pl.BlockSpec((1, 1, Q_LORA), lambda i: (layer, 0, 0)),
                  pl.BlockSpec((1, 1, KV_LORA), lambda i: (layer, 0, 0)),
                  pl.BlockSpec((1, Q_LORA, 2 * hw), lambda i: (layer, 0, 0)),
                  pl.BlockSpec((1, 2 * KV_LORA, hw), lambda i: (layer, 0, 0)),
                  pl.BlockSpec((1, VT_ROWS, KV_LORA), lambda i: (layer, 0, 0))],
        out_specs=[pl.BlockSpec((tm, hw), lambda i: (i, 0)),
                   pl.BlockSpec((tm, hw), lambda i: (i, 0)),
                   pl.BlockSpec((VT_ROWS, tm), lambda i: (0, i)),
                   pl.BlockSpec((tm, 2 * KV_LORA), lambda i: (i, 0))],
        out_shape=[jax.ShapeDtypeStruct((R, hw), BF16),
                   jax.ShapeDtypeStruct((R, hw), BF16),
                   jax.ShapeDtypeStruct((VT_ROWS, R), BF16),
                   jax.ShapeDtypeStruct((R, 2 * KV_LORA), F32)],
        compiler_params=_cp(("arbitrary",)),
        name="mla_prep",
    )(proj, tab, q_norm.reshape(DEPTH, 1, Q_LORA), kv_norm.reshape(DEPTH, 1, KV_LORA),
      wq_p, wk_p, wv_p)


def _kv_cache_kernel(c_ref, wk_ref, wv_ref, k_ref, v_ref):
    lb = c_ref[0, 0].astype(BF16)
    k_ref[0, 0] = _dot(lb, wk_ref[0]).astype(BF16)
    v_ref[0, 0] = _dot_nt(wv_ref[0], lb[:, :KV_LORA]).astype(BF16)


def _kv_cache(cache_p, wk_p, wv_p):
    hw = MLA_HEADS * HEAD_W
    return pl.pallas_call(
        _kv_cache_kernel,
        grid=(DEC_BATCH, DEPTH),
        in_specs=[pl.BlockSpec((1, 1, PAST_LEN, 2 * KV_LORA), lambda b, l: (b, l, 0, 0)),
                  pl.BlockSpec((1, 2 * KV_LORA, hw), lambda b, l: (l, 0, 0)),
                  pl.BlockSpec((1, VT_ROWS, KV_LORA), lambda b, l: (l, 0, 0))],
        out_specs=[pl.BlockSpec((1, 1, PAST_LEN, hw), lambda b, l: (b, l, 0, 0)),
                   pl.BlockSpec((1, 1, VT_ROWS, PAST_LEN), lambda b, l: (b, l, 0, 0))],
        out_shape=[jax.ShapeDtypeStruct((DEC_BATCH, DEPTH, PAST_LEN, hw), BF16),
                   jax.ShapeDtypeStruct((DEC_BATCH, DEPTH, VT_ROWS, PAST_LEN), BF16)],
        compiler_params=_cp(("arbitrary", "arbitrary")),
        name="kv_cache",
    )(cache_p, wk_p, wv_p)


def _attn_kernel(*refs, heads, has_cache):
    if has_cache:
        q_ref, k_ref, vt_ref, kc_ref, vct_ref, o_ref = refs
    else:
        q_ref, k_ref, vt_ref, o_ref = refs
    n_keys = k_ref.shape[0]
    tk = min(TK_ATT, n_keys)
    tiles = ([("cache", 0, PAST_LEN)] if has_cache else []) + [("own", s0, tk) for s0 in range(0, n_keys, tk)]
    units = [(h, t) for t in range(len(tiles)) for h in range(heads)]
    qs = [q_ref[:, h * HEAD_W:(h + 1) * HEAD_W] for h in range(heads)]

    def scores(h, t):
        kind, s0, n = tiles[t]
        sl = slice(h * HEAD_W, (h + 1) * HEAD_W)
        keys = kc_ref[0, 0, :, sl] if kind == "cache" else k_ref[s0:s0 + n, sl]
        st = _dot_nt(keys, qs[h])
        return st, jnp.max(st, axis=0, keepdims=True)

    def values_t(h, t):
        kind, s0, n = tiles[t]
        sv = slice(h * MLA_V, (h + 1) * MLA_V)
        vt = vct_ref[0, 0, sv, :] if kind == "cache" else vt_ref[sv, s0:s0 + n]
        return jnp.concatenate([vt, jnp.ones((ONES_ROWS, n), BF16)], axis=0)

    ahead = ATT_AHEAD
    pending = {u: scores(*u) for u in units[:ahead]}
    m, acc = [None] * heads, [None] * heads
    for i, (h, t) in enumerate(units):
        if i + ahead < len(units):
            pending[units[i + ahead]] = scores(*units[i + ahead])
        st, mt = pending.pop((h, t))
        if m[h] is None:
            m[h] = mt
            acc[h] = _dot(values_t(h, t), jnp.exp2(st - mt).astype(BF16))
        else:
            m_new = jnp.maximum(m[h], mt)
            alpha = jnp.exp2(m[h] - m_new)
            acc[h] = alpha * acc[h] + _dot(values_t(h, t), jnp.exp2(st - m_new).astype(BF16))
            m[h] = m_new
    outs = [acc[h][:MLA_V] / acc[h][MLA_V:MLA_V + 1] for h in range(heads)]
    o_ref[...] = jnp.concatenate(outs, axis=0).T.astype(BF16)


def _attention_ctx(qp, kp, vtp):
    hw = MLA_HEADS * HEAD_W
    return pl.pallas_call(
        functools.partial(_attn_kernel, heads=MLA_HEADS, has_cache=False),
        grid=(BATCH,),
        in_specs=[pl.BlockSpec((SEQ, hw), lambda s: (s, 0)),
                  pl.BlockSpec((SEQ, hw), lambda s: (s, 0)),
                  pl.BlockSpec((VT_ROWS, SEQ), lambda s: (0, s))],
        out_specs=pl.BlockSpec((SEQ, MLA_HEADS * MLA_V), lambda s: (s, 0)),
        out_shape=jax.ShapeDtypeStruct((NCTX, MLA_HEADS * MLA_V), BF16),
        compiler_params=_cp(("arbitrary",)),
        name="attn_ctx",
    )(qp, kp, vtp)


def _attention_smp(qp, kp, vtp, kc, vct, layer):
    hpb = 4
    wq = hpb * HEAD_W
    wv = hpb * MLA_V
    nq = DEC_SEQ // TQ_ATT
    qoff = NCTX // TQ_ATT
    koff = NCTX // DEC_SEQ
    return pl.pallas_call(
        functools.partial(_attn_kernel, heads=hpb, has_cache=True),
        grid=(DEC_BATCH, MLA_HEADS // hpb, nq),
        in_specs=[pl.BlockSpec((TQ_ATT, wq), lambda b, g, i: (qoff + b * nq + i, g)),
                  pl.BlockSpec((DEC_SEQ, wq), lambda b, g, i: (koff + b, g)),
                  pl.BlockSpec((wv, DEC_SEQ), lambda b, g, i: (g, koff + b)),
                  pl.BlockSpec((1, 1, PAST_LEN, wq), lambda b, g, i: (b, layer, 0, g)),
                  pl.BlockSpec((1, 1, wv, PAST_LEN), lambda b, g, i: (b, layer, g, 0))],
        out_specs=pl.BlockSpec((TQ_ATT, hpb * MLA_V), lambda b, g, i: (b * nq + i, g)),
        out_shape=jax.ShapeDtypeStruct((NSMP, MLA_HEADS * MLA_V), BF16),
        compiler_params=_cp(("arbitrary", "arbitrary", "arbitrary")),
        name="attn_smp",
    )(qp, kp, vtp, kc, vct)


def _scan_specs(ctx, blk, widths_and_cols, dir_cols=()):
    nseq, seq_len = (BATCH, SEQ) if ctx else (DEC_BATCH, DEC_SEQ)
    nblk = seq_len // blk
    roff = 0 if ctx else NCTX // blk

    def rb(dd):
        return lambda s, j: roff + s * nblk + (j if dd == 0 else nblk - 1 - j)

    specs = []
    for dd in range(2):
        r = rb(dd)
        specs += [pl.BlockSpec((blk, w), lambda s, j, r=r, c=c: (r(s, j), c)) for w, c in widths_and_cols]
        specs += [pl.BlockSpec((blk, w), lambda s, j, r=r, c=c + dd: (r(s, j), c)) for w, c in dir_cols]
    out_maps = [lambda s, j, r=rb(dd): (r(s, j) - roff, 0) for dd in range(2)]
    return nseq, nblk, specs, out_maps


def _gla_kernel(*refs, nblk, has_s0, nch):
    per_dir = [refs[0:4], refs[4:8]]
    wg_ref, bg_ref = refs[8:10]
    if has_s0:
        s0_ref, of_ref, ob_ref, s_scr = refs[10:]
    else:
        of_ref, ob_ref, sfin_ref, s_scr = refs[10:]
    o_refs = [of_ref, ob_ref]
    j = pl.program_id(1)

    @pl.when(j == 0)
    def _():
        if has_s0:
            s_scr[...] = s0_ref[0, 0]
        else:
            s_scr[...] = jnp.zeros_like(s_scr)

    r64 = lax.broadcasted_iota(jnp.int32, (CHUNK, CHUNK), 0)
    c64 = lax.broadcasted_iota(jnp.int32, (CHUNK, CHUNK), 1)
    incl = [c64 <= r64, c64 >= r64]
    tri_b = [jnp.where(m, 1.0, 0.0).astype(BF16) for m in incl]
    heads = [(slice(h * GLA_DK, (h + 1) * GLA_DK), slice(h * GLA_DV, (h + 1) * GLA_DV))
             for h in range(GLA_HEADS)]

    ch = []
    for c in range(nch):
        for dd in range(2):
            ci = c if dd == 0 else nch - 1 - c
            rows = slice(ci * CHUNK, (ci + 1) * CHUNK)
            glr_ref = per_dir[dd][3]
            ch.append(dict(dd=dd, rows=rows, z=_dot(glr_ref[rows, :], wg_ref[0, dd]) + bg_ref[0, dd]))
    for t in ch:
        t["la"] = _split3(_log_sigmoid(t["z"]) * (1.0 / GLA_TAU))
    for t in ch:
        hi, mid, lo = t["la"]
        tb = tri_b[t["dd"]]
        t["bc"] = _dot(tb, hi) + (_dot(tb, mid) + _dot(tb, lo))
    for t in ch:
        q_ref, k_ref, v_ref, _ = per_dir[t["dd"]]
        rows, bc = t["rows"], t["bc"]
        q = q_ref[rows, :].astype(F32) * (GLA_DK ** -0.5)
        k = k_ref[rows, :].astype(F32)
        bl = bc[CHUNK - 1:CHUNK] if t["dd"] == 0 else bc[0:1]
        t["qd"] = (q * jnp.exp(bc)).astype(BF16)
        t["ki"] = (k * jnp.exp(-bc)).astype(BF16)
        t["ke"] = (k * jnp.exp(bl - bc)).astype(BF16)
        t["dec_t"] = jnp.transpose(jnp.broadcast_to(jnp.exp(bl), (LANE, GLA_HEADS * GLA_DK)))
        t["v"] = v_ref[rows, :]
    for t in ch:
        t["a"] = [jnp.where(incl[t["dd"]], _dot_nt(t["qd"][:, sk], t["ki"][:, sk]), 0.0).astype(BF16)
                  for sk, _ in heads]
    for t in ch:
        t["oi"] = [_dot(t["a"][h], t["v"][:, sv]) for h, (_, sv) in enumerate(heads)]
        t["upd"] = [_dot_tn(t["ke"][:, sk], t["v"][:, sv]) for sk, sv in heads]
    states = [[s_scr[dd, h] for h in range(GLA_HEADS)] for dd in range(2)]
    for t in ch:
        st = states[t["dd"]]
        t["s_in"] = [s.astype(BF16) for s in st]
        for h, (sk, _) in enumerate(heads):
            st[h] = t["dec_t"][sk, :] * st[h] + t["upd"][h]
    for t in ch:
        for h, (sk, sv) in enumerate(heads):
            o_refs[t["dd"]][t["rows"], sv] = t["oi"][h] + _dot(t["qd"][:, sk], t["s_in"][h])
    for dd in range(2):
        for h in range(GLA_HEADS):
            s_scr[dd, h] = states[dd][h]

    if not has_s0:
        @pl.when(j == nblk - 1)
        def _():
            sfin_ref[0] = s_scr[...]


def _gla(proj, wg_p, bg_p, s0, layer, ctx):
    blk = GLA_BLK_CTX if ctx else GLA_BLK_SMP
    hk = GLA_HEADS * GLA_DK
    hv = GLA_HEADS * GLA_DV
    nseq, nblk, in_specs, out_maps = _scan_specs(
        ctx, blk, [(hk, PC_GQ // hk), (hk, PC_GK // hk), (hv, PC_GV // hv), (LANE, PC_GLR // LANE)])
    in_specs += [pl.BlockSpec((1, 2, LANE, hk), lambda s, j: (layer, 0, 0, 0)),
                 pl.BlockSpec((1, 2, 1, hk), lambda s, j: (layer, 0, 0, 0))]
    args = [proj] * 8 + [wg_p, bg_p]
    nrows = NCTX if ctx else NSMP
    out_specs = [pl.BlockSpec((blk, hv), m) for m in out_maps]
    out_shape = [jax.ShapeDtypeStruct((nrows, hv), F32)] * 2
    st_blk = (2, GLA_HEADS, GLA_DK, GLA_DV)
    if ctx:
        out_specs.append(pl.BlockSpec((1,) + st_blk, lambda s, j: (s, 0, 0, 0, 0)))
        out_shape.append(jax.ShapeDtypeStruct((BATCH,) + st_blk, F32))
    else:
        in_specs.append(pl.BlockSpec((1, 1) + st_blk, lambda s, j: (s, layer, 0, 0, 0, 0)))
        args.append(s0)
    return pl.pallas_call(
        functools.partial(_gla_kernel, nblk=nblk, has_s0=not ctx, nch=blk // CHUNK),
        grid=(nseq, nblk),
        in_specs=in_specs, out_specs=out_specs, out_shape=out_shape,
        scratch_shapes=[pltpu.VMEM(st_blk, F32)],
        compiler_params=_cp(("arbitrary", "arbitrary")),
        name="gla_ctx" if ctx else "gla_smp",
    )(*args)


GDN_G = 2
GDN_GW = GDN_G * GDN_DK
HALO = 16


def _gdn_pre_kernel(x_ref, xp_ref, xn_ref, cw_ref, o_ref):
    blk = x_ref.shape[0]
    hk = GDN_HEADS * GDN_DK
    sub = SEQ
    cw = 0.5 * cw_ref[0]
    rowi = lax.broadcasted_iota(jnp.int32, (sub, 1), 0)
    lane = lax.broadcasted_iota(jnp.int32, (sub, LANE), 1)
    lo_half = lane < GDN_DK

    def l2n(t, scale):
        parts = []
        for c in range(hk // LANE):
            tc = t[:, c * LANE:(c + 1) * LANE]
            sq = tc * tc
            s_lo = jnp.sum(jnp.where(lo_half, sq, 0.0), axis=-1, keepdims=True)
            s_hi = jnp.sum(jnp.where(lo_half, 0.0, sq), axis=-1, keepdims=True)
            parts.append(tc * (lax.rsqrt(jnp.where(lo_half, s_lo, s_hi) + EPS) * scale))
        return jnp.concatenate(parts, axis=1)

    for s in range(blk // sub):
        rows = slice(s * sub, (s + 1) * sub)
        x = x_ref[rows, :].astype(F32)
        r0 = pl.program_id(0) * blk + s * sub
        pos0 = jnp.where(r0 < NCTX, r0 % SEQ, (r0 - NCTX) % DEC_SEQ)
        seq_len = jnp.where(r0 < NCTX, SEQ, DEC_SEQ)
        before = xp_ref[...] if s == 0 else x_ref[s * sub - HALO:s * sub, :]
        after = xn_ref[...] if s == blk // sub - 1 else x_ref[(s + 1) * sub:(s + 1) * sub + HALO, :]
        prev_row = jnp.where(pos0 > 0, before.astype(F32)[HALO - 1:HALO], 0.0)
        next_row = jnp.where(pos0 + sub < seq_len, after.astype(F32)[0:1], 0.0)
        x_prev = jnp.where(rowi == 0, prev_row, pltpu.roll(x, 1, 0))
        x_next = jnp.where(rowi == sub - 1, next_row, pltpu.roll(x, sub - 1, 0))
        y = _silu_of_twice(x_prev * cw[0:1] + x * cw[1:2] + x_next * cw[2:3])
        o_ref[rows, :hk] = l2n(y[:, :hk], GDN_DK ** -0.5).astype(BF16)
        o_ref[rows, hk:2 * hk] = l2n(y[:, hk:2 * hk], 1.0).astype(BF16)
        o_ref[rows, 2 * hk:] = y[:, 2 * hk:].astype(BF16)


def _gdn_pre(proj, conv_w, layer):
    cw = 2 * GDN_HEADS * GDN_DK + GDN_HEADS * GDN_DV
    blk = GDN_PRE_BLK
    halo = blk // HALO
    last_h = R // HALO - 1
    return pl.pallas_call(
        _gdn_pre_kernel,
        grid=(R // blk,),
        in_specs=[pl.BlockSpec((blk, cw), lambda i: (i, PC_DQKV // cw)),
                  pl.BlockSpec((HALO, cw), lambda i: (jnp.maximum(i * halo - 1, 0), PC_DQKV // cw)),
                  pl.BlockSpec((HALO, cw), lambda i: (jnp.minimum((i + 1) * halo, last_h), PC_DQKV // cw)),
                  pl.BlockSpec((1, 3, cw), lambda i: (layer, 0, 0))],
        out_specs=pl.BlockSpec((blk, cw), lambda i: (i, 0)),
        out_shape=jax.ShapeDtypeStruct((R, cw), BF16),
        compiler_params=_cp(("arbitrary",)),
        name="gdn_pre",
    )(proj, proj, proj, conv_w)


def _gdn_kernel(*refs, nblk, has_s0):
    per_dir = [refs[0:4], refs[4:8]]
    alog_ref, dtb_ref = refs[8:10]
    if has_s0:
        s0_ref, of_ref, ob_ref, s_scr = refs[10:]
    else:
        of_ref, ob_ref, sfin_ref, s_scr = refs[10:]
    o_refs = [of_ref, ob_ref]
    j = pl.program_id(1)
    ngrp = GDN_HEADS // GDN_G

    def head_block(h):
        g, hh = divmod(h, GDN_G)
        return g, slice(hh * GDN_DK, (hh + 1) * GDN_DK), slice(hh * GDN_DV, (hh + 1) * GDN_DV)

    @pl.when(j == 0)
    def _():
        s_scr[...] = jnp.zeros_like(s_scr)
        if has_s0:
            for dd in range(2):
                for h in range(GDN_HEADS):
                    g, rk, rv = head_block(h)
                    s_scr[dd, g, rk, rv] = s0_ref[0, 0, dd, h]

    row = lax.broadcasted_iota(jnp.int32, (CHUNK, GDN_GW), 0)
    col = lax.broadcasted_iota(jnp.int32, (CHUNK, GDN_GW), 1) % CHUNK
    r64 = lax.broadcasted_iota(jnp.int32, (CHUNK, CHUNK), 0)
    c64 = lax.broadcasted_iota(jnp.int32, (CHUNK, CHUNK), 1)
    dir_masks = []
    for dd in range(2):
        lag = (row - col) if dd == 0 else (col - row)
        tri = (c64 <= r64) if dd == 0 else (c64 >= r64)
        dir_masks.append(dict(incl=lag >= 0, strict=lag > 0,
                              tri_b=jnp.where(tri, 1.0, 0.0).astype(BF16)))
    eye = col == row
    eye_f = jnp.where(eye, 1.0, 0.0)
    blk8 = (row // 8) == (col // 8)
    blk16 = (row // 16) == (col // 16)
    blk32 = (row // 32) == (col // 32)
    brow = lax.broadcasted_iota(jnp.int32, (GDN_GW, GDN_GW), 0) // CHUNK
    bcol = lax.broadcasted_iota(jnp.int32, (GDN_GW, GDN_GW), 1) // CHUNK
    same_head = brow == bcol

    def bdiag(t):
        tb = t.astype(BF16)
        return jnp.where(same_head, jnp.concatenate([tb] * GDN_G, axis=0), jnp.zeros((), BF16))

    def bdmm(a, b):
        return _dot(a.astype(BF16), bdiag(b))

    def setup(dd, c):
        q_ref, k_ref, v_ref, ab_ref = per_dir[dd]
        mk = dir_masks[dd]
        ci = c if dd == 0 else NCH - 1 - c
        rows = slice(ci * CHUNK, (ci + 1) * CHUNK)
        ab = ab_ref[rows, :].astype(F32)
        neg_a = -jnp.exp(alog_ref[0, dd])
        gam = _cumsum_rows(mk["tri_b"], neg_a * _softplus(ab + dtb_ref[0, dd]))
        beta_all = _sigmoid(ab)
        out = []
        for g in range(ngrp):
            gl = slice(g * GDN_GW, (g + 1) * GDN_GW)
            gcol = jnp.concatenate(
                [jnp.broadcast_to(gam[:, g * GDN_G + h:g * GDN_G + h + 1], (CHUNK, GDN_DK))
                 for h in range(GDN_G)], axis=1)
            bcol_ = jnp.concatenate(
                [jnp.broadcast_to(beta_all[:, GDN_HEADS + g * GDN_G + h:GDN_HEADS + g * GDN_G + h + 1],
                                  (CHUNK, GDN_DK)) for h in range(GDN_G)], axis=1)
            grow = jnp.sum(jnp.where(eye, gcol, 0.0), axis=0, keepdims=True)
            glast = gcol[CHUNK - 1:CHUNK] if dd == 0 else gcol[0:1]
            decay = jnp.where(mk["incl"], jnp.exp(gcol - grow), 0.0)
            egc = jnp.exp(gcol)
            qg = q_ref[rows, gl].astype(F32)
            kg = k_ref[rows, gl].astype(F32)
            vg = v_ref[rows, gl].astype(F32)
            kq = _dot_nt(jnp.concatenate([k_ref[rows, gl], q_ref[rows, gl]], axis=0),
                         bdiag(k_ref[rows, gl]))
            out.append(dict(dd=dd, g=g, rows=rows, gl=gl,
                            m=jnp.where(mk["strict"], bcol_ * kq[:CHUNK] * decay, 0.0),
                            aqk=(kq[CHUNK:] * decay).astype(BF16),
                            vb=vg * bcol_, kb=kg * (bcol_ * egc), qe=qg * egc,
                            kend=(kg * jnp.exp(glast - gcol)).astype(BF16),
                            eg=jnp.exp(glast)))
        return out

    def stages(ch):
        pairs = list(zip(ch[0::2], ch[1::2]))

        def bdmm_pairs(lhs, rhs, out):
            for ta, tb in pairs:
                a = jnp.concatenate([ta[lhs].astype(BF16), tb[lhs].astype(BF16)], axis=0)
                w = jnp.concatenate([bdiag(ta[rhs]), bdiag(tb[rhs])], axis=1)
                y = _dot(a, w)
                ta[out] = y[:CHUNK, :GDN_GW]
                tb[out] = y[CHUNK:, GDN_GW:]

        def neumann_a():
            for t in ch:
                t["n8"] = jnp.where(blk8, t["m"], 0.0)
            bdmm_pairs("n8", "n8", "n2")

        def neumann_b():
            for t in ch:
                t["p1"] = eye_f - t["n8"]
            bdmm_pairs("n2", "n2", "n4")
            bdmm_pairs("p1", "n2", "p1n2")
            for t in ch:
                t["p1"] = t["p1"] + t["p1n2"]

        def neumann_c():
            bdmm_pairs("p1", "n4", "p1n4")
            for t in ch:
                t["dinv"] = t["p1"] + t["p1n4"]

        def double_a(off):
            def run():
                for t in ch:
                    t["off"] = jnp.where(off, t["m"], 0.0)
                bdmm_pairs("dinv", "off", "dl")
            return run

        def double_b():
            bdmm_pairs("dl", "dinv", "dld")
            for t in ch:
                t["dinv"] = t["dinv"] - t["dld"]

        def solve():
            for t in ch:
                uw = _dot(t["dinv"].astype(BF16), jnp.concatenate([bdiag(t["vb"]), bdiag(t["kb"])], axis=1))
                t["u"] = uw[:, :GDN_GW]
                t["w"] = uw[:, GDN_GW:]

        def fold():
            for t in ch:
                wu = jnp.concatenate([t["w"], t["u"]], axis=1).astype(BF16)
                cb = _dot_tn(t["kend"], wu)
                t["c"] = jnp.where(same_head, cb[:, :GDN_GW], 0.0).astype(BF16)
                t["b"] = jnp.where(same_head, cb[:, GDN_GW:], 0.0)
                ao = _dot(t["aqk"], jnp.concatenate([bdiag(t["w"]), bdiag(t["u"])], axis=1))
                t["qt"] = (t["qe"] - ao[:, :GDN_GW]).astype(BF16)
                t["o"] = ao[:, GDN_GW:]

        out = [neumann_a, neumann_b, neumann_c]
        for inner, outer in ((blk8, blk16), (blk16, blk32), (blk32, None)):
            off = jnp.logical_not(inner) if outer is None else jnp.logical_and(outer, jnp.logical_not(inner))
            out += [double_a(off), double_b]
        return out + [solve, fold]

    states = [[s_scr[dd, g] for g in range(ngrp)] for dd in range(2)]

    def recur(cur):
        ys = [_dot(jnp.concatenate([t["qt"], t["c"]], axis=0), states[t["dd"]][t["g"]].astype(BF16))
              for t in cur]
        for t, y in zip(cur, ys):
            o_refs[t["dd"]][t["rows"], t["gl"]] = y[:CHUNK] + t["o"]
            states[t["dd"]][t["g"]] = t["eg"] * states[t["dd"]][t["g"]] - y[CHUNK:] + t["b"]

    half = NCH // 2
    chains = {}
    for c in range(half):
        for dd in range(2):
            chains[(dd, c)] = setup(dd, c)
    later = [(dd, c) for c in range(half, NCH) for dd in range(2)]
    wave_a = [t for c in range(half) for dd in range(2) for t in chains[(dd, c)]]
    for k, stage in enumerate(stages(wave_a)):
        stage()
        if k % 2 == 0 and later:
            key = later.pop(0)
            chains[key] = setup(*key)
    for key in later:
        chains[key] = setup(*key)
    wave_b = [t for c in range(half, NCH) for dd in range(2) for t in chains[(dd, c)]]
    done = 0
    for k, stage in enumerate(stages(wave_b)):
        stage()
        if k % 3 == 1 and done < half:
            recur(chains[(0, done)] + chains[(1, done)])
            done += 1
    for c in range(done, NCH):
        recur(chains[(0, c)] + chains[(1, c)])
    for dd in range(2):
        for g in range(ngrp):
            s_scr[dd, g] = states[dd][g]

    if not has_s0:
        @pl.when(j == nblk - 1)
        def _():
            for dd in range(2):
                for h in range(GDN_HEADS):
                    g, rk, rv = head_block(h)
                    sfin_ref[0, dd, h] = s_scr[dd, g, rk, rv]


def _gdn(qkv, proj, alog_p, dtb_p, s0, layer, ctx):
    nseq, nblk = (BATCH, SEQ // SEQ_BLK) if ctx else (DEC_BATCH, DEC_SEQ // SEQ_BLK)
    roff = 0 if ctx else NCTX // SEQ_BLK
    hk = GDN_HEADS * GDN_DK
    hv = GDN_HEADS * GDN_DV

    def rb(dd):
        return lambda s, j: roff + s * nblk + (j if dd == 0 else nblk - 1 - j)

    in_specs, args = [], []
    for dd in range(2):
        r = rb(dd)
        in_specs += [pl.BlockSpec((SEQ_BLK, hk), lambda s, j, r=r: (r(s, j), 0)),
                     pl.BlockSpec((SEQ_BLK, hk), lambda s, j, r=r: (r(s, j), 1)),
                     pl.BlockSpec((SEQ_BLK, hv), lambda s, j, r=r: (r(s, j), 2 * hk // hv)),
                     pl.BlockSpec((SEQ_BLK, LANE), lambda s, j, r=r, dd=dd: (r(s, j), PC_DAB // LANE + dd))]
        args += [qkv, qkv, qkv, proj]
    in_specs += [pl.BlockSpec((1, 2, 1, LANE), lambda s, j: (layer, 0, 0, 0))] * 2
    args += [alog_p, dtb_p]
    nrows = NCTX if ctx else NSMP
    out_specs = [pl.BlockSpec((SEQ_BLK, hv), lambda s, j, r=rb(dd): (r(s, j) - roff, 0)) for dd in range(2)]
    out_shape = [jax.ShapeDtypeStruct((nrows, hv), F32)] * 2
    st_blk = (2, GDN_HEADS, GDN_DK, GDN_DV)
    if ctx:
        out_specs.append(pl.BlockSpec((1,) + st_blk, lambda s, j: (s, 0, 0, 0, 0)))
        out_shape.append(jax.ShapeDtypeStruct((BATCH,) + st_blk, F32))
    else:
        in_specs.append(pl.BlockSpec((1, 1) + st_blk, lambda s, j: (s, layer, 0, 0, 0, 0)))
        args.append(s0)
    return pl.pallas_call(
        functools.partial(_gdn_kernel, nblk=nblk, has_s0=not ctx),
        grid=(nseq, nblk),
        in_specs=in_specs, out_specs=out_specs, out_shape=out_shape,
        scratch_shapes=[pltpu.VMEM((2, GDN_HEADS // GDN_G, GDN_GW, GDN_GW), F32)],
        compiler_params=_cp(("arbitrary", "arbitrary")),
        name="gdn_ctx" if ctx else "gdn_smp",
    )(*args)


def _group_rms(x, width):
    parts = []
    lane = lax.broadcasted_iota(jnp.int32, (x.shape[0], LANE), 1)
    lo_half = lane < 64
    for c in range(x.shape[1] // LANE):
        xc = x[:, c * LANE:(c + 1) * LANE]
        sq = xc * xc
        if width == LANE:
            ms = jnp.mean(sq, axis=-1, keepdims=True)
        else:
            s_lo = jnp.sum(jnp.where(lo_half, sq, 0.0), axis=-1, keepdims=True)
            s_hi = jnp.sum(jnp.where(lo_half, 0.0, sq), axis=-1, keepdims=True)
            ms = jnp.where(lo_half, s_lo, s_hi) * (1.0 / width)
        parts.append(xc * lax.rsqrt(ms + EPS))
    return jnp.concatenate(parts, axis=1)


def _merge_kernel(ymc_ref, yms_ref, ogcf_ref, ogcb_ref, ogsf_ref, ogsb_ref, gr_ref,
                  odcf_ref, odcb_ref, odsf_ref, odsb_ref,
                  dz_ref, gt_ref, *rest, split_h):
    if split_h:
        hc_ref, hs_ref, *rest = rest
        h_rows = lambda: _family_rows((hc_ref, hs_ref), TM_MERGE)
    else:
        h_ref, *rest = rest
        h_rows = lambda: h_ref[...]
    ga_ref, gn_ref, dn_ref, wb_ref, wo_ref, o_ref, ym_scr, og_scr, od_scr = rest
    is_ctx = pl.program_id(0) < NCTX // TM_MERGE

    @pl.when(is_ctx)
    def _():
        ym_scr[...] = ymc_ref[...]
        og_scr[...] = ogcf_ref[...] + ogcb_ref[...]
        od_scr[...] = odcf_ref[...] + odcb_ref[...]

    @pl.when(jnp.logical_not(is_ctx))
    def _():
        ym_scr[...] = yms_ref[...]
        og_scr[...] = ogsf_ref[...] + ogsb_ref[...]
        od_scr[...] = odsf_ref[...] + odsb_ref[...]

    y_gla = _group_rms(og_scr[...], GLA_DV) * gn_ref[0] * _silu_of_twice(gr_ref[...].astype(F32))
    y_gdn = _group_rms(od_scr[...], GDN_DV) * dn_ref[0] * _silu_of_twice(dz_ref[...].astype(F32))
    t = jnp.tanh(gt_ref[...].astype(F32))
    d = D_MODEL
    acc = None
    for n, y_n in enumerate((ym_scr[...], y_gla.astype(BF16), y_gdn.astype(BF16))):
        p = _dot(y_n, wb_ref[0, n])
        gated = p + p * t[:, n * d:(n + 1) * d]
        acc = gated if acc is None else acc + gated
    y = _dot(acc.astype(BF16), wo_ref[0])
    o_ref[...] = h_rows() + ga_ref[0] * y


def _merge(y_mla, o_gla, o_gdn, proj, h, mod3, gla_norm_p, gdn_norm_p, w_branch_b, w_out_b, layer):
    tm = TM_MERGE
    bw = 512
    nc = NCTX // tm
    split_h = isinstance(h, tuple)
    h_specs = _family_specs(tm, D_MODEL) if split_h else [pl.BlockSpec((tm, D_MODEL), lambda i: (i, 0))]
    h_args = list(h) if split_h else [h]

    def cmap(i):
        return jnp.minimum(i, nc - 1)

    def smap(i):
        return jnp.maximum(i - nc, 0)

    return pl.pallas_call(
        functools.partial(_merge_kernel, split_h=split_h),
        grid=(R // tm,),
        in_specs=[pl.BlockSpec((tm, bw), lambda i: (cmap(i), 0)),
                  pl.BlockSpec((tm, bw), lambda i: (smap(i), 0)),
                  pl.BlockSpec((tm, bw), lambda i: (cmap(i), 0)),
                  pl.BlockSpec((tm, bw), lambda i: (cmap(i), 0)),
                  pl.BlockSpec((tm, bw), lambda i: (smap(i), 0)),
                  pl.BlockSpec((tm, bw), lambda i: (smap(i), 0)),
                  pl.BlockSpec((tm, bw), lambda i: (i, PC_GR // bw)),
                  pl.BlockSpec((tm, bw), lambda i: (cmap(i), 0)),
                  pl.BlockSpec((tm, bw), lambda i: (cmap(i), 0)),
                  pl.BlockSpec((tm, bw), lambda i: (smap(i), 0)),
                  pl.BlockSpec((tm, bw), lambda i: (smap(i), 0)),
                  pl.BlockSpec((tm, bw), lambda i: (i, PC_DZ // bw)),
                  pl.BlockSpec((tm, 3 * D_MODEL), lambda i: (i, PC_GATES // (3 * D_MODEL)))] + h_specs + [
                  _mod_spec(layer, 2, tm),
                  pl.BlockSpec((1, 1, bw), lambda i: (layer, 0, 0)),
                  pl.BlockSpec((1, 1, bw), lambda i: (layer, 0, 0)),
                  pl.BlockSpec((1, 3, bw, D_MODEL), lambda i: (layer, 0, 0, 0), pipeline_mode=pl.Buffered(1)),
                  pl.BlockSpec((1, D_MODEL, D_MODEL), lambda i: (layer, 0, 0), pipeline_mode=pl.Buffered(1))],
        out_specs=pl.BlockSpec((tm, D_MODEL), lambda i: (i, 0)),
        out_shape=jax.ShapeDtypeStruct((R, D_MODEL), F32),
        scratch_shapes=[pltpu.VMEM((tm, bw), BF16), pltpu.VMEM((tm, bw), F32), pltpu.VMEM((tm, bw), F32)],
        compiler_params=_cp(("arbitrary",)),
        name="merge",
    )(y_mla[0], y_mla[1], o_gla[0][0], o_gla[0][1], o_gla[1][0], o_gla[1][1], proj,
      o_gdn[0][0], o_gdn[0][1], o_gdn[1][0], o_gdn[1][1],
      proj, proj, *h_args, mod3, gla_norm_p, gdn_norm_p, w_branch_b, w_out_b)


def _ffn_kernel(h_ref, g_ref, sc_ref, sh_ref, gf_ref, wi_ref, wo_ref, *rest, final):
    if final:
        fn_ref, *o_refs = rest
    else:
        gn_ref, scn_ref, shn_ref, *o_refs = rest
    h = h_ref[...]
    xf = (_rms(h, g_ref[0]) * (1.0 + sc_ref[0]) + sh_ref[0]).astype(BF16)

    def gate_up(c):
        f0 = c * TF_FFN
        return (_dot(xf, wi_ref[0, :, f0:f0 + TF_FFN]),
                _dot(xf, wi_ref[0, :, D_FF + f0:D_FF + f0 + TF_FFN]))

    n_chunks = D_FF // TF_FFN
    nxt = gate_up(0)
    acc = None
    for c in range(n_chunks):
        g, u = nxt
        if c + 1 < n_chunks:
            nxt = gate_up(c + 1)
        part = _dot((_silu(g) * u).astype(BF16), wo_ref[0, c * TF_FFN:(c + 1) * TF_FFN, :])
        acc = part if acc is None else acc + part
    out = h + gf_ref[0] * acc
    if not final:
        o_refs[0][...] = out
        o_refs[1][...] = (_rms(out, gn_ref[0]) * (1.0 + scn_ref[0]) + shn_ref[0]).astype(BF16)
    else:
        out = _rms(out, fn_ref[...])
        is_ctx = pl.program_id(0) < NCTX // TM_FFN

        @pl.when(is_ctx)
        def _():
            o_refs[0][...] = out

        @pl.when(jnp.logical_not(is_ctx))
        def _():
            o_refs[1][...] = out


def _ffn(h, mod3, norm_ffn, wi_b, wo_b, tail, layer, final):
    tm = TM_FFN
    if final:
        nc = NCTX // tm
        out_specs = [pl.BlockSpec((tm, D_MODEL), lambda i: (jnp.minimum(i, nc - 1), 0)),
                     pl.BlockSpec((tm, D_MODEL), lambda i: (jnp.maximum(i - nc, 0), 0))]
        out_shape = [jax.ShapeDtypeStruct((NCTX, D_MODEL), F32), jax.ShapeDtypeStruct((NSMP, D_MODEL), F32)]
        tail_specs = [pl.BlockSpec((1, D_MODEL), lambda i: (0, 0))]
        tail_args = [tail.reshape(1, D_MODEL)]
    else:
        out_specs = [pl.BlockSpec((tm, D_MODEL), lambda i: (i, 0))] * 2
        out_shape = [jax.ShapeDtypeStruct((R, D_MODEL), F32), jax.ShapeDtypeStruct((R, D_MODEL), BF16)]
        tail_specs = [pl.BlockSpec((1, 1, D_MODEL), lambda i: (layer + 1, 0, 0)),
                      _mod_spec(layer + 1, 1, tm), _mod_spec(layer + 1, 0, tm)]
        tail_args = [tail.reshape(DEPTH, 1, D_MODEL), mod3, mod3]
    return pl.pallas_call(
        functools.partial(_ffn_kernel, final=final),
        grid=(R // tm,),
        in_specs=[pl.BlockSpec((tm, D_MODEL), lambda i: (i, 0)),
                  pl.BlockSpec((1, 1, D_MODEL), lambda i: (layer, 0, 0)),
                  _mod_spec(layer, 4, tm),
                  _mod_spec(layer, 3, tm),
                  _mod_spec(layer, 5, tm),
                  pl.BlockSpec((1, D_MODEL, 2 * D_FF), lambda i: (layer, 0, 0), pipeline_mode=pl.Buffered(1)),
                  pl.BlockSpec((1, D_FF, D_MODEL), lambda i: (layer, 0, 0), pipeline_mode=pl.Buffered(1))]
        + tail_specs,
        out_specs=out_specs, out_shape=out_shape,
        compiler_params=_cp(("arbitrary",)),
        name="ffn_final" if final else "ffn",
    )(h, norm_ffn.reshape(DEPTH, 1, D_MODEL), mod3, mod3, mod3, wi_b, wo_b, *tail_args)


def _rope_partner(w, axis=-1):
    h = AXIS_DIM // 2
    axis = axis % w.ndim
    parts = []
    for a in range(2):
        x1 = lax.slice_in_dim(w, a * AXIS_DIM, a * AXIS_DIM + h, axis=axis)
        x2 = lax.slice_in_dim(w, a * AXIS_DIM + h, (a + 1) * AXIS_DIM, axis=axis)
        parts += [-x2, x1]
    return jnp.concatenate(parts, axis=axis)


def _pack_w_in_kernel(w_ref, o_ref):
    o_ref[0] = _pack_channels(w_ref[0]).astype(BF16)


def _pack_w_in(w_in):
    tc = 256
    return pl.pallas_call(
        _pack_w_in_kernel,
        grid=(DEPTH, D_MODEL // tc),
        in_specs=[pl.BlockSpec((1, w_in.shape[-1], tc), lambda l, i: (l, 0, i))],
        out_specs=pl.BlockSpec((1, PROJ_W, tc), lambda l, i: (l, 0, i)),
        out_shape=jax.ShapeDtypeStruct((DEPTH, PROJ_W, D_MODEL), BF16),
        compiler_params=_cp(("arbitrary", "arbitrary")),
        name="pack_w_in",
    )(jnp.swapaxes(w_in, 1, 2))


def _pack_channels(wt):
    c = [0]
    for s in (Q_LORA, KV_LORA + ROPE_DIM, 256, 256, 512, 512, 2 * GLA_RANK, 1536, 512,
              2 * GDN_HEADS, 2 * GDN_HEADS, 3 * D_MODEL):
        c.append(c[-1] + s)
    mq, mkv, gq, gk, gv, gr, glr, dqkv, dz, da, db, gates = [wt[c[i]:c[i + 1]] for i in range(12)]

    def zeros(n):
        return jnp.zeros((n,) + wt.shape[1:], wt.dtype)

    kr = mkv[KV_LORA:]
    mla = [mq, mkv[:KV_LORA], kr, _rope_partner(kr, axis=0), zeros(64)]
    dab = []
    for d in range(2):
        dab += [da[d * GDN_HEADS:(d + 1) * GDN_HEADS], db[d * GDN_HEADS:(d + 1) * GDN_HEADS],
                zeros(LANE - 2 * GDN_HEADS)]
    return jnp.concatenate([0.5 * gates, dqkv, 0.5 * dz, gv, 0.5 * gr] + mla
                           + [gq, gk, glr, zeros(LANE - 2 * GLA_RANK)]
                           + dab + [zeros(PROJ_W - PC_DAB - 2 * LANE)], axis=0)


def _pack_mla_weights(w_uq, w_ukv):
    l = w_uq.shape[0]
    qh = w_uq.reshape(l, Q_LORA, MLA_HEADS, MLA_NOPE + ROPE_DIM)
    zq = jnp.zeros((l, Q_LORA, MLA_HEADS, HEAD_W - MLA_NOPE - ROPE_DIM), w_uq.dtype)
    zn = jnp.zeros((l, Q_LORA, MLA_HEADS, MLA_NOPE), w_uq.dtype)
    wq = jnp.concatenate([qh, zq], axis=-1).reshape(l, Q_LORA, MLA_HEADS * HEAD_W)
    wq_sw = jnp.concatenate([zn, _rope_partner(qh[..., MLA_NOPE:]), zq], axis=-1).reshape(l, Q_LORA, MLA_HEADS * HEAD_W)
    wq_p = jnp.concatenate([wq, wq_sw], axis=-1).astype(BF16)

    kvh = w_ukv.reshape(l, KV_LORA, MLA_HEADS, MLA_NOPE + MLA_V)
    zk = jnp.zeros((l, KV_LORA, MLA_HEADS, HEAD_W - MLA_NOPE), w_ukv.dtype)
    wk_top = jnp.concatenate([kvh[..., :MLA_NOPE], zk], axis=-1).reshape(l, KV_LORA, MLA_HEADS * HEAD_W)
    place = np.zeros((KV_LORA, MLA_HEADS, HEAD_W), np.float32)
    idx = np.arange(ROPE_DIM)
    for rep in range(2):
        place[rep * ROPE_DIM + idx, :, MLA_NOPE + idx] = 1.0
    wk_bot = jnp.broadcast_to(jnp.asarray(place.reshape(1, KV_LORA, MLA_HEADS * HEAD_W)),
                              (l, KV_LORA, MLA_HEADS * HEAD_W))
    wk_p = jnp.concatenate([wk_top, wk_bot], axis=1).astype(BF16)
    wv_p = jnp.transpose(kvh[..., MLA_NOPE:], (0, 2, 3, 1)).reshape(l, VT_ROWS, KV_LORA).astype(BF16)
    return wq_p, wk_p, wv_p


def _rope_tables(tm):
    f32 = np.float32
    rows = DEC_SEQ // GRID_W
    row = np.repeat(np.arange(rows, dtype=np.float64), GRID_W)
    col = np.tile(np.arange(GRID_W, dtype=np.float64), rows)
    inv = ROPE_THETA ** (-np.arange(0, AXIS_DIM, 2, dtype=np.float64) / AXIS_DIM)
    ang_r, ang_c = row[:, None] * inv, col[:, None] * inv
    cos32 = np.concatenate([np.cos(ang_r)] * 2 + [np.cos(ang_c)] * 2, axis=-1)
    sin32 = np.concatenate([np.sin(ang_r)] * 2 + [np.sin(ang_c)] * 2, axis=-1)
    cos32 = np.concatenate([np.ones((tm, ROPE_DIM)), cos32], axis=0)
    sin32 = np.concatenate([np.zeros((tm, ROPE_DIM)), sin32], axis=0)
    n = cos32.shape[0]
    scale = (MLA_NOPE + ROPE_DIM) ** -0.5 * math.log2(math.e)
    pad = HEAD_W - MLA_NOPE - ROPE_DIM
    cq = np.concatenate([np.ones((n, MLA_NOPE)), cos32, np.zeros((n, pad))], axis=-1) * scale
    sq = np.concatenate([np.zeros((n, MLA_NOPE)), sin32, np.zeros((n, pad))], axis=-1) * scale
    ck = np.concatenate([cos32, sin32, np.zeros((n, HEAD_W - 2 * ROPE_DIM))], axis=-1)
    return jnp.asarray(np.concatenate([cq, sq, ck], axis=-1).astype(f32))


def _pad_lanes(x, n):
    return jnp.pad(x, [(0, 0)] * (x.ndim - 1) + [(0, n - x.shape[-1])])


def kernel(x_prompt, x_sample, cache_mla, state_gla, state_gdn, c, c_ctx, w_mod, b_mod, norm_mix, w_in,
           mla_q_norm, mla_w_uq, mla_kv_norm, mla_w_ukv, gla_w_gate, gla_b_gate, gla_norm, gdn_conv,
           gdn_a_log, gdn_dt_bias, gdn_norm, w_branch, w_out, norm_ffn, ffn_w_in, ffn_w_out, final_norm):
    w_in_p = _pack_w_in(w_in)
    wq_p, wk_p, wv_p = _pack_mla_weights(mla_w_uq, mla_w_ukv)
    tab = _rope_tables(TM_PREP)
    wg_p = jnp.zeros((DEPTH, 2, LANE, GLA_HEADS * GLA_DK), F32)
    for d in range(2):
        wg_p = wg_p.at[:, d, d * GLA_RANK:(d + 1) * GLA_RANK, :].set(gla_w_gate[:, d])
    wg_p = wg_p.astype(BF16)
    bg_p = gla_b_gate.reshape(DEPTH, 2, 1, GLA_HEADS * GLA_DK)
    alog_p = _pad_lanes(gdn_a_log, LANE).reshape(DEPTH, 2, 1, LANE)
    dtb_p = _pad_lanes(gdn_dt_bias, LANE).reshape(DEPTH, 2, 1, LANE)
    gla_norm_p = jnp.tile(gla_norm, (1, GLA_HEADS)).reshape(DEPTH, 1, GLA_HEADS * GLA_DV)
    gdn_norm_p = jnp.tile(gdn_norm, (1, GDN_HEADS)).reshape(DEPTH, 1, GDN_HEADS * GDN_DV)
    w_branch_b = (0.5 * w_branch).astype(BF16)
    w_out_b = w_out.astype(BF16)
    wi_b = ffn_w_in.astype(BF16)
    wo_b = ffn_w_out.astype(BF16)
    cache_p = _pad_lanes(cache_mla, 2 * KV_LORA)
    cond8 = jnp.concatenate([c_ctx[None, :], c, jnp.zeros((MOD_ROWS - 1 - DEC_BATCH, D_MODEL), F32)], axis=0)

    mod = _modulation(cond8, w_mod, b_mod)
    mod3 = mod.reshape(DEPTH * MOD_ROWS * 6, 1, D_MODEL)
    kc, vc = _kv_cache(cache_p, wk_p, wv_p)

    h = (x_prompt.reshape(NCTX, D_MODEL), x_sample.reshape(NSMP, D_MODEL))
    kv_list, gla_list, gdn_list = [], [], []
    for l in range(DEPTH):
        proj = _inproj(h if l == 0 else xn, mod3, norm_mix, w_in_p, l)
        qp, kp, vtp, own = _mla_prep(proj, tab, mla_q_norm, mla_kv_norm, wq_p, wk_p, wv_p, l)
        y_mla = (_attention_ctx(qp, kp, vtp), _attention_smp(qp, kp, vtp, kc, vc, l))
        ogf_c, ogb_c, sg = _gla(proj, wg_p, bg_p, None, l, True)
        ogf_s, ogb_s = _gla(proj, wg_p, bg_p, state_gla, l, False)
        qkv = _gdn_pre(proj, gdn_conv, l)
        odf_c, odb_c, sd = _gdn(qkv, proj, alog_p, dtb_p, None, l, True)
        odf_s, odb_s = _gdn(qkv, proj, alog_p, dtb_p, state_gdn, l, False)
        h = _merge(y_mla, ((ogf_c, ogb_c), (ogf_s, ogb_s)), ((odf_c, odb_c), (odf_s, odb_s)), proj, h, mod3,
                   gla_norm_p, gdn_norm_p,
                   w_branch_b, w_out_b, l)
        if l == DEPTH - 1:
            h = _ffn(h, mod3, norm_ffn, wi_b, wo_b, final_norm, l, True)
        else:
            h, xn = _ffn(h, mod3, norm_ffn, wi_b, wo_b, norm_mix, l, False)
        kv_list.append(own[:NCTX, :KV_LORA + ROPE_DIM].reshape(BATCH, SEQ, KV_LORA + ROPE_DIM))
        gla_list.append(sg)
        gdn_list.append(sd)

    y_prompt = h[0].reshape(BATCH, SEQ, D_MODEL)
    y_sample = h[1].reshape(DEC_BATCH, DEC_SEQ, D_MODEL)
    return (y_prompt, y_sample, jnp.stack(kv_list, axis=1), jnp.stack(gla_list, axis=1),
            jnp.stack(gdn_list, axis=1))
```

```python
import functools
import math

import numpy as np
import jax
import jax.numpy as jnp
from jax import lax
from jax.experimental import pallas as pl
from jax.experimental.pallas import tpu as pltpu

F32 = jnp.float32
BF16 = jnp.bfloat16

D_MODEL = 1024
BATCH = 16
SEQ = 256
DEPTH = 2
DEC_BATCH = 2
DEC_SEQ = 4096
PAST_LEN = 256
GRID_W = 64
CHUNK = 64
EPS = 1e-6
MLA_HEADS = 8
MLA_NOPE = 64
ROPE_DIM = 32
AXIS_DIM = ROPE_DIM // 2
MLA_V = 64
Q_LORA = 256
KV_LORA = 128
ROPE_THETA = 10000.0
GLA_HEADS = 4
GLA_DK = 64
GLA_DV = 128
GLA_RANK = 16
GLA_TAU = 16.0
GDN_HEADS = 8
GDN_DK = 64
GDN_DV = 64
D_FF = ((8 * D_MODEL + 3 * 256 - 1) // (3 * 256)) * 256
MOD_W = 6 * D_MODEL

NCTX = BATCH * SEQ
NSMP = DEC_BATCH * DEC_SEQ
R = NCTX + NSMP
LANE = 128
HEAD_W = 128
VT_ROWS = MLA_HEADS * MLA_V
SEQ_BLK = 256
NCH = SEQ_BLK // CHUNK
GLA_BLK_CTX = 256
GLA_BLK_SMP = 1024
MOD_ROWS = 8

PC_GATES = 0
PC_DQKV = 3072
PC_DZ = 4608
PC_GV = 5120
PC_GR = 5632
PC_MLA = 6144
PC_GQ = 6656
PC_GK = 6912
PC_GLR = 7168
PC_DAB = 7296
PROJ_W = 7680

TM_IN = 1024
TN_IN = 2560
TM_PREP = 1024
GDN_PRE_BLK = 1024
TM_MERGE = 512
TM_FFN = 512
TF_FFN = 256
TQ_ATT = 256
TK_ATT = 256
ONES_ROWS = 16
ATT_AHEAD = 8
assert all(NCTX % t == 0 and NSMP % t == 0 for t in (TM_IN, TM_PREP, GDN_PRE_BLK, TM_MERGE, TM_FFN))
assert NCTX % DEC_SEQ == 0 and DEC_SEQ % TQ_ATT == 0 and DEC_SEQ % TK_ATT == 0 and PAST_LEN <= TK_ATT
assert SEQ % SEQ_BLK == 0 and DEC_SEQ % SEQ_BLK == 0 and SEQ % GLA_BLK_CTX == 0 and DEC_SEQ % GLA_BLK_SMP == 0
assert GDN_PRE_BLK % SEQ == 0 and DEC_SEQ % GDN_PRE_BLK == 0 and D_FF % TF_FFN == 0 and PROJ_W % TN_IN == 0

VMEM_V7X = 64 * 1024 * 1024
VMEM_LIMIT = VMEM_V7X * 3 // 4


def _cp(sem):
    return pltpu.CompilerParams(dimension_semantics=sem, vmem_limit_bytes=VMEM_LIMIT)


def _dot(a, b):
    return jnp.dot(a, b, preferred_element_type=F32)


def _dot_nt(a, b):
    return lax.dot_general(a, b, (((1,), (1,)), ((), ())), preferred_element_type=F32)


def _dot_tn(a, b):
    return lax.dot_general(a, b, (((0,), (0,)), ((), ())), preferred_element_type=F32)


def _split2(x):
    hi = x.astype(BF16)
    lo = (x - hi.astype(F32)).astype(BF16)
    return hi, lo


def _split3(x):
    hi = x.astype(BF16)
    r1 = x - hi.astype(F32)
    mid = r1.astype(BF16)
    lo = (r1 - mid.astype(F32)).astype(BF16)
    return hi, mid, lo


def _cumsum_rows(tri_b, x):
    hi, mid, lo = _split3(x)
    return _dot(tri_b, hi) + (_dot(tri_b, mid) + _dot(tri_b, lo))


def _sigmoid(x):
    return 0.5 * jnp.tanh(0.5 * x) + 0.5


def _silu(x):
    return x * _sigmoid(x)


def _silu_of_twice(hx):
    return hx + hx * jnp.tanh(hx)


def _softplus(x):
    return jnp.maximum(x, 0.0) + jnp.log(1.0 + jnp.exp(-jnp.abs(x)))


def _log_sigmoid(x):
    return jnp.minimum(x, 0.0) - jnp.log(1.0 + jnp.exp(-jnp.abs(x)))


def _rms(x, g):
    return x * lax.rsqrt(jnp.mean(x * x, axis=-1, keepdims=True) + EPS) * g


def _mod_row(row_start):
    return jnp.where(row_start < NCTX, 0, 1 + (row_start - NCTX) // DEC_SEQ)


def _mod_spec(layer, which, tm, axis=0):
    def imap(*idx):
        return ((layer * MOD_ROWS + _mod_row(idx[axis] * tm)) * 6 + which, 0, 0)
    return pl.BlockSpec((1, 1, D_MODEL), imap)


def _mod_kernel(c_ref, w_ref, b_ref, o_ref):
    x = _silu(c_ref[...])
    xh, xl = _split2(x)
    wh, wl = _split2(w_ref[0])
    o_ref[0] = _dot(xh, wh) + (_dot(xl, wh) + _dot(xh, wl)) + b_ref[0]


def _modulation(cond8, w_mod, b_mod):
    tn = 1536
    return pl.pallas_call(
        _mod_kernel,
        grid=(DEPTH, MOD_W // tn),
        in_specs=[pl.BlockSpec((MOD_ROWS, D_MODEL), lambda l, j: (0, 0)),
                  pl.BlockSpec((1, D_MODEL, tn), lambda l, j: (l, 0, j)),
                  pl.BlockSpec((1, 1, tn), lambda l, j: (l, 0, j))],
        out_specs=pl.BlockSpec((1, MOD_ROWS, tn), lambda l, j: (l, 0, j)),
        out_shape=jax.ShapeDtypeStruct((DEPTH, MOD_ROWS, MOD_W), F32),
        compiler_params=_cp(("arbitrary", "arbitrary")),
        name="modulation",
    )(cond8, w_mod, b_mod.reshape(DEPTH, 1, MOD_W))


def _family_specs(tm, width, axis=0):
    nc = NCTX // tm
    return [pl.BlockSpec((tm, width), lambda *idx: (jnp.minimum(idx[axis], nc - 1), 0)),
            pl.BlockSpec((tm, width), lambda *idx: (jnp.maximum(idx[axis] - nc, 0), 0))]


def _family_rows(refs, tm, axis=0):
    is_ctx = pl.program_id(axis) < NCTX // tm
    return jnp.where(is_ctx, refs[0][...], refs[1][...])


def _inproj_kernel(*refs, normed):
    if normed:
        xn_ref, w_ref, o_ref = refs
        xn = xn_ref[...]
    else:
        hc_ref, hs_ref, g_ref, sc_ref, sh_ref, w_ref, o_ref = refs
        y = _rms(_family_rows((hc_ref, hs_ref), TM_IN, axis=1), g_ref[0])
        xn = (y * (1.0 + sc_ref[0]) + sh_ref[0]).astype(BF16)
    o_ref[...] = _dot_nt(xn, w_ref[0]).astype(BF16)


def _inproj(h, mod3, norm_mix, w_in_p, layer):
    normed = not isinstance(h, tuple)
    if normed:
        in_specs = [pl.BlockSpec((TM_IN, D_MODEL), lambda j, i: (i, 0))]
        args = [h]
    else:
        in_specs = _family_specs(TM_IN, D_MODEL, axis=1) + [
            pl.BlockSpec((1, 1, D_MODEL), lambda j, i: (layer, 0, 0)),
            _mod_spec(layer, 1, TM_IN, axis=1),
            _mod_spec(layer, 0, TM_IN, axis=1)]
        args = list(h) + [norm_mix.reshape(DEPTH, 1, D_MODEL), mod3, mod3]
    return pl.pallas_call(
        functools.partial(_inproj_kernel, normed=normed),
        grid=(PROJ_W // TN_IN, R // TM_IN),
        in_specs=in_specs + [pl.BlockSpec((1, TN_IN, D_MODEL), lambda j, i: (layer, j, 0))],
        out_specs=pl.BlockSpec((TM_IN, TN_IN), lambda j, i: (i, j)),
        out_shape=jax.ShapeDtypeStruct((R, PROJ_W), BF16),
        compiler_params=_cp(("arbitrary", "arbitrary")),
        name="inproj",
    )(*args, w_in_p)


def _mla_prep_kernel(pm_ref, tab_ref, qn_ref, kvn_ref, wq_ref, wk_ref, wv_ref,
                     q_ref, k_ref, v_ref, own_ref):
    pm = pm_ref[...].astype(F32)
    tab = tab_ref[...]
    qn = _rms(pm[:, :Q_LORA], qn_ref[0]).astype(BF16)
    q2 = _dot(qn, wq_ref[0])
    hw = MLA_HEADS * HEAD_W
    cq = jnp.tile(tab[:, :HEAD_W], (1, MLA_HEADS))
    sq = jnp.tile(tab[:, HEAD_W:2 * HEAD_W], (1, MLA_HEADS))
    q_ref[...] = (q2[:, :hw] * cq + q2[:, hw:] * sq).astype(BF16)
    ckv = _rms(pm[:, Q_LORA:Q_LORA + KV_LORA], kvn_ref[0])
    kr = pm[:, Q_LORA + KV_LORA:] * tab[:, 2 * HEAD_W:]
    lhs = jnp.concatenate([ckv, kr], axis=1)
    own_ref[...] = lhs
    lb = lhs.astype(BF16)
    k_ref[...] = _dot(lb, wk_ref[0]).astype(BF16)
    v_ref[...] = _dot_nt(wv_ref[0], lb[:, :KV_LORA]).astype(BF16)


def _mla_prep(proj, tab, q_norm, kv_norm, wq_p, wk_p, wv_p, layer):
    tm = TM_PREP
    hw = MLA_HEADS * HEAD_W

    def tab_map(i):
        r0 = i * tm
        return (jnp.where(r0 < NCTX, 0, 1 + ((r0 - NCTX) % DEC_SEQ) // tm), 0)

    return pl.pallas_call(
        _mla_prep_kernel,
        grid=(R // tm,),
        in_specs=[pl.BlockSpec((tm, 512), lambda i: (i, PC_MLA // 512)),
                  pl.BlockSpec((tm, 3 * HEAD_W), tab_map),
                  pl.BlockSpec((1, 1, Q_LORA), lambda i: (layer, 0, 0)),
                  pl.BlockSpec((1, 1, KV_LORA), lambda i: (layer, 0, 0)),
                  pl.BlockSpec((1, Q_LORA, 2 * hw), lambda i: (layer, 0, 0)),
                  pl.BlockSpec((1, 2 * KV_LORA, hw), lambda i: (layer, 0, 0)),
                  pl.BlockSpec((1, VT_ROWS, KV_LORA), lambda i: (layer, 0, 0))],
        out_specs=[pl.BlockSpec((tm, hw), lambda i: (i, 0)),
                   pl.BlockSpec((tm, hw), lambda i: (i, 0)),
                   pl.BlockSpec((VT_ROWS, tm), lambda i: (0, i)),
                   pl.BlockSpec((tm, 2 * KV_LORA), lambda i: (i, 0))],
        out_shape=[jax.ShapeDtypeStruct((R, hw), BF16),
                   jax.ShapeDtypeStruct((R, hw), BF16),
                   jax.ShapeDtypeStruct((VT_ROWS, R), BF16),
                   jax.ShapeDtypeStruct((R, 2 * KV_LORA), F32)],
        compiler_params=_cp(("arbitrary",)),
        name="mla_prep",
    )(proj, tab, q_norm.reshape(DEPTH, 1, Q_LORA), kv_norm.reshape(DEPTH, 1, KV_LORA),
      wq_p, wk_p, wv_p)


def _kv_cache_kernel(c_ref, wk_ref, wv_ref, k_ref, v_ref):
    lb = c_ref[0, 0].astype(BF16)
    k_ref[0, 0] = _dot(lb, wk_ref[0]).astype(BF16)
    v_ref[0, 0] = _dot_nt(wv_ref[0], lb[:, :KV_LORA]).astype(BF16)


def _kv_cache(cache_p, wk_p, wv_p):
    hw = MLA_HEADS * HEAD_W
    return pl.pallas_call(
        _kv_cache_kernel,
        grid=(DEC_BATCH, DEPTH),
        in_specs=[pl.BlockSpec((1, 1, PAST_LEN, 2 * KV_LORA), lambda b, l: (b, l, 0, 0)),
                  pl.BlockSpec((1, 2 * KV_LORA, hw), lambda b, l: (l, 0, 0)),
                  pl.BlockSpec((1, VT_ROWS, KV_LORA), lambda b, l: (l, 0, 0))],
        out_specs=[pl.BlockSpec((1, 1, PAST_LEN, hw), lambda b, l: (b, l, 0, 0)),
                   pl.BlockSpec((1, 1, VT_ROWS, PAST_LEN), lambda b, l: (b, l, 0, 0))],
        out_shape=[jax.ShapeDtypeStruct((DEC_BATCH, DEPTH, PAST_LEN, hw), BF16),
                   jax.ShapeDtypeStruct((DEC_BATCH, DEPTH, VT_ROWS, PAST_LEN), BF16)],
        compiler_params=_cp(("arbitrary", "arbitrary")),
        name="kv_cache",
    )(cache_p, wk_p, wv_p)


def _attn_kernel(*refs, heads, has_cache):
    if has_cache:
        q_ref, k_ref, vt_ref, kc_ref, vct_ref, o_ref = refs
    else:
        q_ref, k_ref, vt_ref, o_ref = refs
    n_keys = k_ref.shape[0]
    tk = min(TK_ATT, n_keys)
    tiles = ([("cache", 0, PAST_LEN)] if has_cache else []) + [("own", s0, tk) for s0 in range(0, n_keys, tk)]
    units = [(h, t) for t in range(len(tiles)) for h in range(heads)]
    qs = [q_ref[:, h * HEAD_W:(h + 1) * HEAD_W] for h in range(heads)]

    def scores(h, t):
        kind, s0, n = tiles[t]
        sl = slice(h * HEAD_W, (h + 1) * HEAD_W)
        keys = kc_ref[0, 0, :, sl] if kind == "cache" else k_ref[s0:s0 + n, sl]
        st = _dot_nt(keys, qs[h])
        return st, jnp.max(st, axis=0, keepdims=True)

    def values_t(h, t):
        kind, s0, n = tiles[t]
        sv = slice(h * MLA_V, (h + 1) * MLA_V)
        vt = vct_ref[0, 0, sv, :] if kind == "cache" else vt_ref[sv, s0:s0 + n]
        return jnp.concatenate([vt, jnp.ones((ONES_ROWS, n), BF16)], axis=0)

    ahead = ATT_AHEAD
    pending = {u: scores(*u) for u in units[:ahead]}
    m, acc = [None] * heads, [None] * heads
    for i, (h, t) in enumerate(units):
        if i + ahead < len(units):
            pending[units[i + ahead]] = scores(*units[i + ahead])
        st, mt = pending.pop((h, t))
        if m[h] is None:
            m[h] = mt
            acc[h] = _dot(values_t(h, t), jnp.exp2(st - mt).astype(BF16))
        else:
            m_new = jnp.maximum(m[h], mt)
            alpha = jnp.exp2(m[h] - m_new)
            acc[h] = alpha * acc[h] + _dot(values_t(h, t), jnp.exp2(st - m_new).astype(BF16))
            m[h] = m_new
    outs = [acc[h][:MLA_V] / acc[h][MLA_V:MLA_V + 1] for h in range(heads)]
    o_ref[...] = jnp.concatenate(outs, axis=0).T.astype(BF16)


def _attention_ctx(qp, kp, vtp):
    hw = MLA_HEADS * HEAD_W
    return pl.pallas_call(
        functools.partial(_attn_kernel, heads=MLA_HEADS, has_cache=False),
        grid=(BATCH,),
        in_specs=[pl.BlockSpec((SEQ, hw), lambda s: (s, 0)),
                  pl.BlockSpec((SEQ, hw), lambda s: (s, 0)),
                  pl.BlockSpec((VT_ROWS, SEQ), lambda s: (0, s))],
        out_specs=pl.BlockSpec((SEQ, MLA_HEADS * MLA_V), lambda s: (s, 0)),
        out_shape=jax.ShapeDtypeStruct((NCTX, MLA_HEADS * MLA_V), BF16),
        compiler_params=_cp(("arbitrary",)),
        name="attn_ctx",
    )(qp, kp, vtp)


def _attention_smp(qp, kp, vtp, kc, vct, layer):
    hpb = 4
    wq = hpb * HEAD_W
    wv = hpb * MLA_V
    nq = DEC_SEQ // TQ_ATT
    qoff = NCTX // TQ_ATT
    koff = NCTX // DEC_SEQ
    return pl.pallas_call(
        functools.partial(_attn_kernel, heads=hpb, has_cache=True),
        grid=(DEC_BATCH, MLA_HEADS // hpb, nq),
        in_specs=[pl.BlockSpec((TQ_ATT, wq), lambda b, g, i: (qoff + b * nq + i, g)),
                  pl.BlockSpec((DEC_SEQ, wq), lambda b, g, i: (koff + b, g)),
                  pl.BlockSpec((wv, DEC_SEQ), lambda b, g, i: (g, koff + b)),
                  pl.BlockSpec((1, 1, PAST_LEN, wq), lambda b, g, i: (b, layer, 0, g)),
                  pl.BlockSpec((1, 1, wv, PAST_LEN), lambda b, g, i: (b, layer, g, 0))],
        out_specs=pl.BlockSpec((TQ_ATT, hpb * MLA_V), lambda b, g, i: (b * nq + i, g)),
        out_shape=jax.ShapeDtypeStruct((NSMP, MLA_HEADS * MLA_V), BF16),
        compiler_params=_cp(("arbitrary", "arbitrary", "arbitrary")),
        name="attn_smp",
    )(qp, kp, vtp, kc, vct)


def _scan_specs(ctx, blk, widths_and_cols, dir_cols=()):
    nseq, seq_len = (BATCH, SEQ) if ctx else (DEC_BATCH, DEC_SEQ)
    nblk = seq_len // blk
    roff = 0 if ctx else NCTX // blk

    def rb(dd):
        return lambda s, j: roff + s * nblk + (j if dd == 0 else nblk - 1 - j)

    specs = []
    for dd in range(2):
        r = rb(dd)
        specs += [pl.BlockSpec((blk, w), lambda s, j, r=r, c=c: (r(s, j), c)) for w, c in widths_and_cols]
        specs += [pl.BlockSpec((blk, w), lambda s, j, r=r, c=c + dd: (r(s, j), c)) for w, c in dir_cols]
    out_maps = [lambda s, j, r=rb(dd): (r(s, j) - roff, 0) for dd in range(2)]
    return nseq, nblk, specs, out_maps


def _gla_kernel(*refs, nblk, has_s0, nch):
    per_dir = [refs[0:4], refs[4:8]]
    wg_ref, bg_ref = refs[8:10]
    if has_s0:
        s0_ref, of_ref, ob_ref, s_scr = refs[10:]
    else:
        of_ref, ob_ref, sfin_ref, s_scr = refs[10:]
    o_refs = [of_ref, ob_ref]
    j = pl.program_id(1)

    @pl.when(j == 0)
    def _():
        if has_s0:
            s_scr[...] = s0_ref[0, 0]
        else:
            s_scr[...] = jnp.zeros_like(s_scr)

    r64 = lax.broadcasted_iota(jnp.int32, (CHUNK, CHUNK), 0)
    c64 = lax.broadcasted_iota(jnp.int32, (CHUNK, CHUNK), 1)
    incl = [c64 <= r64, c64 >= r64]
    tri_b = [jnp.where(m, 1.0, 0.0).astype(BF16) for m in incl]
    heads = [(slice(h * GLA_DK, (h + 1) * GLA_DK), slice(h * GLA_DV, (h + 1) * GLA_DV))
             for h in range(GLA_HEADS)]

    ch = []
    for c in range(nch):
        for dd in range(2):
            ci = c if dd == 0 else nch - 1 - c
            rows = slice(ci * CHUNK, (ci + 1) * CHUNK)
            glr_ref = per_dir[dd][3]
            ch.append(dict(dd=dd, rows=rows, z=_dot(glr_ref[rows, :], wg_ref[0, dd]) + bg_ref[0, dd]))
    for t in ch:
        t["la"] = _split3(_log_sigmoid(t["z"]) * (1.0 / GLA_TAU))
    for t in ch:
        hi, mid, lo = t["la"]
        tb = tri_b[t["dd"]]
        t["bc"] = _dot(tb, hi) + (_dot(tb, mid) + _dot(tb, lo))
    for t in ch:
        q_ref, k_ref, v_ref, _ = per_dir[t["dd"]]
        rows, bc = t["rows"], t["bc"]
        q = q_ref[rows, :].astype(F32) * (GLA_DK ** -0.5)
        k = k_ref[rows, :].astype(F32)
        bl = bc[CHUNK - 1:CHUNK] if t["dd"] == 0 else bc[0:1]
        t["qd"] = (q * jnp.exp(bc)).astype(BF16)
        t["ki"] = (k * jnp.exp(-bc)).astype(BF16)
        t["ke"] = (k * jnp.exp(bl - bc)).astype(BF16)
        t["dec_t"] = jnp.transpose(jnp.broadcast_to(jnp.exp(bl), (LANE, GLA_HEADS * GLA_DK)))
        t["v"] = v_ref[rows, :]
    for t in ch:
        t["a"] = [jnp.where(incl[t["dd"]], _dot_nt(t["qd"][:, sk], t["ki"][:, sk]), 0.0).astype(BF16)
                  for sk, _ in heads]
    for t in ch:
        t["oi"] = [_dot(t["a"][h], t["v"][:, sv]) for h, (_, sv) in enumerate(heads)]
        t["upd"] = [_dot_tn(t["ke"][:, sk], t["v"][:, sv]) for sk, sv in heads]
    states = [[s_scr[dd, h] for h in range(GLA_HEADS)] for dd in range(2)]
    for t in ch:
        st = states[t["dd"]]
        t["s_in"] = [s.astype(BF16) for s in st]
        for h, (sk, _) in enumerate(heads):
            st[h] = t["dec_t"][sk, :] * st[h] + t["upd"][h]
    for t in ch:
        for h, (sk, sv) in enumerate(heads):
            o_refs[t["dd"]][t["rows"], sv] = t["oi"][h] + _dot(t["qd"][:, sk], t["s_in"][h])
    for dd in range(2):
        for h in range(GLA_HEADS):
            s_scr[dd, h] = states[dd][h]

    if not has_s0:
        @pl.when(j == nblk - 1)
        def _():
            sfin_ref[0] = s_scr[...]


def _gla(proj, wg_p, bg_p, s0, layer, ctx):
    blk = GLA_BLK_CTX if ctx else GLA_BLK_SMP
    hk = GLA_HEADS * GLA_DK
    hv = GLA_HEADS * GLA_DV
    nseq, nblk, in_specs, out_maps = _scan_specs(
        ctx, blk, [(hk, PC_GQ // hk), (hk, PC_GK // hk), (hv, PC_GV // hv), (LANE, PC_GLR // LANE)])
    in_specs += [pl.BlockSpec((1, 2, LANE, hk), lambda s, j: (layer, 0, 0, 0)),
                 pl.BlockSpec((1, 2, 1, hk), lambda s, j: (layer, 0, 0, 0))]
    args = [proj] * 8 + [wg_p, bg_p]
    nrows = NCTX if ctx else NSMP
    out_specs = [pl.BlockSpec((blk, hv), m) for m in out_maps]
    out_shape = [jax.ShapeDtypeStruct((nrows, hv), F32)] * 2
    st_blk = (2, GLA_HEADS, GLA_DK, GLA_DV)
    if ctx:
        out_specs.append(pl.BlockSpec((1,) + st_blk, lambda s, j: (s, 0, 0, 0, 0)))
        out_shape.append(jax.ShapeDtypeStruct((BATCH,) + st_blk, F32))
    else:
        in_specs.append(pl.BlockSpec((1, 1) + st_blk, lambda s, j: (s, layer, 0, 0, 0, 0)))
        args.append(s0)
    return pl.pallas_call(
        functools.partial(_gla_kernel, nblk=nblk, has_s0=not ctx, nch=blk // CHUNK),
        grid=(nseq, nblk),
        in_specs=in_specs, out_specs=out_specs, out_shape=out_shape,
        scratch_shapes=[pltpu.VMEM(st_blk, F32)],
        compiler_params=_cp(("arbitrary", "arbitrary")),
        name="gla_ctx" if ctx else "gla_smp",
    )(*args)


GDN_G = 2
GDN_GW = GDN_G * GDN_DK
HALO = 16


def _gdn_pre_kernel(x_ref, xp_ref, xn_ref, cw_ref, o_ref):
    blk = x_ref.shape[0]
    hk = GDN_HEADS * GDN_DK
    sub = SEQ
    cw = 0.5 * cw_ref[0]
    rowi = lax.broadcasted_iota(jnp.int32, (sub, 1), 0)
    lane = lax.broadcasted_iota(jnp.int32, (sub, LANE), 1)
    lo_half = lane < GDN_DK

    def l2n(t, scale):
        parts = []
        for c in range(hk // LANE):
            tc = t[:, c * LANE:(c + 1) * LANE]
            sq = tc * tc
            s_lo = jnp.sum(jnp.where(lo_half, sq, 0.0), axis=-1, keepdims=True)
            s_hi = jnp.sum(jnp.where(lo_half, 0.0, sq), axis=-1, keepdims=True)
            parts.append(tc * (lax.rsqrt(jnp.where(lo_half, s_lo, s_hi) + EPS) * scale))
        return jnp.concatenate(parts, axis=1)

    for s in range(blk // sub):
        rows = slice(s * sub, (s + 1) * sub)
        x = x_ref[rows, :].astype(F32)
        r0 = pl.program_id(0) * blk + s * sub
        pos0 = jnp.where(r0 < NCTX, r0 % SEQ, (r0 - NCTX) % DEC_SEQ)
        seq_len = jnp.where(r0 < NCTX, SEQ, DEC_SEQ)
        before = xp_ref[...] if s == 0 else x_ref[s * sub - HALO:s * sub, :]
        after = xn_ref[...] if s == blk // sub - 1 else x_ref[(s + 1) * sub:(s + 1) * sub + HALO, :]
        prev_row = jnp.where(pos0 > 0, before.astype(F32)[HALO - 1:HALO], 0.0)
        next_row = jnp.where(pos0 + sub < seq_len, after.astype(F32)[0:1], 0.0)
        x_prev = jnp.where(rowi == 0, prev_row, pltpu.roll(x, 1, 0))
        x_next = jnp.where(rowi == sub - 1, next_row, pltpu.roll(x, sub - 1, 0))
        y = _silu_of_twice(x_prev * cw[0:1] + x * cw[1:2] + x_next * cw[2:3])
        o_ref[rows, :hk] = l2n(y[:, :hk], GDN_DK ** -0.5).astype(BF16)
        o_ref[rows, hk:2 * hk] = l2n(y[:, hk:2 * hk], 1.0).astype(BF16)
        o_ref[rows, 2 * hk:] = y[:, 2 * hk:].astype(BF16)


def _gdn_pre(proj, conv_w, layer):
    cw = 2 * GDN_HEADS * GDN_DK + GDN_HEADS * GDN_DV
    blk = GDN_PRE_BLK
    halo = blk // HALO
    last_h = R // HALO - 1
    return pl.pallas_call(
        _gdn_pre_kernel,
        grid=(R // blk,),
        in_specs=[pl.BlockSpec((blk, cw), lambda i: (i, PC_DQKV // cw)),
                  pl.BlockSpec((HALO, cw), lambda i: (jnp.maximum(i * halo - 1, 0), PC_DQKV // cw)),
                  pl.BlockSpec((HALO, cw), lambda i: (jnp.minimum((i + 1) * halo, last_h), PC_DQKV // cw)),
                  pl.BlockSpec((1, 3, cw), lambda i: (layer, 0, 0))],
        out_specs=pl.BlockSpec((blk, cw), lambda i: (i, 0)),
        out_shape=jax.ShapeDtypeStruct((R, cw), BF16),
        compiler_params=_cp(("arbitrary",)),
        name="gdn_pre",
    )(proj, proj, proj, conv_w)


def _gdn_kernel(*refs, nblk, has_s0):
    per_dir = [refs[0:4], refs[4:8]]
    alog_ref, dtb_ref = refs[8:10]
    if has_s0:
        s0_ref, of_ref, ob_ref, s_scr = refs[10:]
    else:
        of_ref, ob_ref, sfin_ref, s_scr = refs[10:]
    o_refs = [of_ref, ob_ref]
    j = pl.program_id(1)
    ngrp = GDN_HEADS // GDN_G

    def head_block(h):
        g, hh = divmod(h, GDN_G)
        return g, slice(hh * GDN_DK, (hh + 1) * GDN_DK), slice(hh * GDN_DV, (hh + 1) * GDN_DV)

    @pl.when(j == 0)
    def _():
        s_scr[...] = jnp.zeros_like(s_scr)
        if has_s0:
            for dd in range(2):
                for h in range(GDN_HEADS):
                    g, rk, rv = head_block(h)
                    s_scr[dd, g, rk, rv] = s0_ref[0, 0, dd, h]

    row = lax.broadcasted_iota(jnp.int32, (CHUNK, GDN_GW), 0)
    col = lax.broadcasted_iota(jnp.int32, (CHUNK, GDN_GW), 1) % CHUNK
    r64 = lax.broadcasted_iota(jnp.int32, (CHUNK, CHUNK), 0)
    c64 = lax.broadcasted_iota(jnp.int32, (CHUNK, CHUNK), 1)
    dir_masks = []
    for dd in range(2):
        lag = (row - col) if dd == 0 else (col - row)
        tri = (c64 <= r64) if dd == 0 else (c64 >= r64)
        dir_masks.append(dict(incl=lag >= 0, strict=lag > 0,
                              tri_b=jnp.where(tri, 1.0, 0.0).astype(BF16)))
    eye = col == row
    eye_f = jnp.where(eye, 1.0, 0.0)
    blk8 = (row // 8) == (col // 8)
    blk16 = (row // 16) == (col // 16)
    blk32 = (row // 32) == (col // 32)
    brow = lax.broadcasted_iota(jnp.int32, (GDN_GW, GDN_GW), 0) // CHUNK
    bcol = lax.broadcasted_iota(jnp.int32, (GDN_GW, GDN_GW), 1) // CHUNK
    same_head = brow == bcol

    def bdiag(t):
        tb = t.astype(BF16)
        return jnp.where(same_head, jnp.concatenate([tb] * GDN_G, axis=0), jnp.zeros((), BF16))

    def bdmm(a, b):
        return _dot(a.astype(BF16), bdiag(b))

    def setup(dd, c):
        q_ref, k_ref, v_ref, ab_ref = per_dir[dd]
        mk = dir_masks[dd]
        ci = c if dd == 0 else NCH - 1 - c
        rows = slice(ci * CHUNK, (ci + 1) * CHUNK)
        ab = ab_ref[rows, :].astype(F32)
        neg_a = -jnp.exp(alog_ref[0, dd])
        gam = _cumsum_rows(mk["tri_b"], neg_a * _softplus(ab + dtb_ref[0, dd]))
        beta_all = _sigmoid(ab)
        out = []
        for g in range(ngrp):
            gl = slice(g * GDN_GW, (g + 1) * GDN_GW)
            gcol = jnp.concatenate(
                [jnp.broadcast_to(gam[:, g * GDN_G + h:g * GDN_G + h + 1], (CHUNK, GDN_DK))
                 for h in range(GDN_G)], axis=1)
            bcol_ = jnp.concatenate(
                [jnp.broadcast_to(beta_all[:, GDN_HEADS + g * GDN_G + h:GDN_HEADS + g * GDN_G + h + 1],
                                  (CHUNK, GDN_DK)) for h in range(GDN_G)], axis=1)
            grow = jnp.sum(jnp.where(eye, gcol, 0.0), axis=0, keepdims=True)
            glast = gcol[CHUNK - 1:CHUNK] if dd == 0 else gcol[0:1]
            decay = jnp.where(mk["incl"], jnp.exp(gcol - grow), 0.0)
            egc = jnp.exp(gcol)
            qg = q_ref[rows, gl].astype(F32)
            kg = k_ref[rows, gl].astype(F32)
            vg = v_ref[rows, gl].astype(F32)
            kq = _dot_nt(jnp.concatenate([k_ref[rows, gl], q_ref[rows, gl]], axis=0),
                         bdiag(k_ref[rows, gl]))
            out.append(dict(dd=dd, g=g, rows=rows, gl=gl,
                            m=jnp.where(mk["strict"], bcol_ * kq[:CHUNK] * decay, 0.0),
                            aqk=(kq[CHUNK:] * decay).astype(BF16),
                            vb=vg * bcol_, kb=kg * (bcol_ * egc), qe=qg * egc,
                            kend=(kg * jnp.exp(glast - gcol)).astype(BF16),
                            eg=jnp.exp(glast)))
        return out

    def stages(ch):
        pairs = list(zip(ch[0::2], ch[1::2]))

        def bdmm_pairs(lhs, rhs, out):
            for ta, tb in pairs:
                a = jnp.concatenate([ta[lhs].astype(BF16), tb[lhs].astype(BF16)], axis=0)
                w = jnp.concatenate([bdiag(ta[rhs]), bdiag(tb[rhs])], axis=1)
                y = _dot(a, w)
                ta[out] = y[:CHUNK, :GDN_GW]
                tb[out] = y[CHUNK:, GDN_GW:]

        def neumann_a():
            for t in ch:
                t["n8"] = jnp.where(blk8, t["m"], 0.0)
            bdmm_pairs("n8", "n8", "n2")

        def neumann_b():
            for t in ch:
                t["p1"] = eye_f - t["n8"]
            bdmm_pairs("n2", "n2", "n4")
            bdmm_pairs("p1", "n2", "p1n2")
            for t in ch:
                t["p1"] = t["p1"] + t["p1n2"]

        def neumann_c():
            bdmm_pairs("p1", "n4", "p1n4")
            for t in ch:
                t["dinv"] = t["p1"] + t["p1n4"]

        def double_a(off):
            def run():
                for t in ch:
                    t["off"] = jnp.where(off, t["m"], 0.0)
                bdmm_pairs("dinv", "off", "dl")
            return run

        def double_b():
            bdmm_pairs("dl", "dinv", "dld")
            for t in ch:
                t["dinv"] = t["dinv"] - t["dld"]

        def solve():
            for t in ch:
                uw = _dot(t["dinv"].astype(BF16), jnp.concatenate([bdiag(t["vb"]), bdiag(t["kb"])], axis=1))
                t["u"] = uw[:, :GDN_GW]
                t["w"] = uw[:, GDN_GW:]

        def fold():
            for t in ch:
                wu = jnp.concatenate([t["w"], t["u"]], axis=1).astype(BF16)
                cb = _dot_tn(t["kend"], wu)
                t["c"] = jnp.where(same_head, cb[:, :GDN_GW], 0.0).astype(BF16)
                t["b"] = jnp.where(same_head, cb[:, GDN_GW:], 0.0)
                ao = _dot(t["aqk"], jnp.concatenate([bdiag(t["w"]), bdiag(t["u"])], axis=1))
                t["qt"] = (t["qe"] - ao[:, :GDN_GW]).astype(BF16)
                t["o"] = ao[:, GDN_GW:]

        out = [neumann_a, neumann_b, neumann_c]
        for inner, outer in ((blk8, blk16), (blk16, blk32), (blk32, None)):
            off = jnp.logical_not(inner) if outer is None else jnp.logical_and(outer, jnp.logical_not(inner))
            out += [double_a(off), double_b]
        return out + [solve, fold]

    states = [[s_scr[dd, g] for g in range(ngrp)] for dd in range(2)]

    def recur(cur):
        ys = [_dot(jnp.concatenate([t["qt"], t["c"]], axis=0), states[t["dd"]][t["g"]].astype(BF16))
              for t in cur]
        for t, y in zip(cur, ys):
            o_refs[t["dd"]][t["rows"], t["gl"]] = y[:CHUNK] + t["o"]
            states[t["dd"]][t["g"]] = t["eg"] * states[t["dd"]][t["g"]] - y[CHUNK:] + t["b"]

    half = NCH // 2
    chains = {}
    for c in range(half):
        for dd in range(2):
            chains[(dd, c)] = setup(dd, c)
    later = [(dd, c) for c in range(half, NCH) for dd in range(2)]
    wave_a = [t for c in range(half) for dd in range(2) for t in chains[(dd, c)]]
    for k, stage in enumerate(stages(wave_a)):
        stage()
        if k % 2 == 0 and later:
            key = later.pop(0)
            chains[key] = setup(*key)
    for key in later:
        chains[key] = setup(*key)
    wave_b = [t for c in range(half, NCH) for dd in range(2) for t in chains[(dd, c)]]
    done = 0
    for k, stage in enumerate(stages(wave_b)):
        stage()
        if k % 3 == 1 and done < half:
            recur(chains[(0, done)] + chains[(1, done)])
            done += 1
    for c in range(done, NCH):
        recur(chains[(0, c)] + chains[(1, c)])
    for dd in range(2):
        for g in range(ngrp):
            s_scr[dd, g] = states[dd][g]

    if not has_s0:
        @pl.when(j == nblk - 1)
        def _():
            for dd in range(2):
                for h in range(GDN_HEADS):
                    g, rk, rv = head_block(h)
                    sfin_ref[0, dd, h] = s_scr[dd, g, rk, rv]


def _gdn(qkv, proj, alog_p, dtb_p, s0, layer, ctx):
    nseq, nblk = (BATCH, SEQ // SEQ_BLK) if ctx else (DEC_BATCH, DEC_SEQ // SEQ_BLK)
    roff = 0 if ctx else NCTX // SEQ_BLK
    hk = GDN_HEADS * GDN_DK
    hv = GDN_HEADS * GDN_DV

    def rb(dd):
        return lambda s, j: roff + s * nblk + (j if dd == 0 else nblk - 1 - j)

    in_specs, args = [], []
    for dd in range(2):
        r = rb(dd)
        in_specs += [pl.BlockSpec((SEQ_BLK, hk), lambda s, j, r=r: (r(s, j), 0)),
                     pl.BlockSpec((SEQ_BLK, hk), lambda s, j, r=r: (r(s, j), 1)),
                     pl.BlockSpec((SEQ_BLK, hv), lambda s, j, r=r: (r(s, j), 2 * hk // hv)),
                     pl.BlockSpec((SEQ_BLK, LANE), lambda s, j, r=r, dd=dd: (r(s, j), PC_DAB // LANE + dd))]
        args += [qkv, qkv, qkv, proj]
    in_specs += [pl.BlockSpec((1, 2, 1, LANE), lambda s, j: (layer, 0, 0, 0))] * 2
    args += [alog_p, dtb_p]
    nrows = NCTX if ctx else NSMP
    out_specs = [pl.BlockSpec((SEQ_BLK, hv), lambda s, j, r=rb(dd): (r(s, j) - roff, 0)) for dd in range(2)]
    out_shape = [jax.ShapeDtypeStruct((nrows, hv), F32)] * 2
    st_blk = (2, GDN_HEADS, GDN_DK, GDN_DV)
    if ctx:
        out_specs.append(pl.BlockSpec((1,) + st_blk, lambda s, j: (s, 0, 0, 0, 0)))
        out_shape.append(jax.ShapeDtypeStruct((BATCH,) + st_blk, F32))
    else:
        in_specs.append(pl.BlockSpec((1, 1) + st_blk, lambda s, j: (s, layer, 0, 0, 0, 0)))
        args.append(s0)
    return pl.pallas_call(
        functools.partial(_gdn_kernel, nblk=nblk, has_s0=not ctx),
        grid=(nseq, nblk),
        in_specs=in_specs, out_specs=out_specs, out_shape=out_shape,
        scratch_shapes=[pltpu.VMEM((2, GDN_HEADS // GDN_G, GDN_GW, GDN_GW), F32)],
        compiler_params=_cp(("arbitrary", "arbitrary")),
        name="gdn_ctx" if ctx else "gdn_smp",
    )(*args)


def _group_rms(x, width):
    parts = []
    lane = lax.broadcasted_iota(jnp.int32, (x.shape[0], LANE), 1)
    lo_half = lane < 64
    for c in range(x.shape[1] // LANE):
        xc = x[:, c * LANE:(c + 1) * LANE]
        sq = xc * xc
        if width == LANE:
            ms = jnp.mean(sq, axis=-1, keepdims=True)
        else:
            s_lo = jnp.sum(jnp.where(lo_half, sq, 0.0), axis=-1, keepdims=True)
            s_hi = jnp.sum(jnp.where(lo_half, 0.0, sq), axis=-1, keepdims=True)
            ms = jnp.where(lo_half, s_lo, s_hi) * (1.0 / width)
        parts.append(xc * lax.rsqrt(ms + EPS))
    return jnp.concatenate(parts, axis=1)


def _merge_kernel(ymc_ref, yms_ref, ogcf_ref, ogcb_ref, ogsf_ref, ogsb_ref, gr_ref,
                  odcf_ref, odcb_ref, odsf_ref, odsb_ref,
                  dz_ref, gt_ref, *rest, split_h):
    if split_h:
        hc_ref, hs_ref, *rest = rest
        h_rows = lambda: _family_rows((hc_ref, hs_ref), TM_MERGE)
    else:
        h_ref, *rest = rest
        h_rows = lambda: h_ref[...]
    ga_ref, gn_ref, dn_ref, wb_ref, wo_ref, o_ref, ym_scr, og_scr, od_scr = rest
    is_ctx = pl.program_id(0) < NCTX // TM_MERGE

    @pl.when(is_ctx)
    def _():
        ym_scr[...] = ymc_ref[...]
        og_scr[...] = ogcf_ref[...] + ogcb_ref[...]
        od_scr[...] = odcf_ref[...] + odcb_ref[...]

    @pl.when(jnp.logical_not(is_ctx))
    def _():
        ym_scr[...] = yms_ref[...]
        og_scr[...] = ogsf_ref[...] + ogsb_ref[...]
        od_scr[...] = odsf_ref[...] + odsb_ref[...]

    y_gla = _group_rms(og_scr[...], GLA_DV) * gn_ref[0] * _silu_of_twice(gr_ref[...].astype(F32))
    y_gdn = _group_rms(od_scr[...], GDN_DV) * dn_ref[0] * _silu_of_twice(dz_ref[...].astype(F32))
    t = jnp.tanh(gt_ref[...].astype(F32))
    d = D_MODEL
    acc = None
    for n, y_n in enumerate((ym_scr[...], y_gla.astype(BF16), y_gdn.astype(BF16))):
        p = _dot(y_n, wb_ref[0, n])
        gated = p + p * t[:, n * d:(n + 1) * d]
        acc = gated if acc is None else acc + gated
    y = _dot(acc.astype(BF16), wo_ref[0])
    o_ref[...] = h_rows() + ga_ref[0] * y


def _merge(y_mla, o_gla, o_gdn, proj, h, mod3, gla_norm_p, gdn_norm_p, w_branch_b, w_out_b, layer):
    tm = TM_MERGE
    bw = 512
    nc = NCTX // tm
    split_h = isinstance(h, tuple)
    h_specs = _family_specs(tm, D_MODEL) if split_h else [pl.BlockSpec((tm, D_MODEL), lambda i: (i, 0))]
    h_args = list(h) if split_h else [h]

    def cmap(i):
        return jnp.minimum(i, nc - 1)

    def smap(i):
        return jnp.maximum(i - nc, 0)

    return pl.pallas_call(
        functools.partial(_merge_kernel, split_h=split_h),
        grid=(R // tm,),
        in_specs=[pl.BlockSpec((tm, bw), lambda i: (cmap(i), 0)),
                  pl.BlockSpec((tm, bw), lambda i: (smap(i), 0)),
                  pl.BlockSpec((tm, bw), lambda i: (cmap(i), 0)),
                  pl.BlockSpec((tm, bw), lambda i: (cmap(i), 0)),
                  pl.BlockSpec((tm, bw), lambda i: (smap(i), 0)),
                  pl.BlockSpec((tm, bw), lambda i: (smap(i), 0)),
                  pl.BlockSpec((tm, bw), lambda i: (i, PC_GR // bw)),
                  pl.BlockSpec((tm, bw), lambda i: (cmap(i), 0)),
                  pl.BlockSpec((tm, bw), lambda i: (cmap(i), 0)),
                  pl.BlockSpec((tm, bw), lambda i: (smap(i), 0)),
                  pl.BlockSpec((tm, bw), lambda i: (smap(i), 0)),
                  pl.BlockSpec((tm, bw), lambda i: (i, PC_DZ // bw)),
                  pl.BlockSpec((tm, 3 * D_MODEL), lambda i: (i, PC_GATES // (3 * D_MODEL)))] + h_specs + [
                  _mod_spec(layer, 2, tm),
                  pl.BlockSpec((1, 1, bw), lambda i: (layer, 0, 0)),
                  pl.BlockSpec((1, 1, bw), lambda i: (layer, 0, 0)),
                  pl.BlockSpec((1, 3, bw, D_MODEL), lambda i: (layer, 0, 0, 0), pipeline_mode=pl.Buffered(1)),
                  pl.BlockSpec((1, D_MODEL, D_MODEL), lambda i: (layer, 0, 0), pipeline_mode=pl.Buffered(1))],
        out_specs=pl.BlockSpec((tm, D_MODEL), lambda i: (i, 0)),
        out_shape=jax.ShapeDtypeStruct((R, D_MODEL), F32),
        scratch_shapes=[pltpu.VMEM((tm, bw), BF16), pltpu.VMEM((tm, bw), F32), pltpu.VMEM((tm, bw), F32)],
        compiler_params=_cp(("arbitrary",)),
        name="merge",
    )(y_mla[0], y_mla[1], o_gla[0][0], o_gla[0][1], o_gla[1][0], o_gla[1][1], proj,
      o_gdn[0][0], o_gdn[0][1], o_gdn[1][0], o_gdn[1][1],
      proj, proj, *h_args, mod3, gla_norm_p, gdn_norm_p, w_branch_b, w_out_b)


def _ffn_kernel(h_ref, g_ref, sc_ref, sh_ref, gf_ref, wi_ref, wo_ref, *rest, final):
    if final:
        fn_ref, *o_refs = rest
    else:
        gn_ref, scn_ref, shn_ref, *o_refs = rest
    h = h_ref[...]
    xf = (_rms(h, g_ref[0]) * (1.0 + sc_ref[0]) + sh_ref[0]).astype(BF16)

    def gate_up(c):
        f0 = c * TF_FFN
        return (_dot(xf, wi_ref[0, :, f0:f0 + TF_FFN]),
                _dot(xf, wi_ref[0, :, D_FF + f0:D_FF + f0 + TF_FFN]))

    n_chunks = D_FF // TF_FFN
    nxt = gate_up(0)
    acc = None
    for c in range(n_chunks):
        g, u = nxt
        if c + 1 < n_chunks:
            nxt = gate_up(c + 1)
        part = _dot((_silu(g) * u).astype(BF16), wo_ref[0, c * TF_FFN:(c + 1) * TF_FFN, :])
        acc = part if acc is None else acc + part
    out = h + gf_ref[0] * acc
    if not final:
        o_refs[0][...] = out
        o_refs[1][...] = (_rms(out, gn_ref[0]) * (1.0 + scn_ref[0]) + shn_ref[0]).astype(BF16)
    else:
        out = _rms(out, fn_ref[...])
        is_ctx = pl.program_id(0) < NCTX // TM_FFN

        @pl.when(is_ctx)
        def _():
            o_refs[0][...] = out

        @pl.when(jnp.logical_not(is_ctx))
        def _():
            o_refs[1][...] = out


def _ffn(h, mod3, norm_ffn, wi_b, wo_b, tail, layer, final):
    tm = TM_FFN
    if final:
        nc = NCTX // tm
        out_specs = [pl.BlockSpec((tm, D_MODEL), lambda i: (jnp.minimum(i, nc - 1), 0)),
                     pl.BlockSpec((tm, D_MODEL), lambda i: (jnp.maximum(i - nc, 0), 0))]
        out_shape = [jax.ShapeDtypeStruct((NCTX, D_MODEL), F32), jax.ShapeDtypeStruct((NSMP, D_MODEL), F32)]
        tail_specs = [pl.BlockSpec((1, D_MODEL), lambda i: (0, 0))]
        tail_args = [tail.reshape(1, D_MODEL)]
    else:
        out_specs = [pl.BlockSpec((tm, D_MODEL), lambda i: (i, 0))] * 2
        out_shape = [jax.ShapeDtypeStruct((R, D_MODEL), F32), jax.ShapeDtypeStruct((R, D_MODEL), BF16)]
        tail_specs = [pl.BlockSpec((1, 1, D_MODEL), lambda i: (layer + 1, 0, 0)),
                      _mod_spec(layer + 1, 1, tm), _mod_spec(layer + 1, 0, tm)]
        tail_args = [tail.reshape(DEPTH, 1, D_MODEL), mod3, mod3]
    return pl.pallas_call(
        functools.partial(_ffn_kernel, final=final),
        grid=(R // tm,),
        in_specs=[pl.BlockSpec((tm, D_MODEL), lambda i: (i, 0)),
                  pl.BlockSpec((1, 1, D_MODEL), lambda i: (layer, 0, 0)),
                  _mod_spec(layer, 4, tm),
                  _mod_spec(layer, 3, tm),
                  _mod_spec(layer, 5, tm),
                  pl.BlockSpec((1, D_MODEL, 2 * D_FF), lambda i: (layer, 0, 0), pipeline_mode=pl.Buffered(1)),
                  pl.BlockSpec((1, D_FF, D_MODEL), lambda i: (layer, 0, 0), pipeline_mode=pl.Buffered(1))]
        + tail_specs,
        out_specs=out_specs, out_shape=out_shape,
        compiler_params=_cp(("arbitrary",)),
        name="ffn_final" if final else "ffn",
    )(h, norm_ffn.reshape(DEPTH, 1, D_MODEL), mod3, mod3, mod3, wi_b, wo_b, *tail_args)


def _rope_partner(w, axis=-1):
    h = AXIS_DIM // 2
    axis = axis % w.ndim
    parts = []
    for a in range(2):
        x1 = lax.slice_in_dim(w, a * AXIS_DIM, a * AXIS_DIM + h, axis=axis)
        x2 = lax.slice_in_dim(w, a * AXIS_DIM + h, (a + 1) * AXIS_DIM, axis=axis)
        parts += [-x2, x1]
    return jnp.concatenate(parts, axis=axis)


def _pack_w_in_kernel(w_ref, o_ref):
    o_ref[0] = _pack_channels(w_ref[0]).astype(BF16)


def _pack_w_in(w_in):
    tc = 256
    return pl.pallas_call(
        _pack_w_in_kernel,
        grid=(DEPTH, D_MODEL // tc),
        in_specs=[pl.BlockSpec((1, w_in.shape[-1], tc), lambda l, i: (l, 0, i))],
        out_specs=pl.BlockSpec((1, PROJ_W, tc), lambda l, i: (l, 0, i)),
        out_shape=jax.ShapeDtypeStruct((DEPTH, PROJ_W, D_MODEL), BF16),
        compiler_params=_cp(("arbitrary", "arbitrary")),
        name="pack_w_in",
    )(jnp.swapaxes(w_in, 1, 2))


def _pack_channels(wt):
    c = [0]
    for s in (Q_LORA, KV_LORA + ROPE_DIM, 256, 256, 512, 512, 2 * GLA_RANK, 1536, 512,
              2 * GDN_HEADS, 2 * GDN_HEADS, 3 * D_MODEL):
        c.append(c[-1] + s)
    mq, mkv, gq, gk, gv, gr, glr, dqkv, dz, da, db, gates = [wt[c[i]:c[i + 1]] for i in range(12)]

    def zeros(n):
        return jnp.zeros((n,) + wt.shape[1:], wt.dtype)

    kr = mkv[KV_LORA:]
    mla = [mq, mkv[:KV_LORA], kr, _rope_partner(kr, axis=0), zeros(64)]
    dab = []
    for d in range(2):
        dab += [da[d * GDN_HEADS:(d + 1) * GDN_HEADS], db[d * GDN_HEADS:(d + 1) * GDN_HEADS],
                zeros(LANE - 2 * GDN_HEADS)]
    return jnp.concatenate([0.5 * gates, dqkv, 0.5 * dz, gv, 0.5 * gr] + mla
                           + [gq, gk, glr, zeros(LANE - 2 * GLA_RANK)]
                           + dab + [zeros(PROJ_W - PC_DAB - 2 * LANE)], axis=0)


def _pack_mla_weights(w_uq, w_ukv):
    l = w_uq.shape[0]
    qh = w_uq.reshape(l, Q_LORA, MLA_HEADS, MLA_NOPE + ROPE_DIM)
    zq = jnp.zeros((l, Q_LORA, MLA_HEADS, HEAD_W - MLA_NOPE - ROPE_DIM), w_uq.dtype)
    zn = jnp.zeros((l, Q_LORA, MLA_HEADS, MLA_NOPE), w_uq.dtype)
    wq = jnp.concatenate([qh, zq], axis=-1).reshape(l, Q_LORA, MLA_HEADS * HEAD_W)
    wq_sw = jnp.concatenate([zn, _rope_partner(qh[..., MLA_NOPE:]), zq], axis=-1).reshape(l, Q_LORA, MLA_HEADS * HEAD_W)
    wq_p = jnp.concatenate([wq, wq_sw], axis=-1).astype(BF16)

    kvh = w_ukv.reshape(l, KV_LORA, MLA_HEADS, MLA_NOPE + MLA_V)
    zk = jnp.zeros((l, KV_LORA, MLA_HEADS, HEAD_W - MLA_NOPE), w_ukv.dtype)
    wk_top = jnp.concatenate([kvh[..., :MLA_NOPE], zk], axis=-1).reshape(l, KV_LORA, MLA_HEADS * HEAD_W)
    place = np.zeros((KV_LORA, MLA_HEADS, HEAD_W), np.float32)
    idx = np.arange(ROPE_DIM)
    for rep in range(2):
        place[rep * ROPE_DIM + idx, :, MLA_NOPE + idx] = 1.0
    wk_bot = jnp.broadcast_to(jnp.asarray(place.reshape(1, KV_LORA, MLA_HEADS * HEAD_W)),
                              (l, KV_LORA, MLA_HEADS * HEAD_W))
    wk_p = jnp.concatenate([wk_top, wk_bot], axis=1).astype(BF16)
    wv_p = jnp.transpose(kvh[..., MLA_NOPE:], (0, 2, 3, 1)).reshape(l, VT_ROWS, KV_LORA).astype(BF16)
    return wq_p, wk_p, wv_p


def _rope_tables(tm):
    f32 = np.float32
    rows = DEC_SEQ // GRID_W
    row = np.repeat(np.arange(rows, dtype=np.float64), GRID_W)
    col = np.tile(np.arange(GRID_W, dtype=np.float64), rows)
    inv = ROPE_THETA ** (-np.arange(0, AXIS_DIM, 2, dtype=np.float64) / AXIS_DIM)
    ang_r, ang_c = row[:, None] * inv, col[:, None] * inv
    cos32 = np.concatenate([np.cos(ang_r)] * 2 + [np.cos(ang_c)] * 2, axis=-1)
    sin32 = np.concatenate([np.sin(ang_r)] * 2 + [np.sin(ang_c)] * 2, axis=-1)
    cos32 = np.concatenate([np.ones((tm, ROPE_DIM)), cos32], axis=0)
    sin32 = np.concatenate([np.zeros((tm, ROPE_DIM)), sin32], axis=0)
    n = cos32.shape[0]
    scale = (MLA_NOPE + ROPE_DIM) ** -0.5 * math.log2(math.e)
    pad = HEAD_W - MLA_NOPE - ROPE_DIM
    cq = np.concatenate([np.ones((n, MLA_NOPE)), cos32, np.zeros((n, pad))], axis=-1) * scale
    sq = np.concatenate([np.zeros((n, MLA_NOPE)), sin32, np.zeros((n, pad))], axis=-1) * scale
    ck = np.concatenate([cos32, sin32, np.zeros((n, HEAD_W - 2 * ROPE_DIM))], axis=-1)
    return jnp.asarray(np.concatenate([cq, sq, ck], axis=-1).astype(f32))


def _pad_lanes(x, n):
    return jnp.pad(x, [(0, 0)] * (x.ndim - 1) + [(0, n - x.shape[-1])])


def kernel(x_prompt, x_sample, cache_mla, state_gla, state_gdn, c, c_ctx, w_mod, b_mod, norm_mix, w_in,
           mla_q_norm, mla_w_uq, mla_kv_norm, mla_w_ukv, gla_w_gate, gla_b_gate, gla_norm, gdn_conv,
           gdn_a_log, gdn_dt_bias, gdn_norm, w_branch, w_out, norm_ffn, ffn_w_in, ffn_w_out, final_norm):
    assert x_prompt.shape == (BATCH, SEQ, D_MODEL) and x_sample.shape == (DEC_BATCH, DEC_SEQ, D_MODEL)
    assert cache_mla.shape == (DEC_BATCH, DEPTH, PAST_LEN, KV_LORA + ROPE_DIM) and w_in.shape[0] == DEPTH
    w_in_p = _pack_w_in(w_in)
    wq_p, wk_p, wv_p = _pack_mla_weights(mla_w_uq, mla_w_ukv)
    tab = _rope_tables(TM_PREP)
    wg_p = jnp.zeros((DEPTH, 2, LANE, GLA_HEADS * GLA_DK), F32)
    for d in range(2):
        wg_p = wg_p.at[:, d, d * GLA_RANK:(d + 1) * GLA_RANK, :].set(gla_w_gate[:, d])
    wg_p = wg_p.astype(BF16)
    bg_p = gla_b_gate.reshape(DEPTH, 2, 1, GLA_HEADS * GLA_DK)
    alog_p = _pad_lanes(gdn_a_log, LANE).reshape(DEPTH, 2, 1, LANE)
    dtb_p = _pad_lanes(gdn_dt_bias, LANE).reshape(DEPTH, 2, 1, LANE)
    gla_norm_p = jnp.tile(gla_norm, (1, GLA_HEADS)).reshape(DEPTH, 1, GLA_HEADS * GLA_DV)
    gdn_norm_p = jnp.tile(gdn_norm, (1, GDN_HEADS)).reshape(DEPTH, 1, GDN_HEADS * GDN_DV)
    w_branch_b = (0.5 * w_branch).astype(BF16)
    w_out_b = w_out.astype(BF16)
    wi_b = ffn_w_in.astype(BF16)
    wo_b = ffn_w_out.astype(BF16)
    cache_p = _pad_lanes(cache_mla, 2 * KV_LORA)
    cond8 = jnp.concatenate([c_ctx[None, :], c, jnp.zeros((MOD_ROWS - 1 - DEC_BATCH, D_MODEL), F32)], axis=0)

    mod = _modulation(cond8, w_mod, b_mod)
    mod3 = mod.reshape(DEPTH * MOD_ROWS * 6, 1, D_MODEL)
    kc, vc = _kv_cache(cache_p, wk_p, wv_p)

    h = (x_prompt.reshape(NCTX, D_MODEL), x_sample.reshape(NSMP, D_MODEL))
    kv_list, gla_list, gdn_list = [], [], []
    for l in range(DEPTH):
        proj = _inproj(h if l == 0 else xn, mod3, norm_mix, w_in_p, l)
        qp, kp, vtp, own = _mla_prep(proj, tab, mla_q_norm, mla_kv_norm, wq_p, wk_p, wv_p, l)
        y_mla = (_attention_ctx(qp, kp, vtp), _attention_smp(qp, kp, vtp, kc, vc, l))
        ogf_c, ogb_c, sg = _gla(proj, wg_p, bg_p, None, l, True)
        ogf_s, ogb_s = _gla(proj, wg_p, bg_p, state_gla, l, False)
        qkv = _gdn_pre(proj, gdn_conv, l)
        odf_c, odb_c, sd = _gdn(qkv, proj, alog_p, dtb_p, None, l, True)
        odf_s, odb_s = _gdn(qkv, proj, alog_p, dtb_p, state_gdn, l, False)
        h = _merge(y_mla, ((ogf_c, ogb_c), (ogf_s, ogb_s)), ((odf_c, odb_c), (odf_s, odb_s)), proj, h, mod3,
                   gla_norm_p, gdn_norm_p,
                   w_branch_b, w_out_b, l)
        if l == DEPTH - 1:
            h = _ffn(h, mod3, norm_ffn, wi_b, wo_b, final_norm, l, True)
        else:
            h, xn = _ffn(h, mod3, norm_ffn, wi_b, wo_b, norm_mix, l, False)
        kv_list.append(own[:NCTX, :KV_LORA + ROPE_DIM].reshape(BATCH, SEQ, KV_LORA + ROPE_DIM))
        gla_list.append(sg)
        gdn_list.append(sd)

    y_prompt = h[0].reshape(BATCH, SEQ, D_MODEL)
    y_sample = h[1].reshape(DEC_BATCH, DEC_SEQ, D_MODEL)
    return (y_prompt, y_sample, jnp.stack(kv_list, axis=1), jnp.stack(gla_list, axis=1),
            jnp.stack(gdn_list, axis=1))
```

```python
import functools
import math

import numpy as np
import jax
import jax.numpy as jnp
from jax import lax
from jax.experimental import pallas as pl
from jax.experimental.pallas import tpu as pltpu

F32 = jnp.float32
BF16 = jnp.bfloat16

D_MODEL = 1024
BATCH = 16
SEQ = 256
DEPTH = 2
DEC_BATCH = 2
DEC_SEQ = 4096
PAST_LEN = 256
GRID_W = 64
CHUNK = 64
EPS = 1e-6
MLA_HEADS = 8
MLA_NOPE = 64
ROPE_DIM = 32
AXIS_DIM = ROPE_DIM // 2
MLA_V = 64
Q_LORA = 256
KV_LORA = 128
ROPE_THETA = 10000.0
GLA_HEADS = 4
GLA_DK = 64
GLA_DV = 128
GLA_RANK = 16
GLA_TAU = 16.0
GDN_HEADS = 8
GDN_DK = 64
GDN_DV = 64
D_FF = ((8 * D_MODEL + 3 * 256 - 1) // (3 * 256)) * 256
MOD_W = 6 * D_MODEL

NCTX = BATCH * SEQ
NSMP = DEC_BATCH * DEC_SEQ
R = NCTX + NSMP
LANE = 128
HEAD_W = 128
VT_ROWS = MLA_HEADS * MLA_V
SEQ_BLK = 256
GDN_BLK_SMP = 512
GLA_BLK_CTX = 256
GLA_BLK_SMP = 1024
MOD_ROWS = 8

PC_GATES = 0
PC_DQKV = 3072
PC_DZ = 4608
PC_GV = 5120
PC_GR = 5632
PC_MLA = 6144
PC_GQ = 6656
PC_GK = 6912
PC_GLR = 7168
PC_DAB = 7296
PROJ_W = 7680

TM_IN = 1024
TN_IN = 2560
TM_PREP = 1024
GDN_PRE_BLK = 1024
TM_MERGE = 512
TM_FFN = 512
TF_FFN = 256
TQ_ATT = 256
TK_ATT = 256
ONES_ROWS = 16
ATT_AHEAD = 8
assert all(NCTX % t == 0 and NSMP % t == 0 for t in (TM_IN, TM_PREP, GDN_PRE_BLK, TM_MERGE, TM_FFN))
assert NCTX % DEC_SEQ == 0 and DEC_SEQ % TQ_ATT == 0 and DEC_SEQ % TK_ATT == 0 and PAST_LEN <= TK_ATT
assert SEQ % SEQ_BLK == 0 and DEC_SEQ % GDN_BLK_SMP == 0 and SEQ % GLA_BLK_CTX == 0 and DEC_SEQ % GLA_BLK_SMP == 0
assert GDN_PRE_BLK % SEQ == 0 and DEC_SEQ % GDN_PRE_BLK == 0 and D_FF % TF_FFN == 0 and PROJ_W % TN_IN == 0

VMEM_V7X = 64 * 1024 * 1024
VMEM_LIMIT = VMEM_V7X * 3 // 4


def _cp(sem):
    return pltpu.CompilerParams(dimension_semantics=sem, vmem_limit_bytes=VMEM_LIMIT)


def _dot(a, b):
    return jnp.dot(a, b, preferred_element_type=F32)


def _dot_nt(a, b):
    return lax.dot_general(a, b, (((1,), (1,)), ((), ())), preferred_element_type=F32)


def _dot_tn(a, b):
    return lax.dot_general(a, b, (((0,), (0,)), ((), ())), preferred_element_type=F32)


def _split2(x):
    hi = x.astype(BF16)
    lo = (x - hi.astype(F32)).astype(BF16)
    return hi, lo


def _split3(x):
    hi = x.astype(BF16)
    r1 = x - hi.astype(F32)
    mid = r1.astype(BF16)
    lo = (r1 - mid.astype(F32)).astype(BF16)
    return hi, mid, lo


def _cumsum_rows(tri_b, x):
    hi, mid, lo = _split3(x)
    return _dot(tri_b, hi) + (_dot(tri_b, mid) + _dot(tri_b, lo))


def _sigmoid(x):
    return 0.5 * jnp.tanh(0.5 * x) + 0.5


def _silu(x):
    return x * _sigmoid(x)


def _silu_of_twice(hx):
    return hx + hx * jnp.tanh(hx)


def _softplus(x):
    return jnp.maximum(x, 0.0) + jnp.log(1.0 + jnp.exp(-jnp.abs(x)))


def _log_sigmoid(x):
    return jnp.minimum(x, 0.0) - jnp.log(1.0 + jnp.exp(-jnp.abs(x)))


def _rms(x, g):
    return x * lax.rsqrt(jnp.mean(x * x, axis=-1, keepdims=True) + EPS) * g


def _mod_row(row_start):
    return jnp.where(row_start < NCTX, 0, 1 + (row_start - NCTX) // DEC_SEQ)


def _mod_spec(layer, which, tm, axis=0):
    def imap(*idx):
        return ((layer * MOD_ROWS + _mod_row(idx[axis] * tm)) * 6 + which, 0, 0)
    return pl.BlockSpec((1, 1, D_MODEL), imap)


def _mod_kernel(c_ref, w_ref, b_ref, o_ref):
    x = _silu(c_ref[...])
    xh, xl = _split2(x)
    wh, wl = _split2(w_ref[0])
    o_ref[0] = _dot(xh, wh) + (_dot(xl, wh) + _dot(xh, wl)) + b_ref[0]


def _modulation(cond8, w_mod, b_mod):
    tn = 1536
    return pl.pallas_call(
        _mod_kernel,
        grid=(DEPTH, MOD_W // tn),
        in_specs=[pl.BlockSpec((MOD_ROWS, D_MODEL), lambda l, j: (0, 0)),
                  pl.BlockSpec((1, D_MODEL, tn), lambda l, j: (l, 0, j)),
                  pl.BlockSpec((1, 1, tn), lambda l, j: (l, 0, j))],
        out_specs=pl.BlockSpec((1, MOD_ROWS, tn), lambda l, j: (l, 0, j)),
        out_shape=jax.ShapeDtypeStruct((DEPTH, MOD_ROWS, MOD_W), F32),
        compiler_params=_cp(("arbitrary", "arbitrary")),
        name="modulation",
    )(cond8, w_mod, b_mod.reshape(DEPTH, 1, MOD_W))


def _family_specs(tm, width, axis=0):
    nc = NCTX // tm
    return [pl.BlockSpec((tm, width), lambda *idx: (jnp.minimum(idx[axis], nc - 1), 0)),
            pl.BlockSpec((tm, width), lambda *idx: (jnp.maximum(idx[axis] - nc, 0), 0))]


def _family_rows(refs, tm, axis=0):
    is_ctx = pl.program_id(axis) < NCTX // tm
    return jnp.where(is_ctx, refs[0][...], refs[1][...])


def _inproj_kernel(*refs, normed):
    if normed:
        xn_ref, w_ref, o_ref = refs
        xn = xn_ref[...]
    else:
        hc_ref, hs_ref, g_ref, sc_ref, sh_ref, w_ref, o_ref = refs
        y = _rms(_family_rows((hc_ref, hs_ref), TM_IN, axis=1), g_ref[0])
        xn = (y * (1.0 + sc_ref[0]) + sh_ref[0]).astype(BF16)
    o_ref[...] = _dot_nt(xn, w_ref[0]).astype(BF16)


def _inproj(h, mod3, norm_mix, w_in_p, layer):
    normed = not isinstance(h, tuple)
    if normed:
        in_specs = [pl.BlockSpec((TM_IN, D_MODEL), lambda j, i: (i, 0))]
        args = [h]
    else:
        in_specs = _family_specs(TM_IN, D_MODEL, axis=1) + [
            pl.BlockSpec((1, 1, D_MODEL), lambda j, i: (layer, 0, 0)),
            _mod_spec(layer, 1, TM_IN, axis=1),
            _mod_spec(layer, 0, TM_IN, axis=1)]
        args = list(h) + [norm_mix.reshape(DEPTH, 1, D_MODEL), mod3, mod3]
    return pl.pallas_call(
        functools.partial(_inproj_kernel, normed=normed),
        grid=(PROJ_W // TN_IN, R // TM_IN),
        in_specs=in_specs + [pl.BlockSpec((1, TN_IN, D_MODEL), lambda j, i: (layer, j, 0))],
        out_specs=pl.BlockSpec((TM_IN, TN_IN), lambda j, i: (i, j)),
        out_shape=jax.ShapeDtypeStruct((R, PROJ_W), BF16),
        compiler_params=_cp(("arbitrary", "arbitrary")),
        name="inproj",
    )(*args, w_in_p)


def _mla_prep_kernel(pm_ref, tab_ref, qn_ref, kvn_ref, wq_ref, wk_ref, wv_ref,
                     q_ref, k_ref, v_ref, own_ref):
    pm = pm_ref[...].astype(F32)
    tab = tab_ref[...]
    qn = _rms(pm[:, :Q_LORA], qn_ref[0]).astype(BF16)
    q2 = _dot(qn, wq_ref[0])
    hw = MLA_HEADS * HEAD_W
    cq = jnp.tile(tab[:, :HEAD_W], (1, MLA_HEADS))
    sq = jnp.tile(tab[:, HEAD_W:2 * HEAD_W], (1, MLA_HEADS))
    q_ref[...] = (q2[:, :hw] * cq + q2[:, hw:] * sq).astype(BF16)
    ckv = _rms(pm[:, Q_LORA:Q_LORA + KV_LORA], kvn_ref[0])
    kr = pm[:, Q_LORA + KV_LORA:] * tab[:, 2 * HEAD_W:]
    lhs = jnp.concatenate([ckv, kr], axis=1)
    own_ref[...] = lhs
    lb = lhs.astype(BF16)
    k_ref[...] = _dot(lb, wk_ref[0]).astype(BF16)
    v_ref[...] = _dot_nt(wv_ref[0], lb[:, :KV_LORA]).astype(BF16)


def _mla_prep(proj, tab, q_norm, kv_norm, wq_p, wk_p, wv_p, layer):
    tm = TM_PREP
    hw = MLA_HEADS * HEAD_W

    def tab_map(i):
        r0 = i * tm
        return (jnp.where(r0 < NCTX, 0, 1 + ((r0 - NCTX) % DEC_SEQ) // tm), 0)

    return pl.pallas_call(
        _mla_prep_kernel,
        grid=(R // tm,),
        in_specs=[pl.BlockSpec((tm, 512), lambda i: (i, PC_MLA // 512)),
                  pl.BlockSpec((tm, 3 * HEAD_W), tab_map),
                  pl.BlockSpec((1, 1, Q_LORA), lambda i: (layer, 0, 0)),
                  pl.BlockSpec((1, 1, KV_LORA), lambda i: (layer, 0, 0)),
                  pl.BlockSpec((1, Q_LORA, 2 * hw), lambda i: (layer, 0, 0)),
                  pl.BlockSpec((1, 2 * KV_LORA, hw), lambda i: (layer, 0, 0)),
                  pl.BlockSpec((1, VT_ROWS, KV_LORA), lambda i: (layer, 0, 0))],
        out_specs=[pl.BlockSpec((tm, hw), lambda i: (i, 0)),
                   pl.BlockSpec((tm, hw), lambda i: (i, 0)),
                   pl.BlockSpec((VT_ROWS, tm), lambda i: (0, i)),
                   pl.BlockSpec((tm, 2 * KV_LORA), lambda i: (i, 0))],
        out_shape=[jax.ShapeDtypeStruct((R, hw), BF16),
                   jax.ShapeDtypeStruct((R, hw), BF16),
                   jax.ShapeDtypeStruct((VT_ROWS, R), BF16),
                   jax.ShapeDtypeStruct((R, 2 * KV_LORA), F32)],
        compiler_params=_cp(("arbitrary",)),
        name="mla_prep",
    )(proj, tab, q_norm.reshape(DEPTH, 1, Q_LORA), kv_norm.reshape(DEPTH, 1, KV_LORA),
      wq_p, wk_p, wv_p)


def _kv_cache_kernel(c_ref, wk_ref, wv_ref, k_ref, v_ref):
    lb = c_ref[0, 0].astype(BF16)
    k_ref[0, 0] = _dot(lb, wk_ref[0]).astype(BF16)
    v_ref[0, 0] = _dot_nt(wv_ref[0], lb[:, :KV_LORA]).astype(BF16)


def _kv_cache(cache_p, wk_p, wv_p):
    hw = MLA_HEADS * HEAD_W
    return pl.pallas_call(
        _kv_cache_kernel,
        grid=(DEC_BATCH, DEPTH),
        in_specs=[pl.BlockSpec((1, 1, PAST_LEN, 2 * KV_LORA), lambda b, l: (b, l, 0, 0)),
                  pl.BlockSpec((1, 2 * KV_LORA, hw), lambda b, l: (l, 0, 0)),
                  pl.BlockSpec((1, VT_ROWS, KV_LORA), lambda b, l: (l, 0, 0))],
        out_specs=[pl.BlockSpec((1, 1, PAST_LEN, hw), lambda b, l: (b, l, 0, 0)),
                   pl.BlockSpec((1, 1, VT_ROWS, PAST_LEN), lambda b, l: (b, l, 0, 0))],
        out_shape=[jax.ShapeDtypeStruct((DEC_BATCH, DEPTH, PAST_LEN, hw), BF16),
                   jax.ShapeDtypeStruct((DEC_BATCH, DEPTH, VT_ROWS, PAST_LEN), BF16)],
        compiler_params=_cp(("arbitrary", "arbitrary")),
        name="kv_cache",
    )(cache_p, wk_p, wv_p)


def _attn_kernel(*refs, heads, has_cache):
    if has_cache:
        q_ref, k_ref, vt_ref, kc_ref, vct_ref, o_ref = refs
    else:
        q_ref, k_ref, vt_ref, o_ref = refs
    n_keys = k_ref.shape[0]
    tk = min(TK_ATT, n_keys)
    tiles = ([("cache", 0, PAST_LEN)] if has_cache else []) + [("own", s0, tk) for s0 in range(0, n_keys, tk)]
    units = [(h, t) for t in range(len(tiles)) for h in range(heads)]
    qs = [q_ref[:, h * HEAD_W:(h + 1) * HEAD_W] for h in range(heads)]

    def scores(h, t):
        kind, s0, n = tiles[t]
        sl = slice(h * HEAD_W, (h + 1) * HEAD_W)
        keys = kc_ref[0, 0, :, sl] if kind == "cache" else k_ref[s0:s0 + n, sl]
        st = _dot_nt(keys, qs[h])
        return st, jnp.max(st, axis=0, keepdims=True)

    def values_t(h, t):
        kind, s0, n = tiles[t]
        sv = slice(h * MLA_V, (h + 1) * MLA_V)
        vt = vct_ref[0, 0, sv, :] if kind == "cache" else vt_ref[sv, s0:s0 + n]
        return jnp.concatenate([vt, jnp.ones((ONES_ROWS, n), BF16)], axis=0)

    ahead = ATT_AHEAD
    pending = {u: scores(*u) for u in units[:ahead]}
    m, acc = [None] * heads, [None] * heads
    for i, (h, t) in enumerate(units):
        if i + ahead < len(units):
            pending[units[i + ahead]] = scores(*units[i + ahead])
        st, mt = pending.pop((h, t))
        if m[h] is None:
            m[h] = mt
            acc[h] = _dot(values_t(h, t), jnp.exp2(st - mt).astype(BF16))
        else:
            m_new = jnp.maximum(m[h], mt)
            alpha = jnp.exp2(m[h] - m_new)
            acc[h] = alpha * acc[h] + _dot(values_t(h, t), jnp.exp2(st - m_new).astype(BF16))
            m[h] = m_new
    outs = [acc[h][:MLA_V] / acc[h][MLA_V:MLA_V + 1] for h in range(heads)]
    o_ref[...] = jnp.concatenate(outs, axis=0).T.astype(BF16)


def _attention_ctx(qp, kp, vtp):
    hw = MLA_HEADS * HEAD_W
    return pl.pallas_call(
        functools.partial(_attn_kernel, heads=MLA_HEADS, has_cache=False),
        grid=(BATCH,),
        in_specs=[pl.BlockSpec((SEQ, hw), lambda s: (s, 0)),
                  pl.BlockSpec((SEQ, hw), lambda s: (s, 0)),
                  pl.BlockSpec((VT_ROWS, SEQ), lambda s: (0, s))],
        out_specs=pl.BlockSpec((SEQ, MLA_HEADS * MLA_V), lambda s: (s, 0)),
        out_shape=jax.ShapeDtypeStruct((NCTX, MLA_HEADS * MLA_V), BF16),
        compiler_params=_cp(("arbitrary",)),
        name="attn_ctx",
    )(qp, kp, vtp)


def _attention_smp(qp, kp, vtp, kc, vct, layer):
    hpb = 4
    wq = hpb * HEAD_W
    wv = hpb * MLA_V
    nq = DEC_SEQ // TQ_ATT
    qoff = NCTX // TQ_ATT
    koff = NCTX // DEC_SEQ
    return pl.pallas_call(
        functools.partial(_attn_kernel, heads=hpb, has_cache=True),
        grid=(DEC_BATCH, MLA_HEADS // hpb, nq),
        in_specs=[pl.BlockSpec((TQ_ATT, wq), lambda b, g, i: (qoff + b * nq + i, g)),
                  pl.BlockSpec((DEC_SEQ, wq), lambda b, g, i: (koff + b, g)),
                  pl.BlockSpec((wv, DEC_SEQ), lambda b, g, i: (g, koff + b)),
                  pl.BlockSpec((1, 1, PAST_LEN, wq), lambda b, g, i: (b, layer, 0, g)),
                  pl.BlockSpec((1, 1, wv, PAST_LEN), lambda b, g, i: (b, layer, g, 0))],
        out_specs=pl.BlockSpec((TQ_ATT, hpb * MLA_V), lambda b, g, i: (b * nq + i, g)),
        out_shape=jax.ShapeDtypeStruct((NSMP, MLA_HEADS * MLA_V), BF16),
        compiler_params=_cp(("arbitrary", "arbitrary", "arbitrary")),
        name="attn_smp",
    )(qp, kp, vtp, kc, vct)


def _scan_specs(ctx, blk, widths_and_cols, dir_cols=()):
    nseq, seq_len = (BATCH, SEQ) if ctx else (DEC_BATCH, DEC_SEQ)
    nblk = seq_len // blk
    roff = 0 if ctx else NCTX // blk

    def rb(dd):
        return lambda s, j: roff + s * nblk + (j if dd == 0 else nblk - 1 - j)

    specs = []
    for dd in range(2):
        r = rb(dd)
        specs += [pl.BlockSpec((blk, w), lambda s, j, r=r, c=c: (r(s, j), c)) for w, c in widths_and_cols]
        specs += [pl.BlockSpec((blk, w), lambda s, j, r=r, c=c + dd: (r(s, j), c)) for w, c in dir_cols]
    out_maps = [lambda s, j, r=rb(dd): (r(s, j) - roff, 0) for dd in range(2)]
    return nseq, nblk, specs, out_maps


def _gla_kernel(*refs, nblk, has_s0, nch):
    per_dir = [refs[0:4], refs[4:8]]
    wg_ref, bg_ref = refs[8:10]
    if has_s0:
        s0_ref, of_ref, ob_ref, s_scr = refs[10:]
    else:
        of_ref, ob_ref, sfin_ref, s_scr = refs[10:]
    o_refs = [of_ref, ob_ref]
    j = pl.program_id(1)

    @pl.when(j == 0)
    def _():
        if has_s0:
            s_scr[...] = s0_ref[0, 0]
        else:
            s_scr[...] = jnp.zeros_like(s_scr)

    r64 = lax.broadcasted_iota(jnp.int32, (CHUNK, CHUNK), 0)
    c64 = lax.broadcasted_iota(jnp.int32, (CHUNK, CHUNK), 1)
    incl = [c64 <= r64, c64 >= r64]
    tri_b = [jnp.where(m, 1.0, 0.0).astype(BF16) for m in incl]
    heads = [(slice(h * GLA_DK, (h + 1) * GLA_DK), slice(h * GLA_DV, (h + 1) * GLA_DV))
             for h in range(GLA_HEADS)]

    ch = []
    for c in range(nch):
        for dd in range(2):
            ci = c if dd == 0 else nch - 1 - c
            rows = slice(ci * CHUNK, (ci + 1) * CHUNK)
            glr_ref = per_dir[dd][3]
            ch.append(dict(dd=dd, rows=rows, z=_dot(glr_ref[rows, :], wg_ref[0, dd]) + bg_ref[0, dd]))
    for t in ch:
        t["la"] = _split3(_log_sigmoid(t["z"]) * (1.0 / GLA_TAU))
    for t in ch:
        hi, mid, lo = t["la"]
        tb = tri_b[t["dd"]]
        t["bc"] = _dot(tb, hi) + (_dot(tb, mid) + _dot(tb, lo))
    for t in ch:
        q_ref, k_ref, v_ref, _ = per_dir[t["dd"]]
        rows, bc = t["rows"], t["bc"]
        q = q_ref[rows, :].astype(F32) * (GLA_DK ** -0.5)
        k = k_ref[rows, :].astype(F32)
        bl = bc[CHUNK - 1:CHUNK] if t["dd"] == 0 else bc[0:1]
        t["qd"] = (q * jnp.exp(bc)).astype(BF16)
        t["ki"] = (k * jnp.exp(-bc)).astype(BF16)
        t["ke"] = (k * jnp.exp(bl - bc)).astype(BF16)
        t["dec_t"] = jnp.transpose(jnp.broadcast_to(jnp.exp(bl), (LANE, GLA_HEADS * GLA_DK)))
        t["v"] = v_ref[rows, :]
    for t in ch:
        t["a"] = [jnp.where(incl[t["dd"]], _dot_nt(t["qd"][:, sk], t["ki"][:, sk]), 0.0).astype(BF16)
                  for sk, _ in heads]
    for t in ch:
        t["oi"] = [_dot(t["a"][h], t["v"][:, sv]) for h, (_, sv) in enumerate(heads)]
        t["upd"] = [_dot_tn(t["ke"][:, sk], t["v"][:, sv]) for sk, sv in heads]
    states = [[s_scr[dd, h] for h in range(GLA_HEADS)] for dd in range(2)]
    for t in ch:
        st = states[t["dd"]]
        t["s_in"] = [s.astype(BF16) for s in st]
        for h, (sk, _) in enumerate(heads):
            st[h] = t["dec_t"][sk, :] * st[h] + t["upd"][h]
    for t in ch:
        for h, (sk, sv) in enumerate(heads):
            o_refs[t["dd"]][t["rows"], sv] = t["oi"][h] + _dot(t["qd"][:, sk], t["s_in"][h])
    for dd in range(2):
        for h in range(GLA_HEADS):
            s_scr[dd, h] = states[dd][h]

    if not has_s0:
        @pl.when(j == nblk - 1)
        def _():
            sfin_ref[0] = s_scr[...]


def _gla(proj, wg_p, bg_p, s0, layer, ctx):
    blk = GLA_BLK_CTX if ctx else GLA_BLK_SMP
    hk = GLA_HEADS * GLA_DK
    hv = GLA_HEADS * GLA_DV
    nseq, nblk, in_specs, out_maps = _scan_specs(
        ctx, blk, [(hk, PC_GQ // hk), (hk, PC_GK // hk), (hv, PC_GV // hv), (LANE, PC_GLR // LANE)])
    in_specs += [pl.BlockSpec((1, 2, LANE, hk), lambda s, j: (layer, 0, 0, 0)),
                 pl.BlockSpec((1, 2, 1, hk), lambda s, j: (layer, 0, 0, 0))]
    args = [proj] * 8 + [wg_p, bg_p]
    nrows = NCTX if ctx else NSMP
    out_specs = [pl.BlockSpec((blk, hv), m) for m in out_maps]
    out_shape = [jax.ShapeDtypeStruct((nrows, hv), F32)] * 2
    st_blk = (2, GLA_HEADS, GLA_DK, GLA_DV)
    if ctx:
        out_specs.append(pl.BlockSpec((1,) + st_blk, lambda s, j: (s, 0, 0, 0, 0)))
        out_shape.append(jax.ShapeDtypeStruct((BATCH,) + st_blk, F32))
    else:
        in_specs.append(pl.BlockSpec((1, 1) + st_blk, lambda s, j: (s, layer, 0, 0, 0, 0)))
        args.append(s0)
    return pl.pallas_call(
        functools.partial(_gla_kernel, nblk=nblk, has_s0=not ctx, nch=blk // CHUNK),
        grid=(nseq, nblk),
        in_specs=in_specs, out_specs=out_specs, out_shape=out_shape,
        scratch_shapes=[pltpu.VMEM(st_blk, F32)],
        compiler_params=_cp(("arbitrary", "arbitrary")),
        name="gla_ctx" if ctx else "gla_smp",
    )(*args)


GDN_G = 2
GDN_GW = GDN_G * GDN_DK
HALO = 16


def _gdn_pre_kernel(x_ref, xp_ref, xn_ref, cw_ref, o_ref):
    blk = x_ref.shape[0]
    hk = GDN_HEADS * GDN_DK
    sub = SEQ
    cw = 0.5 * cw_ref[0]
    rowi = lax.broadcasted_iota(jnp.int32, (sub, 1), 0)
    lane = lax.broadcasted_iota(jnp.int32, (sub, LANE), 1)
    lo_half = lane < GDN_DK

    def l2n(t, scale):
        parts = []
        for c in range(hk // LANE):
            tc = t[:, c * LANE:(c + 1) * LANE]
            sq = tc * tc
            s_lo = jnp.sum(jnp.where(lo_half, sq, 0.0), axis=-1, keepdims=True)
            s_hi = jnp.sum(jnp.where(lo_half, 0.0, sq), axis=-1, keepdims=True)
            parts.append(tc * (lax.rsqrt(jnp.where(lo_half, s_lo, s_hi) + EPS) * scale))
        return jnp.concatenate(parts, axis=1)

    for s in range(blk // sub):
        rows = slice(s * sub, (s + 1) * sub)
        x = x_ref[rows, :].astype(F32)
        r0 = pl.program_id(0) * blk + s * sub
        pos0 = jnp.where(r0 < NCTX, r0 % SEQ, (r0 - NCTX) % DEC_SEQ)
        seq_len = jnp.where(r0 < NCTX, SEQ, DEC_SEQ)
        before = xp_ref[...] if s == 0 else x_ref[s * sub - HALO:s * sub, :]
        after = xn_ref[...] if s == blk // sub - 1 else x_ref[(s + 1) * sub:(s + 1) * sub + HALO, :]
        prev_row = jnp.where(pos0 > 0, before.astype(F32)[HALO - 1:HALO], 0.0)
        next_row = jnp.where(pos0 + sub < seq_len, after.astype(F32)[0:1], 0.0)
        x_prev = jnp.where(rowi == 0, prev_row, pltpu.roll(x, 1, 0))
        x_next = jnp.where(rowi == sub - 1, next_row, pltpu.roll(x, sub - 1, 0))
        y = _silu_of_twice(x_prev * cw[0:1] + x * cw[1:2] + x_next * cw[2:3])
        o_ref[rows, :hk] = l2n(y[:, :hk], GDN_DK ** -0.5).astype(BF16)
        o_ref[rows, hk:2 * hk] = l2n(y[:, hk:2 * hk], 1.0).astype(BF16)
        o_ref[rows, 2 * hk:] = y[:, 2 * hk:].astype(BF16)


def _gdn_pre(proj, conv_w, layer):
    cw = 2 * GDN_HEADS * GDN_DK + GDN_HEADS * GDN_DV
    blk = GDN_PRE_BLK
    halo = blk // HALO
    last_h = R // HALO - 1
    return pl.pallas_call(
        _gdn_pre_kernel,
        grid=(R // blk,),
        in_specs=[pl.BlockSpec((blk, cw), lambda i: (i, PC_DQKV // cw)),
                  pl.BlockSpec((HALO, cw), lambda i: (jnp.maximum(i * halo - 1, 0), PC_DQKV // cw)),
                  pl.BlockSpec((HALO, cw), lambda i: (jnp.minimum((i + 1) * halo, last_h), PC_DQKV // cw)),
                  pl.BlockSpec((1, 3, cw), lambda i: (layer, 0, 0))],
        out_specs=pl.BlockSpec((blk, cw), lambda i: (i, 0)),
        out_shape=jax.ShapeDtypeStruct((R, cw), BF16),
        compiler_params=_cp(("arbitrary",)),
        name="gdn_pre",
    )(proj, proj, proj, conv_w)


def _gdn_kernel(*refs, nblk, has_s0, nch):
    per_dir = [refs[0:4], refs[4:8]]
    alog_ref, dtb_ref = refs[8:10]
    if has_s0:
        s0_ref, of_ref, ob_ref, s_scr = refs[10:]
    else:
        of_ref, ob_ref, sfin_ref, s_scr = refs[10:]
    o_refs = [of_ref, ob_ref]
    j = pl.program_id(1)
    ngrp = GDN_HEADS // GDN_G

    def head_block(h):
        g, hh = divmod(h, GDN_G)
        return g, slice(hh * GDN_DK, (hh + 1) * GDN_DK), slice(hh * GDN_DV, (hh + 1) * GDN_DV)

    @pl.when(j == 0)
    def _():
        s_scr[...] = jnp.zeros_like(s_scr)
        if has_s0:
            for dd in range(2):
                for h in range(GDN_HEADS):
                    g, rk, rv = head_block(h)
                    s_scr[dd, g, rk, rv] = s0_ref[0, 0, dd, h]

    row = lax.broadcasted_iota(jnp.int32, (CHUNK, GDN_GW), 0)
    col = lax.broadcasted_iota(jnp.int32, (CHUNK, GDN_GW), 1) % CHUNK
    r64 = lax.broadcasted_iota(jnp.int32, (CHUNK, CHUNK), 0)
    c64 = lax.broadcasted_iota(jnp.int32, (CHUNK, CHUNK), 1)
    dir_masks = []
    for dd in range(2):
        lag = (row - col) if dd == 0 else (col - row)
        tri = (c64 <= r64) if dd == 0 else (c64 >= r64)
        dir_masks.append(dict(incl=lag >= 0, strict=lag > 0,
                              tri_b=jnp.where(tri, 1.0, 0.0).astype(BF16)))
    eye = col == row
    eye_f = jnp.where(eye, 1.0, 0.0)
    blk8 = (row // 8) == (col // 8)
    blk16 = (row // 16) == (col // 16)
    blk32 = (row // 32) == (col // 32)
    brow = lax.broadcasted_iota(jnp.int32, (GDN_GW, GDN_GW), 0) // CHUNK
    bcol = lax.broadcasted_iota(jnp.int32, (GDN_GW, GDN_GW), 1) // CHUNK
    same_head = brow == bcol

    def bdiag(t):
        tb = t.astype(BF16)
        return jnp.where(same_head, jnp.concatenate([tb] * GDN_G, axis=0), jnp.zeros((), BF16))

    def bdmm(a, b):
        return _dot(a.astype(BF16), bdiag(b))

    def setup(dd, c):
        q_ref, k_ref, v_ref, ab_ref = per_dir[dd]
        mk = dir_masks[dd]
        ci = c if dd == 0 else nch - 1 - c
        rows = slice(ci * CHUNK, (ci + 1) * CHUNK)
        ab = ab_ref[rows, :].astype(F32)
        neg_a = -jnp.exp(alog_ref[0, dd])
        gam = _cumsum_rows(mk["tri_b"], neg_a * _softplus(ab + dtb_ref[0, dd]))
        beta_all = _sigmoid(ab)
        out = []
        for g in range(ngrp):
            gl = slice(g * GDN_GW, (g + 1) * GDN_GW)
            gcol = jnp.concatenate(
                [jnp.broadcast_to(gam[:, g * GDN_G + h:g * GDN_G + h + 1], (CHUNK, GDN_DK))
                 for h in range(GDN_G)], axis=1)
            bcol_ = jnp.concatenate(
                [jnp.broadcast_to(beta_all[:, GDN_HEADS + g * GDN_G + h:GDN_HEADS + g * GDN_G + h + 1],
                                  (CHUNK, GDN_DK)) for h in range(GDN_G)], axis=1)
            grow = jnp.sum(jnp.where(eye, gcol, 0.0), axis=0, keepdims=True)
            glast = gcol[CHUNK - 1:CHUNK] if dd == 0 else gcol[0:1]
            decay = jnp.where(mk["incl"], jnp.exp(gcol - grow), 0.0)
            egc = jnp.exp(gcol)
            qg = q_ref[rows, gl].astype(F32)
            kg = k_ref[rows, gl].astype(F32)
            vg = v_ref[rows, gl].astype(F32)
            kq = _dot_nt(jnp.concatenate([k_ref[rows, gl], q_ref[rows, gl]], axis=0),
                         bdiag(k_ref[rows, gl]))
            out.append(dict(dd=dd, g=g, rows=rows, gl=gl,
                            m=jnp.where(mk["strict"], bcol_ * kq[:CHUNK] * decay, 0.0),
                            aqk=(kq[CHUNK:] * decay).astype(BF16),
                            vb=vg * bcol_, kb=kg * (bcol_ * egc), qe=qg * egc,
                            kend=(kg * jnp.exp(glast - gcol)).astype(BF16),
                            eg=jnp.exp(glast)))
        return out

    def stages(ch):
        pairs = list(zip(ch[0::2], ch[1::2]))

        def bdmm_pairs(lhs, rhs, out):
            for ta, tb in pairs:
                a = jnp.concatenate([ta[lhs].astype(BF16), tb[lhs].astype(BF16)], axis=0)
                w = jnp.concatenate([bdiag(ta[rhs]), bdiag(tb[rhs])], axis=1)
                y = _dot(a, w)
                ta[out] = y[:CHUNK, :GDN_GW]
                tb[out] = y[CHUNK:, GDN_GW:]

        def neumann_a():
            for t in ch:
                t["n8"] = jnp.where(blk8, t["m"], 0.0)
            bdmm_pairs("n8", "n8", "n2")

        def neumann_b():
            for t in ch:
                t["p1"] = eye_f - t["n8"]
            bdmm_pairs("n2", "n2", "n4")
            bdmm_pairs("p1", "n2", "p1n2")
            for t in ch:
                t["p1"] = t["p1"] + t["p1n2"]

        def neumann_c():
            bdmm_pairs("p1", "n4", "p1n4")
            for t in ch:
                t["dinv"] = t["p1"] + t["p1n4"]

        def double_a(off):
            def run():
                for t in ch:
                    t["off"] = jnp.where(off, t["m"], 0.0)
                bdmm_pairs("dinv", "off", "dl")
            return run

        def double_b():
            bdmm_pairs("dl", "dinv", "dld")
            for t in ch:
                t["dinv"] = t["dinv"] - t["dld"]

        def solve():
            for t in ch:
                uw = _dot(t["dinv"].astype(BF16), jnp.concatenate([bdiag(t["vb"]), bdiag(t["kb"])], axis=1))
                t["u"] = uw[:, :GDN_GW]
                t["w"] = uw[:, GDN_GW:]

        def fold():
            for t in ch:
                wu = jnp.concatenate([t["w"], t["u"]], axis=1).astype(BF16)
                cb = _dot_tn(t["kend"], wu)
                t["c"] = jnp.where(same_head, cb[:, :GDN_GW], 0.0).astype(BF16)
                t["b"] = jnp.where(same_head, cb[:, GDN_GW:], 0.0)
                ao = _dot(t["aqk"], jnp.concatenate([bdiag(t["w"]), bdiag(t["u"])], axis=1))
                t["qt"] = (t["qe"] - ao[:, :GDN_GW]).astype(BF16)
                t["o"] = ao[:, GDN_GW:]

        out = [neumann_a, neumann_b, neumann_c]
        for inner, outer in ((blk8, blk16), (blk16, blk32), (blk32, None)):
            off = jnp.logical_not(inner) if outer is None else jnp.logical_and(outer, jnp.logical_not(inner))
            out += [double_a(off), double_b]
        return out + [solve, fold]

    states = [[s_scr[dd, g] for g in range(ngrp)] for dd in range(2)]

    def recur(cur):
        ys = [_dot(jnp.concatenate([t["qt"], t["c"]], axis=0), states[t["dd"]][t["g"]].astype(BF16))
              for t in cur]
        for t, y in zip(cur, ys):
            o_refs[t["dd"]][t["rows"], t["gl"]] = y[:CHUNK] + t["o"]
            states[t["dd"]][t["g"]] = t["eg"] * states[t["dd"]][t["g"]] - y[CHUNK:] + t["b"]

    half = nch // 2
    chains = {}
    for c in range(half):
        for dd in range(2):
            chains[(dd, c)] = setup(dd, c)
    later = [(dd, c) for c in range(half, nch) for dd in range(2)]
    wave_a = [t for c in range(half) for dd in range(2) for t in chains[(dd, c)]]
    stages_a = stages(wave_a)
    stride = max(1, len(stages_a) // (len(later) + 1))
    for k, stage in enumerate(stages_a):
        stage()
        if k % stride == 0 and later:
            key = later.pop(0)
            chains[key] = setup(*key)
    for key in later:
        chains[key] = setup(*key)
    wave_b = [t for c in range(half, nch) for dd in range(2) for t in chains[(dd, c)]]
    stages_b = stages(wave_b)
    stride = max(1, len(stages_b) // (half + 1))
    done = 0
    for k, stage in enumerate(stages_b):
        stage()
        if k % stride == 1 % stride and done < half:
            recur(chains[(0, done)] + chains[(1, done)])
            done += 1
    for c in range(done, nch):
        recur(chains[(0, c)] + chains[(1, c)])
    for dd in range(2):
        for g in range(ngrp):
            s_scr[dd, g] = states[dd][g]

    if not has_s0:
        @pl.when(j == nblk - 1)
        def _():
            for dd in range(2):
                for h in range(GDN_HEADS):
                    g, rk, rv = head_block(h)
                    sfin_ref[0, dd, h] = s_scr[dd, g, rk, rv]


def _gdn(qkv, proj, alog_p, dtb_p, s0, layer, ctx):
    blk = SEQ_BLK if ctx else GDN_BLK_SMP
    nseq, nblk = (BATCH, SEQ // blk) if ctx else (DEC_BATCH, DEC_SEQ // blk)
    roff = 0 if ctx else NCTX // blk
    hk = GDN_HEADS * GDN_DK
    hv = GDN_HEADS * GDN_DV

    def rb(dd):
        return lambda s, j: roff + s * nblk + (j if dd == 0 else nblk - 1 - j)

    in_specs, args = [], []
    for dd in range(2):
        r = rb(dd)
        in_specs += [pl.BlockSpec((blk, hk), lambda s, j, r=r: (r(s, j), 0)),
                     pl.BlockSpec((blk, hk), lambda s, j, r=r: (r(s, j), 1)),
                     pl.BlockSpec((blk, hv), lambda s, j, r=r: (r(s, j), 2 * hk // hv)),
                     pl.BlockSpec((blk, LANE), lambda s, j, r=r, dd=dd: (r(s, j), PC_DAB // LANE + dd))]
        args += [qkv, qkv, qkv, proj]
    in_specs += [pl.BlockSpec((1, 2, 1, LANE), lambda s, j: (layer, 0, 0, 0))] * 2
    args += [alog_p, dtb_p]
    nrows = NCTX if ctx else NSMP
    out_specs = [pl.BlockSpec((blk, hv), lambda s, j, r=rb(dd): (r(s, j) - roff, 0)) for dd in range(2)]
    out_shape = [jax.ShapeDtypeStruct((nrows, hv), F32)] * 2
    st_blk = (2, GDN_HEADS, GDN_DK, GDN_DV)
    if ctx:
        out_specs.append(pl.BlockSpec((1,) + st_blk, lambda s, j: (s, 0, 0, 0, 0)))
        out_shape.append(jax.ShapeDtypeStruct((BATCH,) + st_blk, F32))
    else:
        in_specs.append(pl.BlockSpec((1, 1) + st_blk, lambda s, j: (s, layer, 0, 0, 0, 0)))
        args.append(s0)
    return pl.pallas_call(
        functools.partial(_gdn_kernel, nblk=nblk, has_s0=not ctx, nch=blk // CHUNK),
        grid=(nseq, nblk),
        in_specs=in_specs, out_specs=out_specs, out_shape=out_shape,
        scratch_shapes=[pltpu.VMEM((2, GDN_HEADS // GDN_G, GDN_GW, GDN_GW), F32)],
        compiler_params=_cp(("arbitrary", "arbitrary")),
        name="gdn_ctx" if ctx else "gdn_smp",
    )(*args)


def _group_rms(x, width):
    parts = []
    lane = lax.broadcasted_iota(jnp.int32, (x.shape[0], LANE), 1)
    lo_half = lane < 64
    for c in range(x.shape[1] // LANE):
        xc = x[:, c * LANE:(c + 1) * LANE]
        sq = xc * xc
        if width == LANE:
            ms = jnp.mean(sq, axis=-1, keepdims=True)
        else:
            s_lo = jnp.sum(jnp.where(lo_half, sq, 0.0), axis=-1, keepdims=True)
            s_hi = jnp.sum(jnp.where(lo_half, 0.0, sq), axis=-1, keepdims=True)
            ms = jnp.where(lo_half, s_lo, s_hi) * (1.0 / width)
        parts.append(xc * lax.rsqrt(ms + EPS))
    return jnp.concatenate(parts, axis=1)


def _merge_kernel(ymc_ref, yms_ref, ogcf_ref, ogcb_ref, ogsf_ref, ogsb_ref, gr_ref,
                  odcf_ref, odcb_ref, odsf_ref, odsb_ref,
                  dz_ref, gt_ref, *rest, split_h):
    if split_h:
        hc_ref, hs_ref, *rest = rest
        h_rows = lambda: _family_rows((hc_ref, hs_ref), TM_MERGE)
    else:
        h_ref, *rest = rest
        h_rows = lambda: h_ref[...]
    ga_ref, gn_ref, dn_ref, wb_ref, wo_ref, o_ref, ym_scr, og_scr, od_scr = rest
    is_ctx = pl.program_id(0) < NCTX // TM_MERGE

    @pl.when(is_ctx)
    def _():
        ym_scr[...] = ymc_ref[...]
        og_scr[...] = ogcf_ref[...] + ogcb_ref[...]
        od_scr[...] = odcf_ref[...] + odcb_ref[...]

    @pl.when(jnp.logical_not(is_ctx))
    def _():
        ym_scr[...] = yms_ref[...]
        og_scr[...] = ogsf_ref[...] + ogsb_ref[...]
        od_scr[...] = odsf_ref[...] + odsb_ref[...]

    y_gla = _group_rms(og_scr[...], GLA_DV) * gn_ref[0] * _silu_of_twice(gr_ref[...].astype(F32))
    y_gdn = _group_rms(od_scr[...], GDN_DV) * dn_ref[0] * _silu_of_twice(dz_ref[...].astype(F32))
    t = jnp.tanh(gt_ref[...].astype(F32))
    d = D_MODEL
    acc = None
    for n, y_n in enumerate((ym_scr[...], y_gla.astype(BF16), y_gdn.astype(BF16))):
        p = _dot(y_n, wb_ref[0, n])
        gated = p + p * t[:, n * d:(n + 1) * d]
        acc = gated if acc is None else acc + gated
    y = _dot(acc.astype(BF16), wo_ref[0])
    o_ref[...] = h_rows() + ga_ref[0] * y


def _merge(y_mla, o_gla, o_gdn, proj, h, mod3, gla_norm_p, gdn_norm_p, w_branch_b, w_out_b, layer):
    tm = TM_MERGE
    bw = 512
    nc = NCTX // tm
    split_h = isinstance(h, tuple)
    h_specs = _family_specs(tm, D_MODEL) if split_h else [pl.BlockSpec((tm, D_MODEL), lambda i: (i, 0))]
    h_args = list(h) if split_h else [h]

    def cmap(i):
        return jnp.minimum(i, nc - 1)

    def smap(i):
        return jnp.maximum(i - nc, 0)

    return pl.pallas_call(
        functools.partial(_merge_kernel, split_h=split_h),
        grid=(R // tm,),
        in_specs=[pl.BlockSpec((tm, bw), lambda i: (cmap(i), 0)),
                  pl.BlockSpec((tm, bw), lambda i: (smap(i), 0)),
                  pl.BlockSpec((tm, bw), lambda i: (cmap(i), 0)),
                  pl.BlockSpec((tm, bw), lambda i: (cmap(i), 0)),
                  pl.BlockSpec((tm, bw), lambda i: (smap(i), 0)),
                  pl.BlockSpec((tm, bw), lambda i: (smap(i), 0)),
                  pl.BlockSpec((tm, bw), lambda i: (i, PC_GR // bw)),
                  pl.BlockSpec((tm, bw), lambda i: (cmap(i), 0)),
                  pl.BlockSpec((tm, bw), lambda i: (cmap(i), 0)),
                  pl.BlockSpec((tm, bw), lambda i: (smap(i), 0)),
                  pl.BlockSpec((tm, bw), lambda i: (smap(i), 0)),
                  pl.BlockSpec((tm, bw), lambda i: (i, PC_DZ // bw)),
                  pl.BlockSpec((tm, 3 * D_MODEL), lambda i: (i, PC_GATES // (3 * D_MODEL)))] + h_specs + [
                  _mod_spec(layer, 2, tm),
                  pl.BlockSpec((1, 1, bw), lambda i: (layer, 0, 0)),
                  pl.BlockSpec((1, 1, bw), lambda i: (layer, 0, 0)),
                  pl.BlockSpec((1, 3, bw, D_MODEL), lambda i: (layer, 0, 0, 0), pipeline_mode=pl.Buffered(1)),
                  pl.BlockSpec((1, D_MODEL, D_MODEL), lambda i: (layer, 0, 0), pipeline_mode=pl.Buffered(1))],
        out_specs=pl.BlockSpec((tm, D_MODEL), lambda i: (i, 0)),
        out_shape=jax.ShapeDtypeStruct((R, D_MODEL), F32),
        scratch_shapes=[pltpu.VMEM((tm, bw), BF16), pltpu.VMEM((tm, bw), F32), pltpu.VMEM((tm, bw), F32)],
        compiler_params=_cp(("arbitrary",)),
        name="merge",
    )(y_mla[0], y_mla[1], o_gla[0][0], o_gla[0][1], o_gla[1][0], o_gla[1][1], proj,
      o_gdn[0][0], o_gdn[0][1], o_gdn[1][0], o_gdn[1][1],
      proj, proj, *h_args, mod3, gla_norm_p, gdn_norm_p, w_branch_b, w_out_b)


def _ffn_kernel(h_ref, g_ref, sc_ref, sh_ref, gf_ref, wi_ref, wo_ref, *rest, final):
    if final:
        fn_ref, *o_refs = rest
    else:
        gn_ref, scn_ref, shn_ref, *o_refs = rest
    h = h_ref[...]
    xf = (_rms(h, g_ref[0]) * (1.0 + sc_ref[0]) + sh_ref[0]).astype(BF16)

    def gate_up(c):
        f0 = c * TF_FFN
        return (_dot(xf, wi_ref[0, :, f0:f0 + TF_FFN]),
                _dot(xf, wi_ref[0, :, D_FF + f0:D_FF + f0 + TF_FFN]))

    n_chunks = D_FF // TF_FFN
    nxt = gate_up(0)
    acc = None
    for c in range(n_chunks):
        g, u = nxt
        if c + 1 < n_chunks:
            nxt = gate_up(c + 1)
        part = _dot((_silu(g) * u).astype(BF16), wo_ref[0, c * TF_FFN:(c + 1) * TF_FFN, :])
        acc = part if acc is None else acc + part
    out = h + gf_ref[0] * acc
    if not final:
        o_refs[0][...] = out
        o_refs[1][...] = (_rms(out, gn_ref[0]) * (1.0 + scn_ref[0]) + shn_ref[0]).astype(BF16)
    else:
        out = _rms(out, fn_ref[...])
        is_ctx = pl.program_id(0) < NCTX // TM_FFN

        @pl.when(is_ctx)
        def _():
            o_refs[0][...] = out

        @pl.when(jnp.logical_not(is_ctx))
        def _():
            o_refs[1][...] = out


def _ffn(h, mod3, norm_ffn, wi_b, wo_b, tail, layer, final):
    tm = TM_FFN
    if final:
        nc = NCTX // tm
        out_specs = [pl.BlockSpec((tm, D_MODEL), lambda i: (jnp.minimum(i, nc - 1), 0)),
                     pl.BlockSpec((tm, D_MODEL), lambda i: (jnp.maximum(i - nc, 0), 0))]
        out_shape = [jax.ShapeDtypeStruct((NCTX, D_MODEL), F32), jax.ShapeDtypeStruct((NSMP, D_MODEL), F32)]
        tail_specs = [pl.BlockSpec((1, D_MODEL), lambda i: (0, 0))]
        tail_args = [tail.reshape(1, D_MODEL)]
    else:
        out_specs = [pl.BlockSpec((tm, D_MODEL), lambda i: (i, 0))] * 2
        out_shape = [jax.ShapeDtypeStruct((R, D_MODEL), F32), jax.ShapeDtypeStruct((R, D_MODEL), BF16)]
        tail_specs = [pl.BlockSpec((1, 1, D_MODEL), lambda i: (layer + 1, 0, 0)),
                      _mod_spec(layer + 1, 1, tm), _mod_spec(layer + 1, 0, tm)]
        tail_args = [tail.reshape(DEPTH, 1, D_MODEL), mod3, mod3]
    return pl.pallas_call(
        functools.partial(_ffn_kernel, final=final),
        grid=(R // tm,),
        in_specs=[pl.BlockSpec((tm, D_MODEL), lambda i: (i, 0)),
                  pl.BlockSpec((1, 1, D_MODEL), lambda i: (layer, 0, 0)),
                  _mod_spec(layer, 4, tm),
                  _mod_spec(layer, 3, tm),
                  _mod_spec(layer, 5, tm),
                  pl.BlockSpec((1, D_MODEL, 2 * D_FF), lambda i: (layer, 0, 0), pipeline_mode=pl.Buffered(1)),
                  pl.BlockSpec((1, D_FF, D_MODEL), lambda i: (layer, 0, 0), pipeline_mode=pl.Buffered(1))]
        + tail_specs,
        out_specs=out_specs, out_shape=out_shape,
        compiler_params=_cp(("arbitrary",)),
        name="ffn_final" if final else "ffn",
    )(h, norm_ffn.reshape(DEPTH, 1, D_MODEL), mod3, mod3, mod3, wi_b, wo_b, *tail_args)


def _rope_partner(w, axis=-1):
    h = AXIS_DIM // 2
    axis = axis % w.ndim
    parts = []
    for a in range(2):
        x1 = lax.slice_in_dim(w, a * AXIS_DIM, a * AXIS_DIM + h, axis=axis)
        x2 = lax.slice_in_dim(w, a * AXIS_DIM + h, (a + 1) * AXIS_DIM, axis=axis)
        parts += [-x2, x1]
    return jnp.concatenate(parts, axis=axis)


def _pack_w_in_kernel(w_ref, o_ref):
    o_ref[0] = _pack_channels(w_ref[0]).astype(BF16)


def _pack_w_in(w_in):
    tc = 256
    return pl.pallas_call(
        _pack_w_in_kernel,
        grid=(DEPTH, D_MODEL // tc),
        in_specs=[pl.BlockSpec((1, w_in.shape[-1], tc), lambda l, i: (l, 0, i))],
        out_specs=pl.BlockSpec((1, PROJ_W, tc), lambda l, i: (l, 0, i)),
        out_shape=jax.ShapeDtypeStruct((DEPTH, PROJ_W, D_MODEL), BF16),
        compiler_params=_cp(("arbitrary", "arbitrary")),
        name="pack_w_in",
    )(jnp.swapaxes(w_in, 1, 2))


def _pack_channels(wt):
    c = [0]
    for s in (Q_LORA, KV_LORA + ROPE_DIM, 256, 256, 512, 512, 2 * GLA_RANK, 1536, 512,
              2 * GDN_HEADS, 2 * GDN_HEADS, 3 * D_MODEL):
        c.append(c[-1] + s)
    mq, mkv, gq, gk, gv, gr, glr, dqkv, dz, da, db, gates = [wt[c[i]:c[i + 1]] for i in range(12)]

    def zeros(n):
        return jnp.zeros((n,) + wt.shape[1:], wt.dtype)

    kr = mkv[KV_LORA:]
    mla = [mq, mkv[:KV_LORA], kr, _rope_partner(kr, axis=0), zeros(64)]
    dab = []
    for d in range(2):
        dab += [da[d * GDN_HEADS:(d + 1) * GDN_HEADS], db[d * GDN_HEADS:(d + 1) * GDN_HEADS],
                zeros(LANE - 2 * GDN_HEADS)]
    return jnp.concatenate([0.5 * gates, dqkv, 0.5 * dz, gv, 0.5 * gr] + mla
                           + [gq, gk, glr, zeros(LANE - 2 * GLA_RANK)]
                           + dab + [zeros(PROJ_W - PC_DAB - 2 * LANE)], axis=0)


def _pack_mla_weights(w_uq, w_ukv):
    l = w_uq.shape[0]
    qh = w_uq.reshape(l, Q_LORA, MLA_HEADS, MLA_NOPE + ROPE_DIM)
    zq = jnp.zeros((l, Q_LORA, MLA_HEADS, HEAD_W - MLA_NOPE - ROPE_DIM), w_uq.dtype)
    zn = jnp.zeros((l, Q_LORA, MLA_HEADS, MLA_NOPE), w_uq.dtype)
    wq = jnp.concatenate([qh, zq], axis=-1).reshape(l, Q_LORA, MLA_HEADS * HEAD_W)
    wq_sw = jnp.concatenate([zn, _rope_partner(qh[..., MLA_NOPE:]), zq], axis=-1).reshape(l, Q_LORA, MLA_HEADS * HEAD_W)
    wq_p = jnp.concatenate([wq, wq_sw], axis=-1).astype(BF16)

    kvh = w_ukv.reshape(l, KV_LORA, MLA_HEADS, MLA_NOPE + MLA_V)
    zk = jnp.zeros((l, KV_LORA, MLA_HEADS, HEAD_W - MLA_NOPE), w_ukv.dtype)
    wk_top = jnp.concatenate([kvh[..., :MLA_NOPE], zk], axis=-1).reshape(l, KV_LORA, MLA_HEADS * HEAD_W)
    place = np.zeros((KV_LORA, MLA_HEADS, HEAD_W), np.float32)
    idx = np.arange(ROPE_DIM)
    for rep in range(2):
        place[rep * ROPE_DIM + idx, :, MLA_NOPE + idx] = 1.0
    wk_bot = jnp.broadcast_to(jnp.asarray(place.reshape(1, KV_LORA, MLA_HEADS * HEAD_W)),
                              (l, KV_LORA, MLA_HEADS * HEAD_W))
    wk_p = jnp.concatenate([wk_top, wk_bot], axis=1).astype(BF16)
    wv_p = jnp.transpose(kvh[..., MLA_NOPE:], (0, 2, 3, 1)).reshape(l, VT_ROWS, KV_LORA).astype(BF16)
    return wq_p, wk_p, wv_p


def _rope_tables(tm):
    f32 = np.float32
    rows = DEC_SEQ // GRID_W
    row = np.repeat(np.arange(rows, dtype=np.float64), GRID_W)
    col = np.tile(np.arange(GRID_W, dtype=np.float64), rows)
    inv = ROPE_THETA ** (-np.arange(0, AXIS_DIM, 2, dtype=np.float64) / AXIS_DIM)
    ang_r, ang_c = row[:, None] * inv, col[:, None] * inv
    cos32 = np.concatenate([np.cos(ang_r)] * 2 + [np.cos(ang_c)] * 2, axis=-1)
    sin32 = np.concatenate([np.sin(ang_r)] * 2 + [np.sin(ang_c)] * 2, axis=-1)
    cos32 = np.concatenate([np.ones((tm, ROPE_DIM)), cos32], axis=0)
    sin32 = np.concatenate([np.zeros((tm, ROPE_DIM)), sin32], axis=0)
    n = cos32.shape[0]
    scale = (MLA_NOPE + ROPE_DIM) ** -0.5 * math.log2(math.e)
    pad = HEAD_W - MLA_NOPE - ROPE_DIM
    cq = np.concatenate([np.ones((n, MLA_NOPE)), cos32, np.zeros((n, pad))], axis=-1) * scale
    sq = np.concatenate([np.zeros((n, MLA_NOPE)), sin32, np.zeros((n, pad))], axis=-1) * scale
    ck = np.concatenate([cos32, sin32, np.zeros((n, HEAD_W - 2 * ROPE_DIM))], axis=-1)
    return jnp.asarray(np.concatenate([cq, sq, ck], axis=-1).astype(f32))


def _pad_lanes(x, n):
    return jnp.pad(x, [(0, 0)] * (x.ndim - 1) + [(0, n - x.shape[-1])])


def kernel(x_prompt, x_sample, cache_mla, state_gla, state_gdn, c, c_ctx, w_mod, b_mod, norm_mix, w_in,
           mla_q_norm, mla_w_uq, mla_kv_norm, mla_w_ukv, gla_w_gate, gla_b_gate, gla_norm, gdn_conv,
           gdn_a_log, gdn_dt_bias, gdn_norm, w_branch, w_out, norm_ffn, ffn_w_in, ffn_w_out, final_norm):
    assert x_prompt.shape == (BATCH, SEQ, D_MODEL) and x_sample.shape == (DEC_BATCH, DEC_SEQ, D_MODEL)
    assert cache_mla.shape == (DEC_BATCH, DEPTH, PAST_LEN, KV_LORA + ROPE_DIM) and w_in.shape[0] == DEPTH
    w_in_p = _pack_w_in(w_in)
    wq_p, wk_p, wv_p = _pack_mla_weights(mla_w_uq, mla_w_ukv)
    tab = _rope_tables(TM_PREP)
    wg_p = jnp.zeros((DEPTH, 2, LANE, GLA_HEADS * GLA_DK), F32)
    for d in range(2):
        wg_p = wg_p.at[:, d, d * GLA_RANK:(d + 1) * GLA_RANK, :].set(gla_w_gate[:, d])
    wg_p = wg_p.astype(BF16)
    bg_p = gla_b_gate.reshape(DEPTH, 2, 1, GLA_HEADS * GLA_DK)
    alog_p = _pad_lanes(gdn_a_log, LANE).reshape(DEPTH, 2, 1, LANE)
    dtb_p = _pad_lanes(gdn_dt_bias, LANE).reshape(DEPTH, 2, 1, LANE)
    gla_norm_p = jnp.tile(gla_norm, (1, GLA_HEADS)).reshape(DEPTH, 1, GLA_HEADS * GLA_DV)
    gdn_norm_p = jnp.tile(gdn_norm, (1, GDN_HEADS)).reshape(DEPTH, 1, GDN_HEADS * GDN_DV)
    w_branch_b = (0.5 * w_branch).astype(BF16)
    w_out_b = w_out.astype(BF16)
    wi_b = ffn_w_in.astype(BF16)
    wo_b = ffn_w_out.astype(BF16)
    cache_p = _pad_lanes(cache_mla, 2 * KV_LORA)
    cond8 = jnp.concatenate([c_ctx[None, :], c, jnp.zeros((MOD_ROWS - 1 - DEC_BATCH, D_MODEL), F32)], axis=0)

    mod = _modulation(cond8, w_mod, b_mod)
    mod3 = mod.reshape(DEPTH * MOD_ROWS * 6, 1, D_MODEL)
    kc, vc = _kv_cache(cache_p, wk_p, wv_p)

    h = (x_prompt.reshape(NCTX, D_MODEL), x_sample.reshape(NSMP, D_MODEL))
    kv_list, gla_list, gdn_list = [], [], []
    for l in range(DEPTH):
        proj = _inproj(h if l == 0 else xn, mod3, norm_mix, w_in_p, l)
        qp, kp, vtp, own = _mla_prep(proj, tab, mla_q_norm, mla_kv_norm, wq_p, wk_p, wv_p, l)
        y_mla = (_attention_ctx(qp, kp, vtp), _attention_smp(qp, kp, vtp, kc, vc, l))
        ogf_c, ogb_c, sg = _gla(proj, wg_p, bg_p, None, l, True)
        ogf_s, ogb_s = _gla(proj, wg_p, bg_p, state_gla, l, False)
        qkv = _gdn_pre(proj, gdn_conv, l)
        odf_c, odb_c, sd = _gdn(qkv, proj, alog_p, dtb_p, None, l, True)
        odf_s, odb_s = _gdn(qkv, proj, alog_p, dtb_p, state_gdn, l, False)
        h = _merge(y_mla, ((ogf_c, ogb_c), (ogf_s, ogb_s)), ((odf_c, odb_c), (odf_s, odb_s)), proj, h, mod3,
                   gla_norm_p, gdn_norm_p,
                   w_branch_b, w_out_b, l)
        if l == DEPTH - 1:
            h = _ffn(h, mod3, norm_ffn, wi_b, wo_b, final_norm, l, True)
        else:
            h, xn = _ffn(h, mod3, norm_ffn, wi_b, wo_b, norm_mix, l, False)
        kv_list.append(own[:NCTX, :KV_LORA + ROPE_DIM].reshape(BATCH, SEQ, KV_LORA + ROPE_DIM))
        gla_list.append(sg)
        gdn_list.append(sd)

    y_prompt = h[0].reshape(BATCH, SEQ, D_MODEL)
    y_sample = h[1].reshape(DEC_BATCH, DEC_SEQ, D_MODEL)
    return (y_prompt, y_sample, jnp.stack(kv_list, axis=1), jnp.stack(gla_list, axis=1),
            jnp.stack(gdn_list, axis=1))
```

```python
import functools
import math

import numpy as np
import jax
import jax.numpy as jnp
from jax import lax
from jax.experimental import pallas as pl
from jax.experimental.pallas import tpu as pltpu

F32 = jnp.float32
BF16 = jnp.bfloat16

D_MODEL = 1024
BATCH = 16
SEQ = 256
DEPTH = 2
DEC_BATCH = 2
DEC_SEQ = 4096
PAST_LEN = 256
GRID_W = 64
CHUNK = 64
EPS = 1e-6
MLA_HEADS = 8
MLA_NOPE = 64
ROPE_DIM = 32
AXIS_DIM = ROPE_DIM // 2
MLA_V = 64
Q_LORA = 256
KV_LORA = 128
ROPE_THETA = 10000.0
GLA_HEADS = 4
GLA_DK = 64
GLA_DV = 128
GLA_RANK = 16
GLA_TAU = 16.0
GDN_HEADS = 8
GDN_DK = 64
GDN_DV = 64
D_FF = ((8 * D_MODEL + 3 * 256 - 1) // (3 * 256)) * 256
MOD_W = 6 * D_MODEL

NCTX = BATCH * SEQ
NSMP = DEC_BATCH * DEC_SEQ
R = NCTX + NSMP
LANE = 128
HEAD_W = 128
VT_ROWS = MLA_HEADS * MLA_V
SEQ_BLK = 256
GDN_BLK_SMP = 512
GLA_BLK_CTX = 256
GLA_BLK_SMP = 1024
MOD_ROWS = 8

PC_GATES = 0
PC_DQKV = 3072
PC_DZ = 4608
PC_GV = 5120
PC_GR = 5632
PC_MLA = 6144
PC_GQ = 6656
PC_GK = 6912
PC_GLR = 7168
PC_DAB = 7296
PROJ_W = 7680

TM_IN = 1024
TN_IN = 2560
TM_PREP = 1024
GDN_PRE_BLK = 1024
TM_MERGE = 512
TM_FFN = 512
TF_FFN = 256
TQ_ATT = 256
TK_ATT = 256
ONES_ROWS = 16
ATT_AHEAD = 8
assert all(NCTX % t == 0 and NSMP % t == 0 for t in (TM_IN, TM_PREP, GDN_PRE_BLK, TM_MERGE, TM_FFN))
assert NCTX % DEC_SEQ == 0 and DEC_SEQ % TQ_ATT == 0 and DEC_SEQ % TK_ATT == 0 and PAST_LEN <= TK_ATT
assert SEQ % SEQ_BLK == 0 and DEC_SEQ % GDN_BLK_SMP == 0 and SEQ % GLA_BLK_CTX == 0 and DEC_SEQ % GLA_BLK_SMP == 0
assert GDN_PRE_BLK % SEQ == 0 and DEC_SEQ % GDN_PRE_BLK == 0 and D_FF % TF_FFN == 0 and PROJ_W % TN_IN == 0

VMEM_V7X = 64 * 1024 * 1024
VMEM_LIMIT = VMEM_V7X * 3 // 4


def _cp(sem):
    return pltpu.CompilerParams(dimension_semantics=sem, vmem_limit_bytes=VMEM_LIMIT)


def _dot(a, b):
    return jnp.dot(a, b, preferred_element_type=F32)


def _dot_nt(a, b):
    return lax.dot_general(a, b, (((1,), (1,)), ((), ())), preferred_element_type=F32)


def _dot_tn(a, b):
    return lax.dot_general(a, b, (((0,), (0,)), ((), ())), preferred_element_type=F32)


def _split2(x):
    hi = x.astype(BF16)
    lo = (x - hi.astype(F32)).astype(BF16)
    return hi, lo


def _split3(x):
    hi = x.astype(BF16)
    r1 = x - hi.astype(F32)
    mid = r1.astype(BF16)
    lo = (r1 - mid.astype(F32)).astype(BF16)
    return hi, mid, lo


def _cumsum_rows(tri_b, x):
    hi, mid, lo = _split3(x)
    return _dot(tri_b, hi) + (_dot(tri_b, mid) + _dot(tri_b, lo))


def _sigmoid(x):
    return 0.5 * jnp.tanh(0.5 * x) + 0.5


def _silu(x):
    return x * _sigmoid(x)


def _silu_of_twice(hx):
    return hx + hx * jnp.tanh(hx)


def _softplus(x):
    return jnp.maximum(x, 0.0) + jnp.log(1.0 + jnp.exp(-jnp.abs(x)))


def _log_sigmoid(x):
    return jnp.minimum(x, 0.0) - jnp.log(1.0 + jnp.exp(-jnp.abs(x)))


def _rms(x, g):
    return x * lax.rsqrt(jnp.mean(x * x, axis=-1, keepdims=True) + EPS) * g


def _mod_row(row_start):
    return jnp.where(row_start < NCTX, 0, 1 + (row_start - NCTX) // DEC_SEQ)


def _mod_spec(layer, which, tm, axis=0):
    def imap(*idx):
        return ((layer * MOD_ROWS + _mod_row(idx[axis] * tm)) * 6 + which, 0, 0)
    return pl.BlockSpec((1, 1, D_MODEL), imap)


def _mod_kernel(c_ref, w_ref, b_ref, o_ref):
    x = _silu(c_ref[...])
    xh, xl = _split2(x)
    wh, wl = _split2(w_ref[0])
    o_ref[0] = _dot(xh, wh) + (_dot(xl, wh) + _dot(xh, wl)) + b_ref[0]


def _modulation(cond8, w_mod, b_mod):
    tn = 1536
    return pl.pallas_call(
        _mod_kernel,
        grid=(DEPTH, MOD_W // tn),
        in_specs=[pl.BlockSpec((MOD_ROWS, D_MODEL), lambda l, j: (0, 0)),
                  pl.BlockSpec((1, D_MODEL, tn), lambda l, j: (l, 0, j)),
                  pl.BlockSpec((1, 1, tn), lambda l, j: (l, 0, j))],
        out_specs=pl.BlockSpec((1, MOD_ROWS, tn), lambda l, j: (l, 0, j)),
        out_shape=jax.ShapeDtypeStruct((DEPTH, MOD_ROWS, MOD_W), F32),
        compiler_params=_cp(("arbitrary", "arbitrary")),
        name="modulation",
    )(cond8, w_mod, b_mod.reshape(DEPTH, 1, MOD_W))


def _family_specs(tm, width, axis=0):
    nc = NCTX // tm
    return [pl.BlockSpec((tm, width), lambda *idx: (jnp.minimum(idx[axis], nc - 1), 0)),
            pl.BlockSpec((tm, width), lambda *idx: (jnp.maximum(idx[axis] - nc, 0), 0))]


def _family_rows(refs, tm, axis=0):
    is_ctx = pl.program_id(axis) < NCTX // tm
    return jnp.where(is_ctx, refs[0][...], refs[1][...])


def _inproj_kernel(*refs, normed):
    if normed:
        xn_ref, w_ref, o_ref = refs
        xn = xn_ref[...]
    else:
        hc_ref, hs_ref, g_ref, sc_ref, sh_ref, w_ref, o_ref = refs
        y = _rms(_family_rows((hc_ref, hs_ref), TM_IN, axis=1), g_ref[0])
        xn = (y * (1.0 + sc_ref[0]) + sh_ref[0]).astype(BF16)
    o_ref[...] = _dot_nt(xn, w_ref[0]).astype(BF16)


def _inproj(h, mod3, norm_mix, w_in_p, layer):
    normed = not isinstance(h, tuple)
    if normed:
        in_specs = [pl.BlockSpec((TM_IN, D_MODEL), lambda j, i: (i, 0))]
        args = [h]
    else:
        in_specs = _family_specs(TM_IN, D_MODEL, axis=1) + [
            pl.BlockSpec((1, 1, D_MODEL), lambda j, i: (layer, 0, 0)),
            _mod_spec(layer, 1, TM_IN, axis=1),
            _mod_spec(layer, 0, TM_IN, axis=1)]
        args = list(h) + [norm_mix.reshape(DEPTH, 1, D_MODEL), mod3, mod3]
    return pl.pallas_call(
        functools.partial(_inproj_kernel, normed=normed),
        grid=(PROJ_W // TN_IN, R // TM_IN),
        in_specs=in_specs + [pl.BlockSpec((1, TN_IN, D_MODEL), lambda j, i: (layer, j, 0))],
        out_specs=pl.BlockSpec((TM_IN, TN_IN), lambda j, i: (i, j)),
        out_shape=jax.ShapeDtypeStruct((R, PROJ_W), BF16),
        compiler_params=_cp(("arbitrary", "arbitrary")),
        name="inproj",
    )(*args, w_in_p)


def _mla_prep_kernel(pm_ref, tab_ref, qn_ref, kvn_ref, wq_ref, wk_ref, wv_ref,
                     q_ref, k_ref, v_ref, own_ref):
    pm = pm_ref[...].astype(F32)
    tab = tab_ref[...]
    qn = _rms(pm[:, :Q_LORA], qn_ref[0]).astype(BF16)
    q2 = _dot(qn, wq_ref[0])
    hw = MLA_HEADS * HEAD_W
    cq = jnp.tile(tab[:, :HEAD_W], (1, MLA_HEADS))
    sq = jnp.tile(tab[:, HEAD_W:2 * HEAD_W], (1, MLA_HEADS))
    q_ref[...] = (q2[:, :hw] * cq + q2[:, hw:] * sq).astype(BF16)
    ckv = _rms(pm[:, Q_LORA:Q_LORA + KV_LORA], kvn_ref[0])
    kr = pm[:, Q_LORA + KV_LORA:] * tab[:, 2 * HEAD_W:]
    lhs = jnp.concatenate([ckv, kr], axis=1)
    own_ref[...] = lhs
    lb = lhs.astype(BF16)
    k_ref[...] = _dot(lb, wk_ref[0]).astype(BF16)
    v_ref[...] = _dot_nt(wv_ref[0], lb[:, :KV_LORA]).astype(BF16)


def _mla_prep(proj, tab, q_norm, kv_norm, wq_p, wk_p, wv_p, layer):
    tm = TM_PREP
    hw = MLA_HEADS * HEAD_W

    def tab_map(i):
        r0 = i * tm
        return (jnp.where(r0 < NCTX, 0, 1 + ((r0 - NCTX) % DEC_SEQ) // tm), 0)

    return pl.pallas_call(
        _mla_prep_kernel,
        grid=(R // tm,),
        in_specs=[pl.BlockSpec((tm, 512), lambda i: (i, PC_MLA // 512)),
                  pl.BlockSpec((tm, 3 * HEAD_W), tab_map),
                  pl.BlockSpec((1, 1, Q_LORA), lambda i: (layer, 0, 0)),
                  pl.BlockSpec((1, 1, KV_LORA), lambda i: (layer, 0, 0)),
                  pl.BlockSpec((1, Q_LORA, 2 * hw), lambda i: (layer, 0, 0)),
                  pl.BlockSpec((1, 2 * KV_LORA, hw), lambda i: (layer, 0, 0)),
                  pl.BlockSpec((1, VT_ROWS, KV_LORA), lambda i: (layer, 0, 0))],
        out_specs=[pl.BlockSpec((tm, hw), lambda i: (i, 0)),
                   pl.BlockSpec((tm, hw), lambda i: (i, 0)),
                   pl.BlockSpec((VT_ROWS, tm), lambda i: (0, i)),
                   pl.BlockSpec((tm, 2 * KV_LORA), lambda i: (i, 0))],
        out_shape=[jax.ShapeDtypeStruct((R, hw), BF16),
                   jax.ShapeDtypeStruct((R, hw), BF16),
                   jax.ShapeDtypeStruct((VT_ROWS, R), BF16),
                   jax.ShapeDtypeStruct((R, 2 * KV_LORA), F32)],
        compiler_params=_cp(("arbitrary",)),
        name="mla_prep",
    )(proj, tab, q_norm.reshape(DEPTH, 1, Q_LORA), kv_norm.reshape(DEPTH, 1, KV_LORA),
      wq_p, wk_p, wv_p)


def _kv_cache_kernel(c_ref, wk_ref, wv_ref, k_ref, v_ref):
    lb = c_ref[0, 0].astype(BF16)
    k_ref[0, 0] = _dot(lb, wk_ref[0]).astype(BF16)
    v_ref[0, 0] = _dot_nt(wv_ref[0], lb[:, :KV_LORA]).astype(BF16)


def _kv_cache(cache_p, wk_p, wv_p):
    hw = MLA_HEADS * HEAD_W
    return pl.pallas_call(
        _kv_cache_kernel,
        grid=(DEC_BATCH, DEPTH),
        in_specs=[pl.BlockSpec((1, 1, PAST_LEN, 2 * KV_LORA), lambda b, l: (b, l, 0, 0)),
                  pl.BlockSpec((1, 2 * KV_LORA, hw), lambda b, l: (l, 0, 0)),
                  pl.BlockSpec((1, VT_ROWS, KV_LORA), lambda b, l: (l, 0, 0))],
        out_specs=[pl.BlockSpec((1, 1, PAST_LEN, hw), lambda b, l: (b, l, 0, 0)),
                   pl.BlockSpec((1, 1, VT_ROWS, PAST_LEN), lambda b, l: (b, l, 0, 0))],
        out_shape=[jax.ShapeDtypeStruct((DEC_BATCH, DEPTH, PAST_LEN, hw), BF16),
                   jax.ShapeDtypeStruct((DEC_BATCH, DEPTH, VT_ROWS, PAST_LEN), BF16)],
        compiler_params=_cp(("arbitrary", "arbitrary")),
        name="kv_cache",
    )(cache_p, wk_p, wv_p)


def _attn_kernel(*refs, heads, has_cache):
    if has_cache:
        q_ref, k_ref, vt_ref, kc_ref, vct_ref, o_ref = refs
    else:
        q_ref, k_ref, vt_ref, o_ref = refs
    n_keys = k_ref.shape[0]
    tk = min(TK_ATT, n_keys)
    tiles = ([("cache", 0, PAST_LEN)] if has_cache else []) + [("own", s0, tk) for s0 in range(0, n_keys, tk)]
    units = [(h, t) for t in range(len(tiles)) for h in range(heads)]
    qs = [q_ref[:, h * HEAD_W:(h + 1) * HEAD_W] for h in range(heads)]

    def scores(h, t):
        kind, s0, n = tiles[t]
        sl = slice(h * HEAD_W, (h + 1) * HEAD_W)
        keys = kc_ref[0, 0, :, sl] if kind == "cache" else k_ref[s0:s0 + n, sl]
        st = _dot_nt(keys, qs[h])
        return st, jnp.max(st, axis=0, keepdims=True)

    def values_t(h, t):
        kind, s0, n = tiles[t]
        sv = slice(h * MLA_V, (h + 1) * MLA_V)
        vt = vct_ref[0, 0, sv, :] if kind == "cache" else vt_ref[sv, s0:s0 + n]
        return jnp.concatenate([vt, jnp.ones((ONES_ROWS, n), BF16)], axis=0)

    ahead = ATT_AHEAD
    pending = {u: scores(*u) for u in units[:ahead]}
    m, acc = [None] * heads, [None] * heads
    for i, (h, t) in enumerate(units):
        if i + ahead < len(units):
            pending[units[i + ahead]] = scores(*units[i + ahead])
        st, mt = pending.pop((h, t))
        if m[h] is None:
            m[h] = mt
            acc[h] = _dot(values_t(h, t), jnp.exp2(st - mt).astype(BF16))
        else:
            m_new = jnp.maximum(m[h], mt)
            alpha = jnp.exp2(m[h] - m_new)
            acc[h] = alpha * acc[h] + _dot(values_t(h, t), jnp.exp2(st - m_new).astype(BF16))
            m[h] = m_new
    outs = [acc[h][:MLA_V] / acc[h][MLA_V:MLA_V + 1] for h in range(heads)]
    o_ref[...] = jnp.concatenate(outs, axis=0).T.astype(BF16)


def _attention_ctx(qp, kp, vtp):
    hw = MLA_HEADS * HEAD_W
    return pl.pallas_call(
        functools.partial(_attn_kernel, heads=MLA_HEADS, has_cache=False),
        grid=(BATCH,),
        in_specs=[pl.BlockSpec((SEQ, hw), lambda s: (s, 0)),
                  pl.BlockSpec((SEQ, hw), lambda s: (s, 0)),
                  pl.BlockSpec((VT_ROWS, SEQ), lambda s: (0, s))],
        out_specs=pl.BlockSpec((SEQ, MLA_HEADS * MLA_V), lambda s: (s, 0)),
        out_shape=jax.ShapeDtypeStruct((NCTX, MLA_HEADS * MLA_V), BF16),
        compiler_params=_cp(("arbitrary",)),
        name="attn_ctx",
    )(qp, kp, vtp)


def _attention_smp(qp, kp, vtp, kc, vct, layer):
    hpb = 4
    wq = hpb * HEAD_W
    wv = hpb * MLA_V
    nq = DEC_SEQ // TQ_ATT
    qoff = NCTX // TQ_ATT
    koff = NCTX // DEC_SEQ
    return pl.pallas_call(
        functools.partial(_attn_kernel, heads=hpb, has_cache=True),
        grid=(DEC_BATCH, MLA_HEADS // hpb, nq),
        in_specs=[pl.BlockSpec((TQ_ATT, wq), lambda b, g, i: (qoff + b * nq + i, g)),
                  pl.BlockSpec((DEC_SEQ, wq), lambda b, g, i: (koff + b, g)),
                  pl.BlockSpec((wv, DEC_SEQ), lambda b, g, i: (g, koff + b)),
                  pl.BlockSpec((1, 1, PAST_LEN, wq), lambda b, g, i: (b, layer, 0, g)),
                  pl.BlockSpec((1, 1, wv, PAST_LEN), lambda b, g, i: (b, layer, g, 0))],
        out_specs=pl.BlockSpec((TQ_ATT, hpb * MLA_V), lambda b, g, i: (b * nq + i, g)),
        out_shape=jax.ShapeDtypeStruct((NSMP, MLA_HEADS * MLA_V), BF16),
        compiler_params=_cp(("arbitrary", "arbitrary", "arbitrary")),
        name="attn_smp",
    )(qp, kp, vtp, kc, vct)


def _scan_specs(ctx, blk, widths_and_cols, dir_cols=()):
    nseq, seq_len = (BATCH, SEQ) if ctx else (DEC_BATCH, DEC_SEQ)
    nblk = seq_len // blk
    roff = 0 if ctx else NCTX // blk

    def rb(dd):
        return lambda s, j: roff + s * nblk + (j if dd == 0 else nblk - 1 - j)

    specs = []
    for dd in range(2):
        r = rb(dd)
        specs += [pl.BlockSpec((blk, w), lambda s, j, r=r, c=c: (r(s, j), c)) for w, c in widths_and_cols]
        specs += [pl.BlockSpec((blk, w), lambda s, j, r=r, c=c + dd: (r(s, j), c)) for w, c in dir_cols]
    out_maps = [lambda s, j, r=rb(dd): (r(s, j) - roff, 0) for dd in range(2)]
    return nseq, nblk, specs, out_maps


def _gla_kernel(*refs, nblk, has_s0, nch):
    per_dir = [refs[0:4], refs[4:8]]
    wg_ref, bg_ref = refs[8:10]
    if has_s0:
        s0_ref, of_ref, ob_ref, s_scr = refs[10:]
    else:
        of_ref, ob_ref, sfin_ref, s_scr = refs[10:]
    o_refs = [of_ref, ob_ref]
    j = pl.program_id(1)

    @pl.when(j == 0)
    def _():
        if has_s0:
            s_scr[...] = s0_ref[0, 0]
        else:
            s_scr[...] = jnp.zeros_like(s_scr)

    r64 = lax.broadcasted_iota(jnp.int32, (CHUNK, CHUNK), 0)
    c64 = lax.broadcasted_iota(jnp.int32, (CHUNK, CHUNK), 1)
    incl = [c64 <= r64, c64 >= r64]
    tri_b = [jnp.where(m, 1.0, 0.0).astype(BF16) for m in incl]
    heads = [(slice(h * GLA_DK, (h + 1) * GLA_DK), slice(h * GLA_DV, (h + 1) * GLA_DV))
             for h in range(GLA_HEADS)]

    ch = []
    for c in range(nch):
        for dd in range(2):
            ci = c if dd == 0 else nch - 1 - c
            rows = slice(ci * CHUNK, (ci + 1) * CHUNK)
            glr_ref = per_dir[dd][3]
            ch.append(dict(dd=dd, rows=rows, z=_dot(glr_ref[rows, :], wg_ref[0, dd]) + bg_ref[0, dd]))
    for t in ch:
        t["la"] = _split3(_log_sigmoid(t["z"]) * (1.0 / GLA_TAU))
    for t in ch:
        hi, mid, lo = t["la"]
        tb = tri_b[t["dd"]]
        t["bc"] = _dot(tb, hi) + (_dot(tb, mid) + _dot(tb, lo))
    for t in ch:
        q_ref, k_ref, v_ref, _ = per_dir[t["dd"]]
        rows, bc = t["rows"], t["bc"]
        q = q_ref[rows, :].astype(F32) * (GLA_DK ** -0.5)
        k = k_ref[rows, :].astype(F32)
        bl = bc[CHUNK - 1:CHUNK] if t["dd"] == 0 else bc[0:1]
        t["qd"] = (q * jnp.exp(bc)).astype(BF16)
        t["ki"] = (k * jnp.exp(-bc)).astype(BF16)
        t["ke"] = (k * jnp.exp(bl - bc)).astype(BF16)
        t["dec_t"] = jnp.transpose(jnp.broadcast_to(jnp.exp(bl), (LANE, GLA_HEADS * GLA_DK)))
        t["v"] = v_ref[rows, :]
    for t in ch:
        t["a"] = [jnp.where(incl[t["dd"]], _dot_nt(t["qd"][:, sk], t["ki"][:, sk]), 0.0).astype(BF16)
                  for sk, _ in heads]
    for t in ch:
        t["oi"] = [_dot(t["a"][h], t["v"][:, sv]) for h, (_, sv) in enumerate(heads)]
        t["upd"] = [_dot_tn(t["ke"][:, sk], t["v"][:, sv]) for sk, sv in heads]
    states = [[s_scr[dd, h] for h in range(GLA_HEADS)] for dd in range(2)]
    for t in ch:
        st = states[t["dd"]]
        t["s_in"] = [s.astype(BF16) for s in st]
        for h, (sk, _) in enumerate(heads):
            st[h] = t["dec_t"][sk, :] * st[h] + t["upd"][h]
    for t in ch:
        for h, (sk, sv) in enumerate(heads):
            o_refs[t["dd"]][t["rows"], sv] = t["oi"][h] + _dot(t["qd"][:, sk], t["s_in"][h])
    for dd in range(2):
        for h in range(GLA_HEADS):
            s_scr[dd, h] = states[dd][h]

    if not has_s0:
        @pl.when(j == nblk - 1)
        def _():
            sfin_ref[0] = s_scr[...]


def _gla(proj, wg_p, bg_p, s0, layer, ctx):
    blk = GLA_BLK_CTX if ctx else GLA_BLK_SMP
    hk = GLA_HEADS * GLA_DK
    hv = GLA_HEADS * GLA_DV
    nseq, nblk, in_specs, out_maps = _scan_specs(
        ctx, blk, [(hk, PC_GQ // hk), (hk, PC_GK // hk), (hv, PC_GV // hv), (LANE, PC_GLR // LANE)])
    in_specs += [pl.BlockSpec((1, 2, LANE, hk), lambda s, j: (layer, 0, 0, 0)),
                 pl.BlockSpec((1, 2, 1, hk), lambda s, j: (layer, 0, 0, 0))]
    args = [proj] * 8 + [wg_p, bg_p]
    nrows = NCTX if ctx else NSMP
    out_specs = [pl.BlockSpec((blk, hv), m) for m in out_maps]
    out_shape = [jax.ShapeDtypeStruct((nrows, hv), F32)] * 2
    st_blk = (2, GLA_HEADS, GLA_DK, GLA_DV)
    if ctx:
        out_specs.append(pl.BlockSpec((1,) + st_blk, lambda s, j: (s, 0, 0, 0, 0)))
        out_shape.append(jax.ShapeDtypeStruct((BATCH,) + st_blk, F32))
    else:
        in_specs.append(pl.BlockSpec((1, 1) + st_blk, lambda s, j: (s, layer, 0, 0, 0, 0)))
        args.append(s0)
    return pl.pallas_call(
        functools.partial(_gla_kernel, nblk=nblk, has_s0=not ctx, nch=blk // CHUNK),
        grid=(nseq, nblk),
        in_specs=in_specs, out_specs=out_specs, out_shape=out_shape,
        scratch_shapes=[pltpu.VMEM(st_blk, F32)],
        compiler_params=_cp(("arbitrary", "arbitrary")),
        name="gla_ctx" if ctx else "gla_smp",
    )(*args)


GDN_G = 2
GDN_GW = GDN_G * GDN_DK
HALO = 16


def _gdn_pre_kernel(x_ref, xp_ref, xn_ref, cw_ref, o_ref):
    blk = x_ref.shape[0]
    hk = GDN_HEADS * GDN_DK
    sub = SEQ
    cw = 0.5 * cw_ref[0]
    rowi = lax.broadcasted_iota(jnp.int32, (sub, 1), 0)
    lane = lax.broadcasted_iota(jnp.int32, (sub, LANE), 1)
    lo_half = lane < GDN_DK

    def l2n(t, scale):
        parts = []
        for c in range(hk // LANE):
            tc = t[:, c * LANE:(c + 1) * LANE]
            sq = tc * tc
            s_lo = jnp.sum(jnp.where(lo_half, sq, 0.0), axis=-1, keepdims=True)
            s_hi = jnp.sum(jnp.where(lo_half, 0.0, sq), axis=-1, keepdims=True)
            parts.append(tc * (lax.rsqrt(jnp.where(lo_half, s_lo, s_hi) + EPS) * scale))
        return jnp.concatenate(parts, axis=1)

    for s in range(blk // sub):
        rows = slice(s * sub, (s + 1) * sub)
        x = x_ref[rows, :].astype(F32)
        r0 = pl.program_id(0) * blk + s * sub
        pos0 = jnp.where(r0 < NCTX, r0 % SEQ, (r0 - NCTX) % DEC_SEQ)
        seq_len = jnp.where(r0 < NCTX, SEQ, DEC_SEQ)
        before = xp_ref[...] if s == 0 else x_ref[s * sub - HALO:s * sub, :]
        after = xn_ref[...] if s == blk // sub - 1 else x_ref[(s + 1) * sub:(s + 1) * sub + HALO, :]
        prev_row = jnp.where(pos0 > 0, before.astype(F32)[HALO - 1:HALO], 0.0)
        next_row = jnp.where(pos0 + sub < seq_len, after.astype(F32)[0:1], 0.0)
        x_prev = jnp.where(rowi == 0, prev_row, pltpu.roll(x, 1, 0))
        x_next = jnp.where(rowi == sub - 1, next_row, pltpu.roll(x, sub - 1, 0))
        y = _silu_of_twice(x_prev * cw[0:1] + x * cw[1:2] + x_next * cw[2:3])
        o_ref[rows, :hk] = l2n(y[:, :hk], GDN_DK ** -0.5).astype(BF16)
        o_ref[rows, hk:2 * hk] = l2n(y[:, hk:2 * hk], 1.0).astype(BF16)
        o_ref[rows, 2 * hk:] = y[:, 2 * hk:].astype(BF16)


def _gdn_pre(proj, conv_w, layer):
    cw = 2 * GDN_HEADS * GDN_DK + GDN_HEADS * GDN_DV
    blk = GDN_PRE_BLK
    halo = blk // HALO
    last_h = R // HALO - 1
    return pl.pallas_call(
        _gdn_pre_kernel,
        grid=(R // blk,),
        in_specs=[pl.BlockSpec((blk, cw), lambda i: (i, PC_DQKV // cw)),
                  pl.BlockSpec((HALO, cw), lambda i: (jnp.maximum(i * halo - 1, 0), PC_DQKV // cw)),
                  pl.BlockSpec((HALO, cw), lambda i: (jnp.minimum((i + 1) * halo, last_h), PC_DQKV // cw)),
                  pl.BlockSpec((1, 3, cw), lambda i: (layer, 0, 0))],
        out_specs=pl.BlockSpec((blk, cw), lambda i: (i, 0)),
        out_shape=jax.ShapeDtypeStruct((R, cw), BF16),
        compiler_params=_cp(("arbitrary",)),
        name="gdn_pre",
    )(proj, proj, proj, conv_w)


def _gdn_kernel(*refs, nblk, has_s0, nch):
    per_dir = [refs[0:4], refs[4:8]]
    alog_ref, dtb_ref = refs[8:10]
    if has_s0:
        s0_ref, of_ref, ob_ref, s_scr = refs[10:]
    else:
        of_ref, ob_ref, sfin_ref, s_scr = refs[10:]
    o_refs = [of_ref, ob_ref]
    j = pl.program_id(1)
    ngrp = GDN_HEADS // GDN_G

    def head_block(h):
        g, hh = divmod(h, GDN_G)
        return g, slice(hh * GDN_DK, (hh + 1) * GDN_DK), slice(hh * GDN_DV, (hh + 1) * GDN_DV)

    @pl.when(j == 0)
    def _():
        s_scr[...] = jnp.zeros_like(s_scr)
        if has_s0:
            for dd in range(2):
                for h in range(GDN_HEADS):
                    g, rk, rv = head_block(h)
                    s_scr[dd, g, rk, rv] = s0_ref[0, 0, dd, h]

    row = lax.broadcasted_iota(jnp.int32, (CHUNK, GDN_GW), 0)
    col = lax.broadcasted_iota(jnp.int32, (CHUNK, GDN_GW), 1) % CHUNK
    r64 = lax.broadcasted_iota(jnp.int32, (CHUNK, CHUNK), 0)
    c64 = lax.broadcasted_iota(jnp.int32, (CHUNK, CHUNK), 1)
    dir_masks = []
    for dd in range(2):
        lag = (row - col) if dd == 0 else (col - row)
        tri = (c64 <= r64) if dd == 0 else (c64 >= r64)
        dir_masks.append(dict(incl=lag >= 0, strict=lag > 0,
                              tri_b=jnp.where(tri, 1.0, 0.0).astype(BF16)))
    eye = col == row
    eye_f = jnp.where(eye, 1.0, 0.0)
    blk8 = (row // 8) == (col // 8)
    blk16 = (row // 16) == (col // 16)
    blk32 = (row // 32) == (col // 32)
    brow = lax.broadcasted_iota(jnp.int32, (GDN_GW, GDN_GW), 0) // CHUNK
    bcol = lax.broadcasted_iota(jnp.int32, (GDN_GW, GDN_GW), 1) // CHUNK
    same_head = brow == bcol

    def bdiag(t):
        tb = t.astype(BF16)
        return jnp.where(same_head, jnp.concatenate([tb] * GDN_G, axis=0), jnp.zeros((), BF16))

    def bdmm(a, b):
        return _dot(a.astype(BF16), bdiag(b))

    kq_cache = {}

    def setup(dd, c):
        q_ref, k_ref, v_ref, ab_ref = per_dir[dd]
        mk = dir_masks[dd]
        ci = c if dd == 0 else nch - 1 - c
        rows = slice(ci * CHUNK, (ci + 1) * CHUNK)
        ab = ab_ref[rows, :].astype(F32)
        neg_a = -jnp.exp(alog_ref[0, dd])
        gam = _cumsum_rows(mk["tri_b"], neg_a * _softplus(ab + dtb_ref[0, dd]))
        beta_all = _sigmoid(ab)
        out = []
        for g in range(ngrp):
            gl = slice(g * GDN_GW, (g + 1) * GDN_GW)
            gcol = jnp.concatenate(
                [jnp.broadcast_to(gam[:, g * GDN_G + h:g * GDN_G + h + 1], (CHUNK, GDN_DK))
                 for h in range(GDN_G)], axis=1)
            bcol_ = jnp.concatenate(
                [jnp.broadcast_to(beta_all[:, GDN_HEADS + g * GDN_G + h:GDN_HEADS + g * GDN_G + h + 1],
                                  (CHUNK, GDN_DK)) for h in range(GDN_G)], axis=1)
            grow = jnp.sum(jnp.where(eye, gcol, 0.0), axis=0, keepdims=True)
            glast = gcol[CHUNK - 1:CHUNK] if dd == 0 else gcol[0:1]
            decay = jnp.where(mk["incl"], jnp.exp(gcol - grow), 0.0)
            egc = jnp.exp(gcol)
            qg = q_ref[rows, gl].astype(F32)
            kg = k_ref[rows, gl].astype(F32)
            vg = v_ref[rows, gl].astype(F32)
            key = (ci, g) if nblk == 1 else (dd, ci, g)
            if key not in kq_cache:
                kq_cache[key] = _dot_nt(jnp.concatenate([k_ref[rows, gl], q_ref[rows, gl]], axis=0),
                                        bdiag(k_ref[rows, gl]))
            kq = kq_cache[key]
            out.append(dict(dd=dd, g=g, rows=rows, gl=gl,
                            m=jnp.where(mk["strict"], bcol_ * kq[:CHUNK] * decay, 0.0),
                            aqk=(kq[CHUNK:] * decay).astype(BF16),
                            vb=vg * bcol_, kb=kg * (bcol_ * egc), qe=qg * egc,
                            kend=(kg * jnp.exp(glast - gcol)).astype(BF16),
                            eg=jnp.exp(glast)))
        return out

    def stages(ch):
        pairs = list(zip(ch[0::2], ch[1::2]))

        def bdmm_pairs(lhs, rhs, out):
            for ta, tb in pairs:
                a = jnp.concatenate([ta[lhs].astype(BF16), tb[lhs].astype(BF16)], axis=0)
                w = jnp.concatenate([bdiag(ta[rhs]), bdiag(tb[rhs])], axis=1)
                y = _dot(a, w)
                ta[out] = y[:CHUNK, :GDN_GW]
                tb[out] = y[CHUNK:, GDN_GW:]

        def neumann_a():
            for t in ch:
                t["n8"] = jnp.where(blk8, t["m"], 0.0)
            bdmm_pairs("n8", "n8", "n2")

        def neumann_b():
            for t in ch:
                t["p1"] = eye_f - t["n8"]
            bdmm_pairs("n2", "n2", "n4")
            bdmm_pairs("p1", "n2", "p1n2")
            for t in ch:
                t["p1"] = t["p1"] + t["p1n2"]

        def neumann_c():
            bdmm_pairs("p1", "n4", "p1n4")
            for t in ch:
                t["dinv"] = t["p1"] + t["p1n4"]

        def double_a(off):
            def run():
                for t in ch:
                    t["off"] = jnp.where(off, t["m"], 0.0)
                bdmm_pairs("dinv", "off", "dl")
            return run

        def double_b():
            bdmm_pairs("dl", "dinv", "dld")
            for t in ch:
                t["dinv"] = t["dinv"] - t["dld"]

        def solve():
            for t in ch:
                uw = _dot(t["dinv"].astype(BF16), jnp.concatenate([bdiag(t["vb"]), bdiag(t["kb"])], axis=1))
                t["u"] = uw[:, :GDN_GW]
                t["w"] = uw[:, GDN_GW:]

        def fold():
            for t in ch:
                wu = jnp.concatenate([t["w"], t["u"]], axis=1).astype(BF16)
                cb = _dot_tn(t["kend"], wu)
                t["c"] = jnp.where(same_head, cb[:, :GDN_GW], 0.0).astype(BF16)
                t["b"] = jnp.where(same_head, cb[:, GDN_GW:], 0.0)
                ao = _dot(t["aqk"], jnp.concatenate([bdiag(t["w"]), bdiag(t["u"])], axis=1))
                t["qt"] = (t["qe"] - ao[:, :GDN_GW]).astype(BF16)
                t["o"] = ao[:, GDN_GW:]

        out = [neumann_a, neumann_b, neumann_c]
        for inner, outer in ((blk8, blk16), (blk16, blk32), (blk32, None)):
            off = jnp.logical_not(inner) if outer is None else jnp.logical_and(outer, jnp.logical_not(inner))
            out += [double_a(off), double_b]
        return out + [solve, fold]

    states = [[s_scr[dd, g] for g in range(ngrp)] for dd in range(2)]

    def recur(cur):
        ys = [_dot(jnp.concatenate([t["qt"], t["c"]], axis=0), states[t["dd"]][t["g"]].astype(BF16))
              for t in cur]
        for t, y in zip(cur, ys):
            o_refs[t["dd"]][t["rows"], t["gl"]] = y[:CHUNK] + t["o"]
            states[t["dd"]][t["g"]] = t["eg"] * states[t["dd"]][t["g"]] - y[CHUNK:] + t["b"]

    half = nch // 2
    chains = {}
    for c in range(half):
        for dd in range(2):
            chains[(dd, c)] = setup(dd, c)
    later = [(dd, c) for c in range(half, nch) for dd in range(2)]
    wave_a = [t for c in range(half) for dd in range(2) for t in chains[(dd, c)]]
    stages_a = stages(wave_a)
    stride = max(1, len(stages_a) // (len(later) + 1))
    for k, stage in enumerate(stages_a):
        stage()
        if k % stride == 0 and later:
            key = later.pop(0)
            chains[key] = setup(*key)
    for key in later:
        chains[key] = setup(*key)
    wave_b = [t for c in range(half, nch) for dd in range(2) for t in chains[(dd, c)]]
    stages_b = stages(wave_b)
    stride = max(1, len(stages_b) // (half + 1))
    done = 0
    for k, stage in enumerate(stages_b):
        stage()
        if k % stride == 1 % stride and done < half:
            recur(chains[(0, done)] + chains[(1, done)])
            done += 1
    for c in range(done, nch):
        recur(chains[(0, c)] + chains[(1, c)])
    for dd in range(2):
        for g in range(ngrp):
            s_scr[dd, g] = states[dd][g]

    if not has_s0:
        @pl.when(j == nblk - 1)
        def _():
            for dd in range(2):
                for h in range(GDN_HEADS):
                    g, rk, rv = head_block(h)
                    sfin_ref[0, dd, h] = s_scr[dd, g, rk, rv]


def _gdn(qkv, proj, alog_p, dtb_p, s0, layer, ctx):
    blk = SEQ_BLK if ctx else GDN_BLK_SMP
    nseq, nblk = (BATCH, SEQ // blk) if ctx else (DEC_BATCH, DEC_SEQ // blk)
    roff = 0 if ctx else NCTX // blk
    hk = GDN_HEADS * GDN_DK
    hv = GDN_HEADS * GDN_DV

    def rb(dd):
        return lambda s, j: roff + s * nblk + (j if dd == 0 else nblk - 1 - j)

    in_specs, args = [], []
    for dd in range(2):
        r = rb(dd)
        in_specs += [pl.BlockSpec((blk, hk), lambda s, j, r=r: (r(s, j), 0)),
                     pl.BlockSpec((blk, hk), lambda s, j, r=r: (r(s, j), 1)),
                     pl.BlockSpec((blk, hv), lambda s, j, r=r: (r(s, j), 2 * hk // hv)),
                     pl.BlockSpec((blk, LANE), lambda s, j, r=r, dd=dd: (r(s, j), PC_DAB // LANE + dd))]
        args += [qkv, qkv, qkv, proj]
    in_specs += [pl.BlockSpec((1, 2, 1, LANE), lambda s, j: (layer, 0, 0, 0))] * 2
    args += [alog_p, dtb_p]
    nrows = NCTX if ctx else NSMP
    out_specs = [pl.BlockSpec((blk, hv), lambda s, j, r=rb(dd): (r(s, j) - roff, 0)) for dd in range(2)]
    out_shape = [jax.ShapeDtypeStruct((nrows, hv), F32)] * 2
    st_blk = (2, GDN_HEADS, GDN_DK, GDN_DV)
    if ctx:
        out_specs.append(pl.BlockSpec((1,) + st_blk, lambda s, j: (s, 0, 0, 0, 0)))
        out_shape.append(jax.ShapeDtypeStruct((BATCH,) + st_blk, F32))
    else:
        in_specs.append(pl.BlockSpec((1, 1) + st_blk, lambda s, j: (s, layer, 0, 0, 0, 0)))
        args.append(s0)
    return pl.pallas_call(
        functools.partial(_gdn_kernel, nblk=nblk, has_s0=not ctx, nch=blk // CHUNK),
        grid=(nseq, nblk),
        in_specs=in_specs, out_specs=out_specs, out_shape=out_shape,
        scratch_shapes=[pltpu.VMEM((2, GDN_HEADS // GDN_G, GDN_GW, GDN_GW), F32)],
        compiler_params=_cp(("arbitrary", "arbitrary")),
        name="gdn_ctx" if ctx else "gdn_smp",
    )(*args)


def _group_rms(x, width):
    parts = []
    lane = lax.broadcasted_iota(jnp.int32, (x.shape[0], LANE), 1)
    lo_half = lane < 64
    for c in range(x.shape[1] // LANE):
        xc = x[:, c * LANE:(c + 1) * LANE]
        sq = xc * xc
        if width == LANE:
            ms = jnp.mean(sq, axis=-1, keepdims=True)
        else:
            s_lo = jnp.sum(jnp.where(lo_half, sq, 0.0), axis=-1, keepdims=True)
            s_hi = jnp.sum(jnp.where(lo_half, 0.0, sq), axis=-1, keepdims=True)
            ms = jnp.where(lo_half, s_lo, s_hi) * (1.0 / width)
        parts.append(xc * lax.rsqrt(ms + EPS))
    return jnp.concatenate(parts, axis=1)


def _merge_kernel(ymc_ref, yms_ref, ogcf_ref, ogcb_ref, ogsf_ref, ogsb_ref, gr_ref,
                  odcf_ref, odcb_ref, odsf_ref, odsb_ref,
                  dz_ref, gt_ref, *rest, split_h):
    if split_h:
        hc_ref, hs_ref, *rest = rest
        h_rows = lambda: _family_rows((hc_ref, hs_ref), TM_MERGE)
    else:
        h_ref, *rest = rest
        h_rows = lambda: h_ref[...]
    ga_ref, gn_ref, dn_ref, wb_ref, wo_ref, o_ref, ym_scr, og_scr, od_scr = rest
    is_ctx = pl.program_id(0) < NCTX // TM_MERGE

    @pl.when(is_ctx)
    def _():
        ym_scr[...] = ymc_ref[...]
        og_scr[...] = ogcf_ref[...] + ogcb_ref[...]
        od_scr[...] = odcf_ref[...] + odcb_ref[...]

    @pl.when(jnp.logical_not(is_ctx))
    def _():
        ym_scr[...] = yms_ref[...]
        og_scr[...] = ogsf_ref[...] + ogsb_ref[...]
        od_scr[...] = odsf_ref[...] + odsb_ref[...]

    y_gla = _group_rms(og_scr[...], GLA_DV) * gn_ref[0] * _silu_of_twice(gr_ref[...].astype(F32))
    y_gdn = _group_rms(od_scr[...], GDN_DV) * dn_ref[0] * _silu_of_twice(dz_ref[...].astype(F32))
    t = jnp.tanh(gt_ref[...].astype(F32))
    d = D_MODEL
    acc = None
    for n, y_n in enumerate((ym_scr[...], y_gla.astype(BF16), y_gdn.astype(BF16))):
        p = _dot(y_n, wb_ref[0, n])
        gated = p + p * t[:, n * d:(n + 1) * d]
        acc = gated if acc is None else acc + gated
    y = _dot(acc.astype(BF16), wo_ref[0])
    o_ref[...] = h_rows() + ga_ref[0] * y


def _merge(y_mla, o_gla, o_gdn, proj, h, mod3, gla_norm_p, gdn_norm_p, w_branch_b, w_out_b, layer):
    tm = TM_MERGE
    bw = 512
    nc = NCTX // tm
    split_h = isinstance(h, tuple)
    h_specs = _family_specs(tm, D_MODEL) if split_h else [pl.BlockSpec((tm, D_MODEL), lambda i: (i, 0))]
    h_args = list(h) if split_h else [h]

    def cmap(i):
        return jnp.minimum(i, nc - 1)

    def smap(i):
        return jnp.maximum(i - nc, 0)

    return pl.pallas_call(
        functools.partial(_merge_kernel, split_h=split_h),
        grid=(R // tm,),
        in_specs=[pl.BlockSpec((tm, bw), lambda i: (cmap(i), 0)),
                  pl.BlockSpec((tm, bw), lambda i: (smap(i), 0)),
                  pl.BlockSpec((tm, bw), lambda i: (cmap(i), 0)),
                  pl.BlockSpec((tm, bw), lambda i: (cmap(i), 0)),
                  pl.BlockSpec((tm, bw), lambda i: (smap(i), 0)),
                  pl.BlockSpec((tm, bw), lambda i: (smap(i), 0)),
                  pl.BlockSpec((tm, bw), lambda i: (i, PC_GR // bw)),
                  pl.BlockSpec((tm, bw), lambda i: (cmap(i), 0)),
                  pl.BlockSpec((tm, bw), lambda i: (cmap(i), 0)),
                  pl.BlockSpec((tm, bw), lambda i: (smap(i), 0)),
                  pl.BlockSpec((tm, bw), lambda i: (smap(i), 0)),
                  pl.BlockSpec((tm, bw), lambda i: (i, PC_DZ // bw)),
                  pl.BlockSpec((tm, 3 * D_MODEL), lambda i: (i, PC_GATES // (3 * D_MODEL)))] + h_specs + [
                  _mod_spec(layer, 2, tm),
                  pl.BlockSpec((1, 1, bw), lambda i: (layer, 0, 0)),
                  pl.BlockSpec((1, 1, bw), lambda i: (layer, 0, 0)),
                  pl.BlockSpec((1, 3, bw, D_MODEL), lambda i: (layer, 0, 0, 0), pipeline_mode=pl.Buffered(1)),
                  pl.BlockSpec((1, D_MODEL, D_MODEL), lambda i: (layer, 0, 0), pipeline_mode=pl.Buffered(1))],
        out_specs=pl.BlockSpec((tm, D_MODEL), lambda i: (i, 0)),
        out_shape=jax.ShapeDtypeStruct((R, D_MODEL), F32),
        scratch_shapes=[pltpu.VMEM((tm, bw), BF16), pltpu.VMEM((tm, bw), F32), pltpu.VMEM((tm, bw), F32)],
        compiler_params=_cp(("arbitrary",)),
        name="merge",
    )(y_mla[0], y_mla[1], o_gla[0][0], o_gla[0][1], o_gla[1][0], o_gla[1][1], proj,
      o_gdn[0][0], o_gdn[0][1], o_gdn[1][0], o_gdn[1][1],
      proj, proj, *h_args, mod3, gla_norm_p, gdn_norm_p, w_branch_b, w_out_b)


def _ffn_kernel(h_ref, g_ref, sc_ref, sh_ref, gf_ref, wi_ref, wo_ref, *rest, final):
    if final:
        fn_ref, *o_refs = rest
    else:
        gn_ref, scn_ref, shn_ref, *o_refs = rest
    h = h_ref[...]
    xf = (_rms(h, g_ref[0]) * (1.0 + sc_ref[0]) + sh_ref[0]).astype(BF16)

    def gate_up(c):
        f0 = c * TF_FFN
        return (_dot(xf, wi_ref[0, :, f0:f0 + TF_FFN]),
                _dot(xf, wi_ref[0, :, D_FF + f0:D_FF + f0 + TF_FFN]))

    n_chunks = D_FF // TF_FFN
    nxt = gate_up(0)
    acc = None
    for c in range(n_chunks):
        g, u = nxt
        if c + 1 < n_chunks:
            nxt = gate_up(c + 1)
        part = _dot((_silu(g) * u).astype(BF16), wo_ref[0, c * TF_FFN:(c + 1) * TF_FFN, :])
        acc = part if acc is None else acc + part
    out = h + gf_ref[0] * acc
    if not final:
        o_refs[0][...] = out
        o_refs[1][...] = (_rms(out, gn_ref[0]) * (1.0 + scn_ref[0]) + shn_ref[0]).astype(BF16)
    else:
        out = _rms(out, fn_ref[...])
        is_ctx = pl.program_id(0) < NCTX // TM_FFN

        @pl.when(is_ctx)
        def _():
            o_refs[0][...] = out

        @pl.when(jnp.logical_not(is_ctx))
        def _():
            o_refs[1][...] = out


def _ffn(h, mod3, norm_ffn, wi_b, wo_b, tail, layer, final):
    tm = TM_FFN
    if final:
        nc = NCTX // tm
        out_specs = [pl.BlockSpec((tm, D_MODEL), lambda i: (jnp.minimum(i, nc - 1), 0)),
                     pl.BlockSpec((tm, D_MODEL), lambda i: (jnp.maximum(i - nc, 0), 0))]
        out_shape = [jax.ShapeDtypeStruct((NCTX, D_MODEL), F32), jax.ShapeDtypeStruct((NSMP, D_MODEL), F32)]
        tail_specs = [pl.BlockSpec((1, D_MODEL), lambda i: (0, 0))]
        tail_args = [tail.reshape(1, D_MODEL)]
    else:
        out_specs = [pl.BlockSpec((tm, D_MODEL), lambda i: (i, 0))] * 2
        out_shape = [jax.ShapeDtypeStruct((R, D_MODEL), F32), jax.ShapeDtypeStruct((R, D_MODEL), BF16)]
        tail_specs = [pl.BlockSpec((1, 1, D_MODEL), lambda i: (layer + 1, 0, 0)),
                      _mod_spec(layer + 1, 1, tm), _mod_spec(layer + 1, 0, tm)]
        tail_args = [tail.reshape(DEPTH, 1, D_MODEL), mod3, mod3]
    return pl.pallas_call(
        functools.partial(_ffn_kernel, final=final),
        grid=(R // tm,),
        in_specs=[pl.BlockSpec((tm, D_MODEL), lambda i: (i, 0)),
                  pl.BlockSpec((1, 1, D_MODEL), lambda i: (layer, 0, 0)),
                  _mod_spec(layer, 4, tm),
                  _mod_spec(layer, 3, tm),
                  _mod_spec(layer, 5, tm),
                  pl.BlockSpec((1, D_MODEL, 2 * D_FF), lambda i: (layer, 0, 0), pipeline_mode=pl.Buffered(1)),
                  pl.BlockSpec((1, D_FF, D_MODEL), lambda i: (layer, 0, 0), pipeline_mode=pl.Buffered(1))]
        + tail_specs,
        out_specs=out_specs, out_shape=out_shape,
        compiler_params=_cp(("arbitrary",)),
        name="ffn_final" if final else "ffn",
    )(h, norm_ffn.reshape(DEPTH, 1, D_MODEL), mod3, mod3, mod3, wi_b, wo_b, *tail_args)


def _rope_partner(w, axis=-1):
    h = AXIS_DIM // 2
    axis = axis % w.ndim
    parts = []
    for a in range(2):
        x1 = lax.slice_in_dim(w, a * AXIS_DIM, a * AXIS_DIM + h, axis=axis)
        x2 = lax.slice_in_dim(w, a * AXIS_DIM + h, (a + 1) * AXIS_DIM, axis=axis)
        parts += [-x2, x1]
    return jnp.concatenate(parts, axis=axis)


def _pack_w_in_kernel(w_ref, o_ref):
    o_ref[0] = _pack_channels(w_ref[0]).astype(BF16)


def _pack_w_in(w_in):
    tc = 256
    return pl.pallas_call(
        _pack_w_in_kernel,
        grid=(DEPTH, D_MODEL // tc),
        in_specs=[pl.BlockSpec((1, w_in.shape[-1], tc), lambda l, i: (l, 0, i))],
        out_specs=pl.BlockSpec((1, PROJ_W, tc), lambda l, i: (l, 0, i)),
        out_shape=jax.ShapeDtypeStruct((DEPTH, PROJ_W, D_MODEL), BF16),
        compiler_params=_cp(("arbitrary", "arbitrary")),
        name="pack_w_in",
    )(jnp.swapaxes(w_in, 1, 2))


def _pack_channels(wt):
    c = [0]
    for s in (Q_LORA, KV_LORA + ROPE_DIM, 256, 256, 512, 512, 2 * GLA_RANK, 1536, 512,
              2 * GDN_HEADS, 2 * GDN_HEADS, 3 * D_MODEL):
        c.append(c[-1] + s)
    mq, mkv, gq, gk, gv, gr, glr, dqkv, dz, da, db, gates = [wt[c[i]:c[i + 1]] for i in range(12)]

    def zeros(n):
        return jnp.zeros((n,) + wt.shape[1:], wt.dtype)

    kr = mkv[KV_LORA:]
    mla = [mq, mkv[:KV_LORA], kr, _rope_partner(kr, axis=0), zeros(64)]
    dab = []
    for d in range(2):
        dab += [da[d * GDN_HEADS:(d + 1) * GDN_HEADS], db[d * GDN_HEADS:(d + 1) * GDN_HEADS],
                zeros(LANE - 2 * GDN_HEADS)]
    return jnp.concatenate([0.5 * gates, dqkv, 0.5 * dz, gv, 0.5 * gr] + mla
                           + [gq, gk, glr, zeros(LANE - 2 * GLA_RANK)]
                           + dab + [zeros(PROJ_W - PC_DAB - 2 * LANE)], axis=0)


def _pack_mla_weights(w_uq, w_ukv):
    l = w_uq.shape[0]
    qh = w_uq.reshape(l, Q_LORA, MLA_HEADS, MLA_NOPE + ROPE_DIM)
    zq = jnp.zeros((l, Q_LORA, MLA_HEADS, HEAD_W - MLA_NOPE - ROPE_DIM), w_uq.dtype)
    zn = jnp.zeros((l, Q_LORA, MLA_HEADS, MLA_NOPE), w_uq.dtype)
    wq = jnp.concatenate([qh, zq], axis=-1).reshape(l, Q_LORA, MLA_HEADS * HEAD_W)
    wq_sw = jnp.concatenate([zn, _rope_partner(qh[..., MLA_NOPE:]), zq], axis=-1).reshape(l, Q_LORA, MLA_HEADS * HEAD_W)
    wq_p = jnp.concatenate([wq, wq_sw], axis=-1).astype(BF16)

    kvh = w_ukv.reshape(l, KV_LORA, MLA_HEADS, MLA_NOPE + MLA_V)
    zk = jnp.zeros((l, KV_LORA, MLA_HEADS, HEAD_W - MLA_NOPE), w_ukv.dtype)
    wk_top = jnp.concatenate([kvh[..., :MLA_NOPE], zk], axis=-1).reshape(l, KV_LORA, MLA_HEADS * HEAD_W)
    place = np.zeros((KV_LORA, MLA_HEADS, HEAD_W), np.float32)
    idx = np.arange(ROPE_DIM)
    for rep in range(2):
        place[rep * ROPE_DIM + idx, :, MLA_NOPE + idx] = 1.0
    wk_bot = jnp.broadcast_to(jnp.asarray(place.reshape(1, KV_LORA, MLA_HEADS * HEAD_W)),
                              (l, KV_LORA, MLA_HEADS * HEAD_W))
    wk_p = jnp.concatenate([wk_top, wk_bot], axis=1).astype(BF16)
    wv_p = jnp.transpose(kvh[..., MLA_NOPE:], (0, 2, 3, 1)).reshape(l, VT_ROWS, KV_LORA).astype(BF16)
    return wq_p, wk_p, wv_p


def _rope_tables(tm):
    f32 = np.float32
    rows = DEC_SEQ // GRID_W
    row = np.repeat(np.arange(rows, dtype=np.float64), GRID_W)
    col = np.tile(np.arange(GRID_W, dtype=np.float64), rows)
    inv = ROPE_THETA ** (-np.arange(0, AXIS_DIM, 2, dtype=np.float64) / AXIS_DIM)
    ang_r, ang_c = row[:, None] * inv, col[:, None] * inv
    cos32 = np.concatenate([np.cos(ang_r)] * 2 + [np.cos(ang_c)] * 2, axis=-1)
    sin32 = np.concatenate([np.sin(ang_r)] * 2 + [np.sin(ang_c)] * 2, axis=-1)
    cos32 = np.concatenate([np.ones((tm, ROPE_DIM)), cos32], axis=0)
    sin32 = np.concatenate([np.zeros((tm, ROPE_DIM)), sin32], axis=0)
    n = cos32.shape[0]
    scale = (MLA_NOPE + ROPE_DIM) ** -0.5 * math.log2(math.e)
    pad = HEAD_W - MLA_NOPE - ROPE_DIM
    cq = np.concatenate([np.ones((n, MLA_NOPE)), cos32, np.zeros((n, pad))], axis=-1) * scale
    sq = np.concatenate([np.zeros((n, MLA_NOPE)), sin32, np.zeros((n, pad))], axis=-1) * scale
    ck = np.concatenate([cos32, sin32, np.zeros((n, HEAD_W - 2 * ROPE_DIM))], axis=-1)
    return jnp.asarray(np.concatenate([cq, sq, ck], axis=-1).astype(f32))


def _pad_lanes(x, n):
    return jnp.pad(x, [(0, 0)] * (x.ndim - 1) + [(0, n - x.shape[-1])])


def kernel(x_prompt, x_sample, cache_mla, state_gla, state_gdn, c, c_ctx, w_mod, b_mod, norm_mix, w_in,
           mla_q_norm, mla_w_uq, mla_kv_norm, mla_w_ukv, gla_w_gate, gla_b_gate, gla_norm, gdn_conv,
           gdn_a_log, gdn_dt_bias, gdn_norm, w_branch, w_out, norm_ffn, ffn_w_in, ffn_w_out, final_norm):
    assert x_prompt.shape == (BATCH, SEQ, D_MODEL) and x_sample.shape == (DEC_BATCH, DEC_SEQ, D_MODEL)
    assert cache_mla.shape == (DEC_BATCH, DEPTH, PAST_LEN, KV_LORA + ROPE_DIM) and w_in.shape[0] == DEPTH
    w_in_p = _pack_w_in(w_in)
    wq_p, wk_p, wv_p = _pack_mla_weights(mla_w_uq, mla_w_ukv)
    tab = _rope_tables(TM_PREP)
    wg_p = jnp.zeros((DEPTH, 2, LANE, GLA_HEADS * GLA_DK), F32)
    for d in range(2):
        wg_p = wg_p.at[:, d, d * GLA_RANK:(d + 1) * GLA_RANK, :].set(gla_w_gate[:, d])
    wg_p = wg_p.astype(BF16)
    bg_p = gla_b_gate.reshape(DEPTH, 2, 1, GLA_HEADS * GLA_DK)
    alog_p = _pad_lanes(gdn_a_log, LANE).reshape(DEPTH, 2, 1, LANE)
    dtb_p = _pad_lanes(gdn_dt_bias, LANE).reshape(DEPTH, 2, 1, LANE)
    gla_norm_p = jnp.tile(gla_norm, (1, GLA_HEADS)).reshape(DEPTH, 1, GLA_HEADS * GLA_DV)
    gdn_norm_p = jnp.tile(gdn_norm, (1, GDN_HEADS)).reshape(DEPTH, 1, GDN_HEADS * GDN_DV)
    w_branch_b = (0.5 * w_branch).astype(BF16)
    w_out_b = w_out.astype(BF16)
    wi_b = ffn_w_in.astype(BF16)
    wo_b = ffn_w_out.astype(BF16)
    cache_p = _pad_lanes(cache_mla, 2 * KV_LORA)
    cond8 = jnp.concatenate([c_ctx[None, :], c, jnp.zeros((MOD_ROWS - 1 - DEC_BATCH, D_MODEL), F32)], axis=0)

    mod = _modulation(cond8, w_mod, b_mod)
    mod3 = mod.reshape(DEPTH * MOD_ROWS * 6, 1, D_MODEL)
    kc, vc = _kv_cache(cache_p, wk_p, wv_p)

    h = (x_prompt.reshape(NCTX, D_MODEL), x_sample.reshape(NSMP, D_MODEL))
    kv_list, gla_list, gdn_list = [], [], []
    for l in range(DEPTH):
        proj = _inproj(h if l == 0 else xn, mod3, norm_mix, w_in_p, l)
        qp, kp, vtp, own = _mla_prep(proj, tab, mla_q_norm, mla_kv_norm, wq_p, wk_p, wv_p, l)
        y_mla = (_attention_ctx(qp, kp, vtp), _attention_smp(qp, kp, vtp, kc, vc, l))
        ogf_c, ogb_c, sg = _gla(proj, wg_p, bg_p, None, l, True)
        ogf_s, ogb_s = _gla(proj, wg_p, bg_p, state_gla, l, False)
        qkv = _gdn_pre(proj, gdn_conv, l)
        odf_c, odb_c, sd = _gdn(qkv, proj, alog_p, dtb_p, None, l, True)
        odf_s, odb_s = _gdn(qkv, proj, alog_p, dtb_p, state_gdn, l, False)
        h = _merge(y_mla, ((ogf_c, ogb_c), (ogf_s, ogb_s)), ((odf_c, odb_c), (odf_s, odb_s)), proj, h, mod3,
                   gla_norm_p, gdn_norm_p,
                   w_branch_b, w_out_b, l)
        if l == DEPTH - 1:
            h = _ffn(h, mod3, norm_ffn, wi_b, wo_b, final_norm, l, True)
        else:
            h, xn = _ffn(h, mod3, norm_ffn, wi_b, wo_b, norm_mix, l, False)
        kv_list.append(own[:NCTX, :KV_LORA + ROPE_DIM].reshape(BATCH, SEQ, KV_LORA + ROPE_DIM))
        gla_list.append(sg)
        gdn_list.append(sd)

    y_prompt = h[0].reshape(BATCH, SEQ, D_MODEL)
    y_sample = h[1].reshape(DEC_BATCH, DEC_SEQ, D_MODEL)
    return (y_prompt, y_sample, jnp.stack(kv_list, axis=1), jnp.stack(gla_list, axis=1),
            jnp.stack(gdn_list, axis=1))
```

```python
import functools
import math

import numpy as np
import jax
import jax.numpy as jnp
from jax import lax
from jax.experimental import pallas as pl
from jax.experimental.pallas import tpu as pltpu

F32 = jnp.float32
BF16 = jnp.bfloat16

D_MODEL = 1024
BATCH = 16
SEQ = 256
DEPTH = 2
DEC_BATCH = 2
DEC_SEQ = 4096
PAST_LEN = 256
GRID_W = 64
CHUNK = 64
EPS = 1e-6
MLA_HEADS = 8
MLA_NOPE = 64
ROPE_DIM = 32
AXIS_DIM = ROPE_DIM // 2
MLA_V = 64
Q_LORA = 256
KV_LORA = 128
ROPE_THETA = 10000.0
GLA_HEADS = 4
GLA_DK = 64
GLA_DV = 128
GLA_RANK = 16
GLA_TAU = 16.0
GDN_HEADS = 8
GDN_DK = 64
GDN_DV = 64
D_FF = ((8 * D_MODEL + 3 * 256 - 1) // (3 * 256)) * 256
MOD_W = 6 * D_MODEL

NCTX = BATCH * SEQ
NSMP = DEC_BATCH * DEC_SEQ
R = NCTX + NSMP
LANE = 128
HEAD_W = 128
VT_ROWS = MLA_HEADS * MLA_V
SEQ_BLK = 256
GDN_BLK_SMP = 512
GLA_BLK_CTX = 256
GLA_BLK_SMP = 1024
MOD_ROWS = 8

PC_GATES = 0
PC_DQKV = 3072
PC_DZ = 4608
PC_GV = 5120
PC_GR = 5632
PC_MLA = 6144
PC_GQ = 6656
PC_GK = 6912
PC_GLR = 7168
PC_DAB = 7296
PROJ_W = 7680

TM_IN = 1024
TN_IN = 2560
TM_PREP = 1024
GDN_PRE_BLK = 1024
TM_MERGE = 512
TM_FFN = 512
TF_FFN = 256
TQ_ATT = 256
TK_ATT = 256
ONES_ROWS = 16
ATT_AHEAD = 8
assert all(NCTX % t == 0 and NSMP % t == 0 for t in (TM_IN, TM_PREP, GDN_PRE_BLK, TM_MERGE, TM_FFN))
assert NCTX % DEC_SEQ == 0 and DEC_SEQ % TQ_ATT == 0 and DEC_SEQ % TK_ATT == 0 and PAST_LEN <= TK_ATT
assert SEQ % SEQ_BLK == 0 and DEC_SEQ % GDN_BLK_SMP == 0 and SEQ % GLA_BLK_CTX == 0 and DEC_SEQ % GLA_BLK_SMP == 0
assert GDN_PRE_BLK % SEQ == 0 and DEC_SEQ % GDN_PRE_BLK == 0 and D_FF % TF_FFN == 0 and PROJ_W % TN_IN == 0

VMEM_V7X = 64 * 1024 * 1024
VMEM_LIMIT = VMEM_V7X * 3 // 4


def _cp(sem):
    return pltpu.CompilerParams(dimension_semantics=sem, vmem_limit_bytes=VMEM_LIMIT)


def _dot(a, b):
    return jnp.dot(a, b, preferred_element_type=F32)


def _dot_nt(a, b):
    return lax.dot_general(a, b, (((1,), (1,)), ((), ())), preferred_element_type=F32)


def _dot_tn(a, b):
    return lax.dot_general(a, b, (((0,), (0,)), ((), ())), preferred_element_type=F32)


def _split2(x):
    hi = x.astype(BF16)
    lo = (x - hi.astype(F32)).astype(BF16)
    return hi, lo


def _split3(x):
    hi = x.astype(BF16)
    r1 = x - hi.astype(F32)
    mid = r1.astype(BF16)
    lo = (r1 - mid.astype(F32)).astype(BF16)
    return hi, mid, lo


def _cumsum_rows(tri_b, x):
    hi, mid, lo = _split3(x)
    return _dot(tri_b, hi) + (_dot(tri_b, mid) + _dot(tri_b, lo))


def _sigmoid(x):
    return 0.5 * jnp.tanh(0.5 * x) + 0.5


def _silu(x):
    return x * _sigmoid(x)


def _silu_of_twice(hx):
    return hx + hx * jnp.tanh(hx)


def _softplus(x):
    return jnp.maximum(x, 0.0) + jnp.log(1.0 + jnp.exp(-jnp.abs(x)))


def _log_sigmoid(x):
    return jnp.minimum(x, 0.0) - jnp.log(1.0 + jnp.exp(-jnp.abs(x)))


def _rms(x, g):
    return x * lax.rsqrt(jnp.mean(x * x, axis=-1, keepdims=True) + EPS) * g


def _mod_row(row_start):
    return jnp.where(row_start < NCTX, 0, 1 + (row_start - NCTX) // DEC_SEQ)


def _mod_spec(layer, which, tm, axis=0):
    def imap(*idx):
        return ((layer * MOD_ROWS + _mod_row(idx[axis] * tm)) * 6 + which, 0, 0)
    return pl.BlockSpec((1, 1, D_MODEL), imap)


def _mod_kernel(c_ref, w_ref, b_ref, o_ref):
    x = _silu(c_ref[...])
    xh, xl = _split2(x)
    wh, wl = _split2(w_ref[0])
    o_ref[0] = _dot(xh, wh) + (_dot(xl, wh) + _dot(xh, wl)) + b_ref[0]


def _modulation(cond8, w_mod, b_mod):
    tn = 1536
    return pl.pallas_call(
        _mod_kernel,
        grid=(DEPTH, MOD_W // tn),
        in_specs=[pl.BlockSpec((MOD_ROWS, D_MODEL), lambda l, j: (0, 0)),
                  pl.BlockSpec((1, D_MODEL, tn), lambda l, j: (l, 0, j)),
                  pl.BlockSpec((1, 1, tn), lambda l, j: (l, 0, j))],
        out_specs=pl.BlockSpec((1, MOD_ROWS, tn), lambda l, j: (l, 0, j)),
        out_shape=jax.ShapeDtypeStruct((DEPTH, MOD_ROWS, MOD_W), F32),
        compiler_params=_cp(("arbitrary", "arbitrary")),
        name="modulation",
    )(cond8, w_mod, b_mod.reshape(DEPTH, 1, MOD_W))


def _family_specs(tm, width, axis=0):
    nc = NCTX // tm
    return [pl.BlockSpec((tm, width), lambda *idx: (jnp.minimum(idx[axis], nc - 1), 0)),
            pl.BlockSpec((tm, width), lambda *idx: (jnp.maximum(idx[axis] - nc, 0), 0))]


def _family_rows(refs, tm, axis=0):
    is_ctx = pl.program_id(axis) < NCTX // tm
    return jnp.where(is_ctx, refs[0][...], refs[1][...])


def _inproj_kernel(*refs, normed):
    if normed:
        xn_ref, w_ref, o_ref = refs
        xn = xn_ref[...]
    else:
        hc_ref, hs_ref, g_ref, sc_ref, sh_ref, w_ref, o_ref = refs
        y = _rms(_family_rows((hc_ref, hs_ref), TM_IN, axis=1), g_ref[0])
        xn = (y * (1.0 + sc_ref[0]) + sh_ref[0]).astype(BF16)
    o_ref[...] = _dot_nt(xn, w_ref[0]).astype(BF16)


def _inproj(h, mod3, norm_mix, w_in_p, layer):
    normed = not isinstance(h, tuple)
    if normed:
        in_specs = [pl.BlockSpec((TM_IN, D_MODEL), lambda j, i: (i, 0))]
        args = [h]
    else:
        in_specs = _family_specs(TM_IN, D_MODEL, axis=1) + [
            pl.BlockSpec((1, 1, D_MODEL), lambda j, i: (layer, 0, 0)),
            _mod_spec(layer, 1, TM_IN, axis=1),
            _mod_spec(layer, 0, TM_IN, axis=1)]
        args = list(h) + [norm_mix.reshape(DEPTH, 1, D_MODEL), mod3, mod3]
    return pl.pallas_call(
        functools.partial(_inproj_kernel, normed=normed),
        grid=(PROJ_W // TN_IN, R // TM_IN),
        in_specs=in_specs + [pl.BlockSpec((1, TN_IN, D_MODEL), lambda j, i: (layer, j, 0))],
        out_specs=pl.BlockSpec((TM_IN, TN_IN), lambda j, i: (i, j)),
        out_shape=jax.ShapeDtypeStruct((R, PROJ_W), BF16),
        compiler_params=_cp(("arbitrary", "arbitrary")),
        name="inproj",
    )(*args, w_in_p)


def _mla_prep_kernel(pm_ref, tab_ref, qn_ref, kvn_ref, wq_ref, wk_ref, wv_ref,
                     q_ref, k_ref, v_ref, own_ref):
    pm = pm_ref[...].astype(F32)
    tab = tab_ref[...]
    qn = _rms(pm[:, :Q_LORA], qn_ref[0]).astype(BF16)
    q2 = _dot(qn, wq_ref[0])
    hw = MLA_HEADS * HEAD_W
    cq = jnp.tile(tab[:, :HEAD_W], (1, MLA_HEADS))
    sq = jnp.tile(tab[:, HEAD_W:2 * HEAD_W], (1, MLA_HEADS))
    q_ref[...] = (q2[:, :hw] * cq + q2[:, hw:] * sq).astype(BF16)
    ckv = _rms(pm[:, Q_LORA:Q_LORA + KV_LORA], kvn_ref[0])
    kr = pm[:, Q_LORA + KV_LORA:] * tab[:, 2 * HEAD_W:]
    lhs = jnp.concatenate([ckv, kr], axis=1)
    own_ref[...] = lhs
    lb = lhs.astype(BF16)
    k_ref[...] = _dot(lb, wk_ref[0]).astype(BF16)
    v_ref[...] = _dot_nt(wv_ref[0], lb[:, :KV_LORA]).astype(BF16)


def _mixer_prep_kernel(*refs):
    _mla_prep_kernel(*refs[:7], *refs[11:15])
    _gdn_pre_kernel(*refs[7:11], refs[15])


def _mixer_prep(proj, tab, q_norm, kv_norm, wq_p, wk_p, wv_p, conv_w, layer):
    tm = TM_PREP
    hw = MLA_HEADS * HEAD_W
    cw = 2 * GDN_HEADS * GDN_DK + GDN_HEADS * GDN_DV
    halo = tm // HALO
    last_h = R // HALO - 1

    def tab_map(i):
        r0 = i * tm
        return (jnp.where(r0 < NCTX, 0, 1 + ((r0 - NCTX) % DEC_SEQ) // tm), 0)

    return pl.pallas_call(
        _mixer_prep_kernel,
        grid=(R // tm,),
        in_specs=[pl.BlockSpec((tm, 512), lambda i: (i, PC_MLA // 512)),
                  pl.BlockSpec((tm, 3 * HEAD_W), tab_map),
                  pl.BlockSpec((1, 1, Q_LORA), lambda i: (layer, 0, 0)),
                  pl.BlockSpec((1, 1, KV_LORA), lambda i: (layer, 0, 0)),
                  pl.BlockSpec((1, Q_LORA, 2 * hw), lambda i: (layer, 0, 0)),
                  pl.BlockSpec((1, 2 * KV_LORA, hw), lambda i: (layer, 0, 0)),
                  pl.BlockSpec((1, VT_ROWS, KV_LORA), lambda i: (layer, 0, 0)),
                  pl.BlockSpec((tm, cw), lambda i: (i, PC_DQKV // cw)),
                  pl.BlockSpec((HALO, cw), lambda i: (jnp.maximum(i * halo - 1, 0), PC_DQKV // cw)),
                  pl.BlockSpec((HALO, cw), lambda i: (jnp.minimum((i + 1) * halo, last_h), PC_DQKV // cw)),
                  pl.BlockSpec((1, 3, cw), lambda i: (layer, 0, 0))],
        out_specs=[pl.BlockSpec((tm, hw), lambda i: (i, 0)),
                   pl.BlockSpec((tm, hw), lambda i: (i, 0)),
                   pl.BlockSpec((VT_ROWS, tm), lambda i: (0, i)),
                   pl.BlockSpec((tm, 2 * KV_LORA), lambda i: (i, 0)),
                   pl.BlockSpec((tm, cw), lambda i: (i, 0))],
        out_shape=[jax.ShapeDtypeStruct((R, hw), BF16),
                   jax.ShapeDtypeStruct((R, hw), BF16),
                   jax.ShapeDtypeStruct((VT_ROWS, R), BF16),
                   jax.ShapeDtypeStruct((R, 2 * KV_LORA), F32),
                   jax.ShapeDtypeStruct((R, cw), BF16)],
        compiler_params=_cp(("arbitrary",)),
        name="mixer_prep",
    )(proj, tab, q_norm.reshape(DEPTH, 1, Q_LORA), kv_norm.reshape(DEPTH, 1, KV_LORA),
      wq_p, wk_p, wv_p, proj, proj, proj, conv_w)


def _kv_cache_kernel(c_ref, wk_ref, wv_ref, k_ref, v_ref):
    lb = c_ref[0, 0].astype(BF16)
    k_ref[0, 0] = _dot(lb, wk_ref[0]).astype(BF16)
    v_ref[0, 0] = _dot_nt(wv_ref[0], lb[:, :KV_LORA]).astype(BF16)


def _kv_cache(cache_p, wk_p, wv_p):
    hw = MLA_HEADS * HEAD_W
    return pl.pallas_call(
        _kv_cache_kernel,
        grid=(DEC_BATCH, DEPTH),
        in_specs=[pl.BlockSpec((1, 1, PAST_LEN, 2 * KV_LORA), lambda b, l: (b, l, 0, 0)),
                  pl.BlockSpec((1, 2 * KV_LORA, hw), lambda b, l: (l, 0, 0)),
                  pl.BlockSpec((1, VT_ROWS, KV_LORA), lambda b, l: (l, 0, 0))],
        out_specs=[pl.BlockSpec((1, 1, PAST_LEN, hw), lambda b, l: (b, l, 0, 0)),
                   pl.BlockSpec((1, 1, VT_ROWS, PAST_LEN), lambda b, l: (b, l, 0, 0))],
        out_shape=[jax.ShapeDtypeStruct((DEC_BATCH, DEPTH, PAST_LEN, hw), BF16),
                   jax.ShapeDtypeStruct((DEC_BATCH, DEPTH, VT_ROWS, PAST_LEN), BF16)],
        compiler_params=_cp(("arbitrary", "arbitrary")),
        name="kv_cache",
    )(cache_p, wk_p, wv_p)


def _attn_kernel(*refs, heads, has_cache):
    if has_cache:
        q_ref, k_ref, vt_ref, kc_ref, vct_ref, o_ref = refs
    else:
        q_ref, k_ref, vt_ref, o_ref = refs
    n_keys = k_ref.shape[0]
    tk = min(TK_ATT, n_keys)
    tiles = ([("cache", 0, PAST_LEN)] if has_cache else []) + [("own", s0, tk) for s0 in range(0, n_keys, tk)]
    units = [(h, t) for t in range(len(tiles)) for h in range(heads)]
    qs = [q_ref[:, h * HEAD_W:(h + 1) * HEAD_W] for h in range(heads)]

    def scores(h, t):
        kind, s0, n = tiles[t]
        sl = slice(h * HEAD_W, (h + 1) * HEAD_W)
        keys = kc_ref[0, 0, :, sl] if kind == "cache" else k_ref[s0:s0 + n, sl]
        st = _dot_nt(keys, qs[h])
        return st, jnp.max(st, axis=0, keepdims=True)

    def values_t(h, t):
        kind, s0, n = tiles[t]
        sv = slice(h * MLA_V, (h + 1) * MLA_V)
        vt = vct_ref[0, 0, sv, :] if kind == "cache" else vt_ref[sv, s0:s0 + n]
        return jnp.concatenate([vt, jnp.ones((ONES_ROWS, n), BF16)], axis=0)

    ahead = ATT_AHEAD
    pending = {u: scores(*u) for u in units[:ahead]}
    m, acc = [None] * heads, [None] * heads
    for i, (h, t) in enumerate(units):
        if i + ahead < len(units):
            pending[units[i + ahead]] = scores(*units[i + ahead])
        st, mt = pending.pop((h, t))
        if m[h] is None:
            m[h] = mt
            acc[h] = _dot(values_t(h, t), jnp.exp2(st - mt).astype(BF16))
        else:
            m_new = jnp.maximum(m[h], mt)
            alpha = jnp.exp2(m[h] - m_new)
            acc[h] = alpha * acc[h] + _dot(values_t(h, t), jnp.exp2(st - m_new).astype(BF16))
            m[h] = m_new
    outs = [acc[h][:MLA_V] / acc[h][MLA_V:MLA_V + 1] for h in range(heads)]
    o_ref[...] = jnp.concatenate(outs, axis=0).T.astype(BF16)


def _attention_ctx(qp, kp, vtp):
    hw = MLA_HEADS * HEAD_W
    return pl.pallas_call(
        functools.partial(_attn_kernel, heads=MLA_HEADS, has_cache=False),
        grid=(BATCH,),
        in_specs=[pl.BlockSpec((SEQ, hw), lambda s: (s, 0)),
                  pl.BlockSpec((SEQ, hw), lambda s: (s, 0)),
                  pl.BlockSpec((VT_ROWS, SEQ), lambda s: (0, s))],
        out_specs=pl.BlockSpec((SEQ, MLA_HEADS * MLA_V), lambda s: (s, 0)),
        out_shape=jax.ShapeDtypeStruct((NCTX, MLA_HEADS * MLA_V), BF16),
        compiler_params=_cp(("arbitrary",)),
        name="attn_ctx",
    )(qp, kp, vtp)


def _attention_smp(qp, kp, vtp, kc, vct, layer):
    hpb = 4
    wq = hpb * HEAD_W
    wv = hpb * MLA_V
    nq = DEC_SEQ // TQ_ATT
    qoff = NCTX // TQ_ATT
    koff = NCTX // DEC_SEQ
    return pl.pallas_call(
        functools.partial(_attn_kernel, heads=hpb, has_cache=True),
        grid=(DEC_BATCH, MLA_HEADS // hpb, nq),
        in_specs=[pl.BlockSpec((TQ_ATT, wq), lambda b, g, i: (qoff + b * nq + i, g)),
                  pl.BlockSpec((DEC_SEQ, wq), lambda b, g, i: (koff + b, g)),
                  pl.BlockSpec((wv, DEC_SEQ), lambda b, g, i: (g, koff + b)),
                  pl.BlockSpec((1, 1, PAST_LEN, wq), lambda b, g, i: (b, layer, 0, g)),
                  pl.BlockSpec((1, 1, wv, PAST_LEN), lambda b, g, i: (b, layer, g, 0))],
        out_specs=pl.BlockSpec((TQ_ATT, hpb * MLA_V), lambda b, g, i: (b * nq + i, g)),
        out_shape=jax.ShapeDtypeStruct((NSMP, MLA_HEADS * MLA_V), BF16),
        compiler_params=_cp(("arbitrary", "arbitrary", "arbitrary")),
        name="attn_smp",
    )(qp, kp, vtp, kc, vct)


def _scan_specs(ctx, blk, widths_and_cols, dir_cols=()):
    nseq, seq_len = (BATCH, SEQ) if ctx else (DEC_BATCH, DEC_SEQ)
    nblk = seq_len // blk
    roff = 0 if ctx else NCTX // blk

    def rb(dd):
        return lambda s, j: roff + s * nblk + (j if dd == 0 else nblk - 1 - j)

    specs = []
    for dd in range(2):
        r = rb(dd)
        specs += [pl.BlockSpec((blk, w), lambda s, j, r=r, c=c: (r(s, j), c)) for w, c in widths_and_cols]
        specs += [pl.BlockSpec((blk, w), lambda s, j, r=r, c=c + dd: (r(s, j), c)) for w, c in dir_cols]
    out_maps = [lambda s, j, r=rb(dd): (r(s, j) - roff, 0) for dd in range(2)]
    return nseq, nblk, specs, out_maps


def _gla_kernel(*refs, nblk, has_s0, nch):
    per_dir = [refs[0:4], refs[4:8]]
    wg_ref, bg_ref = refs[8:10]
    if has_s0:
        s0_ref, of_ref, ob_ref, s_scr = refs[10:]
    else:
        of_ref, ob_ref, sfin_ref, s_scr = refs[10:]
    o_refs = [of_ref, ob_ref]
    j = pl.program_id(1)

    @pl.when(j == 0)
    def _():
        if has_s0:
            s_scr[...] = s0_ref[0, 0]
        else:
            s_scr[...] = jnp.zeros_like(s_scr)

    r64 = lax.broadcasted_iota(jnp.int32, (CHUNK, CHUNK), 0)
    c64 = lax.broadcasted_iota(jnp.int32, (CHUNK, CHUNK), 1)
    incl = [c64 <= r64, c64 >= r64]
    tri_b = [jnp.where(m, 1.0, 0.0).astype(BF16) for m in incl]
    heads = [(slice(h * GLA_DK, (h + 1) * GLA_DK), slice(h * GLA_DV, (h + 1) * GLA_DV))
             for h in range(GLA_HEADS)]

    ch = []
    for c in range(nch):
        for dd in range(2):
            ci = c if dd == 0 else nch - 1 - c
            rows = slice(ci * CHUNK, (ci + 1) * CHUNK)
            glr_ref = per_dir[dd][3]
            ch.append(dict(dd=dd, rows=rows, z=_dot(glr_ref[rows, :], wg_ref[0, dd]) + bg_ref[0, dd]))
    for t in ch:
        t["la"] = _split3(_log_sigmoid(t["z"]) * (1.0 / GLA_TAU))
    for t in ch:
        hi, mid, lo = t["la"]
        tb = tri_b[t["dd"]]
        t["bc"] = _dot(tb, hi) + (_dot(tb, mid) + _dot(tb, lo))
    for t in ch:
        q_ref, k_ref, v_ref, _ = per_dir[t["dd"]]
        rows, bc = t["rows"], t["bc"]
        q = q_ref[rows, :].astype(F32) * (GLA_DK ** -0.5)
        k = k_ref[rows, :].astype(F32)
        bl = bc[CHUNK - 1:CHUNK] if t["dd"] == 0 else bc[0:1]
        t["qd"] = (q * jnp.exp(bc)).astype(BF16)
        t["ki"] = (k * jnp.exp(-bc)).astype(BF16)
        t["ke"] = (k * jnp.exp(bl - bc)).astype(BF16)
        t["dec_t"] = jnp.transpose(jnp.broadcast_to(jnp.exp(bl), (LANE, GLA_HEADS * GLA_DK)))
        t["v"] = v_ref[rows, :]
    for t in ch:
        t["a"] = [jnp.where(incl[t["dd"]], _dot_nt(t["qd"][:, sk], t["ki"][:, sk]), 0.0).astype(BF16)
                  for sk, _ in heads]
    for t in ch:
        t["oi"] = [_dot(t["a"][h], t["v"][:, sv]) for h, (_, sv) in enumerate(heads)]
        t["upd"] = [_dot_tn(t["ke"][:, sk], t["v"][:, sv]) for sk, sv in heads]
    states = [[s_scr[dd, h] for h in range(GLA_HEADS)] for dd in range(2)]
    for t in ch:
        st = states[t["dd"]]
        t["s_in"] = [s.astype(BF16) for s in st]
        for h, (sk, _) in enumerate(heads):
            st[h] = t["dec_t"][sk, :] * st[h] + t["upd"][h]
    for t in ch:
        for h, (sk, sv) in enumerate(heads):
            o_refs[t["dd"]][t["rows"], sv] = t["oi"][h] + _dot(t["qd"][:, sk], t["s_in"][h])
    for dd in range(2):
        for h in range(GLA_HEADS):
            s_scr[dd, h] = states[dd][h]

    if not has_s0:
        @pl.when(j == nblk - 1)
        def _():
            sfin_ref[0] = s_scr[...]


def _gla(proj, wg_p, bg_p, s0, layer, ctx):
    blk = GLA_BLK_CTX if ctx else GLA_BLK_SMP
    hk = GLA_HEADS * GLA_DK
    hv = GLA_HEADS * GLA_DV
    nseq, nblk, in_specs, out_maps = _scan_specs(
        ctx, blk, [(hk, PC_GQ // hk), (hk, PC_GK // hk), (hv, PC_GV // hv), (LANE, PC_GLR // LANE)])
    in_specs += [pl.BlockSpec((1, 2, LANE, hk), lambda s, j: (layer, 0, 0, 0)),
                 pl.BlockSpec((1, 2, 1, hk), lambda s, j: (layer, 0, 0, 0))]
    args = [proj] * 8 + [wg_p, bg_p]
    nrows = NCTX if ctx else NSMP
    out_specs = [pl.BlockSpec((blk, hv), m) for m in out_maps]
    out_shape = [jax.ShapeDtypeStruct((nrows, hv), F32)] * 2
    st_blk = (2, GLA_HEADS, GLA_DK, GLA_DV)
    if ctx:
        out_specs.append(pl.BlockSpec((1,) + st_blk, lambda s, j: (s, 0, 0, 0, 0)))
        out_shape.append(jax.ShapeDtypeStruct((BATCH,) + st_blk, F32))
    else:
        in_specs.append(pl.BlockSpec((1, 1) + st_blk, lambda s, j: (s, layer, 0, 0, 0, 0)))
        args.append(s0)
    return pl.pallas_call(
        functools.partial(_gla_kernel, nblk=nblk, has_s0=not ctx, nch=blk // CHUNK),
        grid=(nseq, nblk),
        in_specs=in_specs, out_specs=out_specs, out_shape=out_shape,
        scratch_shapes=[pltpu.VMEM(st_blk, F32)],
        compiler_params=_cp(("arbitrary", "arbitrary")),
        name="gla_ctx" if ctx else "gla_smp",
    )(*args)


GDN_G = 2
GDN_GW = GDN_G * GDN_DK
HALO = 16


def _gdn_pre_kernel(x_ref, xp_ref, xn_ref, cw_ref, o_ref):
    blk = x_ref.shape[0]
    hk = GDN_HEADS * GDN_DK
    sub = SEQ
    cw = 0.5 * cw_ref[0]
    rowi = lax.broadcasted_iota(jnp.int32, (sub, 1), 0)
    lane = lax.broadcasted_iota(jnp.int32, (sub, LANE), 1)
    lo_half = lane < GDN_DK

    def l2n(t, scale):
        parts = []
        for c in range(hk // LANE):
            tc = t[:, c * LANE:(c + 1) * LANE]
            sq = tc * tc
            s_lo = jnp.sum(jnp.where(lo_half, sq, 0.0), axis=-1, keepdims=True)
            s_hi = jnp.sum(jnp.where(lo_half, 0.0, sq), axis=-1, keepdims=True)
            parts.append(tc * (lax.rsqrt(jnp.where(lo_half, s_lo, s_hi) + EPS) * scale))
        return jnp.concatenate(parts, axis=1)

    for s in range(blk // sub):
        rows = slice(s * sub, (s + 1) * sub)
        x = x_ref[rows, :].astype(F32)
        r0 = pl.program_id(0) * blk + s * sub
        pos0 = jnp.where(r0 < NCTX, r0 % SEQ, (r0 - NCTX) % DEC_SEQ)
        seq_len = jnp.where(r0 < NCTX, SEQ, DEC_SEQ)
        before = xp_ref[...] if s == 0 else x_ref[s * sub - HALO:s * sub, :]
        after = xn_ref[...] if s == blk // sub - 1 else x_ref[(s + 1) * sub:(s + 1) * sub + HALO, :]
        prev_row = jnp.where(pos0 > 0, before.astype(F32)[HALO - 1:HALO], 0.0)
        next_row = jnp.where(pos0 + sub < seq_len, after.astype(F32)[0:1], 0.0)
        x_prev = jnp.where(rowi == 0, prev_row, pltpu.roll(x, 1, 0))
        x_next = jnp.where(rowi == sub - 1, next_row, pltpu.roll(x, sub - 1, 0))
        y = _silu_of_twice(x_prev * cw[0:1] + x * cw[1:2] + x_next * cw[2:3])
        o_ref[rows, :hk] = l2n(y[:, :hk], GDN_DK ** -0.5).astype(BF16)
        o_ref[rows, hk:2 * hk] = l2n(y[:, hk:2 * hk], 1.0).astype(BF16)
        o_ref[rows, 2 * hk:] = y[:, 2 * hk:].astype(BF16)


def _gdn_kernel(*refs, nblk, has_s0, nch):
    per_dir = [refs[0:4], refs[4:8]]
    alog_ref, dtb_ref = refs[8:10]
    if has_s0:
        s0_ref, of_ref, ob_ref, s_scr = refs[10:]
    else:
        of_ref, ob_ref, sfin_ref, s_scr = refs[10:]
    o_refs = [of_ref, ob_ref]
    j = pl.program_id(1)
    ngrp = GDN_HEADS // GDN_G

    def head_block(h):
        g, hh = divmod(h, GDN_G)
        return g, slice(hh * GDN_DK, (hh + 1) * GDN_DK), slice(hh * GDN_DV, (hh + 1) * GDN_DV)

    @pl.when(j == 0)
    def _():
        s_scr[...] = jnp.zeros_like(s_scr)
        if has_s0:
            for dd in range(2):
                for h in range(GDN_HEADS):
                    g, rk, rv = head_block(h)
                    s_scr[dd, g, rk, rv] = s0_ref[0, 0, dd, h]

    row = lax.broadcasted_iota(jnp.int32, (CHUNK, GDN_GW), 0)
    col = lax.broadcasted_iota(jnp.int32, (CHUNK, GDN_GW), 1) % CHUNK
    r64 = lax.broadcasted_iota(jnp.int32, (CHUNK, CHUNK), 0)
    c64 = lax.broadcasted_iota(jnp.int32, (CHUNK, CHUNK), 1)
    dir_masks = []
    for dd in range(2):
        lag = (row - col) if dd == 0 else (col - row)
        tri = (c64 <= r64) if dd == 0 else (c64 >= r64)
        dir_masks.append(dict(incl=lag >= 0, strict=lag > 0,
                              tri_b=jnp.where(tri, 1.0, 0.0).astype(BF16)))
    eye = col == row
    eye_f = jnp.where(eye, 1.0, 0.0)
    blk8 = (row // 8) == (col // 8)
    blk16 = (row // 16) == (col // 16)
    blk32 = (row // 32) == (col // 32)
    brow = lax.broadcasted_iota(jnp.int32, (GDN_GW, GDN_GW), 0) // CHUNK
    bcol = lax.broadcasted_iota(jnp.int32, (GDN_GW, GDN_GW), 1) // CHUNK
    same_head = brow == bcol

    def bdiag(t):
        tb = t.astype(BF16)
        return jnp.where(same_head, jnp.concatenate([tb] * GDN_G, axis=0), jnp.zeros((), BF16))

    def bdmm(a, b):
        return _dot(a.astype(BF16), bdiag(b))

    kq_cache = {}

    def setup(dd, c):
        q_ref, k_ref, v_ref, ab_ref = per_dir[dd]
        mk = dir_masks[dd]
        ci = c if dd == 0 else nch - 1 - c
        rows = slice(ci * CHUNK, (ci + 1) * CHUNK)
        ab = ab_ref[rows, :].astype(F32)
        neg_a = -jnp.exp(alog_ref[0, dd])
        gam = _cumsum_rows(mk["tri_b"], neg_a * _softplus(ab + dtb_ref[0, dd]))
        beta_all = _sigmoid(ab)
        out = []
        for g in range(ngrp):
            gl = slice(g * GDN_GW, (g + 1) * GDN_GW)
            gcol = jnp.concatenate(
                [jnp.broadcast_to(gam[:, g * GDN_G + h:g * GDN_G + h + 1], (CHUNK, GDN_DK))
                 for h in range(GDN_G)], axis=1)
            bcol_ = jnp.concatenate(
                [jnp.broadcast_to(beta_all[:, GDN_HEADS + g * GDN_G + h:GDN_HEADS + g * GDN_G + h + 1],
                                  (CHUNK, GDN_DK)) for h in range(GDN_G)], axis=1)
            grow = jnp.sum(jnp.where(eye, gcol, 0.0), axis=0, keepdims=True)
            glast = gcol[CHUNK - 1:CHUNK] if dd == 0 else gcol[0:1]
            decay = jnp.where(mk["incl"], jnp.exp(gcol - grow), 0.0)
            egc = jnp.exp(gcol)
            qg = q_ref[rows, gl].astype(F32)
            kg = k_ref[rows, gl].astype(F32)
            vg = v_ref[rows, gl].astype(F32)
            key = (ci, g) if nblk == 1 else (dd, ci, g)
            if key not in kq_cache:
                kq_cache[key] = _dot_nt(jnp.concatenate([k_ref[rows, gl], q_ref[rows, gl]], axis=0),
                                        bdiag(k_ref[rows, gl]))
            kq = kq_cache[key]
            out.append(dict(dd=dd, g=g, rows=rows, gl=gl,
                            m=jnp.where(mk["strict"], bcol_ * kq[:CHUNK] * decay, 0.0),
                            aqk=(kq[CHUNK:] * decay).astype(BF16),
                            vb=vg * bcol_, kb=kg * (bcol_ * egc), qe=qg * egc,
                            kend=(kg * jnp.exp(glast - gcol)).astype(BF16),
                            eg=jnp.exp(glast)))
        return out

    def stages(ch):
        pairs = list(zip(ch[0::2], ch[1::2]))

        def bdmm_pairs(lhs, rhs, out):
            for ta, tb in pairs:
                a = jnp.concatenate([ta[lhs].astype(BF16), tb[lhs].astype(BF16)], axis=0)
                w = jnp.concatenate([bdiag(ta[rhs]), bdiag(tb[rhs])], axis=1)
                y = _dot(a, w)
                ta[out] = y[:CHUNK, :GDN_GW]
                tb[out] = y[CHUNK:, GDN_GW:]

        def neumann_a():
            for t in ch:
                t["n8"] = jnp.where(blk8, t["m"], 0.0)
            bdmm_pairs("n8", "n8", "n2")

        def neumann_b():
            for t in ch:
                t["p1"] = eye_f - t["n8"]
            bdmm_pairs("n2", "n2", "n4")
            bdmm_pairs("p1", "n2", "p1n2")
            for t in ch:
                t["p1"] = t["p1"] + t["p1n2"]

        def neumann_c():
            bdmm_pairs("p1", "n4", "p1n4")
            for t in ch:
                t["dinv"] = t["p1"] + t["p1n4"]

        def double_a(off):
            def run():
                for t in ch:
                    t["off"] = jnp.where(off, t["m"], 0.0)
                bdmm_pairs("dinv", "off", "dl")
            return run

        def double_b():
            bdmm_pairs("dl", "dinv", "dld")
            for t in ch:
                t["dinv"] = t["dinv"] - t["dld"]

        def solve():
            for t in ch:
                uw = _dot(t["dinv"].astype(BF16), jnp.concatenate([bdiag(t["vb"]), bdiag(t["kb"])], axis=1))
                t["u"] = uw[:, :GDN_GW]
                t["w"] = uw[:, GDN_GW:]

        def fold():
            for t in ch:
                wu = jnp.concatenate([t["w"], t["u"]], axis=1).astype(BF16)
                cb = _dot_tn(t["kend"], wu)
                t["c"] = jnp.where(same_head, cb[:, :GDN_GW], 0.0).astype(BF16)
                t["b"] = jnp.where(same_head, cb[:, GDN_GW:], 0.0)
                ao = _dot(t["aqk"], jnp.concatenate([bdiag(t["w"]), bdiag(t["u"])], axis=1))
                t["qt"] = (t["qe"] - ao[:, :GDN_GW]).astype(BF16)
                t["o"] = ao[:, GDN_GW:]

        out = [neumann_a, neumann_b, neumann_c]
        for inner, outer in ((blk8, blk16), (blk16, blk32), (blk32, None)):
            off = jnp.logical_not(inner) if outer is None else jnp.logical_and(outer, jnp.logical_not(inner))
            out += [double_a(off), double_b]
        return out + [solve, fold]

    states = [[s_scr[dd, g] for g in range(ngrp)] for dd in range(2)]

    def recur(cur):
        ys = [_dot(jnp.concatenate([t["qt"], t["c"]], axis=0), states[t["dd"]][t["g"]].astype(BF16))
              for t in cur]
        for t, y in zip(cur, ys):
            o_refs[t["dd"]][t["rows"], t["gl"]] = y[:CHUNK] + t["o"]
            states[t["dd"]][t["g"]] = t["eg"] * states[t["dd"]][t["g"]] - y[CHUNK:] + t["b"]

    half = nch // 2
    chains = {}
    for c in range(half):
        for dd in range(2):
            chains[(dd, c)] = setup(dd, c)
    later = [(dd, c) for c in range(half, nch) for dd in range(2)]
    wave_a = [t for c in range(half) for dd in range(2) for t in chains[(dd, c)]]
    stages_a = stages(wave_a)
    stride = max(1, len(stages_a) // (len(later) + 1))
    for k, stage in enumerate(stages_a):
        stage()
        if k % stride == 0 and later:
            key = later.pop(0)
            chains[key] = setup(*key)
    for key in later:
        chains[key] = setup(*key)
    wave_b = [t for c in range(half, nch) for dd in range(2) for t in chains[(dd, c)]]
    stages_b = stages(wave_b)
    stride = max(1, len(stages_b) // (half + 1))
    done = 0
    for k, stage in enumerate(stages_b):
        stage()
        if k % stride == 1 % stride and done < half:
            recur(chains[(0, done)] + chains[(1, done)])
            done += 1
    for c in range(done, nch):
        recur(chains[(0, c)] + chains[(1, c)])
    for dd in range(2):
        for g in range(ngrp):
            s_scr[dd, g] = states[dd][g]

    if not has_s0:
        @pl.when(j == nblk - 1)
        def _():
            for dd in range(2):
                for h in range(GDN_HEADS):
                    g, rk, rv = head_block(h)
                    sfin_ref[0, dd, h] = s_scr[dd, g, rk, rv]


def _gdn(qkv, proj, alog_p, dtb_p, s0, layer, ctx):
    blk = SEQ_BLK if ctx else GDN_BLK_SMP
    nseq, nblk = (BATCH, SEQ // blk) if ctx else (DEC_BATCH, DEC_SEQ // blk)
    roff = 0 if ctx else NCTX // blk
    hk = GDN_HEADS * GDN_DK
    hv = GDN_HEADS * GDN_DV

    def rb(dd):
        return lambda s, j: roff + s * nblk + (j if dd == 0 else nblk - 1 - j)

    in_specs, args = [], []
    for dd in range(2):
        r = rb(dd)
        in_specs += [pl.BlockSpec((blk, hk), lambda s, j, r=r: (r(s, j), 0)),
                     pl.BlockSpec((blk, hk), lambda s, j, r=r: (r(s, j), 1)),
                     pl.BlockSpec((blk, hv), lambda s, j, r=r: (r(s, j), 2 * hk // hv)),
                     pl.BlockSpec((blk, LANE), lambda s, j, r=r, dd=dd: (r(s, j), PC_DAB // LANE + dd))]
        args += [qkv, qkv, qkv, proj]
    in_specs += [pl.BlockSpec((1, 2, 1, LANE), lambda s, j: (layer, 0, 0, 0))] * 2
    args += [alog_p, dtb_p]
    nrows = NCTX if ctx else NSMP
    out_specs = [pl.BlockSpec((blk, hv), lambda s, j, r=rb(dd): (r(s, j) - roff, 0)) for dd in range(2)]
    out_shape = [jax.ShapeDtypeStruct((nrows, hv), F32)] * 2
    st_blk = (2, GDN_HEADS, GDN_DK, GDN_DV)
    if ctx:
        out_specs.append(pl.BlockSpec((1,) + st_blk, lambda s, j: (s, 0, 0, 0, 0)))
        out_shape.append(jax.ShapeDtypeStruct((BATCH,) + st_blk, F32))
    else:
        in_specs.append(pl.BlockSpec((1, 1) + st_blk, lambda s, j: (s, layer, 0, 0, 0, 0)))
        args.append(s0)
    return pl.pallas_call(
        functools.partial(_gdn_kernel, nblk=nblk, has_s0=not ctx, nch=blk // CHUNK),
        grid=(nseq, nblk),
        in_specs=in_specs, out_specs=out_specs, out_shape=out_shape,
        scratch_shapes=[pltpu.VMEM((2, GDN_HEADS // GDN_G, GDN_GW, GDN_GW), F32)],
        compiler_params=_cp(("arbitrary", "arbitrary")),
        name="gdn_ctx" if ctx else "gdn_smp",
    )(*args)


def _group_rms(x, width):
    parts = []
    lane = lax.broadcasted_iota(jnp.int32, (x.shape[0], LANE), 1)
    lo_half = lane < 64
    for c in range(x.shape[1] // LANE):
        xc = x[:, c * LANE:(c + 1) * LANE]
        sq = xc * xc
        if width == LANE:
            ms = jnp.mean(sq, axis=-1, keepdims=True)
        else:
            s_lo = jnp.sum(jnp.where(lo_half, sq, 0.0), axis=-1, keepdims=True)
            s_hi = jnp.sum(jnp.where(lo_half, 0.0, sq), axis=-1, keepdims=True)
            ms = jnp.where(lo_half, s_lo, s_hi) * (1.0 / width)
        parts.append(xc * lax.rsqrt(ms + EPS))
    return jnp.concatenate(parts, axis=1)


def _merge_kernel(ymc_ref, yms_ref, ogcf_ref, ogcb_ref, ogsf_ref, ogsb_ref, gr_ref,
                  odcf_ref, odcb_ref, odsf_ref, odsb_ref,
                  dz_ref, gt_ref, *rest, split_h):
    if split_h:
        hc_ref, hs_ref, *rest = rest
        h_rows = lambda: _family_rows((hc_ref, hs_ref), TM_MERGE)
    else:
        h_ref, *rest = rest
        h_rows = lambda: h_ref[...]
    ga_ref, gn_ref, dn_ref, wb_ref, wo_ref, o_ref, ym_scr, og_scr, od_scr = rest
    is_ctx = pl.program_id(0) < NCTX // TM_MERGE

    @pl.when(is_ctx)
    def _():
        ym_scr[...] = ymc_ref[...]
        og_scr[...] = ogcf_ref[...] + ogcb_ref[...]
        od_scr[...] = odcf_ref[...] + odcb_ref[...]

    @pl.when(jnp.logical_not(is_ctx))
    def _():
        ym_scr[...] = yms_ref[...]
        og_scr[...] = ogsf_ref[...] + ogsb_ref[...]
        od_scr[...] = odsf_ref[...] + odsb_ref[...]

    y_gla = _group_rms(og_scr[...], GLA_DV) * gn_ref[0] * _silu_of_twice(gr_ref[...].astype(F32))
    y_gdn = _group_rms(od_scr[...], GDN_DV) * dn_ref[0] * _silu_of_twice(dz_ref[...].astype(F32))
    t = jnp.tanh(gt_ref[...].astype(F32))
    d = D_MODEL
    acc = None
    for n, y_n in enumerate((ym_scr[...], y_gla.astype(BF16), y_gdn.astype(BF16))):
        p = _dot(y_n, wb_ref[0, n])
        gated = p + p * t[:, n * d:(n + 1) * d]
        acc = gated if acc is None else acc + gated
    y = _dot(acc.astype(BF16), wo_ref[0])
    o_ref[...] = h_rows() + ga_ref[0] * y


def _merge(y_mla, o_gla, o_gdn, proj, h, mod3, gla_norm_p, gdn_norm_p, w_branch_b, w_out_b, layer):
    tm = TM_MERGE
    bw = 512
    nc = NCTX // tm
    split_h = isinstance(h, tuple)
    h_specs = _family_specs(tm, D_MODEL) if split_h else [pl.BlockSpec((tm, D_MODEL), lambda i: (i, 0))]
    h_args = list(h) if split_h else [h]

    def cmap(i):
        return jnp.minimum(i, nc - 1)

    def smap(i):
        return jnp.maximum(i - nc, 0)

    return pl.pallas_call(
        functools.partial(_merge_kernel, split_h=split_h),
        grid=(R // tm,),
        in_specs=[pl.BlockSpec((tm, bw), lambda i: (cmap(i), 0)),
                  pl.BlockSpec((tm, bw), lambda i: (smap(i), 0)),
                  pl.BlockSpec((tm, bw), lambda i: (cmap(i), 0)),
                  pl.BlockSpec((tm, bw), lambda i: (cmap(i), 0)),
                  pl.BlockSpec((tm, bw), lambda i: (smap(i), 0)),
                  pl.BlockSpec((tm, bw), lambda i: (smap(i), 0)),
                  pl.BlockSpec((tm, bw), lambda i: (i, PC_GR // bw)),
                  pl.BlockSpec((tm, bw), lambda i: (cmap(i), 0)),
                  pl.BlockSpec((tm, bw), lambda i: (cmap(i), 0)),
                  pl.BlockSpec((tm, bw), lambda i: (smap(i), 0)),
                  pl.BlockSpec((tm, bw), lambda i: (smap(i), 0)),
                  pl.BlockSpec((tm, bw), lambda i: (i, PC_DZ // bw)),
                  pl.BlockSpec((tm, 3 * D_MODEL), lambda i: (i, PC_GATES // (3 * D_MODEL)))] + h_specs + [
                  _mod_spec(layer, 2, tm),
                  pl.BlockSpec((1, 1, bw), lambda i: (layer, 0, 0)),
                  pl.BlockSpec((1, 1, bw), lambda i: (layer, 0, 0)),
                  pl.BlockSpec((1, 3, bw, D_MODEL), lambda i: (layer, 0, 0, 0), pipeline_mode=pl.Buffered(1)),
                  pl.BlockSpec((1, D_MODEL, D_MODEL), lambda i: (layer, 0, 0), pipeline_mode=pl.Buffered(1))],
        out_specs=pl.BlockSpec((tm, D_MODEL), lambda i: (i, 0)),
        out_shape=jax.ShapeDtypeStruct((R, D_MODEL), F32),
        scratch_shapes=[pltpu.VMEM((tm, bw), BF16), pltpu.VMEM((tm, bw), F32), pltpu.VMEM((tm, bw), F32)],
        compiler_params=_cp(("arbitrary",)),
        name="merge",
    )(y_mla[0], y_mla[1], o_gla[0][0], o_gla[0][1], o_gla[1][0], o_gla[1][1], proj,
      o_gdn[0][0], o_gdn[0][1], o_gdn[1][0], o_gdn[1][1],
      proj, proj, *h_args, mod3, gla_norm_p, gdn_norm_p, w_branch_b, w_out_b)


def _ffn_kernel(h_ref, g_ref, sc_ref, sh_ref, gf_ref, wi_ref, wo_ref, *rest, final):
    if final:
        fn_ref, *o_refs = rest
    else:
        gn_ref, scn_ref, shn_ref, *o_refs = rest
    h = h_ref[...]
    xf = (_rms(h, g_ref[0]) * (1.0 + sc_ref[0]) + sh_ref[0]).astype(BF16)

    def gate_up(c):
        f0 = c * TF_FFN
        return (_dot(xf, wi_ref[0, :, f0:f0 + TF_FFN]),
                _dot(xf, wi_ref[0, :, D_FF + f0:D_FF + f0 + TF_FFN]))

    n_chunks = D_FF // TF_FFN
    nxt = gate_up(0)
    acc = None
    for c in range(n_chunks):
        g, u = nxt
        if c + 1 < n_chunks:
            nxt = gate_up(c + 1)
        part = _dot((_silu(g) * u).astype(BF16), wo_ref[0, c * TF_FFN:(c + 1) * TF_FFN, :])
        acc = part if acc is None else acc + part
    out = h + gf_ref[0] * acc
    if not final:
        o_refs[0][...] = out
        o_refs[1][...] = (_rms(out, gn_ref[0]) * (1.0 + scn_ref[0]) + shn_ref[0]).astype(BF16)
    else:
        out = _rms(out, fn_ref[...])
        is_ctx = pl.program_id(0) < NCTX // TM_FFN

        @pl.when(is_ctx)
        def _():
            o_refs[0][...] = out

        @pl.when(jnp.logical_not(is_ctx))
        def _():
            o_refs[1][...] = out


def _ffn(h, mod3, norm_ffn, wi_b, wo_b, tail, layer, final):
    tm = TM_FFN
    if final:
        nc = NCTX // tm
        out_specs = [pl.BlockSpec((tm, D_MODEL), lambda i: (jnp.minimum(i, nc - 1), 0)),
                     pl.BlockSpec((tm, D_MODEL), lambda i: (jnp.maximum(i - nc, 0), 0))]
        out_shape = [jax.ShapeDtypeStruct((NCTX, D_MODEL), F32), jax.ShapeDtypeStruct((NSMP, D_MODEL), F32)]
        tail_specs = [pl.BlockSpec((1, D_MODEL), lambda i: (0, 0))]
        tail_args = [tail.reshape(1, D_MODEL)]
    else:
        out_specs = [pl.BlockSpec((tm, D_MODEL), lambda i: (i, 0))] * 2
        out_shape = [jax.ShapeDtypeStruct((R, D_MODEL), F32), jax.ShapeDtypeStruct((R, D_MODEL), BF16)]
        tail_specs = [pl.BlockSpec((1, 1, D_MODEL), lambda i: (layer + 1, 0, 0)),
                      _mod_spec(layer + 1, 1, tm), _mod_spec(layer + 1, 0, tm)]
        tail_args = [tail.reshape(DEPTH, 1, D_MODEL), mod3, mod3]
    return pl.pallas_call(
        functools.partial(_ffn_kernel, final=final),
        grid=(R // tm,),
        in_specs=[pl.BlockSpec((tm, D_MODEL), lambda i: (i, 0)),
                  pl.BlockSpec((1, 1, D_MODEL), lambda i: (layer, 0, 0)),
                  _mod_spec(layer, 4, tm),
                  _mod_spec(layer, 3, tm),
                  _mod_spec(layer, 5, tm),
                  pl.BlockSpec((1, D_MODEL, 2 * D_FF), lambda i: (layer, 0, 0), pipeline_mode=pl.Buffered(1)),
                  pl.BlockSpec((1, D_FF, D_MODEL), lambda i: (layer, 0, 0), pipeline_mode=pl.Buffered(1))]
        + tail_specs,
        out_specs=out_specs, out_shape=out_shape,
        compiler_params=_cp(("arbitrary",)),
        name="ffn_final" if final else "ffn",
    )(h, norm_ffn.reshape(DEPTH, 1, D_MODEL), mod3, mod3, mod3, wi_b, wo_b, *tail_args)


def _rope_partner(w, axis=-1):
    h = AXIS_DIM // 2
    axis = axis % w.ndim
    parts = []
    for a in range(2):
        x1 = lax.slice_in_dim(w, a * AXIS_DIM, a * AXIS_DIM + h, axis=axis)
        x2 = lax.slice_in_dim(w, a * AXIS_DIM + h, (a + 1) * AXIS_DIM, axis=axis)
        parts += [-x2, x1]
    return jnp.concatenate(parts, axis=axis)


def _pack_w_in_kernel(w_ref, o_ref):
    o_ref[0] = _pack_channels(w_ref[0]).astype(BF16)


def _pack_w_in(w_in):
    tc = 256
    return pl.pallas_call(
        _pack_w_in_kernel,
        grid=(DEPTH, D_MODEL // tc),
        in_specs=[pl.BlockSpec((1, w_in.shape[-1], tc), lambda l, i: (l, 0, i))],
        out_specs=pl.BlockSpec((1, PROJ_W, tc), lambda l, i: (l, 0, i)),
        out_shape=jax.ShapeDtypeStruct((DEPTH, PROJ_W, D_MODEL), BF16),
        compiler_params=_cp(("arbitrary", "arbitrary")),
        name="pack_w_in",
    )(jnp.swapaxes(w_in, 1, 2))


def _pack_channels(wt):
    c = [0]
    for s in (Q_LORA, KV_LORA + ROPE_DIM, 256, 256, 512, 512, 2 * GLA_RANK, 1536, 512,
              2 * GDN_HEADS, 2 * GDN_HEADS, 3 * D_MODEL):
        c.append(c[-1] + s)
    mq, mkv, gq, gk, gv, gr, glr, dqkv, dz, da, db, gates = [wt[c[i]:c[i + 1]] for i in range(12)]

    def zeros(n):
        return jnp.zeros((n,) + wt.shape[1:], wt.dtype)

    kr = mkv[KV_LORA:]
    mla = [mq, mkv[:KV_LORA], kr, _rope_partner(kr, axis=0), zeros(64)]
    dab = []
    for d in range(2):
        dab += [da[d * GDN_HEADS:(d + 1) * GDN_HEADS], db[d * GDN_HEADS:(d + 1) * GDN_HEADS],
                zeros(LANE - 2 * GDN_HEADS)]
    return jnp.concatenate([0.5 * gates, dqkv, 0.5 * dz, gv, 0.5 * gr] + mla
                           + [gq, gk, glr, zeros(LANE - 2 * GLA_RANK)]
                           + dab + [zeros(PROJ_W - PC_DAB - 2 * LANE)], axis=0)


def _pack_mla_weights(w_uq, w_ukv):
    l = w_uq.shape[0]
    qh = w_uq.reshape(l, Q_LORA, MLA_HEADS, MLA_NOPE + ROPE_DIM)
    zq = jnp.zeros((l, Q_LORA, MLA_HEADS, HEAD_W - MLA_NOPE - ROPE_DIM), w_uq.dtype)
    zn = jnp.zeros((l, Q_LORA, MLA_HEADS, MLA_NOPE), w_uq.dtype)
    wq = jnp.concatenate([qh, zq], axis=-1).reshape(l, Q_LORA, MLA_HEADS * HEAD_W)
    wq_sw = jnp.concatenate([zn, _rope_partner(qh[..., MLA_NOPE:]), zq], axis=-1).reshape(l, Q_LORA, MLA_HEADS * HEAD_W)
    wq_p = jnp.concatenate([wq, wq_sw], axis=-1).astype(BF16)

    kvh = w_ukv.reshape(l, KV_LORA, MLA_HEADS, MLA_NOPE + MLA_V)
    zk = jnp.zeros((l, KV_LORA, MLA_HEADS, HEAD_W - MLA_NOPE), w_ukv.dtype)
    wk_top = jnp.concatenate([kvh[..., :MLA_NOPE], zk], axis=-1).reshape(l, KV_LORA, MLA_HEADS * HEAD_W)
    place = np.zeros((KV_LORA, MLA_HEADS, HEAD_W), np.float32)
    idx = np.arange(ROPE_DIM)
    for rep in range(2):
        place[rep * ROPE_DIM + idx, :, MLA_NOPE + idx] = 1.0
    wk_bot = jnp.broadcast_to(jnp.asarray(place.reshape(1, KV_LORA, MLA_HEADS * HEAD_W)),
                              (l, KV_LORA, MLA_HEADS * HEAD_W))
    wk_p = jnp.concatenate([wk_top, wk_bot], axis=1).astype(BF16)
    wv_p = jnp.transpose(kvh[..., MLA_NOPE:], (0, 2, 3, 1)).reshape(l, VT_ROWS, KV_LORA).astype(BF16)
    return wq_p, wk_p, wv_p


def _rope_tables(tm):
    f32 = np.float32
    rows = DEC_SEQ // GRID_W
    row = np.repeat(np.arange(rows, dtype=np.float64), GRID_W)
    col = np.tile(np.arange(GRID_W, dtype=np.float64), rows)
    inv = ROPE_THETA ** (-np.arange(0, AXIS_DIM, 2, dtype=np.float64) / AXIS_DIM)
    ang_r, ang_c = row[:, None] * inv, col[:, None] * inv
    cos32 = np.concatenate([np.cos(ang_r)] * 2 + [np.cos(ang_c)] * 2, axis=-1)
    sin32 = np.concatenate([np.sin(ang_r)] * 2 + [np.sin(ang_c)] * 2, axis=-1)
    cos32 = np.concatenate([np.ones((tm, ROPE_DIM)), cos32], axis=0)
    sin32 = np.concatenate([np.zeros((tm, ROPE_DIM)), sin32], axis=0)
    n = cos32.shape[0]
    scale = (MLA_NOPE + ROPE_DIM) ** -0.5 * math.log2(math.e)
    pad = HEAD_W - MLA_NOPE - ROPE_DIM
    cq = np.concatenate([np.ones((n, MLA_NOPE)), cos32, np.zeros((n, pad))], axis=-1) * scale
    sq = np.concatenate([np.zeros((n, MLA_NOPE)), sin32, np.zeros((n, pad))], axis=-1) * scale
    ck = np.concatenate([cos32, sin32, np.zeros((n, HEAD_W - 2 * ROPE_DIM))], axis=-1)
    return jnp.asarray(np.concatenate([cq, sq, ck], axis=-1).astype(f32))


def _pad_lanes(x, n):
    return jnp.pad(x, [(0, 0)] * (x.ndim - 1) + [(0, n - x.shape[-1])])


def kernel(x_prompt, x_sample, cache_mla, state_gla, state_gdn, c, c_ctx, w_mod, b_mod, norm_mix, w_in,
           mla_q_norm, mla_w_uq, mla_kv_norm, mla_w_ukv, gla_w_gate, gla_b_gate, gla_norm, gdn_conv,
           gdn_a_log, gdn_dt_bias, gdn_norm, w_branch, w_out, norm_ffn, ffn_w_in, ffn_w_out, final_norm):
    assert x_prompt.shape == (BATCH, SEQ, D_MODEL) and x_sample.shape == (DEC_BATCH, DEC_SEQ, D_MODEL)
    assert cache_mla.shape == (DEC_BATCH, DEPTH, PAST_LEN, KV_LORA + ROPE_DIM) and w_in.shape[0] == DEPTH
    w_in_p = _pack_w_in(w_in)
    wq_p, wk_p, wv_p = _pack_mla_weights(mla_w_uq, mla_w_ukv)
    tab = _rope_tables(TM_PREP)
    wg_p = jnp.zeros((DEPTH, 2, LANE, GLA_HEADS * GLA_DK), F32)
    for d in range(2):
        wg_p = wg_p.at[:, d, d * GLA_RANK:(d + 1) * GLA_RANK, :].set(gla_w_gate[:, d])
    wg_p = wg_p.astype(BF16)
    bg_p = gla_b_gate.reshape(DEPTH, 2, 1, GLA_HEADS * GLA_DK)
    alog_p = _pad_lanes(gdn_a_log, LANE).reshape(DEPTH, 2, 1, LANE)
    dtb_p = _pad_lanes(gdn_dt_bias, LANE).reshape(DEPTH, 2, 1, LANE)
    gla_norm_p = jnp.tile(gla_norm, (1, GLA_HEADS)).reshape(DEPTH, 1, GLA_HEADS * GLA_DV)
    gdn_norm_p = jnp.tile(gdn_norm, (1, GDN_HEADS)).reshape(DEPTH, 1, GDN_HEADS * GDN_DV)
    w_branch_b = (0.5 * w_branch).astype(BF16)
    w_out_b = w_out.astype(BF16)
    wi_b = ffn_w_in.astype(BF16)
    wo_b = ffn_w_out.astype(BF16)
    cache_p = _pad_lanes(cache_mla, 2 * KV_LORA)
    cond8 = jnp.concatenate([c_ctx[None, :], c, jnp.zeros((MOD_ROWS - 1 - DEC_BATCH, D_MODEL), F32)], axis=0)

    mod = _modulation(cond8, w_mod, b_mod)
    mod3 = mod.reshape(DEPTH * MOD_ROWS * 6, 1, D_MODEL)
    kc, vc = _kv_cache(cache_p, wk_p, wv_p)

    h = (x_prompt.reshape(NCTX, D_MODEL), x_sample.reshape(NSMP, D_MODEL))
    kv_list, gla_list, gdn_list = [], [], []
    for l in range(DEPTH):
        proj = _inproj(h if l == 0 else xn, mod3, norm_mix, w_in_p, l)
        qp, kp, vtp, own, qkv = _mixer_prep(proj, tab, mla_q_norm, mla_kv_norm, wq_p, wk_p, wv_p, gdn_conv, l)
        y_mla = (_attention_ctx(qp, kp, vtp), _attention_smp(qp, kp, vtp, kc, vc, l))
        ogf_c, ogb_c, sg = _gla(proj, wg_p, bg_p, None, l, True)
        ogf_s, ogb_s = _gla(proj, wg_p, bg_p, state_gla, l, False)
        odf_c, odb_c, sd = _gdn(qkv, proj, alog_p, dtb_p, None, l, True)
        odf_s, odb_s = _gdn(qkv, proj, alog_p, dtb_p, state_gdn, l, False)
        h = _merge(y_mla, ((ogf_c, ogb_c), (ogf_s, ogb_s)), ((odf_c, odb_c), (odf_s, odb_s)), proj, h, mod3,
                   gla_norm_p, gdn_norm_p,
                   w_branch_b, w_out_b, l)
        if l == DEPTH - 1:
            h = _ffn(h, mod3, norm_ffn, wi_b, wo_b, final_norm, l, True)
        else:
            h, xn = _ffn(h, mod3, norm_ffn, wi_b, wo_b, norm_mix, l, False)
        kv_list.append(own[:NCTX, :KV_LORA + ROPE_DIM].reshape(BATCH, SEQ, KV_LORA + ROPE_DIM))
        gla_list.append(sg)
        gdn_list.append(sd)

    y_prompt = h[0].reshape(BATCH, SEQ, D_MODEL)
    y_sample = h[1].reshape(DEC_BATCH, DEC_SEQ, D_MODEL)
    return (y_prompt, y_sample, jnp.stack(kv_list, axis=1), jnp.stack(gla_list, axis=1),
            jnp.stack(gdn_list, axis=1))
```
